```python
import math
import jax, jax.numpy as jnp
from jax import lax
import numpy as np

D_MODEL = 1024
BATCH = 2
SEQ = 16384
DEPTH = 2
DEC_BATCH = 16
DEC_SEQ = 16
PAST_LEN = 2048

CHUNK = 64
N_AB = (DEPTH + 1) // 2
N_C = DEPTH // 2
EPS = 1e-6

S5_WIDTH = D_MODEL // 2
S5_GROUP = 16
S5_GROUPS = S5_WIDTH // S5_GROUP
S5_STATE = 64

MLA_HEADS = 8
MLA_NOPE = 64
MLA_ROPE = 32
MLA_V = 64
MLA_Q_RANK = 384
MLA_KV_RANK = 256
ROPE_THETA = 10000.0
Q_BLOCK = 128
AB_IN = S5_WIDTH + MLA_Q_RANK + MLA_KV_RANK + MLA_ROPE
AB_OUT = S5_WIDTH + MLA_HEADS * MLA_V

GDN_HEADS = 8
GDN_DK = 128
GDN_DV = 128
CONV_W = 4
GDN_QKV = GDN_HEADS * (2 * GDN_DK + GDN_DV)
GDN_VW = GDN_HEADS * GDN_DV
C_IN = GDN_QKV + GDN_VW + 2 * GDN_HEADS

N_GROUPS = 4
EXPERTS_PER_GROUP = 8
N_EXPERTS = N_GROUPS * EXPERTS_PER_GROUP
D_EXPERT = 512
TOP_K = 2
MOE_BLOCK = 128

kernel_name = 'hybrid_stream_encoder_step'

F32 = jnp.float32


def rmsnorm(x, g):
    xf = x.astype(F32)
    xf = xf * lax.rsqrt(jnp.mean(xf * xf, axis=-1, keepdims=True) + EPS)
    return xf.astype(x.dtype) * g


def l2norm(x):
    return x * lax.rsqrt(jnp.sum(x * x, axis=-1, keepdims=True) + EPS)


def rope(x, pos):
    half = x.shape[-1] // 2
    inv = ROPE_THETA ** (-jnp.arange(half, dtype=F32) / half)
    ang = pos.astype(F32)[:, None] * inv[None, :]
    shape = (1, pos.shape[0]) + (1,) * (x.ndim - 3) + (half,)
    cos = jnp.cos(ang).reshape(shape)
    sin = jnp.sin(ang).reshape(shape)
    xf = x.astype(F32)
    x1, x2 = xf[..., :half], xf[..., half:]
    return jnp.concatenate([x1 * cos - x2 * sin, x1 * sin + x2 * cos], axis=-1).astype(x.dtype)


def s5_combine(e1, e2):
    a1, b1 = e1
    a2, b2 = e2
    return a1 * a2, a2 * b1 + b2


def s5_scan(u, x0, a_re, a_im, b_re, b_im, c_re, c_im, d_skip, log_dt):
    bsz, s, _ = u.shape
    uf = u.astype(F32).reshape(bsz, s, S5_GROUPS, S5_GROUP)
    lam = lax.complex(jnp.minimum(a_re.astype(F32), -1e-4), a_im.astype(F32))
    dt = jnp.exp(log_dt.astype(F32))[:, None]
    lam_bar = jnp.exp(lam * dt)
    b_bar = ((lam_bar - 1.0) / lam)[:, :, None] * lax.complex(b_re.astype(F32), b_im.astype(F32))
    bu = jnp.einsum('bsgc,gpc->bsgp', uf.astype(jnp.complex64), b_bar)
    a = jnp.broadcast_to(lam_bar, (1, s) + lam_bar.shape)
    a_cum, xs = lax.associative_scan(s5_combine, (a, bu), axis=1)
    if x0 is not None:
        xs = xs + a_cum * x0[:, None]
    cmat = lax.complex(c_re.astype(F32), c_im.astype(F32))
    y = jnp.einsum('bsgp,gcp->bsgc', xs, cmat).real + d_skip.astype(F32).reshape(S5_GROUPS, S5_GROUP) * uf
    return y.reshape(bsz, s, S5_WIDTH).astype(u.dtype), xs[:, -1]


def mla_attend(q_nope, q_rope, k_nope, v, k_rope, q_pos, k_pos):
    scale = (MLA_NOPE + MLA_ROPE) ** -0.5
    s = (jnp.einsum('bqhd,bkhd->bhqk', q_nope, k_nope)
         + jnp.einsum('bqhr,bkr->bhqk', q_rope, k_rope)).astype(F32) * scale
    allowed = (k_pos[None, :] // CHUNK) <= (q_pos[:, None] // CHUNK)
    s = jnp.where(allowed[None, None], s, -1e30)
    p = jax.nn.softmax(s, axis=-1).astype(v.dtype)
    return jnp.einsum('bhqk,bkhd->bqhd', p, v)


def mla_attention(q_nope, q_rope, k_nope, v, k_rope, q_pos, k_pos):
    bsz, nq = q_nope.shape[:2]
    if nq <= Q_BLOCK:
        return mla_attend(q_nope, q_rope, k_nope, v, k_rope, q_pos, k_pos)
    nb = nq // Q_BLOCK
    qn = jnp.swapaxes(q_nope.reshape(bsz, nb, Q_BLOCK, MLA_HEADS, MLA_NOPE), 0, 1)
    qr = jnp.swapaxes(q_rope.reshape(bsz, nb, Q_BLOCK, MLA_HEADS, MLA_ROPE), 0, 1)
    qp = q_pos.reshape(nb, Q_BLOCK)

    def block(args):
        qn_b, qr_b, qp_b = args
        return mla_attend(qn_b, qr_b, k_nope, v, k_rope, qp_b, k_pos)

    out = lax.map(block, (qn, qr, qp))
    return jnp.swapaxes(out, 0, 1).reshape(bsz, nq, MLA_HEADS, MLA_V)


def mixer_ab(h, pos, lat_cache, rope_cache, x0, l, p):
    bsz, s, _ = h.shape
    proj = h @ p['w_in_ab'][l]
    u, q_lat, kv_lat, k_r = jnp.split(
        proj, [S5_WIDTH, S5_WIDTH + MLA_Q_RANK, S5_WIDTH + MLA_Q_RANK + MLA_KV_RANK], axis=-1)
    y_s5, s5_last = s5_scan(u, x0, p['s5_a_re'][l], p['s5_a_im'][l], p['s5_b_re'][l], p['s5_b_im'][l],
                            p['s5_c_re'][l], p['s5_c_im'][l], p['s5_d'][l], p['s5_log_dt'][l])
    y_s5 = jax.nn.gelu(y_s5)
    y_s5 = y_s5 * jax.nn.sigmoid(y_s5 @ p['s5_glu_w'][l] + p['s5_glu_b'][l])
    q = (rmsnorm(q_lat, p['mla_q_norm'][l]) @ p['mla_q_up'][l]).reshape(bsz, s, MLA_HEADS, MLA_NOPE + MLA_ROPE)
    q_nope = q[..., :MLA_NOPE]
    q_rope = rope(q[..., MLA_NOPE:], pos)
    lat_new = rmsnorm(kv_lat, p['mla_kv_norm'][l])
    krope_new = rope(k_r, pos)
    if lat_cache is None:
        lat_all, krope_all, k_pos = lat_new, krope_new, pos
    else:
        lat_all = jnp.concatenate([lat_cache.astype(h.dtype), lat_new], axis=1)
        krope_all = jnp.concatenate([rope_cache.astype(h.dtype), krope_new], axis=1)
        k_pos = jnp.arange(lat_all.shape[1])
    kv = (lat_all @ p['mla_kv_up'][l]).reshape(bsz, lat_all.shape[1], MLA_HEADS, MLA_NOPE + MLA_V)
    k_nope, v = kv[..., :MLA_NOPE], kv[..., MLA_NOPE:]
    attn = mla_attention(q_nope, q_rope, k_nope, v, krope_all, pos, k_pos).reshape(bsz, s, MLA_HEADS * MLA_V)
    out = jnp.concatenate([y_s5, attn], axis=-1) @ p['w_out_ab'][l]
    return out, lat_new, krope_new, s5_last


def causal_conv(x, buf, w):
    bsz, s, ch = x.shape
    if buf is None:
        buf = jnp.zeros((bsz, CONV_W - 1, ch), x.dtype)
    xp = jnp.concatenate([buf.astype(x.dtype), x], axis=1)
    y = xp[:, 0:s] * w[0]
    for j in range(1, CONV_W):
        y = y + xp[:, j:j + s] * w[j]
    return y, xp[:, s:]


def gdn_chunk(s0, q, k, v, beta, g):
    L = q.shape[2]
    gam = jnp.cumsum(g, axis=-1)
    diff = gam[..., :, None] - gam[..., None, :]
    strict = jnp.tril(jnp.ones((L, L), dtype=bool), -1)
    incl = jnp.tril(jnp.ones((L, L), dtype=bool))
    a = jnp.where(strict, beta[..., :, None] * jnp.einsum('bhid,bhjd->bhij', k, k)
                  * jnp.exp(jnp.where(strict, diff, 0.0)), 0.0)
    w = lax.linalg.triangular_solve(a, (beta * jnp.exp(gam))[..., None] * k,
                                    left_side=True, lower=True, unit_diagonal=True)
    u = lax.linalg.triangular_solve(a, beta[..., None] * v,
                                    left_side=True, lower=True, unit_diagonal=True)
    v_new = u - jnp.einsum('bhld,bhdv->bhlv', w, s0)
    pm = jnp.where(incl, jnp.einsum('bhid,bhjd->bhij', q, k) * jnp.exp(jnp.where(incl, diff, 0.0)), 0.0)
    o = (jnp.einsum('bhld,bhdv->bhlv', q * jnp.exp(gam)[..., None], s0)
         + jnp.einsum('bhij,bhjv->bhiv', pm, v_new))
    g_end = gam[..., -1:]
    s_new = (jnp.exp(g_end)[..., None] * s0
             + jnp.einsum('bhld,bhlv->bhdv', k * jnp.exp(g_end - gam)[..., None], v_new))
    return s_new, o


def mixer_c(h, conv_buf, s0, l, p):
    bsz, s, _ = h.shape
    proj = h @ p['w_in_c'][l]
    qkv, z, a_in, b_in = jnp.split(proj, [GDN_QKV, GDN_QKV + GDN_VW, GDN_QKV + GDN_VW + GDN_HEADS], axis=-1)
    qkv_c, new_buf = causal_conv(qkv, conv_buf, p['conv_w'][l])
    qkv_c = jax.nn.silu(qkv_c).astype(F32)
    q, k, v = jnp.split(qkv_c, [GDN_HEADS * GDN_DK, 2 * GDN_HEADS * GDN_DK], axis=-1)
    q = l2norm(q.reshape(bsz, s, GDN_HEADS, GDN_DK)) * (GDN_DK ** -0.5)
    k = l2norm(k.reshape(bsz, s, GDN_HEADS, GDN_DK))
    v = v.reshape(bsz, s, GDN_HEADS, GDN_DV)
    beta = jax.nn.sigmoid(b_in.astype(F32))
    g = -jnp.exp(p['gdn_a_log'][l].astype(F32)) * jax.nn.softplus(a_in.astype(F32) + p['gdn_dt_bias'][l].astype(F32))
    q, k, v, beta, g = [jnp.swapaxes(t, 1, 2) for t in (q, k, v, beta, g)]
    if s0 is None:
        s0 = jnp.zeros((bsz, GDN_HEADS, GDN_DK, GDN_DV), F32)
    else:
        s0 = s0.astype(F32)
    if s <= CHUNK:
        s_fin, o = gdn_chunk(s0, q, k, v, beta, g)
    else:
        nc = s // CHUNK

        def to_chunks(t):
            return jnp.moveaxis(t.reshape(t.shape[:2] + (nc, CHUNK) + t.shape[3:]), 2, 0)

        def step(st, xs):
            return gdn_chunk(st, *xs)

        s_fin, o = lax.scan(step, s0, (to_chunks(q), to_chunks(k), to_chunks(v), to_chunks(beta), to_chunks(g)))
        o = jnp.moveaxis(o, 0, 2).reshape(bsz, GDN_HEADS, s, GDN_DV)
    o = jnp.swapaxes(o, 1, 2)
    o = rmsnorm(o, p['gdn_norm'][l].astype(F32)) * jax.nn.silu(z.astype(F32).reshape(bsz, s, GDN_HEADS, GDN_DV))
    out = o.reshape(bsz, s, GDN_VW).astype(h.dtype) @ p['w_out_c'][l]
    return out, new_buf, s_fin.astype(h.dtype)


def hier_moe(h, w_group, b_group, w_expert, b_expert, w1, w3, w2):
    bsz, s, d = h.shape
    xf = h.reshape(bsz * s, d)
    n = bsz * s
    g_prob = jax.nn.softmax((xf @ w_group + b_group).astype(F32), axis=-1)
    g_val, g_idx = lax.top_k(g_prob, 1)
    e_logit = (xf @ w_expert + b_expert).astype(F32).reshape(n, N_GROUPS, EXPERTS_PER_GROUP)
    e_logit = jnp.take_along_axis(e_logit, g_idx[:, :, None], axis=1)[:, 0]
    e_val, e_idx = lax.top_k(jax.nn.softmax(e_logit, axis=-1), TOP_K)
    gate = (g_val * e_val / jnp.sum(e_val, axis=-1, keepdims=True)).reshape(-1)
    eid = (g_idx * EXPERTS_PER_GROUP + e_idx).reshape(-1)
    tok = jnp.repeat(jnp.arange(n, dtype=jnp.int32), TOP_K)
    order = jnp.argsort(eid)
    eid_s, tok_s, gate_s = eid[order], tok[order], gate[order]
    counts = jnp.bincount(eid, length=N_EXPERTS)
    padded = (counts + MOE_BLOCK - 1) // MOE_BLOCK * MOE_BLOCK
    pad_end = jnp.cumsum(padded)
    pad_start = pad_end - padded
    raw_start = jnp.cumsum(counts) - counts
    dest = pad_start[eid_s] + jnp.arange(n * TOP_K, dtype=jnp.int32) - raw_start[eid_s]
    n_blocks = (n * TOP_K + N_EXPERTS * (MOE_BLOCK - 1) + MOE_BLOCK - 1) // MOE_BLOCK
    t = n_blocks * MOE_BLOCK
    buf_tok = jnp.full((t,), n, jnp.int32).at[dest].set(tok_s)
    buf_gate = jnp.zeros((t,), F32).at[dest].set(gate_s)
    block_e = jnp.minimum(jnp.searchsorted(pad_end, jnp.arange(n_blocks) * MOE_BLOCK, side='right'), N_EXPERTS - 1)
    x_pad = jnp.concatenate([xf, jnp.zeros((1, d), xf.dtype)], axis=0)
    xb = x_pad[buf_tok].reshape(n_blocks, MOE_BLOCK, d)

    def expert_block(args):
        xblk, e = args
        return (jax.nn.silu(xblk @ w1[e]) * (xblk @ w3[e])) @ w2[e]

    yb = lax.map(expert_block, (xb, block_e)).reshape(t, d)
    y = jax.ops.segment_sum(yb * buf_gate[:, None].astype(yb.dtype), buf_tok, num_segments=n + 1)[:n]
    return y.reshape(bsz, s, d)


def trunk(x, c, lat_c, rope_c, s5re_c, s5im_c, conv_c, gdn_c, p):
    bsz, s, _ = x.shape
    past = 0 if lat_c is None else lat_c.shape[2]
    pos = past + jnp.arange(s)
    lats, ropes, s5res, s5ims, convs, gdns = [], [], [], [], [], []
    for layer in range(DEPTH):
        l = layer // 2
        mod = jax.nn.silu(c) @ p['w_mod'][layer] + p['b_mod'][layer]
        sh1, sc1, g1, sh2, sc2, g2 = [m[:, None, :] for m in jnp.split(mod, 6, axis=-1)]
        h = rmsnorm(x, p['norm_mix'][layer]) * (1.0 + sc1) + sh1
        if layer % 2 == 0:
            x0 = None if s5re_c is None else lax.complex(s5re_c[l].astype(F32), s5im_c[l].astype(F32))
            mix, lat, kr, s5_last = mixer_ab(h, pos, None if lat_c is None else lat_c[l],
                                             None if rope_c is None else rope_c[l], x0, l, p)
            lats.append(lat)
            ropes.append(kr)
            s5res.append(s5_last.real.astype(x.dtype))
            s5ims.append(s5_last.imag.astype(x.dtype))
        else:
            mix, buf, st = mixer_c(h, None if conv_c is None else conv_c[l], None if gdn_c is None else gdn_c[l], l, p)
            convs.append(buf)
            gdns.append(st)
        x = x + g1 * mix
        h = rmsnorm(x, p['norm_ffn'][layer]) * (1.0 + sc2) + sh2
        x = x + g2 * hier_moe(h, p['moe_w_group'][layer], p['moe_b_group'][layer], p['moe_w_expert'][layer],
                              p['moe_b_expert'][layer], p['moe_w1'][layer], p['moe_w3'][layer], p['moe_w2'][layer])
    y = rmsnorm(x, p['norm_final'])
    return (y, jnp.stack(lats), jnp.stack(ropes), jnp.stack(s5res), jnp.stack(s5ims), jnp.stack(convs), jnp.stack(gdns))


def setup_inputs(seed: int = 0) -> dict:
    key = jax.random.key(seed)
    ks = iter(jax.random.split(key, 64))

    def nrm(shape, scale=1.0):
        return jax.random.normal(next(ks), shape, F32) * scale

    def gain(shape):
        return 1.0 + nrm(shape, 0.01)

    d = D_MODEL
    dt_gdn = jnp.exp(jax.random.uniform(next(ks), (N_C, GDN_HEADS), F32, math.log(1e-3), math.log(1e-1)))
    return {
        'x_prompt': nrm((BATCH, SEQ, d)),
        'x_sample': nrm((DEC_BATCH, DEC_SEQ, d)),
        'c_prompt': nrm((BATCH, d)),
        'c_sample': nrm((DEC_BATCH, d)),
        'cache_mla_latent': nrm((N_AB, DEC_BATCH, PAST_LEN, MLA_KV_RANK)),
        'cache_mla_krope': nrm((N_AB, DEC_BATCH, PAST_LEN, MLA_ROPE)),
        'state_s5_re': nrm((N_AB, DEC_BATCH, S5_GROUPS, S5_STATE), 0.5),
        'state_s5_im': nrm((N_AB, DEC_BATCH, S5_GROUPS, S5_STATE), 0.5),
        'state_conv': nrm((N_C, DEC_BATCH, CONV_W - 1, GDN_QKV)),
        'state_gdn': nrm((N_C, DEC_BATCH, GDN_HEADS, GDN_DK, GDN_DV), 0.1),
        'w_mod': nrm((DEPTH, d, 6 * d), 0.5 * d ** -0.5),
        'b_mod': nrm((DEPTH, 6 * d), 0.02),
        'norm_mix': gain((DEPTH, d)),
        'norm_ffn': gain((DEPTH, d)),
        'norm_final': gain((d,)),
        'w_in_ab': nrm((N_AB, d, AB_IN), d ** -0.5),
        's5_a_re': -0.5 + nrm((N_AB, S5_GROUPS, S5_STATE), 0.01),
        's5_a_im': math.pi * jnp.arange(S5_STATE, dtype=F32) + nrm((N_AB, S5_GROUPS, S5_STATE), 0.01),
        's5_b_re': nrm((N_AB, S5_GROUPS, S5_STATE, S5_GROUP), (2 * S5_GROUP) ** -0.5),
        's5_b_im': nrm((N_AB, S5_GROUPS, S5_STATE, S5_GROUP), (2 * S5_GROUP) ** -0.5),
        's5_c_re': nrm((N_AB, S5_GROUPS, S5_GROUP, S5_STATE), (2 * S5_STATE) ** -0.5),
        's5_c_im': nrm((N_AB, S5_GROUPS, S5_GROUP, S5_STATE), (2 * S5_STATE) ** -0.5),
        's5_d': nrm((N_AB, S5_WIDTH)),
        's5_log_dt': jax.random.uniform(next(ks), (N_AB, S5_GROUPS), F32, math.log(1e-3), math.log(1e-1)),
        's5_glu_w': nrm((N_AB, S5_WIDTH, S5_WIDTH), S5_WIDTH ** -0.5),
        's5_glu_b': nrm((N_AB, S5_WIDTH), 0.02),
        'mla_q_norm': gain((N_AB, MLA_Q_RANK)),
        'mla_q_up': nrm((N_AB, MLA_Q_RANK, MLA_HEADS * (MLA_NOPE + MLA_ROPE)), MLA_Q_RANK ** -0.5),
        'mla_kv_norm': gain((N_AB, MLA_KV_RANK)),
        'mla_kv_up': nrm((N_AB, MLA_KV_RANK, MLA_HEADS * (MLA_NOPE + MLA_V)), MLA_KV_RANK ** -0.5),
        'w_out_ab': nrm((N_AB, AB_OUT, d), AB_OUT ** -0.5),
        'w_in_c': nrm((N_C, d, C_IN), d ** -0.5),
        'conv_w': nrm((N_C, CONV_W, GDN_QKV), CONV_W ** -0.5),
        'gdn_a_log': jnp.log(jax.random.uniform(next(ks), (N_C, GDN_HEADS), F32, 1.0, 16.0)),
        'gdn_dt_bias': dt_gdn + jnp.log(-jnp.expm1(-dt_gdn)),
        'gdn_norm': gain((N_C, GDN_DV)),
        'w_out_c': nrm((N_C, GDN_VW, d), GDN_VW ** -0.5),
        'moe_w_group': nrm((DEPTH, d, N_GROUPS), d ** -0.5),
        'moe_b_group': nrm((DEPTH, N_GROUPS), 0.01),
        'moe_w_expert': nrm((DEPTH, d, N_EXPERTS), d ** -0.5),
        'moe_b_expert': nrm((DEPTH, N_EXPERTS), 0.01),
        'moe_w1': nrm((DEPTH, N_EXPERTS, d, D_EXPERT), d ** -0.5),
        'moe_w3': nrm((DEPTH, N_EXPERTS, d, D_EXPERT), d ** -0.5),
        'moe_w2': nrm((DEPTH, N_EXPERTS, D_EXPERT, d), D_EXPERT ** -0.5),
    }


def reference(x_prompt, x_sample, c_prompt, c_sample, cache_mla_latent, cache_mla_krope, state_s5_re, state_s5_im,
              state_conv, state_gdn, w_mod, b_mod, norm_mix, norm_ffn, norm_final, w_in_ab, s5_a_re, s5_a_im,
              s5_b_re, s5_b_im, s5_c_re, s5_c_im, s5_d, s5_log_dt, s5_glu_w, s5_glu_b, mla_q_norm, mla_q_up,
              mla_kv_norm, mla_kv_up, w_out_ab, w_in_c, conv_w, gdn_a_log, gdn_dt_bias, gdn_norm, w_out_c,
              moe_w_group, moe_b_group, moe_w_expert, moe_b_expert, moe_w1, moe_w3, moe_w2):
    p = dict(w_mod=w_mod, b_mod=b_mod, norm_mix=norm_mix, norm_ffn=norm_ffn, norm_final=norm_final,
             w_in_ab=w_in_ab, s5_a_re=s5_a_re, s5_a_im=s5_a_im, s5_b_re=s5_b_re, s5_b_im=s5_b_im,
             s5_c_re=s5_c_re, s5_c_im=s5_c_im, s5_d=s5_d, s5_log_dt=s5_log_dt, s5_glu_w=s5_glu_w,
             s5_glu_b=s5_glu_b, mla_q_norm=mla_q_norm, mla_q_up=mla_q_up, mla_kv_norm=mla_kv_norm,
             mla_kv_up=mla_kv_up, w_out_ab=w_out_ab, w_in_c=w_in_c, conv_w=conv_w, gdn_a_log=gdn_a_log,
             gdn_dt_bias=gdn_dt_bias, gdn_norm=gdn_norm, w_out_c=w_out_c, moe_w_group=moe_w_group,
             moe_b_group=moe_b_group, moe_w_expert=moe_w_expert, moe_b_expert=moe_b_expert,
             moe_w1=moe_w1, moe_w3=moe_w3, moe_w2=moe_w2)
    y_prompt, p_lat, p_krope, p_s5_re, p_s5_im, p_conv, p_gdn = trunk(
        x_prompt, c_prompt, None, None, None, None, None, None, p)
    y_sample, s_lat, s_krope, s_s5_re, s_s5_im, s_conv, s_gdn = trunk(
        x_sample, c_sample, cache_mla_latent, cache_mla_krope, state_s5_re, state_s5_im, state_conv, state_gdn, p)
    return (y_prompt, y_sample, p_lat, p_krope, p_s5_re, p_s5_im, p_conv, p_gdn,
            s_lat, s_krope, s_s5_re, s_s5_im, s_conv, s_gdn)
```

```python
import functools
import math

import jax
import jax.numpy as jnp
from jax import lax
from jax.experimental import pallas as pl
from jax.experimental.pallas import tpu as pltpu

F32 = jnp.float32
BF16 = jnp.bfloat16
HI = lax.Precision.HIGHEST
EPS = 1e-6

D_MODEL = 1024
CHUNK = 64
S5_WIDTH = 512
S5_GROUP = 16
S5_GROUPS = 32
S5_STATE = 64
S5_CHUNK = 16
S5_PAIR = 2 * S5_CHUNK * S5_GROUP
MLA_HEADS = 8
MLA_NOPE = 64
MLA_ROPE = 32
MLA_V = 64
MLA_Q_RANK = 384
MLA_KV_RANK = 256
ROPE_THETA = 10000.0
GDN_HEADS = 8
GDN_DK = 128
GDN_DV = 128
CONV_W = 4
GDN_QKV = GDN_HEADS * (2 * GDN_DK + GDN_DV)
GDN_VW = GDN_HEADS * GDN_DV
N_GROUPS = 4
EXPERTS_PER_GROUP = 8
N_EXPERTS = 32
D_EXPERT = 512
TOP_K = 2
MOE_BLOCK = 128
LANES = 128
VMEM_LIMIT = 48 * 1024 * 1024


def _cparams(*sem):
    return pltpu.CompilerParams(dimension_semantics=sem, vmem_limit_bytes=VMEM_LIMIT)


def _dot(a, b):
    return jnp.dot(a, b, preferred_element_type=F32)


def _dot_nt(a, b, precision=None):
    return lax.dot_general(a, b, (((1,), (1,)), ((), ())), preferred_element_type=F32, precision=precision)


def _full(arr):
    nd = arr.ndim
    return pl.BlockSpec(arr.shape, lambda *_: (0,) * nd)


def _tok(width, ts):
    return pl.BlockSpec((1, ts, width), lambda b, j: (b, j, 0))


def _modspec(arr, ts):
    if arr.shape[1] == 1:
        return pl.BlockSpec((1, 1, arr.shape[2]), lambda b, j: (b, 0, 0))
    return pl.BlockSpec((1, ts, arr.shape[2]), lambda b, j: (b, j, 0))


def _normmod(x, gain, sc, sh):
    ms = jnp.mean(x * x, axis=-1, keepdims=True)
    return x * lax.rsqrt(ms + EPS) * gain * (1.0 + sc) + sh


def _rms(x, gain):
    return x * lax.rsqrt(jnp.mean(x * x, axis=-1, keepdims=True) + EPS) * gain


def _mod_kernel(c_ref, w_ref, b_ref, o_ref):
    c = c_ref[...]
    a = (c * jax.nn.sigmoid(c)).astype(BF16)
    o_ref[0] = _dot(a, w_ref[0]) + b_ref[0]


def _mod_call(c_all, w_mod, b_mod):
    depth, d, n6 = w_mod.shape
    bp = c_all.shape[0]
    tn = 1536
    return pl.pallas_call(
        _mod_kernel,
        grid=(depth, n6 // tn),
        in_specs=[pl.BlockSpec((bp, d), lambda l, n: (0, 0)),
                  pl.BlockSpec((1, d, tn), lambda l, n: (l, 0, n)),
                  pl.BlockSpec((1, 1, tn), lambda l, n: (l, 0, n))],
        out_specs=pl.BlockSpec((1, bp, tn), lambda l, n: (l, 0, n)),
        out_shape=jax.ShapeDtypeStruct((depth, bp, n6), F32),
        compiler_params=_cparams("parallel", "parallel"),
        name="mod",
    )(c_all, w_mod.astype(BF16), b_mod.reshape(depth, 1, n6))


def _ab_in_kernel(x_ref, gain_ref, sc_ref, sh_ref, w_ref, qg_ref, wqa_ref, wqb_ref, kvg_ref,
                  cq_ref, sq_ref, ck_ref, sk_ref, u_ref, q_ref, lat_ref, kr_ref):
    h = _normmod(x_ref[0], gain_ref[...], sc_ref[0], sh_ref[0]).astype(BF16)
    proj = _dot(h, w_ref[...])
    u_ref[0] = proj[:, :S5_WIDTH]
    o1 = S5_WIDTH + MLA_Q_RANK
    o2 = o1 + MLA_KV_RANK
    qn = _rms(proj[:, S5_WIDTH:o1], qg_ref[...]).astype(BF16)
    qa = _dot(qn, wqa_ref[...])
    qb = _dot(qn, wqb_ref[...])
    cq = cq_ref[...]
    sq = sq_ref[...]
    for hh in range(MLA_HEADS):
        sl = slice(LANES * hh, LANES * (hh + 1))
        q_ref[0, hh] = (qa[:, sl] * cq + qb[:, sl] * sq).astype(BF16)
    lat_ref[0] = _rms(proj[:, o1:o2], kvg_ref[...])
    kr_ref[0] = proj[:, o2:o2 + LANES] * ck_ref[...] + proj[:, o2 + LANES:o2 + 2 * LANES] * sk_ref[...]


def _ab_in_call(x, gain, sc, sh, wp, tabs, ts):
    b, s, d = x.shape
    cq, sq, ck, sk = tabs
    tab = pl.BlockSpec((ts, LANES), lambda bb, j: (j, 0))
    ins = [x, gain, sc, sh, wp["w_in"], wp["q_norm"], wp["wqa"], wp["wqb"], wp["kv_norm"], cq, sq, ck, sk]
    specs = [_tok(d, ts), _full(gain), _modspec(sc, ts), _modspec(sh, ts), _full(wp["w_in"]), _full(wp["q_norm"]),
             _full(wp["wqa"]), _full(wp["wqb"]), _full(wp["kv_norm"]), tab, tab, tab, tab]
    return pl.pallas_call(
        _ab_in_kernel,
        grid=(b, s // ts),
        in_specs=specs,
        out_specs=[_tok(S5_WIDTH, ts),
                   pl.BlockSpec((1, MLA_HEADS, ts, LANES), lambda bb, j: (bb, 0, j, 0)),
                   _tok(MLA_KV_RANK, ts), _tok(LANES, ts)],
        out_shape=[jax.ShapeDtypeStruct((b, s, S5_WIDTH), F32),
                   jax.ShapeDtypeStruct((b, MLA_HEADS, s, LANES), BF16),
                   jax.ShapeDtypeStruct((b, s, MLA_KV_RANK), F32),
                   jax.ShapeDtypeStruct((b, s, LANES), F32)],
        compiler_params=_cparams("parallel", "parallel"),
        name="ab_in",
    )(*ins)


def _kv_kernel(lat_ref, kr_ref, wk_ref, wv_ref, k_ref, v_ref):
    lat = lat_ref[0].astype(BF16)
    kk = _dot(lat, wk_ref[...])
    vv = _dot(lat, wv_ref[...])
    kr = kr_ref[0]
    for hh in range(MLA_HEADS):
        k_ref[0, hh] = (kk[:, LANES * hh:LANES * (hh + 1)] + kr).astype(BF16)
        v_ref[0, hh] = vv[:, MLA_V * hh:MLA_V * (hh + 1)].astype(BF16)


def _kv_call(lat_all, kr_all, wp, ts):
    b, sk, _ = lat_all.shape
    return pl.pallas_call(
        _kv_kernel,
        grid=(b, sk // ts),
        in_specs=[_tok(MLA_KV_RANK, ts), _tok(LANES, ts), _full(wp["wk"]), _full(wp["wv"])],
        out_specs=[pl.BlockSpec((1, MLA_HEADS, ts, LANES), lambda bb, j: (bb, 0, j, 0)),
                   pl.BlockSpec((1, MLA_HEADS, ts, MLA_V), lambda bb, j: (bb, 0, j, 0))],
        out_shape=[jax.ShapeDtypeStruct((b, MLA_HEADS, sk, LANES), BF16),
                   jax.ShapeDtypeStruct((b, MLA_HEADS, sk, MLA_V), BF16)],
        compiler_params=_cparams("parallel", "parallel"),
        name="kv_up",
    )(lat_all, kr_all, wp["wk"], wp["wv"])


def _attn_kernel(q_ref, k_ref, v_ref, o_ref, *, tq, tk, q_off, sk_valid):
    i = pl.program_id(2)
    q = q_ref[0, 0]
    q0 = q_off + i * tq
    lim_full = (q0 // CHUNK + 1) * CHUNK
    lim_tot = jnp.minimum(((q0 + tq - 1) // CHUNK + 1) * CHUNK, sk_valid)
    n_full = jnp.minimum(lim_full // tk, sk_valid // tk)
    n_tot = (lim_tot + tk - 1) // tk

    def step(j, carry, masked):
        m, l, acc = carry
        off = pl.multiple_of(j * tk, tk)
        k = k_ref[0, 0, pl.ds(off, tk), :]
        v = v_ref[0, 0, pl.ds(off, tk), :]
        s = _dot_nt(q, k)
        if masked:
            qpos = q0 + lax.broadcasted_iota(jnp.int32, (tq, tk), 0)
            kpos = off + lax.broadcasted_iota(jnp.int32, (tq, tk), 1)
            ok = jnp.logical_and(kpos // CHUNK <= qpos // CHUNK, kpos < sk_valid)
            s = jnp.where(ok, s, -1e30)
        m_new = jnp.maximum(m, jnp.max(s, axis=-1, keepdims=True))
        alpha = jnp.exp(m - m_new)
        p = jnp.exp(s - m_new)
        l = alpha * l + jnp.sum(p, axis=-1, keepdims=True)
        acc = alpha * acc + _dot(p.astype(BF16), v)
        return m_new, l, acc

    init = (jnp.full((tq, 1), -1e30, F32), jnp.zeros((tq, 1), F32), jnp.zeros((tq, MLA_V), F32))
    carry = lax.fori_loop(0, n_full, functools.partial(step, masked=False), init)
    m, l, acc = lax.fori_loop(n_full, n_tot, functools.partial(step, masked=True), carry)
    o_ref[0, 0] = (acc / l).astype(o_ref.dtype)


def _attn_call(q, k, v, tq, tk, q_off, sk_valid):
    b, h, sq, _ = q.shape
    sk = k.shape[2]
    kern = functools.partial(_attn_kernel, tq=tq, tk=tk, q_off=q_off, sk_valid=sk_valid)
    return pl.pallas_call(
        kern,
        grid=(b, h, sq // tq),
        in_specs=[pl.BlockSpec((1, 1, tq, LANES), lambda bb, hh, i: (bb, hh, i, 0)),
                  pl.BlockSpec((1, 1, sk, LANES), lambda bb, hh, i: (bb, hh, 0, 0)),
                  pl.BlockSpec((1, 1, sk, MLA_V), lambda bb, hh, i: (bb, hh, 0, 0))],
        out_specs=pl.BlockSpec((1, 1, tq, MLA_V), lambda bb, hh, i: (bb, hh, i, 0)),
        out_shape=jax.ShapeDtypeStruct((b, h, sq, MLA_V), BF16),
        compiler_params=_cparams("parallel", "parallel", "arbitrary"),
        name="mla_attn",
    )(q, k, v)


def _s5_w_kernel(u_ref, bre_ref, bim_ref, wre_ref, wim_ref):
    u = u_ref[...]
    wre_ref[...] = _dot(u, bre_ref[0])
    wim_ref[...] = _dot(u, bim_ref[0])


def _s5_w_call(uc, bre, bim, tr):
    r = uc.shape[0]
    npair = bre.shape[0]
    out = jax.ShapeDtypeStruct((r, npair * LANES), F32)
    ospec = pl.BlockSpec((tr, LANES), lambda g, i: (i, g))
    wspec = pl.BlockSpec((1, S5_PAIR, LANES), lambda g, i: (g, 0, 0))
    return pl.pallas_call(
        _s5_w_kernel,
        grid=(npair, r // tr),
        in_specs=[pl.BlockSpec((tr, S5_PAIR), lambda g, i: (i, g)), wspec, wspec],
        out_specs=[ospec, ospec],
        out_shape=[out, out],
        compiler_params=_cparams("parallel", "parallel"),
        name="s5_chunk_in",
    )(uc, bre, bim)


def _s5_scan_kernel(wre_ref, wim_ref, lre_ref, lim_ref, x0re_ref, x0im_ref, ore_ref, oim_ref, sre, sim, *, tc):
    @pl.when(pl.program_id(1) == 0)
    def _():
        sre[...] = x0re_ref[0]
        sim[...] = x0im_ref[0]

    lr = lre_ref[...]
    li = lim_ref[...]

    def blk(t, carry):
        xr, xi = carry
        base = pl.multiple_of(t * 8, 8)
        wr = wre_ref[0, pl.ds(base, 8), :]
        wi = wim_ref[0, pl.ds(base, 8), :]
        rows_r, rows_i = [], []
        for r in range(8):
            nr = lr * xr - li * xi + wr[r:r + 1]
            ni = lr * xi + li * xr + wi[r:r + 1]
            xr, xi = nr, ni
            rows_r.append(xr)
            rows_i.append(xi)
        ore_ref[0, pl.ds(base, 8), :] = jnp.concatenate(rows_r, axis=0)
        oim_ref[0, pl.ds(base, 8), :] = jnp.concatenate(rows_i, axis=0)
        return xr, xi

    xr, xi = lax.fori_loop(0, tc // 8, blk, (sre[...], sim[...]))
    sre[...] = xr
    sim[...] = xi


def _s5_scan_call(wre, wim, lre, lim, x0re, x0im, tc):
    b, nch, n = wre.shape
    wspec = pl.BlockSpec((1, tc, n), lambda bb, c: (bb, c, 0))
    lspec = pl.BlockSpec((1, n), lambda bb, c: (0, 0))
    xspec = pl.BlockSpec((1, 1, n), lambda bb, c: (bb, 0, 0))
    out = jax.ShapeDtypeStruct((b, nch, n), F32)
    return pl.pallas_call(
        functools.partial(_s5_scan_kernel, tc=tc),
        grid=(b, nch // tc),
        in_specs=[wspec, wspec, lspec, lspec, xspec, xspec],
        out_specs=[wspec, wspec],
        out_shape=[out, out],
        scratch_shapes=[pltpu.VMEM((1, n), F32), pltpu.VMEM((1, n), F32)],
        compiler_params=_cparams("parallel", "arbitrary"),
        name="s5_scan",
    )(wre, wim, lre, lim, x0re, x0im)


def _s5_y_kernel(u_ref, xre_ref, xim_ref, t_ref, cre_ref, cim_ref, y_ref):
    y = _dot(u_ref[...], t_ref[0])
    y += _dot(xre_ref[...].astype(BF16), cre_ref[0])
    y += _dot(xim_ref[...].astype(BF16), cim_ref[0])
    y_ref[...] = y


def _s5_y_call(uc, xre, xim, tm, cre, cim, tr):
    r = uc.shape[0]
    npair = tm.shape[0]
    uspec = pl.BlockSpec((tr, S5_PAIR), lambda g, i: (i, g))
    xspec = pl.BlockSpec((tr, LANES), lambda g, i: (i, g))
    return pl.pallas_call(
        _s5_y_kernel,
        grid=(npair, r // tr),
        in_specs=[uspec, xspec, xspec,
                  pl.BlockSpec((1, S5_PAIR, S5_PAIR), lambda g, i: (g, 0, 0)),
                  pl.BlockSpec((1, LANES, S5_PAIR), lambda g, i: (g, 0, 0)),
                  pl.BlockSpec((1, LANES, S5_PAIR), lambda g, i: (g, 0, 0))],
        out_specs=uspec,
        out_shape=jax.ShapeDtypeStruct((r, npair * S5_PAIR), F32),
        compiler_params=_cparams("parallel", "parallel"),
        name="s5_chunk_out",
    )(uc, xre, xim, tm, cre, cim)


def _s5_mats(a_re, a_im, b_re, b_im, c_re, c_im, log_dt):
    g, p = a_re.shape
    L = S5_CHUNK
    lam = lax.complex(jnp.minimum(a_re, -1e-4), a_im)
    lamdt = lam * jnp.exp(log_dt)[:, None]
    lam_bar = jnp.exp(lamdt)
    b_bar = ((lam_bar - 1.0) / lam)[:, :, None] * lax.complex(b_re, b_im)
    cm = lax.complex(c_re, c_im)
    pw = jnp.exp(lamdt[:, :, None] * jnp.arange(L + 1, dtype=F32))
    mm = pw[:, :, :L, None] * b_bar[:, :, None, :]
    kd = jnp.sum((cm[:, :, :, None, None] * mm[:, None]).real, axis=2).transpose(0, 2, 1, 3)
    s_i = jnp.arange(L)[:, None]
    t_i = jnp.arange(L)[None, :]
    dd = t_i - s_i
    tm = jnp.where((dd >= 0)[None, :, :, None, None], kd[:, jnp.clip(dd, 0, L - 1)], 0.0)
    tm = tm.transpose(0, 1, 4, 2, 3).reshape(g, L * S5_GROUP, L * S5_GROUP)
    bp = pw[:, :, L - 1 - jnp.arange(L)][:, :, :, None] * b_bar[:, :, None, :]
    bp = bp.transpose(0, 2, 3, 1).reshape(g, L * S5_GROUP, p)
    cp = cm[:, :, :, None] * pw[:, None, :, 1:]
    cp = cp.transpose(0, 2, 3, 1).reshape(g, p, L * S5_GROUP)

    def pair_diag(m):
        gg, r, c = m.shape
        m2 = m.reshape(gg // 2, 2, r, c)
        z = jnp.zeros_like(m2[:, 0])
        top = jnp.concatenate([m2[:, 0], z], axis=2)
        bot = jnp.concatenate([z, m2[:, 1]], axis=2)
        return jnp.concatenate([top, bot], axis=1)

    return dict(
        tm=pair_diag(tm).astype(BF16),
        bre=pair_diag(bp.real).astype(BF16), bim=pair_diag(bp.imag).astype(BF16),
        cre=pair_diag(cp.real).astype(BF16), cim=pair_diag(-cp.imag).astype(BF16),
        lre=pw[:, :, L].real.reshape(1, g * p), lim=pw[:, :, L].imag.reshape(1, g * p))


def _ab_out_kernel(x_ref, ys_ref, u_ref, at_ref, d_ref, gw_ref, gb_ref, ws_ref, wa_ref, g1_ref, o_ref):
    y = ys_ref[0] + d_ref[...] * u_ref[0]
    y = jax.nn.gelu(y)
    gate = jax.nn.sigmoid(_dot(y.astype(BF16), gw_ref[...]) + gb_ref[...])
    out = _dot((y * gate).astype(BF16), ws_ref[...])
    for hh in range(MLA_HEADS):
        out += _dot(at_ref[0, hh], wa_ref[hh])
    o_ref[0] = x_ref[0] + g1_ref[0] * out


def _ab_out_call(x, ys, u, attn, wp, g1, ts):
    b, s, d = x.shape
    ins = [x, ys, u, attn, wp["s5_d"], wp["glu_w"], wp["glu_b"], wp["w_out_s5"], wp["w_out_at"], g1]
    specs = [_tok(d, ts), _tok(S5_WIDTH, ts), _tok(S5_WIDTH, ts),
             pl.BlockSpec((1, MLA_HEADS, ts, MLA_V), lambda bb, j: (bb, 0, j, 0)),
             _full(wp["s5_d"]), _full(wp["glu_w"]), _full(wp["glu_b"]), _full(wp["w_out_s5"]),
             _full(wp["w_out_at"]), _modspec(g1, ts)]
    return pl.pallas_call(
        _ab_out_kernel,
        grid=(b, s // ts),
        in_specs=specs,
        out_specs=_tok(d, ts),
        out_shape=jax.ShapeDtypeStruct((b, s, d), F32),
        compiler_params=_cparams("parallel", "parallel"),
        name="ab_out",
    )(*ins)


def _c_in_kernel(x_ref, gain_ref, sc_ref, sh_ref, w_ref, qkv_ref, z_ref, ab_ref):
    h = _normmod(x_ref[0], gain_ref[...], sc_ref[0], sh_ref[0]).astype(BF16)
    proj = _dot(h, w_ref[...])
    qkv_ref[0] = proj[:, :GDN_QKV].astype(BF16)
    z_ref[0] = proj[:, GDN_QKV:GDN_QKV + GDN_VW].astype(BF16)
    ab_ref[0] = proj[:, GDN_QKV + GDN_VW:]


def _c_in_call(x, gain, sc, sh, w, ts):
    b, s, d = x.shape
    return pl.pallas_call(
        _c_in_kernel,
        grid=(b, s // ts),
        in_specs=[_tok(d, ts), _full(gain), _modspec(sc, ts), _modspec(sh, ts), _full(w)],
        out_specs=[_tok(GDN_QKV, ts), _tok(GDN_VW, ts), _tok(LANES, ts)],
        out_shape=[jax.ShapeDtypeStruct((b, s, GDN_QKV), BF16),
                   jax.ShapeDtypeStruct((b, s, GDN_VW), BF16),
                   jax.ShapeDtypeStruct((b, s, LANES), F32)],
        compiler_params=_cparams("parallel", "parallel"),
        name="c_in",
    )(x, gain, sc, sh, w)


def _gdn_prep_kernel(qkv_ref, halo_ref, st_ref, cw_ref, ab_ref, alog_ref, dtb_ref,
                     w_ref, u_ref, qg_ref, kgt_ref, pm_ref, ee_ref, *, L):
    j = pl.program_id(1)
    prev = jnp.where(j == 0, st_ref[0].astype(F32), halo_ref[0].astype(F32))
    xx = jnp.concatenate([prev, qkv_ref[0].astype(F32)], axis=0)
    cw = cw_ref[...]
    y = xx[8:8 + L] * cw[CONV_W - 1:CONV_W]
    for t in range(1, CONV_W):
        y += xx[8 - t:8 - t + L] * cw[CONV_W - 1 - t:CONV_W - t]
    y = y * jax.nn.sigmoid(y)

    ab = ab_ref[0]
    g_all = -jnp.exp(alog_ref[...]) * jax.nn.softplus(ab + dtb_ref[...])
    beta_all = jax.nn.sigmoid(ab)
    row = lax.broadcasted_iota(jnp.int32, (L, L), 0)
    col = lax.broadcasted_iota(jnp.int32, (L, L), 1)
    incl = row >= col
    strict = row > col
    gam_all = jnp.dot(incl.astype(F32), g_all, preferred_element_type=F32, precision=HI)
    eye_l = (lax.broadcasted_iota(jnp.int32, (LANES, LANES), 0)
             == lax.broadcasted_iota(jnp.int32, (LANES, LANES), 1))
    gam_t = _dot_nt(eye_l.astype(F32), gam_all, precision=HI)
    ee_ref[0, 0] = jnp.exp(gam_all[L - 1:L])
    eye64 = (row == col).astype(F32)
    eye_bf = eye_l.astype(BF16)

    for hh in range(GDN_HEADS):
        qh = y[:, GDN_DK * hh:GDN_DK * (hh + 1)]
        kh = y[:, GDN_HEADS * GDN_DK + GDN_DK * hh:GDN_HEADS * GDN_DK + GDN_DK * (hh + 1)]
        vh = y[:, 2 * GDN_HEADS * GDN_DK + GDN_DV * hh:2 * GDN_HEADS * GDN_DK + GDN_DV * (hh + 1)]
        qh = qh * lax.rsqrt(jnp.sum(qh * qh, axis=-1, keepdims=True) + EPS) * (GDN_DK ** -0.5)
        kh = kh * lax.rsqrt(jnp.sum(kh * kh, axis=-1, keepdims=True) + EPS)
        gam_c = gam_all[:, hh:hh + 1]
        beta_c = beta_all[:, GDN_HEADS + hh:GDN_HEADS + hh + 1]
        dec = jnp.exp(jnp.minimum(gam_c - gam_t[hh:hh + 1, :], 0.0))
        kb = kh.astype(BF16)
        kk = _dot_nt(kb, kb)
        qk = _dot_nt(qh.astype(BF16), kb)
        a = jnp.where(strict, beta_c * kk * dec, 0.0)
        pm_ref[0, 0, hh] = jnp.where(incl, qk * dec, 0.0).astype(BF16)
        tinv = eye64 - a
        pw = a
        for _ in range(max(L.bit_length() - 2, 0)):
            pw = jnp.dot(pw, pw, preferred_element_type=F32, precision=HI)
            tinv = tinv + jnp.dot(tinv, pw, preferred_element_type=F32, precision=HI)
        eg = jnp.exp(gam_c)
        rhs = jnp.concatenate([(beta_c * eg) * kh, beta_c * vh], axis=1).astype(BF16)
        wu = _dot(tinv.astype(BF16), rhs)
        w_ref[0, hh] = wu[:, :GDN_DK].astype(BF16)
        u_ref[0, hh] = wu[:, GDN_DK:].astype(BF16)
        qg_ref[0, hh] = (qh * eg).astype(BF16)
        kg = (kh * jnp.exp(gam_all[L - 1:L, hh:hh + 1] - gam_c)).astype(BF16)
        kgt_ref[0, 0, hh] = _dot_nt(eye_bf, kg).astype(BF16)


def _gdn_prep_call(qkv, st8, conv_w, ab, alog, dtb, L):
    b, s, _ = qkv.shape
    nch = s // L
    hs = pl.BlockSpec((1, GDN_HEADS, L, LANES), lambda bb, j: (bb, 0, j, 0))
    hshape = jax.ShapeDtypeStruct((b, GDN_HEADS, s, LANES), BF16)
    return pl.pallas_call(
        functools.partial(_gdn_prep_kernel, L=L),
        grid=(b, nch),
        in_specs=[_tok(GDN_QKV, L),
                  pl.BlockSpec((1, 8, GDN_QKV), lambda bb, j: (bb, jnp.maximum(j * (L // 8) - 1, 0), 0)),
                  pl.BlockSpec((1, 8, GDN_QKV), lambda bb, j: (bb, 0, 0)),
                  _full(conv_w), _tok(LANES, L), _full(alog), _full(dtb)],
        out_specs=[hs, hs, hs,
                   pl.BlockSpec((1, 1, GDN_HEADS, GDN_DK, L), lambda bb, j: (bb, j, 0, 0, 0)),
                   pl.BlockSpec((1, 1, GDN_HEADS, L, L), lambda bb, j: (bb, j, 0, 0, 0)),
                   pl.BlockSpec((1, 1, 1, LANES), lambda bb, j: (bb, j, 0, 0))],
        out_shape=[hshape, hshape, hshape,
                   jax.ShapeDtypeStruct((b, nch, GDN_HEADS, GDN_DK, L), BF16),
                   jax.ShapeDtypeStruct((b, nch, GDN_HEADS, L, L), BF16),
                   jax.ShapeDtypeStruct((b, nch, 1, LANES), F32)],
        compiler_params=_cparams("parallel", "parallel"),
        name="gdn_prep",
    )(qkv, qkv, st8, conv_w, ab, alog, dtb)


GDN_BATCH_GROUP = 2


def _gdn_seq_kernel(w_ref, u_ref, qg_ref, kgt_ref, pm_ref, ee_ref, s0_ref, o_ref, sf_ref, s_scr):
    c = pl.program_id(1)

    @pl.when(c == 0)
    def _():
        s_scr[...] = s0_ref[...]

    for bb in range(GDN_BATCH_GROUP):
        ee = ee_ref[bb, 0]
        for hh in range(GDN_HEADS):
            st = s_scr[bb, hh]
            sb = st.astype(BF16)
            v_new = (u_ref[bb, hh].astype(F32) - _dot(w_ref[bb, hh], sb)).astype(BF16)
            o = _dot(qg_ref[bb, hh], sb) + _dot(pm_ref[bb, 0, hh], v_new)
            o_ref[bb, :, GDN_DV * hh:GDN_DV * (hh + 1)] = o.astype(o_ref.dtype)
            s_scr[bb, hh] = ee[:, hh:hh + 1] * st + _dot(kgt_ref[bb, 0, hh], v_new)

    @pl.when(c == pl.num_programs(1) - 1)
    def _():
        sf_ref[...] = s_scr[...]


def _gdn_seq_call(wm, um, qg, kgt, pm, ee, s0, L):
    b, h, s, _ = wm.shape
    nch = s // L
    bg = GDN_BATCH_GROUP
    hs = pl.BlockSpec((bg, h, L, LANES), lambda g, c: (g, 0, c, 0))
    sspec = pl.BlockSpec((bg, h, GDN_DK, GDN_DV), lambda g, c: (g, 0, 0, 0))
    return pl.pallas_call(
        _gdn_seq_kernel,
        grid=(b // bg, nch),
        in_specs=[hs, hs, hs,
                  pl.BlockSpec((bg, 1, h, GDN_DK, L), lambda g, c: (g, c, 0, 0, 0)),
                  pl.BlockSpec((bg, 1, h, L, L), lambda g, c: (g, c, 0, 0, 0)),
                  pl.BlockSpec((bg, 1, 1, LANES), lambda g, c: (g, c, 0, 0)),
                  sspec],
        out_specs=[pl.BlockSpec((bg, L, GDN_VW), lambda g, c: (g, c, 0)), sspec],
        out_shape=[jax.ShapeDtypeStruct((b, s, GDN_VW), BF16),
                   jax.ShapeDtypeStruct((b, h, GDN_DK, GDN_DV), F32)],
        scratch_shapes=[pltpu.VMEM((bg, h, GDN_DK, GDN_DV), F32)],
        compiler_params=_cparams("parallel", "arbitrary"),
        name="gdn_seq",
    )(wm, um, qg, kgt, pm, ee, s0)


def _c_out_kernel(x_ref, o_ref, z_ref, gn_ref, w_ref, g1_ref, out_ref):
    o = o_ref[0].astype(F32)
    z = z_ref[0].astype(F32)
    gn = gn_ref[...]
    parts = []
    for hh in range(GDN_HEADS):
        oh = o[:, GDN_DV * hh:GDN_DV * (hh + 1)]
        zh = z[:, GDN_DV * hh:GDN_DV * (hh + 1)]
        parts.append((_rms(oh, gn) * (zh * jax.nn.sigmoid(zh))).astype(BF16))
    out = _dot(jnp.concatenate(parts, axis=1), w_ref[...])
    out_ref[0] = x_ref[0] + g1_ref[0] * out


def _c_out_call(x, o, z, gn, w, g1, ts):
    b, s, d = x.shape
    return pl.pallas_call(
        _c_out_kernel,
        grid=(b, s // ts),
        in_specs=[_tok(d, ts), _tok(GDN_VW, ts), _tok(GDN_VW, ts), _full(gn), _full(w), _modspec(g1, ts)],
        out_specs=_tok(d, ts),
        out_shape=jax.ShapeDtypeStruct((b, s, d), F32),
        compiler_params=_cparams("parallel", "parallel"),
        name="c_out",
    )(x, o, z, gn, w, g1)


def _router_kernel(x_ref, gain_ref, sc_ref, sh_ref, wr_ref, br_ref, h_ref, re_ref, rg_ref):
    h = _normmod(x_ref[0], gain_ref[...], sc_ref[0], sh_ref[0])
    h_ref[0] = h
    logits = jnp.dot(h, wr_ref[...], preferred_element_type=F32, precision=HI) + br_ref[...]
    lane = lax.broadcasted_iota(jnp.int32, logits.shape, 1).astype(F32)
    neg = jnp.float32(-jnp.inf)
    big = jnp.float32(1e9)
    gl = jnp.where(lane < N_GROUPS, logits, neg)
    gm = jnp.max(gl, axis=-1, keepdims=True)
    g_val = 1.0 / jnp.sum(jnp.exp(gl - gm), axis=-1, keepdims=True)
    g_idx = jnp.min(jnp.where(gl == gm, lane, big), axis=-1, keepdims=True)
    lo = N_GROUPS + EXPERTS_PER_GROUP * g_idx
    el = jnp.where(jnp.logical_and(lane >= lo, lane < lo + EXPERTS_PER_GROUP), logits, neg)
    em = jnp.max(el, axis=-1, keepdims=True)
    i1 = jnp.min(jnp.where(el == em, lane, big), axis=-1, keepdims=True)
    el2 = jnp.where(lane == i1, neg, el)
    em2 = jnp.max(el2, axis=-1, keepdims=True)
    i2 = jnp.min(jnp.where(el2 == em2, lane, big), axis=-1, keepdims=True)
    es = jnp.sum(jnp.exp(el - em), axis=-1, keepdims=True)
    p1 = 1.0 / es
    p2 = jnp.exp(em2 - em) / es
    den = p1 + p2
    re_ref[0] = jnp.where(lane == 0, i1 - N_GROUPS, jnp.where(lane == 1, i2 - N_GROUPS, 0.0)).astype(jnp.int32)
    rg_ref[0] = jnp.where(lane == 0, g_val * p1 / den, jnp.where(lane == 1, g_val * p2 / den, 0.0))


def _router_call(x, gain, sc, sh, wr, br, ts):
    b, s, d = x.shape
    return pl.pallas_call(
        _router_kernel,
        grid=(b, s // ts),
        in_specs=[_tok(d, ts), _full(gain), _modspec(sc, ts), _modspec(sh, ts), _full(wr), _full(br)],
        out_specs=[_tok(d, ts), _tok(LANES, ts), _tok(LANES, ts)],
        out_shape=[jax.ShapeDtypeStruct((b, s, d), F32),
                   jax.ShapeDtypeStruct((b, s, LANES), jnp.int32),
                   jax.ShapeDtypeStruct((b, s, LANES), F32)],
        compiler_params=_cparams("parallel", "parallel"),
        name="moe_router",
    )(x, gain, sc, sh, wr, br)


def _row_copy(src_hbm, dst_vmem, src_row, dst_row, sem):
    return pltpu.make_async_copy(src_hbm.at[pl.ds(src_row, 1)], dst_vmem.at[pl.ds(dst_row, 1)], sem)


def _ffn_kernel(be_ref, nu_ref, tok_ref, h_hbm, w1_ref, w3_ref, w2_ref, y_ref, xb, sem):
    i = pl.program_id(0)

    @pl.when(i < nu_ref[0])
    def _():
        def issue(r, c):
            _row_copy(h_hbm, xb, tok_ref[0, 0, r], r, sem).start()
            return c
        lax.fori_loop(0, MOE_BLOCK, issue, 0)

        def drain(r, c):
            _row_copy(h_hbm, xb, 0, r, sem).wait()
            return c
        lax.fori_loop(0, MOE_BLOCK, drain, 0)
        x = xb[...].astype(BF16)
        a = _dot(x, w1_ref[0])
        g = _dot(x, w3_ref[0])
        mid = (a * jax.nn.sigmoid(a) * g).astype(BF16)
        y_ref[...] = _dot(mid, w2_ref[0])

    @pl.when(i >= nu_ref[0])
    def _():
        y_ref[...] = jnp.zeros_like(y_ref)


def _ffn_call(block_e, n_used, buf_tok, h2, w1, w3, w2):
    n_blocks = block_e.shape[0]
    d = h2.shape[1]
    grid_spec = pltpu.PrefetchScalarGridSpec(
        num_scalar_prefetch=2,
        grid=(n_blocks,),
        in_specs=[pl.BlockSpec((1, 1, MOE_BLOCK), lambda i, be, nu: (i, 0, 0), memory_space=pltpu.SMEM),
                  pl.BlockSpec(memory_space=pl.ANY),
                  pl.BlockSpec((1, d, D_EXPERT), lambda i, be, nu: (be[i], 0, 0)),
                  pl.BlockSpec((1, d, D_EXPERT), lambda i, be, nu: (be[i], 0, 0)),
                  pl.BlockSpec((1, D_EXPERT, d), lambda i, be, nu: (be[i], 0, 0))],
        out_specs=pl.BlockSpec((MOE_BLOCK, d), lambda i, be, nu: (i, 0)),
        scratch_shapes=[pltpu.VMEM((MOE_BLOCK, d), F32), pltpu.SemaphoreType.DMA(())],
    )
    return pl.pallas_call(
        _ffn_kernel,
        grid_spec=grid_spec,
        out_shape=jax.ShapeDtypeStruct((n_blocks * MOE_BLOCK, d), F32),
        compiler_params=_cparams("arbitrary"),
        name="moe_ffn",
    )(block_e, n_used, buf_tok.reshape(n_blocks, 1, MOE_BLOCK), h2, w1, w3, w2)


def _combine_kernel(dest_ref, yb_hbm, x_ref, rg_ref, g2_ref, fin_ref, o_ref, b0, b1, sem, *, tb, final):
    def issue(r, c):
        _row_copy(yb_hbm, b0, dest_ref[0, 0, 2 * r], r, sem).start()
        _row_copy(yb_hbm, b1, dest_ref[0, 0, 2 * r + 1], r, sem).start()
        return c
    lax.fori_loop(0, tb, issue, 0)

    def drain(r, c):
        _row_copy(yb_hbm, b0, 0, r, sem).wait()
        _row_copy(yb_hbm, b1, 0, r, sem).wait()
        return c
    lax.fori_loop(0, tb, drain, 0)
    rg = rg_ref[0]
    y = rg[:, 0:1] * b0[...] + rg[:, 1:2] * b1[...]
    xn = x_ref[0] + g2_ref[0] * y
    if final:
        xn = _rms(xn, fin_ref[...])
    o_ref[0] = xn


def _combine_call(dest, yb, x, rg, g2, fin, tb, final):
    b, s, d = x.shape
    nb = s // tb
    return pl.pallas_call(
        functools.partial(_combine_kernel, tb=tb, final=final),
        grid=(b, nb),
        in_specs=[pl.BlockSpec((1, 1, 2 * tb), lambda bb, j: (bb * nb + j, 0, 0), memory_space=pltpu.SMEM),
                  pl.BlockSpec(memory_space=pl.ANY),
                  _tok(d, tb), _tok(LANES, tb), _modspec(g2, tb), _full(fin)],
        out_specs=_tok(d, tb),
        out_shape=jax.ShapeDtypeStruct((b, s, d), F32),
        scratch_shapes=[pltpu.VMEM((tb, d), F32), pltpu.VMEM((tb, d), F32), pltpu.SemaphoreType.DMA(())],
        compiler_params=_cparams("arbitrary", "arbitrary"),
        name="moe_combine",
    )(dest.reshape(b * nb, 1, 2 * tb), yb, x, rg, g2, fin)


def _moe(x, gain, sc, sh, g2, mp, fin, ts, final):
    b, s, d = x.shape
    n = b * s
    h2, r_e, r_g = _router_call(x, gain, sc, sh, mp["wr"], mp["br"], ts)
    eid = r_e[:, :, :TOP_K].reshape(n * TOP_K)
    onehot = (eid[:, None] == jnp.arange(N_EXPERTS, dtype=jnp.int32)[None, :]).astype(jnp.int32)
    csum = jnp.cumsum(onehot, axis=0)
    counts = csum[-1]
    rank = jnp.sum(onehot * (csum - 1), axis=1)
    padded = (counts + MOE_BLOCK - 1) // MOE_BLOCK * MOE_BLOCK
    pad_end = jnp.cumsum(padded)
    pad_start = pad_end - padded
    dest = (jnp.sum(onehot * pad_start[None, :], axis=1) + rank).astype(jnp.int32)
    n_blocks = (n * TOP_K + N_EXPERTS * (MOE_BLOCK - 1) + MOE_BLOCK - 1) // MOE_BLOCK
    block_e = jnp.minimum(jnp.searchsorted(pad_end, jnp.arange(n_blocks, dtype=jnp.int32) * MOE_BLOCK, side="right"),
                          N_EXPERTS - 1).astype(jnp.int32)
    n_used = (pad_end[-1:] // MOE_BLOCK).astype(jnp.int32)
    tok = jnp.arange(n * TOP_K, dtype=jnp.int32) // TOP_K
    buf_tok = jnp.zeros((n_blocks * MOE_BLOCK,), jnp.int32).at[dest].set(tok)
    yb = _ffn_call(block_e, n_used, buf_tok, h2.reshape(n, d), mp["w1"], mp["w3"], mp["w2"])
    return _combine_call(dest, yb, x, r_g, g2, fin, ts, final)


def _prep_ab(l, p):
    w_in = p["w_in_ab"][l]
    o2 = S5_WIDTH + MLA_Q_RANK + MLA_KV_RANK
    half = MLA_ROPE // 2
    d = w_in.shape[0]
    kr = w_in[:, o2:]
    kr_rot = jnp.concatenate([-kr[:, half:], kr[:, :half]], axis=1)
    z64 = jnp.zeros((d, MLA_NOPE), F32)
    z32 = jnp.zeros((d, LANES - MLA_NOPE - MLA_ROPE), F32)
    w_ext = jnp.concatenate([w_in[:, :o2], z64, kr, z32, z64, kr_rot, z32], axis=1)
    qu = p["mla_q_up"][l].reshape(MLA_Q_RANK, MLA_HEADS, MLA_NOPE + MLA_ROPE)
    qn, qr = qu[:, :, :MLA_NOPE], qu[:, :, MLA_NOPE:]
    zq = jnp.zeros((MLA_Q_RANK, MLA_HEADS, LANES - MLA_NOPE - MLA_ROPE), F32)
    wqa = jnp.concatenate([qn, qr, zq], axis=2).reshape(MLA_Q_RANK, MLA_HEADS * LANES)
    wqb = jnp.concatenate([jnp.zeros_like(qn), -qr[:, :, half:], qr[:, :, :half], zq], axis=2)
    wqb = wqb.reshape(MLA_Q_RANK, MLA_HEADS * LANES)
    kvu = p["mla_kv_up"][l].reshape(MLA_KV_RANK, MLA_HEADS, MLA_NOPE + MLA_V)
    wk = jnp.concatenate([kvu[:, :, :MLA_NOPE], jnp.zeros((MLA_KV_RANK, MLA_HEADS, LANES - MLA_NOPE), F32)], axis=2)
    wo = p["w_out_ab"][l]
    out = dict(
        w_in=w_ext.astype(BF16), q_norm=p["mla_q_norm"][l].reshape(1, -1), kv_norm=p["mla_kv_norm"][l].reshape(1, -1),
        wqa=wqa.astype(BF16), wqb=wqb.astype(BF16),
        wk=wk.reshape(MLA_KV_RANK, MLA_HEADS * LANES).astype(BF16),
        wv=kvu[:, :, MLA_NOPE:].reshape(MLA_KV_RANK, MLA_HEADS * MLA_V).astype(BF16),
        s5_d=p["s5_d"][l].reshape(1, -1), glu_w=p["s5_glu_w"][l].astype(BF16), glu_b=p["s5_glu_b"][l].reshape(1, -1),
        w_out_s5=wo[:S5_WIDTH].astype(BF16),
        w_out_at=wo[S5_WIDTH:].reshape(MLA_HEADS, MLA_V, -1).astype(BF16))
    out.update(_s5_mats(p["s5_a_re"][l], p["s5_a_im"][l], p["s5_b_re"][l], p["s5_b_im"][l],
                        p["s5_c_re"][l], p["s5_c_im"][l], p["s5_log_dt"][l]))
    return out


def _prep_c(l, p):
    w = p["w_in_c"][l]
    pad = jnp.zeros((w.shape[0], LANES - 2 * GDN_HEADS), F32)

    def row(v):
        return jnp.concatenate([v, jnp.zeros((LANES - GDN_HEADS,), F32)]).reshape(1, LANES)

    return dict(w_in=jnp.concatenate([w, pad], axis=1).astype(BF16), conv_w=p["conv_w"][l],
                alog=row(p["gdn_a_log"][l]), dtb=row(p["gdn_dt_bias"][l]),
                gn=p["gdn_norm"][l].reshape(1, -1), w_out=p["w_out_c"][l].astype(BF16))


def _prep_moe(layer, p):
    d = p["moe_w_group"].shape[1]
    wr = jnp.concatenate([p["moe_w_group"][layer], p["moe_w_expert"][layer],
                          jnp.zeros((d, LANES - N_GROUPS - N_EXPERTS), F32)], axis=1)
    br = jnp.concatenate([p["moe_b_group"][layer], p["moe_b_expert"][layer],
                          jnp.zeros((LANES - N_GROUPS - N_EXPERTS,), F32)]).reshape(1, LANES)
    return dict(wr=wr, br=br, w1=p["moe_w1"][layer].astype(BF16), w3=p["moe_w3"][layer].astype(BF16),
                w2=p["moe_w2"][layer].astype(BF16))


def _rope_tables(pos):
    half = MLA_ROPE // 2
    inv = ROPE_THETA ** (-jnp.arange(half, dtype=F32) / half)
    ang = pos.astype(F32)[:, None] * inv[None, :]
    cos, sin = jnp.cos(ang), jnp.sin(ang)
    n = pos.shape[0]
    z64 = jnp.zeros((n, MLA_NOPE), F32)
    z32 = jnp.zeros((n, LANES - MLA_NOPE - MLA_ROPE), F32)
    scale = (MLA_NOPE + MLA_ROPE) ** -0.5
    ck = jnp.concatenate([z64, cos, cos, z32], axis=1)
    sk = jnp.concatenate([z64, sin, sin, z32], axis=1)
    cq = jnp.concatenate([jnp.ones_like(z64), cos, cos, z32], axis=1) * scale
    return cq, sk * scale, ck, sk


def _trunk(x, mods, caches, wab, wc, wmoe, p, flat):
    b, s, d = x.shape
    lat_c, rope_c, s5re_c, s5im_c, conv_c, gdn_c = caches
    past = 0 if lat_c is None else lat_c.shape[2]
    pos = past + jnp.arange(s)
    if flat:
        xt = x.reshape(1, b * s, d)
        ts = b * s
        pos_rows = jnp.tile(pos, b)

        def mrow(m):
            return jnp.repeat(m, s, axis=0).reshape(1, b * s, d)
    else:
        xt = x
        ts = 256
        pos_rows = pos

        def mrow(m):
            return m.reshape(b, 1, d)
    bt, st = xt.shape[:2]
    outs = {}
    depth = p["norm_mix"].shape[0]
    for layer in range(depth):
        l = layer // 2
        sh1, sc1, g1, sh2, sc2, g2 = [mrow(m) for m in jnp.split(mods[layer], 6, axis=-1)]
        gain = p["norm_mix"][layer].reshape(1, d)
        if layer % 2 == 0:
            wp = wab[l]
            u, q, lat, kr = _ab_in_call(xt, gain, sc1, sh1, wp, _rope_tables(pos_rows), ts)
            lat_b = lat.reshape(b, s, MLA_KV_RANK)
            kr_b = kr.reshape(b, s, LANES)
            outs["lat"] = lat_b
            outs["krope"] = kr_b[:, :, MLA_NOPE:MLA_NOPE + MLA_ROPE]
            if lat_c is None:
                lat_all, kr_all, sk_valid, tkv, tq, tk = lat_b, kr_b, s, 256, 256, 256
            else:
                krc = jnp.pad(rope_c[l], ((0, 0), (0, 0), (MLA_NOPE, LANES - MLA_NOPE - MLA_ROPE)))
                sk_valid = past + s
                tk = 128
                skp = (sk_valid + tk - 1) // tk * tk
                lat_all = jnp.pad(jnp.concatenate([lat_c[l], lat_b], axis=1), ((0, 0), (0, skp - sk_valid), (0, 0)))
                kr_all = jnp.pad(jnp.concatenate([krc, kr_b], axis=1), ((0, 0), (0, skp - sk_valid), (0, 0)))
                tkv, tq = tk, s
            kk, vv = _kv_call(lat_all, kr_all, wp, tkv)
            qh = q.reshape(MLA_HEADS, b, s, LANES).transpose(1, 0, 2, 3) if flat else q
            attn = _attn_call(qh, kk, vv, tq, tk, past, sk_valid)
            if flat:
                attn = attn.transpose(1, 0, 2, 3).reshape(1, MLA_HEADS, b * s, MLA_V)
            nchk = s // S5_CHUNK
            r = b * nchk
            uc = u.reshape(b, nchk, S5_CHUNK, S5_GROUPS, S5_GROUP).transpose(0, 1, 3, 2, 4)
            uc = uc.reshape(r, S5_GROUPS * S5_CHUNK * S5_GROUP).astype(BF16)
            tr = min(r, 512)
            wre, wim = _s5_w_call(uc, wp["bre"], wp["bim"], tr)
            ngp = S5_GROUPS * S5_STATE

            def to_pairs(v):
                return v.reshape(b, 1, ngp)

            if s5re_c is None:
                x0re = jnp.zeros((b, 1, ngp), F32)
                x0im = jnp.zeros((b, 1, ngp), F32)
            else:
                x0re, x0im = to_pairs(s5re_c[l]), to_pairs(s5im_c[l])
            nchp = (nchk + 7) // 8 * 8
            tc = min(nchp, 128)
            wre3 = jnp.pad(wre.reshape(b, nchk, ngp), ((0, 0), (0, nchp - nchk), (0, 0)))
            wim3 = jnp.pad(wim.reshape(b, nchk, ngp), ((0, 0), (0, nchp - nchk), (0, 0)))
            xere, xeim = _s5_scan_call(wre3, wim3, wp["lre"], wp["lim"], x0re, x0im, tc)
            outs["s5re"] = xere[:, nchk - 1].reshape(b, S5_GROUPS, S5_STATE)
            outs["s5im"] = xeim[:, nchk - 1].reshape(b, S5_GROUPS, S5_STATE)
            xsre = jnp.concatenate([x0re, xere[:, :nchk - 1]], axis=1).reshape(r, ngp)
            xsim = jnp.concatenate([x0im, xeim[:, :nchk - 1]], axis=1).reshape(r, ngp)
            yc = _s5_y_call(uc, xsre, xsim, wp["tm"], wp["cre"], wp["cim"], tr)
            ys = yc.reshape(b, nchk, S5_GROUPS, S5_CHUNK, S5_GROUP).transpose(0, 1, 3, 2, 4).reshape(bt, st, S5_WIDTH)
            xt = _ab_out_call(xt, ys, u, attn, wp, g1, ts)
        else:
            wp = wc[l]
            qkv, z, ab = _c_in_call(xt, gain, sc1, sh1, wp["w_in"], ts)
            qkv_b = qkv.reshape(b, s, GDN_QKV)
            outs["conv"] = qkv_b[:, s - (CONV_W - 1):].astype(F32)
            if conv_c is None:
                st8 = jnp.zeros((b, 8, GDN_QKV), BF16)
                s0 = jnp.zeros((b, GDN_HEADS, GDN_DK, GDN_DV), F32)
            else:
                st8 = jnp.pad(conv_c[l], ((0, 0), (8 - (CONV_W - 1), 0), (0, 0))).astype(BF16)
                s0 = gdn_c[l]
            lg = min(s, CHUNK)
            wm, um, qg, kgt, pm, ee = _gdn_prep_call(qkv_b, st8, wp["conv_w"], ab.reshape(b, s, LANES),
                                                     wp["alog"], wp["dtb"], lg)
            o, sfin = _gdn_seq_call(wm, um, qg, kgt, pm, ee, s0, lg)
            outs["gdn"] = sfin
            xt = _c_out_call(xt, o.reshape(bt, st, GDN_VW), z, wp["gn"], wp["w_out"], g1, ts)
        final = layer == depth - 1
        xt = _moe(xt, p["norm_ffn"][layer].reshape(1, d), sc2, sh2, g2, wmoe[layer],
                  p["norm_final"].reshape(1, d), ts, final)
    y = xt.reshape(b, s, d)
    return (y, outs["lat"][None], outs["krope"][None], outs["s5re"][None], outs["s5im"][None],
            outs["conv"][None], outs["gdn"][None])


def kernel(x_prompt, x_sample, c_prompt, c_sample, cache_mla_latent, cache_mla_krope, state_s5_re, state_s5_im,
           state_conv, state_gdn, w_mod, b_mod, norm_mix, norm_ffn, norm_final, w_in_ab, s5_a_re, s5_a_im,
           s5_b_re, s5_b_im, s5_c_re, s5_c_im, s5_d, s5_log_dt, s5_glu_w, s5_glu_b, mla_q_norm, mla_q_up,
           mla_kv_norm, mla_kv_up, w_out_ab, w_in_c, conv_w, gdn_a_log, gdn_dt_bias, gdn_norm, w_out_c,
           moe_w_group, moe_b_group, moe_w_expert, moe_b_expert, moe_w1, moe_w3, moe_w2):
    p = dict(w_mod=w_mod, b_mod=b_mod, norm_mix=norm_mix, norm_ffn=norm_ffn, norm_final=norm_final,
             w_in_ab=w_in_ab, s5_a_re=s5_a_re, s5_a_im=s5_a_im, s5_b_re=s5_b_re, s5_b_im=s5_b_im,
             s5_c_re=s5_c_re, s5_c_im=s5_c_im, s5_d=s5_d, s5_log_dt=s5_log_dt, s5_glu_w=s5_glu_w,
             s5_glu_b=s5_glu_b, mla_q_norm=mla_q_norm, mla_q_up=mla_q_up, mla_kv_norm=mla_kv_norm,
             mla_kv_up=mla_kv_up, w_out_ab=w_out_ab, w_in_c=w_in_c, conv_w=conv_w, gdn_a_log=gdn_a_log,
             gdn_dt_bias=gdn_dt_bias, gdn_norm=gdn_norm, w_out_c=w_out_c, moe_w_group=moe_w_group,
             moe_b_group=moe_b_group, moe_w_expert=moe_w_expert, moe_b_expert=moe_b_expert,
             moe_w1=moe_w1, moe_w3=moe_w3, moe_w2=moe_w2)
    depth = norm_mix.shape[0]
    bp, bs = c_prompt.shape[0], c_sample.shape[0]
    nb = (bp + bs + 7) // 8 * 8
    c_all = jnp.pad(jnp.concatenate([c_prompt, c_sample], axis=0), ((0, nb - bp - bs), (0, 0)))
    mods = _mod_call(c_all, w_mod, b_mod)
    wab = [_prep_ab(l, p) for l in range((depth + 1) // 2)]
    wc = [_prep_c(l, p) for l in range(depth // 2)]
    wmoe = [_prep_moe(layer, p) for layer in range(depth)]
    none6 = (None,) * 6
    outp = _trunk(x_prompt, mods[:, :bp], none6, wab, wc, wmoe, p, flat=False)
    caches = (cache_mla_latent, cache_mla_krope, state_s5_re, state_s5_im, state_conv, state_gdn)
    outs = _trunk(x_sample, mods[:, bp:bp + bs], caches, wab, wc, wmoe, p, flat=True)
    return (outp[0], outs[0]) + tuple(outp[1:]) + tuple(outs[1:])
```

```python
import functools
import math

import jax
import jax.numpy as jnp
from jax import lax
from jax.experimental import pallas as pl
from jax.experimental.pallas import tpu as pltpu

F32 = jnp.float32
BF16 = jnp.bfloat16
HI = lax.Precision.HIGHEST
EPS = 1e-6

D_MODEL = 1024
CHUNK = 64
S5_WIDTH = 512
S5_GROUP = 16
S5_GROUPS = 32
S5_STATE = 64
S5_CHUNK = 16
S5_PAIR = 2 * S5_CHUNK * S5_GROUP
MLA_HEADS = 8
MLA_NOPE = 64
MLA_ROPE = 32
MLA_V = 64
MLA_Q_RANK = 384
MLA_KV_RANK = 256
ROPE_THETA = 10000.0
GDN_HEADS = 8
GDN_DK = 128
GDN_DV = 128
CONV_W = 4
GDN_QKV = GDN_HEADS * (2 * GDN_DK + GDN_DV)
GDN_VW = GDN_HEADS * GDN_DV
N_GROUPS = 4
EXPERTS_PER_GROUP = 8
N_EXPERTS = 32
D_EXPERT = 512
TOP_K = 2
MOE_BLOCK = 128
LANES = 128
VMEM_LIMIT = 48 * 1024 * 1024


def _cparams(*sem):
    return pltpu.CompilerParams(dimension_semantics=sem, vmem_limit_bytes=VMEM_LIMIT)


def _dot(a, b):
    return jnp.dot(a, b, preferred_element_type=F32)


def _dot_nt(a, b, precision=None):
    return lax.dot_general(a, b, (((1,), (1,)), ((), ())), preferred_element_type=F32, precision=precision)


def _full(arr):
    nd = arr.ndim
    return pl.BlockSpec(arr.shape, lambda *_: (0,) * nd)


def _tok(width, ts):
    return pl.BlockSpec((1, ts, width), lambda b, j: (b, j, 0))


def _modspec(arr, ts):
    if arr.shape[1] == 1:
        return pl.BlockSpec((1, 1, arr.shape[2]), lambda b, j: (b, 0, 0))
    return pl.BlockSpec((1, ts, arr.shape[2]), lambda b, j: (b, j, 0))


def _normmod(x, gain, sc, sh):
    ms = jnp.mean(x * x, axis=-1, keepdims=True)
    return x * lax.rsqrt(ms + EPS) * gain * (1.0 + sc) + sh


def _rms(x, gain):
    return x * lax.rsqrt(jnp.mean(x * x, axis=-1, keepdims=True) + EPS) * gain


def _mod_kernel(c_ref, w_ref, b_ref, o_ref):
    c = c_ref[...]
    a = (c * jax.nn.sigmoid(c)).astype(BF16)
    o_ref[0] = _dot(a, w_ref[0]) + b_ref[0]


def _mod_call(c_all, w_mod, b_mod):
    depth, d, n6 = w_mod.shape
    bp = c_all.shape[0]
    tn = 1536
    return pl.pallas_call(
        _mod_kernel,
        grid=(depth, n6 // tn),
        in_specs=[pl.BlockSpec((bp, d), lambda l, n: (0, 0)),
                  pl.BlockSpec((1, d, tn), lambda l, n: (l, 0, n)),
                  pl.BlockSpec((1, 1, tn), lambda l, n: (l, 0, n))],
        out_specs=pl.BlockSpec((1, bp, tn), lambda l, n: (l, 0, n)),
        out_shape=jax.ShapeDtypeStruct((depth, bp, n6), F32),
        compiler_params=_cparams("parallel", "parallel"),
        name="mod",
    )(c_all, w_mod.astype(BF16), b_mod.reshape(depth, 1, n6))


def _ab_in_kernel(x_ref, gain_ref, sc_ref, sh_ref, w_ref, qg_ref, wqa_ref, wqb_ref, kvg_ref,
                  cq_ref, sq_ref, ck_ref, sk_ref, u_ref, q_ref, lat_ref, kr_ref):
    h = _normmod(x_ref[0], gain_ref[...], sc_ref[0], sh_ref[0]).astype(BF16)
    proj = _dot(h, w_ref[...])
    u_ref[0] = proj[:, :S5_WIDTH]
    o1 = S5_WIDTH + MLA_Q_RANK
    o2 = o1 + MLA_KV_RANK
    qn = _rms(proj[:, S5_WIDTH:o1], qg_ref[...]).astype(BF16)
    qa = _dot(qn, wqa_ref[...])
    qb = _dot(qn, wqb_ref[...])
    cq = cq_ref[...]
    sq = sq_ref[...]
    for hh in range(MLA_HEADS):
        sl = slice(LANES * hh, LANES * (hh + 1))
        q_ref[0, hh] = (qa[:, sl] * cq + qb[:, sl] * sq).astype(BF16)
    lat_ref[0] = _rms(proj[:, o1:o2], kvg_ref[...])
    kr_ref[0] = proj[:, o2:o2 + LANES] * ck_ref[...] + proj[:, o2 + LANES:o2 + 2 * LANES] * sk_ref[...]


def _ab_in_call(x, gain, sc, sh, wp, tabs, ts):
    b, s, d = x.shape
    cq, sq, ck, sk = tabs
    tab = pl.BlockSpec((ts, LANES), lambda bb, j: (j, 0))
    ins = [x, gain, sc, sh, wp["w_in"], wp["q_norm"], wp["wqa"], wp["wqb"], wp["kv_norm"], cq, sq, ck, sk]
    specs = [_tok(d, ts), _full(gain), _modspec(sc, ts), _modspec(sh, ts), _full(wp["w_in"]), _full(wp["q_norm"]),
             _full(wp["wqa"]), _full(wp["wqb"]), _full(wp["kv_norm"]), tab, tab, tab, tab]
    return pl.pallas_call(
        _ab_in_kernel,
        grid=(b, s // ts),
        in_specs=specs,
        out_specs=[_tok(S5_WIDTH, ts),
                   pl.BlockSpec((1, MLA_HEADS, ts, LANES), lambda bb, j: (bb, 0, j, 0)),
                   _tok(MLA_KV_RANK, ts), _tok(LANES, ts)],
        out_shape=[jax.ShapeDtypeStruct((b, s, S5_WIDTH), F32),
                   jax.ShapeDtypeStruct((b, MLA_HEADS, s, LANES), BF16),
                   jax.ShapeDtypeStruct((b, s, MLA_KV_RANK), F32),
                   jax.ShapeDtypeStruct((b, s, LANES), F32)],
        compiler_params=_cparams("parallel", "parallel"),
        name="ab_in",
    )(*ins)


def _kv_kernel(lat_ref, kr_ref, wk_ref, wv_ref, k_ref, v_ref):
    lat = lat_ref[0].astype(BF16)
    kk = _dot(lat, wk_ref[...])
    vv = _dot(lat, wv_ref[...])
    kr = kr_ref[0]
    ones = (lax.broadcasted_iota(jnp.int32, (1, LANES), 1) >= MLA_V).astype(F32)
    for hh in range(MLA_HEADS):
        k_ref[0, hh] = (kk[:, LANES * hh:LANES * (hh + 1)] + kr).astype(BF16)
        v_ref[0, hh] = (vv[:, LANES * hh:LANES * (hh + 1)] + ones).astype(BF16)


def _kv_call(lat_all, kr_all, wp, ts):
    b, sk, _ = lat_all.shape
    hspec = pl.BlockSpec((1, MLA_HEADS, ts, LANES), lambda bb, j: (bb, 0, j, 0))
    hshape = jax.ShapeDtypeStruct((b, MLA_HEADS, sk, LANES), BF16)
    return pl.pallas_call(
        _kv_kernel,
        grid=(b, sk // ts),
        in_specs=[_tok(MLA_KV_RANK, ts), _tok(LANES, ts), _full(wp["wk"]), _full(wp["wv"])],
        out_specs=[hspec, hspec],
        out_shape=[hshape, hshape],
        compiler_params=_cparams("parallel", "parallel"),
        name="kv_up",
    )(lat_all, kr_all, wp["wk"], wp["wv"])


def _attn_kernel(q_ref, k_ref, v_ref, o_ref, *, tq, tk, q_off, sk_valid):
    i = pl.program_id(2)
    qs = [q_ref[0, 0], q_ref[0, 1]]
    q0 = q_off + i * tq
    lim_full = (q0 // CHUNK + 1) * CHUNK
    lim_tot = jnp.minimum(((q0 + tq - 1) // CHUNK + 1) * CHUNK, sk_valid)
    n_full = jnp.minimum(lim_full // tk, sk_valid // tk)
    n_tot = (lim_tot + tk - 1) // tk

    def step(j, carry, masked):
        off = pl.multiple_of(j * tk, tk)
        if masked:
            qpos = q0 + lax.broadcasted_iota(jnp.int32, (tq, tk), 0)
            kpos = off + lax.broadcasted_iota(jnp.int32, (tq, tk), 1)
            ok = jnp.logical_and(kpos // CHUNK <= qpos // CHUNK, kpos < sk_valid)
        s = [_dot_nt(qs[hh], k_ref[0, hh, pl.ds(off, tk), :]) for hh in range(2)]
        if masked:
            s = [jnp.where(ok, sh, -1e30) for sh in s]
        m_new = [jnp.maximum(carry[hh][0], jnp.max(s[hh], axis=-1, keepdims=True)) for hh in range(2)]
        alpha = [jnp.exp2(carry[hh][0] - m_new[hh]) for hh in range(2)]
        p = [jnp.exp2(s[hh] - m_new[hh]).astype(BF16) for hh in range(2)]
        acc = [alpha[hh] * carry[hh][1] + _dot(p[hh], v_ref[0, hh, pl.ds(off, tk), :]) for hh in range(2)]
        return tuple((m_new[hh], acc[hh]) for hh in range(2))

    one = (jnp.full((tq, 1), -1e30, F32), jnp.zeros((tq, LANES), F32))
    carry = lax.fori_loop(0, n_full, functools.partial(step, masked=False), (one, one))
    (_, a0), (_, a1) = lax.fori_loop(n_full, n_tot, functools.partial(step, masked=True), carry)
    lane = lax.broadcasted_iota(jnp.int32, (tq, LANES), 1)
    r0 = a0 / pltpu.roll(a0, MLA_V, axis=1)
    r1 = a1 / pltpu.roll(a1, MLA_V, axis=1)
    o_ref[0] = jnp.where(lane < MLA_V, r0, pltpu.roll(r1, MLA_V, axis=1)).astype(o_ref.dtype)


def _attn_call(q, k, v, tq, tk, q_off, sk_valid):
    b, h, sq, _ = q.shape
    sk = k.shape[2]
    kern = functools.partial(_attn_kernel, tq=tq, tk=tk, q_off=q_off, sk_valid=sk_valid)
    return pl.pallas_call(
        kern,
        grid=(b, h // 2, sq // tq),
        in_specs=[pl.BlockSpec((1, 2, tq, LANES), lambda bb, hp, i: (bb, hp, i, 0)),
                  pl.BlockSpec((1, 2, sk, LANES), lambda bb, hp, i: (bb, hp, 0, 0), pipeline_mode=pl.Buffered(1)),
                  pl.BlockSpec((1, 2, sk, LANES), lambda bb, hp, i: (bb, hp, 0, 0), pipeline_mode=pl.Buffered(1))],
        out_specs=pl.BlockSpec((1, tq, LANES), lambda bb, hp, i: (bb, i, hp)),
        out_shape=jax.ShapeDtypeStruct((b, sq, h * MLA_V), BF16),
        compiler_params=_cparams("parallel", "parallel", "arbitrary"),
        name="mla_attn",
    )(q, k, v)


def _s5_w_kernel(u_ref, bre_ref, bim_ref, wre_ref, wim_ref):
    u = u_ref[...]
    wre_ref[...] = _dot(u, bre_ref[0])
    wim_ref[...] = _dot(u, bim_ref[0])


def _s5_w_call(uc, bre, bim, tr):
    r = uc.shape[0]
    npair = bre.shape[0]
    out = jax.ShapeDtypeStruct((r, npair * LANES), F32)
    ospec = pl.BlockSpec((tr, LANES), lambda g, i: (i, g))
    wspec = pl.BlockSpec((1, S5_PAIR, LANES), lambda g, i: (g, 0, 0))
    return pl.pallas_call(
        _s5_w_kernel,
        grid=(npair, r // tr),
        in_specs=[pl.BlockSpec((tr, S5_PAIR), lambda g, i: (i, g)), wspec, wspec],
        out_specs=[ospec, ospec],
        out_shape=[out, out],
        compiler_params=_cparams("parallel", "parallel"),
        name="s5_chunk_in",
    )(uc, bre, bim)


def _s5_scan_kernel(wre_ref, wim_ref, lre_ref, lim_ref, x0re_ref, x0im_ref, ore_ref, oim_ref, sre, sim, *, tc):
    @pl.when(pl.program_id(1) == 0)
    def _():
        sre[...] = x0re_ref[0]
        sim[...] = x0im_ref[0]

    lr = lre_ref[...]
    li = lim_ref[...]

    def blk(t, carry):
        xr, xi = carry
        base = pl.multiple_of(t * 8, 8)
        wr = wre_ref[0, pl.ds(base, 8), :]
        wi = wim_ref[0, pl.ds(base, 8), :]
        rows_r, rows_i = [], []
        for r in range(8):
            nr = lr * xr - li * xi + wr[r:r + 1]
            ni = lr * xi + li * xr + wi[r:r + 1]
            xr, xi = nr, ni
            rows_r.append(xr)
            rows_i.append(xi)
        ore_ref[0, pl.ds(base, 8), :] = jnp.concatenate(rows_r, axis=0)
        oim_ref[0, pl.ds(base, 8), :] = jnp.concatenate(rows_i, axis=0)
        return xr, xi

    xr, xi = lax.fori_loop(0, tc // 8, blk, (sre[...], sim[...]))
    sre[...] = xr
    sim[...] = xi


def _s5_scan_call(wre, wim, lre, lim, x0re, x0im, tc):
    b, nch, n = wre.shape
    wspec = pl.BlockSpec((1, tc, n), lambda bb, c: (bb, c, 0))
    lspec = pl.BlockSpec((1, n), lambda bb, c: (0, 0))
    xspec = pl.BlockSpec((1, 1, n), lambda bb, c: (bb, 0, 0))
    out = jax.ShapeDtypeStruct((b, nch, n), F32)
    return pl.pallas_call(
        functools.partial(_s5_scan_kernel, tc=tc),
        grid=(b, nch // tc),
        in_specs=[wspec, wspec, lspec, lspec, xspec, xspec],
        out_specs=[wspec, wspec],
        out_shape=[out, out],
        scratch_shapes=[pltpu.VMEM((1, n), F32), pltpu.VMEM((1, n), F32)],
        compiler_params=_cparams("parallel", "arbitrary"),
        name="s5_scan",
    )(wre, wim, lre, lim, x0re, x0im)


def _s5_y_kernel(u_ref, xre_ref, xim_ref, t_ref, cre_ref, cim_ref, y_ref):
    y = _dot(u_ref[...], t_ref[0])
    y += _dot(xre_ref[...].astype(BF16), cre_ref[0])
    y += _dot(xim_ref[...].astype(BF16), cim_ref[0])
    y_ref[...] = y


def _s5_y_call(uc, xre, xim, tm, cre, cim, tr):
    r = uc.shape[0]
    npair = tm.shape[0]
    uspec = pl.BlockSpec((tr, S5_PAIR), lambda g, i: (i, g))
    xspec = pl.BlockSpec((tr, LANES), lambda g, i: (i, g))
    return pl.pallas_call(
        _s5_y_kernel,
        grid=(npair, r // tr),
        in_specs=[uspec, xspec, xspec,
                  pl.BlockSpec((1, S5_PAIR, S5_PAIR), lambda g, i: (g, 0, 0)),
                  pl.BlockSpec((1, LANES, S5_PAIR), lambda g, i: (g, 0, 0)),
                  pl.BlockSpec((1, LANES, S5_PAIR), lambda g, i: (g, 0, 0))],
        out_specs=uspec,
        out_shape=jax.ShapeDtypeStruct((r, npair * S5_PAIR), F32),
        compiler_params=_cparams("parallel", "parallel"),
        name="s5_chunk_out",
    )(uc, xre, xim, tm, cre, cim)


def _s5_mats(a_re, a_im, b_re, b_im, c_re, c_im, log_dt):
    g, p = a_re.shape
    L = S5_CHUNK
    lam = lax.complex(jnp.minimum(a_re, -1e-4), a_im)
    lamdt = lam * jnp.exp(log_dt)[:, None]
    lam_bar = jnp.exp(lamdt)
    b_bar = ((lam_bar - 1.0) / lam)[:, :, None] * lax.complex(b_re, b_im)
    cm = lax.complex(c_re, c_im)
    pw = jnp.exp(lamdt[:, :, None] * jnp.arange(L + 1, dtype=F32))
    mm = pw[:, :, :L, None] * b_bar[:, :, None, :]
    kd = jnp.sum((cm[:, :, :, None, None] * mm[:, None]).real, axis=2).transpose(0, 2, 1, 3)
    s_i = jnp.arange(L)[:, None]
    t_i = jnp.arange(L)[None, :]
    dd = t_i - s_i
    tm = jnp.where((dd >= 0)[None, :, :, None, None], kd[:, jnp.clip(dd, 0, L - 1)], 0.0)
    tm = tm.transpose(0, 1, 4, 2, 3).reshape(g, L * S5_GROUP, L * S5_GROUP)
    bp = pw[:, :, L - 1 - jnp.arange(L)][:, :, :, None] * b_bar[:, :, None, :]
    bp = bp.transpose(0, 2, 3, 1).reshape(g, L * S5_GROUP, p)
    cp = cm[:, :, :, None] * pw[:, None, :, 1:]
    cp = cp.transpose(0, 2, 3, 1).reshape(g, p, L * S5_GROUP)

    def pair_diag(m):
        gg, r, c = m.shape
        m2 = m.reshape(gg // 2, 2, r, c)
        z = jnp.zeros_like(m2[:, 0])
        top = jnp.concatenate([m2[:, 0], z], axis=2)
        bot = jnp.concatenate([z, m2[:, 1]], axis=2)
        return jnp.concatenate([top, bot], axis=1)

    return dict(
        tm=pair_diag(tm).astype(BF16),
        bre=pair_diag(bp.real).astype(BF16), bim=pair_diag(bp.imag).astype(BF16),
        cre=pair_diag(cp.real).astype(BF16), cim=pair_diag(-cp.imag).astype(BF16),
        lre=pw[:, :, L].real.reshape(1, g * p), lim=pw[:, :, L].imag.reshape(1, g * p))


def _ab_out_kernel(x_ref, ys_ref, u_ref, at_ref, d_ref, gw_ref, gb_ref, ws_ref, wa_ref, g1_ref, o_ref):
    y = ys_ref[0] + d_ref[...] * u_ref[0]
    y = jax.nn.gelu(y)
    gate = jax.nn.sigmoid(_dot(y.astype(BF16), gw_ref[...]) + gb_ref[...])
    out = _dot((y * gate).astype(BF16), ws_ref[...]) + _dot(at_ref[0], wa_ref[...])
    o_ref[0] = x_ref[0] + g1_ref[0] * out


def _ab_out_call(x, ys, u, attn, wp, g1, ts):
    b, s, d = x.shape
    ins = [x, ys, u, attn, wp["s5_d"], wp["glu_w"], wp["glu_b"], wp["w_out_s5"], wp["w_out_at"], g1]
    specs = [_tok(d, ts), _tok(S5_WIDTH, ts), _tok(S5_WIDTH, ts), _tok(MLA_HEADS * MLA_V, ts),
             _full(wp["s5_d"]), _full(wp["glu_w"]), _full(wp["glu_b"]), _full(wp["w_out_s5"]),
             _full(wp["w_out_at"]), _modspec(g1, ts)]
    return pl.pallas_call(
        _ab_out_kernel,
        grid=(b, s // ts),
        in_specs=specs,
        out_specs=_tok(d, ts),
        out_shape=jax.ShapeDtypeStruct((b, s, d), F32),
        compiler_params=_cparams("parallel", "parallel"),
        name="ab_out",
    )(*ins)


def _c_in_kernel(x_ref, gain_ref, sc_ref, sh_ref, w_ref, qkv_ref, z_ref, ab_ref):
    h = _normmod(x_ref[0], gain_ref[...], sc_ref[0], sh_ref[0]).astype(BF16)
    proj = _dot(h, w_ref[...])
    qkv_ref[0] = proj[:, :GDN_QKV].astype(BF16)
    z_ref[0] = proj[:, GDN_QKV:GDN_QKV + GDN_VW].astype(BF16)
    ab_ref[0] = proj[:, GDN_QKV + GDN_VW:]


def _c_in_call(x, gain, sc, sh, w, ts):
    b, s, d = x.shape
    return pl.pallas_call(
        _c_in_kernel,
        grid=(b, s // ts),
        in_specs=[_tok(d, ts), _full(gain), _modspec(sc, ts), _modspec(sh, ts), _full(w)],
        out_specs=[_tok(GDN_QKV, ts), _tok(GDN_VW, ts), _tok(LANES, ts)],
        out_shape=[jax.ShapeDtypeStruct((b, s, GDN_QKV), BF16),
                   jax.ShapeDtypeStruct((b, s, GDN_VW), BF16),
                   jax.ShapeDtypeStruct((b, s, LANES), F32)],
        compiler_params=_cparams("parallel", "parallel"),
        name="c_in",
    )(x, gain, sc, sh, w)


def _gdn_prep_kernel(qkv_ref, halo_ref, st_ref, cw_ref, ab_ref, alog_ref, dtb_ref,
                     wq_ref, u_ref, kgt_ref, pm_ref, ee_ref, *, L):
    j = pl.program_id(1)
    prev = jnp.where(j == 0, st_ref[0].astype(F32), halo_ref[0].astype(F32))
    xx = jnp.concatenate([prev, qkv_ref[0].astype(F32)], axis=0)
    cw = cw_ref[...]
    y = xx[8:8 + L] * cw[CONV_W - 1:CONV_W]
    for t in range(1, CONV_W):
        y += xx[8 - t:8 - t + L] * cw[CONV_W - 1 - t:CONV_W - t]
    y = y * jax.nn.sigmoid(y)

    ab = ab_ref[0]
    g_all = -jnp.exp(alog_ref[...]) * jax.nn.softplus(ab + dtb_ref[...])
    beta_all = jax.nn.sigmoid(ab)
    row = lax.broadcasted_iota(jnp.int32, (L, L), 0)
    col = lax.broadcasted_iota(jnp.int32, (L, L), 1)
    incl = row >= col
    strict = row > col
    gam_all = jnp.dot(incl.astype(F32), g_all, preferred_element_type=F32, precision=HI)
    eye_l = (lax.broadcasted_iota(jnp.int32, (LANES, LANES), 0)
             == lax.broadcasted_iota(jnp.int32, (LANES, LANES), 1))
    gam_t = _dot_nt(eye_l.astype(F32), gam_all, precision=HI)
    ee_ref[0, 0] = jnp.exp(gam_all[L - 1:L])
    eye64 = (row == col).astype(F32)
    eye_bf = eye_l.astype(BF16)

    hs = range(GDN_HEADS)
    koff, voff = GDN_HEADS * GDN_DK, 2 * GDN_HEADS * GDN_DK
    qh = [y[:, GDN_DK * h:GDN_DK * (h + 1)] for h in hs]
    kh = [y[:, koff + GDN_DK * h:koff + GDN_DK * (h + 1)] for h in hs]
    vh = [y[:, voff + GDN_DV * h:voff + GDN_DV * (h + 1)] for h in hs]
    qh = [q * lax.rsqrt(jnp.sum(q * q, axis=-1, keepdims=True) + EPS) * (GDN_DK ** -0.5) for q in qh]
    kh = [k * lax.rsqrt(jnp.sum(k * k, axis=-1, keepdims=True) + EPS) for k in kh]
    gam_c = [gam_all[:, h:h + 1] for h in hs]
    beta_c = [beta_all[:, GDN_HEADS + h:GDN_HEADS + h + 1] for h in hs]
    dec = [jnp.exp(jnp.minimum(gam_c[h] - gam_t[h:h + 1, :], 0.0)) for h in hs]
    kb = [k.astype(BF16) for k in kh]
    kk = [_dot_nt(kb[h], kb[h]) for h in hs]
    qk = [_dot_nt(qh[h].astype(BF16), kb[h]) for h in hs]
    a = [jnp.where(strict, beta_c[h] * kk[h] * dec[h], 0.0) for h in hs]
    for h in hs:
        pm_ref[0, 0, h] = jnp.where(incl, qk[h] * dec[h], 0.0).astype(BF16)
    tinv = [eye64 - a[h] for h in hs]
    pw = [_split(a[h]) for h in hs]
    for _ in range(max(L.bit_length() - 2, 0)):
        pw = [_split(_dot3(pw[h], pw[h])) for h in hs]
        tinv = [tinv[h] + _dot3(_split(tinv[h]), pw[h]) for h in hs]
    eg = [jnp.exp(gam_c[h]) for h in hs]
    rhs = [jnp.concatenate([(beta_c[h] * eg[h]) * kh[h], beta_c[h] * vh[h]], axis=1).astype(BF16) for h in hs]
    wu = [_dot(tinv[h].astype(BF16), rhs[h]) for h in hs]
    for h in hs:
        wq_ref[0, 0, h] = jnp.concatenate([wu[h][:, :GDN_DK], qh[h] * eg[h]], axis=0).astype(BF16)
        u_ref[0, h] = wu[h][:, GDN_DK:].astype(BF16)
    kg = [(kh[h] * jnp.exp(gam_all[L - 1:L, h:h + 1] - gam_c[h])).astype(BF16) for h in hs]
    kgt = [_dot_nt(eye_bf, kg[h]) for h in hs]
    for h in hs:
        kgt_ref[0, 0, h] = kgt[h].astype(BF16)


def _split(a):
    hi = a.astype(BF16)
    return hi, (a - hi.astype(F32)).astype(BF16)


def _dot3(a, b):
    return _dot(a[0], b[0]) + (_dot(a[0], b[1]) + _dot(a[1], b[0]))


def _gdn_prep_call(qkv, st8, conv_w, ab, alog, dtb, L):
    b, s, _ = qkv.shape
    nch = s // L
    return pl.pallas_call(
        functools.partial(_gdn_prep_kernel, L=L),
        grid=(b, nch),
        in_specs=[_tok(GDN_QKV, L),
                  pl.BlockSpec((1, 8, GDN_QKV), lambda bb, j: (bb, jnp.maximum(j * (L // 8) - 1, 0), 0)),
                  pl.BlockSpec((1, 8, GDN_QKV), lambda bb, j: (bb, 0, 0)),
                  _full(conv_w), _tok(LANES, L), _full(alog), _full(dtb)],
        out_specs=[pl.BlockSpec((1, 1, GDN_HEADS, 2 * L, LANES), lambda bb, j: (bb, j, 0, 0, 0)),
                   pl.BlockSpec((1, GDN_HEADS, L, LANES), lambda bb, j: (bb, 0, j, 0)),
                   pl.BlockSpec((1, 1, GDN_HEADS, GDN_DK, L), lambda bb, j: (bb, j, 0, 0, 0)),
                   pl.BlockSpec((1, 1, GDN_HEADS, L, L), lambda bb, j: (bb, j, 0, 0, 0)),
                   pl.BlockSpec((1, 1, 1, LANES), lambda bb, j: (bb, j, 0, 0))],
        out_shape=[jax.ShapeDtypeStruct((b, nch, GDN_HEADS, 2 * L, LANES), BF16),
                   jax.ShapeDtypeStruct((b, GDN_HEADS, s, LANES), BF16),
                   jax.ShapeDtypeStruct((b, nch, GDN_HEADS, GDN_DK, L), BF16),
                   jax.ShapeDtypeStruct((b, nch, GDN_HEADS, L, L), BF16),
                   jax.ShapeDtypeStruct((b, nch, 1, LANES), F32)],
        compiler_params=_cparams("parallel", "parallel"),
        name="gdn_prep",
    )(qkv, qkv, st8, conv_w, ab, alog, dtb)


GDN_BATCH_GROUP = 2


def _gdn_seq_kernel(wq_ref, u_ref, kgt_ref, pm_ref, ee_ref, s0_ref, o_ref, sf_ref, s_scr, *, L):
    c = pl.program_id(1)

    @pl.when(c == 0)
    def _():
        s_scr[...] = s0_ref[...]

    ch = [(bb, hh) for bb in range(GDN_BATCH_GROUP) for hh in range(GDN_HEADS)]
    st = [s_scr[bb, hh] for bb, hh in ch]
    sp = [_split(s) for s in st]
    r = [_dot(wq_ref[bb, 0, hh], sp[i][0]) + _dot(wq_ref[bb, 0, hh], sp[i][1]) for i, (bb, hh) in enumerate(ch)]
    v_new = [(u_ref[bb, hh].astype(F32) - r[i][:L]).astype(BF16) for i, (bb, hh) in enumerate(ch)]
    o = [r[i][L:] + _dot(pm_ref[bb, 0, hh], v_new[i]) for i, (bb, hh) in enumerate(ch)]
    for i, (bb, hh) in enumerate(ch):
        o_ref[bb, :, GDN_DV * hh:GDN_DV * (hh + 1)] = o[i].astype(o_ref.dtype)
    s_new = [ee_ref[bb, 0][:, hh:hh + 1] * st[i] + _dot(kgt_ref[bb, 0, hh], v_new[i]) for i, (bb, hh) in enumerate(ch)]
    for i, (bb, hh) in enumerate(ch):
        s_scr[bb, hh] = s_new[i]

    @pl.when(c == pl.num_programs(1) - 1)
    def _():
        sf_ref[...] = s_scr[...]


def _gdn_seq_call(wq, um, kgt, pm, ee, s0, L):
    b, h, s, _ = um.shape
    nch = s // L
    bg = GDN_BATCH_GROUP
    sspec = pl.BlockSpec((bg, h, GDN_DK, GDN_DV), lambda g, c: (g, 0, 0, 0))
    return pl.pallas_call(
        functools.partial(_gdn_seq_kernel, L=L),
        grid=(b // bg, nch),
        in_specs=[pl.BlockSpec((bg, 1, h, 2 * L, LANES), lambda g, c: (g, c, 0, 0, 0)),
                  pl.BlockSpec((bg, h, L, LANES), lambda g, c: (g, 0, c, 0)),
                  pl.BlockSpec((bg, 1, h, GDN_DK, L), lambda g, c: (g, c, 0, 0, 0)),
                  pl.BlockSpec((bg, 1, h, L, L), lambda g, c: (g, c, 0, 0, 0)),
                  pl.BlockSpec((bg, 1, 1, LANES), lambda g, c: (g, c, 0, 0)),
                  sspec],
        out_specs=[pl.BlockSpec((bg, L, GDN_VW), lambda g, c: (g, c, 0)), sspec],
        out_shape=[jax.ShapeDtypeStruct((b, s, GDN_VW), BF16),
                   jax.ShapeDtypeStruct((b, h, GDN_DK, GDN_DV), F32)],
        scratch_shapes=[pltpu.VMEM((bg, h, GDN_DK, GDN_DV), F32)],
        compiler_params=_cparams("parallel", "arbitrary"),
        name="gdn_seq",
    )(wq, um, kgt, pm, ee, s0)


def _c_out_kernel(x_ref, o_ref, z_ref, gn_ref, w_ref, g1_ref, out_ref):
    o = o_ref[0].astype(F32)
    z = z_ref[0].astype(F32)
    gn = gn_ref[...]
    parts = []
    for hh in range(GDN_HEADS):
        oh = o[:, GDN_DV * hh:GDN_DV * (hh + 1)]
        zh = z[:, GDN_DV * hh:GDN_DV * (hh + 1)]
        parts.append((_rms(oh, gn) * (zh * jax.nn.sigmoid(zh))).astype(BF16))
    out = _dot(jnp.concatenate(parts, axis=1), w_ref[...])
    out_ref[0] = x_ref[0] + g1_ref[0] * out


def _c_out_call(x, o, z, gn, w, g1, ts):
    b, s, d = x.shape
    return pl.pallas_call(
        _c_out_kernel,
        grid=(b, s // ts),
        in_specs=[_tok(d, ts), _tok(GDN_VW, ts), _tok(GDN_VW, ts), _full(gn), _full(w), _modspec(g1, ts)],
        out_specs=_tok(d, ts),
        out_shape=jax.ShapeDtypeStruct((b, s, d), F32),
        compiler_params=_cparams("parallel", "parallel"),
        name="c_out",
    )(x, o, z, gn, w, g1)


def _router_kernel(x_ref, gain_ref, sc_ref, sh_ref, wr_ref, br_ref, h_ref, re_ref, rg_ref):
    h = _normmod(x_ref[0], gain_ref[...], sc_ref[0], sh_ref[0])
    h_ref[0] = h
    logits = jnp.dot(h, wr_ref[...], preferred_element_type=F32, precision=HI) + br_ref[...]
    lane = lax.broadcasted_iota(jnp.int32, logits.shape, 1).astype(F32)
    neg = jnp.float32(-jnp.inf)
    big = jnp.float32(1e9)
    gl = jnp.where(lane < N_GROUPS, logits, neg)
    gm = jnp.max(gl, axis=-1, keepdims=True)
    g_val = 1.0 / jnp.sum(jnp.exp(gl - gm), axis=-1, keepdims=True)
    g_idx = jnp.min(jnp.where(gl == gm, lane, big), axis=-1, keepdims=True)
    lo = N_GROUPS + EXPERTS_PER_GROUP * g_idx
    el = jnp.where(jnp.logical_and(lane >= lo, lane < lo + EXPERTS_PER_GROUP), logits, neg)
    em = jnp.max(el, axis=-1, keepdims=True)
    i1 = jnp.min(jnp.where(el == em, lane, big), axis=-1, keepdims=True)
    el2 = jnp.where(lane == i1, neg, el)
    em2 = jnp.max(el2, axis=-1, keepdims=True)
    i2 = jnp.min(jnp.where(el2 == em2, lane, big), axis=-1, keepdims=True)
    es = jnp.sum(jnp.exp(el - em), axis=-1, keepdims=True)
    p1 = 1.0 / es
    p2 = jnp.exp(em2 - em) / es
    den = p1 + p2
    re_ref[0] = jnp.where(lane == 0, i1 - N_GROUPS, jnp.where(lane == 1, i2 - N_GROUPS, 0.0)).astype(jnp.int32)
    rg_ref[0] = jnp.where(lane == 0, g_val * p1 / den, jnp.where(lane == 1, g_val * p2 / den, 0.0))


def _router_call(x, gain, sc, sh, wr, br, ts):
    b, s, d = x.shape
    return pl.pallas_call(
        _router_kernel,
        grid=(b, s // ts),
        in_specs=[_tok(d, ts), _full(gain), _modspec(sc, ts), _modspec(sh, ts), _full(wr), _full(br)],
        out_specs=[_tok(d, ts), _tok(LANES, ts), _tok(LANES, ts)],
        out_shape=[jax.ShapeDtypeStruct((b, s, d), F32),
                   jax.ShapeDtypeStruct((b, s, LANES), jnp.int32),
                   jax.ShapeDtypeStruct((b, s, LANES), F32)],
        compiler_params=_cparams("parallel", "parallel"),
        name="moe_router",
    )(x, gain, sc, sh, wr, br)


def _row_copy(src_hbm, dst_vmem, src_row, dst_row, sem):
    return pltpu.make_async_copy(src_hbm.at[pl.ds(src_row, 1)], dst_vmem.at[pl.ds(dst_row, 1)], sem)


def _ffn_kernel(be_ref, nu_ref, tok_ref, h_hbm, w1_ref, w3_ref, w2_ref, y_ref, xb, sem):
    i = pl.program_id(0)

    @pl.when(i < nu_ref[0])
    def _():
        def issue(r, c):
            _row_copy(h_hbm, xb, tok_ref[0, 0, r], r, sem).start()
            return c
        lax.fori_loop(0, MOE_BLOCK, issue, 0)

        def drain(r, c):
            _row_copy(h_hbm, xb, 0, r, sem).wait()
            return c
        lax.fori_loop(0, MOE_BLOCK, drain, 0)
        x = xb[...].astype(BF16)
        a = _dot(x, w1_ref[0])
        g = _dot(x, w3_ref[0])
        mid = (a * jax.nn.sigmoid(a) * g).astype(BF16)
        y_ref[...] = _dot(mid, w2_ref[0])

    @pl.when(i >= nu_ref[0])
    def _():
        y_ref[...] = jnp.zeros_like(y_ref)


def _ffn_call(block_e, n_used, buf_tok, h2, w1, w3, w2):
    n_blocks = block_e.shape[0]
    d = h2.shape[1]
    grid_spec = pltpu.PrefetchScalarGridSpec(
        num_scalar_prefetch=2,
        grid=(n_blocks,),
        in_specs=[pl.BlockSpec((1, 1, MOE_BLOCK), lambda i, be, nu: (i, 0, 0), memory_space=pltpu.SMEM),
                  pl.BlockSpec(memory_space=pl.ANY),
                  pl.BlockSpec((1, d, D_EXPERT), lambda i, be, nu: (be[i], 0, 0)),
                  pl.BlockSpec((1, d, D_EXPERT), lambda i, be, nu: (be[i], 0, 0)),
                  pl.BlockSpec((1, D_EXPERT, d), lambda i, be, nu: (be[i], 0, 0))],
        out_specs=pl.BlockSpec((MOE_BLOCK, d), lambda i, be, nu: (i, 0)),
        scratch_shapes=[pltpu.VMEM((MOE_BLOCK, d), F32), pltpu.SemaphoreType.DMA(())],
    )
    return pl.pallas_call(
        _ffn_kernel,
        grid_spec=grid_spec,
        out_shape=jax.ShapeDtypeStruct((n_blocks * MOE_BLOCK, d), F32),
        compiler_params=_cparams("arbitrary"),
        name="moe_ffn",
    )(block_e, n_used, buf_tok.reshape(n_blocks, 1, MOE_BLOCK), h2, w1, w3, w2)


def _combine_kernel(dest_ref, yb_hbm, x_ref, rg_ref, g2_ref, fin_ref, o_ref, b0, b1, sem, *, tb, final):
    def issue(r, c):
        _row_copy(yb_hbm, b0, dest_ref[0, 0, 2 * r], r, sem).start()
        _row_copy(yb_hbm, b1, dest_ref[0, 0, 2 * r + 1], r, sem).start()
        return c
    lax.fori_loop(0, tb, issue, 0)

    def drain(r, c):
        _row_copy(yb_hbm, b0, 0, r, sem).wait()
        _row_copy(yb_hbm, b1, 0, r, sem).wait()
        return c
    lax.fori_loop(0, tb, drain, 0)
    rg = rg_ref[0]
    y = rg[:, 0:1] * b0[...] + rg[:, 1:2] * b1[...]
    xn = x_ref[0] + g2_ref[0] * y
    if final:
        xn = _rms(xn, fin_ref[...])
    o_ref[0] = xn


def _combine_call(dest, yb, x, rg, g2, fin, tb, final):
    b, s, d = x.shape
    nb = s // tb
    return pl.pallas_call(
        functools.partial(_combine_kernel, tb=tb, final=final),
        grid=(b, nb),
        in_specs=[pl.BlockSpec((1, 1, 2 * tb), lambda bb, j: (bb * nb + j, 0, 0), memory_space=pltpu.SMEM),
                  pl.BlockSpec(memory_space=pl.ANY),
                  _tok(d, tb), _tok(LANES, tb), _modspec(g2, tb), _full(fin)],
        out_specs=_tok(d, tb),
        out_shape=jax.ShapeDtypeStruct((b, s, d), F32),
        scratch_shapes=[pltpu.VMEM((tb, d), F32), pltpu.VMEM((tb, d), F32), pltpu.SemaphoreType.DMA(())],
        compiler_params=_cparams("arbitrary", "arbitrary"),
        name="moe_combine",
    )(dest.reshape(b * nb, 1, 2 * tb), yb, x, rg, g2, fin)


def _moe(x, gain, sc, sh, g2, mp, fin, ts, final):
    b, s, d = x.shape
    n = b * s
    h2, r_e, r_g = _router_call(x, gain, sc, sh, mp["wr"], mp["br"], ts)
    eid = r_e[:, :, :TOP_K].reshape(n * TOP_K)
    onehot = (eid[:, None] == jnp.arange(N_EXPERTS, dtype=jnp.int32)[None, :]).astype(jnp.int32)
    csum = jnp.cumsum(onehot, axis=0)
    counts = csum[-1]
    rank = jnp.sum(onehot * (csum - 1), axis=1)
    padded = (counts + MOE_BLOCK - 1) // MOE_BLOCK * MOE_BLOCK
    pad_end = jnp.cumsum(padded)
    pad_start = pad_end - padded
    dest = (jnp.sum(onehot * pad_start[None, :], axis=1) + rank).astype(jnp.int32)
    n_blocks = (n * TOP_K + N_EXPERTS * (MOE_BLOCK - 1) + MOE_BLOCK - 1) // MOE_BLOCK
    block_e = jnp.minimum(jnp.searchsorted(pad_end, jnp.arange(n_blocks, dtype=jnp.int32) * MOE_BLOCK, side="right"),
                          N_EXPERTS - 1).astype(jnp.int32)
    n_used = (pad_end[-1:] // MOE_BLOCK).astype(jnp.int32)
    tok = jnp.arange(n * TOP_K, dtype=jnp.int32) // TOP_K
    buf_tok = jnp.zeros((n_blocks * MOE_BLOCK,), jnp.int32).at[dest].set(tok)
    yb = _ffn_call(block_e, n_used, buf_tok, h2.reshape(n, d), mp["w1"], mp["w3"], mp["w2"])
    return _combine_call(dest, yb, x, r_g, g2, fin, ts, final)


def _prep_ab(l, p):
    w_in = p["w_in_ab"][l]
    o2 = S5_WIDTH + MLA_Q_RANK + MLA_KV_RANK
    half = MLA_ROPE // 2
    d = w_in.shape[0]
    kr = w_in[:, o2:]
    kr_rot = jnp.concatenate([-kr[:, half:], kr[:, :half]], axis=1)
    z64 = jnp.zeros((d, MLA_NOPE), F32)
    z32 = jnp.zeros((d, LANES - MLA_NOPE - MLA_ROPE), F32)
    w_ext = jnp.concatenate([w_in[:, :o2], z64, kr, z32, z64, kr_rot, z32], axis=1)
    qu = p["mla_q_up"][l].reshape(MLA_Q_RANK, MLA_HEADS, MLA_NOPE + MLA_ROPE)
    qn, qr = qu[:, :, :MLA_NOPE], qu[:, :, MLA_NOPE:]
    zq = jnp.zeros((MLA_Q_RANK, MLA_HEADS, LANES - MLA_NOPE - MLA_ROPE), F32)
    wqa = jnp.concatenate([qn, qr, zq], axis=2).reshape(MLA_Q_RANK, MLA_HEADS * LANES)
    wqb = jnp.concatenate([jnp.zeros_like(qn), -qr[:, :, half:], qr[:, :, :half], zq], axis=2)
    wqb = wqb.reshape(MLA_Q_RANK, MLA_HEADS * LANES)
    kvu = p["mla_kv_up"][l].reshape(MLA_KV_RANK, MLA_HEADS, MLA_NOPE + MLA_V)
    wk = jnp.concatenate([kvu[:, :, :MLA_NOPE], jnp.zeros((MLA_KV_RANK, MLA_HEADS, LANES - MLA_NOPE), F32)], axis=2)
    wo = p["w_out_ab"][l]
    out = dict(
        w_in=w_ext.astype(BF16), q_norm=p["mla_q_norm"][l].reshape(1, -1), kv_norm=p["mla_kv_norm"][l].reshape(1, -1),
        wqa=wqa.astype(BF16), wqb=wqb.astype(BF16),
        wk=wk.reshape(MLA_KV_RANK, MLA_HEADS * LANES).astype(BF16),
        wv=jnp.concatenate([kvu[:, :, MLA_NOPE:], jnp.zeros((MLA_KV_RANK, MLA_HEADS, LANES - MLA_V), F32)],
                           axis=2).reshape(MLA_KV_RANK, MLA_HEADS * LANES).astype(BF16),
        s5_d=p["s5_d"][l].reshape(1, -1), glu_w=p["s5_glu_w"][l].astype(BF16), glu_b=p["s5_glu_b"][l].reshape(1, -1),
        w_out_s5=wo[:S5_WIDTH].astype(BF16),
        w_out_at=wo[S5_WIDTH:].astype(BF16))
    out.update(_s5_mats(p["s5_a_re"][l], p["s5_a_im"][l], p["s5_b_re"][l], p["s5_b_im"][l],
                        p["s5_c_re"][l], p["s5_c_im"][l], p["s5_log_dt"][l]))
    return out


def _prep_c(l, p):
    w = p["w_in_c"][l]
    pad = jnp.zeros((w.shape[0], LANES - 2 * GDN_HEADS), F32)

    def row(v):
        return jnp.concatenate([v, jnp.zeros((LANES - GDN_HEADS,), F32)]).reshape(1, LANES)

    return dict(w_in=jnp.concatenate([w, pad], axis=1).astype(BF16), conv_w=p["conv_w"][l],
                alog=row(p["gdn_a_log"][l]), dtb=row(p["gdn_dt_bias"][l]),
                gn=p["gdn_norm"][l].reshape(1, -1), w_out=p["w_out_c"][l].astype(BF16))


def _prep_moe(layer, p):
    d = p["moe_w_group"].shape[1]
    wr = jnp.concatenate([p["moe_w_group"][layer], p["moe_w_expert"][layer],
                          jnp.zeros((d, LANES - N_GROUPS - N_EXPERTS), F32)], axis=1)
    br = jnp.concatenate([p["moe_b_group"][layer], p["moe_b_expert"][layer],
                          jnp.zeros((LANES - N_GROUPS - N_EXPERTS,), F32)]).reshape(1, LANES)
    return dict(wr=wr, br=br, w1=p["moe_w1"][layer].astype(BF16), w3=p["moe_w3"][layer].astype(BF16),
                w2=p["moe_w2"][layer].astype(BF16))


def _rope_tables(pos):
    half = MLA_ROPE // 2
    inv = ROPE_THETA ** (-jnp.arange(half, dtype=F32) / half)
    ang = pos.astype(F32)[:, None] * inv[None, :]
    cos, sin = jnp.cos(ang), jnp.sin(ang)
    n = pos.shape[0]
    z64 = jnp.zeros((n, MLA_NOPE), F32)
    z32 = jnp.zeros((n, LANES - MLA_NOPE - MLA_ROPE), F32)
    scale = (MLA_NOPE + MLA_ROPE) ** -0.5 * math.log2(math.e)
    ck = jnp.concatenate([z64, cos, cos, z32], axis=1)
    sk = jnp.concatenate([z64, sin, sin, z32], axis=1)
    cq = jnp.concatenate([jnp.ones_like(z64), cos, cos, z32], axis=1) * scale
    return cq, sk * scale, ck, sk


def _trunk(x, mods, caches, wab, wc, wmoe, p, flat):
    b, s, d = x.shape
    lat_c, rope_c, s5re_c, s5im_c, conv_c, gdn_c = caches
    past = 0 if lat_c is None else lat_c.shape[2]
    pos = past + jnp.arange(s)
    if flat:
        xt = x.reshape(1, b * s, d)
        ts = b * s
        pos_rows = jnp.tile(pos, b)

        def mrow(m):
            return jnp.repeat(m, s, axis=0).reshape(1, b * s, d)
    else:
        xt = x
        ts = 256
        pos_rows = pos

        def mrow(m):
            return m.reshape(b, 1, d)
    bt, st = xt.shape[:2]
    outs = {}
    depth = p["norm_mix"].shape[0]
    for layer in range(depth):
        l = layer // 2
        sh1, sc1, g1, sh2, sc2, g2 = [mrow(m) for m in jnp.split(mods[layer], 6, axis=-1)]
        gain = p["norm_mix"][layer].reshape(1, d)
        if layer % 2 == 0:
            wp = wab[l]
            u, q, lat, kr = _ab_in_call(xt, gain, sc1, sh1, wp, _rope_tables(pos_rows), ts)
            lat_b = lat.reshape(b, s, MLA_KV_RANK)
            kr_b = kr.reshape(b, s, LANES)
            outs["lat"] = lat_b
            outs["krope"] = kr_b[:, :, MLA_NOPE:MLA_NOPE + MLA_ROPE]
            if lat_c is None:
                lat_all, kr_all, sk_valid, tkv, tq, tk = lat_b, kr_b, s, 256, 256, min(512, s)
            else:
                krc = jnp.pad(rope_c[l], ((0, 0), (0, 0), (MLA_NOPE, LANES - MLA_NOPE - MLA_ROPE)))
                sk_valid = past + s
                tkv = 128
                skp = (sk_valid + tkv - 1) // tkv * tkv
                lat_all = jnp.pad(jnp.concatenate([lat_c[l], lat_b], axis=1), ((0, 0), (0, skp - sk_valid), (0, 0)))
                kr_all = jnp.pad(jnp.concatenate([krc, kr_b], axis=1), ((0, 0), (0, skp - sk_valid), (0, 0)))
                tq, tk = s, skp
            kk, vv = _kv_call(lat_all, kr_all, wp, tkv)
            qh = q.reshape(MLA_HEADS, b, s, LANES).transpose(1, 0, 2, 3) if flat else q
            attn = _attn_call(qh, kk, vv, tq, tk, past, sk_valid).reshape(bt, st, MLA_HEADS * MLA_V)
            nchk = s // S5_CHUNK
            r = b * nchk
            uc = u.reshape(b, nchk, S5_CHUNK, S5_GROUPS, S5_GROUP).transpose(0, 1, 3, 2, 4)
            uc = uc.reshape(r, S5_GROUPS * S5_CHUNK * S5_GROUP).astype(BF16)
            tr = min(r, 512)
            wre, wim = _s5_w_call(uc, wp["bre"], wp["bim"], tr)
            ngp = S5_GROUPS * S5_STATE

            def to_pairs(v):
                return v.reshape(b, 1, ngp)

            if s5re_c is None:
                x0re = jnp.zeros((b, 1, ngp), F32)
                x0im = jnp.zeros((b, 1, ngp), F32)
            else:
                x0re, x0im = to_pairs(s5re_c[l]), to_pairs(s5im_c[l])
            nchp = (nchk + 7) // 8 * 8
            tc = min(nchp, 128)
            wre3 = jnp.pad(wre.reshape(b, nchk, ngp), ((0, 0), (0, nchp - nchk), (0, 0)))
            wim3 = jnp.pad(wim.reshape(b, nchk, ngp), ((0, 0), (0, nchp - nchk), (0, 0)))
            xere, xeim = _s5_scan_call(wre3, wim3, wp["lre"], wp["lim"], x0re, x0im, tc)
            outs["s5re"] = xere[:, nchk - 1].reshape(b, S5_GROUPS, S5_STATE)
            outs["s5im"] = xeim[:, nchk - 1].reshape(b, S5_GROUPS, S5_STATE)
            xsre = jnp.concatenate([x0re, xere[:, :nchk - 1]], axis=1).reshape(r, ngp)
            xsim = jnp.concatenate([x0im, xeim[:, :nchk - 1]], axis=1).reshape(r, ngp)
            yc = _s5_y_call(uc, xsre, xsim, wp["tm"], wp["cre"], wp["cim"], tr)
            ys = yc.reshape(b, nchk, S5_GROUPS, S5_CHUNK, S5_GROUP).transpose(0, 1, 3, 2, 4).reshape(bt, st, S5_WIDTH)
            xt = _ab_out_call(xt, ys, u, attn, wp, g1, ts)
        else:
            wp = wc[l]
            qkv, z, ab = _c_in_call(xt, gain, sc1, sh1, wp["w_in"], ts)
            qkv_b = qkv.reshape(b, s, GDN_QKV)
            outs["conv"] = qkv_b[:, s - (CONV_W - 1):].astype(F32)
            if conv_c is None:
                st8 = jnp.zeros((b, 8, GDN_QKV), BF16)
                s0 = jnp.zeros((b, GDN_HEADS, GDN_DK, GDN_DV), F32)
            else:
                st8 = jnp.pad(conv_c[l], ((0, 0), (8 - (CONV_W - 1), 0), (0, 0))).astype(BF16)
                s0 = gdn_c[l]
            lg = min(s, CHUNK)
            wq, um, kgt, pm, ee = _gdn_prep_call(qkv_b, st8, wp["conv_w"], ab.reshape(b, s, LANES),
                                                 wp["alog"], wp["dtb"], lg)
            o, sfin = _gdn_seq_call(wq, um, kgt, pm, ee, s0, lg)
            outs["gdn"] = sfin
            xt = _c_out_call(xt, o.reshape(bt, st, GDN_VW), z, wp["gn"], wp["w_out"], g1, ts)
        final = layer == depth - 1
        xt = _moe(xt, p["norm_ffn"][layer].reshape(1, d), sc2, sh2, g2, wmoe[layer],
                  p["norm_final"].reshape(1, d), ts, final)
    y = xt.reshape(b, s, d)
    return (y, outs["lat"][None], outs["krope"][None], outs["s5re"][None], outs["s5im"][None],
            outs["conv"][None], outs["gdn"][None])


def kernel(x_prompt, x_sample, c_prompt, c_sample, cache_mla_latent, cache_mla_krope, state_s5_re, state_s5_im,
           state_conv, state_gdn, w_mod, b_mod, norm_mix, norm_ffn, norm_final, w_in_ab, s5_a_re, s5_a_im,
           s5_b_re, s5_b_im, s5_c_re, s5_c_im, s5_d, s5_log_dt, s5_glu_w, s5_glu_b, mla_q_norm, mla_q_up,
           mla_kv_norm, mla_kv_up, w_out_ab, w_in_c, conv_w, gdn_a_log, gdn_dt_bias, gdn_norm, w_out_c,
           moe_w_group, moe_b_group, moe_w_expert, moe_b_expert, moe_w1, moe_w3, moe_w2):
    p = dict(w_mod=w_mod, b_mod=b_mod, norm_mix=norm_mix, norm_ffn=norm_ffn, norm_final=norm_final,
             w_in_ab=w_in_ab, s5_a_re=s5_a_re, s5_a_im=s5_a_im, s5_b_re=s5_b_re, s5_b_im=s5_b_im,
             s5_c_re=s5_c_re, s5_c_im=s5_c_im, s5_d=s5_d, s5_log_dt=s5_log_dt, s5_glu_w=s5_glu_w,
             s5_glu_b=s5_glu_b, mla_q_norm=mla_q_norm, mla_q_up=mla_q_up, mla_kv_norm=mla_kv_norm,
             mla_kv_up=mla_kv_up, w_out_ab=w_out_ab, w_in_c=w_in_c, conv_w=conv_w, gdn_a_log=gdn_a_log,
             gdn_dt_bias=gdn_dt_bias, gdn_norm=gdn_norm, w_out_c=w_out_c, moe_w_group=moe_w_group,
             moe_b_group=moe_b_group, moe_w_expert=moe_w_expert, moe_b_expert=moe_b_expert,
             moe_w1=moe_w1, moe_w3=moe_w3, moe_w2=moe_w2)
    depth = norm_mix.shape[0]
    bp, bs = c_prompt.shape[0], c_sample.shape[0]
    nb = (bp + bs + 7) // 8 * 8
    c_all = jnp.pad(jnp.concatenate([c_prompt, c_sample], axis=0), ((0, nb - bp - bs), (0, 0)))
    mods = _mod_call(c_all, w_mod, b_mod)
    wab = [_prep_ab(l, p) for l in range((depth + 1) // 2)]
    wc = [_prep_c(l, p) for l in range(depth // 2)]
    wmoe = [_prep_moe(layer, p) for layer in range(depth)]
    none6 = (None,) * 6
    outp = _trunk(x_prompt, mods[:, :bp], none6, wab, wc, wmoe, p, flat=False)
    caches = (cache_mla_latent, cache_mla_krope, state_s5_re, state_s5_im, state_conv, state_gdn)
    outs = _trunk(x_sample, mods[:, bp:bp + bs], caches, wab, wc, wmoe, p, flat=True)
    return (outp[0], outs[0]) + tuple(outp[1:]) + tuple(outs[1:])
```

```python
import functools
import math

import jax
import jax.numpy as jnp
from jax import lax
from jax.experimental import pallas as pl
from jax.experimental.pallas import tpu as pltpu

F32 = jnp.float32
BF16 = jnp.bfloat16
HI = lax.Precision.HIGHEST
EPS = 1e-6

D_MODEL = 1024
CHUNK = 64
S5_WIDTH = 512
S5_GROUP = 16
S5_GROUPS = 32
S5_STATE = 64
S5_CHUNK = 16
S5_PAIR = 2 * S5_CHUNK * S5_GROUP
MLA_HEADS = 8
MLA_NOPE = 64
MLA_ROPE = 32
MLA_V = 64
MLA_Q_RANK = 384
MLA_KV_RANK = 256
ROPE_THETA = 10000.0
GDN_HEADS = 8
GDN_DK = 128
GDN_DV = 128
CONV_W = 4
GDN_QKV = GDN_HEADS * (2 * GDN_DK + GDN_DV)
GDN_VW = GDN_HEADS * GDN_DV
N_GROUPS = 4
EXPERTS_PER_GROUP = 8
N_EXPERTS = 32
D_EXPERT = 512
TOP_K = 2
MOE_BLOCK = 128
ATTN_SUB = 256
LANES = 128
VMEM_LIMIT = 48 * 1024 * 1024


def _cparams(*sem):
    return pltpu.CompilerParams(dimension_semantics=sem, vmem_limit_bytes=VMEM_LIMIT)


def _dot(a, b):
    return jnp.dot(a, b, preferred_element_type=F32)


def _dot_nt(a, b, precision=None):
    return lax.dot_general(a, b, (((1,), (1,)), ((), ())), preferred_element_type=F32, precision=precision)


def _full(arr):
    nd = arr.ndim
    return pl.BlockSpec(arr.shape, lambda *_: (0,) * nd)


def _tok(width, ts):
    return pl.BlockSpec((1, ts, width), lambda b, j: (b, j, 0))


def _modspec(arr, ts):
    if arr.shape[1] == 1:
        return pl.BlockSpec((1, 1, arr.shape[2]), lambda b, j: (b, 0, 0))
    return pl.BlockSpec((1, ts, arr.shape[2]), lambda b, j: (b, j, 0))


def _normmod(x, gain, sc, sh):
    ms = jnp.mean(x * x, axis=-1, keepdims=True)
    return x * lax.rsqrt(ms + EPS) * gain * (1.0 + sc) + sh


def _rms(x, gain):
    return x * lax.rsqrt(jnp.mean(x * x, axis=-1, keepdims=True) + EPS) * gain


def _mod_kernel(c_ref, w_ref, b_ref, o_ref):
    c = c_ref[...]
    a = (c * jax.nn.sigmoid(c)).astype(BF16)
    o_ref[0] = _dot(a, w_ref[0]) + b_ref[0]


def _mod_call(c_all, w_mod, b_mod):
    depth, d, n6 = w_mod.shape
    bp = c_all.shape[0]
    tn = 1536
    return pl.pallas_call(
        _mod_kernel,
        grid=(depth, n6 // tn),
        in_specs=[pl.BlockSpec((bp, d), lambda l, n: (0, 0)),
                  pl.BlockSpec((1, d, tn), lambda l, n: (l, 0, n)),
                  pl.BlockSpec((1, 1, tn), lambda l, n: (l, 0, n))],
        out_specs=pl.BlockSpec((1, bp, tn), lambda l, n: (l, 0, n)),
        out_shape=jax.ShapeDtypeStruct((depth, bp, n6), F32),
        compiler_params=_cparams("parallel", "parallel"),
        name="mod",
    )(c_all, w_mod.astype(BF16), b_mod.reshape(depth, 1, n6))


def _ab_in_kernel(x_ref, gain_ref, sc_ref, sh_ref, w_ref, qg_ref, wqa_ref, wqb_ref, kvg_ref,
                  cq_ref, sq_ref, ck_ref, sk_ref, u_ref, q_ref, lat_ref, kr_ref):
    h = _normmod(x_ref[0], gain_ref[...], sc_ref[0], sh_ref[0]).astype(BF16)
    proj = _dot(h, w_ref[...])
    u_ref[0] = proj[:, :S5_WIDTH]
    o1 = S5_WIDTH + MLA_Q_RANK
    o2 = o1 + MLA_KV_RANK
    qn = _rms(proj[:, S5_WIDTH:o1], qg_ref[...]).astype(BF16)
    qa = _dot(qn, wqa_ref[...])
    qb = _dot(qn, wqb_ref[...])
    cq = cq_ref[...]
    sq = sq_ref[...]
    for hh in range(MLA_HEADS):
        sl = slice(LANES * hh, LANES * (hh + 1))
        q_ref[0, hh] = (qa[:, sl] * cq + qb[:, sl] * sq).astype(BF16)
    lat_ref[0] = _rms(proj[:, o1:o2], kvg_ref[...])
    kr_ref[0] = proj[:, o2:o2 + LANES] * ck_ref[...] + proj[:, o2 + LANES:o2 + 2 * LANES] * sk_ref[...]


def _ab_in_call(x, gain, sc, sh, wp, tabs, ts):
    b, s, d = x.shape
    cq, sq, ck, sk = tabs
    tab = pl.BlockSpec((ts, LANES), lambda bb, j: (j, 0))
    ins = [x, gain, sc, sh, wp["w_in"], wp["q_norm"], wp["wqa"], wp["wqb"], wp["kv_norm"], cq, sq, ck, sk]
    specs = [_tok(d, ts), _full(gain), _modspec(sc, ts), _modspec(sh, ts), _full(wp["w_in"]), _full(wp["q_norm"]),
             _full(wp["wqa"]), _full(wp["wqb"]), _full(wp["kv_norm"]), tab, tab, tab, tab]
    return pl.pallas_call(
        _ab_in_kernel,
        grid=(b, s // ts),
        in_specs=specs,
        out_specs=[_tok(S5_WIDTH, ts),
                   pl.BlockSpec((1, MLA_HEADS, ts, LANES), lambda bb, j: (bb, 0, j, 0)),
                   _tok(MLA_KV_RANK, ts), _tok(LANES, ts)],
        out_shape=[jax.ShapeDtypeStruct((b, s, S5_WIDTH), F32),
                   jax.ShapeDtypeStruct((b, MLA_HEADS, s, LANES), BF16),
                   jax.ShapeDtypeStruct((b, s, MLA_KV_RANK), F32),
                   jax.ShapeDtypeStruct((b, s, LANES), F32)],
        compiler_params=_cparams("parallel", "parallel"),
        name="ab_in",
    )(*ins)


def _kv_kernel(lat_ref, kr_ref, wk_ref, wv_ref, k_ref, v_ref):
    lat = lat_ref[0].astype(BF16)
    kk = _dot(lat, wk_ref[...])
    vv = _dot(lat, wv_ref[...])
    kr = kr_ref[0]
    ones = (lax.broadcasted_iota(jnp.int32, (1, LANES), 1) >= MLA_V).astype(F32)
    for hh in range(MLA_HEADS):
        k_ref[0, hh] = (kk[:, LANES * hh:LANES * (hh + 1)] + kr).astype(BF16)
        v_ref[0, hh] = (vv[:, LANES * hh:LANES * (hh + 1)] + ones).astype(BF16)


def _kv_call(lat_all, kr_all, wp, ts):
    b, sk, _ = lat_all.shape
    hspec = pl.BlockSpec((1, MLA_HEADS, ts, LANES), lambda bb, j: (bb, 0, j, 0))
    hshape = jax.ShapeDtypeStruct((b, MLA_HEADS, sk, LANES), BF16)
    return pl.pallas_call(
        _kv_kernel,
        grid=(b, sk // ts),
        in_specs=[_tok(MLA_KV_RANK, ts), _tok(LANES, ts), _full(wp["wk"]), _full(wp["wv"])],
        out_specs=[hspec, hspec],
        out_shape=[hshape, hshape],
        compiler_params=_cparams("parallel", "parallel"),
        name="kv_up",
    )(lat_all, kr_all, wp["wk"], wp["wv"])


def _attn_kernel(q_ref, k_ref, v_ref, o_ref, *, tq, tk, q_off, sk_valid):
    i = pl.program_id(2)
    nsub = max(tq // ATTN_SUB, 1)
    tqs = tq // nsub
    chains = [(hh, qi) for hh in range(2) for qi in range(nsub)]
    qs = [q_ref[0, hh, qi * tqs:(qi + 1) * tqs, :] for hh, qi in chains]
    q0 = q_off + i * tq
    lim_full = (q0 // CHUNK + 1) * CHUNK
    lim_tot = jnp.minimum(((q0 + tq - 1) // CHUNK + 1) * CHUNK, sk_valid)
    n_full = jnp.minimum(lim_full // tk, sk_valid // tk)
    n_tot = (lim_tot + tk - 1) // tk

    def step(j, carry, masked):
        off = pl.multiple_of(j * tk, tk)
        nc = range(len(chains))
        if masked:
            kpos = off + lax.broadcasted_iota(jnp.int32, (tqs, tk), 1)
            ok = []
            for qi in range(nsub):
                qpos = q0 + qi * tqs + lax.broadcasted_iota(jnp.int32, (tqs, tk), 0)
                ok.append(jnp.logical_and(kpos // CHUNK <= qpos // CHUNK, kpos < sk_valid))
        kt = [k_ref[0, hh, pl.ds(off, tk), :] for hh in range(2)]
        vt = [v_ref[0, hh, pl.ds(off, tk), :] for hh in range(2)]
        s = [_dot_nt(qs[c], kt[chains[c][0]]) for c in nc]
        if masked:
            s = [jnp.where(ok[chains[c][1]], s[c], -1e30) for c in nc]
        m_new = [jnp.maximum(carry[c][0], jnp.max(s[c], axis=-1, keepdims=True)) for c in nc]
        alpha = [jnp.exp2(carry[c][0] - m_new[c]) for c in nc]
        p = [jnp.exp2(s[c] - m_new[c]).astype(BF16) for c in nc]
        acc = [alpha[c] * carry[c][1] + _dot(p[c], vt[chains[c][0]]) for c in nc]
        return tuple((m_new[c], acc[c]) for c in nc)

    one = (jnp.full((tqs, 1), -1e30, F32), jnp.zeros((tqs, LANES), F32))
    carry = lax.fori_loop(0, n_full, functools.partial(step, masked=False), (one,) * len(chains))
    carry = lax.fori_loop(n_full, n_tot, functools.partial(step, masked=True), carry)
    lane = lax.broadcasted_iota(jnp.int32, (tqs, LANES), 1)
    for qi in range(nsub):
        a0 = carry[chains.index((0, qi))][1]
        a1 = carry[chains.index((1, qi))][1]
        r0 = a0 / pltpu.roll(a0, MLA_V, axis=1)
        r1 = a1 / pltpu.roll(a1, MLA_V, axis=1)
        o_ref[0, qi * tqs:(qi + 1) * tqs, :] = jnp.where(
            lane < MLA_V, r0, pltpu.roll(r1, MLA_V, axis=1)).astype(o_ref.dtype)


def _attn_call(q, k, v, tq, tk, q_off, sk_valid):
    b, h, sq, _ = q.shape
    sk = k.shape[2]
    kern = functools.partial(_attn_kernel, tq=tq, tk=tk, q_off=q_off, sk_valid=sk_valid)
    return pl.pallas_call(
        kern,
        grid=(b, h // 2, sq // tq),
        in_specs=[pl.BlockSpec((1, 2, tq, LANES), lambda bb, hp, i: (bb, hp, i, 0)),
                  pl.BlockSpec((1, 2, sk, LANES), lambda bb, hp, i: (bb, hp, 0, 0), pipeline_mode=pl.Buffered(1)),
                  pl.BlockSpec((1, 2, sk, LANES), lambda bb, hp, i: (bb, hp, 0, 0), pipeline_mode=pl.Buffered(1))],
        out_specs=pl.BlockSpec((1, tq, LANES), lambda bb, hp, i: (bb, i, hp)),
        out_shape=jax.ShapeDtypeStruct((b, sq, h * MLA_V), BF16),
        compiler_params=_cparams("parallel", "parallel", "arbitrary"),
        name="mla_attn",
    )(q, k, v)


def _s5_w_kernel(u_ref, bre_ref, bim_ref, wre_ref, wim_ref):
    u = u_ref[...]
    wre_ref[...] = _dot(u, bre_ref[0])
    wim_ref[...] = _dot(u, bim_ref[0])


def _s5_w_call(uc, bre, bim, tr):
    r = uc.shape[0]
    npair = bre.shape[0]
    out = jax.ShapeDtypeStruct((r, npair * LANES), F32)
    ospec = pl.BlockSpec((tr, LANES), lambda g, i: (i, g))
    wspec = pl.BlockSpec((1, S5_PAIR, LANES), lambda g, i: (g, 0, 0))
    return pl.pallas_call(
        _s5_w_kernel,
        grid=(npair, r // tr),
        in_specs=[pl.BlockSpec((tr, S5_PAIR), lambda g, i: (i, g)), wspec, wspec],
        out_specs=[ospec, ospec],
        out_shape=[out, out],
        compiler_params=_cparams("parallel", "parallel"),
        name="s5_chunk_in",
    )(uc, bre, bim)


def _s5_scan_kernel(wre_ref, wim_ref, lre_ref, lim_ref, x0re_ref, x0im_ref, ore_ref, oim_ref, sre, sim, *, tc):
    @pl.when(pl.program_id(1) == 0)
    def _():
        sre[...] = x0re_ref[0]
        sim[...] = x0im_ref[0]

    lr = lre_ref[...]
    li = lim_ref[...]

    def blk(t, carry):
        xr, xi = carry
        base = pl.multiple_of(t * 8, 8)
        wr = wre_ref[0, pl.ds(base, 8), :]
        wi = wim_ref[0, pl.ds(base, 8), :]
        rows_r, rows_i = [], []
        for r in range(8):
            nr = lr * xr - li * xi + wr[r:r + 1]
            ni = lr * xi + li * xr + wi[r:r + 1]
            xr, xi = nr, ni
            rows_r.append(xr)
            rows_i.append(xi)
        ore_ref[0, pl.ds(base, 8), :] = jnp.concatenate(rows_r, axis=0)
        oim_ref[0, pl.ds(base, 8), :] = jnp.concatenate(rows_i, axis=0)
        return xr, xi

    xr, xi = lax.fori_loop(0, tc // 8, blk, (sre[...], sim[...]))
    sre[...] = xr
    sim[...] = xi


def _s5_scan_call(wre, wim, lre, lim, x0re, x0im, tc):
    b, nch, n = wre.shape
    wspec = pl.BlockSpec((1, tc, n), lambda bb, c: (bb, c, 0))
    lspec = pl.BlockSpec((1, n), lambda bb, c: (0, 0))
    xspec = pl.BlockSpec((1, 1, n), lambda bb, c: (bb, 0, 0))
    out = jax.ShapeDtypeStruct((b, nch, n), F32)
    return pl.pallas_call(
        functools.partial(_s5_scan_kernel, tc=tc),
        grid=(b, nch // tc),
        in_specs=[wspec, wspec, lspec, lspec, xspec, xspec],
        out_specs=[wspec, wspec],
        out_shape=[out, out],
        scratch_shapes=[pltpu.VMEM((1, n), F32), pltpu.VMEM((1, n), F32)],
        compiler_params=_cparams("parallel", "arbitrary"),
        name="s5_scan",
    )(wre, wim, lre, lim, x0re, x0im)


def _s5_y_kernel(u_ref, xre_ref, xim_ref, t_ref, cre_ref, cim_ref, y_ref):
    y = _dot(u_ref[...], t_ref[0])
    y += _dot(xre_ref[...].astype(BF16), cre_ref[0])
    y += _dot(xim_ref[...].astype(BF16), cim_ref[0])
    y_ref[...] = y


def _s5_y_call(uc, xre, xim, tm, cre, cim, tr):
    r = uc.shape[0]
    npair = tm.shape[0]
    uspec = pl.BlockSpec((tr, S5_PAIR), lambda g, i: (i, g))
    xspec = pl.BlockSpec((tr, LANES), lambda g, i: (i, g))
    return pl.pallas_call(
        _s5_y_kernel,
        grid=(npair, r // tr),
        in_specs=[uspec, xspec, xspec,
                  pl.BlockSpec((1, S5_PAIR, S5_PAIR), lambda g, i: (g, 0, 0)),
                  pl.BlockSpec((1, LANES, S5_PAIR), lambda g, i: (g, 0, 0)),
                  pl.BlockSpec((1, LANES, S5_PAIR), lambda g, i: (g, 0, 0))],
        out_specs=uspec,
        out_shape=jax.ShapeDtypeStruct((r, npair * S5_PAIR), F32),
        compiler_params=_cparams("parallel", "parallel"),
        name="s5_chunk_out",
    )(uc, xre, xim, tm, cre, cim)


def _s5_mats(a_re, a_im, b_re, b_im, c_re, c_im, log_dt):
    g, p = a_re.shape
    L = S5_CHUNK
    lam = lax.complex(jnp.minimum(a_re, -1e-4), a_im)
    lamdt = lam * jnp.exp(log_dt)[:, None]
    lam_bar = jnp.exp(lamdt)
    b_bar = ((lam_bar - 1.0) / lam)[:, :, None] * lax.complex(b_re, b_im)
    cm = lax.complex(c_re, c_im)
    pw = jnp.exp(lamdt[:, :, None] * jnp.arange(L + 1, dtype=F32))
    mm = pw[:, :, :L, None] * b_bar[:, :, None, :]
    kd = jnp.sum((cm[:, :, :, None, None] * mm[:, None]).real, axis=2).transpose(0, 2, 1, 3)
    s_i = jnp.arange(L)[:, None]
    t_i = jnp.arange(L)[None, :]
    dd = t_i - s_i
    tm = jnp.where((dd >= 0)[None, :, :, None, None], kd[:, jnp.clip(dd, 0, L - 1)], 0.0)
    tm = tm.transpose(0, 1, 4, 2, 3).reshape(g, L * S5_GROUP, L * S5_GROUP)
    bp = pw[:, :, L - 1 - jnp.arange(L)][:, :, :, None] * b_bar[:, :, None, :]
    bp = bp.transpose(0, 2, 3, 1).reshape(g, L * S5_GROUP, p)
    cp = cm[:, :, :, None] * pw[:, None, :, 1:]
    cp = cp.transpose(0, 2, 3, 1).reshape(g, p, L * S5_GROUP)

    def pair_diag(m):
        gg, r, c = m.shape
        m2 = m.reshape(gg // 2, 2, r, c)
        z = jnp.zeros_like(m2[:, 0])
        top = jnp.concatenate([m2[:, 0], z], axis=2)
        bot = jnp.concatenate([z, m2[:, 1]], axis=2)
        return jnp.concatenate([top, bot], axis=1)

    return dict(
        tm=pair_diag(tm).astype(BF16),
        bre=pair_diag(bp.real).astype(BF16), bim=pair_diag(bp.imag).astype(BF16),
        cre=pair_diag(cp.real).astype(BF16), cim=pair_diag(-cp.imag).astype(BF16),
        lre=pw[:, :, L].real.reshape(1, g * p), lim=pw[:, :, L].imag.reshape(1, g * p))


def _ab_out_kernel(x_ref, ys_ref, u_ref, at_ref, d_ref, gw_ref, gb_ref, ws_ref, wa_ref, g1_ref, o_ref):
    y = ys_ref[0] + d_ref[...] * u_ref[0]
    y = jax.nn.gelu(y)
    gate = jax.nn.sigmoid(_dot(y.astype(BF16), gw_ref[...]) + gb_ref[...])
    out = _dot((y * gate).astype(BF16), ws_ref[...]) + _dot(at_ref[0], wa_ref[...])
    o_ref[0] = x_ref[0] + g1_ref[0] * out


def _ab_out_call(x, ys, u, attn, wp, g1, ts):
    b, s, d = x.shape
    ins = [x, ys, u, attn, wp["s5_d"], wp["glu_w"], wp["glu_b"], wp["w_out_s5"], wp["w_out_at"], g1]
    specs = [_tok(d, ts), _tok(S5_WIDTH, ts), _tok(S5_WIDTH, ts), _tok(MLA_HEADS * MLA_V, ts),
             _full(wp["s5_d"]), _full(wp["glu_w"]), _full(wp["glu_b"]), _full(wp["w_out_s5"]),
             _full(wp["w_out_at"]), _modspec(g1, ts)]
    return pl.pallas_call(
        _ab_out_kernel,
        grid=(b, s // ts),
        in_specs=specs,
        out_specs=_tok(d, ts),
        out_shape=jax.ShapeDtypeStruct((b, s, d), F32),
        compiler_params=_cparams("parallel", "parallel"),
        name="ab_out",
    )(*ins)


def _c_in_kernel(x_ref, gain_ref, sc_ref, sh_ref, w_ref, qkv_ref, z_ref, ab_ref):
    h = _normmod(x_ref[0], gain_ref[...], sc_ref[0], sh_ref[0]).astype(BF16)
    proj = _dot(h, w_ref[...])
    qkv_ref[0] = proj[:, :GDN_QKV].astype(BF16)
    z_ref[0] = proj[:, GDN_QKV:GDN_QKV + GDN_VW].astype(BF16)
    ab_ref[0] = proj[:, GDN_QKV + GDN_VW:]


def _c_in_call(x, gain, sc, sh, w, ts):
    b, s, d = x.shape
    return pl.pallas_call(
        _c_in_kernel,
        grid=(b, s // ts),
        in_specs=[_tok(d, ts), _full(gain), _modspec(sc, ts), _modspec(sh, ts), _full(w)],
        out_specs=[_tok(GDN_QKV, ts), _tok(GDN_VW, ts), _tok(LANES, ts)],
        out_shape=[jax.ShapeDtypeStruct((b, s, GDN_QKV), BF16),
                   jax.ShapeDtypeStruct((b, s, GDN_VW), BF16),
                   jax.ShapeDtypeStruct((b, s, LANES), F32)],
        compiler_params=_cparams("parallel", "parallel"),
        name="c_in",
    )(x, gain, sc, sh, w)


def _gdn_prep_kernel(qkv_ref, halo_ref, st_ref, cw_ref, ab_ref, alog_ref, dtb_ref,
                     wq_ref, u_ref, kgt_ref, pm_ref, ee_ref, *, L):
    j = pl.program_id(1)
    prev = jnp.where(j == 0, st_ref[0].astype(F32), halo_ref[0].astype(F32))
    xx = jnp.concatenate([prev, qkv_ref[0].astype(F32)], axis=0)
    cw = cw_ref[...]
    y = xx[8:8 + L] * cw[CONV_W - 1:CONV_W]
    for t in range(1, CONV_W):
        y += xx[8 - t:8 - t + L] * cw[CONV_W - 1 - t:CONV_W - t]
    y = y * jax.nn.sigmoid(y)

    ab = ab_ref[0]
    g_all = -jnp.exp(alog_ref[...]) * jax.nn.softplus(ab + dtb_ref[...])
    beta_all = jax.nn.sigmoid(ab)
    row = lax.broadcasted_iota(jnp.int32, (L, L), 0)
    col = lax.broadcasted_iota(jnp.int32, (L, L), 1)
    incl = row >= col
    strict = row > col
    gam_all = jnp.dot(incl.astype(F32), g_all, preferred_element_type=F32, precision=HI)
    eye_l = (lax.broadcasted_iota(jnp.int32, (LANES, LANES), 0)
             == lax.broadcasted_iota(jnp.int32, (LANES, LANES), 1))
    gam_t = _dot_nt(eye_l.astype(F32), gam_all, precision=HI)
    ee_ref[0, 0] = jnp.exp(gam_all[L - 1:L])
    eye64 = (row == col).astype(F32)
    eye_bf = eye_l.astype(BF16)

    hs = range(GDN_HEADS)
    koff, voff = GDN_HEADS * GDN_DK, 2 * GDN_HEADS * GDN_DK
    qh = [y[:, GDN_DK * h:GDN_DK * (h + 1)] for h in hs]
    kh = [y[:, koff + GDN_DK * h:koff + GDN_DK * (h + 1)] for h in hs]
    vh = [y[:, voff + GDN_DV * h:voff + GDN_DV * (h + 1)] for h in hs]
    qh = [q * lax.rsqrt(jnp.sum(q * q, axis=-1, keepdims=True) + EPS) * (GDN_DK ** -0.5) for q in qh]
    kh = [k * lax.rsqrt(jnp.sum(k * k, axis=-1, keepdims=True) + EPS) for k in kh]
    gam_c = [gam_all[:, h:h + 1] for h in hs]
    beta_c = [beta_all[:, GDN_HEADS + h:GDN_HEADS + h + 1] for h in hs]
    dec = [jnp.exp(jnp.minimum(gam_c[h] - gam_t[h:h + 1, :], 0.0)) for h in hs]
    kb = [k.astype(BF16) for k in kh]
    kk = [_dot_nt(kb[h], kb[h]) for h in hs]
    qk = [_dot_nt(qh[h].astype(BF16), kb[h]) for h in hs]
    a = [jnp.where(strict, beta_c[h] * kk[h] * dec[h], 0.0) for h in hs]
    for h in hs:
        pm_ref[0, 0, h] = jnp.where(incl, qk[h] * dec[h], 0.0).astype(BF16)
    tinv = [eye64 - a[h] for h in hs]
    pw = [_split(a[h]) for h in hs]
    for _ in range(max(L.bit_length() - 2, 0)):
        pw = [_split(_dot3(pw[h], pw[h])) for h in hs]
        tinv = [tinv[h] + _dot3(_split(tinv[h]), pw[h]) for h in hs]
    eg = [jnp.exp(gam_c[h]) for h in hs]
    rhs = [jnp.concatenate([(beta_c[h] * eg[h]) * kh[h], beta_c[h] * vh[h]], axis=1).astype(BF16) for h in hs]
    wu = [_dot(tinv[h].astype(BF16), rhs[h]) for h in hs]
    for h in hs:
        wq_ref[0, 0, h] = jnp.concatenate([wu[h][:, :GDN_DK], qh[h] * eg[h]], axis=0).astype(BF16)
        u_ref[0, h] = wu[h][:, GDN_DK:].astype(BF16)
    kg = [(kh[h] * jnp.exp(gam_all[L - 1:L, h:h + 1] - gam_c[h])).astype(BF16) for h in hs]
    kgt = [_dot_nt(eye_bf, kg[h]) for h in hs]
    for h in hs:
        kgt_ref[0, 0, h] = kgt[h].astype(BF16)


def _split(a):
    hi = a.astype(BF16)
    return hi, (a - hi.astype(F32)).astype(BF16)


def _dot3(a, b):
    return _dot(a[0], b[0]) + (_dot(a[0], b[1]) + _dot(a[1], b[0]))


def _gdn_prep_call(qkv, st8, conv_w, ab, alog, dtb, L):
    b, s, _ = qkv.shape
    nch = s // L
    return pl.pallas_call(
        functools.partial(_gdn_prep_kernel, L=L),
        grid=(b, nch),
        in_specs=[_tok(GDN_QKV, L),
                  pl.BlockSpec((1, 8, GDN_QKV), lambda bb, j: (bb, jnp.maximum(j * (L // 8) - 1, 0), 0)),
                  pl.BlockSpec((1, 8, GDN_QKV), lambda bb, j: (bb, 0, 0)),
                  _full(conv_w), _tok(LANES, L), _full(alog), _full(dtb)],
        out_specs=[pl.BlockSpec((1, 1, GDN_HEADS, 2 * L, LANES), lambda bb, j: (bb, j, 0, 0, 0)),
                   pl.BlockSpec((1, GDN_HEADS, L, LANES), lambda bb, j: (bb, 0, j, 0)),
                   pl.BlockSpec((1, 1, GDN_HEADS, GDN_DK, L), lambda bb, j: (bb, j, 0, 0, 0)),
                   pl.BlockSpec((1, 1, GDN_HEADS, L, L), lambda bb, j: (bb, j, 0, 0, 0)),
                   pl.BlockSpec((1, 1, 1, LANES), lambda bb, j: (bb, j, 0, 0))],
        out_shape=[jax.ShapeDtypeStruct((b, nch, GDN_HEADS, 2 * L, LANES), BF16),
                   jax.ShapeDtypeStruct((b, GDN_HEADS, s, LANES), BF16),
                   jax.ShapeDtypeStruct((b, nch, GDN_HEADS, GDN_DK, L), BF16),
                   jax.ShapeDtypeStruct((b, nch, GDN_HEADS, L, L), BF16),
                   jax.ShapeDtypeStruct((b, nch, 1, LANES), F32)],
        compiler_params=_cparams("parallel", "parallel"),
        name="gdn_prep",
    )(qkv, qkv, st8, conv_w, ab, alog, dtb)


GDN_BATCH_GROUP = 2


def _gdn_seq_kernel(wq_ref, u_ref, kgt_ref, pm_ref, ee_ref, s0_ref, o_ref, sf_ref, s_scr, *, L):
    c = pl.program_id(1)

    @pl.when(c == 0)
    def _():
        s_scr[...] = s0_ref[...]

    ch = [(bb, hh) for bb in range(GDN_BATCH_GROUP) for hh in range(GDN_HEADS)]
    st = [s_scr[bb, hh] for bb, hh in ch]
    sp = [_split(s) for s in st]
    r = [_dot(wq_ref[bb, 0, hh], sp[i][0]) + _dot(wq_ref[bb, 0, hh], sp[i][1]) for i, (bb, hh) in enumerate(ch)]
    v_new = [(u_ref[bb, hh].astype(F32) - r[i][:L]).astype(BF16) for i, (bb, hh) in enumerate(ch)]
    o = [r[i][L:] + _dot(pm_ref[bb, 0, hh], v_new[i]) for i, (bb, hh) in enumerate(ch)]
    for i, (bb, hh) in enumerate(ch):
        o_ref[bb, :, GDN_DV * hh:GDN_DV * (hh + 1)] = o[i].astype(o_ref.dtype)
    s_new = [ee_ref[bb, 0][:, hh:hh + 1] * st[i] + _dot(kgt_ref[bb, 0, hh], v_new[i]) for i, (bb, hh) in enumerate(ch)]
    for i, (bb, hh) in enumerate(ch):
        s_scr[bb, hh] = s_new[i]

    @pl.when(c == pl.num_programs(1) - 1)
    def _():
        sf_ref[...] = s_scr[...]


def _gdn_seq_call(wq, um, kgt, pm, ee, s0, L):
    b, h, s, _ = um.shape
    nch = s // L
    bg = GDN_BATCH_GROUP
    sspec = pl.BlockSpec((bg, h, GDN_DK, GDN_DV), lambda g, c: (g, 0, 0, 0))
    return pl.pallas_call(
        functools.partial(_gdn_seq_kernel, L=L),
        grid=(b // bg, nch),
        in_specs=[pl.BlockSpec((bg, 1, h, 2 * L, LANES), lambda g, c: (g, c, 0, 0, 0)),
                  pl.BlockSpec((bg, h, L, LANES), lambda g, c: (g, 0, c, 0)),
                  pl.BlockSpec((bg, 1, h, GDN_DK, L), lambda g, c: (g, c, 0, 0, 0)),
                  pl.BlockSpec((bg, 1, h, L, L), lambda g, c: (g, c, 0, 0, 0)),
                  pl.BlockSpec((bg, 1, 1, LANES), lambda g, c: (g, c, 0, 0)),
                  sspec],
        out_specs=[pl.BlockSpec((bg, L, GDN_VW), lambda g, c: (g, c, 0)), sspec],
        out_shape=[jax.ShapeDtypeStruct((b, s, GDN_VW), BF16),
                   jax.ShapeDtypeStruct((b, h, GDN_DK, GDN_DV), F32)],
        scratch_shapes=[pltpu.VMEM((bg, h, GDN_DK, GDN_DV), F32)],
        compiler_params=_cparams("parallel", "arbitrary"),
        name="gdn_seq",
    )(wq, um, kgt, pm, ee, s0)


def _c_out_kernel(x_ref, o_ref, z_ref, gn_ref, w_ref, g1_ref, out_ref):
    o = o_ref[0].astype(F32)
    z = z_ref[0].astype(F32)
    gn = gn_ref[...]
    parts = []
    for hh in range(GDN_HEADS):
        oh = o[:, GDN_DV * hh:GDN_DV * (hh + 1)]
        zh = z[:, GDN_DV * hh:GDN_DV * (hh + 1)]
        parts.append((_rms(oh, gn) * (zh * jax.nn.sigmoid(zh))).astype(BF16))
    out = _dot(jnp.concatenate(parts, axis=1), w_ref[...])
    out_ref[0] = x_ref[0] + g1_ref[0] * out


def _c_out_call(x, o, z, gn, w, g1, ts):
    b, s, d = x.shape
    return pl.pallas_call(
        _c_out_kernel,
        grid=(b, s // ts),
        in_specs=[_tok(d, ts), _tok(GDN_VW, ts), _tok(GDN_VW, ts), _full(gn), _full(w), _modspec(g1, ts)],
        out_specs=_tok(d, ts),
        out_shape=jax.ShapeDtypeStruct((b, s, d), F32),
        compiler_params=_cparams("parallel", "parallel"),
        name="c_out",
    )(x, o, z, gn, w, g1)


def _pack_halves(v):
    n = v.shape[1] // 2
    bits = pltpu.bitcast(v.astype(BF16).astype(F32), jnp.uint32)
    return (bits[:, :n] >> 16) | (bits[:, n:] & jnp.uint32(0xFFFF0000))


def _unpack_halves(w):
    return pltpu.bitcast(w << 16, F32), pltpu.bitcast(w & jnp.uint32(0xFFFF0000), F32)


def _router_kernel(x_ref, gain_ref, sc_ref, sh_ref, wr_ref, br_ref, h_ref, re_ref, rg_ref, cnt_ref, carry):
    @pl.when(jnp.logical_and(pl.program_id(0) == 0, pl.program_id(1) == 0))
    def _():
        carry[...] = jnp.zeros_like(carry)

    h = _normmod(x_ref[0], gain_ref[...], sc_ref[0], sh_ref[0])
    h_ref[0] = _pack_halves(h)
    logits = jnp.dot(h, wr_ref[...], preferred_element_type=F32, precision=HI) + br_ref[...]
    lane = lax.broadcasted_iota(jnp.int32, logits.shape, 1).astype(F32)
    neg = jnp.float32(-jnp.inf)
    big = jnp.float32(1e9)
    gl = jnp.where(lane < N_GROUPS, logits, neg)
    gm = jnp.max(gl, axis=-1, keepdims=True)
    g_val = 1.0 / jnp.sum(jnp.exp(gl - gm), axis=-1, keepdims=True)
    g_idx = jnp.min(jnp.where(gl == gm, lane, big), axis=-1, keepdims=True)
    lo = N_GROUPS + EXPERTS_PER_GROUP * g_idx
    el = jnp.where(jnp.logical_and(lane >= lo, lane < lo + EXPERTS_PER_GROUP), logits, neg)
    em = jnp.max(el, axis=-1, keepdims=True)
    i1 = jnp.min(jnp.where(el == em, lane, big), axis=-1, keepdims=True)
    el2 = jnp.where(lane == i1, neg, el)
    em2 = jnp.max(el2, axis=-1, keepdims=True)
    i2 = jnp.min(jnp.where(el2 == em2, lane, big), axis=-1, keepdims=True)
    es = jnp.sum(jnp.exp(el - em), axis=-1, keepdims=True)
    p1 = 1.0 / es
    p2 = jnp.exp(em2 - em) / es
    den = p1 + p2
    rg_ref[0] = jnp.where(lane == 0, g_val * p1 / den, jnp.where(lane == 1, g_val * p2 / den, 0.0))
    oh1 = lane == i1
    oh2 = lane == i2
    both = jnp.where(jnp.logical_or(oh1, oh2), 1.0, 0.0)
    ts = both.shape[0]
    tri = (lax.broadcasted_iota(jnp.int32, (ts, ts), 0) > lax.broadcasted_iota(jnp.int32, (ts, ts), 1))
    pre = _dot(jnp.where(tri, 1.0, 0.0).astype(BF16), both.astype(BF16)) + carry[...]
    r1 = jnp.sum(jnp.where(oh1, pre, 0.0), axis=-1, keepdims=True)
    r2 = jnp.sum(jnp.where(oh2, pre, 0.0), axis=-1, keepdims=True)
    re_ref[0] = jnp.where(lane == 0, i1 - N_GROUPS, jnp.where(lane == 1, i2 - N_GROUPS, jnp.where(
        lane == 2, r1, jnp.where(lane == 3, r2, 0.0)))).astype(jnp.int32)
    carry[...] += jnp.sum(both, axis=0, keepdims=True)
    cnt_ref[...] = carry[...]


def _router_call(x, gain, sc, sh, wr, br, ts):
    b, s, d = x.shape
    return pl.pallas_call(
        _router_kernel,
        grid=(b, s // ts),
        in_specs=[_tok(d, ts), _full(gain), _modspec(sc, ts), _modspec(sh, ts), _full(wr), _full(br)],
        out_specs=[_tok(d // 2, ts), _tok(LANES, ts), _tok(LANES, ts), pl.BlockSpec((1, LANES), lambda bb, j: (0, 0))],
        out_shape=[jax.ShapeDtypeStruct((b, s, d // 2), jnp.uint32),
                   jax.ShapeDtypeStruct((b, s, LANES), jnp.int32),
                   jax.ShapeDtypeStruct((b, s, LANES), F32),
                   jax.ShapeDtypeStruct((1, LANES), F32)],
        scratch_shapes=[pltpu.VMEM((1, LANES), F32)],
        compiler_params=_cparams("arbitrary", "arbitrary"),
        name="moe_router",
    )(x, gain, sc, sh, wr, br)


def _row_copy(src, dst, src_row, dst_row, sem):
    return pltpu.make_async_copy(src.at[pl.ds(src_row, 1)], dst.at[pl.ds(dst_row, 1)], sem)


def _dispatch_kernel(dest_ref, h_ref, xin_hbm, xb_hbm, sem, *, tb):
    del xin_hbm

    def issue(r, c):
        _row_copy(h_ref, xb_hbm, r, dest_ref[0, 0, 2 * r], sem).start()
        _row_copy(h_ref, xb_hbm, r, dest_ref[0, 0, 2 * r + 1], sem).start()
        return c
    lax.fori_loop(0, tb, issue, 0)

    def drain(r, c):
        _row_copy(h_ref, xb_hbm, r, 0, sem).wait()
        _row_copy(h_ref, xb_hbm, r, 0, sem).wait()
        return c
    lax.fori_loop(0, tb, drain, 0)


def _dispatch_call(dest, h2, xb0, tb):
    n, dh = h2.shape
    nb = n // tb
    return pl.pallas_call(
        functools.partial(_dispatch_kernel, tb=tb),
        grid=(nb,),
        in_specs=[pl.BlockSpec((1, 1, 2 * tb), lambda j: (j, 0, 0), memory_space=pltpu.SMEM),
                  pl.BlockSpec((tb, dh), lambda j: (j, 0)),
                  pl.BlockSpec(memory_space=pl.ANY)],
        out_specs=pl.BlockSpec(memory_space=pl.ANY),
        out_shape=jax.ShapeDtypeStruct(xb0.shape, xb0.dtype),
        scratch_shapes=[pltpu.SemaphoreType.DMA(())],
        input_output_aliases={2: 0},
        compiler_params=_cparams("arbitrary"),
        name="moe_dispatch",
    )(dest.reshape(nb, 1, 2 * tb), h2, xb0)


def _ffn_kernel(be_ref, nu_ref, x_ref, w1_ref, w3_ref, w2_ref, y_ref):
    i = pl.program_id(0)
    half = w1_ref.shape[1] // 2

    @pl.when(i < nu_ref[0])
    def _():
        lo, hi = _unpack_halves(x_ref[...])
        lo = lo.astype(BF16)
        hi = hi.astype(BF16)
        a = _dot(lo, w1_ref[0, :half]) + _dot(hi, w1_ref[0, half:])
        g = _dot(lo, w3_ref[0, :half]) + _dot(hi, w3_ref[0, half:])
        mid = (a * jax.nn.sigmoid(a) * g).astype(BF16)
        y_ref[...] = _pack_halves(_dot(mid, w2_ref[0]))

    @pl.when(i >= nu_ref[0])
    def _():
        y_ref[...] = jnp.zeros_like(y_ref)


def _ffn_call(block_e, n_used, xb, w1, w3, w2):
    n_blocks = block_e.shape[0]
    dh = xb.shape[1]
    d = 2 * dh
    grid_spec = pltpu.PrefetchScalarGridSpec(
        num_scalar_prefetch=2,
        grid=(n_blocks,),
        in_specs=[pl.BlockSpec((MOE_BLOCK, dh), lambda i, be, nu: (i, 0)),
                  pl.BlockSpec((1, d, D_EXPERT), lambda i, be, nu: (be[i], 0, 0)),
                  pl.BlockSpec((1, d, D_EXPERT), lambda i, be, nu: (be[i], 0, 0)),
                  pl.BlockSpec((1, D_EXPERT, d), lambda i, be, nu: (be[i], 0, 0))],
        out_specs=pl.BlockSpec((MOE_BLOCK, dh), lambda i, be, nu: (i, 0)),
    )
    return pl.pallas_call(
        _ffn_kernel,
        grid_spec=grid_spec,
        out_shape=jax.ShapeDtypeStruct((n_blocks * MOE_BLOCK, dh), jnp.uint32),
        compiler_params=_cparams("arbitrary"),
        name="moe_ffn",
    )(block_e, n_used, xb, w1, w3, w2)


def _combine_kernel(dcur_ref, dnxt_ref, yb_hbm, x_ref, rg_ref, g2_ref, fin_ref, o_ref, b0, b1, sem, *, tb, final):
    j = pl.program_id(0)
    nb = pl.num_programs(0)
    slot = j % 2

    def fetch(dref, sl):
        def issue(r, c):
            _row_copy(yb_hbm, b0.at[sl], dref[0, 0, 2 * r], r, sem.at[sl]).start()
            _row_copy(yb_hbm, b1.at[sl], dref[0, 0, 2 * r + 1], r, sem.at[sl]).start()
            return c
        lax.fori_loop(0, tb, issue, 0)

    @pl.when(j == 0)
    def _():
        fetch(dcur_ref, 0)

    @pl.when(j + 1 < nb)
    def _():
        fetch(dnxt_ref, 1 - slot)

    def drain(r, c):
        _row_copy(yb_hbm, b0.at[slot], 0, r, sem.at[slot]).wait()
        _row_copy(yb_hbm, b1.at[slot], 0, r, sem.at[slot]).wait()
        return c
    lax.fori_loop(0, tb, drain, 0)
    rg = rg_ref[...]
    lo0, hi0 = _unpack_halves(b0[slot])
    lo1, hi1 = _unpack_halves(b1[slot])
    g0 = rg[:, 0:1]
    g1 = rg[:, 1:2]
    y = jnp.concatenate([g0 * lo0 + g1 * lo1, g0 * hi0 + g1 * hi1], axis=1)
    xn = x_ref[...] + g2_ref[0] * y
    if final:
        xn = _rms(xn, fin_ref[...])
    o_ref[...] = xn


def _combine_call(dest, yb, x, rg, g2, fin, tb, final):
    b, s, d = x.shape
    n = b * s
    nb = n // tb
    npb = s // tb
    dh = yb.shape[1]
    if g2.shape[1] == 1:
        gspec = pl.BlockSpec((1, 1, d), lambda j: (j // npb, 0, 0))
    else:
        gspec = pl.BlockSpec((1, tb, d), lambda j: (j // npb, j % npb, 0))
    dblk = dest.reshape(nb, 1, 2 * tb)
    out = pl.pallas_call(
        functools.partial(_combine_kernel, tb=tb, final=final),
        grid=(nb,),
        in_specs=[pl.BlockSpec((1, 1, 2 * tb), lambda j: (j, 0, 0), memory_space=pltpu.SMEM),
                  pl.BlockSpec((1, 1, 2 * tb), lambda j: (jnp.minimum(j + 1, nb - 1), 0, 0), memory_space=pltpu.SMEM),
                  pl.BlockSpec(memory_space=pl.ANY),
                  pl.BlockSpec((tb, d), lambda j: (j, 0)), pl.BlockSpec((tb, LANES), lambda j: (j, 0)),
                  gspec, _full(fin)],
        out_specs=pl.BlockSpec((tb, d), lambda j: (j, 0)),
        out_shape=jax.ShapeDtypeStruct((n, d), F32),
        scratch_shapes=[pltpu.VMEM((2, tb, dh), jnp.uint32), pltpu.VMEM((2, tb, dh), jnp.uint32),
                        pltpu.SemaphoreType.DMA((2,))],
        compiler_params=_cparams("arbitrary"),
        name="moe_combine",
    )(dblk, dblk, yb, x.reshape(n, d), rg.reshape(n, LANES), g2, fin)
    return out.reshape(b, s, d)


def _moe(x, gain, sc, sh, g2, mp, fin, ts, final):
    b, s, d = x.shape
    n = b * s
    h2, r_e, r_g, cnt = _router_call(x, gain, sc, sh, mp["wr"], mp["br"], ts)
    counts = cnt[0, N_GROUPS:N_GROUPS + N_EXPERTS].astype(jnp.int32)
    padded = (counts + MOE_BLOCK - 1) // MOE_BLOCK * MOE_BLOCK
    pad_end = jnp.cumsum(padded)
    pad_start = pad_end - padded
    dest = (pad_start[r_e[:, :, :TOP_K]] + r_e[:, :, TOP_K:2 * TOP_K]).reshape(n * TOP_K)
    n_blocks = (n * TOP_K + N_EXPERTS * (MOE_BLOCK - 1) + MOE_BLOCK - 1) // MOE_BLOCK
    block_e = jnp.minimum(jnp.searchsorted(pad_end, jnp.arange(n_blocks, dtype=jnp.int32) * MOE_BLOCK, side="right"),
                          N_EXPERTS - 1).astype(jnp.int32)
    n_used = (pad_end[-1:] // MOE_BLOCK).astype(jnp.int32)
    xb0 = jnp.zeros((n_blocks * MOE_BLOCK, d // 2), jnp.uint32)
    xb = _dispatch_call(dest, h2.reshape(n, d // 2), xb0, ts)
    yb = _ffn_call(block_e, n_used, xb, mp["w1"], mp["w3"], mp["w2"])
    return _combine_call(dest, yb, x, r_g, g2, fin, ts, final)


def _prep_ab(l, p):
    w_in = p["w_in_ab"][l]
    o2 = S5_WIDTH + MLA_Q_RANK + MLA_KV_RANK
    half = MLA_ROPE // 2
    d = w_in.shape[0]
    kr = w_in[:, o2:]
    kr_rot = jnp.concatenate([-kr[:, half:], kr[:, :half]], axis=1)
    z64 = jnp.zeros((d, MLA_NOPE), F32)
    z32 = jnp.zeros((d, LANES - MLA_NOPE - MLA_ROPE), F32)
    w_ext = jnp.concatenate([w_in[:, :o2], z64, kr, z32, z64, kr_rot, z32], axis=1)
    qu = p["mla_q_up"][l].reshape(MLA_Q_RANK, MLA_HEADS, MLA_NOPE + MLA_ROPE)
    qn, qr = qu[:, :, :MLA_NOPE], qu[:, :, MLA_NOPE:]
    zq = jnp.zeros((MLA_Q_RANK, MLA_HEADS, LANES - MLA_NOPE - MLA_ROPE), F32)
    wqa = jnp.concatenate([qn, qr, zq], axis=2).reshape(MLA_Q_RANK, MLA_HEADS * LANES)
    wqb = jnp.concatenate([jnp.zeros_like(qn), -qr[:, :, half:], qr[:, :, :half], zq], axis=2)
    wqb = wqb.reshape(MLA_Q_RANK, MLA_HEADS * LANES)
    kvu = p["mla_kv_up"][l].reshape(MLA_KV_RANK, MLA_HEADS, MLA_NOPE + MLA_V)
    wk = jnp.concatenate([kvu[:, :, :MLA_NOPE], jnp.zeros((MLA_KV_RANK, MLA_HEADS, LANES - MLA_NOPE), F32)], axis=2)
    wo = p["w_out_ab"][l]
    out = dict(
        w_in=w_ext.astype(BF16), q_norm=p["mla_q_norm"][l].reshape(1, -1), kv_norm=p["mla_kv_norm"][l].reshape(1, -1),
        wqa=wqa.astype(BF16), wqb=wqb.astype(BF16),
        wk=wk.reshape(MLA_KV_RANK, MLA_HEADS * LANES).astype(BF16),
        wv=jnp.concatenate([kvu[:, :, MLA_NOPE:], jnp.zeros((MLA_KV_RANK, MLA_HEADS, LANES - MLA_V), F32)],
                           axis=2).reshape(MLA_KV_RANK, MLA_HEADS * LANES).astype(BF16),
        s5_d=p["s5_d"][l].reshape(1, -1), glu_w=p["s5_glu_w"][l].astype(BF16), glu_b=p["s5_glu_b"][l].reshape(1, -1),
        w_out_s5=wo[:S5_WIDTH].astype(BF16),
        w_out_at=wo[S5_WIDTH:].astype(BF16))
    out.update(_s5_mats(p["s5_a_re"][l], p["s5_a_im"][l], p["s5_b_re"][l], p["s5_b_im"][l],
                        p["s5_c_re"][l], p["s5_c_im"][l], p["s5_log_dt"][l]))
    return out


def _prep_c(l, p):
    w = p["w_in_c"][l]
    pad = jnp.zeros((w.shape[0], LANES - 2 * GDN_HEADS), F32)

    def row(v):
        return jnp.concatenate([v, jnp.zeros((LANES - GDN_HEADS,), F32)]).reshape(1, LANES)

    return dict(w_in=jnp.concatenate([w, pad], axis=1).astype(BF16), conv_w=p["conv_w"][l],
                alog=row(p["gdn_a_log"][l]), dtb=row(p["gdn_dt_bias"][l]),
                gn=p["gdn_norm"][l].reshape(1, -1), w_out=p["w_out_c"][l].astype(BF16))


def _prep_moe(layer, p):
    d = p["moe_w_group"].shape[1]
    wr = jnp.concatenate([p["moe_w_group"][layer], p["moe_w_expert"][layer],
                          jnp.zeros((d, LANES - N_GROUPS - N_EXPERTS), F32)], axis=1)
    br = jnp.concatenate([p["moe_b_group"][layer], p["moe_b_expert"][layer],
                          jnp.zeros((LANES - N_GROUPS - N_EXPERTS,), F32)]).reshape(1, LANES)
    return dict(wr=wr, br=br, w1=p["moe_w1"][layer].astype(BF16), w3=p["moe_w3"][layer].astype(BF16),
                w2=p["moe_w2"][layer].astype(BF16))


def _rope_tables(pos):
    half = MLA_ROPE // 2
    inv = ROPE_THETA ** (-jnp.arange(half, dtype=F32) / half)
    ang = pos.astype(F32)[:, None] * inv[None, :]
    cos, sin = jnp.cos(ang), jnp.sin(ang)
    n = pos.shape[0]
    z64 = jnp.zeros((n, MLA_NOPE), F32)
    z32 = jnp.zeros((n, LANES - MLA_NOPE - MLA_ROPE), F32)
    scale = (MLA_NOPE + MLA_ROPE) ** -0.5 * math.log2(math.e)
    ck = jnp.concatenate([z64, cos, cos, z32], axis=1)
    sk = jnp.concatenate([z64, sin, sin, z32], axis=1)
    cq = jnp.concatenate([jnp.ones_like(z64), cos, cos, z32], axis=1) * scale
    return cq, sk * scale, ck, sk


def _trunk(x, mods, caches, wab, wc, wmoe, p, flat):
    b, s, d = x.shape
    lat_c, rope_c, s5re_c, s5im_c, conv_c, gdn_c = caches
    past = 0 if lat_c is None else lat_c.shape[2]
    pos = past + jnp.arange(s)
    if flat:
        xt = x.reshape(1, b * s, d)
        ts = b * s
        pos_rows = jnp.tile(pos, b)

        def mrow(m):
            return jnp.repeat(m, s, axis=0).reshape(1, b * s, d)
    else:
        xt = x
        ts = 256
        pos_rows = pos

        def mrow(m):
            return m.reshape(b, 1, d)
    bt, st = xt.shape[:2]
    outs = {}
    depth = p["norm_mix"].shape[0]
    for layer in range(depth):
        l = layer // 2
        sh1, sc1, g1, sh2, sc2, g2 = [mrow(m) for m in jnp.split(mods[layer], 6, axis=-1)]
        gain = p["norm_mix"][layer].reshape(1, d)
        if layer % 2 == 0:
            wp = wab[l]
            u, q, lat, kr = _ab_in_call(xt, gain, sc1, sh1, wp, _rope_tables(pos_rows), ts)
            lat_b = lat.reshape(b, s, MLA_KV_RANK)
            kr_b = kr.reshape(b, s, LANES)
            outs["lat"] = lat_b
            outs["krope"] = kr_b[:, :, MLA_NOPE:MLA_NOPE + MLA_ROPE]
            if lat_c is None:
                lat_all, kr_all, sk_valid, tkv, tq, tk = lat_b, kr_b, s, 256, min(512, s), min(512, s)
            else:
                krc = jnp.pad(rope_c[l], ((0, 0), (0, 0), (MLA_NOPE, LANES - MLA_NOPE - MLA_ROPE)))
                sk_valid = past + s
                tkv = 128
                skp = (sk_valid + tkv - 1) // tkv * tkv
                lat_all = jnp.pad(jnp.concatenate([lat_c[l], lat_b], axis=1), ((0, 0), (0, skp - sk_valid), (0, 0)))
                kr_all = jnp.pad(jnp.concatenate([krc, kr_b], axis=1), ((0, 0), (0, skp - sk_valid), (0, 0)))
                tq, tk = s, skp
            kk, vv = _kv_call(lat_all, kr_all, wp, tkv)
            qh = q.reshape(MLA_HEADS, b, s, LANES).transpose(1, 0, 2, 3) if flat else q
            attn = _attn_call(qh, kk, vv, tq, tk, past, sk_valid).reshape(bt, st, MLA_HEADS * MLA_V)
            nchk = s // S5_CHUNK
            r = b * nchk
            uc = u.reshape(b, nchk, S5_CHUNK, S5_GROUPS, S5_GROUP).transpose(0, 1, 3, 2, 4)
            uc = uc.reshape(r, S5_GROUPS * S5_CHUNK * S5_GROUP).astype(BF16)
            tr = min(r, 512)
            wre, wim = _s5_w_call(uc, wp["bre"], wp["bim"], tr)
            ngp = S5_GROUPS * S5_STATE

            def to_pairs(v):
                return v.reshape(b, 1, ngp)

            if s5re_c is None:
                x0re = jnp.zeros((b, 1, ngp), F32)
                x0im = jnp.zeros((b, 1, ngp), F32)
            else:
                x0re, x0im = to_pairs(s5re_c[l]), to_pairs(s5im_c[l])
            nchp = (nchk + 7) // 8 * 8
            tc = min(nchp, 128)
            wre3 = jnp.pad(wre.reshape(b, nchk, ngp), ((0, 0), (0, nchp - nchk), (0, 0)))
            wim3 = jnp.pad(wim.reshape(b, nchk, ngp), ((0, 0), (0, nchp - nchk), (0, 0)))
            xere, xeim = _s5_scan_call(wre3, wim3, wp["lre"], wp["lim"], x0re, x0im, tc)
            outs["s5re"] = xere[:, nchk - 1].reshape(b, S5_GROUPS, S5_STATE)
            outs["s5im"] = xeim[:, nchk - 1].reshape(b, S5_GROUPS, S5_STATE)
            xsre = jnp.concatenate([x0re, xere[:, :nchk - 1]], axis=1).reshape(r, ngp)
            xsim = jnp.concatenate([x0im, xeim[:, :nchk - 1]], axis=1).reshape(r, ngp)
            yc = _s5_y_call(uc, xsre, xsim, wp["tm"], wp["cre"], wp["cim"], tr)
            ys = yc.reshape(b, nchk, S5_GROUPS, S5_CHUNK, S5_GROUP).transpose(0, 1, 3, 2, 4).reshape(bt, st, S5_WIDTH)
            xt = _ab_out_call(xt, ys, u, attn, wp, g1, ts)
        else:
            wp = wc[l]
            qkv, z, ab = _c_in_call(xt, gain, sc1, sh1, wp["w_in"], ts)
            qkv_b = qkv.reshape(b, s, GDN_QKV)
            outs["conv"] = qkv_b[:, s - (CONV_W - 1):].astype(F32)
            if conv_c is None:
                st8 = jnp.zeros((b, 8, GDN_QKV), BF16)
                s0 = jnp.zeros((b, GDN_HEADS, GDN_DK, GDN_DV), F32)
            else:
                st8 = jnp.pad(conv_c[l], ((0, 0), (8 - (CONV_W - 1), 0), (0, 0))).astype(BF16)
                s0 = gdn_c[l]
            lg = min(s, CHUNK)
            wq, um, kgt, pm, ee = _gdn_prep_call(qkv_b, st8, wp["conv_w"], ab.reshape(b, s, LANES),
                                                 wp["alog"], wp["dtb"], lg)
            o, sfin = _gdn_seq_call(wq, um, kgt, pm, ee, s0, lg)
            outs["gdn"] = sfin
            xt = _c_out_call(xt, o.reshape(bt, st, GDN_VW), z, wp["gn"], wp["w_out"], g1, ts)
        final = layer == depth - 1
        xt = _moe(xt, p["norm_ffn"][layer].reshape(1, d), sc2, sh2, g2, wmoe[layer],
                  p["norm_final"].reshape(1, d), ts, final)
    y = xt.reshape(b, s, d)
    return (y, outs["lat"][None], outs["krope"][None], outs["s5re"][None], outs["s5im"][None],
            outs["conv"][None], outs["gdn"][None])


def kernel(x_prompt, x_sample, c_prompt, c_sample, cache_mla_latent, cache_mla_krope, state_s5_re, state_s5_im,
           state_conv, state_gdn, w_mod, b_mod, norm_mix, norm_ffn, norm_final, w_in_ab, s5_a_re, s5_a_im,
           s5_b_re, s5_b_im, s5_c_re, s5_c_im, s5_d, s5_log_dt, s5_glu_w, s5_glu_b, mla_q_norm, mla_q_up,
           mla_kv_norm, mla_kv_up, w_out_ab, w_in_c, conv_w, gdn_a_log, gdn_dt_bias, gdn_norm, w_out_c,
           moe_w_group, moe_b_group, moe_w_expert, moe_b_expert, moe_w1, moe_w3, moe_w2):
    p = dict(w_mod=w_mod, b_mod=b_mod, norm_mix=norm_mix, norm_ffn=norm_ffn, norm_final=norm_final,
             w_in_ab=w_in_ab, s5_a_re=s5_a_re, s5_a_im=s5_a_im, s5_b_re=s5_b_re, s5_b_im=s5_b_im,
             s5_c_re=s5_c_re, s5_c_im=s5_c_im, s5_d=s5_d, s5_log_dt=s5_log_dt, s5_glu_w=s5_glu_w,
             s5_glu_b=s5_glu_b, mla_q_norm=mla_q_norm, mla_q_up=mla_q_up, mla_kv_norm=mla_kv_norm,
             mla_kv_up=mla_kv_up, w_out_ab=w_out_ab, w_in_c=w_in_c, conv_w=conv_w, gdn_a_log=gdn_a_log,
             gdn_dt_bias=gdn_dt_bias, gdn_norm=gdn_norm, w_out_c=w_out_c, moe_w_group=moe_w_group,
             moe_b_group=moe_b_group, moe_w_expert=moe_w_expert, moe_b_expert=moe_b_expert,
             moe_w1=moe_w1, moe_w3=moe_w3, moe_w2=moe_w2)
    depth = norm_mix.shape[0]
    bp, bs = c_prompt.shape[0], c_sample.shape[0]
    nb = (bp + bs + 7) // 8 * 8
    c_all = jnp.pad(jnp.concatenate([c_prompt, c_sample], axis=0), ((0, nb - bp - bs), (0, 0)))
    mods = _mod_call(c_all, w_mod, b_mod)
    wab = [_prep_ab(l, p) for l in range((depth + 1) // 2)]
    wc = [_prep_c(l, p) for l in range(depth // 2)]
    wmoe = [_prep_moe(layer, p) for layer in range(depth)]
    none6 = (None,) * 6
    outp = _trunk(x_prompt, mods[:, :bp], none6, wab, wc, wmoe, p, flat=False)
    caches = (cache_mla_latent, cache_mla_krope, state_s5_re, state_s5_im, state_conv, state_gdn)
    outs = _trunk(x_sample, mods[:, bp:bp + bs], caches, wab, wc, wmoe, p, flat=True)
    return (outp[0], outs[0]) + tuple(outp[1:]) + tuple(outs[1:])
```

```python
import functools
import math

import jax
import jax.numpy as jnp
from jax import lax
from jax.experimental import pallas as pl
from jax.experimental.pallas import tpu as pltpu

F32 = jnp.float32
BF16 = jnp.bfloat16
HI = lax.Precision.HIGHEST
EPS = 1e-6

D_MODEL = 1024
CHUNK = 64
S5_WIDTH = 512
S5_GROUP = 16
S5_GROUPS = 32
S5_STATE = 64
S5_CHUNK = 16
S5_LB = 128 // S5_GROUP
MLA_HEADS = 8
MLA_NOPE = 64
MLA_ROPE = 32
MLA_V = 64
MLA_Q_RANK = 384
MLA_KV_RANK = 256
ROPE_THETA = 10000.0
GDN_HEADS = 8
GDN_DK = 128
GDN_DV = 128
CONV_W = 4
GDN_QKV = GDN_HEADS * (2 * GDN_DK + GDN_DV)
GDN_VW = GDN_HEADS * GDN_DV
N_GROUPS = 4
EXPERTS_PER_GROUP = 8
N_EXPERTS = 32
D_EXPERT = 512
TOP_K = 2
MOE_BLOCK = 128
ATTN_SUB = 256
LANES = 128
VMEM_LIMIT = 48 * 1024 * 1024


def _cparams(*sem):
    return pltpu.CompilerParams(dimension_semantics=sem, vmem_limit_bytes=VMEM_LIMIT)


def _dot(a, b):
    return jnp.dot(a, b, preferred_element_type=F32)


def _dot_nt(a, b, precision=None):
    return lax.dot_general(a, b, (((1,), (1,)), ((), ())), preferred_element_type=F32, precision=precision)


def _full(arr):
    nd = arr.ndim
    return pl.BlockSpec(arr.shape, lambda *_: (0,) * nd)


def _tok(width, ts):
    return pl.BlockSpec((1, ts, width), lambda b, j: (b, j, 0))


def _modspec(arr, ts):
    if arr.shape[1] == 1:
        return pl.BlockSpec((1, 1, arr.shape[2]), lambda b, j: (b, 0, 0))
    return pl.BlockSpec((1, ts, arr.shape[2]), lambda b, j: (b, j, 0))


def _normmod(x, gain, sc, sh):
    ms = jnp.mean(x * x, axis=-1, keepdims=True)
    return x * lax.rsqrt(ms + EPS) * gain * (1.0 + sc) + sh


def _rms(x, gain):
    return x * lax.rsqrt(jnp.mean(x * x, axis=-1, keepdims=True) + EPS) * gain


def _mod_kernel(c_ref, w_ref, b_ref, o_ref):
    c = c_ref[...]
    a = (c * jax.nn.sigmoid(c)).astype(BF16)
    o_ref[0] = _dot(a, w_ref[0]) + b_ref[0]


def _mod_call(c_all, w_mod, b_mod):
    depth, d, n6 = w_mod.shape
    bp = c_all.shape[0]
    tn = 1536
    return pl.pallas_call(
        _mod_kernel,
        grid=(depth, n6 // tn),
        in_specs=[pl.BlockSpec((bp, d), lambda l, n: (0, 0)),
                  pl.BlockSpec((1, d, tn), lambda l, n: (l, 0, n)),
                  pl.BlockSpec((1, 1, tn), lambda l, n: (l, 0, n))],
        out_specs=pl.BlockSpec((1, bp, tn), lambda l, n: (l, 0, n)),
        out_shape=jax.ShapeDtypeStruct((depth, bp, n6), F32),
        compiler_params=_cparams("parallel", "parallel"),
        name="mod",
    )(c_all, w_mod.astype(BF16), b_mod.reshape(depth, 1, n6))


def _ab_in_kernel(x_ref, gain_ref, sc_ref, sh_ref, w_ref, qg_ref, wqa_ref, wqb_ref, kvg_ref,
                  cq_ref, sq_ref, ck_ref, sk_ref, u_ref, q_ref, lat_ref, kr_ref):
    h = _normmod(x_ref[0], gain_ref[...], sc_ref[0], sh_ref[0]).astype(BF16)
    proj = _dot(h, w_ref[...])
    u_ref[0] = proj[:, :S5_WIDTH]
    o1 = S5_WIDTH + MLA_Q_RANK
    o2 = o1 + MLA_KV_RANK
    qn = _rms(proj[:, S5_WIDTH:o1], qg_ref[...]).astype(BF16)
    qa = _dot(qn, wqa_ref[...])
    qb = _dot(qn, wqb_ref[...])
    cq = cq_ref[...]
    sq = sq_ref[...]
    for hh in range(MLA_HEADS):
        sl = slice(LANES * hh, LANES * (hh + 1))
        q_ref[0, hh] = (qa[:, sl] * cq + qb[:, sl] * sq).astype(BF16)
    lat_ref[0] = _rms(proj[:, o1:o2], kvg_ref[...])
    kr_ref[0] = proj[:, o2:o2 + LANES] * ck_ref[...] + proj[:, o2 + LANES:o2 + 2 * LANES] * sk_ref[...]


def _ab_in_call(x, gain, sc, sh, wp, tabs, ts):
    b, s, d = x.shape
    cq, sq, ck, sk = tabs
    tab = pl.BlockSpec((ts, LANES), lambda bb, j: (j, 0))
    ins = [x, gain, sc, sh, wp["w_in"], wp["q_norm"], wp["wqa"], wp["wqb"], wp["kv_norm"], cq, sq, ck, sk]
    specs = [_tok(d, ts), _full(gain), _modspec(sc, ts), _modspec(sh, ts), _full(wp["w_in"]), _full(wp["q_norm"]),
             _full(wp["wqa"]), _full(wp["wqb"]), _full(wp["kv_norm"]), tab, tab, tab, tab]
    return pl.pallas_call(
        _ab_in_kernel,
        grid=(b, s // ts),
        in_specs=specs,
        out_specs=[_tok(S5_WIDTH, ts),
                   pl.BlockSpec((1, MLA_HEADS, ts, LANES), lambda bb, j: (bb, 0, j, 0)),
                   _tok(MLA_KV_RANK, ts), _tok(LANES, ts)],
        out_shape=[jax.ShapeDtypeStruct((b, s, S5_WIDTH), F32),
                   jax.ShapeDtypeStruct((b, MLA_HEADS, s, LANES), BF16),
                   jax.ShapeDtypeStruct((b, s, MLA_KV_RANK), F32),
                   jax.ShapeDtypeStruct((b, s, LANES), F32)],
        compiler_params=_cparams("parallel", "parallel"),
        name="ab_in",
    )(*ins)


def _kv_kernel(lat_ref, kr_ref, wk_ref, wv_ref, k_ref, v_ref, *, transposed):
    lat = lat_ref[0].astype(BF16)
    kk = _dot(lat, wk_ref[...])
    kr = kr_ref[0]
    for hh in range(MLA_HEADS):
        k_ref[0, hh] = (kk[:, LANES * hh:LANES * (hh + 1)] + kr).astype(BF16)
    if transposed:
        ones = (lax.broadcasted_iota(jnp.int32, (LANES, 1), 0) >= MLA_V).astype(F32)
        for hh in range(MLA_HEADS):
            v_ref[0, hh] = (_dot_nt(wv_ref[hh], lat) + ones).astype(BF16)
    else:
        vv = _dot(lat, wv_ref[...])
        ones = (lax.broadcasted_iota(jnp.int32, (1, LANES), 1) >= MLA_V).astype(F32)
        for hh in range(MLA_HEADS):
            v_ref[0, hh] = (vv[:, LANES * hh:LANES * (hh + 1)] + ones).astype(BF16)


def _kv_call(lat_all, kr_all, wp, ts, transposed):
    b, sk, _ = lat_all.shape
    hspec = pl.BlockSpec((1, MLA_HEADS, ts, LANES), lambda bb, j: (bb, 0, j, 0))
    hshape = jax.ShapeDtypeStruct((b, MLA_HEADS, sk, LANES), BF16)
    if transposed:
        wv = wp["wv_t"]
        vspec = pl.BlockSpec((1, MLA_HEADS, LANES, ts), lambda bb, j: (bb, 0, 0, j))
        vshape = jax.ShapeDtypeStruct((b, MLA_HEADS, LANES, sk), BF16)
    else:
        wv, vspec, vshape = wp["wv"], hspec, hshape
    return pl.pallas_call(
        functools.partial(_kv_kernel, transposed=transposed),
        grid=(b, sk // ts),
        in_specs=[_tok(MLA_KV_RANK, ts), _tok(LANES, ts), _full(wp["wk"]), _full(wv)],
        out_specs=[hspec, vspec],
        out_shape=[hshape, vshape],
        compiler_params=_cparams("parallel", "parallel"),
        name="kv_up",
    )(lat_all, kr_all, wp["wk"], wv)


def _attn_kernel(q_ref, k_ref, v_ref, o_ref, *, tq, tk, q_off, sk_valid):
    i = pl.program_id(2)
    nsub = max(tq // ATTN_SUB, 1)
    tqs = tq // nsub
    chains = [(hh, qi) for hh in range(2) for qi in range(nsub)]
    qs = [q_ref[0, hh, qi * tqs:(qi + 1) * tqs, :] for hh, qi in chains]
    q0 = q_off + i * tq
    lim_full = (q0 // CHUNK + 1) * CHUNK
    lim_tot = jnp.minimum(((q0 + tq - 1) // CHUNK + 1) * CHUNK, sk_valid)
    n_full = jnp.minimum(lim_full // tk, sk_valid // tk)
    n_tot = (lim_tot + tk - 1) // tk

    def step(j, carry, masked):
        off = pl.multiple_of(j * tk, tk)
        nc = range(len(chains))
        if masked:
            kpos = off + lax.broadcasted_iota(jnp.int32, (tqs, tk), 1)
            ok = []
            for qi in range(nsub):
                qpos = q0 + qi * tqs + lax.broadcasted_iota(jnp.int32, (tqs, tk), 0)
                ok.append(jnp.logical_and(kpos // CHUNK <= qpos // CHUNK, kpos < sk_valid))
        kt = [k_ref[0, hh, pl.ds(off, tk), :] for hh in range(2)]
        vt = [v_ref[0, hh, pl.ds(off, tk), :] for hh in range(2)]
        s = [_dot_nt(qs[c], kt[chains[c][0]]) for c in nc]
        if masked:
            s = [jnp.where(ok[chains[c][1]], s[c], -1e30) for c in nc]
        m_new = [jnp.maximum(carry[c][0], jnp.max(s[c], axis=-1, keepdims=True)) for c in nc]
        alpha = [jnp.exp2(carry[c][0] - m_new[c]) for c in nc]
        p = [jnp.exp2(s[c] - m_new[c]).astype(BF16) for c in nc]
        acc = [alpha[c] * carry[c][1] + _dot(p[c], vt[chains[c][0]]) for c in nc]
        return tuple((m_new[c], acc[c]) for c in nc)

    one = (jnp.full((tqs, 1), -1e30, F32), jnp.zeros((tqs, LANES), F32))
    carry = lax.fori_loop(0, n_full, functools.partial(step, masked=False), (one,) * len(chains))
    carry = lax.fori_loop(n_full, n_tot, functools.partial(step, masked=True), carry)
    lane = lax.broadcasted_iota(jnp.int32, (tqs, LANES), 1)
    for qi in range(nsub):
        a0 = carry[chains.index((0, qi))][1]
        a1 = carry[chains.index((1, qi))][1]
        r0 = a0 / pltpu.roll(a0, MLA_V, axis=1)
        r1 = a1 / pltpu.roll(a1, MLA_V, axis=1)
        o_ref[0, qi * tqs:(qi + 1) * tqs, :] = jnp.where(
            lane < MLA_V, r0, pltpu.roll(r1, MLA_V, axis=1)).astype(o_ref.dtype)


def _attn_call(q, k, v, tq, tk, q_off, sk_valid):
    b, h, sq, _ = q.shape
    sk = k.shape[2]
    kern = functools.partial(_attn_kernel, tq=tq, tk=tk, q_off=q_off, sk_valid=sk_valid)
    return pl.pallas_call(
        kern,
        grid=(b, h // 2, sq // tq),
        in_specs=[pl.BlockSpec((1, 2, tq, LANES), lambda bb, hp, i: (bb, hp, i, 0)),
                  pl.BlockSpec((1, 2, sk, LANES), lambda bb, hp, i: (bb, hp, 0, 0), pipeline_mode=pl.Buffered(1)),
                  pl.BlockSpec((1, 2, sk, LANES), lambda bb, hp, i: (bb, hp, 0, 0), pipeline_mode=pl.Buffered(1))],
        out_specs=pl.BlockSpec((1, tq, LANES), lambda bb, hp, i: (bb, i, hp)),
        out_shape=jax.ShapeDtypeStruct((b, sq, h * MLA_V), BF16),
        compiler_params=_cparams("parallel", "parallel", "arbitrary"),
        name="mla_attn",
    )(q, k, v)


def _attn_t_kernel(q_ref, k_ref, v_ref, o_ref, *, tq, tk, q_off, sk_valid):
    i = pl.program_id(2)
    nsub = tq // ATTN_SUB
    tqs = ATTN_SUB
    chains = [(hh, qi) for hh in range(2) for qi in range(nsub)]
    qs = [q_ref[0, hh, qi * tqs:(qi + 1) * tqs, :] for hh, qi in chains]
    q0 = q_off + i * tq
    lim_full = (q0 // CHUNK + 1) * CHUNK
    lim_tot = jnp.minimum(((q0 + tq - 1) // CHUNK + 1) * CHUNK, sk_valid)
    n_full = jnp.minimum(lim_full // tk, sk_valid // tk)
    n_tot = (lim_tot + tk - 1) // tk

    nc = range(len(chains))

    def step(j, carry, masked):
        off = pl.multiple_of(j * tk, tk)
        if masked:
            kpos = off + lax.broadcasted_iota(jnp.int32, (tk, tqs), 0)
            ok = []
            for qi in range(nsub):
                qpos = q0 + qi * tqs + lax.broadcasted_iota(jnp.int32, (tk, tqs), 1)
                ok.append(jnp.logical_and(kpos // CHUNK <= qpos // CHUNK, kpos < sk_valid))
        kt = [k_ref[0, hh, pl.ds(off, tk), :] for hh in range(2)]
        vt = [v_ref[0, hh, :, pl.ds(off, tk)] for hh in range(2)]
        s = [_dot_nt(kt[chains[c][0]], qs[c]) for c in nc]
        if masked:
            s = [jnp.where(ok[chains[c][1]], s[c], -1e30) for c in nc]
        m_new = [jnp.maximum(carry[c][0], jnp.max(s[c], axis=0, keepdims=True)) for c in nc]
        alpha = [jnp.exp2(carry[c][0] - m_new[c]) for c in nc]
        p = [jnp.exp2(s[c] - m_new[c]).astype(BF16) for c in nc]
        acc = [alpha[c] * carry[c][1] + _dot(vt[chains[c][0]], p[c]) for c in nc]
        return tuple((m_new[c], acc[c]) for c in nc)

    one = (jnp.full((1, tqs), -1e30, F32), jnp.zeros((LANES, tqs), F32))
    carry = lax.fori_loop(0, n_full, functools.partial(step, masked=False), (one,) * len(chains))
    carry = lax.fori_loop(n_full, n_tot, functools.partial(step, masked=True), carry)
    for qi in range(nsub):
        a0 = carry[chains.index((0, qi))][1]
        a1 = carry[chains.index((1, qi))][1]
        ot = jnp.concatenate([a0[:MLA_V] / a0[MLA_V:], a1[:MLA_V] / a1[MLA_V:]], axis=0)
        o_ref[0, qi * tqs:(qi + 1) * tqs, :] = ot.T.astype(o_ref.dtype)


def _attn_t_call(q, k, vt, tq, tk, q_off, sk_valid):
    b, h, sq, _ = q.shape
    sk = k.shape[2]
    kern = functools.partial(_attn_t_kernel, tq=tq, tk=tk, q_off=q_off, sk_valid=sk_valid)
    return pl.pallas_call(
        kern,
        grid=(b, h // 2, sq // tq),
        in_specs=[pl.BlockSpec((1, 2, tq, LANES), lambda bb, hp, i: (bb, hp, i, 0)),
                  pl.BlockSpec((1, 2, sk, LANES), lambda bb, hp, i: (bb, hp, 0, 0), pipeline_mode=pl.Buffered(1)),
                  pl.BlockSpec((1, 2, LANES, sk), lambda bb, hp, i: (bb, hp, 0, 0), pipeline_mode=pl.Buffered(1))],
        out_specs=pl.BlockSpec((1, tq, LANES), lambda bb, hp, i: (bb, i, hp)),
        out_shape=jax.ShapeDtypeStruct((b, sq, h * MLA_V), BF16),
        compiler_params=_cparams("parallel", "parallel", "arbitrary"),
        name="mla_attn_t",
    )(q, k, vt)


def _chunk_rows(ref, s, tr):
    return ref[pl.ds(s, tr, stride=S5_CHUNK), :]


def _s5_w_kernel(u_ref, bre_ref, bim_ref, wre_ref, wim_ref, *, tr):
    ucat = jnp.concatenate([_chunk_rows(u_ref, s, tr).astype(BF16) for s in range(S5_CHUNK)], axis=1)
    wre_ref[...] = _dot(ucat, bre_ref[0])
    wim_ref[...] = _dot(ucat, bim_ref[0])


def _s5_w_call(u2, bre, bim, tr):
    n = u2.shape[0]
    r = n // S5_CHUNK
    nq = bre.shape[0]
    sw = S5_LB * S5_STATE
    out = jax.ShapeDtypeStruct((r, nq * sw), F32)
    ospec = pl.BlockSpec((tr, sw), lambda q, i: (i, q))
    wspec = pl.BlockSpec((1, S5_CHUNK * LANES, sw), lambda q, i: (q, 0, 0))
    return pl.pallas_call(
        functools.partial(_s5_w_kernel, tr=tr),
        grid=(nq, r // tr),
        in_specs=[pl.BlockSpec((tr * S5_CHUNK, LANES), lambda q, i: (i, q)), wspec, wspec],
        out_specs=[ospec, ospec],
        out_shape=[out, out],
        compiler_params=_cparams("parallel", "parallel"),
        name="s5_chunk_in",
    )(u2, bre, bim)


def _s5_scan_kernel(wre_ref, wim_ref, lre_ref, lim_ref, x0re_ref, x0im_ref, ore_ref, oim_ref, sre, sim, *, tc):
    @pl.when(pl.program_id(1) == 0)
    def _():
        sre[...] = x0re_ref[0]
        sim[...] = x0im_ref[0]

    lr = lre_ref[...]
    li = lim_ref[...]

    def blk(t, carry):
        xr, xi = carry
        base = pl.multiple_of(t * 8, 8)
        wr = wre_ref[0, pl.ds(base, 8), :]
        wi = wim_ref[0, pl.ds(base, 8), :]
        rows_r, rows_i = [], []
        for r in range(8):
            nr = lr * xr - li * xi + wr[r:r + 1]
            ni = lr * xi + li * xr + wi[r:r + 1]
            xr, xi = nr, ni
            rows_r.append(xr)
            rows_i.append(xi)
        ore_ref[0, pl.ds(base, 8), :] = jnp.concatenate(rows_r, axis=0)
        oim_ref[0, pl.ds(base, 8), :] = jnp.concatenate(rows_i, axis=0)
        return xr, xi

    xr, xi = lax.fori_loop(0, tc // 8, blk, (sre[...], sim[...]))
    sre[...] = xr
    sim[...] = xi


def _s5_scan_call(wre, wim, lre, lim, x0re, x0im, tc):
    b, nch, n = wre.shape
    wspec = pl.BlockSpec((1, tc, n), lambda bb, c: (bb, c, 0))
    lspec = pl.BlockSpec((1, n), lambda bb, c: (0, 0))
    xspec = pl.BlockSpec((1, 1, n), lambda bb, c: (bb, 0, 0))
    out = jax.ShapeDtypeStruct((b, nch, n), F32)
    return pl.pallas_call(
        functools.partial(_s5_scan_kernel, tc=tc),
        grid=(b, nch // tc),
        in_specs=[wspec, wspec, lspec, lspec, xspec, xspec],
        out_specs=[wspec, wspec],
        out_shape=[out, out],
        scratch_shapes=[pltpu.VMEM((1, n), F32), pltpu.VMEM((1, n), F32)],
        compiler_params=_cparams("parallel", "arbitrary"),
        name="s5_scan",
    )(wre, wim, lre, lim, x0re, x0im)


def _s5_y_kernel(u_ref, xre_ref, xim_ref, m_ref, c_ref, y_ref, *, tr):
    L = S5_CHUNK
    urev = jnp.concatenate([_chunk_rows(u_ref, s, tr).astype(BF16) for s in reversed(range(L))], axis=1)
    xcat = jnp.concatenate([xre_ref[...].astype(BF16), xim_ref[...].astype(BF16)], axis=1)
    for t in range(L):
        y = _dot(urev[:, (L - 1 - t) * LANES:], m_ref[0, :(t + 1) * LANES, :]) + _dot(xcat, c_ref[0, t])
        y_ref[pl.ds(t, tr, stride=L), :] = y


def _s5_y_call(u2, xre, xim, mst, cc, tr):
    n, w = u2.shape
    nq = mst.shape[0]
    sw = S5_LB * S5_STATE
    uspec = pl.BlockSpec((tr * S5_CHUNK, LANES), lambda q, i: (i, q))
    xspec = pl.BlockSpec((tr, sw), lambda q, i: (i, q))
    return pl.pallas_call(
        functools.partial(_s5_y_kernel, tr=tr),
        grid=(nq, n // (tr * S5_CHUNK)),
        in_specs=[uspec, xspec, xspec,
                  pl.BlockSpec((1, S5_CHUNK * LANES, LANES), lambda q, i: (q, 0, 0)),
                  pl.BlockSpec((1, S5_CHUNK, 2 * sw, LANES), lambda q, i: (q, 0, 0, 0))],
        out_specs=uspec,
        out_shape=jax.ShapeDtypeStruct((n, w), F32),
        compiler_params=_cparams("parallel", "parallel"),
        name="s5_chunk_out",
    )(u2, xre, xim, mst, cc)


def _s5_mats(a_re, a_im, b_re, b_im, c_re, c_im, log_dt):
    g, p = a_re.shape
    L = S5_CHUNK
    lam = lax.complex(jnp.minimum(a_re, -1e-4), a_im)
    lamdt = lam * jnp.exp(log_dt)[:, None]
    lam_bar = jnp.exp(lamdt)
    b_bar = ((lam_bar - 1.0) / lam)[:, :, None] * lax.complex(b_re, b_im)
    cm = lax.complex(c_re, c_im)
    pw = jnp.exp(lamdt[:, :, None] * jnp.arange(L + 1, dtype=F32))
    mm = pw[:, :, :L, None] * b_bar[:, :, None, :]
    kd = jnp.sum((cm[:, :, :, None, None] * mm[:, None]).real, axis=2).transpose(0, 2, 1, 3)
    bp = pw[:, :, L - 1 - jnp.arange(L)][:, :, :, None] * b_bar[:, :, None, :]
    cp = cm[:, :, :, None] * pw[:, None, :, 1:]
    nq = g // S5_LB
    eye = jnp.eye(S5_LB, dtype=F32)

    def lane_block(m, spec, rows, cols):
        return jnp.einsum(spec, m.reshape((nq, S5_LB) + m.shape[1:]), eye).reshape(nq, rows, cols)

    sw = S5_LB * p
    bre = lane_block(bp.real, "qgpsc,hg->qshcgp", L * LANES, sw)
    bim = lane_block(bp.imag, "qgpsc,hg->qshcgp", L * LANES, sw)
    mst = lane_block(kd, "qgdoc,hg->qdhcgo", L * LANES, LANES)
    cre = lane_block(cp.real, "qgopt,hg->qthpgo", L * sw, LANES).reshape(nq, L, sw, LANES)
    cim = lane_block(-cp.imag, "qgopt,hg->qthpgo", L * sw, LANES).reshape(nq, L, sw, LANES)
    return dict(
        bre=bre.astype(BF16), bim=bim.astype(BF16), mst=mst.astype(BF16),
        cc=jnp.concatenate([cre, cim], axis=2).astype(BF16),
        lre=pw[:, :, L].real.reshape(1, g * p), lim=pw[:, :, L].imag.reshape(1, g * p))


def _ab_out_kernel(x_ref, ys_ref, u_ref, at_ref, d_ref, gw_ref, gb_ref, ws_ref, wa_ref, g1_ref, o_ref):
    y = ys_ref[0] + d_ref[...] * u_ref[0]
    y = jax.nn.gelu(y)
    gate = jax.nn.sigmoid(_dot(y.astype(BF16), gw_ref[...]) + gb_ref[...])
    out = _dot((y * gate).astype(BF16), ws_ref[...]) + _dot(at_ref[0], wa_ref[...])
    o_ref[0] = x_ref[0] + g1_ref[0] * out


def _ab_out_call(x, ys, u, attn, wp, g1, ts):
    b, s, d = x.shape
    ins = [x, ys, u, attn, wp["s5_d"], wp["glu_w"], wp["glu_b"], wp["w_out_s5"], wp["w_out_at"], g1]
    specs = [_tok(d, ts), _tok(S5_WIDTH, ts), _tok(S5_WIDTH, ts), _tok(MLA_HEADS * MLA_V, ts),
             _full(wp["s5_d"]), _full(wp["glu_w"]), _full(wp["glu_b"]), _full(wp["w_out_s5"]),
             _full(wp["w_out_at"]), _modspec(g1, ts)]
    return pl.pallas_call(
        _ab_out_kernel,
        grid=(b, s // ts),
        in_specs=specs,
        out_specs=_tok(d, ts),
        out_shape=jax.ShapeDtypeStruct((b, s, d), F32),
        compiler_params=_cparams("parallel", "parallel"),
        name="ab_out",
    )(*ins)


def _c_in_kernel(x_ref, gain_ref, sc_ref, sh_ref, w_ref, qkv_ref, z_ref, ab_ref):
    h = _normmod(x_ref[0], gain_ref[...], sc_ref[0], sh_ref[0]).astype(BF16)
    proj = _dot(h, w_ref[...])
    qkv_ref[0] = proj[:, :GDN_QKV].astype(BF16)
    z_ref[0] = proj[:, GDN_QKV:GDN_QKV + GDN_VW].astype(BF16)
    ab_ref[0] = proj[:, GDN_QKV + GDN_VW:]


def _c_in_call(x, gain, sc, sh, w, ts):
    b, s, d = x.shape
    return pl.pallas_call(
        _c_in_kernel,
        grid=(b, s // ts),
        in_specs=[_tok(d, ts), _full(gain), _modspec(sc, ts), _modspec(sh, ts), _full(w)],
        out_specs=[_tok(GDN_QKV, ts), _tok(GDN_VW, ts), _tok(LANES, ts)],
        out_shape=[jax.ShapeDtypeStruct((b, s, GDN_QKV), BF16),
                   jax.ShapeDtypeStruct((b, s, GDN_VW), BF16),
                   jax.ShapeDtypeStruct((b, s, LANES), F32)],
        compiler_params=_cparams("parallel", "parallel"),
        name="c_in",
    )(x, gain, sc, sh, w)


def _gdn_prep_kernel(qkv_ref, halo_ref, st_ref, cw_ref, ab_ref, alog_ref, dtb_ref,
                     wq_ref, u_ref, kgt_ref, pm_ref, ee_ref, *, L):
    j = pl.program_id(1)
    prev = jnp.where(j == 0, st_ref[0].astype(F32), halo_ref[0].astype(F32))
    xx = jnp.concatenate([prev, qkv_ref[0].astype(F32)], axis=0)
    cw = cw_ref[...]
    y = xx[8:8 + L] * cw[CONV_W - 1:CONV_W]
    for t in range(1, CONV_W):
        y += xx[8 - t:8 - t + L] * cw[CONV_W - 1 - t:CONV_W - t]
    y = y * jax.nn.sigmoid(y)

    ab = ab_ref[0]
    g_all = -jnp.exp(alog_ref[...]) * jax.nn.softplus(ab + dtb_ref[...])
    beta_all = jax.nn.sigmoid(ab)
    row = lax.broadcasted_iota(jnp.int32, (L, L), 0)
    col = lax.broadcasted_iota(jnp.int32, (L, L), 1)
    incl = row >= col
    strict = row > col
    gam_all = jnp.dot(incl.astype(F32), g_all, preferred_element_type=F32, precision=HI)
    eye_l = (lax.broadcasted_iota(jnp.int32, (LANES, LANES), 0)
             == lax.broadcasted_iota(jnp.int32, (LANES, LANES), 1))
    gam_t = _dot_nt(eye_l.astype(F32), gam_all, precision=HI)
    ee_ref[0, 0] = jnp.exp(gam_all[L - 1:L])
    eye64 = (row == col).astype(F32)
    eye_bf = eye_l.astype(BF16)

    hs = range(GDN_HEADS)
    koff, voff = GDN_HEADS * GDN_DK, 2 * GDN_HEADS * GDN_DK
    qh = [y[:, GDN_DK * h:GDN_DK * (h + 1)] for h in hs]
    kh = [y[:, koff + GDN_DK * h:koff + GDN_DK * (h + 1)] for h in hs]
    vh = [y[:, voff + GDN_DV * h:voff + GDN_DV * (h + 1)] for h in hs]
    qh = [q * lax.rsqrt(jnp.sum(q * q, axis=-1, keepdims=True) + EPS) * (GDN_DK ** -0.5) for q in qh]
    kh = [k * lax.rsqrt(jnp.sum(k * k, axis=-1, keepdims=True) + EPS) for k in kh]
    gam_c = [gam_all[:, h:h + 1] for h in hs]
    beta_c = [beta_all[:, GDN_HEADS + h:GDN_HEADS + h + 1] for h in hs]
    dec = [jnp.exp(jnp.minimum(gam_c[h] - gam_t[h:h + 1, :], 0.0)) for h in hs]
    kb = [k.astype(BF16) for k in kh]
    kk = [_dot_nt(kb[h], kb[h]) for h in hs]
    qk = [_dot_nt(qh[h].astype(BF16), kb[h]) for h in hs]
    a = [jnp.where(strict, beta_c[h] * kk[h] * dec[h], 0.0) for h in hs]
    for h in hs:
        pm_ref[0, 0, h] = jnp.where(incl, qk[h] * dec[h], 0.0).astype(BF16)
    tinv = [eye64 - a[h] for h in hs]
    pw = [_split(a[h]) for h in hs]
    for _ in range(max(L.bit_length() - 2, 0)):
        pw = [_split(_dot3(pw[h], pw[h])) for h in hs]
        tinv = [tinv[h] + _dot3(_split(tinv[h]), pw[h]) for h in hs]
    eg = [jnp.exp(gam_c[h]) for h in hs]
    rhs = [jnp.concatenate([(beta_c[h] * eg[h]) * kh[h], beta_c[h] * vh[h]], axis=1).astype(BF16) for h in hs]
    wu = [_dot(tinv[h].astype(BF16), rhs[h]) for h in hs]
    for h in hs:
        wq_ref[0, 0, h] = jnp.concatenate([wu[h][:, :GDN_DK], qh[h] * eg[h]], axis=0).astype(BF16)
        u_ref[0, h] = wu[h][:, GDN_DK:].astype(BF16)
    kg = [(kh[h] * jnp.exp(gam_all[L - 1:L, h:h + 1] - gam_c[h])).astype(BF16) for h in hs]
    kgt = [_dot_nt(eye_bf, kg[h]) for h in hs]
    for h in hs:
        kgt_ref[0, 0, h] = kgt[h].astype(BF16)


def _split(a):
    hi = a.astype(BF16)
    return hi, (a - hi.astype(F32)).astype(BF16)


def _dot3(a, b):
    return _dot(a[0], b[0]) + (_dot(a[0], b[1]) + _dot(a[1], b[0]))


def _gdn_prep_call(qkv, st8, conv_w, ab, alog, dtb, L):
    b, s, _ = qkv.shape
    nch = s // L
    return pl.pallas_call(
        functools.partial(_gdn_prep_kernel, L=L),
        grid=(b, nch),
        in_specs=[_tok(GDN_QKV, L),
                  pl.BlockSpec((1, 8, GDN_QKV), lambda bb, j: (bb, jnp.maximum(j * (L // 8) - 1, 0), 0)),
                  pl.BlockSpec((1, 8, GDN_QKV), lambda bb, j: (bb, 0, 0)),
                  _full(conv_w), _tok(LANES, L), _full(alog), _full(dtb)],
        out_specs=[pl.BlockSpec((1, 1, GDN_HEADS, 2 * L, LANES), lambda bb, j: (bb, j, 0, 0, 0)),
                   pl.BlockSpec((1, GDN_HEADS, L, LANES), lambda bb, j: (bb, 0, j, 0)),
                   pl.BlockSpec((1, 1, GDN_HEADS, GDN_DK, L), lambda bb, j: (bb, j, 0, 0, 0)),
                   pl.BlockSpec((1, 1, GDN_HEADS, L, L), lambda bb, j: (bb, j, 0, 0, 0)),
                   pl.BlockSpec((1, 1, 1, LANES), lambda bb, j: (bb, j, 0, 0))],
        out_shape=[jax.ShapeDtypeStruct((b, nch, GDN_HEADS, 2 * L, LANES), BF16),
                   jax.ShapeDtypeStruct((b, GDN_HEADS, s, LANES), BF16),
                   jax.ShapeDtypeStruct((b, nch, GDN_HEADS, GDN_DK, L), BF16),
                   jax.ShapeDtypeStruct((b, nch, GDN_HEADS, L, L), BF16),
                   jax.ShapeDtypeStruct((b, nch, 1, LANES), F32)],
        compiler_params=_cparams("parallel", "parallel"),
        name="gdn_prep",
    )(qkv, qkv, st8, conv_w, ab, alog, dtb)


GDN_BATCH_GROUP = 2


def _gdn_seq_kernel(wq_ref, u_ref, kgt_ref, pm_ref, ee_ref, s0_ref, o_ref, sf_ref, s_scr, *, L):
    c = pl.program_id(1)

    @pl.when(c == 0)
    def _():
        s_scr[...] = s0_ref[...]

    ch = [(bb, hh) for bb in range(GDN_BATCH_GROUP) for hh in range(GDN_HEADS)]
    st = [s_scr[bb, hh] for bb, hh in ch]
    sp = [_split(s) for s in st]
    r = [_dot(wq_ref[bb, 0, hh], sp[i][0]) + _dot(wq_ref[bb, 0, hh], sp[i][1]) for i, (bb, hh) in enumerate(ch)]
    v_new = [(u_ref[bb, hh].astype(F32) - r[i][:L]).astype(BF16) for i, (bb, hh) in enumerate(ch)]
    o = [r[i][L:] + _dot(pm_ref[bb, 0, hh], v_new[i]) for i, (bb, hh) in enumerate(ch)]
    for i, (bb, hh) in enumerate(ch):
        o_ref[bb, :, GDN_DV * hh:GDN_DV * (hh + 1)] = o[i].astype(o_ref.dtype)
    s_new = [ee_ref[bb, 0][:, hh:hh + 1] * st[i] + _dot(kgt_ref[bb, 0, hh], v_new[i]) for i, (bb, hh) in enumerate(ch)]
    for i, (bb, hh) in enumerate(ch):
        s_scr[bb, hh] = s_new[i]

    @pl.when(c == pl.num_programs(1) - 1)
    def _():
        sf_ref[...] = s_scr[...]


def _gdn_seq_call(wq, um, kgt, pm, ee, s0, L):
    b, h, s, _ = um.shape
    nch = s // L
    bg = GDN_BATCH_GROUP
    sspec = pl.BlockSpec((bg, h, GDN_DK, GDN_DV), lambda g, c: (g, 0, 0, 0))
    return pl.pallas_call(
        functools.partial(_gdn_seq_kernel, L=L),
        grid=(b // bg, nch),
        in_specs=[pl.BlockSpec((bg, 1, h, 2 * L, LANES), lambda g, c: (g, c, 0, 0, 0)),
                  pl.BlockSpec((bg, h, L, LANES), lambda g, c: (g, 0, c, 0)),
                  pl.BlockSpec((bg, 1, h, GDN_DK, L), lambda g, c: (g, c, 0, 0, 0)),
                  pl.BlockSpec((bg, 1, h, L, L), lambda g, c: (g, c, 0, 0, 0)),
                  pl.BlockSpec((bg, 1, 1, LANES), lambda g, c: (g, c, 0, 0)),
                  sspec],
        out_specs=[pl.BlockSpec((bg, L, GDN_VW), lambda g, c: (g, c, 0)), sspec],
        out_shape=[jax.ShapeDtypeStruct((b, s, GDN_VW), BF16),
                   jax.ShapeDtypeStruct((b, h, GDN_DK, GDN_DV), F32)],
        scratch_shapes=[pltpu.VMEM((bg, h, GDN_DK, GDN_DV), F32)],
        compiler_params=_cparams("parallel", "arbitrary"),
        name="gdn_seq",
    )(wq, um, kgt, pm, ee, s0)


def _c_out_kernel(x_ref, o_ref, z_ref, gn_ref, w_ref, g1_ref, out_ref):
    o = o_ref[0].astype(F32)
    z = z_ref[0].astype(F32)
    gn = gn_ref[...]
    parts = []
    for hh in range(GDN_HEADS):
        oh = o[:, GDN_DV * hh:GDN_DV * (hh + 1)]
        zh = z[:, GDN_DV * hh:GDN_DV * (hh + 1)]
        parts.append((_rms(oh, gn) * (zh * jax.nn.sigmoid(zh))).astype(BF16))
    out = _dot(jnp.concatenate(parts, axis=1), w_ref[...])
    out_ref[0] = x_ref[0] + g1_ref[0] * out


def _c_out_call(x, o, z, gn, w, g1, ts):
    b, s, d = x.shape
    return pl.pallas_call(
        _c_out_kernel,
        grid=(b, s // ts),
        in_specs=[_tok(d, ts), _tok(GDN_VW, ts), _tok(GDN_VW, ts), _full(gn), _full(w), _modspec(g1, ts)],
        out_specs=_tok(d, ts),
        out_shape=jax.ShapeDtypeStruct((b, s, d), F32),
        compiler_params=_cparams("parallel", "parallel"),
        name="c_out",
    )(x, o, z, gn, w, g1)


def _pack_halves(v):
    n = v.shape[1] // 2
    bits = pltpu.bitcast(v.astype(BF16).astype(F32), jnp.uint32)
    return (bits[:, :n] >> 16) | (bits[:, n:] & jnp.uint32(0xFFFF0000))


def _unpack_halves(w):
    return pltpu.bitcast(w << 16, F32), pltpu.bitcast(w & jnp.uint32(0xFFFF0000), F32)


def _router_kernel(x_ref, gain_ref, sc_ref, sh_ref, wr_ref, br_ref, h_ref, re_ref, rg_ref, cnt_ref, carry):
    @pl.when(jnp.logical_and(pl.program_id(0) == 0, pl.program_id(1) == 0))
    def _():
        carry[...] = jnp.zeros_like(carry)

    h = _normmod(x_ref[0], gain_ref[...], sc_ref[0], sh_ref[0])
    h_ref[0] = _pack_halves(h)
    logits = jnp.dot(h, wr_ref[...], preferred_element_type=F32, precision=HI) + br_ref[...]
    lane = lax.broadcasted_iota(jnp.int32, logits.shape, 1).astype(F32)
    neg = jnp.float32(-jnp.inf)
    big = jnp.float32(1e9)
    gl = jnp.where(lane < N_GROUPS, logits, neg)
    gm = jnp.max(gl, axis=-1, keepdims=True)
    g_val = 1.0 / jnp.sum(jnp.exp(gl - gm), axis=-1, keepdims=True)
    g_idx = jnp.min(jnp.where(gl == gm, lane, big), axis=-1, keepdims=True)
    lo = N_GROUPS + EXPERTS_PER_GROUP * g_idx
    el = jnp.where(jnp.logical_and(lane >= lo, lane < lo + EXPERTS_PER_GROUP), logits, neg)
    em = jnp.max(el, axis=-1, keepdims=True)
    i1 = jnp.min(jnp.where(el == em, lane, big), axis=-1, keepdims=True)
    el2 = jnp.where(lane == i1, neg, el)
    em2 = jnp.max(el2, axis=-1, keepdims=True)
    i2 = jnp.min(jnp.where(el2 == em2, lane, big), axis=-1, keepdims=True)
    es = jnp.sum(jnp.exp(el - em), axis=-1, keepdims=True)
    p1 = 1.0 / es
    p2 = jnp.exp(em2 - em) / es
    den = p1 + p2
    rg_ref[0] = jnp.where(lane == 0, g_val * p1 / den, jnp.where(lane == 1, g_val * p2 / den, 0.0))
    oh1 = lane == i1
    oh2 = lane == i2
    both = jnp.where(jnp.logical_or(oh1, oh2), 1.0, 0.0)
    ts = both.shape[0]
    tri = (lax.broadcasted_iota(jnp.int32, (ts, ts), 0) > lax.broadcasted_iota(jnp.int32, (ts, ts), 1))
    pre = _dot(jnp.where(tri, 1.0, 0.0).astype(BF16), both.astype(BF16)) + carry[...]
    r1 = jnp.sum(jnp.where(oh1, pre, 0.0), axis=-1, keepdims=True)
    r2 = jnp.sum(jnp.where(oh2, pre, 0.0), axis=-1, keepdims=True)
    re_ref[0] = jnp.where(lane == 0, i1 - N_GROUPS, jnp.where(lane == 1, i2 - N_GROUPS, jnp.where(
        lane == 2, r1, jnp.where(lane == 3, r2, 0.0)))).astype(jnp.int32)
    carry[...] += jnp.sum(both, axis=0, keepdims=True)
    cnt_ref[...] = carry[...]


def _router_call(x, gain, sc, sh, wr, br, ts):
    b, s, d = x.shape
    return pl.pallas_call(
        _router_kernel,
        grid=(b, s // ts),
        in_specs=[_tok(d, ts), _full(gain), _modspec(sc, ts), _modspec(sh, ts), _full(wr), _full(br)],
        out_specs=[_tok(d // 2, ts), _tok(LANES, ts), _tok(LANES, ts), pl.BlockSpec((1, LANES), lambda bb, j: (0, 0))],
        out_shape=[jax.ShapeDtypeStruct((b, s, d // 2), jnp.uint32),
                   jax.ShapeDtypeStruct((b, s, LANES), jnp.int32),
                   jax.ShapeDtypeStruct((b, s, LANES), F32),
                   jax.ShapeDtypeStruct((1, LANES), F32)],
        scratch_shapes=[pltpu.VMEM((1, LANES), F32)],
        compiler_params=_cparams("arbitrary", "arbitrary"),
        name="moe_router",
    )(x, gain, sc, sh, wr, br)


def _row_copy(src, dst, src_row, dst_row, sem):
    return pltpu.make_async_copy(src.at[pl.ds(src_row, 1)], dst.at[pl.ds(dst_row, 1)], sem)


ROUTE_W = 2 * TOP_K
DMA_UNROLL = 8


def _slot(ps_ref, rt_ref, r, k):
    return ps_ref[rt_ref[0, 0, ROUTE_W * r + k]] + rt_ref[0, 0, ROUTE_W * r + TOP_K + k]


def _dispatch_kernel(ps_ref, rt_ref, h_ref, xin_hbm, xb_hbm, sem, *, tb):
    del xin_hbm

    def issue(r, c):
        for k in range(TOP_K):
            _row_copy(h_ref, xb_hbm, r, _slot(ps_ref, rt_ref, r, k), sem).start()
        return c
    lax.fori_loop(0, tb, issue, 0, unroll=DMA_UNROLL)
    for k in range(TOP_K):
        pltpu.make_async_copy(h_ref, xb_hbm.at[pl.ds(0, tb)], sem).wait()


def _dispatch_call(pad_start, route, h2, xb0, tb):
    n, dh = h2.shape
    nb = n // tb
    grid_spec = pltpu.PrefetchScalarGridSpec(
        num_scalar_prefetch=1,
        grid=(nb,),
        in_specs=[pl.BlockSpec((1, 1, ROUTE_W * tb), lambda j, ps: (j, 0, 0), memory_space=pltpu.SMEM),
                  pl.BlockSpec((tb, dh), lambda j, ps: (j, 0)),
                  pl.BlockSpec(memory_space=pl.ANY)],
        out_specs=pl.BlockSpec(memory_space=pl.ANY),
        scratch_shapes=[pltpu.SemaphoreType.DMA(())],
    )
    return pl.pallas_call(
        functools.partial(_dispatch_kernel, tb=tb),
        grid_spec=grid_spec,
        out_shape=jax.ShapeDtypeStruct(xb0.shape, xb0.dtype),
        input_output_aliases={3: 0},
        compiler_params=_cparams("arbitrary"),
        name="moe_dispatch",
    )(pad_start, route, h2, xb0)


def _ffn_kernel(be_ref, nu_ref, x_ref, w1_ref, w3_ref, w2_ref, y_ref):
    i = pl.program_id(0)
    half = w1_ref.shape[1] // 2

    @pl.when(i < nu_ref[0])
    def _():
        lo, hi = _unpack_halves(x_ref[...])
        lo = lo.astype(BF16)
        hi = hi.astype(BF16)
        a = _dot(lo, w1_ref[0, :half]) + _dot(hi, w1_ref[0, half:])
        g = _dot(lo, w3_ref[0, :half]) + _dot(hi, w3_ref[0, half:])
        mid = (a * jax.nn.sigmoid(a) * g).astype(BF16)
        y_ref[...] = _pack_halves(_dot(mid, w2_ref[0]))

    @pl.when(i >= nu_ref[0])
    def _():
        y_ref[...] = jnp.zeros_like(y_ref)


def _ffn_call(block_e, n_used, xb, w1, w3, w2):
    n_blocks = block_e.shape[0]
    dh = xb.shape[1]
    d = 2 * dh
    grid_spec = pltpu.PrefetchScalarGridSpec(
        num_scalar_prefetch=2,
        grid=(n_blocks,),
        in_specs=[pl.BlockSpec((MOE_BLOCK, dh), lambda i, be, nu: (i, 0)),
                  pl.BlockSpec((1, d, D_EXPERT), lambda i, be, nu: (be[i], 0, 0)),
                  pl.BlockSpec((1, d, D_EXPERT), lambda i, be, nu: (be[i], 0, 0)),
                  pl.BlockSpec((1, D_EXPERT, d), lambda i, be, nu: (be[i], 0, 0))],
        out_specs=pl.BlockSpec((MOE_BLOCK, dh), lambda i, be, nu: (i, 0)),
    )
    return pl.pallas_call(
        _ffn_kernel,
        grid_spec=grid_spec,
        out_shape=jax.ShapeDtypeStruct((n_blocks * MOE_BLOCK, dh), jnp.uint32),
        compiler_params=_cparams("arbitrary"),
        name="moe_ffn",
    )(block_e, n_used, xb, w1, w3, w2)


def _combine_kernel(ps_ref, rcur_ref, rnxt_ref, yb_hbm, x_ref, rg_ref, g2_ref, fin_ref, o_ref, b0, b1, sem,
                    *, tb, final):
    j = pl.program_id(0)
    nb = pl.num_programs(0)
    slot = j % 2

    def fetch(rref, sl):
        def issue(r, c):
            _row_copy(yb_hbm, b0.at[sl], _slot(ps_ref, rref, r, 0), r, sem.at[sl]).start()
            _row_copy(yb_hbm, b1.at[sl], _slot(ps_ref, rref, r, 1), r, sem.at[sl]).start()
            return c
        lax.fori_loop(0, tb, issue, 0, unroll=DMA_UNROLL)

    @pl.when(j == 0)
    def _():
        fetch(rcur_ref, 0)

    @pl.when(j + 1 < nb)
    def _():
        fetch(rnxt_ref, 1 - slot)

    pltpu.make_async_copy(yb_hbm.at[pl.ds(0, tb)], b0.at[slot], sem.at[slot]).wait()
    pltpu.make_async_copy(yb_hbm.at[pl.ds(0, tb)], b1.at[slot], sem.at[slot]).wait()
    rg = rg_ref[...]
    lo0, hi0 = _unpack_halves(b0[slot])
    lo1, hi1 = _unpack_halves(b1[slot])
    g0 = rg[:, 0:1]
    g1 = rg[:, 1:2]
    y = jnp.concatenate([g0 * lo0 + g1 * lo1, g0 * hi0 + g1 * hi1], axis=1)
    xn = x_ref[...] + g2_ref[0] * y
    if final:
        xn = _rms(xn, fin_ref[...])
    o_ref[...] = xn


def _combine_call(pad_start, route, yb, x, rg, g2, fin, tb, final):
    b, s, d = x.shape
    n = b * s
    nb = n // tb
    npb = s // tb
    dh = yb.shape[1]
    if g2.shape[1] == 1:
        gspec = pl.BlockSpec((1, 1, d), lambda j, ps: (j // npb, 0, 0))
    else:
        gspec = pl.BlockSpec((1, tb, d), lambda j, ps: (j // npb, j % npb, 0))
    grid_spec = pltpu.PrefetchScalarGridSpec(
        num_scalar_prefetch=1,
        grid=(nb,),
        in_specs=[pl.BlockSpec((1, 1, ROUTE_W * tb), lambda j, ps: (j, 0, 0), memory_space=pltpu.SMEM),
                  pl.BlockSpec((1, 1, ROUTE_W * tb), lambda j, ps: (jnp.minimum(j + 1, nb - 1), 0, 0),
                               memory_space=pltpu.SMEM),
                  pl.BlockSpec(memory_space=pl.ANY),
                  pl.BlockSpec((tb, d), lambda j, ps: (j, 0)), pl.BlockSpec((tb, LANES), lambda j, ps: (j, 0)),
                  gspec, pl.BlockSpec(fin.shape, lambda j, ps: (0, 0))],
        out_specs=pl.BlockSpec((tb, d), lambda j, ps: (j, 0)),
        scratch_shapes=[pltpu.VMEM((2, tb, dh), jnp.uint32), pltpu.VMEM((2, tb, dh), jnp.uint32),
                        pltpu.SemaphoreType.DMA((2,))],
    )
    out = pl.pallas_call(
        functools.partial(_combine_kernel, tb=tb, final=final),
        grid_spec=grid_spec,
        out_shape=jax.ShapeDtypeStruct((n, d), F32),
        compiler_params=_cparams("arbitrary"),
        name="moe_combine",
    )(pad_start, route, route, yb, x.reshape(n, d), rg.reshape(n, LANES), g2, fin)
    return out.reshape(b, s, d)


def _moe(x, gain, sc, sh, g2, mp, fin, ts, final):
    b, s, d = x.shape
    n = b * s
    h2, r_e, r_g, cnt = _router_call(x, gain, sc, sh, mp["wr"], mp["br"], ts)
    counts = cnt[0, N_GROUPS:N_GROUPS + N_EXPERTS].astype(jnp.int32)
    padded = (counts + MOE_BLOCK - 1) // MOE_BLOCK * MOE_BLOCK
    pad_end = jnp.cumsum(padded)
    pad_start = (pad_end - padded).astype(jnp.int32)
    n_blocks = (n * TOP_K + N_EXPERTS * (MOE_BLOCK - 1) + MOE_BLOCK - 1) // MOE_BLOCK
    blk0 = jnp.arange(n_blocks, dtype=jnp.int32) * MOE_BLOCK
    block_e = jnp.minimum(jnp.sum((pad_end[None, :] <= blk0[:, None]).astype(jnp.int32), axis=1), N_EXPERTS - 1)
    n_used = (pad_end[-1:] // MOE_BLOCK).astype(jnp.int32)
    route = r_e[:, :, :ROUTE_W].reshape(n // ts, 1, ROUTE_W * ts)
    xb0 = jnp.zeros((n_blocks * MOE_BLOCK, d // 2), jnp.uint32)
    xb = _dispatch_call(pad_start, route, h2.reshape(n, d // 2), xb0, ts)
    yb = _ffn_call(block_e, n_used, xb, mp["w1"], mp["w3"], mp["w2"])
    return _combine_call(pad_start, route, yb, x, r_g, g2, fin, ts, final)


def _prep_ab(l, p):
    w_in = p["w_in_ab"][l]
    o2 = S5_WIDTH + MLA_Q_RANK + MLA_KV_RANK
    half = MLA_ROPE // 2
    d = w_in.shape[0]
    kr = w_in[:, o2:]
    kr_rot = jnp.concatenate([-kr[:, half:], kr[:, :half]], axis=1)
    z64 = jnp.zeros((d, MLA_NOPE), F32)
    z32 = jnp.zeros((d, LANES - MLA_NOPE - MLA_ROPE), F32)
    w_ext = jnp.concatenate([w_in[:, :o2], z64, kr, z32, z64, kr_rot, z32], axis=1)
    qu = p["mla_q_up"][l].reshape(MLA_Q_RANK, MLA_HEADS, MLA_NOPE + MLA_ROPE)
    qn, qr = qu[:, :, :MLA_NOPE], qu[:, :, MLA_NOPE:]
    zq = jnp.zeros((MLA_Q_RANK, MLA_HEADS, LANES - MLA_NOPE - MLA_ROPE), F32)
    wqa = jnp.concatenate([qn, qr, zq], axis=2).reshape(MLA_Q_RANK, MLA_HEADS * LANES)
    wqb = jnp.concatenate([jnp.zeros_like(qn), -qr[:, :, half:], qr[:, :, :half], zq], axis=2)
    wqb = wqb.reshape(MLA_Q_RANK, MLA_HEADS * LANES)
    kvu = p["mla_kv_up"][l].reshape(MLA_KV_RANK, MLA_HEADS, MLA_NOPE + MLA_V)
    wk = jnp.concatenate([kvu[:, :, :MLA_NOPE], jnp.zeros((MLA_KV_RANK, MLA_HEADS, LANES - MLA_NOPE), F32)], axis=2)
    wo = p["w_out_ab"][l]
    out = dict(
        w_in=w_ext.astype(BF16), q_norm=p["mla_q_norm"][l].reshape(1, -1), kv_norm=p["mla_kv_norm"][l].reshape(1, -1),
        wqa=wqa.astype(BF16), wqb=wqb.astype(BF16),
        wk=wk.reshape(MLA_KV_RANK, MLA_HEADS * LANES).astype(BF16),
        wv=jnp.concatenate([kvu[:, :, MLA_NOPE:], jnp.zeros((MLA_KV_RANK, MLA_HEADS, LANES - MLA_V), F32)],
                           axis=2).reshape(MLA_KV_RANK, MLA_HEADS * LANES).astype(BF16),
        wv_t=jnp.concatenate([kvu[:, :, MLA_NOPE:], jnp.zeros((MLA_KV_RANK, MLA_HEADS, LANES - MLA_V), F32)],
                             axis=2).transpose(1, 2, 0).astype(BF16),
        s5_d=p["s5_d"][l].reshape(1, -1), glu_w=p["s5_glu_w"][l].astype(BF16), glu_b=p["s5_glu_b"][l].reshape(1, -1),
        w_out_s5=wo[:S5_WIDTH].astype(BF16),
        w_out_at=wo[S5_WIDTH:].astype(BF16))
    out.update(_s5_mats(p["s5_a_re"][l], p["s5_a_im"][l], p["s5_b_re"][l], p["s5_b_im"][l],
                        p["s5_c_re"][l], p["s5_c_im"][l], p["s5_log_dt"][l]))
    return out


def _prep_c(l, p):
    w = p["w_in_c"][l]
    pad = jnp.zeros((w.shape[0], LANES - 2 * GDN_HEADS), F32)

    def row(v):
        return jnp.concatenate([v, jnp.zeros((LANES - GDN_HEADS,), F32)]).reshape(1, LANES)

    return dict(w_in=jnp.concatenate([w, pad], axis=1).astype(BF16), conv_w=p["conv_w"][l],
                alog=row(p["gdn_a_log"][l]), dtb=row(p["gdn_dt_bias"][l]),
                gn=p["gdn_norm"][l].reshape(1, -1), w_out=p["w_out_c"][l].astype(BF16))


def _prep_moe(layer, p):
    d = p["moe_w_group"].shape[1]
    wr = jnp.concatenate([p["moe_w_group"][layer], p["moe_w_expert"][layer],
                          jnp.zeros((d, LANES - N_GROUPS - N_EXPERTS), F32)], axis=1)
    br = jnp.concatenate([p["moe_b_group"][layer], p["moe_b_expert"][layer],
                          jnp.zeros((LANES - N_GROUPS - N_EXPERTS,), F32)]).reshape(1, LANES)
    return dict(wr=wr, br=br, w1=p["moe_w1"][layer].astype(BF16), w3=p["moe_w3"][layer].astype(BF16),
                w2=p["moe_w2"][layer].astype(BF16))


def _rope_tables(pos):
    half = MLA_ROPE // 2
    inv = ROPE_THETA ** (-jnp.arange(half, dtype=F32) / half)
    ang = pos.astype(F32)[:, None] * inv[None, :]
    cos, sin = jnp.cos(ang), jnp.sin(ang)
    n = pos.shape[0]
    z64 = jnp.zeros((n, MLA_NOPE), F32)
    z32 = jnp.zeros((n, LANES - MLA_NOPE - MLA_ROPE), F32)
    scale = (MLA_NOPE + MLA_ROPE) ** -0.5 * math.log2(math.e)
    ck = jnp.concatenate([z64, cos, cos, z32], axis=1)
    sk = jnp.concatenate([z64, sin, sin, z32], axis=1)
    cq = jnp.concatenate([jnp.ones_like(z64), cos, cos, z32], axis=1) * scale
    return cq, sk * scale, ck, sk


def _trunk(x, mods, caches, wab, wc, wmoe, p, flat):
    b, s, d = x.shape
    lat_c, rope_c, s5re_c, s5im_c, conv_c, gdn_c = caches
    past = 0 if lat_c is None else lat_c.shape[2]
    pos = past + jnp.arange(s)
    if flat:
        xt = x.reshape(1, b * s, d)
        ts = b * s
        pos_rows = jnp.tile(pos, b)

        def mrow(m):
            return jnp.repeat(m, s, axis=0).reshape(1, b * s, d)
    else:
        xt = x
        ts = 256
        pos_rows = pos

        def mrow(m):
            return m.reshape(b, 1, d)
    bt, st = xt.shape[:2]
    outs = {}
    depth = p["norm_mix"].shape[0]
    for layer in range(depth):
        l = layer // 2
        sh1, sc1, g1, sh2, sc2, g2 = [mrow(m) for m in jnp.split(mods[layer], 6, axis=-1)]
        gain = p["norm_mix"][layer].reshape(1, d)
        if layer % 2 == 0:
            wp = wab[l]
            u, q, lat, kr = _ab_in_call(xt, gain, sc1, sh1, wp, _rope_tables(pos_rows), ts)
            lat_b = lat.reshape(b, s, MLA_KV_RANK)
            kr_b = kr.reshape(b, s, LANES)
            outs["lat"] = lat_b
            outs["krope"] = kr_b[:, :, MLA_NOPE:MLA_NOPE + MLA_ROPE]
            if lat_c is None:
                lat_all, kr_all, sk_valid, tkv, tq, tk = lat_b, kr_b, s, 256, min(512, s), min(512, s)
            else:
                krc = jnp.pad(rope_c[l], ((0, 0), (0, 0), (MLA_NOPE, LANES - MLA_NOPE - MLA_ROPE)))
                sk_valid = past + s
                tkv = 128
                skp = (sk_valid + tkv - 1) // tkv * tkv
                lat_all = jnp.pad(jnp.concatenate([lat_c[l], lat_b], axis=1), ((0, 0), (0, skp - sk_valid), (0, 0)))
                kr_all = jnp.pad(jnp.concatenate([krc, kr_b], axis=1), ((0, 0), (0, skp - sk_valid), (0, 0)))
                tq, tk = s, skp
            transposed = tq % ATTN_SUB == 0
            kk, vv = _kv_call(lat_all, kr_all, wp, tkv, transposed)
            qh = q.reshape(MLA_HEADS, b, s, LANES).transpose(1, 0, 2, 3) if flat else q
            attn = (_attn_t_call if transposed else _attn_call)(qh, kk, vv, tq, tk, past, sk_valid)
            attn = attn.reshape(bt, st, MLA_HEADS * MLA_V)
            nchk = s // S5_CHUNK
            r = b * nchk
            u2 = u.reshape(b * s, S5_WIDTH)
            tr = min(r, 256)
            wre, wim = _s5_w_call(u2, wp["bre"], wp["bim"], tr)
            ngp = S5_GROUPS * S5_STATE

            def to_pairs(v):
                return v.reshape(b, 1, ngp)

            if s5re_c is None:
                x0re = jnp.zeros((b, 1, ngp), F32)
                x0im = jnp.zeros((b, 1, ngp), F32)
            else:
                x0re, x0im = to_pairs(s5re_c[l]), to_pairs(s5im_c[l])
            nchp = (nchk + 7) // 8 * 8
            tc = min(nchp, 128)
            wre3 = jnp.pad(wre.reshape(b, nchk, ngp), ((0, 0), (0, nchp - nchk), (0, 0)))
            wim3 = jnp.pad(wim.reshape(b, nchk, ngp), ((0, 0), (0, nchp - nchk), (0, 0)))
            xere, xeim = _s5_scan_call(wre3, wim3, wp["lre"], wp["lim"], x0re, x0im, tc)
            outs["s5re"] = xere[:, nchk - 1].reshape(b, S5_GROUPS, S5_STATE)
            outs["s5im"] = xeim[:, nchk - 1].reshape(b, S5_GROUPS, S5_STATE)
            xsre = jnp.concatenate([x0re, xere[:, :nchk - 1]], axis=1).reshape(r, ngp)
            xsim = jnp.concatenate([x0im, xeim[:, :nchk - 1]], axis=1).reshape(r, ngp)
            ys = _s5_y_call(u2, xsre, xsim, wp["mst"], wp["cc"], tr).reshape(bt, st, S5_WIDTH)
            xt = _ab_out_call(xt, ys, u, attn, wp, g1, ts)
        else:
            wp = wc[l]
            qkv, z, ab = _c_in_call(xt, gain, sc1, sh1, wp["w_in"], ts)
            qkv_b = qkv.reshape(b, s, GDN_QKV)
            outs["conv"] = qkv_b[:, s - (CONV_W - 1):].astype(F32)
            if conv_c is None:
                st8 = jnp.zeros((b, 8, GDN_QKV), BF16)
                s0 = jnp.zeros((b, GDN_HEADS, GDN_DK, GDN_DV), F32)
            else:
                st8 = jnp.pad(conv_c[l], ((0, 0), (8 - (CONV_W - 1), 0), (0, 0))).astype(BF16)
                s0 = gdn_c[l]
            lg = min(s, CHUNK)
            wq, um, kgt, pm, ee = _gdn_prep_call(qkv_b, st8, wp["conv_w"], ab.reshape(b, s, LANES),
                                                 wp["alog"], wp["dtb"], lg)
            o, sfin = _gdn_seq_call(wq, um, kgt, pm, ee, s0, lg)
            outs["gdn"] = sfin
            xt = _c_out_call(xt, o.reshape(bt, st, GDN_VW), z, wp["gn"], wp["w_out"], g1, ts)
        final = layer == depth - 1
        xt = _moe(xt, p["norm_ffn"][layer].reshape(1, d), sc2, sh2, g2, wmoe[layer],
                  p["norm_final"].reshape(1, d), ts, final)
    y = xt.reshape(b, s, d)
    return (y, outs["lat"][None], outs["krope"][None], outs["s5re"][None], outs["s5im"][None],
            outs["conv"][None], outs["gdn"][None])


def kernel(x_prompt, x_sample, c_prompt, c_sample, cache_mla_latent, cache_mla_krope, state_s5_re, state_s5_im,
           state_conv, state_gdn, w_mod, b_mod, norm_mix, norm_ffn, norm_final, w_in_ab, s5_a_re, s5_a_im,
           s5_b_re, s5_b_im, s5_c_re, s5_c_im, s5_d, s5_log_dt, s5_glu_w, s5_glu_b, mla_q_norm, mla_q_up,
           mla_kv_norm, mla_kv_up, w_out_ab, w_in_c, conv_w, gdn_a_log, gdn_dt_bias, gdn_norm, w_out_c,
           moe_w_group, moe_b_group, moe_w_expert, moe_b_expert, moe_w1, moe_w3, moe_w2):
    p = dict(w_mod=w_mod, b_mod=b_mod, norm_mix=norm_mix, norm_ffn=norm_ffn, norm_final=norm_final,
             w_in_ab=w_in_ab, s5_a_re=s5_a_re, s5_a_im=s5_a_im, s5_b_re=s5_b_re, s5_b_im=s5_b_im,
             s5_c_re=s5_c_re, s5_c_im=s5_c_im, s5_d=s5_d, s5_log_dt=s5_log_dt, s5_glu_w=s5_glu_w,
             s5_glu_b=s5_glu_b, mla_q_norm=mla_q_norm, mla_q_up=mla_q_up, mla_kv_norm=mla_kv_norm,
             mla_kv_up=mla_kv_up, w_out_ab=w_out_ab, w_in_c=w_in_c, conv_w=conv_w, gdn_a_log=gdn_a_log,
             gdn_dt_bias=gdn_dt_bias, gdn_norm=gdn_norm, w_out_c=w_out_c, moe_w_group=moe_w_group,
             moe_b_group=moe_b_group, moe_w_expert=moe_w_expert, moe_b_expert=moe_b_expert,
             moe_w1=moe_w1, moe_w3=moe_w3, moe_w2=moe_w2)
    depth = norm_mix.shape[0]
    bp, bs = c_prompt.shape[0], c_sample.shape[0]
    nb = (bp + bs + 7) // 8 * 8
    c_all = jnp.pad(jnp.concatenate([c_prompt, c_sample], axis=0), ((0, nb - bp - bs), (0, 0)))
    mods = _mod_call(c_all, w_mod, b_mod)
    wab = [_prep_ab(l, p) for l in range((depth + 1) // 2)]
    wc = [_prep_c(l, p) for l in range(depth // 2)]
    wmoe = [_prep_moe(layer, p) for layer in range(depth)]
    none6 = (None,) * 6
    outp = _trunk(x_prompt, mods[:, :bp], none6, wab, wc, wmoe, p, flat=False)
    caches = (cache_mla_latent, cache_mla_krope, state_s5_re, state_s5_im, state_conv, state_gdn)
    outs = _trunk(x_sample, mods[:, bp:bp + bs], caches, wab, wc, wmoe, p, flat=True)
    return (outp[0], outs[0]) + tuple(outp[1:]) + tuple(outs[1:])
```

```python
import functools
import math

import jax
import jax.numpy as jnp
from jax import lax
from jax.experimental import pallas as pl
from jax.experimental.pallas import tpu as pltpu

F32 = jnp.float32
BF16 = jnp.bfloat16
HI = lax.Precision.HIGHEST
EPS = 1e-6

D_MODEL = 1024
CHUNK = 64
S5_WIDTH = 512
S5_GROUP = 16
S5_GROUPS = 32
S5_STATE = 64
S5_CHUNK = 16
S5_LB = 128 // S5_GROUP
MLA_HEADS = 8
MLA_NOPE = 64
MLA_ROPE = 32
MLA_V = 64
MLA_Q_RANK = 384
MLA_KV_RANK = 256
ROPE_THETA = 10000.0
GDN_HEADS = 8
GDN_DK = 128
GDN_DV = 128
CONV_W = 4
GDN_QKV = GDN_HEADS * (2 * GDN_DK + GDN_DV)
GDN_VW = GDN_HEADS * GDN_DV
N_GROUPS = 4
EXPERTS_PER_GROUP = 8
N_EXPERTS = 32
D_EXPERT = 512
TOP_K = 2
MOE_BLOCK = 256
ATTN_SUB = 256
LANES = 128
VMEM_LIMIT = 48 * 1024 * 1024


def _cparams(*sem):
    return pltpu.CompilerParams(dimension_semantics=sem, vmem_limit_bytes=VMEM_LIMIT)


def _dot(a, b):
    return jnp.dot(a, b, preferred_element_type=F32)


def _dot_nt(a, b, precision=None):
    return lax.dot_general(a, b, (((1,), (1,)), ((), ())), preferred_element_type=F32, precision=precision)


def _full(arr):
    nd = arr.ndim
    return pl.BlockSpec(arr.shape, lambda *_: (0,) * nd)


def _tok(width, ts):
    return pl.BlockSpec((1, ts, width), lambda b, j: (b, j, 0))


def _modspec(arr, ts):
    if arr.shape[1] == 1:
        return pl.BlockSpec((1, 1, arr.shape[2]), lambda b, j: (b, 0, 0))
    return pl.BlockSpec((1, ts, arr.shape[2]), lambda b, j: (b, j, 0))


def _normmod(x, gain, sc, sh):
    ms = jnp.mean(x * x, axis=-1, keepdims=True)
    return x * lax.rsqrt(ms + EPS) * gain * (1.0 + sc) + sh


def _rms(x, gain):
    return x * lax.rsqrt(jnp.mean(x * x, axis=-1, keepdims=True) + EPS) * gain


def _mod_kernel(c_ref, w_ref, b_ref, o_ref):
    c = c_ref[...]
    a = (c * jax.nn.sigmoid(c)).astype(BF16)
    o_ref[0] = _dot(a, w_ref[0]) + b_ref[0]


def _mod_call(c_all, w_mod, b_mod):
    depth, d, n6 = w_mod.shape
    bp = c_all.shape[0]
    tn = 1536
    return pl.pallas_call(
        _mod_kernel,
        grid=(depth, n6 // tn),
        in_specs=[pl.BlockSpec((bp, d), lambda l, n: (0, 0)),
                  pl.BlockSpec((1, d, tn), lambda l, n: (l, 0, n)),
                  pl.BlockSpec((1, 1, tn), lambda l, n: (l, 0, n))],
        out_specs=pl.BlockSpec((1, bp, tn), lambda l, n: (l, 0, n)),
        out_shape=jax.ShapeDtypeStruct((depth, bp, n6), F32),
        compiler_params=_cparams("parallel", "parallel"),
        name="mod",
    )(c_all, w_mod.astype(BF16), b_mod.reshape(depth, 1, n6))


def _ab_in_kernel(x_ref, gain_ref, sc_ref, sh_ref, w_ref, qg_ref, wqa_ref, wqb_ref, kvg_ref,
                  cq_ref, sq_ref, ck_ref, sk_ref, u_ref, q_ref, lat_ref, kr_ref):
    h = _normmod(x_ref[0], gain_ref[...], sc_ref[0], sh_ref[0]).astype(BF16)
    proj = _dot(h, w_ref[...])
    u_ref[0] = proj[:, :S5_WIDTH]
    o1 = S5_WIDTH + MLA_Q_RANK
    o2 = o1 + MLA_KV_RANK
    qn = _rms(proj[:, S5_WIDTH:o1], qg_ref[...]).astype(BF16)
    qa = _dot(qn, wqa_ref[...])
    qb = _dot(qn, wqb_ref[...])
    cq = cq_ref[...]
    sq = sq_ref[...]
    for hh in range(MLA_HEADS):
        sl = slice(LANES * hh, LANES * (hh + 1))
        q_ref[0, hh] = (qa[:, sl] * cq + qb[:, sl] * sq).astype(BF16)
    lat_ref[0] = _rms(proj[:, o1:o2], kvg_ref[...])
    kr_ref[0] = proj[:, o2:o2 + LANES] * ck_ref[...] + proj[:, o2 + LANES:o2 + 2 * LANES] * sk_ref[...]


def _ab_in_call(x, gain, sc, sh, wp, tabs, ts):
    b, s, d = x.shape
    cq, sq, ck, sk = tabs
    tab = pl.BlockSpec((ts, LANES), lambda bb, j: (j, 0))
    ins = [x, gain, sc, sh, wp["w_in"], wp["q_norm"], wp["wqa"], wp["wqb"], wp["kv_norm"], cq, sq, ck, sk]
    specs = [_tok(d, ts), _full(gain), _modspec(sc, ts), _modspec(sh, ts), _full(wp["w_in"]), _full(wp["q_norm"]),
             _full(wp["wqa"]), _full(wp["wqb"]), _full(wp["kv_norm"]), tab, tab, tab, tab]
    return pl.pallas_call(
        _ab_in_kernel,
        grid=(b, s // ts),
        in_specs=specs,
        out_specs=[_tok(S5_WIDTH, ts),
                   pl.BlockSpec((1, MLA_HEADS, ts, LANES), lambda bb, j: (bb, 0, j, 0)),
                   _tok(MLA_KV_RANK, ts), _tok(LANES, ts)],
        out_shape=[jax.ShapeDtypeStruct((b, s, S5_WIDTH), F32),
                   jax.ShapeDtypeStruct((b, MLA_HEADS, s, LANES), BF16),
                   jax.ShapeDtypeStruct((b, s, MLA_KV_RANK), F32),
                   jax.ShapeDtypeStruct((b, s, LANES), F32)],
        compiler_params=_cparams("parallel", "parallel"),
        name="ab_in",
    )(*ins)


def _kv_kernel(lat_ref, kr_ref, wk_ref, wv_ref, k_ref, v_ref):
    lat = lat_ref[0].astype(BF16)
    kk = _dot(lat, wk_ref[...])
    vv = _dot(lat, wv_ref[...])
    kr = kr_ref[0]
    ones = (lax.broadcasted_iota(jnp.int32, (1, LANES), 1) >= MLA_V).astype(F32)
    for hh in range(MLA_HEADS):
        k_ref[0, hh] = (kk[:, LANES * hh:LANES * (hh + 1)] + kr).astype(BF16)
        v_ref[0, hh] = (vv[:, LANES * hh:LANES * (hh + 1)] + ones).astype(BF16)


def _kv_call(lat_all, kr_all, wp, ts):
    b, sk, _ = lat_all.shape
    hspec = pl.BlockSpec((1, MLA_HEADS, ts, LANES), lambda bb, j: (bb, 0, j, 0))
    hshape = jax.ShapeDtypeStruct((b, MLA_HEADS, sk, LANES), BF16)
    return pl.pallas_call(
        _kv_kernel,
        grid=(b, sk // ts),
        in_specs=[_tok(MLA_KV_RANK, ts), _tok(LANES, ts), _full(wp["wk"]), _full(wp["wv"])],
        out_specs=[hspec, hspec],
        out_shape=[hshape, hshape],
        compiler_params=_cparams("parallel", "parallel"),
        name="kv_up",
    )(lat_all, kr_all, wp["wk"], wp["wv"])


def _attn_kernel(q_ref, k_ref, v_ref, o_ref, *, tq, tk, q_off, sk_valid):
    i = pl.program_id(2)
    nsub = max(tq // ATTN_SUB, 1)
    tqs = tq // nsub
    chains = [(hh, qi) for hh in range(2) for qi in range(nsub)]
    qs = [q_ref[0, hh, qi * tqs:(qi + 1) * tqs, :] for hh, qi in chains]
    q0 = q_off + i * tq
    lim_full = (q0 // CHUNK + 1) * CHUNK
    lim_tot = jnp.minimum(((q0 + tq - 1) // CHUNK + 1) * CHUNK, sk_valid)
    n_full = jnp.minimum(lim_full // tk, sk_valid // tk)
    n_tot = (lim_tot + tk - 1) // tk

    def step(j, carry, masked):
        off = pl.multiple_of(j * tk, tk)
        nc = range(len(chains))
        if masked:
            kpos = off + lax.broadcasted_iota(jnp.int32, (tqs, tk), 1)
            ok = []
            for qi in range(nsub):
                qpos = q0 + qi * tqs + lax.broadcasted_iota(jnp.int32, (tqs, tk), 0)
                ok.append(jnp.logical_and(kpos // CHUNK <= qpos // CHUNK, kpos < sk_valid))
        kt = [k_ref[0, hh, pl.ds(off, tk), :] for hh in range(2)]
        vt = [v_ref[0, hh, pl.ds(off, tk), :] for hh in range(2)]
        s = [_dot_nt(qs[c], kt[chains[c][0]]) for c in nc]
        if masked:
            s = [jnp.where(ok[chains[c][1]], s[c], -1e30) for c in nc]
        m_new = [jnp.maximum(carry[c][0], jnp.max(s[c], axis=-1, keepdims=True)) for c in nc]
        alpha = [jnp.exp2(carry[c][0] - m_new[c]) for c in nc]
        p = [jnp.exp2(s[c] - m_new[c]).astype(BF16) for c in nc]
        acc = [alpha[c] * carry[c][1] + _dot(p[c], vt[chains[c][0]]) for c in nc]
        return tuple((m_new[c], acc[c]) for c in nc)

    one = (jnp.full((tqs, 1), -1e30, F32), jnp.zeros((tqs, LANES), F32))
    carry = lax.fori_loop(0, n_full, functools.partial(step, masked=False), (one,) * len(chains))
    carry = lax.fori_loop(n_full, n_tot, functools.partial(step, masked=True), carry)
    lane = lax.broadcasted_iota(jnp.int32, (tqs, LANES), 1)
    for qi in range(nsub):
        a0 = carry[chains.index((0, qi))][1]
        a1 = carry[chains.index((1, qi))][1]
        r0 = a0 / pltpu.roll(a0, MLA_V, axis=1)
        r1 = a1 / pltpu.roll(a1, MLA_V, axis=1)
        o_ref[0, qi * tqs:(qi + 1) * tqs, :] = jnp.where(
            lane < MLA_V, r0, pltpu.roll(r1, MLA_V, axis=1)).astype(o_ref.dtype)


def _attn_call(q, k, v, tq, tk, q_off, sk_valid):
    b, h, sq, _ = q.shape
    sk = k.shape[2]
    kern = functools.partial(_attn_kernel, tq=tq, tk=tk, q_off=q_off, sk_valid=sk_valid)
    return pl.pallas_call(
        kern,
        grid=(b, h // 2, sq // tq),
        in_specs=[pl.BlockSpec((1, 2, tq, LANES), lambda bb, hp, i: (bb, hp, i, 0)),
                  pl.BlockSpec((1, 2, sk, LANES), lambda bb, hp, i: (bb, hp, 0, 0), pipeline_mode=pl.Buffered(1)),
                  pl.BlockSpec((1, 2, sk, LANES), lambda bb, hp, i: (bb, hp, 0, 0), pipeline_mode=pl.Buffered(1))],
        out_specs=pl.BlockSpec((1, tq, LANES), lambda bb, hp, i: (bb, i, hp)),
        out_shape=jax.ShapeDtypeStruct((b, sq, h * MLA_V), BF16),
        compiler_params=_cparams("parallel", "parallel", "arbitrary"),
        name="mla_attn",
    )(q, k, v)


def _chunk_rows(ref, s, tr):
    return ref[pl.ds(s, tr, stride=S5_CHUNK), :]


def _s5_w_kernel(u_ref, bre_ref, bim_ref, wre_ref, wim_ref, *, tr):
    ucat = jnp.concatenate([_chunk_rows(u_ref, s, tr).astype(BF16) for s in range(S5_CHUNK)], axis=1)
    wre_ref[...] = _dot(ucat, bre_ref[0])
    wim_ref[...] = _dot(ucat, bim_ref[0])


def _s5_w_call(u2, bre, bim, tr):
    n = u2.shape[0]
    r = n // S5_CHUNK
    nq = bre.shape[0]
    sw = S5_LB * S5_STATE
    out = jax.ShapeDtypeStruct((r, nq * sw), F32)
    ospec = pl.BlockSpec((tr, sw), lambda q, i: (i, q))
    wspec = pl.BlockSpec((1, S5_CHUNK * LANES, sw), lambda q, i: (q, 0, 0))
    return pl.pallas_call(
        functools.partial(_s5_w_kernel, tr=tr),
        grid=(nq, r // tr),
        in_specs=[pl.BlockSpec((tr * S5_CHUNK, LANES), lambda q, i: (i, q)), wspec, wspec],
        out_specs=[ospec, ospec],
        out_shape=[out, out],
        compiler_params=_cparams("parallel", "parallel"),
        name="s5_chunk_in",
    )(u2, bre, bim)


def _s5_scan_kernel(wre_ref, wim_ref, lre_ref, lim_ref, x0re_ref, x0im_ref, ore_ref, oim_ref, sre, sim, *, tc):
    @pl.when(pl.program_id(1) == 0)
    def _():
        sre[...] = x0re_ref[0]
        sim[...] = x0im_ref[0]

    lr = lre_ref[...]
    li = lim_ref[...]

    def blk(t, carry):
        xr, xi = carry
        base = pl.multiple_of(t * 8, 8)
        wr = wre_ref[0, pl.ds(base, 8), :]
        wi = wim_ref[0, pl.ds(base, 8), :]
        rows_r, rows_i = [], []
        for r in range(8):
            nr = lr * xr - li * xi + wr[r:r + 1]
            ni = lr * xi + li * xr + wi[r:r + 1]
            xr, xi = nr, ni
            rows_r.append(xr)
            rows_i.append(xi)
        ore_ref[0, pl.ds(base, 8), :] = jnp.concatenate(rows_r, axis=0)
        oim_ref[0, pl.ds(base, 8), :] = jnp.concatenate(rows_i, axis=0)
        return xr, xi

    xr, xi = lax.fori_loop(0, tc // 8, blk, (sre[...], sim[...]))
    sre[...] = xr
    sim[...] = xi


def _s5_scan_call(wre, wim, lre, lim, x0re, x0im, tc):
    b, nch, n = wre.shape
    wspec = pl.BlockSpec((1, tc, n), lambda bb, c: (bb, c, 0))
    lspec = pl.BlockSpec((1, n), lambda bb, c: (0, 0))
    xspec = pl.BlockSpec((1, 1, n), lambda bb, c: (bb, 0, 0))
    out = jax.ShapeDtypeStruct((b, nch, n), F32)
    return pl.pallas_call(
        functools.partial(_s5_scan_kernel, tc=tc),
        grid=(b, nch // tc),
        in_specs=[wspec, wspec, lspec, lspec, xspec, xspec],
        out_specs=[wspec, wspec],
        out_shape=[out, out],
        scratch_shapes=[pltpu.VMEM((1, n), F32), pltpu.VMEM((1, n), F32)],
        compiler_params=_cparams("parallel", "arbitrary"),
        name="s5_scan",
    )(wre, wim, lre, lim, x0re, x0im)


def _s5_y_kernel(u_ref, xre_ref, xim_ref, m_ref, c_ref, y_ref, *, tr):
    L = S5_CHUNK
    urev = jnp.concatenate([_chunk_rows(u_ref, s, tr).astype(BF16) for s in reversed(range(L))], axis=1)
    xcat = jnp.concatenate([xre_ref[...].astype(BF16), xim_ref[...].astype(BF16)], axis=1)
    for t in range(L):
        y = _dot(urev[:, (L - 1 - t) * LANES:], m_ref[0, :(t + 1) * LANES, :]) + _dot(xcat, c_ref[0, t])
        y_ref[pl.ds(t, tr, stride=L), :] = y


def _s5_y_call(u2, xre, xim, mst, cc, tr):
    n, w = u2.shape
    nq = mst.shape[0]
    sw = S5_LB * S5_STATE
    uspec = pl.BlockSpec((tr * S5_CHUNK, LANES), lambda q, i: (i, q))
    xspec = pl.BlockSpec((tr, sw), lambda q, i: (i, q))
    return pl.pallas_call(
        functools.partial(_s5_y_kernel, tr=tr),
        grid=(nq, n // (tr * S5_CHUNK)),
        in_specs=[uspec, xspec, xspec,
                  pl.BlockSpec((1, S5_CHUNK * LANES, LANES), lambda q, i: (q, 0, 0)),
                  pl.BlockSpec((1, S5_CHUNK, 2 * sw, LANES), lambda q, i: (q, 0, 0, 0))],
        out_specs=uspec,
        out_shape=jax.ShapeDtypeStruct((n, w), F32),
        compiler_params=_cparams("parallel", "parallel"),
        name="s5_chunk_out",
    )(u2, xre, xim, mst, cc)


def _s5_mats(a_re, a_im, b_re, b_im, c_re, c_im, log_dt):
    g, p = a_re.shape
    L = S5_CHUNK
    lam = lax.complex(jnp.minimum(a_re, -1e-4), a_im)
    lamdt = lam * jnp.exp(log_dt)[:, None]
    lam_bar = jnp.exp(lamdt)
    b_bar = ((lam_bar - 1.0) / lam)[:, :, None] * lax.complex(b_re, b_im)
    cm = lax.complex(c_re, c_im)
    pw = jnp.exp(lamdt[:, :, None] * jnp.arange(L + 1, dtype=F32))
    mm = pw[:, :, :L, None] * b_bar[:, :, None, :]
    kd = jnp.sum((cm[:, :, :, None, None] * mm[:, None]).real, axis=2).transpose(0, 2, 1, 3)
    bp = pw[:, :, L - 1 - jnp.arange(L)][:, :, :, None] * b_bar[:, :, None, :]
    cp = cm[:, :, :, None] * pw[:, None, :, 1:]
    nq = g // S5_LB
    eye = jnp.eye(S5_LB, dtype=F32)

    def lane_block(m, spec, rows, cols):
        return jnp.einsum(spec, m.reshape((nq, S5_LB) + m.shape[1:]), eye).reshape(nq, rows, cols)

    sw = S5_LB * p
    bre = lane_block(bp.real, "qgpsc,hg->qshcgp", L * LANES, sw)
    bim = lane_block(bp.imag, "qgpsc,hg->qshcgp", L * LANES, sw)
    mst = lane_block(kd, "qgdoc,hg->qdhcgo", L * LANES, LANES)
    cre = lane_block(cp.real, "qgopt,hg->qthpgo", L * sw, LANES).reshape(nq, L, sw, LANES)
    cim = lane_block(-cp.imag, "qgopt,hg->qthpgo", L * sw, LANES).reshape(nq, L, sw, LANES)
    return dict(
        bre=bre.astype(BF16), bim=bim.astype(BF16), mst=mst.astype(BF16),
        cc=jnp.concatenate([cre, cim], axis=2).astype(BF16),
        lre=pw[:, :, L].real.reshape(1, g * p), lim=pw[:, :, L].imag.reshape(1, g * p))


def _ab_out_kernel(x_ref, ys_ref, u_ref, at_ref, d_ref, gw_ref, gb_ref, ws_ref, wa_ref, g1_ref, o_ref):
    y = ys_ref[0] + d_ref[...] * u_ref[0]
    y = jax.nn.gelu(y)
    gate = jax.nn.sigmoid(_dot(y.astype(BF16), gw_ref[...]) + gb_ref[...])
    out = _dot((y * gate).astype(BF16), ws_ref[...]) + _dot(at_ref[0], wa_ref[...])
    o_ref[0] = x_ref[0] + g1_ref[0] * out


def _ab_out_call(x, ys, u, attn, wp, g1, ts):
    b, s, d = x.shape
    ins = [x, ys, u, attn, wp["s5_d"], wp["glu_w"], wp["glu_b"], wp["w_out_s5"], wp["w_out_at"], g1]
    specs = [_tok(d, ts), _tok(S5_WIDTH, ts), _tok(S5_WIDTH, ts), _tok(MLA_HEADS * MLA_V, ts),
             _full(wp["s5_d"]), _full(wp["glu_w"]), _full(wp["glu_b"]), _full(wp["w_out_s5"]),
             _full(wp["w_out_at"]), _modspec(g1, ts)]
    return pl.pallas_call(
        _ab_out_kernel,
        grid=(b, s // ts),
        in_specs=specs,
        out_specs=_tok(d, ts),
        out_shape=jax.ShapeDtypeStruct((b, s, d), F32),
        compiler_params=_cparams("parallel", "parallel"),
        name="ab_out",
    )(*ins)


def _c_in_kernel(x_ref, gain_ref, sc_ref, sh_ref, w_ref, qkv_ref, z_ref, ab_ref):
    h = _normmod(x_ref[0], gain_ref[...], sc_ref[0], sh_ref[0]).astype(BF16)
    proj = _dot(h, w_ref[...])
    qkv_ref[0] = proj[:, :GDN_QKV].astype(BF16)
    z_ref[0] = proj[:, GDN_QKV:GDN_QKV + GDN_VW].astype(BF16)
    ab_ref[0] = proj[:, GDN_QKV + GDN_VW:]


def _c_in_call(x, gain, sc, sh, w, ts):
    b, s, d = x.shape
    return pl.pallas_call(
        _c_in_kernel,
        grid=(b, s // ts),
        in_specs=[_tok(d, ts), _full(gain), _modspec(sc, ts), _modspec(sh, ts), _full(w)],
        out_specs=[_tok(GDN_QKV, ts), _tok(GDN_VW, ts), _tok(LANES, ts)],
        out_shape=[jax.ShapeDtypeStruct((b, s, GDN_QKV), BF16),
                   jax.ShapeDtypeStruct((b, s, GDN_VW), BF16),
                   jax.ShapeDtypeStruct((b, s, LANES), F32)],
        compiler_params=_cparams("parallel", "parallel"),
        name="c_in",
    )(x, gain, sc, sh, w)


def _gdn_prep_kernel(qkv_ref, halo_ref, st_ref, cw_ref, ab_ref, alog_ref, dtb_ref,
                     wq_ref, u_ref, kgt_ref, pm_ref, ee_ref, *, L):
    j = pl.program_id(1)
    prev = jnp.where(j == 0, st_ref[0].astype(F32), halo_ref[0].astype(F32))
    xx = jnp.concatenate([prev, qkv_ref[0].astype(F32)], axis=0)
    cw = cw_ref[...]
    y = xx[8:8 + L] * cw[CONV_W - 1:CONV_W]
    for t in range(1, CONV_W):
        y += xx[8 - t:8 - t + L] * cw[CONV_W - 1 - t:CONV_W - t]
    y = y * jax.nn.sigmoid(y)

    ab = ab_ref[0]
    g_all = -jnp.exp(alog_ref[...]) * jax.nn.softplus(ab + dtb_ref[...])
    beta_all = jax.nn.sigmoid(ab)
    row = lax.broadcasted_iota(jnp.int32, (L, L), 0)
    col = lax.broadcasted_iota(jnp.int32, (L, L), 1)
    incl = row >= col
    strict = row > col
    gam_all = jnp.dot(incl.astype(F32), g_all, preferred_element_type=F32, precision=HI)
    eye_l = (lax.broadcasted_iota(jnp.int32, (LANES, LANES), 0)
             == lax.broadcasted_iota(jnp.int32, (LANES, LANES), 1))
    gam_t = _dot_nt(eye_l.astype(F32), gam_all, precision=HI)
    ee_ref[0, 0] = jnp.exp(gam_all[L - 1:L])
    eye64 = (row == col).astype(F32)
    eye_bf = eye_l.astype(BF16)

    hs = range(GDN_HEADS)
    koff, voff = GDN_HEADS * GDN_DK, 2 * GDN_HEADS * GDN_DK
    qh = [y[:, GDN_DK * h:GDN_DK * (h + 1)] for h in hs]
    kh = [y[:, koff + GDN_DK * h:koff + GDN_DK * (h + 1)] for h in hs]
    vh = [y[:, voff + GDN_DV * h:voff + GDN_DV * (h + 1)] for h in hs]
    qh = [q * lax.rsqrt(jnp.sum(q * q, axis=-1, keepdims=True) + EPS) * (GDN_DK ** -0.5) for q in qh]
    kh = [k * lax.rsqrt(jnp.sum(k * k, axis=-1, keepdims=True) + EPS) for k in kh]
    gam_c = [gam_all[:, h:h + 1] for h in hs]
    beta_c = [beta_all[:, GDN_HEADS + h:GDN_HEADS + h + 1] for h in hs]
    dec = [jnp.exp(jnp.minimum(gam_c[h] - gam_t[h:h + 1, :], 0.0)) for h in hs]
    kb = [k.astype(BF16) for k in kh]
    kk = [_dot_nt(kb[h], kb[h]) for h in hs]
    qk = [_dot_nt(qh[h].astype(BF16), kb[h]) for h in hs]
    a = [jnp.where(strict, beta_c[h] * kk[h] * dec[h], 0.0) for h in hs]
    for h in hs:
        pm_ref[0, 0, h] = jnp.where(incl, qk[h] * dec[h], 0.0).astype(BF16)
    tinv = [eye64 - a[h] for h in hs]
    pw = [_split(a[h]) for h in hs]
    for _ in range(max(L.bit_length() - 2, 0)):
        pw = [_split(_dot3(pw[h], pw[h])) for h in hs]
        tinv = [tinv[h] + _dot3(_split(tinv[h]), pw[h]) for h in hs]
    eg = [jnp.exp(gam_c[h]) for h in hs]
    rhs = [jnp.concatenate([(beta_c[h] * eg[h]) * kh[h], beta_c[h] * vh[h]], axis=1).astype(BF16) for h in hs]
    wu = [_dot(tinv[h].astype(BF16), rhs[h]) for h in hs]
    for h in hs:
        wq_ref[0, 0, h] = jnp.concatenate([wu[h][:, :GDN_DK], qh[h] * eg[h]], axis=0).astype(BF16)
        u_ref[0, h] = wu[h][:, GDN_DK:].astype(BF16)
    kg = [(kh[h] * jnp.exp(gam_all[L - 1:L, h:h + 1] - gam_c[h])).astype(BF16) for h in hs]
    kgt = [_dot_nt(eye_bf, kg[h]) for h in hs]
    for h in hs:
        kgt_ref[0, 0, h] = kgt[h].astype(BF16)


def _split(a):
    hi = a.astype(BF16)
    return hi, (a - hi.astype(F32)).astype(BF16)


def _dot3(a, b):
    return _dot(a[0], b[0]) + (_dot(a[0], b[1]) + _dot(a[1], b[0]))


def _gdn_prep_call(qkv, st8, conv_w, ab, alog, dtb, L):
    b, s, _ = qkv.shape
    nch = s // L
    return pl.pallas_call(
        functools.partial(_gdn_prep_kernel, L=L),
        grid=(b, nch),
        in_specs=[_tok(GDN_QKV, L),
                  pl.BlockSpec((1, 8, GDN_QKV), lambda bb, j: (bb, jnp.maximum(j * (L // 8) - 1, 0), 0)),
                  pl.BlockSpec((1, 8, GDN_QKV), lambda bb, j: (bb, 0, 0)),
                  _full(conv_w), _tok(LANES, L), _full(alog), _full(dtb)],
        out_specs=[pl.BlockSpec((1, 1, GDN_HEADS, 2 * L, LANES), lambda bb, j: (bb, j, 0, 0, 0)),
                   pl.BlockSpec((1, GDN_HEADS, L, LANES), lambda bb, j: (bb, 0, j, 0)),
                   pl.BlockSpec((1, 1, GDN_HEADS, GDN_DK, L), lambda bb, j: (bb, j, 0, 0, 0)),
                   pl.BlockSpec((1, 1, GDN_HEADS, L, L), lambda bb, j: (bb, j, 0, 0, 0)),
                   pl.BlockSpec((1, 1, 1, LANES), lambda bb, j: (bb, j, 0, 0))],
        out_shape=[jax.ShapeDtypeStruct((b, nch, GDN_HEADS, 2 * L, LANES), BF16),
                   jax.ShapeDtypeStruct((b, GDN_HEADS, s, LANES), BF16),
                   jax.ShapeDtypeStruct((b, nch, GDN_HEADS, GDN_DK, L), BF16),
                   jax.ShapeDtypeStruct((b, nch, GDN_HEADS, L, L), BF16),
                   jax.ShapeDtypeStruct((b, nch, 1, LANES), F32)],
        compiler_params=_cparams("parallel", "parallel"),
        name="gdn_prep",
    )(qkv, qkv, st8, conv_w, ab, alog, dtb)


GDN_BATCH_GROUP = 2


def _gdn_seq_kernel(wq_ref, u_ref, kgt_ref, pm_ref, ee_ref, s0_ref, o_ref, sf_ref, s_scr, *, L):
    c = pl.program_id(1)

    @pl.when(c == 0)
    def _():
        s_scr[...] = s0_ref[...]

    ch = [(bb, hh) for bb in range(GDN_BATCH_GROUP) for hh in range(GDN_HEADS)]
    st = [s_scr[bb, hh] for bb, hh in ch]
    sp = [_split(s) for s in st]
    r = [_dot(wq_ref[bb, 0, hh], sp[i][0]) + _dot(wq_ref[bb, 0, hh], sp[i][1]) for i, (bb, hh) in enumerate(ch)]
    v_new = [(u_ref[bb, hh].astype(F32) - r[i][:L]).astype(BF16) for i, (bb, hh) in enumerate(ch)]
    o = [r[i][L:] + _dot(pm_ref[bb, 0, hh], v_new[i]) for i, (bb, hh) in enumerate(ch)]
    for i, (bb, hh) in enumerate(ch):
        o_ref[bb, :, GDN_DV * hh:GDN_DV * (hh + 1)] = o[i].astype(o_ref.dtype)
    s_new = [ee_ref[bb, 0][:, hh:hh + 1] * st[i] + _dot(kgt_ref[bb, 0, hh], v_new[i]) for i, (bb, hh) in enumerate(ch)]
    for i, (bb, hh) in enumerate(ch):
        s_scr[bb, hh] = s_new[i]

    @pl.when(c == pl.num_programs(1) - 1)
    def _():
        sf_ref[...] = s_scr[...]


def _gdn_seq_call(wq, um, kgt, pm, ee, s0, L):
    b, h, s, _ = um.shape
    nch = s // L
    bg = GDN_BATCH_GROUP
    sspec = pl.BlockSpec((bg, h, GDN_DK, GDN_DV), lambda g, c: (g, 0, 0, 0))
    return pl.pallas_call(
        functools.partial(_gdn_seq_kernel, L=L),
        grid=(b // bg, nch),
        in_specs=[pl.BlockSpec((bg, 1, h, 2 * L, LANES), lambda g, c: (g, c, 0, 0, 0)),
                  pl.BlockSpec((bg, h, L, LANES), lambda g, c: (g, 0, c, 0)),
                  pl.BlockSpec((bg, 1, h, GDN_DK, L), lambda g, c: (g, c, 0, 0, 0)),
                  pl.BlockSpec((bg, 1, h, L, L), lambda g, c: (g, c, 0, 0, 0)),
                  pl.BlockSpec((bg, 1, 1, LANES), lambda g, c: (g, c, 0, 0)),
                  sspec],
        out_specs=[pl.BlockSpec((bg, L, GDN_VW), lambda g, c: (g, c, 0)), sspec],
        out_shape=[jax.ShapeDtypeStruct((b, s, GDN_VW), BF16),
                   jax.ShapeDtypeStruct((b, h, GDN_DK, GDN_DV), F32)],
        scratch_shapes=[pltpu.VMEM((bg, h, GDN_DK, GDN_DV), F32)],
        compiler_params=_cparams("parallel", "arbitrary"),
        name="gdn_seq",
    )(wq, um, kgt, pm, ee, s0)


def _c_out_kernel(x_ref, o_ref, z_ref, gn_ref, w_ref, g1_ref, out_ref):
    o = o_ref[0].astype(F32)
    z = z_ref[0].astype(F32)
    gn = gn_ref[...]
    parts = []
    for hh in range(GDN_HEADS):
        oh = o[:, GDN_DV * hh:GDN_DV * (hh + 1)]
        zh = z[:, GDN_DV * hh:GDN_DV * (hh + 1)]
        parts.append((_rms(oh, gn) * (zh * jax.nn.sigmoid(zh))).astype(BF16))
    out = _dot(jnp.concatenate(parts, axis=1), w_ref[...])
    out_ref[0] = x_ref[0] + g1_ref[0] * out


def _c_out_call(x, o, z, gn, w, g1, ts):
    b, s, d = x.shape
    return pl.pallas_call(
        _c_out_kernel,
        grid=(b, s // ts),
        in_specs=[_tok(d, ts), _tok(GDN_VW, ts), _tok(GDN_VW, ts), _full(gn), _full(w), _modspec(g1, ts)],
        out_specs=_tok(d, ts),
        out_shape=jax.ShapeDtypeStruct((b, s, d), F32),
        compiler_params=_cparams("parallel", "parallel"),
        name="c_out",
    )(x, o, z, gn, w, g1)


def _pack_halves(v):
    n = v.shape[1] // 2
    bits = pltpu.bitcast(v.astype(BF16).astype(F32), jnp.uint32)
    return (bits[:, :n] >> 16) | (bits[:, n:] & jnp.uint32(0xFFFF0000))


def _unpack_halves(w):
    return pltpu.bitcast(w << 16, F32), pltpu.bitcast(w & jnp.uint32(0xFFFF0000), F32)


def _router_kernel(x_ref, gain_ref, sc_ref, sh_ref, wr_ref, br_ref, h_ref, re_ref, rg_ref, cnt_ref, carry):
    @pl.when(jnp.logical_and(pl.program_id(0) == 0, pl.program_id(1) == 0))
    def _():
        carry[...] = jnp.zeros_like(carry)

    h = _normmod(x_ref[0], gain_ref[...], sc_ref[0], sh_ref[0])
    h_ref[0] = _pack_halves(h)
    logits = jnp.dot(h, wr_ref[...], preferred_element_type=F32, precision=HI) + br_ref[...]
    lane = lax.broadcasted_iota(jnp.int32, logits.shape, 1).astype(F32)
    neg = jnp.float32(-jnp.inf)
    big = jnp.float32(1e9)
    gl = jnp.where(lane < N_GROUPS, logits, neg)
    gm = jnp.max(gl, axis=-1, keepdims=True)
    g_val = 1.0 / jnp.sum(jnp.exp(gl - gm), axis=-1, keepdims=True)
    g_idx = jnp.min(jnp.where(gl == gm, lane, big), axis=-1, keepdims=True)
    lo = N_GROUPS + EXPERTS_PER_GROUP * g_idx
    el = jnp.where(jnp.logical_and(lane >= lo, lane < lo + EXPERTS_PER_GROUP), logits, neg)
    em = jnp.max(el, axis=-1, keepdims=True)
    i1 = jnp.min(jnp.where(el == em, lane, big), axis=-1, keepdims=True)
    el2 = jnp.where(lane == i1, neg, el)
    em2 = jnp.max(el2, axis=-1, keepdims=True)
    i2 = jnp.min(jnp.where(el2 == em2, lane, big), axis=-1, keepdims=True)
    es = jnp.sum(jnp.exp(el - em), axis=-1, keepdims=True)
    p1 = 1.0 / es
    p2 = jnp.exp(em2 - em) / es
    den = p1 + p2
    rg_ref[0] = jnp.where(lane == 0, g_val * p1 / den, jnp.where(lane == 1, g_val * p2 / den, 0.0))
    oh1 = lane == i1
    oh2 = lane == i2
    both = jnp.where(jnp.logical_or(oh1, oh2), 1.0, 0.0)
    ts = both.shape[0]
    tri = (lax.broadcasted_iota(jnp.int32, (ts, ts), 0) > lax.broadcasted_iota(jnp.int32, (ts, ts), 1))
    pre = _dot(jnp.where(tri, 1.0, 0.0).astype(BF16), both.astype(BF16)) + carry[...]
    r1 = jnp.sum(jnp.where(oh1, pre, 0.0), axis=-1, keepdims=True)
    r2 = jnp.sum(jnp.where(oh2, pre, 0.0), axis=-1, keepdims=True)
    re_ref[0] = jnp.where(lane == 0, i1 - N_GROUPS, jnp.where(lane == 1, i2 - N_GROUPS, jnp.where(
        lane == 2, r1, jnp.where(lane == 3, r2, 0.0)))).astype(jnp.int32)
    carry[...] += jnp.sum(both, axis=0, keepdims=True)
    cnt_ref[...] = carry[...]


def _router_call(x, gain, sc, sh, wr, br, ts):
    b, s, d = x.shape
    return pl.pallas_call(
        _router_kernel,
        grid=(b, s // ts),
        in_specs=[_tok(d, ts), _full(gain), _modspec(sc, ts), _modspec(sh, ts), _full(wr), _full(br)],
        out_specs=[_tok(d // 2, ts), _tok(LANES, ts), _tok(LANES, ts), pl.BlockSpec((1, LANES), lambda bb, j: (0, 0))],
        out_shape=[jax.ShapeDtypeStruct((b, s, d // 2), jnp.uint32),
                   jax.ShapeDtypeStruct((b, s, LANES), jnp.int32),
                   jax.ShapeDtypeStruct((b, s, LANES), F32),
                   jax.ShapeDtypeStruct((1, LANES), F32)],
        scratch_shapes=[pltpu.VMEM((1, LANES), F32)],
        compiler_params=_cparams("arbitrary", "arbitrary"),
        name="moe_router",
    )(x, gain, sc, sh, wr, br)


def _row_copy(src, dst, src_row, dst_row, sem):
    return pltpu.make_async_copy(src.at[pl.ds(src_row, 1)], dst.at[pl.ds(dst_row, 1)], sem)


ROUTE_W = 2 * TOP_K
DMA_UNROLL = 8


def _slot(ps_ref, rt_ref, r, k):
    return ps_ref[rt_ref[0, 0, ROUTE_W * r + k]] + rt_ref[0, 0, ROUTE_W * r + TOP_K + k]


def _dispatch_kernel(ps_ref, rt_ref, h_ref, xin_hbm, xb_hbm, sem, *, tb):
    del xin_hbm

    def issue(r, c):
        for k in range(TOP_K):
            _row_copy(h_ref, xb_hbm, r, _slot(ps_ref, rt_ref, r, k), sem).start()
        return c
    lax.fori_loop(0, tb, issue, 0, unroll=DMA_UNROLL)
    for k in range(TOP_K):
        pltpu.make_async_copy(h_ref, xb_hbm.at[pl.ds(0, tb)], sem).wait()


def _dispatch_call(pad_start, route, h2, xb0, tb):
    n, dh = h2.shape
    nb = n // tb
    grid_spec = pltpu.PrefetchScalarGridSpec(
        num_scalar_prefetch=1,
        grid=(nb,),
        in_specs=[pl.BlockSpec((1, 1, ROUTE_W * tb), lambda j, ps: (j, 0, 0), memory_space=pltpu.SMEM),
                  pl.BlockSpec((tb, dh), lambda j, ps: (j, 0)),
                  pl.BlockSpec(memory_space=pl.ANY)],
        out_specs=pl.BlockSpec(memory_space=pl.ANY),
        scratch_shapes=[pltpu.SemaphoreType.DMA(())],
    )
    return pl.pallas_call(
        functools.partial(_dispatch_kernel, tb=tb),
        grid_spec=grid_spec,
        out_shape=jax.ShapeDtypeStruct(xb0.shape, xb0.dtype),
        input_output_aliases={3: 0},
        compiler_params=_cparams("arbitrary"),
        name="moe_dispatch",
    )(pad_start, route, h2, xb0)


def _ffn_kernel(be_ref, nu_ref, x_ref, w1_ref, w3_ref, w2_ref, y_ref, w1b, w3b, w2b):
    i = pl.program_id(0)
    half = w1b.shape[0] // 2

    @pl.when(jnp.logical_and(i < nu_ref[0], jnp.logical_or(i == 0, be_ref[i] != be_ref[jnp.maximum(i - 1, 0)])))
    def _():
        w1b[...] = w1_ref[0, 0].astype(BF16)
        w3b[...] = w3_ref[0, 0].astype(BF16)
        w2b[...] = w2_ref[0, 0].astype(BF16)

    @pl.when(i < nu_ref[0])
    def _():
        lo, hi = _unpack_halves(x_ref[...])
        lo = lo.astype(BF16)
        hi = hi.astype(BF16)
        a = _dot(lo, w1b[:half]) + _dot(hi, w1b[half:])
        g = _dot(lo, w3b[:half]) + _dot(hi, w3b[half:])
        mid = (a * jax.nn.sigmoid(a) * g).astype(BF16)
        y_ref[...] = _pack_halves(_dot(mid, w2b[...]))

    @pl.when(i >= nu_ref[0])
    def _():
        y_ref[...] = jnp.zeros_like(y_ref)


def _ffn_call(block_e, n_used, xb, w1, w3, w2, layer):
    n_blocks = block_e.shape[0]
    dh = xb.shape[1]
    d = 2 * dh
    grid_spec = pltpu.PrefetchScalarGridSpec(
        num_scalar_prefetch=2,
        grid=(n_blocks,),
        in_specs=[pl.BlockSpec((MOE_BLOCK, dh), lambda i, be, nu: (i, 0)),
                  pl.BlockSpec((1, 1, d, D_EXPERT), lambda i, be, nu: (layer, be[i], 0, 0)),
                  pl.BlockSpec((1, 1, d, D_EXPERT), lambda i, be, nu: (layer, be[i], 0, 0)),
                  pl.BlockSpec((1, 1, D_EXPERT, d), lambda i, be, nu: (layer, be[i], 0, 0))],
        out_specs=pl.BlockSpec((MOE_BLOCK, dh), lambda i, be, nu: (i, 0)),
        scratch_shapes=[pltpu.VMEM((d, D_EXPERT), BF16), pltpu.VMEM((d, D_EXPERT), BF16),
                        pltpu.VMEM((D_EXPERT, d), BF16)],
    )
    return pl.pallas_call(
        _ffn_kernel,
        grid_spec=grid_spec,
        out_shape=jax.ShapeDtypeStruct((n_blocks * MOE_BLOCK, dh), jnp.uint32),
        compiler_params=_cparams("arbitrary"),
        name="moe_ffn",
    )(block_e, n_used, xb, w1, w3, w2)


def _combine_kernel(ps_ref, rcur_ref, rnxt_ref, yb_hbm, x_ref, rg_ref, g2_ref, fin_ref, o_ref, b0, b1, sem,
                    *, tb, final):
    j = pl.program_id(0)
    nb = pl.num_programs(0)
    slot = j % 2

    def fetch(rref, sl):
        def issue(r, c):
            _row_copy(yb_hbm, b0.at[sl], _slot(ps_ref, rref, r, 0), r, sem.at[sl]).start()
            _row_copy(yb_hbm, b1.at[sl], _slot(ps_ref, rref, r, 1), r, sem.at[sl]).start()
            return c
        lax.fori_loop(0, tb, issue, 0, unroll=DMA_UNROLL)

    @pl.when(j == 0)
    def _():
        fetch(rcur_ref, 0)

    @pl.when(j + 1 < nb)
    def _():
        fetch(rnxt_ref, 1 - slot)

    pltpu.make_async_copy(yb_hbm.at[pl.ds(0, tb)], b0.at[slot], sem.at[slot]).wait()
    pltpu.make_async_copy(yb_hbm.at[pl.ds(0, tb)], b1.at[slot], sem.at[slot]).wait()
    rg = rg_ref[...]
    lo0, hi0 = _unpack_halves(b0[slot])
    lo1, hi1 = _unpack_halves(b1[slot])
    g0 = rg[:, 0:1]
    g1 = rg[:, 1:2]
    y = jnp.concatenate([g0 * lo0 + g1 * lo1, g0 * hi0 + g1 * hi1], axis=1)
    xn = x_ref[...] + g2_ref[0] * y
    if final:
        xn = _rms(xn, fin_ref[...])
    o_ref[...] = xn


def _combine_call(pad_start, route, yb, x, rg, g2, fin, tb, final):
    b, s, d = x.shape
    n = b * s
    nb = n // tb
    npb = s // tb
    dh = yb.shape[1]
    if g2.shape[1] == 1:
        gspec = pl.BlockSpec((1, 1, d), lambda j, ps: (j // npb, 0, 0))
    else:
        gspec = pl.BlockSpec((1, tb, d), lambda j, ps: (j // npb, j % npb, 0))
    grid_spec = pltpu.PrefetchScalarGridSpec(
        num_scalar_prefetch=1,
        grid=(nb,),
        in_specs=[pl.BlockSpec((1, 1, ROUTE_W * tb), lambda j, ps: (j, 0, 0), memory_space=pltpu.SMEM),
                  pl.BlockSpec((1, 1, ROUTE_W * tb), lambda j, ps: (jnp.minimum(j + 1, nb - 1), 0, 0),
                               memory_space=pltpu.SMEM),
                  pl.BlockSpec(memory_space=pl.ANY),
                  pl.BlockSpec((tb, d), lambda j, ps: (j, 0)), pl.BlockSpec((tb, LANES), lambda j, ps: (j, 0)),
                  gspec, pl.BlockSpec(fin.shape, lambda j, ps: (0, 0))],
        out_specs=pl.BlockSpec((tb, d), lambda j, ps: (j, 0)),
        scratch_shapes=[pltpu.VMEM((2, tb, dh), jnp.uint32), pltpu.VMEM((2, tb, dh), jnp.uint32),
                        pltpu.SemaphoreType.DMA((2,))],
    )
    out = pl.pallas_call(
        functools.partial(_combine_kernel, tb=tb, final=final),
        grid_spec=grid_spec,
        out_shape=jax.ShapeDtypeStruct((n, d), F32),
        compiler_params=_cparams("arbitrary"),
        name="moe_combine",
    )(pad_start, route, route, yb, x.reshape(n, d), rg.reshape(n, LANES), g2, fin)
    return out.reshape(b, s, d)


def _moe(x, gain, sc, sh, g2, mp, fin, ts, final):
    b, s, d = x.shape
    n = b * s
    h2, r_e, r_g, cnt = _router_call(x, gain, sc, sh, mp["wr"], mp["br"], ts)
    counts = cnt[0, N_GROUPS:N_GROUPS + N_EXPERTS].astype(jnp.int32)
    padded = (counts + MOE_BLOCK - 1) // MOE_BLOCK * MOE_BLOCK
    pad_end = jnp.cumsum(padded)
    pad_start = (pad_end - padded).astype(jnp.int32)
    n_blocks = (n * TOP_K + N_EXPERTS * (MOE_BLOCK - 1) + MOE_BLOCK - 1) // MOE_BLOCK
    blk0 = jnp.arange(n_blocks, dtype=jnp.int32) * MOE_BLOCK
    block_e = jnp.minimum(jnp.sum((pad_end[None, :] <= blk0[:, None]).astype(jnp.int32), axis=1), N_EXPERTS - 1)
    n_used = (pad_end[-1:] // MOE_BLOCK).astype(jnp.int32)
    route = r_e[:, :, :ROUTE_W].reshape(n // ts, 1, ROUTE_W * ts)
    xb0 = jnp.zeros((n_blocks * MOE_BLOCK, d // 2), jnp.uint32)
    xb = _dispatch_call(pad_start, route, h2.reshape(n, d // 2), xb0, ts)
    yb = _ffn_call(block_e, n_used, xb, mp["w1"], mp["w3"], mp["w2"], mp["layer"])
    return _combine_call(pad_start, route, yb, x, r_g, g2, fin, ts, final)


def _prep_ab(l, p):
    w_in = p["w_in_ab"][l]
    o2 = S5_WIDTH + MLA_Q_RANK + MLA_KV_RANK
    half = MLA_ROPE // 2
    d = w_in.shape[0]
    kr = w_in[:, o2:]
    kr_rot = jnp.concatenate([-kr[:, half:], kr[:, :half]], axis=1)
    z64 = jnp.zeros((d, MLA_NOPE), F32)
    z32 = jnp.zeros((d, LANES - MLA_NOPE - MLA_ROPE), F32)
    w_ext = jnp.concatenate([w_in[:, :o2], z64, kr, z32, z64, kr_rot, z32], axis=1)
    qu = p["mla_q_up"][l].reshape(MLA_Q_RANK, MLA_HEADS, MLA_NOPE + MLA_ROPE)
    qn, qr = qu[:, :, :MLA_NOPE], qu[:, :, MLA_NOPE:]
    zq = jnp.zeros((MLA_Q_RANK, MLA_HEADS, LANES - MLA_NOPE - MLA_ROPE), F32)
    wqa = jnp.concatenate([qn, qr, zq], axis=2).reshape(MLA_Q_RANK, MLA_HEADS * LANES)
    wqb = jnp.concatenate([jnp.zeros_like(qn), -qr[:, :, half:], qr[:, :, :half], zq], axis=2)
    wqb = wqb.reshape(MLA_Q_RANK, MLA_HEADS * LANES)
    kvu = p["mla_kv_up"][l].reshape(MLA_KV_RANK, MLA_HEADS, MLA_NOPE + MLA_V)
    wk = jnp.concatenate([kvu[:, :, :MLA_NOPE], jnp.zeros((MLA_KV_RANK, MLA_HEADS, LANES - MLA_NOPE), F32)], axis=2)
    wo = p["w_out_ab"][l]
    out = dict(
        w_in=w_ext.astype(BF16), q_norm=p["mla_q_norm"][l].reshape(1, -1), kv_norm=p["mla_kv_norm"][l].reshape(1, -1),
        wqa=wqa.astype(BF16), wqb=wqb.astype(BF16),
        wk=wk.reshape(MLA_KV_RANK, MLA_HEADS * LANES).astype(BF16),
        wv=jnp.concatenate([kvu[:, :, MLA_NOPE:], jnp.zeros((MLA_KV_RANK, MLA_HEADS, LANES - MLA_V), F32)],
                           axis=2).reshape(MLA_KV_RANK, MLA_HEADS * LANES).astype(BF16),
        s5_d=p["s5_d"][l].reshape(1, -1), glu_w=p["s5_glu_w"][l].astype(BF16), glu_b=p["s5_glu_b"][l].reshape(1, -1),
        w_out_s5=wo[:S5_WIDTH].astype(BF16),
        w_out_at=wo[S5_WIDTH:].astype(BF16))
    out.update(_s5_mats(p["s5_a_re"][l], p["s5_a_im"][l], p["s5_b_re"][l], p["s5_b_im"][l],
                        p["s5_c_re"][l], p["s5_c_im"][l], p["s5_log_dt"][l]))
    return out


def _prep_c(l, p):
    w = p["w_in_c"][l]
    pad = jnp.zeros((w.shape[0], LANES - 2 * GDN_HEADS), F32)

    def row(v):
        return jnp.concatenate([v, jnp.zeros((LANES - GDN_HEADS,), F32)]).reshape(1, LANES)

    return dict(w_in=jnp.concatenate([w, pad], axis=1).astype(BF16), conv_w=p["conv_w"][l],
                alog=row(p["gdn_a_log"][l]), dtb=row(p["gdn_dt_bias"][l]),
                gn=p["gdn_norm"][l].reshape(1, -1), w_out=p["w_out_c"][l].astype(BF16))


def _prep_moe(layer, p):
    d = p["moe_w_group"].shape[1]
    wr = jnp.concatenate([p["moe_w_group"][layer], p["moe_w_expert"][layer],
                          jnp.zeros((d, LANES - N_GROUPS - N_EXPERTS), F32)], axis=1)
    br = jnp.concatenate([p["moe_b_group"][layer], p["moe_b_expert"][layer],
                          jnp.zeros((LANES - N_GROUPS - N_EXPERTS,), F32)]).reshape(1, LANES)
    return dict(wr=wr, br=br, w1=p["moe_w1"], w3=p["moe_w3"], w2=p["moe_w2"], layer=layer)


def _rope_tables(pos):
    half = MLA_ROPE // 2
    inv = ROPE_THETA ** (-jnp.arange(half, dtype=F32) / half)
    ang = pos.astype(F32)[:, None] * inv[None, :]
    cos, sin = jnp.cos(ang), jnp.sin(ang)
    n = pos.shape[0]
    z64 = jnp.zeros((n, MLA_NOPE), F32)
    z32 = jnp.zeros((n, LANES - MLA_NOPE - MLA_ROPE), F32)
    scale = (MLA_NOPE + MLA_ROPE) ** -0.5 * math.log2(math.e)
    ck = jnp.concatenate([z64, cos, cos, z32], axis=1)
    sk = jnp.concatenate([z64, sin, sin, z32], axis=1)
    cq = jnp.concatenate([jnp.ones_like(z64), cos, cos, z32], axis=1) * scale
    return cq, sk * scale, ck, sk


def _trunk(x, mods, caches, wab, wc, wmoe, p, flat):
    b, s, d = x.shape
    lat_c, rope_c, s5re_c, s5im_c, conv_c, gdn_c = caches
    past = 0 if lat_c is None else lat_c.shape[2]
    pos = past + jnp.arange(s)
    if flat:
        xt = x.reshape(1, b * s, d)
        ts = b * s
        pos_rows = jnp.tile(pos, b)

        def mrow(m):
            return jnp.repeat(m, s, axis=0).reshape(1, b * s, d)
    else:
        xt = x
        ts = 256
        pos_rows = pos

        def mrow(m):
            return m.reshape(b, 1, d)
    bt, st = xt.shape[:2]
    outs = {}
    depth = p["norm_mix"].shape[0]
    for layer in range(depth):
        l = layer // 2
        sh1, sc1, g1, sh2, sc2, g2 = [mrow(m) for m in jnp.split(mods[layer], 6, axis=-1)]
        gain = p["norm_mix"][layer].reshape(1, d)
        if layer % 2 == 0:
            wp = wab[l]
            u, q, lat, kr = _ab_in_call(xt, gain, sc1, sh1, wp, _rope_tables(pos_rows), ts)
            lat_b = lat.reshape(b, s, MLA_KV_RANK)
            kr_b = kr.reshape(b, s, LANES)
            outs["lat"] = lat_b
            outs["krope"] = kr_b[:, :, MLA_NOPE:MLA_NOPE + MLA_ROPE]
            if lat_c is None:
                lat_all, kr_all, sk_valid, tkv, tq, tk = lat_b, kr_b, s, 256, min(512, s), min(512, s)
            else:
                krc = jnp.pad(rope_c[l], ((0, 0), (0, 0), (MLA_NOPE, LANES - MLA_NOPE - MLA_ROPE)))
                sk_valid = past + s
                skp = (sk_valid + LANES - 1) // LANES * LANES
                lat_all = jnp.pad(jnp.concatenate([lat_c[l], lat_b], axis=1), ((0, 0), (0, skp - sk_valid), (0, 0)))
                kr_all = jnp.pad(jnp.concatenate([krc, kr_b], axis=1), ((0, 0), (0, skp - sk_valid), (0, 0)))
                tkv, tq, tk = skp, s, skp
            kk, vv = _kv_call(lat_all, kr_all, wp, tkv)
            qh = q.reshape(MLA_HEADS, b, s, LANES).transpose(1, 0, 2, 3) if flat else q
            attn = _attn_call(qh, kk, vv, tq, tk, past, sk_valid).reshape(bt, st, MLA_HEADS * MLA_V)
            nchk = s // S5_CHUNK
            r = b * nchk
            u2 = u.reshape(b * s, S5_WIDTH)
            tr = min(r, 256)
            wre, wim = _s5_w_call(u2, wp["bre"], wp["bim"], tr)
            ngp = S5_GROUPS * S5_STATE

            def to_pairs(v):
                return v.reshape(b, 1, ngp)

            if s5re_c is None:
                x0re = jnp.zeros((b, 1, ngp), F32)
                x0im = jnp.zeros((b, 1, ngp), F32)
            else:
                x0re, x0im = to_pairs(s5re_c[l]), to_pairs(s5im_c[l])
            nchp = (nchk + 7) // 8 * 8
            tc = min(nchp, 128)
            wre3 = jnp.pad(wre.reshape(b, nchk, ngp), ((0, 0), (0, nchp - nchk), (0, 0)))
            wim3 = jnp.pad(wim.reshape(b, nchk, ngp), ((0, 0), (0, nchp - nchk), (0, 0)))
            xere, xeim = _s5_scan_call(wre3, wim3, wp["lre"], wp["lim"], x0re, x0im, tc)
            outs["s5re"] = xere[:, nchk - 1].reshape(b, S5_GROUPS, S5_STATE)
            outs["s5im"] = xeim[:, nchk - 1].reshape(b, S5_GROUPS, S5_STATE)
            xsre = jnp.concatenate([x0re, xere[:, :nchk - 1]], axis=1).reshape(r, ngp)
            xsim = jnp.concatenate([x0im, xeim[:, :nchk - 1]], axis=1).reshape(r, ngp)
            ys = _s5_y_call(u2, xsre, xsim, wp["mst"], wp["cc"], tr).reshape(bt, st, S5_WIDTH)
            xt = _ab_out_call(xt, ys, u, attn, wp, g1, ts)
        else:
            wp = wc[l]
            qkv, z, ab = _c_in_call(xt, gain, sc1, sh1, wp["w_in"], ts)
            qkv_b = qkv.reshape(b, s, GDN_QKV)
            outs["conv"] = qkv_b[:, s - (CONV_W - 1):].astype(F32)
            if conv_c is None:
                st8 = jnp.zeros((b, 8, GDN_QKV), BF16)
                s0 = jnp.zeros((b, GDN_HEADS, GDN_DK, GDN_DV), F32)
            else:
                st8 = jnp.pad(conv_c[l], ((0, 0), (8 - (CONV_W - 1), 0), (0, 0))).astype(BF16)
                s0 = gdn_c[l]
            lg = min(s, CHUNK)
            wq, um, kgt, pm, ee = _gdn_prep_call(qkv_b, st8, wp["conv_w"], ab.reshape(b, s, LANES),
                                                 wp["alog"], wp["dtb"], lg)
            o, sfin = _gdn_seq_call(wq, um, kgt, pm, ee, s0, lg)
            outs["gdn"] = sfin
            xt = _c_out_call(xt, o.reshape(bt, st, GDN_VW), z, wp["gn"], wp["w_out"], g1, ts)
        final = layer == depth - 1
        xt = _moe(xt, p["norm_ffn"][layer].reshape(1, d), sc2, sh2, g2, wmoe[layer],
                  p["norm_final"].reshape(1, d), ts, final)
    y = xt.reshape(b, s, d)
    return (y, outs["lat"][None], outs["krope"][None], outs["s5re"][None], outs["s5im"][None],
            outs["conv"][None], outs["gdn"][None])


def kernel(x_prompt, x_sample, c_prompt, c_sample, cache_mla_latent, cache_mla_krope, state_s5_re, state_s5_im,
           state_conv, state_gdn, w_mod, b_mod, norm_mix, norm_ffn, norm_final, w_in_ab, s5_a_re, s5_a_im,
           s5_b_re, s5_b_im, s5_c_re, s5_c_im, s5_d, s5_log_dt, s5_glu_w, s5_glu_b, mla_q_norm, mla_q_up,
           mla_kv_norm, mla_kv_up, w_out_ab, w_in_c, conv_w, gdn_a_log, gdn_dt_bias, gdn_norm, w_out_c,
           moe_w_group, moe_b_group, moe_w_expert, moe_b_expert, moe_w1, moe_w3, moe_w2):
    p = dict(w_mod=w_mod, b_mod=b_mod, norm_mix=norm_mix, norm_ffn=norm_ffn, norm_final=norm_final,
             w_in_ab=w_in_ab, s5_a_re=s5_a_re, s5_a_im=s5_a_im, s5_b_re=s5_b_re, s5_b_im=s5_b_im,
             s5_c_re=s5_c_re, s5_c_im=s5_c_im, s5_d=s5_d, s5_log_dt=s5_log_dt, s5_glu_w=s5_glu_w,
             s5_glu_b=s5_glu_b, mla_q_norm=mla_q_norm, mla_q_up=mla_q_up, mla_kv_norm=mla_kv_norm,
             mla_kv_up=mla_kv_up, w_out_ab=w_out_ab, w_in_c=w_in_c, conv_w=conv_w, gdn_a_log=gdn_a_log,
             gdn_dt_bias=gdn_dt_bias, gdn_norm=gdn_norm, w_out_c=w_out_c, moe_w_group=moe_w_group,
             moe_b_group=moe_b_group, moe_w_expert=moe_w_expert, moe_b_expert=moe_b_expert,
             moe_w1=moe_w1, moe_w3=moe_w3, moe_w2=moe_w2)
    depth = norm_mix.shape[0]
    bp, bs = c_prompt.shape[0], c_sample.shape[0]
    nb = (bp + bs + 7) // 8 * 8
    c_all = jnp.pad(jnp.concatenate([c_prompt, c_sample], axis=0), ((0, nb - bp - bs), (0, 0)))
    mods = _mod_call(c_all, w_mod, b_mod)
    wab = [_prep_ab(l, p) for l in range((depth + 1) // 2)]
    wc = [_prep_c(l, p) for l in range(depth // 2)]
    wmoe = [_prep_moe(layer, p) for layer in range(depth)]
    none6 = (None,) * 6
    outp = _trunk(x_prompt, mods[:, :bp], none6, wab, wc, wmoe, p, flat=False)
    caches = (cache_mla_latent, cache_mla_krope, state_s5_re, state_s5_im, state_conv, state_gdn)
    outs = _trunk(x_sample, mods[:, bp:bp + bs], caches, wab, wc, wmoe, p, flat=True)
    return (outp[0], outs[0]) + tuple(outp[1:]) + tuple(outs[1:])
```

```python
import functools
import math

import jax
import jax.numpy as jnp
from jax import lax
from jax.experimental import pallas as pl
from jax.experimental.pallas import tpu as pltpu

F32 = jnp.float32
BF16 = jnp.bfloat16
HI = lax.Precision.HIGHEST
EPS = 1e-6

D_MODEL = 1024
CHUNK = 64
S5_WIDTH = 512
S5_GROUP = 16
S5_GROUPS = 32
S5_STATE = 64
S5_CHUNK = 16
S5_LB = 128 // S5_GROUP
MLA_HEADS = 8
MLA_NOPE = 64
MLA_ROPE = 32
MLA_V = 64
MLA_Q_RANK = 384
MLA_KV_RANK = 256
ROPE_THETA = 10000.0
GDN_HEADS = 8
GDN_DK = 128
GDN_DV = 128
CONV_W = 4
GDN_QKV = GDN_HEADS * (2 * GDN_DK + GDN_DV)
GDN_VW = GDN_HEADS * GDN_DV
N_GROUPS = 4
EXPERTS_PER_GROUP = 8
N_EXPERTS = 32
D_EXPERT = 512
TOP_K = 2
MOE_BLOCK = 256
ATTN_SUB = 512
LANES = 128
VMEM_LIMIT = 48 * 1024 * 1024


def _cparams(*sem):
    return pltpu.CompilerParams(dimension_semantics=sem, vmem_limit_bytes=VMEM_LIMIT)


def _dot(a, b):
    return jnp.dot(a, b, preferred_element_type=F32)


def _dot_nt(a, b, precision=None):
    return lax.dot_general(a, b, (((1,), (1,)), ((), ())), preferred_element_type=F32, precision=precision)


def _full(arr):
    nd = arr.ndim
    return pl.BlockSpec(arr.shape, lambda *_: (0,) * nd)


def _tok(width, ts):
    return pl.BlockSpec((1, ts, width), lambda b, j: (b, j, 0))


def _modspec(arr, ts):
    if arr.shape[1] == 1:
        return pl.BlockSpec((1, 1, arr.shape[2]), lambda b, j: (b, 0, 0))
    return pl.BlockSpec((1, ts, arr.shape[2]), lambda b, j: (b, j, 0))


def _normmod(x, gain, sc, sh):
    ms = jnp.mean(x * x, axis=-1, keepdims=True)
    return x * lax.rsqrt(ms + EPS) * gain * (1.0 + sc) + sh


def _rms(x, gain):
    return x * lax.rsqrt(jnp.mean(x * x, axis=-1, keepdims=True) + EPS) * gain


def _mod_kernel(c_ref, w_ref, b_ref, o_ref):
    c = c_ref[...]
    a = (c * jax.nn.sigmoid(c)).astype(BF16)
    o_ref[0] = _dot(a, w_ref[0]) + b_ref[0]


def _mod_call(c_all, w_mod, b_mod):
    depth, d, n6 = w_mod.shape
    bp = c_all.shape[0]
    tn = 1536
    return pl.pallas_call(
        _mod_kernel,
        grid=(depth, n6 // tn),
        in_specs=[pl.BlockSpec((bp, d), lambda l, n: (0, 0)),
                  pl.BlockSpec((1, d, tn), lambda l, n: (l, 0, n)),
                  pl.BlockSpec((1, 1, tn), lambda l, n: (l, 0, n))],
        out_specs=pl.BlockSpec((1, bp, tn), lambda l, n: (l, 0, n)),
        out_shape=jax.ShapeDtypeStruct((depth, bp, n6), F32),
        compiler_params=_cparams("parallel", "parallel"),
        name="mod",
    )(c_all, w_mod.astype(BF16), b_mod.reshape(depth, 1, n6))


def _ab_in_kernel(x_ref, gain_ref, sc_ref, sh_ref, w_ref, qg_ref, wqa_ref, wqb_ref, kvg_ref,
                  cq_ref, sq_ref, ck_ref, sk_ref, u_ref, q_ref, lat_ref, kr_ref):
    h = _normmod(x_ref[0], gain_ref[...], sc_ref[0], sh_ref[0]).astype(BF16)
    proj = _dot(h, w_ref[...])
    u_ref[0] = proj[:, :S5_WIDTH]
    o1 = S5_WIDTH + MLA_Q_RANK
    o2 = o1 + MLA_KV_RANK
    qn = _rms(proj[:, S5_WIDTH:o1], qg_ref[...]).astype(BF16)
    qa = _dot(qn, wqa_ref[...])
    qb = _dot(qn, wqb_ref[...])
    cq = cq_ref[...]
    sq = sq_ref[...]
    for hh in range(MLA_HEADS):
        sl = slice(LANES * hh, LANES * (hh + 1))
        q_ref[0, hh] = (qa[:, sl] * cq + qb[:, sl] * sq).astype(BF16)
    lat_ref[0] = _rms(proj[:, o1:o2], kvg_ref[...])
    kr_ref[0] = proj[:, o2:o2 + LANES] * ck_ref[...] + proj[:, o2 + LANES:o2 + 2 * LANES] * sk_ref[...]


def _ab_in_call(x, gain, sc, sh, wp, tabs, ts):
    b, s, d = x.shape
    cq, sq, ck, sk = tabs
    tab = pl.BlockSpec((ts, LANES), lambda bb, j: (j, 0))
    ins = [x, gain, sc, sh, wp["w_in"], wp["q_norm"], wp["wqa"], wp["wqb"], wp["kv_norm"], cq, sq, ck, sk]
    specs = [_tok(d, ts), _full(gain), _modspec(sc, ts), _modspec(sh, ts), _full(wp["w_in"]), _full(wp["q_norm"]),
             _full(wp["wqa"]), _full(wp["wqb"]), _full(wp["kv_norm"]), tab, tab, tab, tab]
    return pl.pallas_call(
        _ab_in_kernel,
        grid=(b, s // ts),
        in_specs=specs,
        out_specs=[_tok(S5_WIDTH, ts),
                   pl.BlockSpec((1, MLA_HEADS, ts, LANES), lambda bb, j: (bb, 0, j, 0)),
                   _tok(MLA_KV_RANK, ts), _tok(LANES, ts)],
        out_shape=[jax.ShapeDtypeStruct((b, s, S5_WIDTH), F32),
                   jax.ShapeDtypeStruct((b, MLA_HEADS, s, LANES), BF16),
                   jax.ShapeDtypeStruct((b, s, MLA_KV_RANK), F32),
                   jax.ShapeDtypeStruct((b, s, LANES), F32)],
        compiler_params=_cparams("parallel", "parallel"),
        name="ab_in",
    )(*ins)


def _kv_kernel(lat_ref, kr_ref, wk_ref, wv_ref, k_ref, v_ref):
    lat = lat_ref[0].astype(BF16)
    kk = _dot(lat, wk_ref[...])
    vv = _dot(lat, wv_ref[...])
    kr = kr_ref[0]
    ones = (lax.broadcasted_iota(jnp.int32, (1, LANES), 1) >= MLA_V).astype(F32)
    for hh in range(MLA_HEADS):
        k_ref[0, hh] = (kk[:, LANES * hh:LANES * (hh + 1)] + kr).astype(BF16)
        v_ref[0, hh] = (vv[:, LANES * hh:LANES * (hh + 1)] + ones).astype(BF16)


def _kv_call(lat_all, kr_all, wp, ts):
    b, sk, _ = lat_all.shape
    hspec = pl.BlockSpec((1, MLA_HEADS, ts, LANES), lambda bb, j: (bb, 0, j, 0))
    hshape = jax.ShapeDtypeStruct((b, MLA_HEADS, sk, LANES), BF16)
    return pl.pallas_call(
        _kv_kernel,
        grid=(b, sk // ts),
        in_specs=[_tok(MLA_KV_RANK, ts), _tok(LANES, ts), _full(wp["wk"]), _full(wp["wv"])],
        out_specs=[hspec, hspec],
        out_shape=[hshape, hshape],
        compiler_params=_cparams("parallel", "parallel"),
        name="kv_up",
    )(lat_all, kr_all, wp["wk"], wp["wv"])


def _attn_kernel(q_ref, k_ref, v_ref, o_ref, *, tq, tk, q_off, sk_valid):
    i = pl.program_id(2)
    nsub = max(tq // ATTN_SUB, 1)
    tqs = tq // nsub
    chains = [(hh, qi) for hh in range(2) for qi in range(nsub)]
    qs = [q_ref[0, hh, qi * tqs:(qi + 1) * tqs, :] for hh, qi in chains]
    q0 = q_off + i * tq
    lim_full = (q0 // CHUNK + 1) * CHUNK
    lim_tot = jnp.minimum(((q0 + tq - 1) // CHUNK + 1) * CHUNK, sk_valid)
    n_full = jnp.minimum(lim_full // tk, sk_valid // tk)
    n_tot = (lim_tot + tk - 1) // tk

    def step(j, carry, masked):
        off = pl.multiple_of(j * tk, tk)
        nc = range(len(chains))
        if masked:
            kpos = off + lax.broadcasted_iota(jnp.int32, (tqs, tk), 1)
            ok = []
            for qi in range(nsub):
                qpos = q0 + qi * tqs + lax.broadcasted_iota(jnp.int32, (tqs, tk), 0)
                ok.append(jnp.logical_and(kpos // CHUNK <= qpos // CHUNK, kpos < sk_valid))
        kt = [k_ref[0, hh, pl.ds(off, tk), :] for hh in range(2)]
        vt = [v_ref[0, hh, pl.ds(off, tk), :] for hh in range(2)]
        s = [_dot_nt(qs[c], kt[chains[c][0]]) for c in nc]
        if masked:
            s = [jnp.where(ok[chains[c][1]], s[c], -1e30) for c in nc]
        m_new = [jnp.maximum(carry[c][0], jnp.max(s[c], axis=-1, keepdims=True)) for c in nc]
        alpha = [jnp.exp2(carry[c][0] - m_new[c]) for c in nc]
        p = [jnp.exp2(s[c] - m_new[c]).astype(BF16) for c in nc]
        acc = [alpha[c] * carry[c][1] + _dot(p[c], vt[chains[c][0]]) for c in nc]
        return tuple((m_new[c], acc[c]) for c in nc)

    one = (jnp.full((tqs, 1), -1e30, F32), jnp.zeros((tqs, LANES), F32))
    carry = lax.fori_loop(0, n_full, functools.partial(step, masked=False), (one,) * len(chains))
    carry = lax.fori_loop(n_full, n_tot, functools.partial(step, masked=True), carry)
    lane = lax.broadcasted_iota(jnp.int32, (tqs, LANES), 1)
    for qi in range(nsub):
        a0 = carry[chains.index((0, qi))][1]
        a1 = carry[chains.index((1, qi))][1]
        r0 = a0 / pltpu.roll(a0, MLA_V, axis=1)
        r1 = a1 / pltpu.roll(a1, MLA_V, axis=1)
        o_ref[0, qi * tqs:(qi + 1) * tqs, :] = jnp.where(
            lane < MLA_V, r0, pltpu.roll(r1, MLA_V, axis=1)).astype(o_ref.dtype)


def _attn_call(q, k, v, tq, tk, q_off, sk_valid):
    b, h, sq, _ = q.shape
    sk = k.shape[2]
    kern = functools.partial(_attn_kernel, tq=tq, tk=tk, q_off=q_off, sk_valid=sk_valid)
    kv_mode = pl.Buffered(2 if sq == tq else 1)
    return pl.pallas_call(
        kern,
        grid=(b, h // 2, sq // tq),
        in_specs=[pl.BlockSpec((1, 2, tq, LANES), lambda bb, hp, i: (bb, hp, i, 0)),
                  pl.BlockSpec((1, 2, sk, LANES), lambda bb, hp, i: (bb, hp, 0, 0), pipeline_mode=kv_mode),
                  pl.BlockSpec((1, 2, sk, LANES), lambda bb, hp, i: (bb, hp, 0, 0), pipeline_mode=kv_mode)],
        out_specs=pl.BlockSpec((1, tq, LANES), lambda bb, hp, i: (bb, i, hp)),
        out_shape=jax.ShapeDtypeStruct((b, sq, h * MLA_V), BF16),
        compiler_params=_cparams("parallel", "parallel", "arbitrary"),
        name="mla_attn",
    )(q, k, v)


def _chunk_rows(ref, s, tr):
    return ref[pl.ds(s, tr, stride=S5_CHUNK), :]


def _s5_w_kernel(u_ref, bre_ref, bim_ref, wre_ref, wim_ref, *, tr):
    ucat = jnp.concatenate([_chunk_rows(u_ref, s, tr).astype(BF16) for s in range(S5_CHUNK)], axis=1)
    wre_ref[...] = _dot(ucat, bre_ref[0])
    wim_ref[...] = _dot(ucat, bim_ref[0])


def _s5_w_call(u2, bre, bim, tr):
    n = u2.shape[0]
    r = n // S5_CHUNK
    nq = bre.shape[0]
    sw = S5_LB * S5_STATE
    out = jax.ShapeDtypeStruct((r, nq * sw), F32)
    ospec = pl.BlockSpec((tr, sw), lambda q, i: (i, q))
    wspec = pl.BlockSpec((1, S5_CHUNK * LANES, sw), lambda q, i: (q, 0, 0))
    return pl.pallas_call(
        functools.partial(_s5_w_kernel, tr=tr),
        grid=(nq, r // tr),
        in_specs=[pl.BlockSpec((tr * S5_CHUNK, LANES), lambda q, i: (i, q)), wspec, wspec],
        out_specs=[ospec, ospec],
        out_shape=[out, out],
        compiler_params=_cparams("parallel", "parallel"),
        name="s5_chunk_in",
    )(u2, bre, bim)


def _s5_scan_kernel(wre_ref, wim_ref, lre_ref, lim_ref, x0re_ref, x0im_ref, ore_ref, oim_ref, sre, sim, *, tc):
    @pl.when(pl.program_id(1) == 0)
    def _():
        sre[...] = x0re_ref[0]
        sim[...] = x0im_ref[0]

    lr = lre_ref[...]
    li = lim_ref[...]

    def blk(t, carry):
        xr, xi = carry
        base = pl.multiple_of(t * 8, 8)
        wr = wre_ref[0, pl.ds(base, 8), :]
        wi = wim_ref[0, pl.ds(base, 8), :]
        rows_r, rows_i = [], []
        for r in range(8):
            nr = lr * xr - li * xi + wr[r:r + 1]
            ni = lr * xi + li * xr + wi[r:r + 1]
            xr, xi = nr, ni
            rows_r.append(xr)
            rows_i.append(xi)
        ore_ref[0, pl.ds(base, 8), :] = jnp.concatenate(rows_r, axis=0)
        oim_ref[0, pl.ds(base, 8), :] = jnp.concatenate(rows_i, axis=0)
        return xr, xi

    xr, xi = lax.fori_loop(0, tc // 8, blk, (sre[...], sim[...]))
    sre[...] = xr
    sim[...] = xi


def _s5_scan_call(wre, wim, lre, lim, x0re, x0im, tc):
    b, nch, n = wre.shape
    wspec = pl.BlockSpec((1, tc, n), lambda bb, c: (bb, c, 0))
    lspec = pl.BlockSpec((1, n), lambda bb, c: (0, 0))
    xspec = pl.BlockSpec((1, 1, n), lambda bb, c: (bb, 0, 0))
    out = jax.ShapeDtypeStruct((b, nch, n), F32)
    return pl.pallas_call(
        functools.partial(_s5_scan_kernel, tc=tc),
        grid=(b, nch // tc),
        in_specs=[wspec, wspec, lspec, lspec, xspec, xspec],
        out_specs=[wspec, wspec],
        out_shape=[out, out],
        scratch_shapes=[pltpu.VMEM((1, n), F32), pltpu.VMEM((1, n), F32)],
        compiler_params=_cparams("parallel", "arbitrary"),
        name="s5_scan",
    )(wre, wim, lre, lim, x0re, x0im)


def _s5_y_kernel(u_ref, xre_ref, xim_ref, m_ref, c_ref, y_ref, *, tr):
    L = S5_CHUNK
    urev = jnp.concatenate([_chunk_rows(u_ref, s, tr).astype(BF16) for s in reversed(range(L))], axis=1)
    xcat = jnp.concatenate([xre_ref[...].astype(BF16), xim_ref[...].astype(BF16)], axis=1)
    for t in range(L):
        y = _dot(urev[:, (L - 1 - t) * LANES:], m_ref[0, :(t + 1) * LANES, :]) + _dot(xcat, c_ref[0, t])
        y_ref[pl.ds(t, tr, stride=L), :] = y


def _s5_y_call(u2, xre, xim, mst, cc, tr):
    n, w = u2.shape
    nq = mst.shape[0]
    sw = S5_LB * S5_STATE
    uspec = pl.BlockSpec((tr * S5_CHUNK, LANES), lambda q, i: (i, q))
    xspec = pl.BlockSpec((tr, sw), lambda q, i: (i, q))
    return pl.pallas_call(
        functools.partial(_s5_y_kernel, tr=tr),
        grid=(nq, n // (tr * S5_CHUNK)),
        in_specs=[uspec, xspec, xspec,
                  pl.BlockSpec((1, S5_CHUNK * LANES, LANES), lambda q, i: (q, 0, 0)),
                  pl.BlockSpec((1, S5_CHUNK, 2 * sw, LANES), lambda q, i: (q, 0, 0, 0))],
        out_specs=uspec,
        out_shape=jax.ShapeDtypeStruct((n, w), F32),
        compiler_params=_cparams("parallel", "parallel"),
        name="s5_chunk_out",
    )(u2, xre, xim, mst, cc)


def _s5_mats(a_re, a_im, b_re, b_im, c_re, c_im, log_dt):
    g, p = a_re.shape
    L = S5_CHUNK
    lam = lax.complex(jnp.minimum(a_re, -1e-4), a_im)
    lamdt = lam * jnp.exp(log_dt)[:, None]
    lam_bar = jnp.exp(lamdt)
    b_bar = ((lam_bar - 1.0) / lam)[:, :, None] * lax.complex(b_re, b_im)
    cm = lax.complex(c_re, c_im)
    pw = jnp.exp(lamdt[:, :, None] * jnp.arange(L + 1, dtype=F32))
    mm = pw[:, :, :L, None] * b_bar[:, :, None, :]
    kd = jnp.sum((cm[:, :, :, None, None] * mm[:, None]).real, axis=2).transpose(0, 2, 1, 3)
    bp = pw[:, :, L - 1 - jnp.arange(L)][:, :, :, None] * b_bar[:, :, None, :]
    cp = cm[:, :, :, None] * pw[:, None, :, 1:]
    nq = g // S5_LB
    eye = jnp.eye(S5_LB, dtype=F32)

    def lane_block(m, spec, rows, cols):
        return jnp.einsum(spec, m.reshape((nq, S5_LB) + m.shape[1:]), eye).reshape(nq, rows, cols)

    sw = S5_LB * p
    bre = lane_block(bp.real, "qgpsc,hg->qshcgp", L * LANES, sw)
    bim = lane_block(bp.imag, "qgpsc,hg->qshcgp", L * LANES, sw)
    mst = lane_block(kd, "qgdoc,hg->qdhcgo", L * LANES, LANES)
    cre = lane_block(cp.real, "qgopt,hg->qthpgo", L * sw, LANES).reshape(nq, L, sw, LANES)
    cim = lane_block(-cp.imag, "qgopt,hg->qthpgo", L * sw, LANES).reshape(nq, L, sw, LANES)
    return dict(
        bre=bre.astype(BF16), bim=bim.astype(BF16), mst=mst.astype(BF16),
        cc=jnp.concatenate([cre, cim], axis=2).astype(BF16),
        lre=pw[:, :, L].real.reshape(1, g * p), lim=pw[:, :, L].imag.reshape(1, g * p))


def _ab_out_kernel(x_ref, ys_ref, u_ref, at_ref, d_ref, gw_ref, gb_ref, ws_ref, wa_ref, g1_ref, o_ref):
    y = ys_ref[0] + d_ref[...] * u_ref[0]
    y = jax.nn.gelu(y)
    gate = jax.nn.sigmoid(_dot(y.astype(BF16), gw_ref[...]) + gb_ref[...])
    out = _dot((y * gate).astype(BF16), ws_ref[...]) + _dot(at_ref[0], wa_ref[...])
    o_ref[0] = x_ref[0] + g1_ref[0] * out


def _ab_out_call(x, ys, u, attn, wp, g1, ts):
    b, s, d = x.shape
    ins = [x, ys, u, attn, wp["s5_d"], wp["glu_w"], wp["glu_b"], wp["w_out_s5"], wp["w_out_at"], g1]
    specs = [_tok(d, ts), _tok(S5_WIDTH, ts), _tok(S5_WIDTH, ts), _tok(MLA_HEADS * MLA_V, ts),
             _full(wp["s5_d"]), _full(wp["glu_w"]), _full(wp["glu_b"]), _full(wp["w_out_s5"]),
             _full(wp["w_out_at"]), _modspec(g1, ts)]
    return pl.pallas_call(
        _ab_out_kernel,
        grid=(b, s // ts),
        in_specs=specs,
        out_specs=_tok(d, ts),
        out_shape=jax.ShapeDtypeStruct((b, s, d), F32),
        compiler_params=_cparams("parallel", "parallel"),
        name="ab_out",
    )(*ins)


def _c_in_kernel(x_ref, gain_ref, sc_ref, sh_ref, w_ref, qkv_ref, z_ref, ab_ref):
    h = _normmod(x_ref[0], gain_ref[...], sc_ref[0], sh_ref[0]).astype(BF16)
    proj = _dot(h, w_ref[...])
    qkv_ref[0] = proj[:, :GDN_QKV].astype(BF16)
    z_ref[0] = proj[:, GDN_QKV:GDN_QKV + GDN_VW].astype(BF16)
    ab_ref[0] = proj[:, GDN_QKV + GDN_VW:]


def _c_in_call(x, gain, sc, sh, w, ts):
    b, s, d = x.shape
    return pl.pallas_call(
        _c_in_kernel,
        grid=(b, s // ts),
        in_specs=[_tok(d, ts), _full(gain), _modspec(sc, ts), _modspec(sh, ts), _full(w)],
        out_specs=[_tok(GDN_QKV, ts), _tok(GDN_VW, ts), _tok(LANES, ts)],
        out_shape=[jax.ShapeDtypeStruct((b, s, GDN_QKV), BF16),
                   jax.ShapeDtypeStruct((b, s, GDN_VW), BF16),
                   jax.ShapeDtypeStruct((b, s, LANES), F32)],
        compiler_params=_cparams("parallel", "parallel"),
        name="c_in",
    )(x, gain, sc, sh, w)


def _gdn_prep_kernel(qkv_ref, halo_ref, st_ref, cw_ref, ab_ref, alog_ref, dtb_ref,
                     wq_ref, u_ref, kgt_ref, pm_ref, ee_ref, *, L):
    j = pl.program_id(1)
    prev = jnp.where(j == 0, st_ref[0].astype(F32), halo_ref[0].astype(F32))
    xx = jnp.concatenate([prev, qkv_ref[0].astype(F32)], axis=0)
    cw = cw_ref[...]
    y = xx[8:8 + L] * cw[CONV_W - 1:CONV_W]
    for t in range(1, CONV_W):
        y += xx[8 - t:8 - t + L] * cw[CONV_W - 1 - t:CONV_W - t]
    y = y * jax.nn.sigmoid(y)

    ab = ab_ref[0]
    g_all = -jnp.exp(alog_ref[...]) * jax.nn.softplus(ab + dtb_ref[...])
    beta_all = jax.nn.sigmoid(ab)
    row = lax.broadcasted_iota(jnp.int32, (L, L), 0)
    col = lax.broadcasted_iota(jnp.int32, (L, L), 1)
    incl = row >= col
    strict = row > col
    gam_all = jnp.dot(incl.astype(F32), g_all, preferred_element_type=F32, precision=HI)
    eye_l = (lax.broadcasted_iota(jnp.int32, (LANES, LANES), 0)
             == lax.broadcasted_iota(jnp.int32, (LANES, LANES), 1))
    gam_t = _dot_nt(eye_l.astype(F32), gam_all, precision=HI)
    ee_ref[0, 0] = jnp.exp(gam_all[L - 1:L])
    eye64 = (row == col).astype(F32)
    eye_bf = eye_l.astype(BF16)

    hs = range(GDN_HEADS)
    koff, voff = GDN_HEADS * GDN_DK, 2 * GDN_HEADS * GDN_DK
    qh = [y[:, GDN_DK * h:GDN_DK * (h + 1)] for h in hs]
    kh = [y[:, koff + GDN_DK * h:koff + GDN_DK * (h + 1)] for h in hs]
    vh = [y[:, voff + GDN_DV * h:voff + GDN_DV * (h + 1)] for h in hs]
    qh = [q * lax.rsqrt(jnp.sum(q * q, axis=-1, keepdims=True) + EPS) * (GDN_DK ** -0.5) for q in qh]
    kh = [k * lax.rsqrt(jnp.sum(k * k, axis=-1, keepdims=True) + EPS) for k in kh]
    gam_c = [gam_all[:, h:h + 1] for h in hs]
    beta_c = [beta_all[:, GDN_HEADS + h:GDN_HEADS + h + 1] for h in hs]
    dec = [jnp.exp(jnp.minimum(gam_c[h] - gam_t[h:h + 1, :], 0.0)) for h in hs]
    kb = [k.astype(BF16) for k in kh]
    kk = [_dot_nt(kb[h], kb[h]) for h in hs]
    qk = [_dot_nt(qh[h].astype(BF16), kb[h]) for h in hs]
    a = [jnp.where(strict, beta_c[h] * kk[h] * dec[h], 0.0) for h in hs]
    for h in hs:
        pm_ref[0, 0, h] = jnp.where(incl, qk[h] * dec[h], 0.0).astype(BF16)
    tinv = [eye64 - a[h] for h in hs]
    pw = [_split(a[h]) for h in hs]
    for _ in range(max(L.bit_length() - 2, 0)):
        pw = [_split(_dot3(pw[h], pw[h])) for h in hs]
        tinv = [tinv[h] + _dot3(_split(tinv[h]), pw[h]) for h in hs]
    eg = [jnp.exp(gam_c[h]) for h in hs]
    rhs = [jnp.concatenate([(beta_c[h] * eg[h]) * kh[h], beta_c[h] * vh[h]], axis=1).astype(BF16) for h in hs]
    wu = [_dot(tinv[h].astype(BF16), rhs[h]) for h in hs]
    for h in hs:
        wq_ref[0, 0, h] = jnp.concatenate([wu[h][:, :GDN_DK], qh[h] * eg[h]], axis=0).astype(BF16)
        u_ref[0, h] = wu[h][:, GDN_DK:].astype(BF16)
    kg = [(kh[h] * jnp.exp(gam_all[L - 1:L, h:h + 1] - gam_c[h])).astype(BF16) for h in hs]
    kgt = [_dot_nt(eye_bf, kg[h]) for h in hs]
    for h in hs:
        kgt_ref[0, 0, h] = kgt[h].astype(BF16)


def _split(a):
    hi = a.astype(BF16)
    return hi, (a - hi.astype(F32)).astype(BF16)


def _dot3(a, b):
    return _dot(a[0], b[0]) + (_dot(a[0], b[1]) + _dot(a[1], b[0]))


def _gdn_prep_call(qkv, st8, conv_w, ab, alog, dtb, L):
    b, s, _ = qkv.shape
    nch = s // L
    return pl.pallas_call(
        functools.partial(_gdn_prep_kernel, L=L),
        grid=(b, nch),
        in_specs=[_tok(GDN_QKV, L),
                  pl.BlockSpec((1, 8, GDN_QKV), lambda bb, j: (bb, jnp.maximum(j * (L // 8) - 1, 0), 0)),
                  pl.BlockSpec((1, 8, GDN_QKV), lambda bb, j: (bb, 0, 0)),
                  _full(conv_w), _tok(LANES, L), _full(alog), _full(dtb)],
        out_specs=[pl.BlockSpec((1, 1, GDN_HEADS, 2 * L, LANES), lambda bb, j: (bb, j, 0, 0, 0)),
                   pl.BlockSpec((1, GDN_HEADS, L, LANES), lambda bb, j: (bb, 0, j, 0)),
                   pl.BlockSpec((1, 1, GDN_HEADS, GDN_DK, L), lambda bb, j: (bb, j, 0, 0, 0)),
                   pl.BlockSpec((1, 1, GDN_HEADS, L, L), lambda bb, j: (bb, j, 0, 0, 0)),
                   pl.BlockSpec((1, 1, 1, LANES), lambda bb, j: (bb, j, 0, 0))],
        out_shape=[jax.ShapeDtypeStruct((b, nch, GDN_HEADS, 2 * L, LANES), BF16),
                   jax.ShapeDtypeStruct((b, GDN_HEADS, s, LANES), BF16),
                   jax.ShapeDtypeStruct((b, nch, GDN_HEADS, GDN_DK, L), BF16),
                   jax.ShapeDtypeStruct((b, nch, GDN_HEADS, L, L), BF16),
                   jax.ShapeDtypeStruct((b, nch, 1, LANES), F32)],
        compiler_params=_cparams("parallel", "parallel"),
        name="gdn_prep",
    )(qkv, qkv, st8, conv_w, ab, alog, dtb)


GDN_BATCH_GROUP = 2


def _gdn_seq_kernel(wq_ref, u_ref, kgt_ref, pm_ref, ee_ref, s0_ref, o_ref, sf_ref, s_scr, *, L):
    c = pl.program_id(1)

    @pl.when(c == 0)
    def _():
        s_scr[...] = s0_ref[...]

    ch = [(bb, hh) for bb in range(GDN_BATCH_GROUP) for hh in range(GDN_HEADS)]
    st = [s_scr[bb, hh] for bb, hh in ch]
    sp = [_split(s) for s in st]
    r = [_dot(wq_ref[bb, 0, hh], sp[i][0]) + _dot(wq_ref[bb, 0, hh], sp[i][1]) for i, (bb, hh) in enumerate(ch)]
    v_new = [(u_ref[bb, hh].astype(F32) - r[i][:L]).astype(BF16) for i, (bb, hh) in enumerate(ch)]
    o = [r[i][L:] + _dot(pm_ref[bb, 0, hh], v_new[i]) for i, (bb, hh) in enumerate(ch)]
    for i, (bb, hh) in enumerate(ch):
        o_ref[bb, :, GDN_DV * hh:GDN_DV * (hh + 1)] = o[i].astype(o_ref.dtype)
    s_new = [ee_ref[bb, 0][:, hh:hh + 1] * st[i] + _dot(kgt_ref[bb, 0, hh], v_new[i]) for i, (bb, hh) in enumerate(ch)]
    for i, (bb, hh) in enumerate(ch):
        s_scr[bb, hh] = s_new[i]

    @pl.when(c == pl.num_programs(1) - 1)
    def _():
        sf_ref[...] = s_scr[...]


def _gdn_seq_call(wq, um, kgt, pm, ee, s0, L):
    b, h, s, _ = um.shape
    nch = s // L
    bg = GDN_BATCH_GROUP
    sspec = pl.BlockSpec((bg, h, GDN_DK, GDN_DV), lambda g, c: (g, 0, 0, 0))
    return pl.pallas_call(
        functools.partial(_gdn_seq_kernel, L=L),
        grid=(b // bg, nch),
        in_specs=[pl.BlockSpec((bg, 1, h, 2 * L, LANES), lambda g, c: (g, c, 0, 0, 0)),
                  pl.BlockSpec((bg, h, L, LANES), lambda g, c: (g, 0, c, 0)),
                  pl.BlockSpec((bg, 1, h, GDN_DK, L), lambda g, c: (g, c, 0, 0, 0)),
                  pl.BlockSpec((bg, 1, h, L, L), lambda g, c: (g, c, 0, 0, 0)),
                  pl.BlockSpec((bg, 1, 1, LANES), lambda g, c: (g, c, 0, 0)),
                  sspec],
        out_specs=[pl.BlockSpec((bg, L, GDN_VW), lambda g, c: (g, c, 0)), sspec],
        out_shape=[jax.ShapeDtypeStruct((b, s, GDN_VW), BF16),
                   jax.ShapeDtypeStruct((b, h, GDN_DK, GDN_DV), F32)],
        scratch_shapes=[pltpu.VMEM((bg, h, GDN_DK, GDN_DV), F32)],
        compiler_params=_cparams("parallel", "arbitrary"),
        name="gdn_seq",
    )(wq, um, kgt, pm, ee, s0)


def _c_out_kernel(x_ref, o_ref, z_ref, gn_ref, w_ref, g1_ref, out_ref):
    o = o_ref[0].astype(F32)
    z = z_ref[0].astype(F32)
    gn = gn_ref[...]
    parts = []
    for hh in range(GDN_HEADS):
        oh = o[:, GDN_DV * hh:GDN_DV * (hh + 1)]
        zh = z[:, GDN_DV * hh:GDN_DV * (hh + 1)]
        parts.append((_rms(oh, gn) * (zh * jax.nn.sigmoid(zh))).astype(BF16))
    out = _dot(jnp.concatenate(parts, axis=1), w_ref[...])
    out_ref[0] = x_ref[0] + g1_ref[0] * out


def _c_out_call(x, o, z, gn, w, g1, ts):
    b, s, d = x.shape
    return pl.pallas_call(
        _c_out_kernel,
        grid=(b, s // ts),
        in_specs=[_tok(d, ts), _tok(GDN_VW, ts), _tok(GDN_VW, ts), _full(gn), _full(w), _modspec(g1, ts)],
        out_specs=_tok(d, ts),
        out_shape=jax.ShapeDtypeStruct((b, s, d), F32),
        compiler_params=_cparams("parallel", "parallel"),
        name="c_out",
    )(x, o, z, gn, w, g1)


def _pack_halves(v):
    n = v.shape[1] // 2
    bits = pltpu.bitcast(v.astype(BF16).astype(F32), jnp.uint32)
    return (bits[:, :n] >> 16) | (bits[:, n:] & jnp.uint32(0xFFFF0000))


def _unpack_halves(w):
    return pltpu.bitcast(w << 16, F32), pltpu.bitcast(w & jnp.uint32(0xFFFF0000), F32)


def _router_kernel(x_ref, gain_ref, sc_ref, sh_ref, wr_ref, br_ref, h_ref, re_ref, rg_ref, cnt_ref, carry):
    @pl.when(jnp.logical_and(pl.program_id(0) == 0, pl.program_id(1) == 0))
    def _():
        carry[...] = jnp.zeros_like(carry)

    h = _normmod(x_ref[0], gain_ref[...], sc_ref[0], sh_ref[0])
    h_ref[0] = _pack_halves(h)
    logits = _dot3(_split(h), (wr_ref[0], wr_ref[1])) + br_ref[...]
    lane = lax.broadcasted_iota(jnp.int32, logits.shape, 1).astype(F32)
    neg = jnp.float32(-jnp.inf)
    big = jnp.float32(1e9)
    gl = jnp.where(lane < N_GROUPS, logits, neg)
    gm = jnp.max(gl, axis=-1, keepdims=True)
    g_val = 1.0 / jnp.sum(jnp.exp(gl - gm), axis=-1, keepdims=True)
    g_idx = jnp.min(jnp.where(gl == gm, lane, big), axis=-1, keepdims=True)
    lo = N_GROUPS + EXPERTS_PER_GROUP * g_idx
    el = jnp.where(jnp.logical_and(lane >= lo, lane < lo + EXPERTS_PER_GROUP), logits, neg)
    em = jnp.max(el, axis=-1, keepdims=True)
    i1 = jnp.min(jnp.where(el == em, lane, big), axis=-1, keepdims=True)
    el2 = jnp.where(lane == i1, neg, el)
    em2 = jnp.max(el2, axis=-1, keepdims=True)
    i2 = jnp.min(jnp.where(el2 == em2, lane, big), axis=-1, keepdims=True)
    es = jnp.sum(jnp.exp(el - em), axis=-1, keepdims=True)
    p1 = 1.0 / es
    p2 = jnp.exp(em2 - em) / es
    den = p1 + p2
    rg_ref[0] = jnp.where(lane == 0, g_val * p1 / den, jnp.where(lane == 1, g_val * p2 / den, 0.0))
    oh1 = lane == i1
    oh2 = lane == i2
    both = jnp.where(jnp.logical_or(oh1, oh2), 1.0, 0.0)
    ts = both.shape[0]
    tri = (lax.broadcasted_iota(jnp.int32, (ts, ts), 0) > lax.broadcasted_iota(jnp.int32, (ts, ts), 1))
    pre = _dot(jnp.where(tri, 1.0, 0.0).astype(BF16), both.astype(BF16)) + carry[...]
    r1 = jnp.sum(jnp.where(oh1, pre, 0.0), axis=-1, keepdims=True)
    r2 = jnp.sum(jnp.where(oh2, pre, 0.0), axis=-1, keepdims=True)
    re_ref[0] = jnp.where(lane == 0, i1 - N_GROUPS, jnp.where(lane == 1, i2 - N_GROUPS, jnp.where(
        lane == 2, r1, jnp.where(lane == 3, r2, 0.0)))).astype(jnp.int32)
    carry[...] += jnp.sum(both, axis=0, keepdims=True)
    cnt_ref[...] = carry[...]


def _router_call(x, gain, sc, sh, wr, br, ts):
    b, s, d = x.shape
    return pl.pallas_call(
        _router_kernel,
        grid=(b, s // ts),
        in_specs=[_tok(d, ts), _full(gain), _modspec(sc, ts), _modspec(sh, ts), _full(wr), _full(br)],
        out_specs=[_tok(d // 2, ts), _tok(LANES, ts), _tok(LANES, ts), pl.BlockSpec((1, LANES), lambda bb, j: (0, 0))],
        out_shape=[jax.ShapeDtypeStruct((b, s, d // 2), jnp.uint32),
                   jax.ShapeDtypeStruct((b, s, LANES), jnp.int32),
                   jax.ShapeDtypeStruct((b, s, LANES), F32),
                   jax.ShapeDtypeStruct((1, LANES), F32)],
        scratch_shapes=[pltpu.VMEM((1, LANES), F32)],
        compiler_params=_cparams("arbitrary", "arbitrary"),
        name="moe_router",
    )(x, gain, sc, sh, wr, br)


def _row_copy(src, dst, src_row, dst_row, sem):
    return pltpu.make_async_copy(src.at[pl.ds(src_row, 1)], dst.at[pl.ds(dst_row, 1)], sem)


ROUTE_W = 2 * TOP_K
DMA_UNROLL = 8


def _slot(ps_ref, rt_ref, r, k):
    return ps_ref[rt_ref[0, 0, ROUTE_W * r + k]] + rt_ref[0, 0, ROUTE_W * r + TOP_K + k]


def _dispatch_kernel(ps_ref, rt_ref, h_ref, xin_hbm, xb_hbm, sem, *, tb):
    del xin_hbm

    def issue(r, c):
        for k in range(TOP_K):
            _row_copy(h_ref, xb_hbm, r, _slot(ps_ref, rt_ref, r, k), sem).start()
        return c
    lax.fori_loop(0, tb, issue, 0, unroll=DMA_UNROLL)
    for k in range(TOP_K):
        pltpu.make_async_copy(h_ref, xb_hbm.at[pl.ds(0, tb)], sem).wait()


def _dispatch_call(pad_start, route, h2, xb0, tb):
    n, dh = h2.shape
    nb = n // tb
    grid_spec = pltpu.PrefetchScalarGridSpec(
        num_scalar_prefetch=1,
        grid=(nb,),
        in_specs=[pl.BlockSpec((1, 1, ROUTE_W * tb), lambda j, ps: (j, 0, 0), memory_space=pltpu.SMEM),
                  pl.BlockSpec((tb, dh), lambda j, ps: (j, 0)),
                  pl.BlockSpec(memory_space=pl.ANY)],
        out_specs=pl.BlockSpec(memory_space=pl.ANY),
        scratch_shapes=[pltpu.SemaphoreType.DMA(())],
    )
    return pl.pallas_call(
        functools.partial(_dispatch_kernel, tb=tb),
        grid_spec=grid_spec,
        out_shape=jax.ShapeDtypeStruct(xb0.shape, xb0.dtype),
        input_output_aliases={3: 0},
        compiler_params=_cparams("arbitrary"),
        name="moe_dispatch",
    )(pad_start, route, h2, xb0)


def _ffn_kernel(be_ref, nu_ref, x_ref, w1_ref, w3_ref, w2_ref, y_ref, w1b, w3b, w2b):
    i = pl.program_id(0)
    half = w1b.shape[0] // 2

    @pl.when(jnp.logical_and(i < nu_ref[0], jnp.logical_or(i == 0, be_ref[i] != be_ref[jnp.maximum(i - 1, 0)])))
    def _():
        w1b[...] = w1_ref[0, 0].astype(BF16)
        w3b[...] = w3_ref[0, 0].astype(BF16)
        w2b[...] = w2_ref[0, 0].astype(BF16)

    @pl.when(i < nu_ref[0])
    def _():
        lo, hi = _unpack_halves(x_ref[...])
        lo = lo.astype(BF16)
        hi = hi.astype(BF16)
        a = _dot(lo, w1b[:half]) + _dot(hi, w1b[half:])
        g = _dot(lo, w3b[:half]) + _dot(hi, w3b[half:])
        mid = (a * jax.nn.sigmoid(a) * g).astype(BF16)
        y_ref[...] = _pack_halves(_dot(mid, w2b[...]))

    @pl.when(i >= nu_ref[0])
    def _():
        y_ref[...] = jnp.zeros_like(y_ref)


def _ffn_call(block_e, n_used, xb, w1, w3, w2, layer):
    n_blocks = block_e.shape[0]
    dh = xb.shape[1]
    d = 2 * dh
    grid_spec = pltpu.PrefetchScalarGridSpec(
        num_scalar_prefetch=2,
        grid=(n_blocks,),
        in_specs=[pl.BlockSpec((MOE_BLOCK, dh), lambda i, be, nu: (i, 0)),
                  pl.BlockSpec((1, 1, d, D_EXPERT), lambda i, be, nu: (layer, be[i], 0, 0)),
                  pl.BlockSpec((1, 1, d, D_EXPERT), lambda i, be, nu: (layer, be[i], 0, 0)),
                  pl.BlockSpec((1, 1, D_EXPERT, d), lambda i, be, nu: (layer, be[i], 0, 0))],
        out_specs=pl.BlockSpec((MOE_BLOCK, dh), lambda i, be, nu: (i, 0)),
        scratch_shapes=[pltpu.VMEM((d, D_EXPERT), BF16), pltpu.VMEM((d, D_EXPERT), BF16),
                        pltpu.VMEM((D_EXPERT, d), BF16)],
    )
    return pl.pallas_call(
        _ffn_kernel,
        grid_spec=grid_spec,
        out_shape=jax.ShapeDtypeStruct((n_blocks * MOE_BLOCK, dh), jnp.uint32),
        compiler_params=_cparams("arbitrary"),
        name="moe_ffn",
    )(block_e, n_used, xb, w1, w3, w2)


def _combine_kernel(ps_ref, rcur_ref, rnxt_ref, yb_hbm, x_ref, rg_ref, g2_ref, fin_ref, o_ref, b0, b1, sem,
                    *, tb, final):
    j = pl.program_id(0)
    nb = pl.num_programs(0)
    slot = j % 2

    def fetch(rref, sl):
        def issue(r, c):
            _row_copy(yb_hbm, b0.at[sl], _slot(ps_ref, rref, r, 0), r, sem.at[sl]).start()
            _row_copy(yb_hbm, b1.at[sl], _slot(ps_ref, rref, r, 1), r, sem.at[sl]).start()
            return c
        lax.fori_loop(0, tb, issue, 0, unroll=DMA_UNROLL)

    @pl.when(j == 0)
    def _():
        fetch(rcur_ref, 0)

    @pl.when(j + 1 < nb)
    def _():
        fetch(rnxt_ref, 1 - slot)

    pltpu.make_async_copy(yb_hbm.at[pl.ds(0, tb)], b0.at[slot], sem.at[slot]).wait()
    pltpu.make_async_copy(yb_hbm.at[pl.ds(0, tb)], b1.at[slot], sem.at[slot]).wait()
    rg = rg_ref[...]
    lo0, hi0 = _unpack_halves(b0[slot])
    lo1, hi1 = _unpack_halves(b1[slot])
    g0 = rg[:, 0:1]
    g1 = rg[:, 1:2]
    y = jnp.concatenate([g0 * lo0 + g1 * lo1, g0 * hi0 + g1 * hi1], axis=1)
    xn = x_ref[...] + g2_ref[0] * y
    if final:
        xn = _rms(xn, fin_ref[...])
    o_ref[...] = xn


def _combine_call(pad_start, route, yb, x, rg, g2, fin, tb, final):
    b, s, d = x.shape
    n = b * s
    nb = n // tb
    npb = s // tb
    dh = yb.shape[1]
    if g2.shape[1] == 1:
        gspec = pl.BlockSpec((1, 1, d), lambda j, ps: (j // npb, 0, 0))
    else:
        gspec = pl.BlockSpec((1, tb, d), lambda j, ps: (j // npb, j % npb, 0))
    grid_spec = pltpu.PrefetchScalarGridSpec(
        num_scalar_prefetch=1,
        grid=(nb,),
        in_specs=[pl.BlockSpec((1, 1, ROUTE_W * tb), lambda j, ps: (j, 0, 0), memory_space=pltpu.SMEM),
                  pl.BlockSpec((1, 1, ROUTE_W * tb), lambda j, ps: (jnp.minimum(j + 1, nb - 1), 0, 0),
                               memory_space=pltpu.SMEM),
                  pl.BlockSpec(memory_space=pl.ANY),
                  pl.BlockSpec((tb, d), lambda j, ps: (j, 0)), pl.BlockSpec((tb, LANES), lambda j, ps: (j, 0)),
                  gspec, pl.BlockSpec(fin.shape, lambda j, ps: (0, 0))],
        out_specs=pl.BlockSpec((tb, d), lambda j, ps: (j, 0)),
        scratch_shapes=[pltpu.VMEM((2, tb, dh), jnp.uint32), pltpu.VMEM((2, tb, dh), jnp.uint32),
                        pltpu.SemaphoreType.DMA((2,))],
    )
    out = pl.pallas_call(
        functools.partial(_combine_kernel, tb=tb, final=final),
        grid_spec=grid_spec,
        out_shape=jax.ShapeDtypeStruct((n, d), F32),
        compiler_params=_cparams("arbitrary"),
        name="moe_combine",
    )(pad_start, route, route, yb, x.reshape(n, d), rg.reshape(n, LANES), g2, fin)
    return out.reshape(b, s, d)


def _moe(x, gain, sc, sh, g2, mp, fin, ts, final):
    b, s, d = x.shape
    n = b * s
    h2, r_e, r_g, cnt = _router_call(x, gain, sc, sh, mp["wr"], mp["br"], ts)
    counts = cnt[0, N_GROUPS:N_GROUPS + N_EXPERTS].astype(jnp.int32)
    padded = (counts + MOE_BLOCK - 1) // MOE_BLOCK * MOE_BLOCK
    pad_end = jnp.cumsum(padded)
    pad_start = (pad_end - padded).astype(jnp.int32)
    n_blocks = (n * TOP_K + N_EXPERTS * (MOE_BLOCK - 1) + MOE_BLOCK - 1) // MOE_BLOCK
    blk0 = jnp.arange(n_blocks, dtype=jnp.int32) * MOE_BLOCK
    block_e = jnp.minimum(jnp.sum((pad_end[None, :] <= blk0[:, None]).astype(jnp.int32), axis=1), N_EXPERTS - 1)
    n_used = (pad_end[-1:] // MOE_BLOCK).astype(jnp.int32)
    route = r_e[:, :, :ROUTE_W].reshape(n // ts, 1, ROUTE_W * ts)
    xb0 = jnp.zeros((n_blocks * MOE_BLOCK, d // 2), jnp.uint32)
    xb = _dispatch_call(pad_start, route, h2.reshape(n, d // 2), xb0, ts)
    yb = _ffn_call(block_e, n_used, xb, mp["w1"], mp["w3"], mp["w2"], mp["layer"])
    return _combine_call(pad_start, route, yb, x, r_g, g2, fin, ts, final)


def _prep_ab(l, p):
    w_in = p["w_in_ab"][l]
    o2 = S5_WIDTH + MLA_Q_RANK + MLA_KV_RANK
    half = MLA_ROPE // 2
    d = w_in.shape[0]
    kr = w_in[:, o2:]
    kr_rot = jnp.concatenate([-kr[:, half:], kr[:, :half]], axis=1)
    z64 = jnp.zeros((d, MLA_NOPE), F32)
    z32 = jnp.zeros((d, LANES - MLA_NOPE - MLA_ROPE), F32)
    w_ext = jnp.concatenate([w_in[:, :o2], z64, kr, z32, z64, kr_rot, z32], axis=1)
    qu = p["mla_q_up"][l].reshape(MLA_Q_RANK, MLA_HEADS, MLA_NOPE + MLA_ROPE)
    qn, qr = qu[:, :, :MLA_NOPE], qu[:, :, MLA_NOPE:]
    zq = jnp.zeros((MLA_Q_RANK, MLA_HEADS, LANES - MLA_NOPE - MLA_ROPE), F32)
    wqa = jnp.concatenate([qn, qr, zq], axis=2).reshape(MLA_Q_RANK, MLA_HEADS * LANES)
    wqb = jnp.concatenate([jnp.zeros_like(qn), -qr[:, :, half:], qr[:, :, :half], zq], axis=2)
    wqb = wqb.reshape(MLA_Q_RANK, MLA_HEADS * LANES)
    kvu = p["mla_kv_up"][l].reshape(MLA_KV_RANK, MLA_HEADS, MLA_NOPE + MLA_V)
    wk = jnp.concatenate([kvu[:, :, :MLA_NOPE], jnp.zeros((MLA_KV_RANK, MLA_HEADS, LANES - MLA_NOPE), F32)], axis=2)
    wo = p["w_out_ab"][l]
    out = dict(
        w_in=w_ext.astype(BF16), q_norm=p["mla_q_norm"][l].reshape(1, -1), kv_norm=p["mla_kv_norm"][l].reshape(1, -1),
        wqa=wqa.astype(BF16), wqb=wqb.astype(BF16),
        wk=wk.reshape(MLA_KV_RANK, MLA_HEADS * LANES).astype(BF16),
        wv=jnp.concatenate([kvu[:, :, MLA_NOPE:], jnp.zeros((MLA_KV_RANK, MLA_HEADS, LANES - MLA_V), F32)],
                           axis=2).reshape(MLA_KV_RANK, MLA_HEADS * LANES).astype(BF16),
        s5_d=p["s5_d"][l].reshape(1, -1), glu_w=p["s5_glu_w"][l].astype(BF16), glu_b=p["s5_glu_b"][l].reshape(1, -1),
        w_out_s5=wo[:S5_WIDTH].astype(BF16),
        w_out_at=wo[S5_WIDTH:].astype(BF16))
    out.update(_s5_mats(p["s5_a_re"][l], p["s5_a_im"][l], p["s5_b_re"][l], p["s5_b_im"][l],
                        p["s5_c_re"][l], p["s5_c_im"][l], p["s5_log_dt"][l]))
    return out


def _prep_c(l, p):
    w = p["w_in_c"][l]
    pad = jnp.zeros((w.shape[0], LANES - 2 * GDN_HEADS), F32)

    def row(v):
        return jnp.concatenate([v, jnp.zeros((LANES - GDN_HEADS,), F32)]).reshape(1, LANES)

    return dict(w_in=jnp.concatenate([w, pad], axis=1).astype(BF16), conv_w=p["conv_w"][l],
                alog=row(p["gdn_a_log"][l]), dtb=row(p["gdn_dt_bias"][l]),
                gn=p["gdn_norm"][l].reshape(1, -1), w_out=p["w_out_c"][l].astype(BF16))


def _prep_moe(layer, p):
    d = p["moe_w_group"].shape[1]
    wr = jnp.concatenate([p["moe_w_group"][layer], p["moe_w_expert"][layer],
                          jnp.zeros((d, LANES - N_GROUPS - N_EXPERTS), F32)], axis=1)
    br = jnp.concatenate([p["moe_b_group"][layer], p["moe_b_expert"][layer],
                          jnp.zeros((LANES - N_GROUPS - N_EXPERTS,), F32)]).reshape(1, LANES)
    wr_hi = wr.astype(BF16)
    wr_lo = (wr - wr_hi.astype(F32)).astype(BF16)
    return dict(wr=jnp.stack([wr_hi, wr_lo]), br=br, w1=p["moe_w1"], w3=p["moe_w3"], w2=p["moe_w2"], layer=layer)


def _rope_tables(pos):
    half = MLA_ROPE // 2
    inv = ROPE_THETA ** (-jnp.arange(half, dtype=F32) / half)
    ang = pos.astype(F32)[:, None] * inv[None, :]
    cos, sin = jnp.cos(ang), jnp.sin(ang)
    n = pos.shape[0]
    z64 = jnp.zeros((n, MLA_NOPE), F32)
    z32 = jnp.zeros((n, LANES - MLA_NOPE - MLA_ROPE), F32)
    scale = (MLA_NOPE + MLA_ROPE) ** -0.5 * math.log2(math.e)
    ck = jnp.concatenate([z64, cos, cos, z32], axis=1)
    sk = jnp.concatenate([z64, sin, sin, z32], axis=1)
    cq = jnp.concatenate([jnp.ones_like(z64), cos, cos, z32], axis=1) * scale
    return cq, sk * scale, ck, sk


def _trunk(x, mods, caches, wab, wc, wmoe, p, flat):
    b, s, d = x.shape
    lat_c, rope_c, s5re_c, s5im_c, conv_c, gdn_c = caches
    past = 0 if lat_c is None else lat_c.shape[2]
    pos = past + jnp.arange(s)
    if flat:
        xt = x.reshape(1, b * s, d)
        ts = b * s
        pos_rows = jnp.tile(pos, b)

        def mrow(m):
            return jnp.repeat(m, s, axis=0).reshape(1, b * s, d)
    else:
        xt = x
        ts = 256
        pos_rows = pos

        def mrow(m):
            return m.reshape(b, 1, d)
    bt, st = xt.shape[:2]
    outs = {}
    depth = p["norm_mix"].shape[0]
    for layer in range(depth):
        l = layer // 2
        sh1, sc1, g1, sh2, sc2, g2 = [mrow(m) for m in jnp.split(mods[layer], 6, axis=-1)]
        gain = p["norm_mix"][layer].reshape(1, d)
        if layer % 2 == 0:
            wp = wab[l]
            u, q, lat, kr = _ab_in_call(xt, gain, sc1, sh1, wp, _rope_tables(pos_rows), ts)
            lat_b = lat.reshape(b, s, MLA_KV_RANK)
            kr_b = kr.reshape(b, s, LANES)
            outs["lat"] = lat_b
            outs["krope"] = kr_b[:, :, MLA_NOPE:MLA_NOPE + MLA_ROPE]
            if lat_c is None:
                lat_all, kr_all, sk_valid, tkv, tq, tk = lat_b, kr_b, s, 256, min(512, s), min(512, s)
            else:
                krc = jnp.pad(rope_c[l], ((0, 0), (0, 0), (MLA_NOPE, LANES - MLA_NOPE - MLA_ROPE)))
                sk_valid = past + s
                skp = (sk_valid + LANES - 1) // LANES * LANES
                lat_all = jnp.pad(jnp.concatenate([lat_c[l], lat_b], axis=1), ((0, 0), (0, skp - sk_valid), (0, 0)))
                kr_all = jnp.pad(jnp.concatenate([krc, kr_b], axis=1), ((0, 0), (0, skp - sk_valid), (0, 0)))
                tkv, tq, tk = skp, s, skp
            kk, vv = _kv_call(lat_all, kr_all, wp, tkv)
            qh = q.reshape(MLA_HEADS, b, s, LANES).transpose(1, 0, 2, 3) if flat else q
            attn = _attn_call(qh, kk, vv, tq, tk, past, sk_valid).reshape(bt, st, MLA_HEADS * MLA_V)
            nchk = s // S5_CHUNK
            r = b * nchk
            u2 = u.reshape(b * s, S5_WIDTH)
            tr = min(r, 256)
            wre, wim = _s5_w_call(u2, wp["bre"], wp["bim"], tr)
            ngp = S5_GROUPS * S5_STATE

            def to_pairs(v):
                return v.reshape(b, 1, ngp)

            if s5re_c is None:
                x0re = jnp.zeros((b, 1, ngp), F32)
                x0im = jnp.zeros((b, 1, ngp), F32)
            else:
                x0re, x0im = to_pairs(s5re_c[l]), to_pairs(s5im_c[l])
            nchp = (nchk + 7) // 8 * 8
            tc = min(nchp, 128)
            wre3 = jnp.pad(wre.reshape(b, nchk, ngp), ((0, 0), (0, nchp - nchk), (0, 0)))
            wim3 = jnp.pad(wim.reshape(b, nchk, ngp), ((0, 0), (0, nchp - nchk), (0, 0)))
            xere, xeim = _s5_scan_call(wre3, wim3, wp["lre"], wp["lim"], x0re, x0im, tc)
            outs["s5re"] = xere[:, nchk - 1].reshape(b, S5_GROUPS, S5_STATE)
            outs["s5im"] = xeim[:, nchk - 1].reshape(b, S5_GROUPS, S5_STATE)
            xsre = jnp.concatenate([x0re, xere[:, :nchk - 1]], axis=1).reshape(r, ngp)
            xsim = jnp.concatenate([x0im, xeim[:, :nchk - 1]], axis=1).reshape(r, ngp)
            ys = _s5_y_call(u2, xsre, xsim, wp["mst"], wp["cc"], tr).reshape(bt, st, S5_WIDTH)
            xt = _ab_out_call(xt, ys, u, attn, wp, g1, ts)
        else:
            wp = wc[l]
            qkv, z, ab = _c_in_call(xt, gain, sc1, sh1, wp["w_in"], ts)
            qkv_b = qkv.reshape(b, s, GDN_QKV)
            outs["conv"] = qkv_b[:, s - (CONV_W - 1):].astype(F32)
            if conv_c is None:
                st8 = jnp.zeros((b, 8, GDN_QKV), BF16)
                s0 = jnp.zeros((b, GDN_HEADS, GDN_DK, GDN_DV), F32)
            else:
                st8 = jnp.pad(conv_c[l], ((0, 0), (8 - (CONV_W - 1), 0), (0, 0))).astype(BF16)
                s0 = gdn_c[l]
            lg = min(s, CHUNK)
            wq, um, kgt, pm, ee = _gdn_prep_call(qkv_b, st8, wp["conv_w"], ab.reshape(b, s, LANES),
                                                 wp["alog"], wp["dtb"], lg)
            o, sfin = _gdn_seq_call(wq, um, kgt, pm, ee, s0, lg)
            outs["gdn"] = sfin
            xt = _c_out_call(xt, o.reshape(bt, st, GDN_VW), z, wp["gn"], wp["w_out"], g1, ts)
        final = layer == depth - 1
        xt = _moe(xt, p["norm_ffn"][layer].reshape(1, d), sc2, sh2, g2, wmoe[layer],
                  p["norm_final"].reshape(1, d), ts, final)
    y = xt.reshape(b, s, d)
    return (y, outs["lat"][None], outs["krope"][None], outs["s5re"][None], outs["s5im"][None],
            outs["conv"][None], outs["gdn"][None])


def kernel(x_prompt, x_sample, c_prompt, c_sample, cache_mla_latent, cache_mla_krope, state_s5_re, state_s5_im,
           state_conv, state_gdn, w_mod, b_mod, norm_mix, norm_ffn, norm_final, w_in_ab, s5_a_re, s5_a_im,
           s5_b_re, s5_b_im, s5_c_re, s5_c_im, s5_d, s5_log_dt, s5_glu_w, s5_glu_b, mla_q_norm, mla_q_up,
           mla_kv_norm, mla_kv_up, w_out_ab, w_in_c, conv_w, gdn_a_log, gdn_dt_bias, gdn_norm, w_out_c,
           moe_w_group, moe_b_group, moe_w_expert, moe_b_expert, moe_w1, moe_w3, moe_w2):
    p = dict(w_mod=w_mod, b_mod=b_mod, norm_mix=norm_mix, norm_ffn=norm_ffn, norm_final=norm_final,
             w_in_ab=w_in_ab, s5_a_re=s5_a_re, s5_a_im=s5_a_im, s5_b_re=s5_b_re, s5_b_im=s5_b_im,
             s5_c_re=s5_c_re, s5_c_im=s5_c_im, s5_d=s5_d, s5_log_dt=s5_log_dt, s5_glu_w=s5_glu_w,
             s5_glu_b=s5_glu_b, mla_q_norm=mla_q_norm, mla_q_up=mla_q_up, mla_kv_norm=mla_kv_norm,
             mla_kv_up=mla_kv_up, w_out_ab=w_out_ab, w_in_c=w_in_c, conv_w=conv_w, gdn_a_log=gdn_a_log,
             gdn_dt_bias=gdn_dt_bias, gdn_norm=gdn_norm, w_out_c=w_out_c, moe_w_group=moe_w_group,
             moe_b_group=moe_b_group, moe_w_expert=moe_w_expert, moe_b_expert=moe_b_expert,
             moe_w1=moe_w1, moe_w3=moe_w3, moe_w2=moe_w2)
    depth = norm_mix.shape[0]
    bp, bs = c_prompt.shape[0], c_sample.shape[0]
    nb = (bp + bs + 7) // 8 * 8
    c_all = jnp.pad(jnp.concatenate([c_prompt, c_sample], axis=0), ((0, nb - bp - bs), (0, 0)))
    mods = _mod_call(c_all, w_mod, b_mod)
    wab = [_prep_ab(l, p) for l in range((depth + 1) // 2)]
    wc = [_prep_c(l, p) for l in range(depth // 2)]
    wmoe = [_prep_moe(layer, p) for layer in range(depth)]
    none6 = (None,) * 6
    outp = _trunk(x_prompt, mods[:, :bp], none6, wab, wc, wmoe, p, flat=False)
    caches = (cache_mla_latent, cache_mla_krope, state_s5_re, state_s5_im, state_conv, state_gdn)
    outs = _trunk(x_sample, mods[:, bp:bp + bs], caches, wab, wc, wmoe, p, flat=True)
    return (outp[0], outs[0]) + tuple(outp[1:]) + tuple(outs[1:])
```

```python
import functools
import math

import jax
import jax.numpy as jnp
from jax import lax
from jax.experimental import pallas as pl
from jax.experimental.pallas import tpu as pltpu

F32 = jnp.float32
BF16 = jnp.bfloat16
HI = lax.Precision.HIGHEST
EPS = 1e-6

D_MODEL = 1024
CHUNK = 64
S5_WIDTH = 512
S5_GROUP = 16
S5_GROUPS = 32
S5_STATE = 64
S5_CHUNK = 16
S5_LB = 128 // S5_GROUP
MLA_HEADS = 8
MLA_NOPE = 64
MLA_ROPE = 32
MLA_V = 64
MLA_Q_RANK = 384
MLA_KV_RANK = 256
ROPE_THETA = 10000.0
GDN_HEADS = 8
GDN_DK = 128
GDN_DV = 128
CONV_W = 4
GDN_QKV = GDN_HEADS * (2 * GDN_DK + GDN_DV)
GDN_VW = GDN_HEADS * GDN_DV
N_GROUPS = 4
EXPERTS_PER_GROUP = 8
N_EXPERTS = 32
D_EXPERT = 512
TOP_K = 2
MOE_BLOCK = 512
ATTN_SUB = 512
LANES = 128
VMEM_LIMIT = 48 * 1024 * 1024


def _cparams(*sem):
    return pltpu.CompilerParams(dimension_semantics=sem, vmem_limit_bytes=VMEM_LIMIT)


def _dot(a, b):
    return jnp.dot(a, b, preferred_element_type=F32)


def _dot_nt(a, b, precision=None):
    return lax.dot_general(a, b, (((1,), (1,)), ((), ())), preferred_element_type=F32, precision=precision)


def _full(arr):
    nd = arr.ndim
    return pl.BlockSpec(arr.shape, lambda *_: (0,) * nd)


def _tok(width, ts):
    return pl.BlockSpec((1, ts, width), lambda b, j: (b, j, 0))


def _modspec(arr, ts):
    if arr.shape[1] == 1:
        return pl.BlockSpec((1, 1, arr.shape[2]), lambda b, j: (b, 0, 0))
    return pl.BlockSpec((1, ts, arr.shape[2]), lambda b, j: (b, j, 0))


def _normmod(x, gain, sc, sh):
    ms = jnp.mean(x * x, axis=-1, keepdims=True)
    return x * lax.rsqrt(ms + EPS) * gain * (1.0 + sc) + sh


def _rms(x, gain):
    return x * lax.rsqrt(jnp.mean(x * x, axis=-1, keepdims=True) + EPS) * gain


def _mod_kernel(c_ref, w_ref, b_ref, o_ref):
    c = c_ref[...]
    a = (c * jax.nn.sigmoid(c)).astype(BF16)
    o_ref[0] = _dot(a, w_ref[0]) + b_ref[0]


def _mod_call(c_all, w_mod, b_mod):
    depth, d, n6 = w_mod.shape
    bp = c_all.shape[0]
    tn = 1536
    return pl.pallas_call(
        _mod_kernel,
        grid=(depth, n6 // tn),
        in_specs=[pl.BlockSpec((bp, d), lambda l, n: (0, 0)),
                  pl.BlockSpec((1, d, tn), lambda l, n: (l, 0, n)),
                  pl.BlockSpec((1, 1, tn), lambda l, n: (l, 0, n))],
        out_specs=pl.BlockSpec((1, bp, tn), lambda l, n: (l, 0, n)),
        out_shape=jax.ShapeDtypeStruct((depth, bp, n6), F32),
        compiler_params=_cparams("parallel", "parallel"),
        name="mod",
    )(c_all, w_mod.astype(BF16), b_mod.reshape(depth, 1, n6))


def _ab_in_kernel(x_ref, gain_ref, sc_ref, sh_ref, w_ref, qg_ref, wqa_ref, wqb_ref, kvg_ref,
                  cq_ref, sq_ref, ck_ref, sk_ref, u_ref, q_ref, lat_ref, kr_ref):
    h = _normmod(x_ref[0], gain_ref[...], sc_ref[0], sh_ref[0]).astype(BF16)
    proj = _dot(h, w_ref[...])
    u_ref[0] = proj[:, :S5_WIDTH]
    o1 = S5_WIDTH + MLA_Q_RANK
    o2 = o1 + MLA_KV_RANK
    qn = _rms(proj[:, S5_WIDTH:o1], qg_ref[...]).astype(BF16)
    qa = _dot(qn, wqa_ref[...])
    qb = _dot(qn, wqb_ref[...])
    cq = cq_ref[...]
    sq = sq_ref[...]
    for hh in range(MLA_HEADS):
        sl = slice(LANES * hh, LANES * (hh + 1))
        q_ref[0, hh] = (qa[:, sl] * cq + qb[:, sl] * sq).astype(BF16)
    lat_ref[0] = _rms(proj[:, o1:o2], kvg_ref[...])
    kr_ref[0] = proj[:, o2:o2 + LANES] * ck_ref[...] + proj[:, o2 + LANES:o2 + 2 * LANES] * sk_ref[...]


def _ab_in_call(x, gain, sc, sh, wp, tabs, ts):
    b, s, d = x.shape
    cq, sq, ck, sk = tabs
    tab = pl.BlockSpec((ts, LANES), lambda bb, j: (j, 0))
    ins = [x, gain, sc, sh, wp["w_in"], wp["q_norm"], wp["wqa"], wp["wqb"], wp["kv_norm"], cq, sq, ck, sk]
    specs = [_tok(d, ts), _full(gain), _modspec(sc, ts), _modspec(sh, ts), _full(wp["w_in"]), _full(wp["q_norm"]),
             _full(wp["wqa"]), _full(wp["wqb"]), _full(wp["kv_norm"]), tab, tab, tab, tab]
    return pl.pallas_call(
        _ab_in_kernel,
        grid=(b, s // ts),
        in_specs=specs,
        out_specs=[_tok(S5_WIDTH, ts),
                   pl.BlockSpec((1, MLA_HEADS, ts, LANES), lambda bb, j: (bb, 0, j, 0)),
                   _tok(MLA_KV_RANK, ts), _tok(LANES, ts)],
        out_shape=[jax.ShapeDtypeStruct((b, s, S5_WIDTH), F32),
                   jax.ShapeDtypeStruct((b, MLA_HEADS, s, LANES), BF16),
                   jax.ShapeDtypeStruct((b, s, MLA_KV_RANK), F32),
                   jax.ShapeDtypeStruct((b, s, LANES), F32)],
        compiler_params=_cparams("parallel", "parallel"),
        name="ab_in",
    )(*ins)


def _kv_kernel(lat_ref, kr_ref, wk_ref, wv_ref, k_ref, v_ref):
    lat = lat_ref[0].astype(BF16)
    kk = _dot(lat, wk_ref[...])
    vv = _dot(lat, wv_ref[...])
    kr = kr_ref[0]
    ones = (lax.broadcasted_iota(jnp.int32, (1, LANES), 1) >= MLA_V).astype(F32)
    for hh in range(MLA_HEADS):
        k_ref[0, hh] = (kk[:, LANES * hh:LANES * (hh + 1)] + kr).astype(BF16)
        v_ref[0, hh] = (vv[:, LANES * hh:LANES * (hh + 1)] + ones).astype(BF16)


def _kv_call(lat_all, kr_all, wp, ts):
    b, sk, _ = lat_all.shape
    hspec = pl.BlockSpec((1, MLA_HEADS, ts, LANES), lambda bb, j: (bb, 0, j, 0))
    hshape = jax.ShapeDtypeStruct((b, MLA_HEADS, sk, LANES), BF16)
    return pl.pallas_call(
        _kv_kernel,
        grid=(b, sk // ts),
        in_specs=[_tok(MLA_KV_RANK, ts), _tok(LANES, ts), _full(wp["wk"]), _full(wp["wv"])],
        out_specs=[hspec, hspec],
        out_shape=[hshape, hshape],
        compiler_params=_cparams("parallel", "parallel"),
        name="kv_up",
    )(lat_all, kr_all, wp["wk"], wp["wv"])


def _attn_kernel(q_ref, k_ref, v_ref, o_ref, *, tq, tk, q_off, sk_valid):
    i = pl.program_id(2)
    nsub = max(tq // ATTN_SUB, 1)
    tqs = tq // nsub
    chains = [(hh, qi) for hh in range(2) for qi in range(nsub)]
    qs = [q_ref[0, hh, qi * tqs:(qi + 1) * tqs, :] for hh, qi in chains]
    q0 = q_off + i * tq
    lim_full = (q0 // CHUNK + 1) * CHUNK
    lim_tot = jnp.minimum(((q0 + tq - 1) // CHUNK + 1) * CHUNK, sk_valid)
    n_full = jnp.minimum(lim_full // tk, sk_valid // tk)
    n_tot = (lim_tot + tk - 1) // tk

    def step(j, carry, masked):
        off = pl.multiple_of(j * tk, tk)
        nc = range(len(chains))
        if masked:
            kpos = off + lax.broadcasted_iota(jnp.int32, (tqs, tk), 1)
            ok = []
            for qi in range(nsub):
                qpos = q0 + qi * tqs + lax.broadcasted_iota(jnp.int32, (tqs, tk), 0)
                ok.append(jnp.logical_and(kpos // CHUNK <= qpos // CHUNK, kpos < sk_valid))
        kt = [k_ref[0, hh, pl.ds(off, tk), :] for hh in range(2)]
        vt = [v_ref[0, hh, pl.ds(off, tk), :] for hh in range(2)]
        s = [_dot_nt(qs[c], kt[chains[c][0]]) for c in nc]
        if masked:
            s = [jnp.where(ok[chains[c][1]], s[c], -1e30) for c in nc]
        m_new = [jnp.maximum(carry[c][0], jnp.max(s[c], axis=-1, keepdims=True)) for c in nc]
        alpha = [jnp.exp2(carry[c][0] - m_new[c]) for c in nc]
        p = [jnp.exp2(s[c] - m_new[c]).astype(BF16) for c in nc]
        acc = [alpha[c] * carry[c][1] + _dot(p[c], vt[chains[c][0]]) for c in nc]
        return tuple((m_new[c], acc[c]) for c in nc)

    one = (jnp.full((tqs, 1), -1e30, F32), jnp.zeros((tqs, LANES), F32))
    carry = lax.fori_loop(0, n_full, functools.partial(step, masked=False), (one,) * len(chains))
    carry = lax.fori_loop(n_full, n_tot, functools.partial(step, masked=True), carry)
    lane = lax.broadcasted_iota(jnp.int32, (tqs, LANES), 1)
    for qi in range(nsub):
        a0 = carry[chains.index((0, qi))][1]
        a1 = carry[chains.index((1, qi))][1]
        r0 = a0 / pltpu.roll(a0, MLA_V, axis=1)
        r1 = a1 / pltpu.roll(a1, MLA_V, axis=1)
        o_ref[0, qi * tqs:(qi + 1) * tqs, :] = jnp.where(
            lane < MLA_V, r0, pltpu.roll(r1, MLA_V, axis=1)).astype(o_ref.dtype)


def _attn_call(q, k, v, tq, tk, q_off, sk_valid):
    b, h, sq, _ = q.shape
    sk = k.shape[2]
    kern = functools.partial(_attn_kernel, tq=tq, tk=tk, q_off=q_off, sk_valid=sk_valid)
    kv_mode = pl.Buffered(2 if sq == tq else 1)
    return pl.pallas_call(
        kern,
        grid=(b, h // 2, sq // tq),
        in_specs=[pl.BlockSpec((1, 2, tq, LANES), lambda bb, hp, i: (bb, hp, i, 0)),
                  pl.BlockSpec((1, 2, sk, LANES), lambda bb, hp, i: (bb, hp, 0, 0), pipeline_mode=kv_mode),
                  pl.BlockSpec((1, 2, sk, LANES), lambda bb, hp, i: (bb, hp, 0, 0), pipeline_mode=kv_mode)],
        out_specs=pl.BlockSpec((1, tq, LANES), lambda bb, hp, i: (bb, i, hp)),
        out_shape=jax.ShapeDtypeStruct((b, sq, h * MLA_V), BF16),
        compiler_params=_cparams("parallel", "parallel", "arbitrary"),
        name="mla_attn",
    )(q, k, v)


def _chunk_rows(ref, s, tr):
    return ref[pl.ds(s, tr, stride=S5_CHUNK), :]


def _s5_w_kernel(u_ref, bre_ref, bim_ref, wre_ref, wim_ref, *, tr):
    ucat = jnp.concatenate([_chunk_rows(u_ref, s, tr).astype(BF16) for s in range(S5_CHUNK)], axis=1)
    wre_ref[...] = _dot(ucat, bre_ref[0])
    wim_ref[...] = _dot(ucat, bim_ref[0])


def _s5_w_call(u2, bre, bim, tr):
    n = u2.shape[0]
    r = n // S5_CHUNK
    nq = bre.shape[0]
    sw = S5_LB * S5_STATE
    out = jax.ShapeDtypeStruct((r, nq * sw), F32)
    ospec = pl.BlockSpec((tr, sw), lambda q, i: (i, q))
    wspec = pl.BlockSpec((1, S5_CHUNK * LANES, sw), lambda q, i: (q, 0, 0))
    return pl.pallas_call(
        functools.partial(_s5_w_kernel, tr=tr),
        grid=(nq, r // tr),
        in_specs=[pl.BlockSpec((tr * S5_CHUNK, LANES), lambda q, i: (i, q)), wspec, wspec],
        out_specs=[ospec, ospec],
        out_shape=[out, out],
        compiler_params=_cparams("parallel", "parallel"),
        name="s5_chunk_in",
    )(u2, bre, bim)


def _s5_scan_kernel(wre_ref, wim_ref, lre_ref, lim_ref, x0re_ref, x0im_ref, ore_ref, oim_ref, sre, sim, *, tc):
    @pl.when(pl.program_id(1) == 0)
    def _():
        sre[...] = x0re_ref[0]
        sim[...] = x0im_ref[0]

    lr = lre_ref[...]
    li = lim_ref[...]

    def blk(t, carry):
        xr, xi = carry
        base = pl.multiple_of(t * 8, 8)
        wr = wre_ref[0, pl.ds(base, 8), :]
        wi = wim_ref[0, pl.ds(base, 8), :]
        rows_r, rows_i = [], []
        for r in range(8):
            nr = lr * xr - li * xi + wr[r:r + 1]
            ni = lr * xi + li * xr + wi[r:r + 1]
            xr, xi = nr, ni
            rows_r.append(xr)
            rows_i.append(xi)
        ore_ref[0, pl.ds(base, 8), :] = jnp.concatenate(rows_r, axis=0)
        oim_ref[0, pl.ds(base, 8), :] = jnp.concatenate(rows_i, axis=0)
        return xr, xi

    xr, xi = lax.fori_loop(0, tc // 8, blk, (sre[...], sim[...]))
    sre[...] = xr
    sim[...] = xi


def _s5_scan_call(wre, wim, lre, lim, x0re, x0im, tc):
    b, nch, n = wre.shape
    wspec = pl.BlockSpec((1, tc, n), lambda bb, c: (bb, c, 0))
    lspec = pl.BlockSpec((1, n), lambda bb, c: (0, 0))
    xspec = pl.BlockSpec((1, 1, n), lambda bb, c: (bb, 0, 0))
    out = jax.ShapeDtypeStruct((b, nch, n), F32)
    return pl.pallas_call(
        functools.partial(_s5_scan_kernel, tc=tc),
        grid=(b, nch // tc),
        in_specs=[wspec, wspec, lspec, lspec, xspec, xspec],
        out_specs=[wspec, wspec],
        out_shape=[out, out],
        scratch_shapes=[pltpu.VMEM((1, n), F32), pltpu.VMEM((1, n), F32)],
        compiler_params=_cparams("parallel", "arbitrary"),
        name="s5_scan",
    )(wre, wim, lre, lim, x0re, x0im)


def _s5_y_kernel(u_ref, xre_ref, xim_ref, m_ref, c_ref, y_ref, *, tr):
    L = S5_CHUNK
    urev = jnp.concatenate([_chunk_rows(u_ref, s, tr).astype(BF16) for s in reversed(range(L))], axis=1)
    xcat = jnp.concatenate([xre_ref[...].astype(BF16), xim_ref[...].astype(BF16)], axis=1)
    for t in range(L):
        y = _dot(urev[:, (L - 1 - t) * LANES:], m_ref[0, :(t + 1) * LANES, :]) + _dot(xcat, c_ref[0, t])
        y_ref[pl.ds(t, tr, stride=L), :] = y


def _s5_y_call(u2, xre, xim, mst, cc, tr):
    n, w = u2.shape
    nq = mst.shape[0]
    sw = S5_LB * S5_STATE
    uspec = pl.BlockSpec((tr * S5_CHUNK, LANES), lambda q, i: (i, q))
    xspec = pl.BlockSpec((tr, sw), lambda q, i: (i, q))
    return pl.pallas_call(
        functools.partial(_s5_y_kernel, tr=tr),
        grid=(nq, n // (tr * S5_CHUNK)),
        in_specs=[uspec, xspec, xspec,
                  pl.BlockSpec((1, S5_CHUNK * LANES, LANES), lambda q, i: (q, 0, 0)),
                  pl.BlockSpec((1, S5_CHUNK, 2 * sw, LANES), lambda q, i: (q, 0, 0, 0))],
        out_specs=uspec,
        out_shape=jax.ShapeDtypeStruct((n, w), F32),
        compiler_params=_cparams("parallel", "parallel"),
        name="s5_chunk_out",
    )(u2, xre, xim, mst, cc)


def _s5_mats(a_re, a_im, b_re, b_im, c_re, c_im, log_dt):
    g, p = a_re.shape
    L = S5_CHUNK
    lam = lax.complex(jnp.minimum(a_re, -1e-4), a_im)
    lamdt = lam * jnp.exp(log_dt)[:, None]
    lam_bar = jnp.exp(lamdt)
    b_bar = ((lam_bar - 1.0) / lam)[:, :, None] * lax.complex(b_re, b_im)
    cm = lax.complex(c_re, c_im)
    pw = jnp.exp(lamdt[:, :, None] * jnp.arange(L + 1, dtype=F32))
    mm = pw[:, :, :L, None] * b_bar[:, :, None, :]
    kd = jnp.sum((cm[:, :, :, None, None] * mm[:, None]).real, axis=2).transpose(0, 2, 1, 3)
    bp = pw[:, :, L - 1 - jnp.arange(L)][:, :, :, None] * b_bar[:, :, None, :]
    cp = cm[:, :, :, None] * pw[:, None, :, 1:]
    nq = g // S5_LB
    eye = jnp.eye(S5_LB, dtype=F32)

    def lane_block(m, spec, rows, cols):
        return jnp.einsum(spec, m.reshape((nq, S5_LB) + m.shape[1:]), eye).reshape(nq, rows, cols)

    sw = S5_LB * p
    bre = lane_block(bp.real, "qgpsc,hg->qshcgp", L * LANES, sw)
    bim = lane_block(bp.imag, "qgpsc,hg->qshcgp", L * LANES, sw)
    mst = lane_block(kd, "qgdoc,hg->qdhcgo", L * LANES, LANES)
    cre = lane_block(cp.real, "qgopt,hg->qthpgo", L * sw, LANES).reshape(nq, L, sw, LANES)
    cim = lane_block(-cp.imag, "qgopt,hg->qthpgo", L * sw, LANES).reshape(nq, L, sw, LANES)
    return dict(
        bre=bre.astype(BF16), bim=bim.astype(BF16), mst=mst.astype(BF16),
        cc=jnp.concatenate([cre, cim], axis=2).astype(BF16),
        lre=pw[:, :, L].real.reshape(1, g * p), lim=pw[:, :, L].imag.reshape(1, g * p))


def _ab_out_kernel(x_ref, ys_ref, u_ref, at_ref, d_ref, gw_ref, gb_ref, ws_ref, wa_ref, g1_ref, o_ref):
    y = ys_ref[0] + d_ref[...] * u_ref[0]
    y = jax.nn.gelu(y)
    gate = jax.nn.sigmoid(_dot(y.astype(BF16), gw_ref[...]) + gb_ref[...])
    out = _dot((y * gate).astype(BF16), ws_ref[...]) + _dot(at_ref[0], wa_ref[...])
    o_ref[0] = x_ref[0] + g1_ref[0] * out


def _ab_out_call(x, ys, u, attn, wp, g1, ts):
    b, s, d = x.shape
    ins = [x, ys, u, attn, wp["s5_d"], wp["glu_w"], wp["glu_b"], wp["w_out_s5"], wp["w_out_at"], g1]
    specs = [_tok(d, ts), _tok(S5_WIDTH, ts), _tok(S5_WIDTH, ts), _tok(MLA_HEADS * MLA_V, ts),
             _full(wp["s5_d"]), _full(wp["glu_w"]), _full(wp["glu_b"]), _full(wp["w_out_s5"]),
             _full(wp["w_out_at"]), _modspec(g1, ts)]
    return pl.pallas_call(
        _ab_out_kernel,
        grid=(b, s // ts),
        in_specs=specs,
        out_specs=_tok(d, ts),
        out_shape=jax.ShapeDtypeStruct((b, s, d), F32),
        compiler_params=_cparams("parallel", "parallel"),
        name="ab_out",
    )(*ins)


def _c_in_kernel(x_ref, gain_ref, sc_ref, sh_ref, w_ref, qkv_ref, z_ref, ab_ref):
    h = _normmod(x_ref[0], gain_ref[...], sc_ref[0], sh_ref[0]).astype(BF16)
    proj = _dot(h, w_ref[...])
    qkv_ref[0] = proj[:, :GDN_QKV].astype(BF16)
    z_ref[0] = proj[:, GDN_QKV:GDN_QKV + GDN_VW].astype(BF16)
    ab_ref[0] = proj[:, GDN_QKV + GDN_VW:]


def _c_in_call(x, gain, sc, sh, w, ts):
    b, s, d = x.shape
    return pl.pallas_call(
        _c_in_kernel,
        grid=(b, s // ts),
        in_specs=[_tok(d, ts), _full(gain), _modspec(sc, ts), _modspec(sh, ts), _full(w)],
        out_specs=[_tok(GDN_QKV, ts), _tok(GDN_VW, ts), _tok(LANES, ts)],
        out_shape=[jax.ShapeDtypeStruct((b, s, GDN_QKV), BF16),
                   jax.ShapeDtypeStruct((b, s, GDN_VW), BF16),
                   jax.ShapeDtypeStruct((b, s, LANES), F32)],
        compiler_params=_cparams("parallel", "parallel"),
        name="c_in",
    )(x, gain, sc, sh, w)


def _gdn_prep_kernel(qkv_ref, halo_ref, st_ref, cw_ref, ab_ref, alog_ref, dtb_ref,
                     wq_ref, u_ref, kgt_ref, pm_ref, ee_ref, *, L, nc):
    j = pl.program_id(1)
    rows = nc * L
    prev = jnp.where(j == 0, st_ref[0].astype(F32), halo_ref[0].astype(F32))
    xx = jnp.concatenate([prev, qkv_ref[0].astype(F32)], axis=0)
    cw = cw_ref[...]
    y = xx[8:8 + rows] * cw[CONV_W - 1:CONV_W]
    for t in range(1, CONV_W):
        y += xx[8 - t:8 - t + rows] * cw[CONV_W - 1 - t:CONV_W - t]
    y = y * jax.nn.sigmoid(y)

    ab = ab_ref[0]
    g_all = -jnp.exp(alog_ref[...]) * jax.nn.softplus(ab + dtb_ref[...])
    beta_all = jax.nn.sigmoid(ab)
    row = lax.broadcasted_iota(jnp.int32, (L, L), 0)
    col = lax.broadcasted_iota(jnp.int32, (L, L), 1)
    incl = row >= col
    strict = row > col
    brow = lax.broadcasted_iota(jnp.int32, (rows, rows), 0)
    bcol = lax.broadcasted_iota(jnp.int32, (rows, rows), 1)
    incl_blk = jnp.logical_and(brow >= bcol, brow // L == bcol // L)
    gam_all = jnp.dot(incl_blk.astype(F32), g_all, preferred_element_type=F32, precision=HI)
    eye_l = (lax.broadcasted_iota(jnp.int32, (LANES, LANES), 0)
             == lax.broadcasted_iota(jnp.int32, (LANES, LANES), 1))
    gam_t = _dot_nt(eye_l.astype(F32), gam_all, precision=HI)
    for ci in range(nc):
        ee_ref[0, ci] = jnp.exp(gam_all[(ci + 1) * L - 1:(ci + 1) * L])
    eye64 = (row == col).astype(F32)
    eye_bf = eye_l.astype(BF16)

    ch = [(ci, h) for ci in range(nc) for h in range(GDN_HEADS)]
    cs = range(len(ch))
    koff, voff = GDN_HEADS * GDN_DK, 2 * GDN_HEADS * GDN_DK
    qh = [y[ci * L:(ci + 1) * L, GDN_DK * h:GDN_DK * (h + 1)] for ci, h in ch]
    kh = [y[ci * L:(ci + 1) * L, koff + GDN_DK * h:koff + GDN_DK * (h + 1)] for ci, h in ch]
    vh = [y[ci * L:(ci + 1) * L, voff + GDN_DV * h:voff + GDN_DV * (h + 1)] for ci, h in ch]
    qh = [q * lax.rsqrt(jnp.sum(q * q, axis=-1, keepdims=True) + EPS) * (GDN_DK ** -0.5) for q in qh]
    kh = [k * lax.rsqrt(jnp.sum(k * k, axis=-1, keepdims=True) + EPS) for k in kh]
    gam_c = [gam_all[ci * L:(ci + 1) * L, h:h + 1] for ci, h in ch]
    beta_c = [beta_all[ci * L:(ci + 1) * L, GDN_HEADS + h:GDN_HEADS + h + 1] for ci, h in ch]
    dec = [jnp.exp(jnp.minimum(gam_c[c] - gam_t[h:h + 1, ci * L:(ci + 1) * L], 0.0))
           for c, (ci, h) in enumerate(ch)]
    kb = [k.astype(BF16) for k in kh]
    kk = [_dot_nt(kb[c], kb[c]) for c in cs]
    qk = [_dot_nt(qh[c].astype(BF16), kb[c]) for c in cs]
    a = [jnp.where(strict, beta_c[c] * kk[c] * dec[c], 0.0) for c in cs]
    for c, (ci, h) in enumerate(ch):
        pm_ref[0, ci, h] = jnp.where(incl, qk[c] * dec[c], 0.0).astype(BF16)
    tinv = [eye64 - a[c] for c in cs]
    pw = [_split(a[c]) for c in cs]
    for _ in range(max(L.bit_length() - 2, 0)):
        pw = [_split(_dot3(pw[c], pw[c])) for c in cs]
        tinv = [tinv[c] + _dot3(_split(tinv[c]), pw[c]) for c in cs]
    eg = [jnp.exp(gam_c[c]) for c in cs]
    rhs = [jnp.concatenate([(beta_c[c] * eg[c]) * kh[c], beta_c[c] * vh[c]], axis=1).astype(BF16) for c in cs]
    wu = [_dot(tinv[c].astype(BF16), rhs[c]) for c in cs]
    for c, (ci, h) in enumerate(ch):
        wq_ref[0, ci, h] = jnp.concatenate([wu[c][:, :GDN_DK], qh[c] * eg[c]], axis=0).astype(BF16)
        u_ref[0, h, ci * L:(ci + 1) * L, :] = wu[c][:, GDN_DK:].astype(BF16)
    kg = [(kh[c] * jnp.exp(gam_all[(ci + 1) * L - 1:(ci + 1) * L, h:h + 1] - gam_c[c])).astype(BF16)
          for c, (ci, h) in enumerate(ch)]
    kgt = [_dot_nt(eye_bf, kg[c]) for c in cs]
    for c, (ci, h) in enumerate(ch):
        kgt_ref[0, ci, h] = kgt[c].astype(BF16)


def _split(a):
    hi = a.astype(BF16)
    return hi, (a - hi.astype(F32)).astype(BF16)


def _dot3(a, b):
    return _dot(a[0], b[0]) + (_dot(a[0], b[1]) + _dot(a[1], b[0]))


GDN_PREP_CHUNKS = 2


def _gdn_prep_call(qkv, st8, conv_w, ab, alog, dtb, L):
    b, s, _ = qkv.shape
    nch = s // L
    nc = GDN_PREP_CHUNKS if nch % GDN_PREP_CHUNKS == 0 else 1
    rows = nc * L
    return pl.pallas_call(
        functools.partial(_gdn_prep_kernel, L=L, nc=nc),
        grid=(b, nch // nc),
        in_specs=[_tok(GDN_QKV, rows),
                  pl.BlockSpec((1, 8, GDN_QKV), lambda bb, j: (bb, jnp.maximum(j * (rows // 8) - 1, 0), 0)),
                  pl.BlockSpec((1, 8, GDN_QKV), lambda bb, j: (bb, 0, 0)),
                  _full(conv_w), _tok(LANES, rows), _full(alog), _full(dtb)],
        out_specs=[pl.BlockSpec((1, nc, GDN_HEADS, 2 * L, LANES), lambda bb, j: (bb, j, 0, 0, 0)),
                   pl.BlockSpec((1, GDN_HEADS, rows, LANES), lambda bb, j: (bb, 0, j, 0)),
                   pl.BlockSpec((1, nc, GDN_HEADS, GDN_DK, L), lambda bb, j: (bb, j, 0, 0, 0)),
                   pl.BlockSpec((1, nc, GDN_HEADS, L, L), lambda bb, j: (bb, j, 0, 0, 0)),
                   pl.BlockSpec((1, nc, 1, LANES), lambda bb, j: (bb, j, 0, 0))],
        out_shape=[jax.ShapeDtypeStruct((b, nch, GDN_HEADS, 2 * L, LANES), BF16),
                   jax.ShapeDtypeStruct((b, GDN_HEADS, s, LANES), BF16),
                   jax.ShapeDtypeStruct((b, nch, GDN_HEADS, GDN_DK, L), BF16),
                   jax.ShapeDtypeStruct((b, nch, GDN_HEADS, L, L), BF16),
                   jax.ShapeDtypeStruct((b, nch, 1, LANES), F32)],
        compiler_params=_cparams("parallel", "parallel"),
        name="gdn_prep",
    )(qkv, qkv, st8, conv_w, ab, alog, dtb)


GDN_BATCH_GROUP = 2


def _gdn_seq_kernel(wq_ref, u_ref, kgt_ref, pm_ref, ee_ref, s0_ref, o_ref, sf_ref, s_scr, *, L):
    c = pl.program_id(1)

    @pl.when(c == 0)
    def _():
        s_scr[...] = s0_ref[...]

    ch = [(bb, hh) for bb in range(GDN_BATCH_GROUP) for hh in range(GDN_HEADS)]
    st = [s_scr[bb, hh] for bb, hh in ch]
    sp = [_split(s) for s in st]
    r = [_dot(wq_ref[bb, 0, hh], sp[i][0]) + _dot(wq_ref[bb, 0, hh], sp[i][1]) for i, (bb, hh) in enumerate(ch)]
    v_new = [(u_ref[bb, hh].astype(F32) - r[i][:L]).astype(BF16) for i, (bb, hh) in enumerate(ch)]
    o = [r[i][L:] + _dot(pm_ref[bb, 0, hh], v_new[i]) for i, (bb, hh) in enumerate(ch)]
    for i, (bb, hh) in enumerate(ch):
        o_ref[bb, :, GDN_DV * hh:GDN_DV * (hh + 1)] = o[i].astype(o_ref.dtype)
    s_new = [ee_ref[bb, 0][:, hh:hh + 1] * st[i] + _dot(kgt_ref[bb, 0, hh], v_new[i]) for i, (bb, hh) in enumerate(ch)]
    for i, (bb, hh) in enumerate(ch):
        s_scr[bb, hh] = s_new[i]

    @pl.when(c == pl.num_programs(1) - 1)
    def _():
        sf_ref[...] = s_scr[...]


def _gdn_seq_call(wq, um, kgt, pm, ee, s0, L):
    b, h, s, _ = um.shape
    nch = s // L
    bg = GDN_BATCH_GROUP
    sspec = pl.BlockSpec((bg, h, GDN_DK, GDN_DV), lambda g, c: (g, 0, 0, 0))
    return pl.pallas_call(
        functools.partial(_gdn_seq_kernel, L=L),
        grid=(b // bg, nch),
        in_specs=[pl.BlockSpec((bg, 1, h, 2 * L, LANES), lambda g, c: (g, c, 0, 0, 0)),
                  pl.BlockSpec((bg, h, L, LANES), lambda g, c: (g, 0, c, 0)),
                  pl.BlockSpec((bg, 1, h, GDN_DK, L), lambda g, c: (g, c, 0, 0, 0)),
                  pl.BlockSpec((bg, 1, h, L, L), lambda g, c: (g, c, 0, 0, 0)),
                  pl.BlockSpec((bg, 1, 1, LANES), lambda g, c: (g, c, 0, 0)),
                  sspec],
        out_specs=[pl.BlockSpec((bg, L, GDN_VW), lambda g, c: (g, c, 0)), sspec],
        out_shape=[jax.ShapeDtypeStruct((b, s, GDN_VW), BF16),
                   jax.ShapeDtypeStruct((b, h, GDN_DK, GDN_DV), F32)],
        scratch_shapes=[pltpu.VMEM((bg, h, GDN_DK, GDN_DV), F32)],
        compiler_params=_cparams("parallel", "arbitrary"),
        name="gdn_seq",
    )(wq, um, kgt, pm, ee, s0)


def _c_out_kernel(x_ref, o_ref, z_ref, gn_ref, w_ref, g1_ref, out_ref):
    o = o_ref[0].astype(F32)
    z = z_ref[0].astype(F32)
    gn = gn_ref[...]
    parts = []
    for hh in range(GDN_HEADS):
        oh = o[:, GDN_DV * hh:GDN_DV * (hh + 1)]
        zh = z[:, GDN_DV * hh:GDN_DV * (hh + 1)]
        parts.append((_rms(oh, gn) * (zh * jax.nn.sigmoid(zh))).astype(BF16))
    out = _dot(jnp.concatenate(parts, axis=1), w_ref[...])
    out_ref[0] = x_ref[0] + g1_ref[0] * out


def _c_out_call(x, o, z, gn, w, g1, ts):
    b, s, d = x.shape
    return pl.pallas_call(
        _c_out_kernel,
        grid=(b, s // ts),
        in_specs=[_tok(d, ts), _tok(GDN_VW, ts), _tok(GDN_VW, ts), _full(gn), _full(w), _modspec(g1, ts)],
        out_specs=_tok(d, ts),
        out_shape=jax.ShapeDtypeStruct((b, s, d), F32),
        compiler_params=_cparams("parallel", "parallel"),
        name="c_out",
    )(x, o, z, gn, w, g1)


def _pack_halves(v):
    n = v.shape[1] // 2
    bits = pltpu.bitcast(v.astype(BF16).astype(F32), jnp.uint32)
    return (bits[:, :n] >> 16) | (bits[:, n:] & jnp.uint32(0xFFFF0000))


def _unpack_halves(w):
    return pltpu.bitcast(w << 16, F32), pltpu.bitcast(w & jnp.uint32(0xFFFF0000), F32)


def _router_kernel(x_ref, gain_ref, sc_ref, sh_ref, wr_ref, br_ref, h_ref, re_ref, rg_ref, cnt_ref, carry):
    @pl.when(jnp.logical_and(pl.program_id(0) == 0, pl.program_id(1) == 0))
    def _():
        carry[...] = jnp.zeros_like(carry)

    h = _normmod(x_ref[0], gain_ref[...], sc_ref[0], sh_ref[0])
    h_ref[0] = _pack_halves(h)
    logits = _dot3(_split(h), (wr_ref[0], wr_ref[1])) + br_ref[...]
    lane = lax.broadcasted_iota(jnp.int32, logits.shape, 1).astype(F32)
    neg = jnp.float32(-jnp.inf)
    big = jnp.float32(1e9)
    gl = jnp.where(lane < N_GROUPS, logits, neg)
    gm = jnp.max(gl, axis=-1, keepdims=True)
    g_val = 1.0 / jnp.sum(jnp.exp(gl - gm), axis=-1, keepdims=True)
    g_idx = jnp.min(jnp.where(gl == gm, lane, big), axis=-1, keepdims=True)
    lo = N_GROUPS + EXPERTS_PER_GROUP * g_idx
    el = jnp.where(jnp.logical_and(lane >= lo, lane < lo + EXPERTS_PER_GROUP), logits, neg)
    em = jnp.max(el, axis=-1, keepdims=True)
    i1 = jnp.min(jnp.where(el == em, lane, big), axis=-1, keepdims=True)
    el2 = jnp.where(lane == i1, neg, el)
    em2 = jnp.max(el2, axis=-1, keepdims=True)
    i2 = jnp.min(jnp.where(el2 == em2, lane, big), axis=-1, keepdims=True)
    es = jnp.sum(jnp.exp(el - em), axis=-1, keepdims=True)
    p1 = 1.0 / es
    p2 = jnp.exp(em2 - em) / es
    den = p1 + p2
    rg_ref[0] = jnp.where(lane == 0, g_val * p1 / den, jnp.where(lane == 1, g_val * p2 / den, 0.0))
    oh1 = lane == i1
    oh2 = lane == i2
    both = jnp.where(jnp.logical_or(oh1, oh2), 1.0, 0.0)
    ts = both.shape[0]
    tri = (lax.broadcasted_iota(jnp.int32, (ts, ts), 0) > lax.broadcasted_iota(jnp.int32, (ts, ts), 1))
    pre = _dot(jnp.where(tri, 1.0, 0.0).astype(BF16), both.astype(BF16)) + carry[...]
    r1 = jnp.sum(jnp.where(oh1, pre, 0.0), axis=-1, keepdims=True)
    r2 = jnp.sum(jnp.where(oh2, pre, 0.0), axis=-1, keepdims=True)
    re_ref[0] = jnp.where(lane == 0, i1 - N_GROUPS, jnp.where(lane == 1, i2 - N_GROUPS, jnp.where(
        lane == 2, r1, jnp.where(lane == 3, r2, 0.0)))).astype(jnp.int32)
    carry[...] += jnp.sum(both, axis=0, keepdims=True)
    cnt_ref[...] = carry[...]


def _router_call(x, gain, sc, sh, wr, br, ts):
    b, s, d = x.shape
    return pl.pallas_call(
        _router_kernel,
        grid=(b, s // ts),
        in_specs=[_tok(d, ts), _full(gain), _modspec(sc, ts), _modspec(sh, ts), _full(wr), _full(br)],
        out_specs=[_tok(d // 2, ts), _tok(LANES, ts), _tok(LANES, ts), pl.BlockSpec((1, LANES), lambda bb, j: (0, 0))],
        out_shape=[jax.ShapeDtypeStruct((b, s, d // 2), jnp.uint32),
                   jax.ShapeDtypeStruct((b, s, LANES), jnp.int32),
                   jax.ShapeDtypeStruct((b, s, LANES), F32),
                   jax.ShapeDtypeStruct((1, LANES), F32)],
        scratch_shapes=[pltpu.VMEM((1, LANES), F32)],
        compiler_params=_cparams("arbitrary", "arbitrary"),
        name="moe_router",
    )(x, gain, sc, sh, wr, br)


def _row_copy(src, dst, src_row, dst_row, sem):
    return pltpu.make_async_copy(src.at[pl.ds(src_row, 1)], dst.at[pl.ds(dst_row, 1)], sem)


ROUTE_W = 2 * TOP_K
DMA_UNROLL = 8


def _slot(ps_ref, rt_ref, r, k):
    return ps_ref[rt_ref[0, 0, ROUTE_W * r + k]] + rt_ref[0, 0, ROUTE_W * r + TOP_K + k]


def _dispatch_kernel(ps_ref, rt_ref, h_ref, xin_hbm, xb_hbm, sem, *, tb):
    del xin_hbm

    def issue(r, c):
        for k in range(TOP_K):
            _row_copy(h_ref, xb_hbm, r, _slot(ps_ref, rt_ref, r, k), sem).start()
        return c
    lax.fori_loop(0, tb, issue, 0, unroll=DMA_UNROLL)
    for k in range(TOP_K):
        pltpu.make_async_copy(h_ref, xb_hbm.at[pl.ds(0, tb)], sem).wait()


def _dispatch_call(pad_start, route, h2, xb0, tb):
    n, dh = h2.shape
    nb = n // tb
    grid_spec = pltpu.PrefetchScalarGridSpec(
        num_scalar_prefetch=1,
        grid=(nb,),
        in_specs=[pl.BlockSpec((1, 1, ROUTE_W * tb), lambda j, ps: (j, 0, 0), memory_space=pltpu.SMEM),
                  pl.BlockSpec((tb, dh), lambda j, ps: (j, 0)),
                  pl.BlockSpec(memory_space=pl.ANY)],
        out_specs=pl.BlockSpec(memory_space=pl.ANY),
        scratch_shapes=[pltpu.SemaphoreType.DMA(())],
    )
    return pl.pallas_call(
        functools.partial(_dispatch_kernel, tb=tb),
        grid_spec=grid_spec,
        out_shape=jax.ShapeDtypeStruct(xb0.shape, xb0.dtype),
        input_output_aliases={3: 0},
        compiler_params=_cparams("arbitrary"),
        name="moe_dispatch",
    )(pad_start, route, h2, xb0)


def _ffn_kernel(be_ref, nu_ref, x_ref, w1_ref, w3_ref, w2_ref, y_ref, w1b, w3b, w2b):
    i = pl.program_id(0)
    half = w1b.shape[0] // 2

    @pl.when(jnp.logical_and(i < nu_ref[0], jnp.logical_or(i == 0, be_ref[i] != be_ref[jnp.maximum(i - 1, 0)])))
    def _():
        w1b[...] = w1_ref[0, 0].astype(BF16)
        w3b[...] = w3_ref[0, 0].astype(BF16)
        w2b[...] = w2_ref[0, 0].astype(BF16)

    @pl.when(i < nu_ref[0])
    def _():
        lo, hi = _unpack_halves(x_ref[...])
        lo = lo.astype(BF16)
        hi = hi.astype(BF16)
        a = _dot(lo, w1b[:half]) + _dot(hi, w1b[half:])
        g = _dot(lo, w3b[:half]) + _dot(hi, w3b[half:])
        mid = (a * jax.nn.sigmoid(a) * g).astype(BF16)
        y_ref[...] = _pack_halves(_dot(mid, w2b[...]))

    @pl.when(i >= nu_ref[0])
    def _():
        y_ref[...] = jnp.zeros_like(y_ref)


def _ffn_call(block_e, n_used, xb, w1, w3, w2, layer):
    n_blocks = block_e.shape[0]
    dh = xb.shape[1]
    d = 2 * dh
    grid_spec = pltpu.PrefetchScalarGridSpec(
        num_scalar_prefetch=2,
        grid=(n_blocks,),
        in_specs=[pl.BlockSpec((MOE_BLOCK, dh), lambda i, be, nu: (i, 0)),
                  pl.BlockSpec((1, 1, d, D_EXPERT), lambda i, be, nu: (layer, be[i], 0, 0)),
                  pl.BlockSpec((1, 1, d, D_EXPERT), lambda i, be, nu: (layer, be[i], 0, 0)),
                  pl.BlockSpec((1, 1, D_EXPERT, d), lambda i, be, nu: (layer, be[i], 0, 0))],
        out_specs=pl.BlockSpec((MOE_BLOCK, dh), lambda i, be, nu: (i, 0)),
        scratch_shapes=[pltpu.VMEM((d, D_EXPERT), BF16), pltpu.VMEM((d, D_EXPERT), BF16),
                        pltpu.VMEM((D_EXPERT, d), BF16)],
    )
    return pl.pallas_call(
        _ffn_kernel,
        grid_spec=grid_spec,
        out_shape=jax.ShapeDtypeStruct((n_blocks * MOE_BLOCK, dh), jnp.uint32),
        compiler_params=_cparams("arbitrary"),
        name="moe_ffn",
    )(block_e, n_used, xb, w1, w3, w2)


def _combine_kernel(ps_ref, rcur_ref, rnxt_ref, yb_hbm, x_ref, rg_ref, g2_ref, fin_ref, o_ref, b0, b1, sem,
                    *, tb, final):
    j = pl.program_id(0)
    nb = pl.num_programs(0)
    slot = j % 2

    def fetch(rref, sl):
        def issue(r, c):
            _row_copy(yb_hbm, b0.at[sl], _slot(ps_ref, rref, r, 0), r, sem.at[sl]).start()
            _row_copy(yb_hbm, b1.at[sl], _slot(ps_ref, rref, r, 1), r, sem.at[sl]).start()
            return c
        lax.fori_loop(0, tb, issue, 0, unroll=DMA_UNROLL)

    @pl.when(j == 0)
    def _():
        fetch(rcur_ref, 0)

    @pl.when(j + 1 < nb)
    def _():
        fetch(rnxt_ref, 1 - slot)

    pltpu.make_async_copy(yb_hbm.at[pl.ds(0, tb)], b0.at[slot], sem.at[slot]).wait()
    pltpu.make_async_copy(yb_hbm.at[pl.ds(0, tb)], b1.at[slot], sem.at[slot]).wait()
    rg = rg_ref[...]
    lo0, hi0 = _unpack_halves(b0[slot])
    lo1, hi1 = _unpack_halves(b1[slot])
    g0 = rg[:, 0:1]
    g1 = rg[:, 1:2]
    y = jnp.concatenate([g0 * lo0 + g1 * lo1, g0 * hi0 + g1 * hi1], axis=1)
    xn = x_ref[...] + g2_ref[0] * y
    if final:
        xn = _rms(xn, fin_ref[...])
    o_ref[...] = xn


def _combine_call(pad_start, route, yb, x, rg, g2, fin, tb, final):
    b, s, d = x.shape
    n = b * s
    nb = n // tb
    npb = s // tb
    dh = yb.shape[1]
    if g2.shape[1] == 1:
        gspec = pl.BlockSpec((1, 1, d), lambda j, ps: (j // npb, 0, 0))
    else:
        gspec = pl.BlockSpec((1, tb, d), lambda j, ps: (j // npb, j % npb, 0))
    grid_spec = pltpu.PrefetchScalarGridSpec(
        num_scalar_prefetch=1,
        grid=(nb,),
        in_specs=[pl.BlockSpec((1, 1, ROUTE_W * tb), lambda j, ps: (j, 0, 0), memory_space=pltpu.SMEM),
                  pl.BlockSpec((1, 1, ROUTE_W * tb), lambda j, ps: (jnp.minimum(j + 1, nb - 1), 0, 0),
                               memory_space=pltpu.SMEM),
                  pl.BlockSpec(memory_space=pl.ANY),
                  pl.BlockSpec((tb, d), lambda j, ps: (j, 0)), pl.BlockSpec((tb, LANES), lambda j, ps: (j, 0)),
                  gspec, pl.BlockSpec(fin.shape, lambda j, ps: (0, 0))],
        out_specs=pl.BlockSpec((tb, d), lambda j, ps: (j, 0)),
        scratch_shapes=[pltpu.VMEM((2, tb, dh), jnp.uint32), pltpu.VMEM((2, tb, dh), jnp.uint32),
                        pltpu.SemaphoreType.DMA((2,))],
    )
    out = pl.pallas_call(
        functools.partial(_combine_kernel, tb=tb, final=final),
        grid_spec=grid_spec,
        out_shape=jax.ShapeDtypeStruct((n, d), F32),
        compiler_params=_cparams("arbitrary"),
        name="moe_combine",
    )(pad_start, route, route, yb, x.reshape(n, d), rg.reshape(n, LANES), g2, fin)
    return out.reshape(b, s, d)


def _moe(x, gain, sc, sh, g2, mp, fin, ts, final):
    b, s, d = x.shape
    n = b * s
    h2, r_e, r_g, cnt = _router_call(x, gain, sc, sh, mp["wr"], mp["br"], ts)
    counts = cnt[0, N_GROUPS:N_GROUPS + N_EXPERTS].astype(jnp.int32)
    padded = (counts + MOE_BLOCK - 1) // MOE_BLOCK * MOE_BLOCK
    pad_end = jnp.cumsum(padded)
    pad_start = (pad_end - padded).astype(jnp.int32)
    n_blocks = (n * TOP_K + N_EXPERTS * (MOE_BLOCK - 1) + MOE_BLOCK - 1) // MOE_BLOCK
    blk0 = jnp.arange(n_blocks, dtype=jnp.int32) * MOE_BLOCK
    block_e = jnp.minimum(jnp.sum((pad_end[None, :] <= blk0[:, None]).astype(jnp.int32), axis=1), N_EXPERTS - 1)
    n_used = (pad_end[-1:] // MOE_BLOCK).astype(jnp.int32)
    route = r_e[:, :, :ROUTE_W].reshape(n // ts, 1, ROUTE_W * ts)
    xb0 = jnp.zeros((n_blocks * MOE_BLOCK, d // 2), jnp.uint32)
    xb = _dispatch_call(pad_start, route, h2.reshape(n, d // 2), xb0, ts)
    yb = _ffn_call(block_e, n_used, xb, mp["w1"], mp["w3"], mp["w2"], mp["layer"])
    return _combine_call(pad_start, route, yb, x, r_g, g2, fin, ts, final)


def _prep_ab(l, p):
    w_in = p["w_in_ab"][l]
    o2 = S5_WIDTH + MLA_Q_RANK + MLA_KV_RANK
    half = MLA_ROPE // 2
    d = w_in.shape[0]
    kr = w_in[:, o2:]
    kr_rot = jnp.concatenate([-kr[:, half:], kr[:, :half]], axis=1)
    z64 = jnp.zeros((d, MLA_NOPE), F32)
    z32 = jnp.zeros((d, LANES - MLA_NOPE - MLA_ROPE), F32)
    w_ext = jnp.concatenate([w_in[:, :o2], z64, kr, z32, z64, kr_rot, z32], axis=1)
    qu = p["mla_q_up"][l].reshape(MLA_Q_RANK, MLA_HEADS, MLA_NOPE + MLA_ROPE)
    qn, qr = qu[:, :, :MLA_NOPE], qu[:, :, MLA_NOPE:]
    zq = jnp.zeros((MLA_Q_RANK, MLA_HEADS, LANES - MLA_NOPE - MLA_ROPE), F32)
    wqa = jnp.concatenate([qn, qr, zq], axis=2).reshape(MLA_Q_RANK, MLA_HEADS * LANES)
    wqb = jnp.concatenate([jnp.zeros_like(qn), -qr[:, :, half:], qr[:, :, :half], zq], axis=2)
    wqb = wqb.reshape(MLA_Q_RANK, MLA_HEADS * LANES)
    kvu = p["mla_kv_up"][l].reshape(MLA_KV_RANK, MLA_HEADS, MLA_NOPE + MLA_V)
    wk = jnp.concatenate([kvu[:, :, :MLA_NOPE], jnp.zeros((MLA_KV_RANK, MLA_HEADS, LANES - MLA_NOPE), F32)], axis=2)
    wo = p["w_out_ab"][l]
    out = dict(
        w_in=w_ext.astype(BF16), q_norm=p["mla_q_norm"][l].reshape(1, -1), kv_norm=p["mla_kv_norm"][l].reshape(1, -1),
        wqa=wqa.astype(BF16), wqb=wqb.astype(BF16),
        wk=wk.reshape(MLA_KV_RANK, MLA_HEADS * LANES).astype(BF16),
        wv=jnp.concatenate([kvu[:, :, MLA_NOPE:], jnp.zeros((MLA_KV_RANK, MLA_HEADS, LANES - MLA_V), F32)],
                           axis=2).reshape(MLA_KV_RANK, MLA_HEADS * LANES).astype(BF16),
        s5_d=p["s5_d"][l].reshape(1, -1), glu_w=p["s5_glu_w"][l].astype(BF16), glu_b=p["s5_glu_b"][l].reshape(1, -1),
        w_out_s5=wo[:S5_WIDTH].astype(BF16),
        w_out_at=wo[S5_WIDTH:].astype(BF16))
    out.update(_s5_mats(p["s5_a_re"][l], p["s5_a_im"][l], p["s5_b_re"][l], p["s5_b_im"][l],
                        p["s5_c_re"][l], p["s5_c_im"][l], p["s5_log_dt"][l]))
    return out


def _prep_c(l, p):
    w = p["w_in_c"][l]
    pad = jnp.zeros((w.shape[0], LANES - 2 * GDN_HEADS), F32)

    def row(v):
        return jnp.concatenate([v, jnp.zeros((LANES - GDN_HEADS,), F32)]).reshape(1, LANES)

    return dict(w_in=jnp.concatenate([w, pad], axis=1).astype(BF16), conv_w=p["conv_w"][l],
                alog=row(p["gdn_a_log"][l]), dtb=row(p["gdn_dt_bias"][l]),
                gn=p["gdn_norm"][l].reshape(1, -1), w_out=p["w_out_c"][l].astype(BF16))


def _prep_moe(layer, p):
    d = p["moe_w_group"].shape[1]
    wr = jnp.concatenate([p["moe_w_group"][layer], p["moe_w_expert"][layer],
                          jnp.zeros((d, LANES - N_GROUPS - N_EXPERTS), F32)], axis=1)
    br = jnp.concatenate([p["moe_b_group"][layer], p["moe_b_expert"][layer],
                          jnp.zeros((LANES - N_GROUPS - N_EXPERTS,), F32)]).reshape(1, LANES)
    wr_hi = wr.astype(BF16)
    wr_lo = (wr - wr_hi.astype(F32)).astype(BF16)
    return dict(wr=jnp.stack([wr_hi, wr_lo]), br=br, w1=p["moe_w1"], w3=p["moe_w3"], w2=p["moe_w2"], layer=layer)


def _rope_tables(pos):
    half = MLA_ROPE // 2
    inv = ROPE_THETA ** (-jnp.arange(half, dtype=F32) / half)
    ang = pos.astype(F32)[:, None] * inv[None, :]
    cos, sin = jnp.cos(ang), jnp.sin(ang)
    n = pos.shape[0]
    z64 = jnp.zeros((n, MLA_NOPE), F32)
    z32 = jnp.zeros((n, LANES - MLA_NOPE - MLA_ROPE), F32)
    scale = (MLA_NOPE + MLA_ROPE) ** -0.5 * math.log2(math.e)
    ck = jnp.concatenate([z64, cos, cos, z32], axis=1)
    sk = jnp.concatenate([z64, sin, sin, z32], axis=1)
    cq = jnp.concatenate([jnp.ones_like(z64), cos, cos, z32], axis=1) * scale
    return cq, sk * scale, ck, sk


def _trunk(x, mods, caches, wab, wc, wmoe, p, flat):
    b, s, d = x.shape
    lat_c, rope_c, s5re_c, s5im_c, conv_c, gdn_c = caches
    past = 0 if lat_c is None else lat_c.shape[2]
    pos = past + jnp.arange(s)
    if flat:
        xt = x.reshape(1, b * s, d)
        ts = b * s
        pos_rows = jnp.tile(pos, b)

        def mrow(m):
            return jnp.repeat(m, s, axis=0).reshape(1, b * s, d)
    else:
        xt = x
        ts = 256
        pos_rows = pos

        def mrow(m):
            return m.reshape(b, 1, d)
    bt, st = xt.shape[:2]
    outs = {}
    depth = p["norm_mix"].shape[0]
    for layer in range(depth):
        l = layer // 2
        sh1, sc1, g1, sh2, sc2, g2 = [mrow(m) for m in jnp.split(mods[layer], 6, axis=-1)]
        gain = p["norm_mix"][layer].reshape(1, d)
        if layer % 2 == 0:
            wp = wab[l]
            u, q, lat, kr = _ab_in_call(xt, gain, sc1, sh1, wp, _rope_tables(pos_rows), ts)
            lat_b = lat.reshape(b, s, MLA_KV_RANK)
            kr_b = kr.reshape(b, s, LANES)
            outs["lat"] = lat_b
            outs["krope"] = kr_b[:, :, MLA_NOPE:MLA_NOPE + MLA_ROPE]
            if lat_c is None:
                lat_all, kr_all, sk_valid, tkv, tq, tk = lat_b, kr_b, s, 256, min(512, s), min(512, s)
            else:
                krc = jnp.pad(rope_c[l], ((0, 0), (0, 0), (MLA_NOPE, LANES - MLA_NOPE - MLA_ROPE)))
                sk_valid = past + s
                skp = (sk_valid + LANES - 1) // LANES * LANES
                lat_all = jnp.pad(jnp.concatenate([lat_c[l], lat_b], axis=1), ((0, 0), (0, skp - sk_valid), (0, 0)))
                kr_all = jnp.pad(jnp.concatenate([krc, kr_b], axis=1), ((0, 0), (0, skp - sk_valid), (0, 0)))
                tkv, tq, tk = skp, s, skp
            kk, vv = _kv_call(lat_all, kr_all, wp, tkv)
            qh = q.reshape(MLA_HEADS, b, s, LANES).transpose(1, 0, 2, 3) if flat else q
            attn = _attn_call(qh, kk, vv, tq, tk, past, sk_valid).reshape(bt, st, MLA_HEADS * MLA_V)
            nchk = s // S5_CHUNK
            r = b * nchk
            u2 = u.reshape(b * s, S5_WIDTH)
            tr = min(r, 256)
            wre, wim = _s5_w_call(u2, wp["bre"], wp["bim"], tr)
            ngp = S5_GROUPS * S5_STATE

            def to_pairs(v):
                return v.reshape(b, 1, ngp)

            if s5re_c is None:
                x0re = jnp.zeros((b, 1, ngp), F32)
                x0im = jnp.zeros((b, 1, ngp), F32)
            else:
                x0re, x0im = to_pairs(s5re_c[l]), to_pairs(s5im_c[l])
            nchp = (nchk + 7) // 8 * 8
            tc = min(nchp, 128)
            wre3 = jnp.pad(wre.reshape(b, nchk, ngp), ((0, 0), (0, nchp - nchk), (0, 0)))
            wim3 = jnp.pad(wim.reshape(b, nchk, ngp), ((0, 0), (0, nchp - nchk), (0, 0)))
            xere, xeim = _s5_scan_call(wre3, wim3, wp["lre"], wp["lim"], x0re, x0im, tc)
            outs["s5re"] = xere[:, nchk - 1].reshape(b, S5_GROUPS, S5_STATE)
            outs["s5im"] = xeim[:, nchk - 1].reshape(b, S5_GROUPS, S5_STATE)
            xsre = jnp.concatenate([x0re, xere[:, :nchk - 1]], axis=1).reshape(r, ngp)
            xsim = jnp.concatenate([x0im, xeim[:, :nchk - 1]], axis=1).reshape(r, ngp)
            ys = _s5_y_call(u2, xsre, xsim, wp["mst"], wp["cc"], tr).reshape(bt, st, S5_WIDTH)
            xt = _ab_out_call(xt, ys, u, attn, wp, g1, ts)
        else:
            wp = wc[l]
            qkv, z, ab = _c_in_call(xt, gain, sc1, sh1, wp["w_in"], ts)
            qkv_b = qkv.reshape(b, s, GDN_QKV)
            outs["conv"] = qkv_b[:, s - (CONV_W - 1):].astype(F32)
            if conv_c is None:
                st8 = jnp.zeros((b, 8, GDN_QKV), BF16)
                s0 = jnp.zeros((b, GDN_HEADS, GDN_DK, GDN_DV), F32)
            else:
                st8 = jnp.pad(conv_c[l], ((0, 0), (8 - (CONV_W - 1), 0), (0, 0))).astype(BF16)
                s0 = gdn_c[l]
            lg = min(s, CHUNK)
            wq, um, kgt, pm, ee = _gdn_prep_call(qkv_b, st8, wp["conv_w"], ab.reshape(b, s, LANES),
                                                 wp["alog"], wp["dtb"], lg)
            o, sfin = _gdn_seq_call(wq, um, kgt, pm, ee, s0, lg)
            outs["gdn"] = sfin
            xt = _c_out_call(xt, o.reshape(bt, st, GDN_VW), z, wp["gn"], wp["w_out"], g1, ts)
        final = layer == depth - 1
        xt = _moe(xt, p["norm_ffn"][layer].reshape(1, d), sc2, sh2, g2, wmoe[layer],
                  p["norm_final"].reshape(1, d), ts, final)
    y = xt.reshape(b, s, d)
    return (y, outs["lat"][None], outs["krope"][None], outs["s5re"][None], outs["s5im"][None],
            outs["conv"][None], outs["gdn"][None])


def kernel(x_prompt, x_sample, c_prompt, c_sample, cache_mla_latent, cache_mla_krope, state_s5_re, state_s5_im,
           state_conv, state_gdn, w_mod, b_mod, norm_mix, norm_ffn, norm_final, w_in_ab, s5_a_re, s5_a_im,
           s5_b_re, s5_b_im, s5_c_re, s5_c_im, s5_d, s5_log_dt, s5_glu_w, s5_glu_b, mla_q_norm, mla_q_up,
           mla_kv_norm, mla_kv_up, w_out_ab, w_in_c, conv_w, gdn_a_log, gdn_dt_bias, gdn_norm, w_out_c,
           moe_w_group, moe_b_group, moe_w_expert, moe_b_expert, moe_w1, moe_w3, moe_w2):
    p = dict(w_mod=w_mod, b_mod=b_mod, norm_mix=norm_mix, norm_ffn=norm_ffn, norm_final=norm_final,
             w_in_ab=w_in_ab, s5_a_re=s5_a_re, s5_a_im=s5_a_im, s5_b_re=s5_b_re, s5_b_im=s5_b_im,
             s5_c_re=s5_c_re, s5_c_im=s5_c_im, s5_d=s5_d, s5_log_dt=s5_log_dt, s5_glu_w=s5_glu_w,
             s5_glu_b=s5_glu_b, mla_q_norm=mla_q_norm, mla_q_up=mla_q_up, mla_kv_norm=mla_kv_norm,
             mla_kv_up=mla_kv_up, w_out_ab=w_out_ab, w_in_c=w_in_c, conv_w=conv_w, gdn_a_log=gdn_a_log,
             gdn_dt_bias=gdn_dt_bias, gdn_norm=gdn_norm, w_out_c=w_out_c, moe_w_group=moe_w_group,
             moe_b_group=moe_b_group, moe_w_expert=moe_w_expert, moe_b_expert=moe_b_expert,
             moe_w1=moe_w1, moe_w3=moe_w3, moe_w2=moe_w2)
    depth = norm_mix.shape[0]
    bp, bs = c_prompt.shape[0], c_sample.shape[0]
    nb = (bp + bs + 7) // 8 * 8
    c_all = jnp.pad(jnp.concatenate([c_prompt, c_sample], axis=0), ((0, nb - bp - bs), (0, 0)))
    mods = _mod_call(c_all, w_mod, b_mod)
    wab = [_prep_ab(l, p) for l in range((depth + 1) // 2)]
    wc = [_prep_c(l, p) for l in range(depth // 2)]
    wmoe = [_prep_moe(layer, p) for layer in range(depth)]
    none6 = (None,) * 6
    outp = _trunk(x_prompt, mods[:, :bp], none6, wab, wc, wmoe, p, flat=False)
    caches = (cache_mla_latent, cache_mla_krope, state_s5_re, state_s5_im, state_conv, state_gdn)
    outs = _trunk(x_sample, mods[:, bp:bp + bs], caches, wab, wc, wmoe, p, flat=True)
    return (outp[0], outs[0]) + tuple(outp[1:]) + tuple(outs[1:])
```

```python
import functools
import math

import jax
import jax.numpy as jnp
from jax import lax
from jax.experimental import pallas as pl
from jax.experimental.pallas import tpu as pltpu

F32 = jnp.float32
BF16 = jnp.bfloat16
HI = lax.Precision.HIGHEST
EPS = 1e-6

D_MODEL = 1024
CHUNK = 64
S5_WIDTH = 512
S5_GROUP = 16
S5_GROUPS = 32
S5_STATE = 64
S5_CHUNK = 16
S5_LB = 128 // S5_GROUP
MLA_HEADS = 8
MLA_NOPE = 64
MLA_ROPE = 32
MLA_V = 64
MLA_Q_RANK = 384
MLA_KV_RANK = 256
ROPE_THETA = 10000.0
GDN_HEADS = 8
GDN_DK = 128
GDN_DV = 128
CONV_W = 4
GDN_QKV = GDN_HEADS * (2 * GDN_DK + GDN_DV)
GDN_VW = GDN_HEADS * GDN_DV
N_GROUPS = 4
EXPERTS_PER_GROUP = 8
N_EXPERTS = 32
D_EXPERT = 512
TOP_K = 2
MOE_BLOCK = 512
ATTN_SUB = 512
ATTN_TQ = 512
ATTN_TK = 512
TOKEN_BLOCK = 256
S5_ROW_BLOCK = 256
S5_SCAN_BLOCK = 128
LANES = 128
VMEM_LIMIT = 48 * 1024 * 1024


def _cparams(*sem):
    return pltpu.CompilerParams(dimension_semantics=sem, vmem_limit_bytes=VMEM_LIMIT)


def _dot(a, b):
    return jnp.dot(a, b, preferred_element_type=F32)


def _dot_nt(a, b, precision=None):
    return lax.dot_general(a, b, (((1,), (1,)), ((), ())), preferred_element_type=F32, precision=precision)


def _full(arr):
    nd = arr.ndim
    return pl.BlockSpec(arr.shape, lambda *_: (0,) * nd)


def _tok(width, ts):
    return pl.BlockSpec((1, ts, width), lambda b, j: (b, j, 0))


def _modspec(arr, ts):
    if arr.shape[1] == 1:
        return pl.BlockSpec((1, 1, arr.shape[2]), lambda b, j: (b, 0, 0))
    return pl.BlockSpec((1, ts, arr.shape[2]), lambda b, j: (b, j, 0))


def _normmod(x, gain, sc, sh):
    ms = jnp.mean(x * x, axis=-1, keepdims=True)
    return x * lax.rsqrt(ms + EPS) * gain * (1.0 + sc) + sh


def _rms(x, gain):
    return x * lax.rsqrt(jnp.mean(x * x, axis=-1, keepdims=True) + EPS) * gain


def _mod_kernel(c_ref, w_ref, b_ref, o_ref):
    c = c_ref[...]
    a = (c * jax.nn.sigmoid(c)).astype(BF16)
    o_ref[0] = _dot(a, w_ref[0]) + b_ref[0]


def _mod_call(c_all, w_mod, b_mod):
    depth, d, n6 = w_mod.shape
    bp = c_all.shape[0]
    tn = 1536
    return pl.pallas_call(
        _mod_kernel,
        grid=(depth, n6 // tn),
        in_specs=[pl.BlockSpec((bp, d), lambda l, n: (0, 0)),
                  pl.BlockSpec((1, d, tn), lambda l, n: (l, 0, n)),
                  pl.BlockSpec((1, 1, tn), lambda l, n: (l, 0, n))],
        out_specs=pl.BlockSpec((1, bp, tn), lambda l, n: (l, 0, n)),
        out_shape=jax.ShapeDtypeStruct((depth, bp, n6), F32),
        compiler_params=_cparams("parallel", "parallel"),
        name="mod",
    )(c_all, w_mod.astype(BF16), b_mod.reshape(depth, 1, n6))


def _ab_in_kernel(x_ref, gain_ref, sc_ref, sh_ref, w_ref, qg_ref, wqa_ref, wqb_ref, kvg_ref,
                  cq_ref, sq_ref, ck_ref, sk_ref, u_ref, q_ref, lat_ref, kr_ref):
    h = _normmod(x_ref[0], gain_ref[...], sc_ref[0], sh_ref[0]).astype(BF16)
    proj = _dot(h, w_ref[...])
    u_ref[0] = proj[:, :S5_WIDTH]
    o1 = S5_WIDTH + MLA_Q_RANK
    o2 = o1 + MLA_KV_RANK
    qn = _rms(proj[:, S5_WIDTH:o1], qg_ref[...]).astype(BF16)
    qa = _dot(qn, wqa_ref[...])
    qb = _dot(qn, wqb_ref[...])
    cq = cq_ref[...]
    sq = sq_ref[...]
    for hh in range(MLA_HEADS):
        sl = slice(LANES * hh, LANES * (hh + 1))
        q_ref[0, hh] = (qa[:, sl] * cq + qb[:, sl] * sq).astype(BF16)
    lat_ref[0] = _rms(proj[:, o1:o2], kvg_ref[...])
    kr_ref[0] = proj[:, o2:o2 + LANES] * ck_ref[...] + proj[:, o2 + LANES:o2 + 2 * LANES] * sk_ref[...]


def _ab_in_call(x, gain, sc, sh, wp, tabs, ts):
    b, s, d = x.shape
    cq, sq, ck, sk = tabs
    tab = pl.BlockSpec((ts, LANES), lambda bb, j: (j, 0))
    ins = [x, gain, sc, sh, wp["w_in"], wp["q_norm"], wp["wqa"], wp["wqb"], wp["kv_norm"], cq, sq, ck, sk]
    specs = [_tok(d, ts), _full(gain), _modspec(sc, ts), _modspec(sh, ts), _full(wp["w_in"]), _full(wp["q_norm"]),
             _full(wp["wqa"]), _full(wp["wqb"]), _full(wp["kv_norm"]), tab, tab, tab, tab]
    return pl.pallas_call(
        _ab_in_kernel,
        grid=(b, s // ts),
        in_specs=specs,
        out_specs=[_tok(S5_WIDTH, ts),
                   pl.BlockSpec((1, MLA_HEADS, ts, LANES), lambda bb, j: (bb, 0, j, 0)),
                   _tok(MLA_KV_RANK, ts), _tok(LANES, ts)],
        out_shape=[jax.ShapeDtypeStruct((b, s, S5_WIDTH), F32),
                   jax.ShapeDtypeStruct((b, MLA_HEADS, s, LANES), BF16),
                   jax.ShapeDtypeStruct((b, s, MLA_KV_RANK), F32),
                   jax.ShapeDtypeStruct((b, s, LANES), F32)],
        compiler_params=_cparams("parallel", "parallel"),
        name="ab_in",
    )(*ins)


def _kv_kernel(lat_ref, kr_ref, wk_ref, wv_ref, k_ref, v_ref):
    lat = lat_ref[0].astype(BF16)
    kk = _dot(lat, wk_ref[...])
    vv = _dot(lat, wv_ref[...])
    kr = kr_ref[0]
    ones = (lax.broadcasted_iota(jnp.int32, (1, LANES), 1) >= MLA_V).astype(F32)
    for hh in range(MLA_HEADS):
        k_ref[0, hh] = (kk[:, LANES * hh:LANES * (hh + 1)] + kr).astype(BF16)
        v_ref[0, hh] = (vv[:, LANES * hh:LANES * (hh + 1)] + ones).astype(BF16)


def _kv_call(lat_all, kr_all, wp, ts):
    b, sk, _ = lat_all.shape
    hspec = pl.BlockSpec((1, MLA_HEADS, ts, LANES), lambda bb, j: (bb, 0, j, 0))
    hshape = jax.ShapeDtypeStruct((b, MLA_HEADS, sk, LANES), BF16)
    return pl.pallas_call(
        _kv_kernel,
        grid=(b, sk // ts),
        in_specs=[_tok(MLA_KV_RANK, ts), _tok(LANES, ts), _full(wp["wk"]), _full(wp["wv"])],
        out_specs=[hspec, hspec],
        out_shape=[hshape, hshape],
        compiler_params=_cparams("parallel", "parallel"),
        name="kv_up",
    )(lat_all, kr_all, wp["wk"], wp["wv"])


def _attn_kernel(q_ref, k_ref, v_ref, o_ref, *, tq, tk, q_off, sk_valid):
    i = pl.program_id(2)
    nsub = max(tq // ATTN_SUB, 1)
    tqs = tq // nsub
    chains = [(hh, qi) for hh in range(2) for qi in range(nsub)]
    qs = [q_ref[0, hh, qi * tqs:(qi + 1) * tqs, :] for hh, qi in chains]
    q0 = q_off + i * tq
    lim_full = (q0 // CHUNK + 1) * CHUNK
    lim_tot = jnp.minimum(((q0 + tq - 1) // CHUNK + 1) * CHUNK, sk_valid)
    n_full = jnp.minimum(lim_full // tk, sk_valid // tk)
    n_tot = (lim_tot + tk - 1) // tk

    def step(j, carry, masked):
        off = pl.multiple_of(j * tk, tk)
        nc = range(len(chains))
        if masked:
            kpos = off + lax.broadcasted_iota(jnp.int32, (tqs, tk), 1)
            ok = []
            for qi in range(nsub):
                qpos = q0 + qi * tqs + lax.broadcasted_iota(jnp.int32, (tqs, tk), 0)
                ok.append(jnp.logical_and(kpos // CHUNK <= qpos // CHUNK, kpos < sk_valid))
        kt = [k_ref[0, hh, pl.ds(off, tk), :] for hh in range(2)]
        vt = [v_ref[0, hh, pl.ds(off, tk), :] for hh in range(2)]
        s = [_dot_nt(qs[c], kt[chains[c][0]]) for c in nc]
        if masked:
            s = [jnp.where(ok[chains[c][1]], s[c], -1e30) for c in nc]
        m_new = [jnp.maximum(carry[c][0], jnp.max(s[c], axis=-1, keepdims=True)) for c in nc]
        alpha = [jnp.exp2(carry[c][0] - m_new[c]) for c in nc]
        p = [jnp.exp2(s[c] - m_new[c]).astype(BF16) for c in nc]
        acc = [alpha[c] * carry[c][1] + _dot(p[c], vt[chains[c][0]]) for c in nc]
        return tuple((m_new[c], acc[c]) for c in nc)

    one = (jnp.full((tqs, 1), -1e30, F32), jnp.zeros((tqs, LANES), F32))
    carry = lax.fori_loop(0, n_full, functools.partial(step, masked=False), (one,) * len(chains))
    carry = lax.fori_loop(n_full, n_tot, functools.partial(step, masked=True), carry)
    lane = lax.broadcasted_iota(jnp.int32, (tqs, LANES), 1)
    for qi in range(nsub):
        a0 = carry[chains.index((0, qi))][1]
        a1 = carry[chains.index((1, qi))][1]
        r0 = a0 / pltpu.roll(a0, MLA_V, axis=1)
        r1 = a1 / pltpu.roll(a1, MLA_V, axis=1)
        o_ref[0, qi * tqs:(qi + 1) * tqs, :] = jnp.where(
            lane < MLA_V, r0, pltpu.roll(r1, MLA_V, axis=1)).astype(o_ref.dtype)


def _attn_call(q, k, v, tq, tk, q_off, sk_valid):
    b, h, sq, _ = q.shape
    sk = k.shape[2]
    kern = functools.partial(_attn_kernel, tq=tq, tk=tk, q_off=q_off, sk_valid=sk_valid)
    kv_mode = pl.Buffered(2 if sq == tq else 1)
    return pl.pallas_call(
        kern,
        grid=(b, h // 2, sq // tq),
        in_specs=[pl.BlockSpec((1, 2, tq, LANES), lambda bb, hp, i: (bb, hp, i, 0)),
                  pl.BlockSpec((1, 2, sk, LANES), lambda bb, hp, i: (bb, hp, 0, 0), pipeline_mode=kv_mode),
                  pl.BlockSpec((1, 2, sk, LANES), lambda bb, hp, i: (bb, hp, 0, 0), pipeline_mode=kv_mode)],
        out_specs=pl.BlockSpec((1, tq, LANES), lambda bb, hp, i: (bb, i, hp)),
        out_shape=jax.ShapeDtypeStruct((b, sq, h * MLA_V), BF16),
        compiler_params=_cparams("parallel", "parallel", "arbitrary"),
        name="mla_attn",
    )(q, k, v)


def _chunk_rows(ref, s, tr):
    return ref[pl.ds(s, tr, stride=S5_CHUNK), :]


def _s5_w_kernel(u_ref, bre_ref, bim_ref, wre_ref, wim_ref, *, tr):
    ucat = jnp.concatenate([_chunk_rows(u_ref, s, tr).astype(BF16) for s in range(S5_CHUNK)], axis=1)
    wre_ref[...] = _dot(ucat, bre_ref[0])
    wim_ref[...] = _dot(ucat, bim_ref[0])


def _s5_w_call(u2, bre, bim, tr):
    n = u2.shape[0]
    r = n // S5_CHUNK
    nq = bre.shape[0]
    sw = S5_LB * S5_STATE
    out = jax.ShapeDtypeStruct((r, nq * sw), F32)
    ospec = pl.BlockSpec((tr, sw), lambda q, i: (i, q))
    wspec = pl.BlockSpec((1, S5_CHUNK * LANES, sw), lambda q, i: (q, 0, 0))
    return pl.pallas_call(
        functools.partial(_s5_w_kernel, tr=tr),
        grid=(nq, r // tr),
        in_specs=[pl.BlockSpec((tr * S5_CHUNK, LANES), lambda q, i: (i, q)), wspec, wspec],
        out_specs=[ospec, ospec],
        out_shape=[out, out],
        compiler_params=_cparams("parallel", "parallel"),
        name="s5_chunk_in",
    )(u2, bre, bim)


def _s5_scan_kernel(wre_ref, wim_ref, lre_ref, lim_ref, x0re_ref, x0im_ref, ore_ref, oim_ref, sre, sim, *, tc):
    @pl.when(pl.program_id(1) == 0)
    def _():
        sre[...] = x0re_ref[0]
        sim[...] = x0im_ref[0]

    lr = lre_ref[...]
    li = lim_ref[...]

    def blk(t, carry):
        xr, xi = carry
        base = pl.multiple_of(t * 8, 8)
        wr = wre_ref[0, pl.ds(base, 8), :]
        wi = wim_ref[0, pl.ds(base, 8), :]
        rows_r, rows_i = [], []
        for r in range(8):
            nr = lr * xr - li * xi + wr[r:r + 1]
            ni = lr * xi + li * xr + wi[r:r + 1]
            xr, xi = nr, ni
            rows_r.append(xr)
            rows_i.append(xi)
        ore_ref[0, pl.ds(base, 8), :] = jnp.concatenate(rows_r, axis=0)
        oim_ref[0, pl.ds(base, 8), :] = jnp.concatenate(rows_i, axis=0)
        return xr, xi

    xr, xi = lax.fori_loop(0, tc // 8, blk, (sre[...], sim[...]))
    sre[...] = xr
    sim[...] = xi


def _s5_scan_call(wre, wim, lre, lim, x0re, x0im, tc):
    b, nch, n = wre.shape
    wspec = pl.BlockSpec((1, tc, n), lambda bb, c: (bb, c, 0))
    lspec = pl.BlockSpec((1, n), lambda bb, c: (0, 0))
    xspec = pl.BlockSpec((1, 1, n), lambda bb, c: (bb, 0, 0))
    out = jax.ShapeDtypeStruct((b, nch, n), F32)
    return pl.pallas_call(
        functools.partial(_s5_scan_kernel, tc=tc),
        grid=(b, nch // tc),
        in_specs=[wspec, wspec, lspec, lspec, xspec, xspec],
        out_specs=[wspec, wspec],
        out_shape=[out, out],
        scratch_shapes=[pltpu.VMEM((1, n), F32), pltpu.VMEM((1, n), F32)],
        compiler_params=_cparams("parallel", "arbitrary"),
        name="s5_scan",
    )(wre, wim, lre, lim, x0re, x0im)


def _s5_y_kernel(u_ref, xre_ref, xim_ref, m_ref, c_ref, y_ref, *, tr):
    L = S5_CHUNK
    urev = jnp.concatenate([_chunk_rows(u_ref, s, tr).astype(BF16) for s in reversed(range(L))], axis=1)
    xcat = jnp.concatenate([xre_ref[...].astype(BF16), xim_ref[...].astype(BF16)], axis=1)
    for t in range(L):
        y = _dot(urev[:, (L - 1 - t) * LANES:], m_ref[0, :(t + 1) * LANES, :]) + _dot(xcat, c_ref[0, t])
        y_ref[pl.ds(t, tr, stride=L), :] = y


def _s5_y_call(u2, xre, xim, mst, cc, tr):
    n, w = u2.shape
    nq = mst.shape[0]
    sw = S5_LB * S5_STATE
    uspec = pl.BlockSpec((tr * S5_CHUNK, LANES), lambda q, i: (i, q))
    xspec = pl.BlockSpec((tr, sw), lambda q, i: (i, q))
    return pl.pallas_call(
        functools.partial(_s5_y_kernel, tr=tr),
        grid=(nq, n // (tr * S5_CHUNK)),
        in_specs=[uspec, xspec, xspec,
                  pl.BlockSpec((1, S5_CHUNK * LANES, LANES), lambda q, i: (q, 0, 0)),
                  pl.BlockSpec((1, S5_CHUNK, 2 * sw, LANES), lambda q, i: (q, 0, 0, 0))],
        out_specs=uspec,
        out_shape=jax.ShapeDtypeStruct((n, w), F32),
        compiler_params=_cparams("parallel", "parallel"),
        name="s5_chunk_out",
    )(u2, xre, xim, mst, cc)


def _s5_mats(a_re, a_im, b_re, b_im, c_re, c_im, log_dt):
    g, p = a_re.shape
    L = S5_CHUNK
    lam = lax.complex(jnp.minimum(a_re, -1e-4), a_im)
    lamdt = lam * jnp.exp(log_dt)[:, None]
    lam_bar = jnp.exp(lamdt)
    b_bar = ((lam_bar - 1.0) / lam)[:, :, None] * lax.complex(b_re, b_im)
    cm = lax.complex(c_re, c_im)
    pw = jnp.exp(lamdt[:, :, None] * jnp.arange(L + 1, dtype=F32))
    mm = pw[:, :, :L, None] * b_bar[:, :, None, :]
    kd = jnp.sum((cm[:, :, :, None, None] * mm[:, None]).real, axis=2).transpose(0, 2, 1, 3)
    bp = pw[:, :, L - 1 - jnp.arange(L)][:, :, :, None] * b_bar[:, :, None, :]
    cp = cm[:, :, :, None] * pw[:, None, :, 1:]
    nq = g // S5_LB
    eye = jnp.eye(S5_LB, dtype=F32)

    def lane_block(m, spec, rows, cols):
        return jnp.einsum(spec, m.reshape((nq, S5_LB) + m.shape[1:]), eye).reshape(nq, rows, cols)

    sw = S5_LB * p
    bre = lane_block(bp.real, "qgpsc,hg->qshcgp", L * LANES, sw)
    bim = lane_block(bp.imag, "qgpsc,hg->qshcgp", L * LANES, sw)
    mst = lane_block(kd, "qgdoc,hg->qdhcgo", L * LANES, LANES)
    cre = lane_block(cp.real, "qgopt,hg->qthpgo", L * sw, LANES).reshape(nq, L, sw, LANES)
    cim = lane_block(-cp.imag, "qgopt,hg->qthpgo", L * sw, LANES).reshape(nq, L, sw, LANES)
    return dict(
        bre=bre.astype(BF16), bim=bim.astype(BF16), mst=mst.astype(BF16),
        cc=jnp.concatenate([cre, cim], axis=2).astype(BF16),
        lre=pw[:, :, L].real.reshape(1, g * p), lim=pw[:, :, L].imag.reshape(1, g * p))


def _ab_out_kernel(x_ref, ys_ref, u_ref, at_ref, d_ref, gw_ref, gb_ref, ws_ref, wa_ref, g1_ref, *route_refs):
    o_ref = route_refs[5]
    y = ys_ref[0] + d_ref[...] * u_ref[0]
    y = jax.nn.gelu(y)
    gate = jax.nn.sigmoid(_dot(y.astype(BF16), gw_ref[...]) + gb_ref[...])
    out = _dot((y * gate).astype(BF16), ws_ref[...]) + _dot(at_ref[0], wa_ref[...])
    xn = x_ref[0] + g1_ref[0] * out
    o_ref[0] = xn
    _route(xn, *route_refs[:5], *route_refs[6:])


def _ab_out_call(x, ys, u, attn, wp, g1, rt, ts):
    b, s, d = x.shape
    r_ins, r_specs, r_ospecs, r_oshape, r_scratch = _route_plumbing(rt, b, s, d, ts)
    ins = [x, ys, u, attn, wp["s5_d"], wp["glu_w"], wp["glu_b"], wp["w_out_s5"], wp["w_out_at"], g1] + r_ins
    specs = [_tok(d, ts), _tok(S5_WIDTH, ts), _tok(S5_WIDTH, ts), _tok(MLA_HEADS * MLA_V, ts),
             _full(wp["s5_d"]), _full(wp["glu_w"]), _full(wp["glu_b"]), _full(wp["w_out_s5"]),
             _full(wp["w_out_at"]), _modspec(g1, ts)] + r_specs
    return pl.pallas_call(
        _ab_out_kernel,
        grid=(b, s // ts),
        in_specs=specs,
        out_specs=[_tok(d, ts)] + r_ospecs,
        out_shape=[jax.ShapeDtypeStruct((b, s, d), F32)] + r_oshape,
        scratch_shapes=r_scratch,
        compiler_params=_cparams("arbitrary", "arbitrary"),
        name="ab_out",
    )(*ins)


def _c_in_kernel(x_ref, gain_ref, sc_ref, sh_ref, w_ref, qkv_ref, z_ref, ab_ref):
    h = _normmod(x_ref[0], gain_ref[...], sc_ref[0], sh_ref[0]).astype(BF16)
    proj = _dot(h, w_ref[...])
    qkv_ref[0] = proj[:, :GDN_QKV].astype(BF16)
    z_ref[0] = proj[:, GDN_QKV:GDN_QKV + GDN_VW].astype(BF16)
    ab_ref[0] = proj[:, GDN_QKV + GDN_VW:]


def _c_in_call(x, gain, sc, sh, w, ts):
    b, s, d = x.shape
    return pl.pallas_call(
        _c_in_kernel,
        grid=(b, s // ts),
        in_specs=[_tok(d, ts), _full(gain), _modspec(sc, ts), _modspec(sh, ts), _full(w)],
        out_specs=[_tok(GDN_QKV, ts), _tok(GDN_VW, ts), _tok(LANES, ts)],
        out_shape=[jax.ShapeDtypeStruct((b, s, GDN_QKV), BF16),
                   jax.ShapeDtypeStruct((b, s, GDN_VW), BF16),
                   jax.ShapeDtypeStruct((b, s, LANES), F32)],
        compiler_params=_cparams("parallel", "parallel"),
        name="c_in",
    )(x, gain, sc, sh, w)


def _gdn_prep_kernel(qkv_ref, halo_ref, st_ref, cw_ref, ab_ref, alog_ref, dtb_ref,
                     wq_ref, u_ref, kgt_ref, pm_ref, ee_ref, *, L, nc):
    j = pl.program_id(1)
    rows = nc * L
    prev = jnp.where(j == 0, st_ref[0].astype(F32), halo_ref[0].astype(F32))
    xx = jnp.concatenate([prev, qkv_ref[0].astype(F32)], axis=0)
    cw = cw_ref[...]
    y = xx[8:8 + rows] * cw[CONV_W - 1:CONV_W]
    for t in range(1, CONV_W):
        y += xx[8 - t:8 - t + rows] * cw[CONV_W - 1 - t:CONV_W - t]
    y = y * jax.nn.sigmoid(y)

    ab = ab_ref[0]
    g_all = -jnp.exp(alog_ref[...]) * jax.nn.softplus(ab + dtb_ref[...])
    beta_all = jax.nn.sigmoid(ab)
    row = lax.broadcasted_iota(jnp.int32, (L, L), 0)
    col = lax.broadcasted_iota(jnp.int32, (L, L), 1)
    incl = row >= col
    strict = row > col
    brow = lax.broadcasted_iota(jnp.int32, (rows, rows), 0)
    bcol = lax.broadcasted_iota(jnp.int32, (rows, rows), 1)
    incl_blk = jnp.logical_and(brow >= bcol, brow // L == bcol // L)
    gam_all = jnp.dot(incl_blk.astype(F32), g_all, preferred_element_type=F32, precision=HI)
    eye_l = (lax.broadcasted_iota(jnp.int32, (LANES, LANES), 0)
             == lax.broadcasted_iota(jnp.int32, (LANES, LANES), 1))
    gam_t = _dot_nt(eye_l.astype(F32), gam_all, precision=HI)
    for ci in range(nc):
        ee_ref[0, ci] = jnp.exp(gam_all[(ci + 1) * L - 1:(ci + 1) * L])
    eye64 = (row == col).astype(F32)
    eye_bf = eye_l.astype(BF16)

    ch = [(ci, h) for ci in range(nc) for h in range(GDN_HEADS)]
    cs = range(len(ch))
    koff, voff = GDN_HEADS * GDN_DK, 2 * GDN_HEADS * GDN_DK
    qh = [y[ci * L:(ci + 1) * L, GDN_DK * h:GDN_DK * (h + 1)] for ci, h in ch]
    kh = [y[ci * L:(ci + 1) * L, koff + GDN_DK * h:koff + GDN_DK * (h + 1)] for ci, h in ch]
    vh = [y[ci * L:(ci + 1) * L, voff + GDN_DV * h:voff + GDN_DV * (h + 1)] for ci, h in ch]
    qh = [q * lax.rsqrt(jnp.sum(q * q, axis=-1, keepdims=True) + EPS) * (GDN_DK ** -0.5) for q in qh]
    kh = [k * lax.rsqrt(jnp.sum(k * k, axis=-1, keepdims=True) + EPS) for k in kh]
    gam_c = [gam_all[ci * L:(ci + 1) * L, h:h + 1] for ci, h in ch]
    beta_c = [beta_all[ci * L:(ci + 1) * L, GDN_HEADS + h:GDN_HEADS + h + 1] for ci, h in ch]
    dec = [jnp.exp(jnp.minimum(gam_c[c] - gam_t[h:h + 1, ci * L:(ci + 1) * L], 0.0))
           for c, (ci, h) in enumerate(ch)]
    kb = [k.astype(BF16) for k in kh]
    kk = [_dot_nt(kb[c], kb[c]) for c in cs]
    qk = [_dot_nt(qh[c].astype(BF16), kb[c]) for c in cs]
    a = [jnp.where(strict, beta_c[c] * kk[c] * dec[c], 0.0) for c in cs]
    for c, (ci, h) in enumerate(ch):
        pm_ref[0, ci, h] = jnp.where(incl, qk[c] * dec[c], 0.0).astype(BF16)
    tinv = [eye64 - a[c] for c in cs]
    pw = [_split(a[c]) for c in cs]
    for _ in range(max(L.bit_length() - 2, 0)):
        pw = [_split(_dot3(pw[c], pw[c])) for c in cs]
        tinv = [tinv[c] + _dot3(_split(tinv[c]), pw[c]) for c in cs]
    eg = [jnp.exp(gam_c[c]) for c in cs]
    rhs = [jnp.concatenate([(beta_c[c] * eg[c]) * kh[c], beta_c[c] * vh[c]], axis=1).astype(BF16) for c in cs]
    wu = [_dot(tinv[c].astype(BF16), rhs[c]) for c in cs]
    for c, (ci, h) in enumerate(ch):
        wq_ref[0, ci, h] = jnp.concatenate([wu[c][:, :GDN_DK], qh[c] * eg[c]], axis=0).astype(BF16)
        u_ref[0, h, ci * L:(ci + 1) * L, :] = wu[c][:, GDN_DK:].astype(BF16)
    kg = [(kh[c] * jnp.exp(gam_all[(ci + 1) * L - 1:(ci + 1) * L, h:h + 1] - gam_c[c])).astype(BF16)
          for c, (ci, h) in enumerate(ch)]
    kgt = [_dot_nt(eye_bf, kg[c]) for c in cs]
    for c, (ci, h) in enumerate(ch):
        kgt_ref[0, ci, h] = kgt[c].astype(BF16)


def _split(a):
    hi = a.astype(BF16)
    return hi, (a - hi.astype(F32)).astype(BF16)


def _dot3(a, b):
    return _dot(a[0], b[0]) + (_dot(a[0], b[1]) + _dot(a[1], b[0]))


GDN_PREP_CHUNKS = 2


def _gdn_prep_call(qkv, st8, conv_w, ab, alog, dtb, L):
    b, s, _ = qkv.shape
    nch = s // L
    nc = GDN_PREP_CHUNKS if nch % GDN_PREP_CHUNKS == 0 else 1
    rows = nc * L
    return pl.pallas_call(
        functools.partial(_gdn_prep_kernel, L=L, nc=nc),
        grid=(b, nch // nc),
        in_specs=[_tok(GDN_QKV, rows),
                  pl.BlockSpec((1, 8, GDN_QKV), lambda bb, j: (bb, jnp.maximum(j * (rows // 8) - 1, 0), 0)),
                  pl.BlockSpec((1, 8, GDN_QKV), lambda bb, j: (bb, 0, 0)),
                  _full(conv_w), _tok(LANES, rows), _full(alog), _full(dtb)],
        out_specs=[pl.BlockSpec((1, nc, GDN_HEADS, 2 * L, LANES), lambda bb, j: (bb, j, 0, 0, 0)),
                   pl.BlockSpec((1, GDN_HEADS, rows, LANES), lambda bb, j: (bb, 0, j, 0)),
                   pl.BlockSpec((1, nc, GDN_HEADS, GDN_DK, L), lambda bb, j: (bb, j, 0, 0, 0)),
                   pl.BlockSpec((1, nc, GDN_HEADS, L, L), lambda bb, j: (bb, j, 0, 0, 0)),
                   pl.BlockSpec((1, nc, 1, LANES), lambda bb, j: (bb, j, 0, 0))],
        out_shape=[jax.ShapeDtypeStruct((b, nch, GDN_HEADS, 2 * L, LANES), BF16),
                   jax.ShapeDtypeStruct((b, GDN_HEADS, s, LANES), BF16),
                   jax.ShapeDtypeStruct((b, nch, GDN_HEADS, GDN_DK, L), BF16),
                   jax.ShapeDtypeStruct((b, nch, GDN_HEADS, L, L), BF16),
                   jax.ShapeDtypeStruct((b, nch, 1, LANES), F32)],
        compiler_params=_cparams("parallel", "parallel"),
        name="gdn_prep",
    )(qkv, qkv, st8, conv_w, ab, alog, dtb)


GDN_BATCH_GROUP = 2


def _gdn_seq_kernel(wq_ref, u_ref, kgt_ref, pm_ref, ee_ref, s0_ref, o_ref, sf_ref, s_scr, *, L):
    c = pl.program_id(1)

    @pl.when(c == 0)
    def _():
        s_scr[...] = s0_ref[...]

    ch = [(bb, hh) for bb in range(GDN_BATCH_GROUP) for hh in range(GDN_HEADS)]
    st = [s_scr[bb, hh] for bb, hh in ch]
    sp = [_split(s) for s in st]
    r = [_dot(wq_ref[bb, 0, hh], sp[i][0]) + _dot(wq_ref[bb, 0, hh], sp[i][1]) for i, (bb, hh) in enumerate(ch)]
    v_new = [(u_ref[bb, hh].astype(F32) - r[i][:L]).astype(BF16) for i, (bb, hh) in enumerate(ch)]
    o = [r[i][L:] + _dot(pm_ref[bb, 0, hh], v_new[i]) for i, (bb, hh) in enumerate(ch)]
    for i, (bb, hh) in enumerate(ch):
        o_ref[bb, :, GDN_DV * hh:GDN_DV * (hh + 1)] = o[i].astype(o_ref.dtype)
    s_new = [ee_ref[bb, 0][:, hh:hh + 1] * st[i] + _dot(kgt_ref[bb, 0, hh], v_new[i]) for i, (bb, hh) in enumerate(ch)]
    for i, (bb, hh) in enumerate(ch):
        s_scr[bb, hh] = s_new[i]

    @pl.when(c == pl.num_programs(1) - 1)
    def _():
        sf_ref[...] = s_scr[...]


def _gdn_seq_call(wq, um, kgt, pm, ee, s0, L):
    b, h, s, _ = um.shape
    nch = s // L
    bg = GDN_BATCH_GROUP
    sspec = pl.BlockSpec((bg, h, GDN_DK, GDN_DV), lambda g, c: (g, 0, 0, 0))
    return pl.pallas_call(
        functools.partial(_gdn_seq_kernel, L=L),
        grid=(b // bg, nch),
        in_specs=[pl.BlockSpec((bg, 1, h, 2 * L, LANES), lambda g, c: (g, c, 0, 0, 0)),
                  pl.BlockSpec((bg, h, L, LANES), lambda g, c: (g, 0, c, 0)),
                  pl.BlockSpec((bg, 1, h, GDN_DK, L), lambda g, c: (g, c, 0, 0, 0)),
                  pl.BlockSpec((bg, 1, h, L, L), lambda g, c: (g, c, 0, 0, 0)),
                  pl.BlockSpec((bg, 1, 1, LANES), lambda g, c: (g, c, 0, 0)),
                  sspec],
        out_specs=[pl.BlockSpec((bg, L, GDN_VW), lambda g, c: (g, c, 0)), sspec],
        out_shape=[jax.ShapeDtypeStruct((b, s, GDN_VW), BF16),
                   jax.ShapeDtypeStruct((b, h, GDN_DK, GDN_DV), F32)],
        scratch_shapes=[pltpu.VMEM((bg, h, GDN_DK, GDN_DV), F32)],
        compiler_params=_cparams("parallel", "arbitrary"),
        name="gdn_seq",
    )(wq, um, kgt, pm, ee, s0)


def _c_out_kernel(x_ref, o_ref, z_ref, gn_ref, w_ref, g1_ref, *route_refs):
    out_ref = route_refs[5]
    o = o_ref[0].astype(F32)
    z = z_ref[0].astype(F32)
    gn = gn_ref[...]
    parts = []
    for hh in range(GDN_HEADS):
        oh = o[:, GDN_DV * hh:GDN_DV * (hh + 1)]
        zh = z[:, GDN_DV * hh:GDN_DV * (hh + 1)]
        parts.append((_rms(oh, gn) * (zh * jax.nn.sigmoid(zh))).astype(BF16))
    out = _dot(jnp.concatenate(parts, axis=1), w_ref[...])
    xn = x_ref[0] + g1_ref[0] * out
    out_ref[0] = xn
    _route(xn, *route_refs[:5], *route_refs[6:])


def _c_out_call(x, o, z, gn, w, g1, rt, ts):
    b, s, d = x.shape
    r_ins, r_specs, r_ospecs, r_oshape, r_scratch = _route_plumbing(rt, b, s, d, ts)
    return pl.pallas_call(
        _c_out_kernel,
        grid=(b, s // ts),
        in_specs=[_tok(d, ts), _tok(GDN_VW, ts), _tok(GDN_VW, ts), _full(gn), _full(w), _modspec(g1, ts)] + r_specs,
        out_specs=[_tok(d, ts)] + r_ospecs,
        out_shape=[jax.ShapeDtypeStruct((b, s, d), F32)] + r_oshape,
        scratch_shapes=r_scratch,
        compiler_params=_cparams("arbitrary", "arbitrary"),
        name="c_out",
    )(x, o, z, gn, w, g1, *r_ins)


def _pack_halves(v):
    n = v.shape[1] // 2
    bits = pltpu.bitcast(v.astype(BF16).astype(F32), jnp.uint32)
    return (bits[:, :n] >> 16) | (bits[:, n:] & jnp.uint32(0xFFFF0000))


def _unpack_halves(w):
    return pltpu.bitcast(w << 16, F32), pltpu.bitcast(w & jnp.uint32(0xFFFF0000), F32)


def _route(xn, gain_ref, sc_ref, sh_ref, wr_ref, br_ref, h_ref, re_ref, rg_ref, cnt_ref, carry):
    @pl.when(jnp.logical_and(pl.program_id(0) == 0, pl.program_id(1) == 0))
    def _():
        carry[...] = jnp.zeros_like(carry)

    h = _normmod(xn, gain_ref[...], sc_ref[0], sh_ref[0])
    h_ref[0] = _pack_halves(h)
    logits = _dot3(_split(h), (wr_ref[0], wr_ref[1])) + br_ref[...]
    lane = lax.broadcasted_iota(jnp.int32, logits.shape, 1).astype(F32)
    neg = jnp.float32(-jnp.inf)
    big = jnp.float32(1e9)
    gl = jnp.where(lane < N_GROUPS, logits, neg)
    gm = jnp.max(gl, axis=-1, keepdims=True)
    g_val = 1.0 / jnp.sum(jnp.exp(gl - gm), axis=-1, keepdims=True)
    g_idx = jnp.min(jnp.where(gl == gm, lane, big), axis=-1, keepdims=True)
    lo = N_GROUPS + EXPERTS_PER_GROUP * g_idx
    el = jnp.where(jnp.logical_and(lane >= lo, lane < lo + EXPERTS_PER_GROUP), logits, neg)
    em = jnp.max(el, axis=-1, keepdims=True)
    i1 = jnp.min(jnp.where(el == em, lane, big), axis=-1, keepdims=True)
    el2 = jnp.where(lane == i1, neg, el)
    em2 = jnp.max(el2, axis=-1, keepdims=True)
    i2 = jnp.min(jnp.where(el2 == em2, lane, big), axis=-1, keepdims=True)
    es = jnp.sum(jnp.exp(el - em), axis=-1, keepdims=True)
    p1 = 1.0 / es
    p2 = jnp.exp(em2 - em) / es
    den = p1 + p2
    rg_ref[0] = jnp.where(lane == 0, g_val * p1 / den, jnp.where(lane == 1, g_val * p2 / den, 0.0))
    oh1 = lane == i1
    oh2 = lane == i2
    both = jnp.where(jnp.logical_or(oh1, oh2), 1.0, 0.0)
    ts = both.shape[0]
    tri = (lax.broadcasted_iota(jnp.int32, (ts, ts), 0) > lax.broadcasted_iota(jnp.int32, (ts, ts), 1))
    pre = _dot(jnp.where(tri, 1.0, 0.0).astype(BF16), both.astype(BF16)) + carry[...]
    r1 = jnp.sum(jnp.where(oh1, pre, 0.0), axis=-1, keepdims=True)
    r2 = jnp.sum(jnp.where(oh2, pre, 0.0), axis=-1, keepdims=True)
    re_ref[0] = jnp.where(lane == 0, i1 - N_GROUPS, jnp.where(lane == 1, i2 - N_GROUPS, jnp.where(
        lane == 2, r1, jnp.where(lane == 3, r2, 0.0)))).astype(jnp.int32)
    carry[...] += jnp.sum(both, axis=0, keepdims=True)
    cnt_ref[...] = carry[...]


def _route_plumbing(rt, b, s, d, ts):
    gain, sc, sh, wr, br = rt
    ins = [gain, sc, sh, wr, br]
    in_specs = [_full(gain), _modspec(sc, ts), _modspec(sh, ts), _full(wr), _full(br)]
    out_specs = [_tok(d // 2, ts), _tok(LANES, ts), _tok(LANES, ts), pl.BlockSpec((1, LANES), lambda bb, j: (0, 0))]
    out_shape = [jax.ShapeDtypeStruct((b, s, d // 2), jnp.uint32),
                 jax.ShapeDtypeStruct((b, s, LANES), jnp.int32),
                 jax.ShapeDtypeStruct((b, s, LANES), F32),
                 jax.ShapeDtypeStruct((1, LANES), F32)]
    return ins, in_specs, out_specs, out_shape, [pltpu.VMEM((1, LANES), F32)]


def _row_copy(src, dst, src_row, dst_row, sem):
    return pltpu.make_async_copy(src.at[pl.ds(src_row, 1)], dst.at[pl.ds(dst_row, 1)], sem)


ROUTE_W = 2 * TOP_K
DMA_UNROLL = 8


def _slot(ps_ref, rt_ref, r, k):
    return ps_ref[rt_ref[0, 0, ROUTE_W * r + k]] + rt_ref[0, 0, ROUTE_W * r + TOP_K + k]


def _dispatch_kernel(ps_ref, rt_ref, h_ref, xin_hbm, xb_hbm, sem, *, tb):
    del xin_hbm

    def issue(r, c):
        for k in range(TOP_K):
            _row_copy(h_ref, xb_hbm, r, _slot(ps_ref, rt_ref, r, k), sem).start()
        return c
    lax.fori_loop(0, tb, issue, 0, unroll=DMA_UNROLL)
    for k in range(TOP_K):
        pltpu.make_async_copy(h_ref, xb_hbm.at[pl.ds(0, tb)], sem).wait()


def _dispatch_call(pad_start, route, h2, xb0, tb):
    n, dh = h2.shape
    nb = n // tb
    grid_spec = pltpu.PrefetchScalarGridSpec(
        num_scalar_prefetch=1,
        grid=(nb,),
        in_specs=[pl.BlockSpec((1, 1, ROUTE_W * tb), lambda j, ps: (j, 0, 0), memory_space=pltpu.SMEM),
                  pl.BlockSpec((tb, dh), lambda j, ps: (j, 0)),
                  pl.BlockSpec(memory_space=pl.ANY)],
        out_specs=pl.BlockSpec(memory_space=pl.ANY),
        scratch_shapes=[pltpu.SemaphoreType.DMA(())],
    )
    return pl.pallas_call(
        functools.partial(_dispatch_kernel, tb=tb),
        grid_spec=grid_spec,
        out_shape=jax.ShapeDtypeStruct(xb0.shape, xb0.dtype),
        input_output_aliases={3: 0},
        compiler_params=_cparams("arbitrary"),
        name="moe_dispatch",
    )(pad_start, route, h2, xb0)


def _ffn_kernel(be_ref, nu_ref, x_ref, w1_ref, w3_ref, w2_ref, y_ref, w1b, w3b, w2b):
    i = pl.program_id(0)
    half = w1b.shape[0] // 2

    @pl.when(jnp.logical_and(i < nu_ref[0], jnp.logical_or(i == 0, be_ref[i] != be_ref[jnp.maximum(i - 1, 0)])))
    def _():
        w1b[...] = w1_ref[0, 0].astype(BF16)
        w3b[...] = w3_ref[0, 0].astype(BF16)
        w2b[...] = w2_ref[0, 0].astype(BF16)

    @pl.when(i < nu_ref[0])
    def _():
        lo, hi = _unpack_halves(x_ref[...])
        lo = lo.astype(BF16)
        hi = hi.astype(BF16)
        a = _dot(lo, w1b[:half]) + _dot(hi, w1b[half:])
        g = _dot(lo, w3b[:half]) + _dot(hi, w3b[half:])
        mid = (a * jax.nn.sigmoid(a) * g).astype(BF16)
        y_ref[...] = _pack_halves(_dot(mid, w2b[...]))

    @pl.when(i >= nu_ref[0])
    def _():
        y_ref[...] = jnp.zeros_like(y_ref)


def _ffn_call(block_e, n_used, xb, w1, w3, w2, layer):
    n_blocks = block_e.shape[0]
    dh = xb.shape[1]
    d = 2 * dh
    grid_spec = pltpu.PrefetchScalarGridSpec(
        num_scalar_prefetch=2,
        grid=(n_blocks,),
        in_specs=[pl.BlockSpec((MOE_BLOCK, dh), lambda i, be, nu: (i, 0)),
                  pl.BlockSpec((1, 1, d, D_EXPERT), lambda i, be, nu: (layer, be[i], 0, 0)),
                  pl.BlockSpec((1, 1, d, D_EXPERT), lambda i, be, nu: (layer, be[i], 0, 0)),
                  pl.BlockSpec((1, 1, D_EXPERT, d), lambda i, be, nu: (layer, be[i], 0, 0))],
        out_specs=pl.BlockSpec((MOE_BLOCK, dh), lambda i, be, nu: (i, 0)),
        scratch_shapes=[pltpu.VMEM((d, D_EXPERT), BF16), pltpu.VMEM((d, D_EXPERT), BF16),
                        pltpu.VMEM((D_EXPERT, d), BF16)],
    )
    return pl.pallas_call(
        _ffn_kernel,
        grid_spec=grid_spec,
        out_shape=jax.ShapeDtypeStruct((n_blocks * MOE_BLOCK, dh), jnp.uint32),
        compiler_params=_cparams("arbitrary"),
        name="moe_ffn",
    )(block_e, n_used, xb, w1, w3, w2)


def _combine_kernel(ps_ref, rcur_ref, rnxt_ref, yb_hbm, x_ref, rg_ref, g2_ref, fin_ref, o_ref, b0, b1, sem,
                    *, tb, final):
    j = pl.program_id(0)
    nb = pl.num_programs(0)
    slot = j % 2

    def fetch(rref, sl):
        def issue(r, c):
            _row_copy(yb_hbm, b0.at[sl], _slot(ps_ref, rref, r, 0), r, sem.at[sl]).start()
            _row_copy(yb_hbm, b1.at[sl], _slot(ps_ref, rref, r, 1), r, sem.at[sl]).start()
            return c
        lax.fori_loop(0, tb, issue, 0, unroll=DMA_UNROLL)

    @pl.when(j == 0)
    def _():
        fetch(rcur_ref, 0)

    @pl.when(j + 1 < nb)
    def _():
        fetch(rnxt_ref, 1 - slot)

    pltpu.make_async_copy(yb_hbm.at[pl.ds(0, tb)], b0.at[slot], sem.at[slot]).wait()
    pltpu.make_async_copy(yb_hbm.at[pl.ds(0, tb)], b1.at[slot], sem.at[slot]).wait()
    rg = rg_ref[...]
    lo0, hi0 = _unpack_halves(b0[slot])
    lo1, hi1 = _unpack_halves(b1[slot])
    g0 = rg[:, 0:1]
    g1 = rg[:, 1:2]
    y = jnp.concatenate([g0 * lo0 + g1 * lo1, g0 * hi0 + g1 * hi1], axis=1)
    xn = x_ref[...] + g2_ref[0] * y
    if final:
        xn = _rms(xn, fin_ref[...])
    o_ref[...] = xn


def _combine_call(pad_start, route, yb, x, rg, g2, fin, tb, final):
    b, s, d = x.shape
    n = b * s
    nb = n // tb
    npb = s // tb
    dh = yb.shape[1]
    if g2.shape[1] == 1:
        gspec = pl.BlockSpec((1, 1, d), lambda j, ps: (j // npb, 0, 0))
    else:
        gspec = pl.BlockSpec((1, tb, d), lambda j, ps: (j // npb, j % npb, 0))
    grid_spec = pltpu.PrefetchScalarGridSpec(
        num_scalar_prefetch=1,
        grid=(nb,),
        in_specs=[pl.BlockSpec((1, 1, ROUTE_W * tb), lambda j, ps: (j, 0, 0), memory_space=pltpu.SMEM),
                  pl.BlockSpec((1, 1, ROUTE_W * tb), lambda j, ps: (jnp.minimum(j + 1, nb - 1), 0, 0),
                               memory_space=pltpu.SMEM),
                  pl.BlockSpec(memory_space=pl.ANY),
                  pl.BlockSpec((tb, d), lambda j, ps: (j, 0)), pl.BlockSpec((tb, LANES), lambda j, ps: (j, 0)),
                  gspec, pl.BlockSpec(fin.shape, lambda j, ps: (0, 0))],
        out_specs=pl.BlockSpec((tb, d), lambda j, ps: (j, 0)),
        scratch_shapes=[pltpu.VMEM((2, tb, dh), jnp.uint32), pltpu.VMEM((2, tb, dh), jnp.uint32),
                        pltpu.SemaphoreType.DMA((2,))],
    )
    out = pl.pallas_call(
        functools.partial(_combine_kernel, tb=tb, final=final),
        grid_spec=grid_spec,
        out_shape=jax.ShapeDtypeStruct((n, d), F32),
        compiler_params=_cparams("arbitrary"),
        name="moe_combine",
    )(pad_start, route, route, yb, x.reshape(n, d), rg.reshape(n, LANES), g2, fin)
    return out.reshape(b, s, d)


def _moe(x, routed, g2, mp, fin, ts, final):
    b, s, d = x.shape
    n = b * s
    h2, r_e, r_g, cnt = routed
    counts = cnt[0, N_GROUPS:N_GROUPS + N_EXPERTS].astype(jnp.int32)
    padded = (counts + MOE_BLOCK - 1) // MOE_BLOCK * MOE_BLOCK
    pad_end = jnp.cumsum(padded)
    pad_start = (pad_end - padded).astype(jnp.int32)
    n_blocks = (n * TOP_K + N_EXPERTS * (MOE_BLOCK - 1) + MOE_BLOCK - 1) // MOE_BLOCK
    blk0 = jnp.arange(n_blocks, dtype=jnp.int32) * MOE_BLOCK
    block_e = jnp.minimum(jnp.sum((pad_end[None, :] <= blk0[:, None]).astype(jnp.int32), axis=1), N_EXPERTS - 1)
    n_used = (pad_end[-1:] // MOE_BLOCK).astype(jnp.int32)
    route = r_e[:, :, :ROUTE_W].reshape(n // ts, 1, ROUTE_W * ts)
    xb0 = jnp.zeros((n_blocks * MOE_BLOCK, d // 2), jnp.uint32)
    xb = _dispatch_call(pad_start, route, h2.reshape(n, d // 2), xb0, ts)
    yb = _ffn_call(block_e, n_used, xb, mp["w1"], mp["w3"], mp["w2"], mp["layer"])
    return _combine_call(pad_start, route, yb, x, r_g, g2, fin, ts, final)


def _prep_ab(l, p):
    w_in = p["w_in_ab"][l]
    o2 = S5_WIDTH + MLA_Q_RANK + MLA_KV_RANK
    half = MLA_ROPE // 2
    d = w_in.shape[0]
    kr = w_in[:, o2:]
    kr_rot = jnp.concatenate([-kr[:, half:], kr[:, :half]], axis=1)
    z64 = jnp.zeros((d, MLA_NOPE), F32)
    z32 = jnp.zeros((d, LANES - MLA_NOPE - MLA_ROPE), F32)
    w_ext = jnp.concatenate([w_in[:, :o2], z64, kr, z32, z64, kr_rot, z32], axis=1)
    qu = p["mla_q_up"][l].reshape(MLA_Q_RANK, MLA_HEADS, MLA_NOPE + MLA_ROPE)
    qn, qr = qu[:, :, :MLA_NOPE], qu[:, :, MLA_NOPE:]
    zq = jnp.zeros((MLA_Q_RANK, MLA_HEADS, LANES - MLA_NOPE - MLA_ROPE), F32)
    wqa = jnp.concatenate([qn, qr, zq], axis=2).reshape(MLA_Q_RANK, MLA_HEADS * LANES)
    wqb = jnp.concatenate([jnp.zeros_like(qn), -qr[:, :, half:], qr[:, :, :half], zq], axis=2)
    wqb = wqb.reshape(MLA_Q_RANK, MLA_HEADS * LANES)
    kvu = p["mla_kv_up"][l].reshape(MLA_KV_RANK, MLA_HEADS, MLA_NOPE + MLA_V)
    wk = jnp.concatenate([kvu[:, :, :MLA_NOPE], jnp.zeros((MLA_KV_RANK, MLA_HEADS, LANES - MLA_NOPE), F32)], axis=2)
    wo = p["w_out_ab"][l]
    out = dict(
        w_in=w_ext.astype(BF16), q_norm=p["mla_q_norm"][l].reshape(1, -1), kv_norm=p["mla_kv_norm"][l].reshape(1, -1),
        wqa=wqa.astype(BF16), wqb=wqb.astype(BF16),
        wk=wk.reshape(MLA_KV_RANK, MLA_HEADS * LANES).astype(BF16),
        wv=jnp.concatenate([kvu[:, :, MLA_NOPE:], jnp.zeros((MLA_KV_RANK, MLA_HEADS, LANES - MLA_V), F32)],
                           axis=2).reshape(MLA_KV_RANK, MLA_HEADS * LANES).astype(BF16),
        s5_d=p["s5_d"][l].reshape(1, -1), glu_w=p["s5_glu_w"][l].astype(BF16), glu_b=p["s5_glu_b"][l].reshape(1, -1),
        w_out_s5=wo[:S5_WIDTH].astype(BF16),
        w_out_at=wo[S5_WIDTH:].astype(BF16))
    out.update(_s5_mats(p["s5_a_re"][l], p["s5_a_im"][l], p["s5_b_re"][l], p["s5_b_im"][l],
                        p["s5_c_re"][l], p["s5_c_im"][l], p["s5_log_dt"][l]))
    return out


def _prep_c(l, p):
    w = p["w_in_c"][l]
    pad = jnp.zeros((w.shape[0], LANES - 2 * GDN_HEADS), F32)

    def row(v):
        return jnp.concatenate([v, jnp.zeros((LANES - GDN_HEADS,), F32)]).reshape(1, LANES)

    return dict(w_in=jnp.concatenate([w, pad], axis=1).astype(BF16), conv_w=p["conv_w"][l],
                alog=row(p["gdn_a_log"][l]), dtb=row(p["gdn_dt_bias"][l]),
                gn=p["gdn_norm"][l].reshape(1, -1), w_out=p["w_out_c"][l].astype(BF16))


def _prep_moe(layer, p):
    d = p["moe_w_group"].shape[1]
    wr = jnp.concatenate([p["moe_w_group"][layer], p["moe_w_expert"][layer],
                          jnp.zeros((d, LANES - N_GROUPS - N_EXPERTS), F32)], axis=1)
    br = jnp.concatenate([p["moe_b_group"][layer], p["moe_b_expert"][layer],
                          jnp.zeros((LANES - N_GROUPS - N_EXPERTS,), F32)]).reshape(1, LANES)
    wr_hi = wr.astype(BF16)
    wr_lo = (wr - wr_hi.astype(F32)).astype(BF16)
    return dict(wr=jnp.stack([wr_hi, wr_lo]), br=br, w1=p["moe_w1"], w3=p["moe_w3"], w2=p["moe_w2"], layer=layer)


def _rope_tables(pos):
    half = MLA_ROPE // 2
    inv = ROPE_THETA ** (-jnp.arange(half, dtype=F32) / half)
    ang = pos.astype(F32)[:, None] * inv[None, :]
    cos, sin = jnp.cos(ang), jnp.sin(ang)
    n = pos.shape[0]
    z64 = jnp.zeros((n, MLA_NOPE), F32)
    z32 = jnp.zeros((n, LANES - MLA_NOPE - MLA_ROPE), F32)
    scale = (MLA_NOPE + MLA_ROPE) ** -0.5 * math.log2(math.e)
    ck = jnp.concatenate([z64, cos, cos, z32], axis=1)
    sk = jnp.concatenate([z64, sin, sin, z32], axis=1)
    cq = jnp.concatenate([jnp.ones_like(z64), cos, cos, z32], axis=1) * scale
    return cq, sk * scale, ck, sk


def _trunk(x, mods, caches, wab, wc, wmoe, p, flat):
    b, s, d = x.shape
    lat_c, rope_c, s5re_c, s5im_c, conv_c, gdn_c = caches
    past = 0 if lat_c is None else lat_c.shape[2]
    pos = past + jnp.arange(s)
    if flat:
        xt = x.reshape(1, b * s, d)
        ts = b * s
        pos_rows = jnp.tile(pos, b)

        def mrow(m):
            return jnp.repeat(m, s, axis=0).reshape(1, b * s, d)
    else:
        xt = x
        ts = TOKEN_BLOCK
        pos_rows = pos

        def mrow(m):
            return m.reshape(b, 1, d)
    bt, st = xt.shape[:2]
    outs = {}
    depth = p["norm_mix"].shape[0]
    for layer in range(depth):
        l = layer // 2
        sh1, sc1, g1, sh2, sc2, g2 = [mrow(m) for m in jnp.split(mods[layer], 6, axis=-1)]
        gain = p["norm_mix"][layer].reshape(1, d)
        rt = (p["norm_ffn"][layer].reshape(1, d), sc2, sh2, wmoe[layer]["wr"], wmoe[layer]["br"])
        if layer % 2 == 0:
            wp = wab[l]
            u, q, lat, kr = _ab_in_call(xt, gain, sc1, sh1, wp, _rope_tables(pos_rows), ts)
            lat_b = lat.reshape(b, s, MLA_KV_RANK)
            kr_b = kr.reshape(b, s, LANES)
            outs["lat"] = lat_b
            outs["krope"] = kr_b[:, :, MLA_NOPE:MLA_NOPE + MLA_ROPE]
            if lat_c is None:
                lat_all, kr_all, sk_valid, tkv = lat_b, kr_b, s, TOKEN_BLOCK
                tq, tk = min(ATTN_TQ, s), min(ATTN_TK, s)
            else:
                krc = jnp.pad(rope_c[l], ((0, 0), (0, 0), (MLA_NOPE, LANES - MLA_NOPE - MLA_ROPE)))
                sk_valid = past + s
                skp = (sk_valid + LANES - 1) // LANES * LANES
                lat_all = jnp.pad(jnp.concatenate([lat_c[l], lat_b], axis=1), ((0, 0), (0, skp - sk_valid), (0, 0)))
                kr_all = jnp.pad(jnp.concatenate([krc, kr_b], axis=1), ((0, 0), (0, skp - sk_valid), (0, 0)))
                tkv, tq, tk = skp, s, skp
            kk, vv = _kv_call(lat_all, kr_all, wp, tkv)
            qh = q.reshape(MLA_HEADS, b, s, LANES).transpose(1, 0, 2, 3) if flat else q
            attn = _attn_call(qh, kk, vv, tq, tk, past, sk_valid).reshape(bt, st, MLA_HEADS * MLA_V)
            nchk = s // S5_CHUNK
            r = b * nchk
            u2 = u.reshape(b * s, S5_WIDTH)
            tr = min(r, S5_ROW_BLOCK)
            wre, wim = _s5_w_call(u2, wp["bre"], wp["bim"], tr)
            ngp = S5_GROUPS * S5_STATE

            def to_pairs(v):
                return v.reshape(b, 1, ngp)

            if s5re_c is None:
                x0re = jnp.zeros((b, 1, ngp), F32)
                x0im = jnp.zeros((b, 1, ngp), F32)
            else:
                x0re, x0im = to_pairs(s5re_c[l]), to_pairs(s5im_c[l])
            nchp = (nchk + 7) // 8 * 8
            tc = min(nchp, S5_SCAN_BLOCK)
            wre3 = jnp.pad(wre.reshape(b, nchk, ngp), ((0, 0), (0, nchp - nchk), (0, 0)))
            wim3 = jnp.pad(wim.reshape(b, nchk, ngp), ((0, 0), (0, nchp - nchk), (0, 0)))
            xere, xeim = _s5_scan_call(wre3, wim3, wp["lre"], wp["lim"], x0re, x0im, tc)
            outs["s5re"] = xere[:, nchk - 1].reshape(b, S5_GROUPS, S5_STATE)
            outs["s5im"] = xeim[:, nchk - 1].reshape(b, S5_GROUPS, S5_STATE)
            xsre = jnp.concatenate([x0re, xere[:, :nchk - 1]], axis=1).reshape(r, ngp)
            xsim = jnp.concatenate([x0im, xeim[:, :nchk - 1]], axis=1).reshape(r, ngp)
            ys = _s5_y_call(u2, xsre, xsim, wp["mst"], wp["cc"], tr).reshape(bt, st, S5_WIDTH)
            xt, *routed = _ab_out_call(xt, ys, u, attn, wp, g1, rt, ts)
        else:
            wp = wc[l]
            qkv, z, ab = _c_in_call(xt, gain, sc1, sh1, wp["w_in"], ts)
            qkv_b = qkv.reshape(b, s, GDN_QKV)
            outs["conv"] = qkv_b[:, s - (CONV_W - 1):].astype(F32)
            if conv_c is None:
                st8 = jnp.zeros((b, 8, GDN_QKV), BF16)
                s0 = jnp.zeros((b, GDN_HEADS, GDN_DK, GDN_DV), F32)
            else:
                st8 = jnp.pad(conv_c[l], ((0, 0), (8 - (CONV_W - 1), 0), (0, 0))).astype(BF16)
                s0 = gdn_c[l]
            lg = min(s, CHUNK)
            wq, um, kgt, pm, ee = _gdn_prep_call(qkv_b, st8, wp["conv_w"], ab.reshape(b, s, LANES),
                                                 wp["alog"], wp["dtb"], lg)
            o, sfin = _gdn_seq_call(wq, um, kgt, pm, ee, s0, lg)
            outs["gdn"] = sfin
            xt, *routed = _c_out_call(xt, o.reshape(bt, st, GDN_VW), z, wp["gn"], wp["w_out"], g1, rt, ts)
        final = layer == depth - 1
        xt = _moe(xt, routed, g2, wmoe[layer], p["norm_final"].reshape(1, d), ts, final)
    y = xt.reshape(b, s, d)
    return (y, outs["lat"][None], outs["krope"][None], outs["s5re"][None], outs["s5im"][None],
            outs["conv"][None], outs["gdn"][None])


def kernel(x_prompt, x_sample, c_prompt, c_sample, cache_mla_latent, cache_mla_krope, state_s5_re, state_s5_im,
           state_conv, state_gdn, w_mod, b_mod, norm_mix, norm_ffn, norm_final, w_in_ab, s5_a_re, s5_a_im,
           s5_b_re, s5_b_im, s5_c_re, s5_c_im, s5_d, s5_log_dt, s5_glu_w, s5_glu_b, mla_q_norm, mla_q_up,
           mla_kv_norm, mla_kv_up, w_out_ab, w_in_c, conv_w, gdn_a_log, gdn_dt_bias, gdn_norm, w_out_c,
           moe_w_group, moe_b_group, moe_w_expert, moe_b_expert, moe_w1, moe_w3, moe_w2):
    p = dict(w_mod=w_mod, b_mod=b_mod, norm_mix=norm_mix, norm_ffn=norm_ffn, norm_final=norm_final,
             w_in_ab=w_in_ab, s5_a_re=s5_a_re, s5_a_im=s5_a_im, s5_b_re=s5_b_re, s5_b_im=s5_b_im,
             s5_c_re=s5_c_re, s5_c_im=s5_c_im, s5_d=s5_d, s5_log_dt=s5_log_dt, s5_glu_w=s5_glu_w,
             s5_glu_b=s5_glu_b, mla_q_norm=mla_q_norm, mla_q_up=mla_q_up, mla_kv_norm=mla_kv_norm,
             mla_kv_up=mla_kv_up, w_out_ab=w_out_ab, w_in_c=w_in_c, conv_w=conv_w, gdn_a_log=gdn_a_log,
             gdn_dt_bias=gdn_dt_bias, gdn_norm=gdn_norm, w_out_c=w_out_c, moe_w_group=moe_w_group,
             moe_b_group=moe_b_group, moe_w_expert=moe_w_expert, moe_b_expert=moe_b_expert,
             moe_w1=moe_w1, moe_w3=moe_w3, moe_w2=moe_w2)
    depth = norm_mix.shape[0]
    bp, bs = c_prompt.shape[0], c_sample.shape[0]
    nb = (bp + bs + 7) // 8 * 8
    c_all = jnp.pad(jnp.concatenate([c_prompt, c_sample], axis=0), ((0, nb - bp - bs), (0, 0)))
    mods = _mod_call(c_all, w_mod, b_mod)
    wab = [_prep_ab(l, p) for l in range((depth + 1) // 2)]
    wc = [_prep_c(l, p) for l in range(depth // 2)]
    wmoe = [_prep_moe(layer, p) for layer in range(depth)]
    none6 = (None,) * 6
    outp = _trunk(x_prompt, mods[:, :bp], none6, wab, wc, wmoe, p, flat=False)
    caches = (cache_mla_latent, cache_mla_krope, state_s5_re, state_s5_im, state_conv, state_gdn)
    outs = _trunk(x_sample, mods[:, bp:bp + bs], caches, wab, wc, wmoe, p, flat=True)
    return (outp[0], outs[0]) + tuple(outp[1:]) + tuple(outs[1:])
```

```python
import functools
import math

import jax
import jax.numpy as jnp
from jax import lax
from jax.experimental import pallas as pl
from jax.experimental.pallas import tpu as pltpu

F32 = jnp.float32
BF16 = jnp.bfloat16
HI = lax.Precision.HIGHEST
EPS = 1e-6

D_MODEL = 1024
CHUNK = 64
S5_WIDTH = 512
S5_GROUP = 16
S5_GROUPS = 32
S5_STATE = 64
S5_CHUNK = 16
S5_LB = 128 // S5_GROUP
MLA_HEADS = 8
MLA_NOPE = 64
MLA_ROPE = 32
MLA_V = 64
MLA_Q_RANK = 384
MLA_KV_RANK = 256
ROPE_THETA = 10000.0
GDN_HEADS = 8
GDN_DK = 128
GDN_DV = 128
CONV_W = 4
GDN_QKV = GDN_HEADS * (2 * GDN_DK + GDN_DV)
GDN_VW = GDN_HEADS * GDN_DV
N_GROUPS = 4
EXPERTS_PER_GROUP = 8
N_EXPERTS = 32
D_EXPERT = 512
TOP_K = 2
MOE_BLOCK = 512
MOE_BLOCK_SMALL = 128
ATTN_SUB = 512
ATTN_TQ = 512
ATTN_TK = 512
TOKEN_BLOCK = 256
S5_ROW_BLOCK = 256
S5_SCAN_BLOCK = 128
LANES = 128
VMEM_LIMIT = 48 * 1024 * 1024


def _cparams(*sem):
    return pltpu.CompilerParams(dimension_semantics=sem, vmem_limit_bytes=VMEM_LIMIT)


def _dot(a, b):
    return jnp.dot(a, b, preferred_element_type=F32)


def _dot_nt(a, b, precision=None):
    return lax.dot_general(a, b, (((1,), (1,)), ((), ())), preferred_element_type=F32, precision=precision)


def _full(arr):
    nd = arr.ndim
    return pl.BlockSpec(arr.shape, lambda *_: (0,) * nd)


def _tok(width, ts):
    return pl.BlockSpec((1, ts, width), lambda b, j: (b, j, 0))


def _modspec(arr, ts):
    if arr.shape[1] == 1:
        return pl.BlockSpec((1, 1, arr.shape[2]), lambda b, j: (b, 0, 0))
    return pl.BlockSpec((1, ts, arr.shape[2]), lambda b, j: (b, j, 0))


def _normmod(x, gain, sc, sh):
    ms = jnp.mean(x * x, axis=-1, keepdims=True)
    return x * lax.rsqrt(ms + EPS) * gain * (1.0 + sc) + sh


def _rms(x, gain):
    return x * lax.rsqrt(jnp.mean(x * x, axis=-1, keepdims=True) + EPS) * gain


def _mod_kernel(c_ref, w_ref, b_ref, o_ref):
    c = c_ref[...]
    a = (c * jax.nn.sigmoid(c)).astype(BF16)
    o_ref[0] = _dot(a, w_ref[0]) + b_ref[0]


def _mod_call(c_all, w_mod, b_mod):
    depth, d, n6 = w_mod.shape
    bp = c_all.shape[0]
    tn = 1536
    return pl.pallas_call(
        _mod_kernel,
        grid=(depth, n6 // tn),
        in_specs=[pl.BlockSpec((bp, d), lambda l, n: (0, 0)),
                  pl.BlockSpec((1, d, tn), lambda l, n: (l, 0, n)),
                  pl.BlockSpec((1, 1, tn), lambda l, n: (l, 0, n))],
        out_specs=pl.BlockSpec((1, bp, tn), lambda l, n: (l, 0, n)),
        out_shape=jax.ShapeDtypeStruct((depth, bp, n6), F32),
        compiler_params=_cparams("parallel", "parallel"),
        name="mod",
    )(c_all, w_mod.astype(BF16), b_mod.reshape(depth, 1, n6))


def _ab_in_kernel(x_ref, gain_ref, sc_ref, sh_ref, w_ref, qg_ref, wqa_ref, wqb_ref, kvg_ref,
                  cq_ref, sq_ref, ck_ref, sk_ref, u_ref, q_ref, lat_ref, kr_ref):
    h = _normmod(x_ref[0], gain_ref[...], sc_ref[0], sh_ref[0]).astype(BF16)
    proj = _dot(h, w_ref[...])
    u_ref[0] = proj[:, :S5_WIDTH]
    o1 = S5_WIDTH + MLA_Q_RANK
    o2 = o1 + MLA_KV_RANK
    qn = _rms(proj[:, S5_WIDTH:o1], qg_ref[...]).astype(BF16)
    qa = _dot(qn, wqa_ref[...])
    qb = _dot(qn, wqb_ref[...])
    cq = cq_ref[...]
    sq = sq_ref[...]
    for hh in range(MLA_HEADS):
        sl = slice(LANES * hh, LANES * (hh + 1))
        q_ref[0, hh] = (qa[:, sl] * cq + qb[:, sl] * sq).astype(BF16)
    lat_ref[0] = _rms(proj[:, o1:o2], kvg_ref[...])
    kr_ref[0] = proj[:, o2:o2 + LANES] * ck_ref[...] + proj[:, o2 + LANES:o2 + 2 * LANES] * sk_ref[...]


def _ab_in_call(x, gain, sc, sh, wp, tabs, ts):
    b, s, d = x.shape
    cq, sq, ck, sk = tabs
    tab = pl.BlockSpec((ts, LANES), lambda bb, j: (j, 0))
    ins = [x, gain, sc, sh, wp["w_in"], wp["q_norm"], wp["wqa"], wp["wqb"], wp["kv_norm"], cq, sq, ck, sk]
    specs = [_tok(d, ts), _full(gain), _modspec(sc, ts), _modspec(sh, ts), _full(wp["w_in"]), _full(wp["q_norm"]),
             _full(wp["wqa"]), _full(wp["wqb"]), _full(wp["kv_norm"]), tab, tab, tab, tab]
    return pl.pallas_call(
        _ab_in_kernel,
        grid=(b, s // ts),
        in_specs=specs,
        out_specs=[_tok(S5_WIDTH, ts),
                   pl.BlockSpec((1, MLA_HEADS, ts, LANES), lambda bb, j: (bb, 0, j, 0)),
                   _tok(MLA_KV_RANK, ts), _tok(LANES, ts)],
        out_shape=[jax.ShapeDtypeStruct((b, s, S5_WIDTH), F32),
                   jax.ShapeDtypeStruct((b, MLA_HEADS, s, LANES), BF16),
                   jax.ShapeDtypeStruct((b, s, MLA_KV_RANK), F32),
                   jax.ShapeDtypeStruct((b, s, LANES), F32)],
        compiler_params=_cparams("parallel", "parallel"),
        name="ab_in",
    )(*ins)


def _kv_kernel(lat_ref, kr_ref, wk_ref, wv_ref, k_ref, v_ref):
    lat = lat_ref[0].astype(BF16)
    kk = _dot(lat, wk_ref[...])
    vv = _dot(lat, wv_ref[...])
    kr = kr_ref[0]
    ones = (lax.broadcasted_iota(jnp.int32, (1, LANES), 1) >= MLA_V).astype(F32)
    for hh in range(MLA_HEADS):
        k_ref[0, hh] = (kk[:, LANES * hh:LANES * (hh + 1)] + kr).astype(BF16)
        v_ref[0, hh] = (vv[:, LANES * hh:LANES * (hh + 1)] + ones).astype(BF16)


def _kv_call(lat_all, kr_all, wp, ts):
    b, sk, _ = lat_all.shape
    hspec = pl.BlockSpec((1, MLA_HEADS, ts, LANES), lambda bb, j: (bb, 0, j, 0))
    hshape = jax.ShapeDtypeStruct((b, MLA_HEADS, sk, LANES), BF16)
    return pl.pallas_call(
        _kv_kernel,
        grid=(b, sk // ts),
        in_specs=[_tok(MLA_KV_RANK, ts), _tok(LANES, ts), _full(wp["wk"]), _full(wp["wv"])],
        out_specs=[hspec, hspec],
        out_shape=[hshape, hshape],
        compiler_params=_cparams("parallel", "parallel"),
        name="kv_up",
    )(lat_all, kr_all, wp["wk"], wp["wv"])


def _attn_kernel(q_ref, k_ref, v_ref, *rest, tq, tk, q_off, sk_valid):
    o_ref = rest[-1]
    bias_ref = rest[0] if len(rest) == 2 else None
    i = pl.program_id(2)
    nsub = max(tq // ATTN_SUB, 1)
    tqs = tq // nsub
    chains = [(hh, qi) for hh in range(2) for qi in range(nsub)]
    qs = [q_ref[0, hh, qi * tqs:(qi + 1) * tqs, :] for hh, qi in chains]
    q0 = q_off + i * tq
    lim_full = (q0 // CHUNK + 1) * CHUNK
    lim_tot = jnp.minimum(((q0 + tq - 1) // CHUNK + 1) * CHUNK, sk_valid)
    n_full = jnp.minimum(lim_full // tk, sk_valid // tk)
    n_tot = (lim_tot + tk - 1) // tk

    def step(j, carry, masked):
        off = pl.multiple_of(j * tk, tk)
        nc = range(len(chains))
        if masked and bias_ref is None:
            kpos = off + lax.broadcasted_iota(jnp.int32, (tqs, tk), 1)
            ok = []
            for qi in range(nsub):
                qpos = q0 + qi * tqs + lax.broadcasted_iota(jnp.int32, (tqs, tk), 0)
                ok.append(jnp.logical_and(kpos // CHUNK <= qpos // CHUNK, kpos < sk_valid))
        kt = [k_ref[0, hh, pl.ds(off, tk), :] for hh in range(2)]
        vt = [v_ref[0, hh, pl.ds(off, tk), :] for hh in range(2)]
        s = [_dot_nt(qs[c], kt[chains[c][0]]) for c in nc]
        if masked and bias_ref is None:
            s = [jnp.where(ok[chains[c][1]], s[c], -1e30) for c in nc]
        elif masked:
            s = [s[c] + bias_ref[chains[c][1] * tqs:(chains[c][1] + 1) * tqs, :] for c in nc]
        m_new = [jnp.maximum(carry[c][0], jnp.max(s[c], axis=-1, keepdims=True)) for c in nc]
        alpha = [jnp.exp2(carry[c][0] - m_new[c]) for c in nc]
        p = [jnp.exp2(s[c] - m_new[c]).astype(BF16) for c in nc]
        acc = [alpha[c] * carry[c][1] + _dot(p[c], vt[chains[c][0]]) for c in nc]
        return tuple((m_new[c], acc[c]) for c in nc)

    one = (jnp.full((tqs, 1), -1e30, F32), jnp.zeros((tqs, LANES), F32))
    carry = lax.fori_loop(0, n_full, functools.partial(step, masked=False), (one,) * len(chains))
    carry = lax.fori_loop(n_full, n_tot, functools.partial(step, masked=True), carry)
    lane = lax.broadcasted_iota(jnp.int32, (tqs, LANES), 1)
    for qi in range(nsub):
        a0 = carry[chains.index((0, qi))][1]
        a1 = carry[chains.index((1, qi))][1]
        r0 = a0 / pltpu.roll(a0, MLA_V, axis=1)
        r1 = a1 / pltpu.roll(a1, MLA_V, axis=1)
        o_ref[0, qi * tqs:(qi + 1) * tqs, :] = jnp.where(
            lane < MLA_V, r0, pltpu.roll(r1, MLA_V, axis=1)).astype(o_ref.dtype)


def _attn_call(q, k, v, tq, tk, q_off, sk_valid):
    b, h, sq, _ = q.shape
    sk = k.shape[2]
    kern = functools.partial(_attn_kernel, tq=tq, tk=tk, q_off=q_off, sk_valid=sk_valid)
    kv_mode = pl.Buffered(2 if sq == tq else 1)
    ins = [q, k, v]
    specs = [pl.BlockSpec((1, 2, tq, LANES), lambda bb, hp, i: (bb, hp, i, 0)),
             pl.BlockSpec((1, 2, sk, LANES), lambda bb, hp, i: (bb, hp, 0, 0), pipeline_mode=kv_mode),
             pl.BlockSpec((1, 2, sk, LANES), lambda bb, hp, i: (bb, hp, 0, 0), pipeline_mode=kv_mode)]
    if q_off == 0 and tq == tk and sk_valid % tk == 0:
        r = jnp.arange(tq, dtype=jnp.int32)
        bias = jnp.where(r[None, :] // CHUNK <= r[:, None] // CHUNK, 0.0, -1e30).astype(F32)
        ins.append(bias)
        specs.append(pl.BlockSpec((tq, tk), lambda bb, hp, i: (0, 0)))
    return pl.pallas_call(
        kern,
        grid=(b, h // 2, sq // tq),
        in_specs=specs,
        out_specs=pl.BlockSpec((1, tq, LANES), lambda bb, hp, i: (bb, i, hp)),
        out_shape=jax.ShapeDtypeStruct((b, sq, h * MLA_V), BF16),
        compiler_params=_cparams("parallel", "parallel", "arbitrary"),
        name="mla_attn",
    )(*ins)


def _chunk_rows(ref, s, tr):
    return ref[pl.ds(s, tr, stride=S5_CHUNK), :]


def _s5_w_kernel(u_ref, bre_ref, bim_ref, wre_ref, wim_ref, *, tr):
    ucat = jnp.concatenate([_chunk_rows(u_ref, s, tr).astype(BF16) for s in range(S5_CHUNK)], axis=1)
    wre_ref[...] = _dot(ucat, bre_ref[0])
    wim_ref[...] = _dot(ucat, bim_ref[0])


def _s5_w_call(u2, bre, bim, tr):
    n = u2.shape[0]
    r = n // S5_CHUNK
    nq = bre.shape[0]
    sw = S5_LB * S5_STATE
    out = jax.ShapeDtypeStruct((r, nq * sw), F32)
    ospec = pl.BlockSpec((tr, sw), lambda q, i: (i, q))
    wspec = pl.BlockSpec((1, S5_CHUNK * LANES, sw), lambda q, i: (q, 0, 0))
    return pl.pallas_call(
        functools.partial(_s5_w_kernel, tr=tr),
        grid=(nq, r // tr),
        in_specs=[pl.BlockSpec((tr * S5_CHUNK, LANES), lambda q, i: (i, q)), wspec, wspec],
        out_specs=[ospec, ospec],
        out_shape=[out, out],
        compiler_params=_cparams("parallel", "parallel"),
        name="s5_chunk_in",
    )(u2, bre, bim)


def _s5_scan_kernel(wre_ref, wim_ref, lre_ref, lim_ref, x0re_ref, x0im_ref, ore_ref, oim_ref, sre, sim, *, tc):
    @pl.when(pl.program_id(1) == 0)
    def _():
        sre[...] = x0re_ref[0]
        sim[...] = x0im_ref[0]

    lr = lre_ref[...]
    li = lim_ref[...]

    def blk(t, carry):
        xr, xi = carry
        base = pl.multiple_of(t * 8, 8)
        wr = wre_ref[0, pl.ds(base, 8), :]
        wi = wim_ref[0, pl.ds(base, 8), :]
        rows_r, rows_i = [], []
        for r in range(8):
            nr = lr * xr - li * xi + wr[r:r + 1]
            ni = lr * xi + li * xr + wi[r:r + 1]
            xr, xi = nr, ni
            rows_r.append(xr)
            rows_i.append(xi)
        ore_ref[0, pl.ds(base, 8), :] = jnp.concatenate(rows_r, axis=0)
        oim_ref[0, pl.ds(base, 8), :] = jnp.concatenate(rows_i, axis=0)
        return xr, xi

    xr, xi = lax.fori_loop(0, tc // 8, blk, (sre[...], sim[...]))
    sre[...] = xr
    sim[...] = xi


def _s5_scan_call(wre, wim, lre, lim, x0re, x0im, tc):
    b, nch, n = wre.shape
    wspec = pl.BlockSpec((1, tc, n), lambda bb, c: (bb, c, 0))
    lspec = pl.BlockSpec((1, n), lambda bb, c: (0, 0))
    xspec = pl.BlockSpec((1, 1, n), lambda bb, c: (bb, 0, 0))
    out = jax.ShapeDtypeStruct((b, nch, n), F32)
    return pl.pallas_call(
        functools.partial(_s5_scan_kernel, tc=tc),
        grid=(b, nch // tc),
        in_specs=[wspec, wspec, lspec, lspec, xspec, xspec],
        out_specs=[wspec, wspec],
        out_shape=[out, out],
        scratch_shapes=[pltpu.VMEM((1, n), F32), pltpu.VMEM((1, n), F32)],
        compiler_params=_cparams("parallel", "arbitrary"),
        name="s5_scan",
    )(wre, wim, lre, lim, x0re, x0im)


def _s5_y_kernel(u_ref, xre_ref, xim_ref, m_ref, c_ref, y_ref, *, tr):
    L = S5_CHUNK
    urev = jnp.concatenate([_chunk_rows(u_ref, s, tr).astype(BF16) for s in reversed(range(L))], axis=1)
    xcat = jnp.concatenate([xre_ref[...].astype(BF16), xim_ref[...].astype(BF16)], axis=1)
    for t in range(L):
        y = _dot(urev[:, (L - 1 - t) * LANES:], m_ref[0, :(t + 1) * LANES, :]) + _dot(xcat, c_ref[0, t])
        y_ref[pl.ds(t, tr, stride=L), :] = y


def _s5_y_call(u2, xre, xim, mst, cc, tr):
    n, w = u2.shape
    nq = mst.shape[0]
    sw = S5_LB * S5_STATE
    uspec = pl.BlockSpec((tr * S5_CHUNK, LANES), lambda q, i: (i, q))
    xspec = pl.BlockSpec((tr, sw), lambda q, i: (i, q))
    return pl.pallas_call(
        functools.partial(_s5_y_kernel, tr=tr),
        grid=(nq, n // (tr * S5_CHUNK)),
        in_specs=[uspec, xspec, xspec,
                  pl.BlockSpec((1, S5_CHUNK * LANES, LANES), lambda q, i: (q, 0, 0)),
                  pl.BlockSpec((1, S5_CHUNK, 2 * sw, LANES), lambda q, i: (q, 0, 0, 0))],
        out_specs=uspec,
        out_shape=jax.ShapeDtypeStruct((n, w), F32),
        compiler_params=_cparams("parallel", "parallel"),
        name="s5_chunk_out",
    )(u2, xre, xim, mst, cc)


def _s5_mats(a_re, a_im, b_re, b_im, c_re, c_im, log_dt):
    g, p = a_re.shape
    L = S5_CHUNK
    lam = lax.complex(jnp.minimum(a_re, -1e-4), a_im)
    lamdt = lam * jnp.exp(log_dt)[:, None]
    lam_bar = jnp.exp(lamdt)
    b_bar = ((lam_bar - 1.0) / lam)[:, :, None] * lax.complex(b_re, b_im)
    cm = lax.complex(c_re, c_im)
    pw = jnp.exp(lamdt[:, :, None] * jnp.arange(L + 1, dtype=F32))
    mm = pw[:, :, :L, None] * b_bar[:, :, None, :]
    kd = jnp.sum((cm[:, :, :, None, None] * mm[:, None]).real, axis=2).transpose(0, 2, 1, 3)
    bp = pw[:, :, L - 1 - jnp.arange(L)][:, :, :, None] * b_bar[:, :, None, :]
    cp = cm[:, :, :, None] * pw[:, None, :, 1:]
    nq = g // S5_LB
    eye = jnp.eye(S5_LB, dtype=F32)

    def lane_block(m, spec, rows, cols):
        return jnp.einsum(spec, m.reshape((nq, S5_LB) + m.shape[1:]), eye).reshape(nq, rows, cols)

    sw = S5_LB * p
    bre = lane_block(bp.real, "qgpsc,hg->qshcgp", L * LANES, sw)
    bim = lane_block(bp.imag, "qgpsc,hg->qshcgp", L * LANES, sw)
    mst = lane_block(kd, "qgdoc,hg->qdhcgo", L * LANES, LANES)
    cre = lane_block(cp.real, "qgopt,hg->qthpgo", L * sw, LANES).reshape(nq, L, sw, LANES)
    cim = lane_block(-cp.imag, "qgopt,hg->qthpgo", L * sw, LANES).reshape(nq, L, sw, LANES)
    return dict(
        bre=bre.astype(BF16), bim=bim.astype(BF16), mst=mst.astype(BF16),
        cc=jnp.concatenate([cre, cim], axis=2).astype(BF16),
        lre=pw[:, :, L].real.reshape(1, g * p), lim=pw[:, :, L].imag.reshape(1, g * p))


def _ab_out_kernel(x_ref, ys_ref, u_ref, at_ref, d_ref, gw_ref, gb_ref, ws_ref, wa_ref, g1_ref, *route_refs):
    o_ref = route_refs[5]
    y = ys_ref[0] + d_ref[...] * u_ref[0]
    y = jax.nn.gelu(y)
    gate = jax.nn.sigmoid(_dot(y.astype(BF16), gw_ref[...]) + gb_ref[...])
    out = _dot((y * gate).astype(BF16), ws_ref[...]) + _dot(at_ref[0], wa_ref[...])
    xn = x_ref[0] + g1_ref[0] * out
    o_ref[0] = xn
    _route(xn, *route_refs[:5], *route_refs[6:])


def _ab_out_call(x, ys, u, attn, wp, g1, rt, ts):
    b, s, d = x.shape
    r_ins, r_specs, r_ospecs, r_oshape, r_scratch = _route_plumbing(rt, b, s, d, ts)
    ins = [x, ys, u, attn, wp["s5_d"], wp["glu_w"], wp["glu_b"], wp["w_out_s5"], wp["w_out_at"], g1] + r_ins
    specs = [_tok(d, ts), _tok(S5_WIDTH, ts), _tok(S5_WIDTH, ts), _tok(MLA_HEADS * MLA_V, ts),
             _full(wp["s5_d"]), _full(wp["glu_w"]), _full(wp["glu_b"]), _full(wp["w_out_s5"]),
             _full(wp["w_out_at"]), _modspec(g1, ts)] + r_specs
    return pl.pallas_call(
        _ab_out_kernel,
        grid=(b, s // ts),
        in_specs=specs,
        out_specs=[_tok(d, ts)] + r_ospecs,
        out_shape=[jax.ShapeDtypeStruct((b, s, d), F32)] + r_oshape,
        scratch_shapes=r_scratch,
        compiler_params=_cparams("arbitrary", "arbitrary"),
        name="ab_out",
    )(*ins)


def _c_in_kernel(x_ref, gain_ref, sc_ref, sh_ref, w_ref, qkv_ref, z_ref, ab_ref):
    h = _normmod(x_ref[0], gain_ref[...], sc_ref[0], sh_ref[0]).astype(BF16)
    proj = _dot(h, w_ref[...])
    qkv_ref[0] = proj[:, :GDN_QKV].astype(BF16)
    z_ref[0] = proj[:, GDN_QKV:GDN_QKV + GDN_VW].astype(BF16)
    ab_ref[0] = proj[:, GDN_QKV + GDN_VW:]


def _c_in_call(x, gain, sc, sh, w, ts):
    b, s, d = x.shape
    return pl.pallas_call(
        _c_in_kernel,
        grid=(b, s // ts),
        in_specs=[_tok(d, ts), _full(gain), _modspec(sc, ts), _modspec(sh, ts), _full(w)],
        out_specs=[_tok(GDN_QKV, ts), _tok(GDN_VW, ts), _tok(LANES, ts)],
        out_shape=[jax.ShapeDtypeStruct((b, s, GDN_QKV), BF16),
                   jax.ShapeDtypeStruct((b, s, GDN_VW), BF16),
                   jax.ShapeDtypeStruct((b, s, LANES), F32)],
        compiler_params=_cparams("parallel", "parallel"),
        name="c_in",
    )(x, gain, sc, sh, w)


def _gdn_prep_kernel(qkv_ref, halo_ref, st_ref, cw_ref, ab_ref, alog_ref, dtb_ref,
                     wq_ref, u_ref, kgt_ref, pm_ref, ee_ref, *, L, nc):
    j = pl.program_id(1)
    rows = nc * L
    prev = jnp.where(j == 0, st_ref[0].astype(F32), halo_ref[0].astype(F32))
    xx = jnp.concatenate([prev, qkv_ref[0].astype(F32)], axis=0)
    cw = cw_ref[...]
    y = xx[8:8 + rows] * cw[CONV_W - 1:CONV_W]
    for t in range(1, CONV_W):
        y += xx[8 - t:8 - t + rows] * cw[CONV_W - 1 - t:CONV_W - t]
    y = y * jax.nn.sigmoid(y)

    ab = ab_ref[0]
    g_all = -jnp.exp(alog_ref[...]) * jax.nn.softplus(ab + dtb_ref[...])
    beta_all = jax.nn.sigmoid(ab)
    row = lax.broadcasted_iota(jnp.int32, (L, L), 0)
    col = lax.broadcasted_iota(jnp.int32, (L, L), 1)
    incl = row >= col
    strict = row > col
    brow = lax.broadcasted_iota(jnp.int32, (rows, rows), 0)
    bcol = lax.broadcasted_iota(jnp.int32, (rows, rows), 1)
    incl_blk = jnp.logical_and(brow >= bcol, brow // L == bcol // L)
    gam_all = jnp.dot(incl_blk.astype(F32), g_all, preferred_element_type=F32, precision=HI)
    eye_l = (lax.broadcasted_iota(jnp.int32, (LANES, LANES), 0)
             == lax.broadcasted_iota(jnp.int32, (LANES, LANES), 1))
    gam_t = _dot_nt(eye_l.astype(F32), gam_all, precision=HI)
    for ci in range(nc):
        ee_ref[0, ci] = jnp.exp(gam_all[(ci + 1) * L - 1:(ci + 1) * L])
    eye64 = (row == col).astype(F32)
    eye_bf = eye_l.astype(BF16)

    ch = [(ci, h) for ci in range(nc) for h in range(GDN_HEADS)]
    cs = range(len(ch))
    koff, voff = GDN_HEADS * GDN_DK, 2 * GDN_HEADS * GDN_DK
    qh = [y[ci * L:(ci + 1) * L, GDN_DK * h:GDN_DK * (h + 1)] for ci, h in ch]
    kh = [y[ci * L:(ci + 1) * L, koff + GDN_DK * h:koff + GDN_DK * (h + 1)] for ci, h in ch]
    vh = [y[ci * L:(ci + 1) * L, voff + GDN_DV * h:voff + GDN_DV * (h + 1)] for ci, h in ch]
    qh = [q * lax.rsqrt(jnp.sum(q * q, axis=-1, keepdims=True) + EPS) * (GDN_DK ** -0.5) for q in qh]
    kh = [k * lax.rsqrt(jnp.sum(k * k, axis=-1, keepdims=True) + EPS) for k in kh]
    gam_c = [gam_all[ci * L:(ci + 1) * L, h:h + 1] for ci, h in ch]
    beta_c = [beta_all[ci * L:(ci + 1) * L, GDN_HEADS + h:GDN_HEADS + h + 1] for ci, h in ch]
    dec = [jnp.exp(jnp.minimum(gam_c[c] - gam_t[h:h + 1, ci * L:(ci + 1) * L], 0.0))
           for c, (ci, h) in enumerate(ch)]
    kb = [k.astype(BF16) for k in kh]
    kk = [_dot_nt(kb[c], kb[c]) for c in cs]
    qk = [_dot_nt(qh[c].astype(BF16), kb[c]) for c in cs]
    a = [jnp.where(strict, beta_c[c] * kk[c] * dec[c], 0.0) for c in cs]
    for c, (ci, h) in enumerate(ch):
        pm_ref[0, ci, h] = jnp.where(incl, qk[c] * dec[c], 0.0).astype(BF16)
    tinv = [eye64 - a[c] for c in cs]
    pw = [_split(a[c]) for c in cs]
    for _ in range(max(L.bit_length() - 2, 0)):
        pw = [_split(_dot3(pw[c], pw[c])) for c in cs]
        tinv = [tinv[c] + _dot3(_split(tinv[c]), pw[c]) for c in cs]
    eg = [jnp.exp(gam_c[c]) for c in cs]
    rhs = [jnp.concatenate([(beta_c[c] * eg[c]) * kh[c], beta_c[c] * vh[c]], axis=1).astype(BF16) for c in cs]
    wu = [_dot(tinv[c].astype(BF16), rhs[c]) for c in cs]
    for c, (ci, h) in enumerate(ch):
        wq_ref[0, ci, h] = jnp.concatenate([wu[c][:, :GDN_DK], qh[c] * eg[c]], axis=0).astype(BF16)
        u_ref[0, h, ci * L:(ci + 1) * L, :] = wu[c][:, GDN_DK:].astype(BF16)
    kg = [(kh[c] * jnp.exp(gam_all[(ci + 1) * L - 1:(ci + 1) * L, h:h + 1] - gam_c[c])).astype(BF16)
          for c, (ci, h) in enumerate(ch)]
    kgt = [_dot_nt(eye_bf, kg[c]) for c in cs]
    for c, (ci, h) in enumerate(ch):
        kgt_ref[0, ci, h] = kgt[c].astype(BF16)


def _split(a):
    hi = a.astype(BF16)
    return hi, (a - hi.astype(F32)).astype(BF16)


def _dot3(a, b):
    return _dot(a[0], b[0]) + (_dot(a[0], b[1]) + _dot(a[1], b[0]))


GDN_PREP_CHUNKS = 2


def _gdn_prep_call(qkv, st8, conv_w, ab, alog, dtb, L):
    b, s, _ = qkv.shape
    nch = s // L
    nc = GDN_PREP_CHUNKS if nch % GDN_PREP_CHUNKS == 0 else 1
    rows = nc * L
    return pl.pallas_call(
        functools.partial(_gdn_prep_kernel, L=L, nc=nc),
        grid=(b, nch // nc),
        in_specs=[_tok(GDN_QKV, rows),
                  pl.BlockSpec((1, 8, GDN_QKV), lambda bb, j: (bb, jnp.maximum(j * (rows // 8) - 1, 0), 0)),
                  pl.BlockSpec((1, 8, GDN_QKV), lambda bb, j: (bb, 0, 0)),
                  _full(conv_w), _tok(LANES, rows), _full(alog), _full(dtb)],
        out_specs=[pl.BlockSpec((1, nc, GDN_HEADS, 2 * L, LANES), lambda bb, j: (bb, j, 0, 0, 0)),
                   pl.BlockSpec((1, GDN_HEADS, rows, LANES), lambda bb, j: (bb, 0, j, 0)),
                   pl.BlockSpec((1, nc, GDN_HEADS, GDN_DK, L), lambda bb, j: (bb, j, 0, 0, 0)),
                   pl.BlockSpec((1, nc, GDN_HEADS, L, L), lambda bb, j: (bb, j, 0, 0, 0)),
                   pl.BlockSpec((1, nc, 1, LANES), lambda bb, j: (bb, j, 0, 0))],
        out_shape=[jax.ShapeDtypeStruct((b, nch, GDN_HEADS, 2 * L, LANES), BF16),
                   jax.ShapeDtypeStruct((b, GDN_HEADS, s, LANES), BF16),
                   jax.ShapeDtypeStruct((b, nch, GDN_HEADS, GDN_DK, L), BF16),
                   jax.ShapeDtypeStruct((b, nch, GDN_HEADS, L, L), BF16),
                   jax.ShapeDtypeStruct((b, nch, 1, LANES), F32)],
        compiler_params=_cparams("parallel", "parallel"),
        name="gdn_prep",
    )(qkv, qkv, st8, conv_w, ab, alog, dtb)


GDN_BATCH_GROUP = 2


def _gdn_seq_kernel(wq_ref, u_ref, kgt_ref, pm_ref, ee_ref, s0_ref, o_ref, sf_ref, s_scr, *, L):
    c = pl.program_id(1)

    @pl.when(c == 0)
    def _():
        s_scr[...] = s0_ref[...]

    ch = [(bb, hh) for bb in range(GDN_BATCH_GROUP) for hh in range(GDN_HEADS)]
    st = [s_scr[bb, hh] for bb, hh in ch]
    sp = [_split(s) for s in st]
    r = [_dot(wq_ref[bb, 0, hh], sp[i][0]) + _dot(wq_ref[bb, 0, hh], sp[i][1]) for i, (bb, hh) in enumerate(ch)]
    v_new = [(u_ref[bb, hh].astype(F32) - r[i][:L]).astype(BF16) for i, (bb, hh) in enumerate(ch)]
    o = [r[i][L:] + _dot(pm_ref[bb, 0, hh], v_new[i]) for i, (bb, hh) in enumerate(ch)]
    for i, (bb, hh) in enumerate(ch):
        o_ref[bb, :, GDN_DV * hh:GDN_DV * (hh + 1)] = o[i].astype(o_ref.dtype)
    s_new = [ee_ref[bb, 0][:, hh:hh + 1] * st[i] + _dot(kgt_ref[bb, 0, hh], v_new[i]) for i, (bb, hh) in enumerate(ch)]
    for i, (bb, hh) in enumerate(ch):
        s_scr[bb, hh] = s_new[i]

    @pl.when(c == pl.num_programs(1) - 1)
    def _():
        sf_ref[...] = s_scr[...]


def _gdn_seq_call(wq, um, kgt, pm, ee, s0, L):
    b, h, s, _ = um.shape
    nch = s // L
    bg = GDN_BATCH_GROUP
    sspec = pl.BlockSpec((bg, h, GDN_DK, GDN_DV), lambda g, c: (g, 0, 0, 0))
    return pl.pallas_call(
        functools.partial(_gdn_seq_kernel, L=L),
        grid=(b // bg, nch),
        in_specs=[pl.BlockSpec((bg, 1, h, 2 * L, LANES), lambda g, c: (g, c, 0, 0, 0)),
                  pl.BlockSpec((bg, h, L, LANES), lambda g, c: (g, 0, c, 0)),
                  pl.BlockSpec((bg, 1, h, GDN_DK, L), lambda g, c: (g, c, 0, 0, 0)),
                  pl.BlockSpec((bg, 1, h, L, L), lambda g, c: (g, c, 0, 0, 0)),
                  pl.BlockSpec((bg, 1, 1, LANES), lambda g, c: (g, c, 0, 0)),
                  sspec],
        out_specs=[pl.BlockSpec((bg, L, GDN_VW), lambda g, c: (g, c, 0)), sspec],
        out_shape=[jax.ShapeDtypeStruct((b, s, GDN_VW), BF16),
                   jax.ShapeDtypeStruct((b, h, GDN_DK, GDN_DV), F32)],
        scratch_shapes=[pltpu.VMEM((bg, h, GDN_DK, GDN_DV), F32)],
        compiler_params=_cparams("parallel", "arbitrary"),
        name="gdn_seq",
    )(wq, um, kgt, pm, ee, s0)


def _c_out_kernel(x_ref, o_ref, z_ref, gn_ref, w_ref, g1_ref, *route_refs):
    out_ref = route_refs[5]
    o = o_ref[0].astype(F32)
    z = z_ref[0].astype(F32)
    gn = gn_ref[...]
    parts = []
    for hh in range(GDN_HEADS):
        oh = o[:, GDN_DV * hh:GDN_DV * (hh + 1)]
        zh = z[:, GDN_DV * hh:GDN_DV * (hh + 1)]
        parts.append((_rms(oh, gn) * (zh * jax.nn.sigmoid(zh))).astype(BF16))
    out = _dot(jnp.concatenate(parts, axis=1), w_ref[...])
    xn = x_ref[0] + g1_ref[0] * out
    out_ref[0] = xn
    _route(xn, *route_refs[:5], *route_refs[6:])


def _c_out_call(x, o, z, gn, w, g1, rt, ts):
    b, s, d = x.shape
    r_ins, r_specs, r_ospecs, r_oshape, r_scratch = _route_plumbing(rt, b, s, d, ts)
    return pl.pallas_call(
        _c_out_kernel,
        grid=(b, s // ts),
        in_specs=[_tok(d, ts), _tok(GDN_VW, ts), _tok(GDN_VW, ts), _full(gn), _full(w), _modspec(g1, ts)] + r_specs,
        out_specs=[_tok(d, ts)] + r_ospecs,
        out_shape=[jax.ShapeDtypeStruct((b, s, d), F32)] + r_oshape,
        scratch_shapes=r_scratch,
        compiler_params=_cparams("arbitrary", "arbitrary"),
        name="c_out",
    )(x, o, z, gn, w, g1, *r_ins)


def _pack_halves(v):
    n = v.shape[1] // 2
    bits = pltpu.bitcast(v.astype(BF16).astype(F32), jnp.uint32)
    return (bits[:, :n] >> 16) | (bits[:, n:] & jnp.uint32(0xFFFF0000))


def _unpack_halves(w):
    return pltpu.bitcast(w << 16, F32), pltpu.bitcast(w & jnp.uint32(0xFFFF0000), F32)


def _route(xn, gain_ref, sc_ref, sh_ref, wr_ref, br_ref, h_ref, re_ref, rg_ref, cnt_ref, carry):
    @pl.when(jnp.logical_and(pl.program_id(0) == 0, pl.program_id(1) == 0))
    def _():
        carry[...] = jnp.zeros_like(carry)

    h = _normmod(xn, gain_ref[...], sc_ref[0], sh_ref[0])
    h_ref[0] = _pack_halves(h)
    logits = _dot3(_split(h), (wr_ref[0], wr_ref[1])) + br_ref[...]
    lane = lax.broadcasted_iota(jnp.int32, logits.shape, 1).astype(F32)
    neg = jnp.float32(-jnp.inf)
    big = jnp.float32(1e9)
    gl = jnp.where(lane < N_GROUPS, logits, neg)
    gm = jnp.max(gl, axis=-1, keepdims=True)
    g_val = 1.0 / jnp.sum(jnp.exp(gl - gm), axis=-1, keepdims=True)
    g_idx = jnp.min(jnp.where(gl == gm, lane, big), axis=-1, keepdims=True)
    lo = N_GROUPS + EXPERTS_PER_GROUP * g_idx
    el = jnp.where(jnp.logical_and(lane >= lo, lane < lo + EXPERTS_PER_GROUP), logits, neg)
    em = jnp.max(el, axis=-1, keepdims=True)
    i1 = jnp.min(jnp.where(el == em, lane, big), axis=-1, keepdims=True)
    el2 = jnp.where(lane == i1, neg, el)
    em2 = jnp.max(el2, axis=-1, keepdims=True)
    i2 = jnp.min(jnp.where(el2 == em2, lane, big), axis=-1, keepdims=True)
    es = jnp.sum(jnp.exp(el - em), axis=-1, keepdims=True)
    p1 = 1.0 / es
    p2 = jnp.exp(em2 - em) / es
    den = p1 + p2
    rg_ref[0] = jnp.where(lane == 0, g_val * p1 / den, jnp.where(lane == 1, g_val * p2 / den, 0.0))
    oh1 = lane == i1
    oh2 = lane == i2
    both = jnp.where(jnp.logical_or(oh1, oh2), 1.0, 0.0)
    ts = both.shape[0]
    tri = (lax.broadcasted_iota(jnp.int32, (ts, ts), 0) > lax.broadcasted_iota(jnp.int32, (ts, ts), 1))
    pre = _dot(jnp.where(tri, 1.0, 0.0).astype(BF16), both.astype(BF16)) + carry[...]
    r1 = jnp.sum(jnp.where(oh1, pre, 0.0), axis=-1, keepdims=True)
    r2 = jnp.sum(jnp.where(oh2, pre, 0.0), axis=-1, keepdims=True)
    re_ref[0] = jnp.where(lane == 0, i1 - N_GROUPS, jnp.where(lane == 1, i2 - N_GROUPS, jnp.where(
        lane == 2, r1, jnp.where(lane == 3, r2, 0.0)))).astype(jnp.int32)
    carry[...] += jnp.sum(both, axis=0, keepdims=True)
    cnt_ref[...] = carry[...]


def _route_plumbing(rt, b, s, d, ts):
    gain, sc, sh, wr, br = rt
    ins = [gain, sc, sh, wr, br]
    in_specs = [_full(gain), _modspec(sc, ts), _modspec(sh, ts), _full(wr), _full(br)]
    out_specs = [_tok(d // 2, ts), _tok(LANES, ts), _tok(LANES, ts), pl.BlockSpec((1, LANES), lambda bb, j: (0, 0))]
    out_shape = [jax.ShapeDtypeStruct((b, s, d // 2), jnp.uint32),
                 jax.ShapeDtypeStruct((b, s, LANES), jnp.int32),
                 jax.ShapeDtypeStruct((b, s, LANES), F32),
                 jax.ShapeDtypeStruct((1, LANES), F32)]
    return ins, in_specs, out_specs, out_shape, [pltpu.VMEM((1, LANES), F32)]


def _row_copy(src, dst, src_row, dst_row, sem):
    return pltpu.make_async_copy(src.at[pl.ds(src_row, 1)], dst.at[pl.ds(dst_row, 1)], sem)


ROUTE_W = 2 * TOP_K
DMA_UNROLL = 8


def _slot(ps_ref, rt_ref, r, k):
    return ps_ref[rt_ref[0, 0, ROUTE_W * r + k]] + rt_ref[0, 0, ROUTE_W * r + TOP_K + k]


def _dispatch_kernel(ps_ref, rt_ref, h_ref, xin_hbm, xb_hbm, sem, *, tb):
    del xin_hbm

    def issue(r, c):
        for k in range(TOP_K):
            _row_copy(h_ref, xb_hbm, r, _slot(ps_ref, rt_ref, r, k), sem).start()
        return c
    lax.fori_loop(0, tb, issue, 0, unroll=DMA_UNROLL)
    for k in range(TOP_K):
        pltpu.make_async_copy(h_ref, xb_hbm.at[pl.ds(0, tb)], sem).wait()


def _dispatch_call(pad_start, route, h2, xb0, tb):
    n, dh = h2.shape
    nb = n // tb
    grid_spec = pltpu.PrefetchScalarGridSpec(
        num_scalar_prefetch=1,
        grid=(nb,),
        in_specs=[pl.BlockSpec((1, 1, ROUTE_W * tb), lambda j, ps: (j, 0, 0), memory_space=pltpu.SMEM),
                  pl.BlockSpec((tb, dh), lambda j, ps: (j, 0)),
                  pl.BlockSpec(memory_space=pl.ANY)],
        out_specs=pl.BlockSpec(memory_space=pl.ANY),
        scratch_shapes=[pltpu.SemaphoreType.DMA(())],
    )
    return pl.pallas_call(
        functools.partial(_dispatch_kernel, tb=tb),
        grid_spec=grid_spec,
        out_shape=jax.ShapeDtypeStruct(xb0.shape, xb0.dtype),
        input_output_aliases={3: 0},
        compiler_params=_cparams("arbitrary"),
        name="moe_dispatch",
    )(pad_start, route, h2, xb0)


def _ffn_kernel(be_ref, nu_ref, x_ref, w1_ref, w3_ref, w2_ref, y_ref, w1b, w3b, w2b):
    i = pl.program_id(0)
    half = w1b.shape[0] // 2

    @pl.when(jnp.logical_and(i < nu_ref[0], jnp.logical_or(i == 0, be_ref[i] != be_ref[jnp.maximum(i - 1, 0)])))
    def _():
        w1b[...] = w1_ref[0, 0].astype(BF16)
        w3b[...] = w3_ref[0, 0].astype(BF16)
        w2b[...] = w2_ref[0, 0].astype(BF16)

    @pl.when(i < nu_ref[0])
    def _():
        lo, hi = _unpack_halves(x_ref[...])
        lo = lo.astype(BF16)
        hi = hi.astype(BF16)
        a = _dot(lo, w1b[:half]) + _dot(hi, w1b[half:])
        g = _dot(lo, w3b[:half]) + _dot(hi, w3b[half:])
        mid = (a * jax.nn.sigmoid(a) * g).astype(BF16)
        y_ref[...] = _pack_halves(_dot(mid, w2b[...]))

    @pl.when(i >= nu_ref[0])
    def _():
        y_ref[...] = jnp.zeros_like(y_ref)


def _ffn_call(block_e, n_used, xb, w1, w3, w2, layer):
    n_blocks = block_e.shape[0]
    dh = xb.shape[1]
    d = 2 * dh
    blk = xb.shape[0] // n_blocks
    grid_spec = pltpu.PrefetchScalarGridSpec(
        num_scalar_prefetch=2,
        grid=(n_blocks,),
        in_specs=[pl.BlockSpec((blk, dh), lambda i, be, nu: (i, 0)),
                  pl.BlockSpec((1, 1, d, D_EXPERT), lambda i, be, nu: (layer, be[i], 0, 0)),
                  pl.BlockSpec((1, 1, d, D_EXPERT), lambda i, be, nu: (layer, be[i], 0, 0)),
                  pl.BlockSpec((1, 1, D_EXPERT, d), lambda i, be, nu: (layer, be[i], 0, 0))],
        out_specs=pl.BlockSpec((blk, dh), lambda i, be, nu: (i, 0)),
        scratch_shapes=[pltpu.VMEM((d, D_EXPERT), BF16), pltpu.VMEM((d, D_EXPERT), BF16),
                        pltpu.VMEM((D_EXPERT, d), BF16)],
    )
    return pl.pallas_call(
        _ffn_kernel,
        grid_spec=grid_spec,
        out_shape=jax.ShapeDtypeStruct(xb.shape, jnp.uint32),
        compiler_params=_cparams("arbitrary"),
        name="moe_ffn",
    )(block_e, n_used, xb, w1, w3, w2)


def _combine_kernel(ps_ref, rcur_ref, rnxt_ref, yb_hbm, x_ref, rg_ref, g2_ref, fin_ref, o_ref, b0, b1, sem,
                    *, tb, final):
    j = pl.program_id(0)
    nb = pl.num_programs(0)
    slot = j % 2

    def fetch(rref, sl):
        def issue(r, c):
            _row_copy(yb_hbm, b0.at[sl], _slot(ps_ref, rref, r, 0), r, sem.at[sl]).start()
            _row_copy(yb_hbm, b1.at[sl], _slot(ps_ref, rref, r, 1), r, sem.at[sl]).start()
            return c
        lax.fori_loop(0, tb, issue, 0, unroll=DMA_UNROLL)

    @pl.when(j == 0)
    def _():
        fetch(rcur_ref, 0)

    @pl.when(j + 1 < nb)
    def _():
        fetch(rnxt_ref, 1 - slot)

    pltpu.make_async_copy(yb_hbm.at[pl.ds(0, tb)], b0.at[slot], sem.at[slot]).wait()
    pltpu.make_async_copy(yb_hbm.at[pl.ds(0, tb)], b1.at[slot], sem.at[slot]).wait()
    rg = rg_ref[...]
    lo0, hi0 = _unpack_halves(b0[slot])
    lo1, hi1 = _unpack_halves(b1[slot])
    g0 = rg[:, 0:1]
    g1 = rg[:, 1:2]
    y = jnp.concatenate([g0 * lo0 + g1 * lo1, g0 * hi0 + g1 * hi1], axis=1)
    xn = x_ref[...] + g2_ref[0] * y
    if final:
        xn = _rms(xn, fin_ref[...])
    o_ref[...] = xn


def _combine_call(pad_start, route, yb, x, rg, g2, fin, tb, final):
    b, s, d = x.shape
    n = b * s
    nb = n // tb
    npb = s // tb
    dh = yb.shape[1]
    if g2.shape[1] == 1:
        gspec = pl.BlockSpec((1, 1, d), lambda j, ps: (j // npb, 0, 0))
    else:
        gspec = pl.BlockSpec((1, tb, d), lambda j, ps: (j // npb, j % npb, 0))
    grid_spec = pltpu.PrefetchScalarGridSpec(
        num_scalar_prefetch=1,
        grid=(nb,),
        in_specs=[pl.BlockSpec((1, 1, ROUTE_W * tb), lambda j, ps: (j, 0, 0), memory_space=pltpu.SMEM),
                  pl.BlockSpec((1, 1, ROUTE_W * tb), lambda j, ps: (jnp.minimum(j + 1, nb - 1), 0, 0),
                               memory_space=pltpu.SMEM),
                  pl.BlockSpec(memory_space=pl.ANY),
                  pl.BlockSpec((tb, d), lambda j, ps: (j, 0)), pl.BlockSpec((tb, LANES), lambda j, ps: (j, 0)),
                  gspec, pl.BlockSpec(fin.shape, lambda j, ps: (0, 0))],
        out_specs=pl.BlockSpec((tb, d), lambda j, ps: (j, 0)),
        scratch_shapes=[pltpu.VMEM((2, tb, dh), jnp.uint32), pltpu.VMEM((2, tb, dh), jnp.uint32),
                        pltpu.SemaphoreType.DMA((2,))],
    )
    out = pl.pallas_call(
        functools.partial(_combine_kernel, tb=tb, final=final),
        grid_spec=grid_spec,
        out_shape=jax.ShapeDtypeStruct((n, d), F32),
        compiler_params=_cparams("arbitrary"),
        name="moe_combine",
    )(pad_start, route, route, yb, x.reshape(n, d), rg.reshape(n, LANES), g2, fin)
    return out.reshape(b, s, d)


def _moe(x, routed, g2, mp, fin, ts, final):
    b, s, d = x.shape
    n = b * s
    h2, r_e, r_g, cnt = routed
    counts = cnt[0, N_GROUPS:N_GROUPS + N_EXPERTS].astype(jnp.int32)
    blk = MOE_BLOCK if n * TOP_K >= N_EXPERTS * MOE_BLOCK else MOE_BLOCK_SMALL
    padded = (counts + blk - 1) // blk * blk
    pad_end = jnp.cumsum(padded)
    pad_start = (pad_end - padded).astype(jnp.int32)
    n_blocks = (n * TOP_K + N_EXPERTS * (blk - 1) + blk - 1) // blk
    blk0 = jnp.arange(n_blocks, dtype=jnp.int32) * blk
    block_e = jnp.minimum(jnp.sum((pad_end[None, :] <= blk0[:, None]).astype(jnp.int32), axis=1), N_EXPERTS - 1)
    n_used = (pad_end[-1:] // blk).astype(jnp.int32)
    route = r_e[:, :, :ROUTE_W].reshape(n // ts, 1, ROUTE_W * ts)
    xb0 = jnp.zeros((n_blocks * blk, d // 2), jnp.uint32)
    xb = _dispatch_call(pad_start, route, h2.reshape(n, d // 2), xb0, ts)
    yb = _ffn_call(block_e, n_used, xb, mp["w1"], mp["w3"], mp["w2"], mp["layer"])
    return _combine_call(pad_start, route, yb, x, r_g, g2, fin, ts, final)


def _prep_ab(l, p):
    w_in = p["w_in_ab"][l]
    o2 = S5_WIDTH + MLA_Q_RANK + MLA_KV_RANK
    half = MLA_ROPE // 2
    d = w_in.shape[0]
    kr = w_in[:, o2:]
    kr_rot = jnp.concatenate([-kr[:, half:], kr[:, :half]], axis=1)
    z64 = jnp.zeros((d, MLA_NOPE), F32)
    z32 = jnp.zeros((d, LANES - MLA_NOPE - MLA_ROPE), F32)
    w_ext = jnp.concatenate([w_in[:, :o2], z64, kr, z32, z64, kr_rot, z32], axis=1)
    qu = p["mla_q_up"][l].reshape(MLA_Q_RANK, MLA_HEADS, MLA_NOPE + MLA_ROPE)
    qn, qr = qu[:, :, :MLA_NOPE], qu[:, :, MLA_NOPE:]
    zq = jnp.zeros((MLA_Q_RANK, MLA_HEADS, LANES - MLA_NOPE - MLA_ROPE), F32)
    wqa = jnp.concatenate([qn, qr, zq], axis=2).reshape(MLA_Q_RANK, MLA_HEADS * LANES)
    wqb = jnp.concatenate([jnp.zeros_like(qn), -qr[:, :, half:], qr[:, :, :half], zq], axis=2)
    wqb = wqb.reshape(MLA_Q_RANK, MLA_HEADS * LANES)
    kvu = p["mla_kv_up"][l].reshape(MLA_KV_RANK, MLA_HEADS, MLA_NOPE + MLA_V)
    wk = jnp.concatenate([kvu[:, :, :MLA_NOPE], jnp.zeros((MLA_KV_RANK, MLA_HEADS, LANES - MLA_NOPE), F32)], axis=2)
    wo = p["w_out_ab"][l]
    out = dict(
        w_in=w_ext.astype(BF16), q_norm=p["mla_q_norm"][l].reshape(1, -1), kv_norm=p["mla_kv_norm"][l].reshape(1, -1),
        wqa=wqa.astype(BF16), wqb=wqb.astype(BF16),
        wk=wk.reshape(MLA_KV_RANK, MLA_HEADS * LANES).astype(BF16),
        wv=jnp.concatenate([kvu[:, :, MLA_NOPE:], jnp.zeros((MLA_KV_RANK, MLA_HEADS, LANES - MLA_V), F32)],
                           axis=2).reshape(MLA_KV_RANK, MLA_HEADS * LANES).astype(BF16),
        s5_d=p["s5_d"][l].reshape(1, -1), glu_w=p["s5_glu_w"][l].astype(BF16), glu_b=p["s5_glu_b"][l].reshape(1, -1),
        w_out_s5=wo[:S5_WIDTH].astype(BF16),
        w_out_at=wo[S5_WIDTH:].astype(BF16))
    out.update(_s5_mats(p["s5_a_re"][l], p["s5_a_im"][l], p["s5_b_re"][l], p["s5_b_im"][l],
                        p["s5_c_re"][l], p["s5_c_im"][l], p["s5_log_dt"][l]))
    return out


def _prep_c(l, p):
    w = p["w_in_c"][l]
    pad = jnp.zeros((w.shape[0], LANES - 2 * GDN_HEADS), F32)

    def row(v):
        return jnp.concatenate([v, jnp.zeros((LANES - GDN_HEADS,), F32)]).reshape(1, LANES)

    return dict(w_in=jnp.concatenate([w, pad], axis=1).astype(BF16), conv_w=p["conv_w"][l],
                alog=row(p["gdn_a_log"][l]), dtb=row(p["gdn_dt_bias"][l]),
                gn=p["gdn_norm"][l].reshape(1, -1), w_out=p["w_out_c"][l].astype(BF16))


def _prep_moe(layer, p):
    d = p["moe_w_group"].shape[1]
    wr = jnp.concatenate([p["moe_w_group"][layer], p["moe_w_expert"][layer],
                          jnp.zeros((d, LANES - N_GROUPS - N_EXPERTS), F32)], axis=1)
    br = jnp.concatenate([p["moe_b_group"][layer], p["moe_b_expert"][layer],
                          jnp.zeros((LANES - N_GROUPS - N_EXPERTS,), F32)]).reshape(1, LANES)
    wr_hi = wr.astype(BF16)
    wr_lo = (wr - wr_hi.astype(F32)).astype(BF16)
    return dict(wr=jnp.stack([wr_hi, wr_lo]), br=br, w1=p["moe_w1"], w3=p["moe_w3"], w2=p["moe_w2"], layer=layer)


def _rope_tables(pos):
    half = MLA_ROPE // 2
    inv = ROPE_THETA ** (-jnp.arange(half, dtype=F32) / half)
    ang = pos.astype(F32)[:, None] * inv[None, :]
    cos, sin = jnp.cos(ang), jnp.sin(ang)
    n = pos.shape[0]
    z64 = jnp.zeros((n, MLA_NOPE), F32)
    z32 = jnp.zeros((n, LANES - MLA_NOPE - MLA_ROPE), F32)
    scale = (MLA_NOPE + MLA_ROPE) ** -0.5 * math.log2(math.e)
    ck = jnp.concatenate([z64, cos, cos, z32], axis=1)
    sk = jnp.concatenate([z64, sin, sin, z32], axis=1)
    cq = jnp.concatenate([jnp.ones_like(z64), cos, cos, z32], axis=1) * scale
    return cq, sk * scale, ck, sk


def _trunk(x, mods, caches, wab, wc, wmoe, p, flat):
    b, s, d = x.shape
    lat_c, rope_c, s5re_c, s5im_c, conv_c, gdn_c = caches
    past = 0 if lat_c is None else lat_c.shape[2]
    pos = past + jnp.arange(s)
    if flat:
        xt = x.reshape(1, b * s, d)
        ts = b * s
        pos_rows = jnp.tile(pos, b)

        def mrow(m):
            return jnp.repeat(m, s, axis=0).reshape(1, b * s, d)
    else:
        xt = x
        ts = TOKEN_BLOCK
        pos_rows = pos

        def mrow(m):
            return m.reshape(b, 1, d)
    bt, st = xt.shape[:2]
    outs = {}
    depth = p["norm_mix"].shape[0]
    for layer in range(depth):
        l = layer // 2
        sh1, sc1, g1, sh2, sc2, g2 = [mrow(m) for m in jnp.split(mods[layer], 6, axis=-1)]
        gain = p["norm_mix"][layer].reshape(1, d)
        rt = (p["norm_ffn"][layer].reshape(1, d), sc2, sh2, wmoe[layer]["wr"], wmoe[layer]["br"])
        if layer % 2 == 0:
            wp = wab[l]
            u, q, lat, kr = _ab_in_call(xt, gain, sc1, sh1, wp, _rope_tables(pos_rows), ts)
            lat_b = lat.reshape(b, s, MLA_KV_RANK)
            kr_b = kr.reshape(b, s, LANES)
            outs["lat"] = lat_b
            outs["krope"] = kr_b[:, :, MLA_NOPE:MLA_NOPE + MLA_ROPE]
            if lat_c is None:
                lat_all, kr_all, sk_valid, tkv = lat_b, kr_b, s, TOKEN_BLOCK
                tq, tk = min(ATTN_TQ, s), min(ATTN_TK, s)
            else:
                krc = jnp.pad(rope_c[l], ((0, 0), (0, 0), (MLA_NOPE, LANES - MLA_NOPE - MLA_ROPE)))
                sk_valid = past + s
                skp = (sk_valid + LANES - 1) // LANES * LANES
                lat_all = jnp.pad(jnp.concatenate([lat_c[l], lat_b], axis=1), ((0, 0), (0, skp - sk_valid), (0, 0)))
                kr_all = jnp.pad(jnp.concatenate([krc, kr_b], axis=1), ((0, 0), (0, skp - sk_valid), (0, 0)))
                tkv, tq, tk = skp, s, skp
            kk, vv = _kv_call(lat_all, kr_all, wp, tkv)
            qh = q.reshape(MLA_HEADS, b, s, LANES).transpose(1, 0, 2, 3) if flat else q
            attn = _attn_call(qh, kk, vv, tq, tk, past, sk_valid).reshape(bt, st, MLA_HEADS * MLA_V)
            nchk = s // S5_CHUNK
            r = b * nchk
            u2 = u.reshape(b * s, S5_WIDTH)
            tr = min(r, S5_ROW_BLOCK)
            wre, wim = _s5_w_call(u2, wp["bre"], wp["bim"], tr)
            ngp = S5_GROUPS * S5_STATE

            def to_pairs(v):
                return v.reshape(b, 1, ngp)

            if s5re_c is None:
                x0re = jnp.zeros((b, 1, ngp), F32)
                x0im = jnp.zeros((b, 1, ngp), F32)
            else:
                x0re, x0im = to_pairs(s5re_c[l]), to_pairs(s5im_c[l])
            nchp = (nchk + 7) // 8 * 8
            tc = min(nchp, S5_SCAN_BLOCK)
            wre3 = jnp.pad(wre.reshape(b, nchk, ngp), ((0, 0), (0, nchp - nchk), (0, 0)))
            wim3 = jnp.pad(wim.reshape(b, nchk, ngp), ((0, 0), (0, nchp - nchk), (0, 0)))
            xere, xeim = _s5_scan_call(wre3, wim3, wp["lre"], wp["lim"], x0re, x0im, tc)
            outs["s5re"] = xere[:, nchk - 1].reshape(b, S5_GROUPS, S5_STATE)
            outs["s5im"] = xeim[:, nchk - 1].reshape(b, S5_GROUPS, S5_STATE)
            xsre = jnp.concatenate([x0re, xere[:, :nchk - 1]], axis=1).reshape(r, ngp)
            xsim = jnp.concatenate([x0im, xeim[:, :nchk - 1]], axis=1).reshape(r, ngp)
            ys = _s5_y_call(u2, xsre, xsim, wp["mst"], wp["cc"], tr).reshape(bt, st, S5_WIDTH)
            xt, *routed = _ab_out_call(xt, ys, u, attn, wp, g1, rt, ts)
        else:
            wp = wc[l]
            qkv, z, ab = _c_in_call(xt, gain, sc1, sh1, wp["w_in"], ts)
            qkv_b = qkv.reshape(b, s, GDN_QKV)
            outs["conv"] = qkv_b[:, s - (CONV_W - 1):].astype(F32)
            if conv_c is None:
                st8 = jnp.zeros((b, 8, GDN_QKV), BF16)
                s0 = jnp.zeros((b, GDN_HEADS, GDN_DK, GDN_DV), F32)
            else:
                st8 = jnp.pad(conv_c[l], ((0, 0), (8 - (CONV_W - 1), 0), (0, 0))).astype(BF16)
                s0 = gdn_c[l]
            lg = min(s, CHUNK)
            wq, um, kgt, pm, ee = _gdn_prep_call(qkv_b, st8, wp["conv_w"], ab.reshape(b, s, LANES),
                                                 wp["alog"], wp["dtb"], lg)
            o, sfin = _gdn_seq_call(wq, um, kgt, pm, ee, s0, lg)
            outs["gdn"] = sfin
            xt, *routed = _c_out_call(xt, o.reshape(bt, st, GDN_VW), z, wp["gn"], wp["w_out"], g1, rt, ts)
        final = layer == depth - 1
        xt = _moe(xt, routed, g2, wmoe[layer], p["norm_final"].reshape(1, d), ts, final)
    y = xt.reshape(b, s, d)
    return (y, outs["lat"][None], outs["krope"][None], outs["s5re"][None], outs["s5im"][None],
            outs["conv"][None], outs["gdn"][None])


def kernel(x_prompt, x_sample, c_prompt, c_sample, cache_mla_latent, cache_mla_krope, state_s5_re, state_s5_im,
           state_conv, state_gdn, w_mod, b_mod, norm_mix, norm_ffn, norm_final, w_in_ab, s5_a_re, s5_a_im,
           s5_b_re, s5_b_im, s5_c_re, s5_c_im, s5_d, s5_log_dt, s5_glu_w, s5_glu_b, mla_q_norm, mla_q_up,
           mla_kv_norm, mla_kv_up, w_out_ab, w_in_c, conv_w, gdn_a_log, gdn_dt_bias, gdn_norm, w_out_c,
           moe_w_group, moe_b_group, moe_w_expert, moe_b_expert, moe_w1, moe_w3, moe_w2):
    p = dict(w_mod=w_mod, b_mod=b_mod, norm_mix=norm_mix, norm_ffn=norm_ffn, norm_final=norm_final,
             w_in_ab=w_in_ab, s5_a_re=s5_a_re, s5_a_im=s5_a_im, s5_b_re=s5_b_re, s5_b_im=s5_b_im,
             s5_c_re=s5_c_re, s5_c_im=s5_c_im, s5_d=s5_d, s5_log_dt=s5_log_dt, s5_glu_w=s5_glu_w,
             s5_glu_b=s5_glu_b, mla_q_norm=mla_q_norm, mla_q_up=mla_q_up, mla_kv_norm=mla_kv_norm,
             mla_kv_up=mla_kv_up, w_out_ab=w_out_ab, w_in_c=w_in_c, conv_w=conv_w, gdn_a_log=gdn_a_log,
             gdn_dt_bias=gdn_dt_bias, gdn_norm=gdn_norm, w_out_c=w_out_c, moe_w_group=moe_w_group,
             moe_b_group=moe_b_group, moe_w_expert=moe_w_expert, moe_b_expert=moe_b_expert,
             moe_w1=moe_w1, moe_w3=moe_w3, moe_w2=moe_w2)
    depth = norm_mix.shape[0]
    bp, bs = c_prompt.shape[0], c_sample.shape[0]
    nb = (bp + bs + 7) // 8 * 8
    c_all = jnp.pad(jnp.concatenate([c_prompt, c_sample], axis=0), ((0, nb - bp - bs), (0, 0)))
    mods = _mod_call(c_all, w_mod, b_mod)
    wab = [_prep_ab(l, p) for l in range((depth + 1) // 2)]
    wc = [_prep_c(l, p) for l in range(depth // 2)]
    wmoe = [_prep_moe(layer, p) for layer in range(depth)]
    none6 = (None,) * 6
    outp = _trunk(x_prompt, mods[:, :bp], none6, wab, wc, wmoe, p, flat=False)
    caches = (cache_mla_latent, cache_mla_krope, state_s5_re, state_s5_im, state_conv, state_gdn)
    outs = _trunk(x_sample, mods[:, bp:bp + bs], caches, wab, wc, wmoe, p, flat=True)
    return (outp[0], outs[0]) + tuple(outp[1:]) + tuple(outs[1:])
```

```python
import functools
import math

import jax
import jax.numpy as jnp
from jax import lax
from jax.experimental import pallas as pl
from jax.experimental.pallas import tpu as pltpu

F32 = jnp.float32
BF16 = jnp.bfloat16
HI = lax.Precision.HIGHEST
EPS = 1e-6

D_MODEL = 1024
CHUNK = 64
S5_WIDTH = 512
S5_GROUP = 16
S5_GROUPS = 32
S5_STATE = 64
S5_CHUNK = 16
S5_LB = 128 // S5_GROUP
MLA_HEADS = 8
MLA_NOPE = 64
MLA_ROPE = 32
MLA_V = 64
MLA_Q_RANK = 384
MLA_KV_RANK = 256
ROPE_THETA = 10000.0
GDN_HEADS = 8
GDN_DK = 128
GDN_DV = 128
CONV_W = 4
GDN_QKV = GDN_HEADS * (2 * GDN_DK + GDN_DV)
GDN_VW = GDN_HEADS * GDN_DV
N_GROUPS = 4
EXPERTS_PER_GROUP = 8
N_EXPERTS = 32
D_EXPERT = 512
TOP_K = 2
MOE_BLOCK = 512
MOE_BLOCK_SMALL = 128
ATTN_SUB = 512
ATTN_TQ = 512
ATTN_TK = 512
TOKEN_BLOCK = 256
S5_ROW_BLOCK = 256
S5_SCAN_BLOCK = 128
LANES = 128
VMEM_LIMIT = 48 * 1024 * 1024


def _cparams(*sem):
    return pltpu.CompilerParams(dimension_semantics=sem, vmem_limit_bytes=VMEM_LIMIT)


def _dot(a, b):
    return jnp.dot(a, b, preferred_element_type=F32)


def _dot_nt(a, b, precision=None):
    return lax.dot_general(a, b, (((1,), (1,)), ((), ())), preferred_element_type=F32, precision=precision)


def _full(arr):
    nd = arr.ndim
    return pl.BlockSpec(arr.shape, lambda *_: (0,) * nd)


def _tok(width, ts):
    return pl.BlockSpec((1, ts, width), lambda b, j: (b, j, 0))


def _modspec(arr, ts):
    if arr.shape[1] == 1:
        return pl.BlockSpec((1, 1, arr.shape[2]), lambda b, j: (b, 0, 0))
    return pl.BlockSpec((1, ts, arr.shape[2]), lambda b, j: (b, j, 0))


def _normmod(x, gain, sc, sh):
    ms = jnp.mean(x * x, axis=-1, keepdims=True)
    return x * lax.rsqrt(ms + EPS) * gain * (1.0 + sc) + sh


def _rms(x, gain):
    return x * lax.rsqrt(jnp.mean(x * x, axis=-1, keepdims=True) + EPS) * gain


def _mod_kernel(c_ref, w_ref, b_ref, o_ref):
    c = c_ref[...]
    a = (c * jax.nn.sigmoid(c)).astype(BF16)
    o_ref[0] = _dot(a, w_ref[0]) + b_ref[0]


def _mod_call(c_all, w_mod, b_mod):
    depth, d, n6 = w_mod.shape
    bp = c_all.shape[0]
    tn = 1536
    return pl.pallas_call(
        _mod_kernel,
        grid=(depth, n6 // tn),
        in_specs=[pl.BlockSpec((bp, d), lambda l, n: (0, 0)),
                  pl.BlockSpec((1, d, tn), lambda l, n: (l, 0, n)),
                  pl.BlockSpec((1, 1, tn), lambda l, n: (l, 0, n))],
        out_specs=pl.BlockSpec((1, bp, tn), lambda l, n: (l, 0, n)),
        out_shape=jax.ShapeDtypeStruct((depth, bp, n6), F32),
        compiler_params=_cparams("parallel", "parallel"),
        name="mod",
    )(c_all, w_mod.astype(BF16), b_mod.reshape(depth, 1, n6))


def _ab_in_kernel(x_ref, gain_ref, sc_ref, sh_ref, w_ref, qg_ref, wqa_ref, wqb_ref, kvg_ref,
                  cq_ref, sq_ref, ck_ref, sk_ref, u_ref, q_ref, lat_ref, kr_ref):
    h = _normmod(x_ref[0], gain_ref[...], sc_ref[0], sh_ref[0]).astype(BF16)
    proj = _dot(h, w_ref[...])
    u_ref[0] = proj[:, :S5_WIDTH]
    o1 = S5_WIDTH + MLA_Q_RANK
    o2 = o1 + MLA_KV_RANK
    qn = _rms(proj[:, S5_WIDTH:o1], qg_ref[...]).astype(BF16)
    qa = _dot(qn, wqa_ref[...])
    qb = _dot(qn, wqb_ref[...])
    cq = cq_ref[...]
    sq = sq_ref[...]
    for hh in range(MLA_HEADS):
        sl = slice(LANES * hh, LANES * (hh + 1))
        q_ref[0, hh] = (qa[:, sl] * cq + qb[:, sl] * sq).astype(BF16)
    lat_ref[0] = _rms(proj[:, o1:o2], kvg_ref[...])
    kr_ref[0] = proj[:, o2:o2 + LANES] * ck_ref[...] + proj[:, o2 + LANES:o2 + 2 * LANES] * sk_ref[...]


def _ab_in_call(x, gain, sc, sh, wp, tabs, ts):
    b, s, d = x.shape
    cq, sq, ck, sk = tabs
    tab = pl.BlockSpec((ts, LANES), lambda bb, j: (j, 0))
    ins = [x, gain, sc, sh, wp["w_in"], wp["q_norm"], wp["wqa"], wp["wqb"], wp["kv_norm"], cq, sq, ck, sk]
    specs = [_tok(d, ts), _full(gain), _modspec(sc, ts), _modspec(sh, ts), _full(wp["w_in"]), _full(wp["q_norm"]),
             _full(wp["wqa"]), _full(wp["wqb"]), _full(wp["kv_norm"]), tab, tab, tab, tab]
    return pl.pallas_call(
        _ab_in_kernel,
        grid=(b, s // ts),
        in_specs=specs,
        out_specs=[_tok(S5_WIDTH, ts),
                   pl.BlockSpec((1, MLA_HEADS, ts, LANES), lambda bb, j: (bb, 0, j, 0)),
                   _tok(MLA_KV_RANK, ts), _tok(LANES, ts)],
        out_shape=[jax.ShapeDtypeStruct((b, s, S5_WIDTH), F32),
                   jax.ShapeDtypeStruct((b, MLA_HEADS, s, LANES), BF16),
                   jax.ShapeDtypeStruct((b, s, MLA_KV_RANK), F32),
                   jax.ShapeDtypeStruct((b, s, LANES), F32)],
        compiler_params=_cparams("parallel", "parallel"),
        name="ab_in",
    )(*ins)


def _kv_kernel(lat_ref, kr_ref, wk_ref, wv_ref, k_ref, v_ref):
    lat = lat_ref[0].astype(BF16)
    kk = _dot(lat, wk_ref[...])
    vv = _dot(lat, wv_ref[...])
    kr = kr_ref[0]
    ones = (lax.broadcasted_iota(jnp.int32, (1, LANES), 1) >= MLA_V).astype(F32)
    for hh in range(MLA_HEADS):
        k_ref[0, hh] = (kk[:, LANES * hh:LANES * (hh + 1)] + kr).astype(BF16)
        v_ref[0, hh] = (vv[:, LANES * hh:LANES * (hh + 1)] + ones).astype(BF16)


def _kv_call(lat_all, kr_all, wp, ts):
    b, sk, _ = lat_all.shape
    hspec = pl.BlockSpec((1, MLA_HEADS, ts, LANES), lambda bb, j: (bb, 0, j, 0))
    hshape = jax.ShapeDtypeStruct((b, MLA_HEADS, sk, LANES), BF16)
    return pl.pallas_call(
        _kv_kernel,
        grid=(b, sk // ts),
        in_specs=[_tok(MLA_KV_RANK, ts), _tok(LANES, ts), _full(wp["wk"]), _full(wp["wv"])],
        out_specs=[hspec, hspec],
        out_shape=[hshape, hshape],
        compiler_params=_cparams("parallel", "parallel"),
        name="kv_up",
    )(lat_all, kr_all, wp["wk"], wp["wv"])


def _attn_kernel(q_ref, k_ref, v_ref, *rest, tq, tk, q_off, sk_valid):
    o_ref = rest[-1]
    bias_ref = rest[0] if len(rest) == 2 else None
    i = pl.program_id(2)
    nsub = max(tq // ATTN_SUB, 1)
    tqs = tq // nsub
    chains = [(hh, qi) for hh in range(2) for qi in range(nsub)]
    qs = [q_ref[0, hh, qi * tqs:(qi + 1) * tqs, :] for hh, qi in chains]
    q0 = q_off + i * tq
    lim_full = (q0 // CHUNK + 1) * CHUNK
    lim_tot = jnp.minimum(((q0 + tq - 1) // CHUNK + 1) * CHUNK, sk_valid)
    n_full = jnp.minimum(lim_full // tk, sk_valid // tk)
    n_tot = (lim_tot + tk - 1) // tk

    def step(j, carry, masked):
        off = pl.multiple_of(j * tk, tk)
        nc = range(len(chains))
        if masked and bias_ref is None:
            kpos = off + lax.broadcasted_iota(jnp.int32, (tqs, tk), 1)
            ok = []
            for qi in range(nsub):
                qpos = q0 + qi * tqs + lax.broadcasted_iota(jnp.int32, (tqs, tk), 0)
                ok.append(jnp.logical_and(kpos // CHUNK <= qpos // CHUNK, kpos < sk_valid))
        kt = [k_ref[0, hh, pl.ds(off, tk), :] for hh in range(2)]
        vt = [v_ref[0, hh, pl.ds(off, tk), :] for hh in range(2)]
        s = [_dot_nt(qs[c], kt[chains[c][0]]) for c in nc]
        if masked and bias_ref is None:
            s = [jnp.where(ok[chains[c][1]], s[c], -1e30) for c in nc]
        elif masked:
            s = [s[c] + bias_ref[chains[c][1] * tqs:(chains[c][1] + 1) * tqs, :] for c in nc]
        m_new = [jnp.maximum(carry[c][0], jnp.max(s[c], axis=-1, keepdims=True)) for c in nc]
        alpha = [jnp.exp2(carry[c][0] - m_new[c]) for c in nc]
        p = [jnp.exp2((s[c] - m_new[c]).astype(BF16)) for c in nc]
        acc = [alpha[c] * carry[c][1] + _dot(p[c], vt[chains[c][0]]) for c in nc]
        return tuple((m_new[c], acc[c]) for c in nc)

    one = (jnp.full((tqs, 1), -1e30, F32), jnp.zeros((tqs, LANES), F32))
    carry = lax.fori_loop(0, n_full, functools.partial(step, masked=False), (one,) * len(chains))
    carry = lax.fori_loop(n_full, n_tot, functools.partial(step, masked=True), carry)
    lane = lax.broadcasted_iota(jnp.int32, (tqs, LANES), 1)
    for qi in range(nsub):
        a0 = carry[chains.index((0, qi))][1]
        a1 = carry[chains.index((1, qi))][1]
        r0 = a0 / pltpu.roll(a0, MLA_V, axis=1)
        r1 = a1 / pltpu.roll(a1, MLA_V, axis=1)
        o_ref[0, qi * tqs:(qi + 1) * tqs, :] = jnp.where(
            lane < MLA_V, r0, pltpu.roll(r1, MLA_V, axis=1)).astype(o_ref.dtype)


def _attn_call(q, k, v, tq, tk, q_off, sk_valid):
    b, h, sq, _ = q.shape
    sk = k.shape[2]
    kern = functools.partial(_attn_kernel, tq=tq, tk=tk, q_off=q_off, sk_valid=sk_valid)
    kv_mode = pl.Buffered(2 if sq == tq else 1)
    ins = [q, k, v]
    specs = [pl.BlockSpec((1, 2, tq, LANES), lambda bb, hp, i: (bb, hp, i, 0)),
             pl.BlockSpec((1, 2, sk, LANES), lambda bb, hp, i: (bb, hp, 0, 0), pipeline_mode=kv_mode),
             pl.BlockSpec((1, 2, sk, LANES), lambda bb, hp, i: (bb, hp, 0, 0), pipeline_mode=kv_mode)]
    if q_off == 0 and tq == tk and sk_valid % tk == 0:
        r = jnp.arange(tq, dtype=jnp.int32)
        bias = jnp.where(r[None, :] // CHUNK <= r[:, None] // CHUNK, 0.0, -1e30).astype(F32)
        ins.append(bias)
        specs.append(pl.BlockSpec((tq, tk), lambda bb, hp, i: (0, 0)))
    return pl.pallas_call(
        kern,
        grid=(b, h // 2, sq // tq),
        in_specs=specs,
        out_specs=pl.BlockSpec((1, tq, LANES), lambda bb, hp, i: (bb, i, hp)),
        out_shape=jax.ShapeDtypeStruct((b, sq, h * MLA_V), BF16),
        compiler_params=_cparams("parallel", "parallel", "arbitrary"),
        name="mla_attn",
    )(*ins)


def _chunk_rows(ref, s, tr):
    return ref[pl.ds(s, tr, stride=S5_CHUNK), :]


def _s5_w_kernel(u_ref, bre_ref, bim_ref, wre_ref, wim_ref, *, tr):
    ucat = jnp.concatenate([_chunk_rows(u_ref, s, tr).astype(BF16) for s in range(S5_CHUNK)], axis=1)
    wre_ref[...] = _dot(ucat, bre_ref[0])
    wim_ref[...] = _dot(ucat, bim_ref[0])


def _s5_w_call(u2, bre, bim, tr):
    n = u2.shape[0]
    r = n // S5_CHUNK
    nq = bre.shape[0]
    sw = S5_LB * S5_STATE
    out = jax.ShapeDtypeStruct((r, nq * sw), F32)
    ospec = pl.BlockSpec((tr, sw), lambda q, i: (i, q))
    wspec = pl.BlockSpec((1, S5_CHUNK * LANES, sw), lambda q, i: (q, 0, 0))
    return pl.pallas_call(
        functools.partial(_s5_w_kernel, tr=tr),
        grid=(nq, r // tr),
        in_specs=[pl.BlockSpec((tr * S5_CHUNK, LANES), lambda q, i: (i, q)), wspec, wspec],
        out_specs=[ospec, ospec],
        out_shape=[out, out],
        compiler_params=_cparams("parallel", "parallel"),
        name="s5_chunk_in",
    )(u2, bre, bim)


def _s5_scan_kernel(wre_ref, wim_ref, lre_ref, lim_ref, x0re_ref, x0im_ref, ore_ref, oim_ref, sre, sim, *, tc):
    @pl.when(pl.program_id(1) == 0)
    def _():
        sre[...] = x0re_ref[0]
        sim[...] = x0im_ref[0]

    lr = lre_ref[...]
    li = lim_ref[...]

    def blk(t, carry):
        xr, xi = carry
        base = pl.multiple_of(t * 8, 8)
        wr = wre_ref[0, pl.ds(base, 8), :]
        wi = wim_ref[0, pl.ds(base, 8), :]
        rows_r, rows_i = [], []
        for r in range(8):
            nr = lr * xr - li * xi + wr[r:r + 1]
            ni = lr * xi + li * xr + wi[r:r + 1]
            xr, xi = nr, ni
            rows_r.append(xr)
            rows_i.append(xi)
        ore_ref[0, pl.ds(base, 8), :] = jnp.concatenate(rows_r, axis=0)
        oim_ref[0, pl.ds(base, 8), :] = jnp.concatenate(rows_i, axis=0)
        return xr, xi

    xr, xi = lax.fori_loop(0, tc // 8, blk, (sre[...], sim[...]))
    sre[...] = xr
    sim[...] = xi


def _s5_scan_call(wre, wim, lre, lim, x0re, x0im, tc):
    b, nch, n = wre.shape
    wspec = pl.BlockSpec((1, tc, n), lambda bb, c: (bb, c, 0))
    lspec = pl.BlockSpec((1, n), lambda bb, c: (0, 0))
    xspec = pl.BlockSpec((1, 1, n), lambda bb, c: (bb, 0, 0))
    out = jax.ShapeDtypeStruct((b, nch, n), F32)
    return pl.pallas_call(
        functools.partial(_s5_scan_kernel, tc=tc),
        grid=(b, nch // tc),
        in_specs=[wspec, wspec, lspec, lspec, xspec, xspec],
        out_specs=[wspec, wspec],
        out_shape=[out, out],
        scratch_shapes=[pltpu.VMEM((1, n), F32), pltpu.VMEM((1, n), F32)],
        compiler_params=_cparams("parallel", "arbitrary"),
        name="s5_scan",
    )(wre, wim, lre, lim, x0re, x0im)


def _s5_y_kernel(u_ref, xre_ref, xim_ref, m_ref, c_ref, y_ref, *, tr):
    L = S5_CHUNK
    urev = jnp.concatenate([_chunk_rows(u_ref, s, tr).astype(BF16) for s in reversed(range(L))], axis=1)
    xcat = jnp.concatenate([xre_ref[...].astype(BF16), xim_ref[...].astype(BF16)], axis=1)
    for t in range(L):
        y = _dot(urev[:, (L - 1 - t) * LANES:], m_ref[0, :(t + 1) * LANES, :]) + _dot(xcat, c_ref[0, t])
        y_ref[pl.ds(t, tr, stride=L), :] = y


def _s5_y_call(u2, xre, xim, mst, cc, tr):
    n, w = u2.shape
    nq = mst.shape[0]
    sw = S5_LB * S5_STATE
    uspec = pl.BlockSpec((tr * S5_CHUNK, LANES), lambda q, i: (i, q))
    xspec = pl.BlockSpec((tr, sw), lambda q, i: (i, q))
    return pl.pallas_call(
        functools.partial(_s5_y_kernel, tr=tr),
        grid=(nq, n // (tr * S5_CHUNK)),
        in_specs=[uspec, xspec, xspec,
                  pl.BlockSpec((1, S5_CHUNK * LANES, LANES), lambda q, i: (q, 0, 0)),
                  pl.BlockSpec((1, S5_CHUNK, 2 * sw, LANES), lambda q, i: (q, 0, 0, 0))],
        out_specs=uspec,
        out_shape=jax.ShapeDtypeStruct((n, w), F32),
        compiler_params=_cparams("parallel", "parallel"),
        name="s5_chunk_out",
    )(u2, xre, xim, mst, cc)


def _s5_mats(a_re, a_im, b_re, b_im, c_re, c_im, log_dt):
    g, p = a_re.shape
    L = S5_CHUNK
    lam = lax.complex(jnp.minimum(a_re, -1e-4), a_im)
    lamdt = lam * jnp.exp(log_dt)[:, None]
    lam_bar = jnp.exp(lamdt)
    b_bar = ((lam_bar - 1.0) / lam)[:, :, None] * lax.complex(b_re, b_im)
    cm = lax.complex(c_re, c_im)
    pw = jnp.exp(lamdt[:, :, None] * jnp.arange(L + 1, dtype=F32))
    mm = pw[:, :, :L, None] * b_bar[:, :, None, :]
    kd = jnp.sum((cm[:, :, :, None, None] * mm[:, None]).real, axis=2).transpose(0, 2, 1, 3)
    bp = pw[:, :, L - 1 - jnp.arange(L)][:, :, :, None] * b_bar[:, :, None, :]
    cp = cm[:, :, :, None] * pw[:, None, :, 1:]
    nq = g // S5_LB
    eye = jnp.eye(S5_LB, dtype=F32)

    def lane_block(m, spec, rows, cols):
        return jnp.einsum(spec, m.reshape((nq, S5_LB) + m.shape[1:]), eye).reshape(nq, rows, cols)

    sw = S5_LB * p
    bre = lane_block(bp.real, "qgpsc,hg->qshcgp", L * LANES, sw)
    bim = lane_block(bp.imag, "qgpsc,hg->qshcgp", L * LANES, sw)
    mst = lane_block(kd, "qgdoc,hg->qdhcgo", L * LANES, LANES)
    cre = lane_block(cp.real, "qgopt,hg->qthpgo", L * sw, LANES).reshape(nq, L, sw, LANES)
    cim = lane_block(-cp.imag, "qgopt,hg->qthpgo", L * sw, LANES).reshape(nq, L, sw, LANES)
    return dict(
        bre=bre.astype(BF16), bim=bim.astype(BF16), mst=mst.astype(BF16),
        cc=jnp.concatenate([cre, cim], axis=2).astype(BF16),
        lre=pw[:, :, L].real.reshape(1, g * p), lim=pw[:, :, L].imag.reshape(1, g * p))


def _ab_out_kernel(x_ref, ys_ref, u_ref, at_ref, d_ref, gw_ref, gb_ref, ws_ref, wa_ref, g1_ref, *route_refs):
    o_ref = route_refs[5]
    y = ys_ref[0] + d_ref[...] * u_ref[0]
    y = jax.nn.gelu(y)
    gate = jax.nn.sigmoid(_dot(y.astype(BF16), gw_ref[...]) + gb_ref[...])
    out = _dot((y * gate).astype(BF16), ws_ref[...]) + _dot(at_ref[0], wa_ref[...])
    xn = x_ref[0] + g1_ref[0] * out
    o_ref[0] = xn
    _route(xn, *route_refs[:5], *route_refs[6:])


def _ab_out_call(x, ys, u, attn, wp, g1, rt, ts):
    b, s, d = x.shape
    r_ins, r_specs, r_ospecs, r_oshape, r_scratch = _route_plumbing(rt, b, s, d, ts)
    ins = [x, ys, u, attn, wp["s5_d"], wp["glu_w"], wp["glu_b"], wp["w_out_s5"], wp["w_out_at"], g1] + r_ins
    specs = [_tok(d, ts), _tok(S5_WIDTH, ts), _tok(S5_WIDTH, ts), _tok(MLA_HEADS * MLA_V, ts),
             _full(wp["s5_d"]), _full(wp["glu_w"]), _full(wp["glu_b"]), _full(wp["w_out_s5"]),
             _full(wp["w_out_at"]), _modspec(g1, ts)] + r_specs
    return pl.pallas_call(
        _ab_out_kernel,
        grid=(b, s // ts),
        in_specs=specs,
        out_specs=[_tok(d, ts)] + r_ospecs,
        out_shape=[jax.ShapeDtypeStruct((b, s, d), F32)] + r_oshape,
        scratch_shapes=r_scratch,
        compiler_params=_cparams("arbitrary", "arbitrary"),
        name="ab_out",
    )(*ins)


def _c_in_kernel(x_ref, gain_ref, sc_ref, sh_ref, w_ref, qkv_ref, z_ref, ab_ref):
    h = _normmod(x_ref[0], gain_ref[...], sc_ref[0], sh_ref[0]).astype(BF16)
    proj = _dot(h, w_ref[...])
    qkv_ref[0] = proj[:, :GDN_QKV].astype(BF16)
    z_ref[0] = proj[:, GDN_QKV:GDN_QKV + GDN_VW].astype(BF16)
    ab_ref[0] = proj[:, GDN_QKV + GDN_VW:]


def _c_in_call(x, gain, sc, sh, w, ts):
    b, s, d = x.shape
    return pl.pallas_call(
        _c_in_kernel,
        grid=(b, s // ts),
        in_specs=[_tok(d, ts), _full(gain), _modspec(sc, ts), _modspec(sh, ts), _full(w)],
        out_specs=[_tok(GDN_QKV, ts), _tok(GDN_VW, ts), _tok(LANES, ts)],
        out_shape=[jax.ShapeDtypeStruct((b, s, GDN_QKV), BF16),
                   jax.ShapeDtypeStruct((b, s, GDN_VW), BF16),
                   jax.ShapeDtypeStruct((b, s, LANES), F32)],
        compiler_params=_cparams("parallel", "parallel"),
        name="c_in",
    )(x, gain, sc, sh, w)


def _gdn_prep_kernel(qkv_ref, halo_ref, st_ref, cw_ref, ab_ref, alog_ref, dtb_ref,
                     wq_ref, u_ref, kgt_ref, pm_ref, ee_ref, *, L, nc):
    j = pl.program_id(1)
    rows = nc * L
    prev = jnp.where(j == 0, st_ref[0].astype(F32), halo_ref[0].astype(F32))
    xx = jnp.concatenate([prev, qkv_ref[0].astype(F32)], axis=0)
    cw = cw_ref[...]
    y = xx[8:8 + rows] * cw[CONV_W - 1:CONV_W]
    for t in range(1, CONV_W):
        y += xx[8 - t:8 - t + rows] * cw[CONV_W - 1 - t:CONV_W - t]
    y = y * jax.nn.sigmoid(y)

    ab = ab_ref[0]
    g_all = -jnp.exp(alog_ref[...]) * jax.nn.softplus(ab + dtb_ref[...])
    beta_all = jax.nn.sigmoid(ab)
    row = lax.broadcasted_iota(jnp.int32, (L, L), 0)
    col = lax.broadcasted_iota(jnp.int32, (L, L), 1)
    incl = row >= col
    strict = row > col
    brow = lax.broadcasted_iota(jnp.int32, (rows, rows), 0)
    bcol = lax.broadcasted_iota(jnp.int32, (rows, rows), 1)
    incl_blk = jnp.logical_and(brow >= bcol, brow // L == bcol // L)
    gam_all = jnp.dot(incl_blk.astype(F32), g_all, preferred_element_type=F32, precision=HI)
    eye_l = (lax.broadcasted_iota(jnp.int32, (LANES, LANES), 0)
             == lax.broadcasted_iota(jnp.int32, (LANES, LANES), 1))
    gam_t = _dot_nt(eye_l.astype(F32), gam_all, precision=HI)
    for ci in range(nc):
        ee_ref[0, ci] = jnp.exp(gam_all[(ci + 1) * L - 1:(ci + 1) * L])
    eye64 = (row == col).astype(F32)
    eye_bf = eye_l.astype(BF16)

    ch = [(ci, h) for ci in range(nc) for h in range(GDN_HEADS)]
    cs = range(len(ch))
    koff, voff = GDN_HEADS * GDN_DK, 2 * GDN_HEADS * GDN_DK
    qh = [y[ci * L:(ci + 1) * L, GDN_DK * h:GDN_DK * (h + 1)] for ci, h in ch]
    kh = [y[ci * L:(ci + 1) * L, koff + GDN_DK * h:koff + GDN_DK * (h + 1)] for ci, h in ch]
    vh = [y[ci * L:(ci + 1) * L, voff + GDN_DV * h:voff + GDN_DV * (h + 1)] for ci, h in ch]
    qh = [q * lax.rsqrt(jnp.sum(q * q, axis=-1, keepdims=True) + EPS) * (GDN_DK ** -0.5) for q in qh]
    kh = [k * lax.rsqrt(jnp.sum(k * k, axis=-1, keepdims=True) + EPS) for k in kh]
    gam_c = [gam_all[ci * L:(ci + 1) * L, h:h + 1] for ci, h in ch]
    beta_c = [beta_all[ci * L:(ci + 1) * L, GDN_HEADS + h:GDN_HEADS + h + 1] for ci, h in ch]
    dec = [jnp.exp(jnp.minimum(gam_c[c] - gam_t[h:h + 1, ci * L:(ci + 1) * L], 0.0))
           for c, (ci, h) in enumerate(ch)]
    kb = [k.astype(BF16) for k in kh]
    kk = [_dot_nt(kb[c], kb[c]) for c in cs]
    qk = [_dot_nt(qh[c].astype(BF16), kb[c]) for c in cs]
    a = [jnp.where(strict, beta_c[c] * kk[c] * dec[c], 0.0) for c in cs]
    for c, (ci, h) in enumerate(ch):
        pm_ref[0, ci, h] = jnp.where(incl, qk[c] * dec[c], 0.0).astype(BF16)
    tinv = [eye64 - a[c] for c in cs]
    pw = [_split(a[c]) for c in cs]
    for _ in range(max(L.bit_length() - 2, 0)):
        pw = [_split(_dot3(pw[c], pw[c])) for c in cs]
        tinv = [tinv[c] + _dot3(_split(tinv[c]), pw[c]) for c in cs]
    eg = [jnp.exp(gam_c[c]) for c in cs]
    rhs = [jnp.concatenate([(beta_c[c] * eg[c]) * kh[c], beta_c[c] * vh[c]], axis=1).astype(BF16) for c in cs]
    wu = [_dot(tinv[c].astype(BF16), rhs[c]) for c in cs]
    for c, (ci, h) in enumerate(ch):
        wq_ref[0, ci, h] = jnp.concatenate([wu[c][:, :GDN_DK], qh[c] * eg[c]], axis=0).astype(BF16)
        u_ref[0, h, ci * L:(ci + 1) * L, :] = wu[c][:, GDN_DK:].astype(BF16)
    kg = [(kh[c] * jnp.exp(gam_all[(ci + 1) * L - 1:(ci + 1) * L, h:h + 1] - gam_c[c])).astype(BF16)
          for c, (ci, h) in enumerate(ch)]
    kgt = [_dot_nt(eye_bf, kg[c]) for c in cs]
    for c, (ci, h) in enumerate(ch):
        kgt_ref[0, ci, h] = kgt[c].astype(BF16)


def _split(a):
    hi = a.astype(BF16)
    return hi, (a - hi.astype(F32)).astype(BF16)


def _dot3(a, b):
    return _dot(a[0], b[0]) + (_dot(a[0], b[1]) + _dot(a[1], b[0]))


GDN_PREP_CHUNKS = 2


def _gdn_prep_call(qkv, st8, conv_w, ab, alog, dtb, L):
    b, s, _ = qkv.shape
    nch = s // L
    nc = GDN_PREP_CHUNKS if nch % GDN_PREP_CHUNKS == 0 else 1
    rows = nc * L
    return pl.pallas_call(
        functools.partial(_gdn_prep_kernel, L=L, nc=nc),
        grid=(b, nch // nc),
        in_specs=[_tok(GDN_QKV, rows),
                  pl.BlockSpec((1, 8, GDN_QKV), lambda bb, j: (bb, jnp.maximum(j * (rows // 8) - 1, 0), 0)),
                  pl.BlockSpec((1, 8, GDN_QKV), lambda bb, j: (bb, 0, 0)),
                  _full(conv_w), _tok(LANES, rows), _full(alog), _full(dtb)],
        out_specs=[pl.BlockSpec((1, nc, GDN_HEADS, 2 * L, LANES), lambda bb, j: (bb, j, 0, 0, 0)),
                   pl.BlockSpec((1, GDN_HEADS, rows, LANES), lambda bb, j: (bb, 0, j, 0)),
                   pl.BlockSpec((1, nc, GDN_HEADS, GDN_DK, L), lambda bb, j: (bb, j, 0, 0, 0)),
                   pl.BlockSpec((1, nc, GDN_HEADS, L, L), lambda bb, j: (bb, j, 0, 0, 0)),
                   pl.BlockSpec((1, nc, 1, LANES), lambda bb, j: (bb, j, 0, 0))],
        out_shape=[jax.ShapeDtypeStruct((b, nch, GDN_HEADS, 2 * L, LANES), BF16),
                   jax.ShapeDtypeStruct((b, GDN_HEADS, s, LANES), BF16),
                   jax.ShapeDtypeStruct((b, nch, GDN_HEADS, GDN_DK, L), BF16),
                   jax.ShapeDtypeStruct((b, nch, GDN_HEADS, L, L), BF16),
                   jax.ShapeDtypeStruct((b, nch, 1, LANES), F32)],
        compiler_params=_cparams("parallel", "parallel"),
        name="gdn_prep",
    )(qkv, qkv, st8, conv_w, ab, alog, dtb)


GDN_BATCH_GROUP = 2


GDN_SEQ_CHUNKS = 2


def _gdn_seq_kernel(wq_ref, u_ref, kgt_ref, pm_ref, ee_ref, s0_ref, o_ref, sf_ref, s_scr, *, L, nc):
    c = pl.program_id(1)

    @pl.when(c == 0)
    def _():
        s_scr[...] = s0_ref[...]

    ch = [(bb, hh) for bb in range(GDN_BATCH_GROUP) for hh in range(GDN_HEADS)]
    st = [s_scr[bb, hh] for bb, hh in ch]
    for ci in range(nc):
        rows = slice(ci * L, (ci + 1) * L)
        sp = [_split(s) for s in st]
        r = [_dot(wq_ref[bb, ci, hh], sp[i][0]) + _dot(wq_ref[bb, ci, hh], sp[i][1]) for i, (bb, hh) in enumerate(ch)]
        v_new = [(u_ref[bb, hh, rows, :].astype(F32) - r[i][:L]).astype(BF16) for i, (bb, hh) in enumerate(ch)]
        o = [r[i][L:] + _dot(pm_ref[bb, ci, hh], v_new[i]) for i, (bb, hh) in enumerate(ch)]
        for i, (bb, hh) in enumerate(ch):
            o_ref[bb, rows, GDN_DV * hh:GDN_DV * (hh + 1)] = o[i].astype(o_ref.dtype)
        st = [ee_ref[bb, ci][:, hh:hh + 1] * st[i] + _dot(kgt_ref[bb, ci, hh], v_new[i])
              for i, (bb, hh) in enumerate(ch)]
    for i, (bb, hh) in enumerate(ch):
        s_scr[bb, hh] = st[i]

    @pl.when(c == pl.num_programs(1) - 1)
    def _():
        sf_ref[...] = s_scr[...]


def _gdn_seq_call(wq, um, kgt, pm, ee, s0, L):
    b, h, s, _ = um.shape
    nch = s // L
    bg = GDN_BATCH_GROUP
    nc = GDN_SEQ_CHUNKS if nch % GDN_SEQ_CHUNKS == 0 else 1
    sspec = pl.BlockSpec((bg, h, GDN_DK, GDN_DV), lambda g, c: (g, 0, 0, 0))
    return pl.pallas_call(
        functools.partial(_gdn_seq_kernel, L=L, nc=nc),
        grid=(b // bg, nch // nc),
        in_specs=[pl.BlockSpec((bg, nc, h, 2 * L, LANES), lambda g, c: (g, c, 0, 0, 0)),
                  pl.BlockSpec((bg, h, nc * L, LANES), lambda g, c: (g, 0, c, 0)),
                  pl.BlockSpec((bg, nc, h, GDN_DK, L), lambda g, c: (g, c, 0, 0, 0)),
                  pl.BlockSpec((bg, nc, h, L, L), lambda g, c: (g, c, 0, 0, 0)),
                  pl.BlockSpec((bg, nc, 1, LANES), lambda g, c: (g, c, 0, 0)),
                  sspec],
        out_specs=[pl.BlockSpec((bg, nc * L, GDN_VW), lambda g, c: (g, c, 0)), sspec],
        out_shape=[jax.ShapeDtypeStruct((b, s, GDN_VW), BF16),
                   jax.ShapeDtypeStruct((b, h, GDN_DK, GDN_DV), F32)],
        scratch_shapes=[pltpu.VMEM((bg, h, GDN_DK, GDN_DV), F32)],
        compiler_params=_cparams("parallel", "arbitrary"),
        name="gdn_seq",
    )(wq, um, kgt, pm, ee, s0)


def _c_out_kernel(x_ref, o_ref, z_ref, gn_ref, w_ref, g1_ref, *route_refs):
    out_ref = route_refs[5]
    o = o_ref[0].astype(F32)
    z = z_ref[0].astype(F32)
    gn = gn_ref[...]
    parts = []
    for hh in range(GDN_HEADS):
        oh = o[:, GDN_DV * hh:GDN_DV * (hh + 1)]
        zh = z[:, GDN_DV * hh:GDN_DV * (hh + 1)]
        parts.append((_rms(oh, gn) * (zh * jax.nn.sigmoid(zh))).astype(BF16))
    out = _dot(jnp.concatenate(parts, axis=1), w_ref[...])
    xn = x_ref[0] + g1_ref[0] * out
    out_ref[0] = xn
    _route(xn, *route_refs[:5], *route_refs[6:])


def _c_out_call(x, o, z, gn, w, g1, rt, ts):
    b, s, d = x.shape
    r_ins, r_specs, r_ospecs, r_oshape, r_scratch = _route_plumbing(rt, b, s, d, ts)
    return pl.pallas_call(
        _c_out_kernel,
        grid=(b, s // ts),
        in_specs=[_tok(d, ts), _tok(GDN_VW, ts), _tok(GDN_VW, ts), _full(gn), _full(w), _modspec(g1, ts)] + r_specs,
        out_specs=[_tok(d, ts)] + r_ospecs,
        out_shape=[jax.ShapeDtypeStruct((b, s, d), F32)] + r_oshape,
        scratch_shapes=r_scratch,
        compiler_params=_cparams("arbitrary", "arbitrary"),
        name="c_out",
    )(x, o, z, gn, w, g1, *r_ins)


def _pack_halves(v):
    n = v.shape[1] // 2
    bits = pltpu.bitcast(v.astype(BF16).astype(F32), jnp.uint32)
    return (bits[:, :n] >> 16) | (bits[:, n:] & jnp.uint32(0xFFFF0000))


def _unpack_halves(w):
    return pltpu.bitcast(w << 16, F32), pltpu.bitcast(w & jnp.uint32(0xFFFF0000), F32)


def _route(xn, gain_ref, sc_ref, sh_ref, wr_ref, br_ref, h_ref, re_ref, rg_ref, cnt_ref, carry):
    @pl.when(jnp.logical_and(pl.program_id(0) == 0, pl.program_id(1) == 0))
    def _():
        carry[...] = jnp.zeros_like(carry)

    h = _normmod(xn, gain_ref[...], sc_ref[0], sh_ref[0])
    h_ref[0] = _pack_halves(h)
    logits = _dot3(_split(h), (wr_ref[0], wr_ref[1])) + br_ref[...]
    lane = lax.broadcasted_iota(jnp.int32, logits.shape, 1).astype(F32)
    neg = jnp.float32(-jnp.inf)
    big = jnp.float32(1e9)
    gl = jnp.where(lane < N_GROUPS, logits, neg)
    gm = jnp.max(gl, axis=-1, keepdims=True)
    g_val = 1.0 / jnp.sum(jnp.exp(gl - gm), axis=-1, keepdims=True)
    g_idx = jnp.min(jnp.where(gl == gm, lane, big), axis=-1, keepdims=True)
    lo = N_GROUPS + EXPERTS_PER_GROUP * g_idx
    el = jnp.where(jnp.logical_and(lane >= lo, lane < lo + EXPERTS_PER_GROUP), logits, neg)
    em = jnp.max(el, axis=-1, keepdims=True)
    i1 = jnp.min(jnp.where(el == em, lane, big), axis=-1, keepdims=True)
    el2 = jnp.where(lane == i1, neg, el)
    em2 = jnp.max(el2, axis=-1, keepdims=True)
    i2 = jnp.min(jnp.where(el2 == em2, lane, big), axis=-1, keepdims=True)
    es = jnp.sum(jnp.exp(el - em), axis=-1, keepdims=True)
    p1 = 1.0 / es
    p2 = jnp.exp(em2 - em) / es
    den = p1 + p2
    rg_ref[0] = jnp.where(lane == 0, g_val * p1 / den, jnp.where(lane == 1, g_val * p2 / den, 0.0))
    oh1 = lane == i1
    oh2 = lane == i2
    both = jnp.where(jnp.logical_or(oh1, oh2), 1.0, 0.0)
    ts = both.shape[0]
    tri = (lax.broadcasted_iota(jnp.int32, (ts, ts), 0) > lax.broadcasted_iota(jnp.int32, (ts, ts), 1))
    pre = _dot(jnp.where(tri, 1.0, 0.0).astype(BF16), both.astype(BF16)) + carry[...]
    r1 = jnp.sum(jnp.where(oh1, pre, 0.0), axis=-1, keepdims=True)
    r2 = jnp.sum(jnp.where(oh2, pre, 0.0), axis=-1, keepdims=True)
    re_ref[0] = jnp.where(lane == 0, i1 - N_GROUPS, jnp.where(lane == 1, i2 - N_GROUPS, jnp.where(
        lane == 2, r1, jnp.where(lane == 3, r2, 0.0)))).astype(jnp.int32)
    carry[...] += jnp.sum(both, axis=0, keepdims=True)
    cnt_ref[...] = carry[...]


def _route_plumbing(rt, b, s, d, ts):
    gain, sc, sh, wr, br = rt
    ins = [gain, sc, sh, wr, br]
    in_specs = [_full(gain), _modspec(sc, ts), _modspec(sh, ts), _full(wr), _full(br)]
    out_specs = [_tok(d // 2, ts), _tok(LANES, ts), _tok(LANES, ts), pl.BlockSpec((1, LANES), lambda bb, j: (0, 0))]
    out_shape = [jax.ShapeDtypeStruct((b, s, d // 2), jnp.uint32),
                 jax.ShapeDtypeStruct((b, s, LANES), jnp.int32),
                 jax.ShapeDtypeStruct((b, s, LANES), F32),
                 jax.ShapeDtypeStruct((1, LANES), F32)]
    return ins, in_specs, out_specs, out_shape, [pltpu.VMEM((1, LANES), F32)]


def _row_copy(src, dst, src_row, dst_row, sem):
    return pltpu.make_async_copy(src.at[pl.ds(src_row, 1)], dst.at[pl.ds(dst_row, 1)], sem)


ROUTE_W = 2 * TOP_K
DMA_UNROLL = 8


def _slot(ps_ref, rt_ref, r, k):
    return ps_ref[rt_ref[0, 0, ROUTE_W * r + k]] + rt_ref[0, 0, ROUTE_W * r + TOP_K + k]


def _dispatch_kernel(ps_ref, rt_ref, h_ref, xin_hbm, xb_hbm, sem, *, tb):
    del xin_hbm

    def issue(r, c):
        for k in range(TOP_K):
            _row_copy(h_ref, xb_hbm, r, _slot(ps_ref, rt_ref, r, k), sem).start()
        return c
    lax.fori_loop(0, tb, issue, 0, unroll=DMA_UNROLL)
    for k in range(TOP_K):
        pltpu.make_async_copy(h_ref, xb_hbm.at[pl.ds(0, tb)], sem).wait()


def _dispatch_call(pad_start, route, h2, xb0, tb):
    n, dh = h2.shape
    nb = n // tb
    grid_spec = pltpu.PrefetchScalarGridSpec(
        num_scalar_prefetch=1,
        grid=(nb,),
        in_specs=[pl.BlockSpec((1, 1, ROUTE_W * tb), lambda j, ps: (j, 0, 0), memory_space=pltpu.SMEM),
                  pl.BlockSpec((tb, dh), lambda j, ps: (j, 0)),
                  pl.BlockSpec(memory_space=pl.ANY)],
        out_specs=pl.BlockSpec(memory_space=pl.ANY),
        scratch_shapes=[pltpu.SemaphoreType.DMA(())],
    )
    return pl.pallas_call(
        functools.partial(_dispatch_kernel, tb=tb),
        grid_spec=grid_spec,
        out_shape=jax.ShapeDtypeStruct(xb0.shape, xb0.dtype),
        input_output_aliases={3: 0},
        compiler_params=_cparams("arbitrary"),
        name="moe_dispatch",
    )(pad_start, route, h2, xb0)


def _ffn_kernel(be_ref, nu_ref, x_ref, w1_ref, w3_ref, w2_ref, y_ref, w1b, w3b, w2b):
    i = pl.program_id(0)
    half = w1b.shape[0] // 2

    @pl.when(jnp.logical_and(i < nu_ref[0], jnp.logical_or(i == 0, be_ref[i] != be_ref[jnp.maximum(i - 1, 0)])))
    def _():
        w1b[...] = w1_ref[0, 0].astype(BF16)
        w3b[...] = w3_ref[0, 0].astype(BF16)
        w2b[...] = w2_ref[0, 0].astype(BF16)

    @pl.when(i < nu_ref[0])
    def _():
        lo, hi = _unpack_halves(x_ref[...])
        lo = lo.astype(BF16)
        hi = hi.astype(BF16)
        a = _dot(lo, w1b[:half]) + _dot(hi, w1b[half:])
        g = _dot(lo, w3b[:half]) + _dot(hi, w3b[half:])
        mid = (a * jax.nn.sigmoid(a) * g).astype(BF16)
        y_ref[...] = _pack_halves(_dot(mid, w2b[...]))

    @pl.when(i >= nu_ref[0])
    def _():
        y_ref[...] = jnp.zeros_like(y_ref)


def _ffn_call(block_e, n_used, xb, w1, w3, w2, layer):
    n_blocks = block_e.shape[0]
    dh = xb.shape[1]
    d = 2 * dh
    blk = xb.shape[0] // n_blocks
    grid_spec = pltpu.PrefetchScalarGridSpec(
        num_scalar_prefetch=2,
        grid=(n_blocks,),
        in_specs=[pl.BlockSpec((blk, dh), lambda i, be, nu: (i, 0)),
                  pl.BlockSpec((1, 1, d, D_EXPERT), lambda i, be, nu: (layer, be[i], 0, 0)),
                  pl.BlockSpec((1, 1, d, D_EXPERT), lambda i, be, nu: (layer, be[i], 0, 0)),
                  pl.BlockSpec((1, 1, D_EXPERT, d), lambda i, be, nu: (layer, be[i], 0, 0))],
        out_specs=pl.BlockSpec((blk, dh), lambda i, be, nu: (i, 0)),
        scratch_shapes=[pltpu.VMEM((d, D_EXPERT), BF16), pltpu.VMEM((d, D_EXPERT), BF16),
                        pltpu.VMEM((D_EXPERT, d), BF16)],
    )
    return pl.pallas_call(
        _ffn_kernel,
        grid_spec=grid_spec,
        out_shape=jax.ShapeDtypeStruct(xb.shape, jnp.uint32),
        compiler_params=_cparams("arbitrary"),
        name="moe_ffn",
    )(block_e, n_used, xb, w1, w3, w2)


def _combine_kernel(ps_ref, rcur_ref, rnxt_ref, yb_hbm, x_ref, rg_ref, g2_ref, fin_ref, o_ref, b0, b1, sem,
                    *, tb, final):
    j = pl.program_id(0)
    nb = pl.num_programs(0)
    slot = j % 2

    def fetch(rref, sl):
        def issue(r, c):
            _row_copy(yb_hbm, b0.at[sl], _slot(ps_ref, rref, r, 0), r, sem.at[sl]).start()
            _row_copy(yb_hbm, b1.at[sl], _slot(ps_ref, rref, r, 1), r, sem.at[sl]).start()
            return c
        lax.fori_loop(0, tb, issue, 0, unroll=DMA_UNROLL)

    @pl.when(j == 0)
    def _():
        fetch(rcur_ref, 0)

    @pl.when(j + 1 < nb)
    def _():
        fetch(rnxt_ref, 1 - slot)

    pltpu.make_async_copy(yb_hbm.at[pl.ds(0, tb)], b0.at[slot], sem.at[slot]).wait()
    pltpu.make_async_copy(yb_hbm.at[pl.ds(0, tb)], b1.at[slot], sem.at[slot]).wait()
    rg = rg_ref[...]
    lo0, hi0 = _unpack_halves(b0[slot])
    lo1, hi1 = _unpack_halves(b1[slot])
    g0 = rg[:, 0:1]
    g1 = rg[:, 1:2]
    y = jnp.concatenate([g0 * lo0 + g1 * lo1, g0 * hi0 + g1 * hi1], axis=1)
    xn = x_ref[...] + g2_ref[0] * y
    if final:
        xn = _rms(xn, fin_ref[...])
    o_ref[...] = xn


def _combine_call(pad_start, route, yb, x, rg, g2, fin, tb, final):
    b, s, d = x.shape
    n = b * s
    nb = n // tb
    npb = s // tb
    dh = yb.shape[1]
    if g2.shape[1] == 1:
        gspec = pl.BlockSpec((1, 1, d), lambda j, ps: (j // npb, 0, 0))
    else:
        gspec = pl.BlockSpec((1, tb, d), lambda j, ps: (j // npb, j % npb, 0))
    grid_spec = pltpu.PrefetchScalarGridSpec(
        num_scalar_prefetch=1,
        grid=(nb,),
        in_specs=[pl.BlockSpec((1, 1, ROUTE_W * tb), lambda j, ps: (j, 0, 0), memory_space=pltpu.SMEM),
                  pl.BlockSpec((1, 1, ROUTE_W * tb), lambda j, ps: (jnp.minimum(j + 1, nb - 1), 0, 0),
                               memory_space=pltpu.SMEM),
                  pl.BlockSpec(memory_space=pl.ANY),
                  pl.BlockSpec((tb, d), lambda j, ps: (j, 0)), pl.BlockSpec((tb, LANES), lambda j, ps: (j, 0)),
                  gspec, pl.BlockSpec(fin.shape, lambda j, ps: (0, 0))],
        out_specs=pl.BlockSpec((tb, d), lambda j, ps: (j, 0)),
        scratch_shapes=[pltpu.VMEM((2, tb, dh), jnp.uint32), pltpu.VMEM((2, tb, dh), jnp.uint32),
                        pltpu.SemaphoreType.DMA((2,))],
    )
    out = pl.pallas_call(
        functools.partial(_combine_kernel, tb=tb, final=final),
        grid_spec=grid_spec,
        out_shape=jax.ShapeDtypeStruct((n, d), F32),
        compiler_params=_cparams("arbitrary"),
        name="moe_combine",
    )(pad_start, route, route, yb, x.reshape(n, d), rg.reshape(n, LANES), g2, fin)
    return out.reshape(b, s, d)


def _moe(x, routed, g2, mp, fin, ts, final):
    b, s, d = x.shape
    n = b * s
    h2, r_e, r_g, cnt = routed
    counts = cnt[0, N_GROUPS:N_GROUPS + N_EXPERTS].astype(jnp.int32)
    blk = MOE_BLOCK if n * TOP_K >= N_EXPERTS * MOE_BLOCK else MOE_BLOCK_SMALL
    padded = (counts + blk - 1) // blk * blk
    pad_end = jnp.cumsum(padded)
    pad_start = (pad_end - padded).astype(jnp.int32)
    n_blocks = (n * TOP_K + N_EXPERTS * (blk - 1) + blk - 1) // blk
    blk0 = jnp.arange(n_blocks, dtype=jnp.int32) * blk
    block_e = jnp.minimum(jnp.sum((pad_end[None, :] <= blk0[:, None]).astype(jnp.int32), axis=1), N_EXPERTS - 1)
    n_used = (pad_end[-1:] // blk).astype(jnp.int32)
    route = r_e[:, :, :ROUTE_W].reshape(n // ts, 1, ROUTE_W * ts)
    xb0 = jnp.zeros((n_blocks * blk, d // 2), jnp.uint32)
    xb = _dispatch_call(pad_start, route, h2.reshape(n, d // 2), xb0, ts)
    yb = _ffn_call(block_e, n_used, xb, mp["w1"], mp["w3"], mp["w2"], mp["layer"])
    return _combine_call(pad_start, route, yb, x, r_g, g2, fin, ts, final)


def _prep_ab(l, p):
    w_in = p["w_in_ab"][l]
    o2 = S5_WIDTH + MLA_Q_RANK + MLA_KV_RANK
    half = MLA_ROPE // 2
    d = w_in.shape[0]
    kr = w_in[:, o2:]
    kr_rot = jnp.concatenate([-kr[:, half:], kr[:, :half]], axis=1)
    z64 = jnp.zeros((d, MLA_NOPE), F32)
    z32 = jnp.zeros((d, LANES - MLA_NOPE - MLA_ROPE), F32)
    w_ext = jnp.concatenate([w_in[:, :o2], z64, kr, z32, z64, kr_rot, z32], axis=1)
    qu = p["mla_q_up"][l].reshape(MLA_Q_RANK, MLA_HEADS, MLA_NOPE + MLA_ROPE)
    qn, qr = qu[:, :, :MLA_NOPE], qu[:, :, MLA_NOPE:]
    zq = jnp.zeros((MLA_Q_RANK, MLA_HEADS, LANES - MLA_NOPE - MLA_ROPE), F32)
    wqa = jnp.concatenate([qn, qr, zq], axis=2).reshape(MLA_Q_RANK, MLA_HEADS * LANES)
    wqb = jnp.concatenate([jnp.zeros_like(qn), -qr[:, :, half:], qr[:, :, :half], zq], axis=2)
    wqb = wqb.reshape(MLA_Q_RANK, MLA_HEADS * LANES)
    kvu = p["mla_kv_up"][l].reshape(MLA_KV_RANK, MLA_HEADS, MLA_NOPE + MLA_V)
    wk = jnp.concatenate([kvu[:, :, :MLA_NOPE], jnp.zeros((MLA_KV_RANK, MLA_HEADS, LANES - MLA_NOPE), F32)], axis=2)
    wo = p["w_out_ab"][l]
    out = dict(
        w_in=w_ext.astype(BF16), q_norm=p["mla_q_norm"][l].reshape(1, -1), kv_norm=p["mla_kv_norm"][l].reshape(1, -1),
        wqa=wqa.astype(BF16), wqb=wqb.astype(BF16),
        wk=wk.reshape(MLA_KV_RANK, MLA_HEADS * LANES).astype(BF16),
        wv=jnp.concatenate([kvu[:, :, MLA_NOPE:], jnp.zeros((MLA_KV_RANK, MLA_HEADS, LANES - MLA_V), F32)],
                           axis=2).reshape(MLA_KV_RANK, MLA_HEADS * LANES).astype(BF16),
        s5_d=p["s5_d"][l].reshape(1, -1), glu_w=p["s5_glu_w"][l].astype(BF16), glu_b=p["s5_glu_b"][l].reshape(1, -1),
        w_out_s5=wo[:S5_WIDTH].astype(BF16),
        w_out_at=wo[S5_WIDTH:].astype(BF16))
    out.update(_s5_mats(p["s5_a_re"][l], p["s5_a_im"][l], p["s5_b_re"][l], p["s5_b_im"][l],
                        p["s5_c_re"][l], p["s5_c_im"][l], p["s5_log_dt"][l]))
    return out


def _prep_c(l, p):
    w = p["w_in_c"][l]
    pad = jnp.zeros((w.shape[0], LANES - 2 * GDN_HEADS), F32)

    def row(v):
        return jnp.concatenate([v, jnp.zeros((LANES - GDN_HEADS,), F32)]).reshape(1, LANES)

    return dict(w_in=jnp.concatenate([w, pad], axis=1).astype(BF16), conv_w=p["conv_w"][l],
                alog=row(p["gdn_a_log"][l]), dtb=row(p["gdn_dt_bias"][l]),
                gn=p["gdn_norm"][l].reshape(1, -1), w_out=p["w_out_c"][l].astype(BF16))


def _prep_moe(layer, p):
    d = p["moe_w_group"].shape[1]
    wr = jnp.concatenate([p["moe_w_group"][layer], p["moe_w_expert"][layer],
                          jnp.zeros((d, LANES - N_GROUPS - N_EXPERTS), F32)], axis=1)
    br = jnp.concatenate([p["moe_b_group"][layer], p["moe_b_expert"][layer],
                          jnp.zeros((LANES - N_GROUPS - N_EXPERTS,), F32)]).reshape(1, LANES)
    wr_hi = wr.astype(BF16)
    wr_lo = (wr - wr_hi.astype(F32)).astype(BF16)
    return dict(wr=jnp.stack([wr_hi, wr_lo]), br=br, w1=p["moe_w1"], w3=p["moe_w3"], w2=p["moe_w2"], layer=layer)


def _rope_tables(pos):
    half = MLA_ROPE // 2
    inv = ROPE_THETA ** (-jnp.arange(half, dtype=F32) / half)
    ang = pos.astype(F32)[:, None] * inv[None, :]
    cos, sin = jnp.cos(ang), jnp.sin(ang)
    n = pos.shape[0]
    z64 = jnp.zeros((n, MLA_NOPE), F32)
    z32 = jnp.zeros((n, LANES - MLA_NOPE - MLA_ROPE), F32)
    scale = (MLA_NOPE + MLA_ROPE) ** -0.5 * math.log2(math.e)
    ck = jnp.concatenate([z64, cos, cos, z32], axis=1)
    sk = jnp.concatenate([z64, sin, sin, z32], axis=1)
    cq = jnp.concatenate([jnp.ones_like(z64), cos, cos, z32], axis=1) * scale
    return cq, sk * scale, ck, sk


def _trunk(x, mods, caches, wab, wc, wmoe, p, flat):
    b, s, d = x.shape
    lat_c, rope_c, s5re_c, s5im_c, conv_c, gdn_c = caches
    past = 0 if lat_c is None else lat_c.shape[2]
    pos = past + jnp.arange(s)
    if flat:
        xt = x.reshape(1, b * s, d)
        ts = b * s
        pos_rows = jnp.tile(pos, b)

        def mrow(m):
            return jnp.repeat(m, s, axis=0).reshape(1, b * s, d)
    else:
        xt = x
        ts = TOKEN_BLOCK
        pos_rows = pos

        def mrow(m):
            return m.reshape(b, 1, d)
    bt, st = xt.shape[:2]
    outs = {}
    depth = p["norm_mix"].shape[0]
    for layer in range(depth):
        l = layer // 2
        sh1, sc1, g1, sh2, sc2, g2 = [mrow(m) for m in jnp.split(mods[layer], 6, axis=-1)]
        gain = p["norm_mix"][layer].reshape(1, d)
        rt = (p["norm_ffn"][layer].reshape(1, d), sc2, sh2, wmoe[layer]["wr"], wmoe[layer]["br"])
        if layer % 2 == 0:
            wp = wab[l]
            u, q, lat, kr = _ab_in_call(xt, gain, sc1, sh1, wp, _rope_tables(pos_rows), ts)
            lat_b = lat.reshape(b, s, MLA_KV_RANK)
            kr_b = kr.reshape(b, s, LANES)
            outs["lat"] = lat_b
            outs["krope"] = kr_b[:, :, MLA_NOPE:MLA_NOPE + MLA_ROPE]
            if lat_c is None:
                lat_all, kr_all, sk_valid, tkv = lat_b, kr_b, s, TOKEN_BLOCK
                tq, tk = min(ATTN_TQ, s), min(ATTN_TK, s)
            else:
                krc = jnp.pad(rope_c[l], ((0, 0), (0, 0), (MLA_NOPE, LANES - MLA_NOPE - MLA_ROPE)))
                sk_valid = past + s
                skp = (sk_valid + LANES - 1) // LANES * LANES
                lat_all = jnp.pad(jnp.concatenate([lat_c[l], lat_b], axis=1), ((0, 0), (0, skp - sk_valid), (0, 0)))
                kr_all = jnp.pad(jnp.concatenate([krc, kr_b], axis=1), ((0, 0), (0, skp - sk_valid), (0, 0)))
                tkv, tq, tk = skp, s, skp
            kk, vv = _kv_call(lat_all, kr_all, wp, tkv)
            qh = q.reshape(MLA_HEADS, b, s, LANES).transpose(1, 0, 2, 3) if flat else q
            attn = _attn_call(qh, kk, vv, tq, tk, past, sk_valid).reshape(bt, st, MLA_HEADS * MLA_V)
            nchk = s // S5_CHUNK
            r = b * nchk
            u2 = u.reshape(b * s, S5_WIDTH)
            tr = min(r, S5_ROW_BLOCK)
            wre, wim = _s5_w_call(u2, wp["bre"], wp["bim"], tr)
            ngp = S5_GROUPS * S5_STATE

            def to_pairs(v):
                return v.reshape(b, 1, ngp)

            if s5re_c is None:
                x0re = jnp.zeros((b, 1, ngp), F32)
                x0im = jnp.zeros((b, 1, ngp), F32)
            else:
                x0re, x0im = to_pairs(s5re_c[l]), to_pairs(s5im_c[l])
            nchp = (nchk + 7) // 8 * 8
            tc = min(nchp, S5_SCAN_BLOCK)
            wre3 = jnp.pad(wre.reshape(b, nchk, ngp), ((0, 0), (0, nchp - nchk), (0, 0)))
            wim3 = jnp.pad(wim.reshape(b, nchk, ngp), ((0, 0), (0, nchp - nchk), (0, 0)))
            xere, xeim = _s5_scan_call(wre3, wim3, wp["lre"], wp["lim"], x0re, x0im, tc)
            outs["s5re"] = xere[:, nchk - 1].reshape(b, S5_GROUPS, S5_STATE)
            outs["s5im"] = xeim[:, nchk - 1].reshape(b, S5_GROUPS, S5_STATE)
            xsre = jnp.concatenate([x0re, xere[:, :nchk - 1]], axis=1).reshape(r, ngp)
            xsim = jnp.concatenate([x0im, xeim[:, :nchk - 1]], axis=1).reshape(r, ngp)
            ys = _s5_y_call(u2, xsre, xsim, wp["mst"], wp["cc"], tr).reshape(bt, st, S5_WIDTH)
            xt, *routed = _ab_out_call(xt, ys, u, attn, wp, g1, rt, ts)
        else:
            wp = wc[l]
            qkv, z, ab = _c_in_call(xt, gain, sc1, sh1, wp["w_in"], ts)
            qkv_b = qkv.reshape(b, s, GDN_QKV)
            outs["conv"] = qkv_b[:, s - (CONV_W - 1):].astype(F32)
            if conv_c is None:
                st8 = jnp.zeros((b, 8, GDN_QKV), BF16)
                s0 = jnp.zeros((b, GDN_HEADS, GDN_DK, GDN_DV), F32)
            else:
                st8 = jnp.pad(conv_c[l], ((0, 0), (8 - (CONV_W - 1), 0), (0, 0))).astype(BF16)
                s0 = gdn_c[l]
            lg = min(s, CHUNK)
            wq, um, kgt, pm, ee = _gdn_prep_call(qkv_b, st8, wp["conv_w"], ab.reshape(b, s, LANES),
                                                 wp["alog"], wp["dtb"], lg)
            o, sfin = _gdn_seq_call(wq, um, kgt, pm, ee, s0, lg)
            outs["gdn"] = sfin
            xt, *routed = _c_out_call(xt, o.reshape(bt, st, GDN_VW), z, wp["gn"], wp["w_out"], g1, rt, ts)
        final = layer == depth - 1
        xt = _moe(xt, routed, g2, wmoe[layer], p["norm_final"].reshape(1, d), ts, final)
    y = xt.reshape(b, s, d)
    return (y, outs["lat"][None], outs["krope"][None], outs["s5re"][None], outs["s5im"][None],
            outs["conv"][None], outs["gdn"][None])


def kernel(x_prompt, x_sample, c_prompt, c_sample, cache_mla_latent, cache_mla_krope, state_s5_re, state_s5_im,
           state_conv, state_gdn, w_mod, b_mod, norm_mix, norm_ffn, norm_final, w_in_ab, s5_a_re, s5_a_im,
           s5_b_re, s5_b_im, s5_c_re, s5_c_im, s5_d, s5_log_dt, s5_glu_w, s5_glu_b, mla_q_norm, mla_q_up,
           mla_kv_norm, mla_kv_up, w_out_ab, w_in_c, conv_w, gdn_a_log, gdn_dt_bias, gdn_norm, w_out_c,
           moe_w_group, moe_b_group, moe_w_expert, moe_b_expert, moe_w1, moe_w3, moe_w2):
    p = dict(w_mod=w_mod, b_mod=b_mod, norm_mix=norm_mix, norm_ffn=norm_ffn, norm_final=norm_final,
             w_in_ab=w_in_ab, s5_a_re=s5_a_re, s5_a_im=s5_a_im, s5_b_re=s5_b_re, s5_b_im=s5_b_im,
             s5_c_re=s5_c_re, s5_c_im=s5_c_im, s5_d=s5_d, s5_log_dt=s5_log_dt, s5_glu_w=s5_glu_w,
             s5_glu_b=s5_glu_b, mla_q_norm=mla_q_norm, mla_q_up=mla_q_up, mla_kv_norm=mla_kv_norm,
             mla_kv_up=mla_kv_up, w_out_ab=w_out_ab, w_in_c=w_in_c, conv_w=conv_w, gdn_a_log=gdn_a_log,
             gdn_dt_bias=gdn_dt_bias, gdn_norm=gdn_norm, w_out_c=w_out_c, moe_w_group=moe_w_group,
             moe_b_group=moe_b_group, moe_w_expert=moe_w_expert, moe_b_expert=moe_b_expert,
             moe_w1=moe_w1, moe_w3=moe_w3, moe_w2=moe_w2)
    depth = norm_mix.shape[0]
    bp, bs = c_prompt.shape[0], c_sample.shape[0]
    nb = (bp + bs + 7) // 8 * 8
    c_all = jnp.pad(jnp.concatenate([c_prompt, c_sample], axis=0), ((0, nb - bp - bs), (0, 0)))
    mods = _mod_call(c_all, w_mod, b_mod)
    wab = [_prep_ab(l, p) for l in range((depth + 1) // 2)]
    wc = [_prep_c(l, p) for l in range(depth // 2)]
    wmoe = [_prep_moe(layer, p) for layer in range(depth)]
    none6 = (None,) * 6
    outp = _trunk(x_prompt, mods[:, :bp], none6, wab, wc, wmoe, p, flat=False)
    caches = (cache_mla_latent, cache_mla_krope, state_s5_re, state_s5_im, state_conv, state_gdn)
    outs = _trunk(x_sample, mods[:, bp:bp + bs], caches, wab, wc, wmoe, p, flat=True)
    return (outp[0], outs[0]) + tuple(outp[1:]) + tuple(outs[1:])
```

```python
import functools
import math

import jax
import jax.numpy as jnp
from jax import lax
from jax.experimental import pallas as pl
from jax.experimental.pallas import tpu as pltpu

F32 = jnp.float32
BF16 = jnp.bfloat16
HI = lax.Precision.HIGHEST
EPS = 1e-6

D_MODEL = 1024
CHUNK = 64
S5_WIDTH = 512
S5_GROUP = 16
S5_GROUPS = 32
S5_STATE = 64
S5_CHUNK = 16
S5_LB = 128 // S5_GROUP
MLA_HEADS = 8
MLA_NOPE = 64
MLA_ROPE = 32
MLA_V = 64
MLA_Q_RANK = 384
MLA_KV_RANK = 256
ROPE_THETA = 10000.0
GDN_HEADS = 8
GDN_DK = 128
GDN_DV = 128
CONV_W = 4
GDN_QKV = GDN_HEADS * (2 * GDN_DK + GDN_DV)
GDN_VW = GDN_HEADS * GDN_DV
N_GROUPS = 4
EXPERTS_PER_GROUP = 8
N_EXPERTS = 32
D_EXPERT = 512
TOP_K = 2
MOE_BLOCK = 512
MOE_BLOCK_SMALL = 128
ATTN_SUB = 512
ATTN_TQ = 512
ATTN_TK = 512
TOKEN_BLOCK = 256
S5_ROW_BLOCK = 256
S5_SCAN_BLOCK = 128
LANES = 128
VMEM_LIMIT = 48 * 1024 * 1024


def _cparams(*sem):
    return pltpu.CompilerParams(dimension_semantics=sem, vmem_limit_bytes=VMEM_LIMIT)


def _dot(a, b):
    return jnp.dot(a, b, preferred_element_type=F32)


def _dot_nt(a, b, precision=None):
    return lax.dot_general(a, b, (((1,), (1,)), ((), ())), preferred_element_type=F32, precision=precision)


def _full(arr):
    nd = arr.ndim
    return pl.BlockSpec(arr.shape, lambda *_: (0,) * nd)


def _tok(width, ts):
    return pl.BlockSpec((1, ts, width), lambda b, j: (b, j, 0))


def _modspec(arr, ts):
    if arr.shape[1] == 1:
        return pl.BlockSpec((1, 1, arr.shape[2]), lambda b, j: (b, 0, 0))
    return pl.BlockSpec((1, ts, arr.shape[2]), lambda b, j: (b, j, 0))


def _normmod(x, gain, sc, sh):
    ms = jnp.mean(x * x, axis=-1, keepdims=True)
    return x * lax.rsqrt(ms + EPS) * gain * (1.0 + sc) + sh


def _rms(x, gain):
    return x * lax.rsqrt(jnp.mean(x * x, axis=-1, keepdims=True) + EPS) * gain


def _mod_kernel(c_ref, w_ref, b_ref, o_ref):
    c = c_ref[...]
    a = (c * jax.nn.sigmoid(c)).astype(BF16)
    o_ref[0] = _dot(a, w_ref[0]) + b_ref[0]


def _mod_call(c_all, w_mod, b_mod):
    depth, d, n6 = w_mod.shape
    bp = c_all.shape[0]
    tn = 1536
    return pl.pallas_call(
        _mod_kernel,
        grid=(depth, n6 // tn),
        in_specs=[pl.BlockSpec((bp, d), lambda l, n: (0, 0)),
                  pl.BlockSpec((1, d, tn), lambda l, n: (l, 0, n)),
                  pl.BlockSpec((1, 1, tn), lambda l, n: (l, 0, n))],
        out_specs=pl.BlockSpec((1, bp, tn), lambda l, n: (l, 0, n)),
        out_shape=jax.ShapeDtypeStruct((depth, bp, n6), F32),
        compiler_params=_cparams("parallel", "parallel"),
        name="mod",
    )(c_all, w_mod.astype(BF16), b_mod.reshape(depth, 1, n6))


def _ab_in_kernel(x_ref, gain_ref, sc_ref, sh_ref, w_ref, qg_ref, wqa_ref, wqb_ref, kvg_ref,
                  cq_ref, sq_ref, ck_ref, sk_ref, u_ref, q_ref, lat_ref, kr_ref):
    h = _normmod(x_ref[0], gain_ref[...], sc_ref[0], sh_ref[0]).astype(BF16)
    proj = _dot(h, w_ref[...])
    u_ref[0] = proj[:, :S5_WIDTH]
    o1 = S5_WIDTH + MLA_Q_RANK
    o2 = o1 + MLA_KV_RANK
    qn = _rms(proj[:, S5_WIDTH:o1], qg_ref[...]).astype(BF16)
    qa = _dot(qn, wqa_ref[...])
    qb = _dot(qn, wqb_ref[...])
    cq = cq_ref[...]
    sq = sq_ref[...]
    for hh in range(MLA_HEADS):
        sl = slice(LANES * hh, LANES * (hh + 1))
        q_ref[0, hh] = (qa[:, sl] * cq + qb[:, sl] * sq).astype(BF16)
    lat_ref[0] = _rms(proj[:, o1:o2], kvg_ref[...])
    kr_ref[0] = proj[:, o2:o2 + LANES] * ck_ref[...] + proj[:, o2 + LANES:o2 + 2 * LANES] * sk_ref[...]


def _ab_in_call(x, gain, sc, sh, wp, tabs, ts):
    b, s, d = x.shape
    cq, sq, ck, sk = tabs
    tab = pl.BlockSpec((ts, LANES), lambda bb, j: (j, 0))
    ins = [x, gain, sc, sh, wp["w_in"], wp["q_norm"], wp["wqa"], wp["wqb"], wp["kv_norm"], cq, sq, ck, sk]
    specs = [_tok(d, ts), _full(gain), _modspec(sc, ts), _modspec(sh, ts), _full(wp["w_in"]), _full(wp["q_norm"]),
             _full(wp["wqa"]), _full(wp["wqb"]), _full(wp["kv_norm"]), tab, tab, tab, tab]
    return pl.pallas_call(
        _ab_in_kernel,
        grid=(b, s // ts),
        in_specs=specs,
        out_specs=[_tok(S5_WIDTH, ts),
                   pl.BlockSpec((1, MLA_HEADS, ts, LANES), lambda bb, j: (bb, 0, j, 0)),
                   _tok(MLA_KV_RANK, ts), _tok(LANES, ts)],
        out_shape=[jax.ShapeDtypeStruct((b, s, S5_WIDTH), F32),
                   jax.ShapeDtypeStruct((b, MLA_HEADS, s, LANES), BF16),
                   jax.ShapeDtypeStruct((b, s, MLA_KV_RANK), F32),
                   jax.ShapeDtypeStruct((b, s, LANES), F32)],
        compiler_params=_cparams("parallel", "parallel"),
        name="ab_in",
    )(*ins)


def _kv_kernel(lat_ref, kr_ref, wk_ref, wv_ref, k_ref, v_ref):
    lat = lat_ref[0].astype(BF16)
    kk = _dot(lat, wk_ref[...])
    vv = _dot(lat, wv_ref[...])
    kr = kr_ref[0]
    ones = (lax.broadcasted_iota(jnp.int32, (1, LANES), 1) >= MLA_V).astype(F32)
    for hh in range(MLA_HEADS):
        k_ref[0, hh] = (kk[:, LANES * hh:LANES * (hh + 1)] + kr).astype(BF16)
        v_ref[0, hh] = (vv[:, LANES * hh:LANES * (hh + 1)] + ones).astype(BF16)


def _kv_call(lat_all, kr_all, wp, ts):
    b, sk, _ = lat_all.shape
    hspec = pl.BlockSpec((1, MLA_HEADS, ts, LANES), lambda bb, j: (bb, 0, j, 0))
    hshape = jax.ShapeDtypeStruct((b, MLA_HEADS, sk, LANES), BF16)
    return pl.pallas_call(
        _kv_kernel,
        grid=(b, sk // ts),
        in_specs=[_tok(MLA_KV_RANK, ts), _tok(LANES, ts), _full(wp["wk"]), _full(wp["wv"])],
        out_specs=[hspec, hspec],
        out_shape=[hshape, hshape],
        compiler_params=_cparams("parallel", "parallel"),
        name="kv_up",
    )(lat_all, kr_all, wp["wk"], wp["wv"])


def _attn_kernel(q_ref, k_ref, v_ref, *rest, tq, tk, q_off, sk_valid):
    o_ref = rest[-1]
    bias_ref = rest[0] if len(rest) == 2 else None
    i = pl.program_id(2)
    nsub = max(tq // ATTN_SUB, 1)
    tqs = tq // nsub
    chains = [(hh, qi) for hh in range(2) for qi in range(nsub)]
    qs = [q_ref[0, hh, qi * tqs:(qi + 1) * tqs, :] for hh, qi in chains]
    q0 = q_off + i * tq
    lim_full = (q0 // CHUNK + 1) * CHUNK
    lim_tot = jnp.minimum(((q0 + tq - 1) // CHUNK + 1) * CHUNK, sk_valid)
    n_full = jnp.minimum(lim_full // tk, sk_valid // tk)
    n_tot = (lim_tot + tk - 1) // tk

    def step(j, carry, masked):
        off = pl.multiple_of(j * tk, tk)
        nc = range(len(chains))
        if masked and bias_ref is None:
            kpos = off + lax.broadcasted_iota(jnp.int32, (tqs, tk), 1)
            ok = []
            for qi in range(nsub):
                qpos = q0 + qi * tqs + lax.broadcasted_iota(jnp.int32, (tqs, tk), 0)
                ok.append(jnp.logical_and(kpos // CHUNK <= qpos // CHUNK, kpos < sk_valid))
        kt = [k_ref[0, hh, pl.ds(off, tk), :] for hh in range(2)]
        vt = [v_ref[0, hh, pl.ds(off, tk), :] for hh in range(2)]
        s = [_dot_nt(qs[c], kt[chains[c][0]]) for c in nc]
        if masked and bias_ref is None:
            s = [jnp.where(ok[chains[c][1]], s[c], -1e30) for c in nc]
        elif masked:
            s = [s[c] + bias_ref[chains[c][1] * tqs:(chains[c][1] + 1) * tqs, :] for c in nc]
        m_new = [jnp.maximum(carry[c][0], jnp.max(s[c], axis=-1, keepdims=True)) for c in nc]
        alpha = [jnp.exp2(carry[c][0] - m_new[c]) for c in nc]
        p = [jnp.exp2((s[c] - m_new[c]).astype(BF16)) for c in nc]
        acc = [alpha[c] * carry[c][1] + _dot(p[c], vt[chains[c][0]]) for c in nc]
        return tuple((m_new[c], acc[c]) for c in nc)

    one = (jnp.full((tqs, 1), -1e30, F32), jnp.zeros((tqs, LANES), F32))
    carry = lax.fori_loop(0, n_full, functools.partial(step, masked=False), (one,) * len(chains))
    carry = lax.fori_loop(n_full, n_tot, functools.partial(step, masked=True), carry)
    lane = lax.broadcasted_iota(jnp.int32, (tqs, LANES), 1)
    for qi in range(nsub):
        a0 = carry[chains.index((0, qi))][1]
        a1 = carry[chains.index((1, qi))][1]
        r0 = a0 / pltpu.roll(a0, MLA_V, axis=1)
        r1 = a1 / pltpu.roll(a1, MLA_V, axis=1)
        o_ref[0, qi * tqs:(qi + 1) * tqs, :] = jnp.where(
            lane < MLA_V, r0, pltpu.roll(r1, MLA_V, axis=1)).astype(o_ref.dtype)


def _attn_call(q, k, v, tq, tk, q_off, sk_valid):
    b, h, sq, _ = q.shape
    sk = k.shape[2]
    kern = functools.partial(_attn_kernel, tq=tq, tk=tk, q_off=q_off, sk_valid=sk_valid)
    kv_mode = pl.Buffered(2 if sq == tq else 1)
    ins = [q, k, v]
    specs = [pl.BlockSpec((1, 2, tq, LANES), lambda bb, hp, i: (bb, hp, i, 0)),
             pl.BlockSpec((1, 2, sk, LANES), lambda bb, hp, i: (bb, hp, 0, 0), pipeline_mode=kv_mode),
             pl.BlockSpec((1, 2, sk, LANES), lambda bb, hp, i: (bb, hp, 0, 0), pipeline_mode=kv_mode)]
    if q_off == 0 and tq == tk and sk_valid % tk == 0:
        r = jnp.arange(tq, dtype=jnp.int32)
        bias = jnp.where(r[None, :] // CHUNK <= r[:, None] // CHUNK, 0.0, -1e30).astype(F32)
        ins.append(bias)
        specs.append(pl.BlockSpec((tq, tk), lambda bb, hp, i: (0, 0)))
    return pl.pallas_call(
        kern,
        grid=(b, h // 2, sq // tq),
        in_specs=specs,
        out_specs=pl.BlockSpec((1, tq, LANES), lambda bb, hp, i: (bb, i, hp)),
        out_shape=jax.ShapeDtypeStruct((b, sq, h * MLA_V), BF16),
        compiler_params=_cparams("parallel", "parallel", "arbitrary"),
        name="mla_attn",
    )(*ins)


def _chunk_rows(ref, s, tr):
    return ref[pl.ds(s, tr, stride=S5_CHUNK), :]


def _s5_w_kernel(u_ref, bre_ref, bim_ref, wre_ref, wim_ref, *, tr):
    ucat = jnp.concatenate([_chunk_rows(u_ref, s, tr).astype(BF16) for s in range(S5_CHUNK)], axis=1)
    wre_ref[...] = _dot(ucat, bre_ref[0])
    wim_ref[...] = _dot(ucat, bim_ref[0])


def _s5_w_call(u2, bre, bim, tr):
    n = u2.shape[0]
    r = n // S5_CHUNK
    nq = bre.shape[0]
    sw = S5_LB * S5_STATE
    out = jax.ShapeDtypeStruct((r, nq * sw), F32)
    ospec = pl.BlockSpec((tr, sw), lambda q, i: (i, q))
    wspec = pl.BlockSpec((1, S5_CHUNK * LANES, sw), lambda q, i: (q, 0, 0))
    return pl.pallas_call(
        functools.partial(_s5_w_kernel, tr=tr),
        grid=(nq, r // tr),
        in_specs=[pl.BlockSpec((tr * S5_CHUNK, LANES), lambda q, i: (i, q)), wspec, wspec],
        out_specs=[ospec, ospec],
        out_shape=[out, out],
        compiler_params=_cparams("parallel", "parallel"),
        name="s5_chunk_in",
    )(u2, bre, bim)


def _s5_scan_kernel(wre_ref, wim_ref, lre_ref, lim_ref, x0re_ref, x0im_ref, ore_ref, oim_ref, sre, sim, *, tc):
    @pl.when(pl.program_id(1) == 0)
    def _():
        sre[...] = x0re_ref[0]
        sim[...] = x0im_ref[0]

    lr = lre_ref[...]
    li = lim_ref[...]

    def blk(t, carry):
        xr, xi = carry
        base = pl.multiple_of(t * 8, 8)
        wr = wre_ref[0, pl.ds(base, 8), :]
        wi = wim_ref[0, pl.ds(base, 8), :]
        rows_r, rows_i = [], []
        for r in range(8):
            nr = lr * xr - li * xi + wr[r:r + 1]
            ni = lr * xi + li * xr + wi[r:r + 1]
            xr, xi = nr, ni
            rows_r.append(xr)
            rows_i.append(xi)
        ore_ref[0, pl.ds(base, 8), :] = jnp.concatenate(rows_r, axis=0)
        oim_ref[0, pl.ds(base, 8), :] = jnp.concatenate(rows_i, axis=0)
        return xr, xi

    xr, xi = lax.fori_loop(0, tc // 8, blk, (sre[...], sim[...]))
    sre[...] = xr
    sim[...] = xi


def _s5_scan_call(wre, wim, lre, lim, x0re, x0im, tc):
    b, nch, n = wre.shape
    wspec = pl.BlockSpec((1, tc, n), lambda bb, c: (bb, c, 0))
    lspec = pl.BlockSpec((1, n), lambda bb, c: (0, 0))
    xspec = pl.BlockSpec((1, 1, n), lambda bb, c: (bb, 0, 0))
    out = jax.ShapeDtypeStruct((b, nch, n), F32)
    return pl.pallas_call(
        functools.partial(_s5_scan_kernel, tc=tc),
        grid=(b, nch // tc),
        in_specs=[wspec, wspec, lspec, lspec, xspec, xspec],
        out_specs=[wspec, wspec],
        out_shape=[out, out],
        scratch_shapes=[pltpu.VMEM((1, n), F32), pltpu.VMEM((1, n), F32)],
        compiler_params=_cparams("parallel", "arbitrary"),
        name="s5_scan",
    )(wre, wim, lre, lim, x0re, x0im)


def _s5_y_kernel(u_ref, xre_ref, xim_ref, m_ref, c_ref, y_ref, *, tr):
    L = S5_CHUNK
    urev = jnp.concatenate([_chunk_rows(u_ref, s, tr).astype(BF16) for s in reversed(range(L))], axis=1)
    xcat = jnp.concatenate([xre_ref[...].astype(BF16), xim_ref[...].astype(BF16)], axis=1)
    for t in range(L):
        y = _dot(urev[:, (L - 1 - t) * LANES:], m_ref[0, :(t + 1) * LANES, :]) + _dot(xcat, c_ref[0, t])
        y_ref[pl.ds(t, tr, stride=L), :] = y


def _s5_y_call(u2, xre, xim, mst, cc, tr):
    n, w = u2.shape
    nq = mst.shape[0]
    sw = S5_LB * S5_STATE
    uspec = pl.BlockSpec((tr * S5_CHUNK, LANES), lambda q, i: (i, q))
    xspec = pl.BlockSpec((tr, sw), lambda q, i: (i, q))
    return pl.pallas_call(
        functools.partial(_s5_y_kernel, tr=tr),
        grid=(nq, n // (tr * S5_CHUNK)),
        in_specs=[uspec, xspec, xspec,
                  pl.BlockSpec((1, S5_CHUNK * LANES, LANES), lambda q, i: (q, 0, 0)),
                  pl.BlockSpec((1, S5_CHUNK, 2 * sw, LANES), lambda q, i: (q, 0, 0, 0))],
        out_specs=uspec,
        out_shape=jax.ShapeDtypeStruct((n, w), F32),
        compiler_params=_cparams("parallel", "parallel"),
        name="s5_chunk_out",
    )(u2, xre, xim, mst, cc)


def _s5_mats(a_re, a_im, b_re, b_im, c_re, c_im, log_dt):
    g, p = a_re.shape
    L = S5_CHUNK
    lam = lax.complex(jnp.minimum(a_re, -1e-4), a_im)
    lamdt = lam * jnp.exp(log_dt)[:, None]
    lam_bar = jnp.exp(lamdt)
    b_bar = ((lam_bar - 1.0) / lam)[:, :, None] * lax.complex(b_re, b_im)
    cm = lax.complex(c_re, c_im)
    pw = jnp.exp(lamdt[:, :, None] * jnp.arange(L + 1, dtype=F32))
    mm = pw[:, :, :L, None] * b_bar[:, :, None, :]
    kd = jnp.sum((cm[:, :, :, None, None] * mm[:, None]).real, axis=2).transpose(0, 2, 1, 3)
    bp = pw[:, :, L - 1 - jnp.arange(L)][:, :, :, None] * b_bar[:, :, None, :]
    cp = cm[:, :, :, None] * pw[:, None, :, 1:]
    nq = g // S5_LB
    eye = jnp.eye(S5_LB, dtype=F32)

    def lane_block(m, spec, rows, cols):
        return jnp.einsum(spec, m.reshape((nq, S5_LB) + m.shape[1:]), eye).reshape(nq, rows, cols)

    sw = S5_LB * p
    bre = lane_block(bp.real, "qgpsc,hg->qshcgp", L * LANES, sw)
    bim = lane_block(bp.imag, "qgpsc,hg->qshcgp", L * LANES, sw)
    mst = lane_block(kd, "qgdoc,hg->qdhcgo", L * LANES, LANES)
    cre = lane_block(cp.real, "qgopt,hg->qthpgo", L * sw, LANES).reshape(nq, L, sw, LANES)
    cim = lane_block(-cp.imag, "qgopt,hg->qthpgo", L * sw, LANES).reshape(nq, L, sw, LANES)
    return dict(
        bre=bre.astype(BF16), bim=bim.astype(BF16), mst=mst.astype(BF16),
        cc=jnp.concatenate([cre, cim], axis=2).astype(BF16),
        lre=pw[:, :, L].real.reshape(1, g * p), lim=pw[:, :, L].imag.reshape(1, g * p))


def _ab_out_kernel(x_ref, ys_ref, u_ref, at_ref, d_ref, gw_ref, gb_ref, ws_ref, wa_ref, g1_ref, *route_refs):
    o_ref = route_refs[5]
    y = ys_ref[0] + d_ref[...] * u_ref[0]
    y = jax.nn.gelu(y)
    gate = jax.nn.sigmoid(_dot(y.astype(BF16), gw_ref[...]) + gb_ref[...])
    out = _dot((y * gate).astype(BF16), ws_ref[...]) + _dot(at_ref[0], wa_ref[...])
    xn = x_ref[0] + g1_ref[0] * out
    o_ref[0] = xn
    _route(xn, *route_refs[:5], *route_refs[6:])


def _ab_out_call(x, ys, u, attn, wp, g1, rt, ts):
    b, s, d = x.shape
    r_ins, r_specs, r_ospecs, r_oshape, r_scratch = _route_plumbing(rt, b, s, d, ts)
    ins = [x, ys, u, attn, wp["s5_d"], wp["glu_w"], wp["glu_b"], wp["w_out_s5"], wp["w_out_at"], g1] + r_ins
    specs = [_tok(d, ts), _tok(S5_WIDTH, ts), _tok(S5_WIDTH, ts), _tok(MLA_HEADS * MLA_V, ts),
             _full(wp["s5_d"]), _full(wp["glu_w"]), _full(wp["glu_b"]), _full(wp["w_out_s5"]),
             _full(wp["w_out_at"]), _modspec(g1, ts)] + r_specs
    return pl.pallas_call(
        _ab_out_kernel,
        grid=(b, s // ts),
        in_specs=specs,
        out_specs=[_tok(d, ts)] + r_ospecs,
        out_shape=[jax.ShapeDtypeStruct((b, s, d), F32)] + r_oshape,
        scratch_shapes=r_scratch,
        compiler_params=_cparams("arbitrary", "arbitrary"),
        name="ab_out",
    )(*ins)


def _c_in_kernel(x_ref, gain_ref, sc_ref, sh_ref, w_ref, qkv_ref, z_ref, ab_ref):
    h = _normmod(x_ref[0], gain_ref[...], sc_ref[0], sh_ref[0]).astype(BF16)
    proj = _dot(h, w_ref[...])
    qkv_ref[0] = proj[:, :GDN_QKV].astype(BF16)
    z_ref[0] = proj[:, GDN_QKV:GDN_QKV + GDN_VW].astype(BF16)
    ab_ref[0] = proj[:, GDN_QKV + GDN_VW:]


def _c_in_call(x, gain, sc, sh, w, ts):
    b, s, d = x.shape
    return pl.pallas_call(
        _c_in_kernel,
        grid=(b, s // ts),
        in_specs=[_tok(d, ts), _full(gain), _modspec(sc, ts), _modspec(sh, ts), _full(w)],
        out_specs=[_tok(GDN_QKV, ts), _tok(GDN_VW, ts), _tok(LANES, ts)],
        out_shape=[jax.ShapeDtypeStruct((b, s, GDN_QKV), BF16),
                   jax.ShapeDtypeStruct((b, s, GDN_VW), BF16),
                   jax.ShapeDtypeStruct((b, s, LANES), F32)],
        compiler_params=_cparams("parallel", "parallel"),
        name="c_in",
    )(x, gain, sc, sh, w)


def _gdn_prep_kernel(qkv_ref, halo_ref, st_ref, cw_ref, ab_ref, alog_ref, dtb_ref,
                     wq_ref, u_ref, kgt_ref, pm_ref, ee_ref, *, L, nc):
    j = pl.program_id(1)
    rows = nc * L
    prev = jnp.where(j == 0, st_ref[0].astype(F32), halo_ref[0].astype(F32))
    xx = jnp.concatenate([prev, qkv_ref[0].astype(F32)], axis=0)
    cw = cw_ref[...]
    y = xx[8:8 + rows] * cw[CONV_W - 1:CONV_W]
    for t in range(1, CONV_W):
        y += xx[8 - t:8 - t + rows] * cw[CONV_W - 1 - t:CONV_W - t]
    y = y * jax.nn.sigmoid(y)

    ab = ab_ref[0]
    g_all = -jnp.exp(alog_ref[...]) * jax.nn.softplus(ab + dtb_ref[...])
    beta_all = jax.nn.sigmoid(ab)
    row = lax.broadcasted_iota(jnp.int32, (L, L), 0)
    col = lax.broadcasted_iota(jnp.int32, (L, L), 1)
    incl = row >= col
    strict = row > col
    brow = lax.broadcasted_iota(jnp.int32, (rows, rows), 0)
    bcol = lax.broadcasted_iota(jnp.int32, (rows, rows), 1)
    incl_blk = jnp.logical_and(brow >= bcol, brow // L == bcol // L)
    gam_all = jnp.dot(incl_blk.astype(F32), g_all, preferred_element_type=F32, precision=HI)
    eye_l = (lax.broadcasted_iota(jnp.int32, (LANES, LANES), 0)
             == lax.broadcasted_iota(jnp.int32, (LANES, LANES), 1))
    gam_t = _dot_nt(eye_l.astype(F32), gam_all, precision=HI)
    for ci in range(nc):
        ee_ref[0, ci] = jnp.exp(gam_all[(ci + 1) * L - 1:(ci + 1) * L])
    eye64 = (row == col).astype(F32)
    eye_bf = eye_l.astype(BF16)

    ch = [(ci, h) for ci in range(nc) for h in range(GDN_HEADS)]
    cs = range(len(ch))
    koff, voff = GDN_HEADS * GDN_DK, 2 * GDN_HEADS * GDN_DK
    qh = [y[ci * L:(ci + 1) * L, GDN_DK * h:GDN_DK * (h + 1)] for ci, h in ch]
    kh = [y[ci * L:(ci + 1) * L, koff + GDN_DK * h:koff + GDN_DK * (h + 1)] for ci, h in ch]
    vh = [y[ci * L:(ci + 1) * L, voff + GDN_DV * h:voff + GDN_DV * (h + 1)] for ci, h in ch]
    qh = [q * lax.rsqrt(jnp.sum(q * q, axis=-1, keepdims=True) + EPS) * (GDN_DK ** -0.5) for q in qh]
    kh = [k * lax.rsqrt(jnp.sum(k * k, axis=-1, keepdims=True) + EPS) for k in kh]
    gam_c = [gam_all[ci * L:(ci + 1) * L, h:h + 1] for ci, h in ch]
    beta_c = [beta_all[ci * L:(ci + 1) * L, GDN_HEADS + h:GDN_HEADS + h + 1] for ci, h in ch]
    dec = [jnp.exp(jnp.minimum(gam_c[c] - gam_t[h:h + 1, ci * L:(ci + 1) * L], 0.0))
           for c, (ci, h) in enumerate(ch)]
    kb = [k.astype(BF16) for k in kh]
    kk = [_dot_nt(kb[c], kb[c]) for c in cs]
    qk = [_dot_nt(qh[c].astype(BF16), kb[c]) for c in cs]
    a = [jnp.where(strict, beta_c[c] * kk[c] * dec[c], 0.0) for c in cs]
    for c, (ci, h) in enumerate(ch):
        pm_ref[0, ci, h] = jnp.where(incl, qk[c] * dec[c], 0.0).astype(BF16)
    tinv = [eye64 - a[c] for c in cs]
    pw = [_split(a[c]) for c in cs]
    for _ in range(max(L.bit_length() - 2, 0)):
        pw = [_split(_dot3(pw[c], pw[c])) for c in cs]
        tinv = [tinv[c] + _dot3(_split(tinv[c]), pw[c]) for c in cs]
    eg = [jnp.exp(gam_c[c]) for c in cs]
    rhs = [jnp.concatenate([(beta_c[c] * eg[c]) * kh[c], beta_c[c] * vh[c]], axis=1).astype(BF16) for c in cs]
    wu = [_dot(tinv[c].astype(BF16), rhs[c]) for c in cs]
    for c, (ci, h) in enumerate(ch):
        wq_ref[0, ci, h] = jnp.concatenate([wu[c][:, :GDN_DK], qh[c] * eg[c]], axis=0).astype(BF16)
        u_ref[0, h, ci * L:(ci + 1) * L, :] = wu[c][:, GDN_DK:].astype(BF16)
    kg = [(kh[c] * jnp.exp(gam_all[(ci + 1) * L - 1:(ci + 1) * L, h:h + 1] - gam_c[c])).astype(BF16)
          for c, (ci, h) in enumerate(ch)]
    kgt = [_dot_nt(eye_bf, kg[c]) for c in cs]
    for c, (ci, h) in enumerate(ch):
        kgt_ref[0, ci, h] = kgt[c].astype(BF16)


def _split(a):
    hi = a.astype(BF16)
    return hi, (a - hi.astype(F32)).astype(BF16)


def _dot3(a, b):
    return _dot(a[0], b[0]) + (_dot(a[0], b[1]) + _dot(a[1], b[0]))


GDN_PREP_CHUNKS = 2


def _gdn_prep_call(qkv, st8, conv_w, ab, alog, dtb, L):
    b, s, _ = qkv.shape
    nch = s // L
    nc = GDN_PREP_CHUNKS if nch % GDN_PREP_CHUNKS == 0 else 1
    rows = nc * L
    return pl.pallas_call(
        functools.partial(_gdn_prep_kernel, L=L, nc=nc),
        grid=(b, nch // nc),
        in_specs=[_tok(GDN_QKV, rows),
                  pl.BlockSpec((1, 8, GDN_QKV), lambda bb, j: (bb, jnp.maximum(j * (rows // 8) - 1, 0), 0)),
                  pl.BlockSpec((1, 8, GDN_QKV), lambda bb, j: (bb, 0, 0)),
                  _full(conv_w), _tok(LANES, rows), _full(alog), _full(dtb)],
        out_specs=[pl.BlockSpec((1, nc, GDN_HEADS, 2 * L, LANES), lambda bb, j: (bb, j, 0, 0, 0)),
                   pl.BlockSpec((1, GDN_HEADS, rows, LANES), lambda bb, j: (bb, 0, j, 0)),
                   pl.BlockSpec((1, nc, GDN_HEADS, GDN_DK, L), lambda bb, j: (bb, j, 0, 0, 0)),
                   pl.BlockSpec((1, nc, GDN_HEADS, L, L), lambda bb, j: (bb, j, 0, 0, 0)),
                   pl.BlockSpec((1, nc, 1, LANES), lambda bb, j: (bb, j, 0, 0))],
        out_shape=[jax.ShapeDtypeStruct((b, nch, GDN_HEADS, 2 * L, LANES), BF16),
                   jax.ShapeDtypeStruct((b, GDN_HEADS, s, LANES), BF16),
                   jax.ShapeDtypeStruct((b, nch, GDN_HEADS, GDN_DK, L), BF16),
                   jax.ShapeDtypeStruct((b, nch, GDN_HEADS, L, L), BF16),
                   jax.ShapeDtypeStruct((b, nch, 1, LANES), F32)],
        compiler_params=_cparams("parallel", "parallel"),
        name="gdn_prep",
    )(qkv, qkv, st8, conv_w, ab, alog, dtb)


GDN_BATCH_GROUP = 2


GDN_SEQ_CHUNKS = 8


def _gdn_seq_kernel(wq_ref, u_ref, kgt_ref, pm_ref, ee_ref, s0_ref, o_ref, sf_ref, s_scr, *, L, nc):
    c = pl.program_id(1)

    @pl.when(c == 0)
    def _():
        s_scr[...] = s0_ref[...]

    ch = [(bb, hh) for bb in range(GDN_BATCH_GROUP) for hh in range(GDN_HEADS)]
    st = [s_scr[bb, hh] for bb, hh in ch]
    for ci in range(nc):
        rows = slice(ci * L, (ci + 1) * L)
        sp = [_split(s) for s in st]
        r = [_dot(wq_ref[bb, ci, hh], sp[i][0]) + _dot(wq_ref[bb, ci, hh], sp[i][1]) for i, (bb, hh) in enumerate(ch)]
        v_new = [(u_ref[bb, hh, rows, :].astype(F32) - r[i][:L]).astype(BF16) for i, (bb, hh) in enumerate(ch)]
        o = [r[i][L:] + _dot(pm_ref[bb, ci, hh], v_new[i]) for i, (bb, hh) in enumerate(ch)]
        for i, (bb, hh) in enumerate(ch):
            o_ref[bb, rows, GDN_DV * hh:GDN_DV * (hh + 1)] = o[i].astype(o_ref.dtype)
        st = [ee_ref[bb, ci][:, hh:hh + 1] * st[i] + _dot(kgt_ref[bb, ci, hh], v_new[i])
              for i, (bb, hh) in enumerate(ch)]
    for i, (bb, hh) in enumerate(ch):
        s_scr[bb, hh] = st[i]

    @pl.when(c == pl.num_programs(1) - 1)
    def _():
        sf_ref[...] = s_scr[...]


def _gdn_seq_call(wq, um, kgt, pm, ee, s0, L):
    b, h, s, _ = um.shape
    nch = s // L
    bg = GDN_BATCH_GROUP
    nc = GDN_SEQ_CHUNKS if nch % GDN_SEQ_CHUNKS == 0 else 1
    sspec = pl.BlockSpec((bg, h, GDN_DK, GDN_DV), lambda g, c: (g, 0, 0, 0))
    return pl.pallas_call(
        functools.partial(_gdn_seq_kernel, L=L, nc=nc),
        grid=(b // bg, nch // nc),
        in_specs=[pl.BlockSpec((bg, nc, h, 2 * L, LANES), lambda g, c: (g, c, 0, 0, 0)),
                  pl.BlockSpec((bg, h, nc * L, LANES), lambda g, c: (g, 0, c, 0)),
                  pl.BlockSpec((bg, nc, h, GDN_DK, L), lambda g, c: (g, c, 0, 0, 0)),
                  pl.BlockSpec((bg, nc, h, L, L), lambda g, c: (g, c, 0, 0, 0)),
                  pl.BlockSpec((bg, nc, 1, LANES), lambda g, c: (g, c, 0, 0)),
                  sspec],
        out_specs=[pl.BlockSpec((bg, nc * L, GDN_VW), lambda g, c: (g, c, 0)), sspec],
        out_shape=[jax.ShapeDtypeStruct((b, s, GDN_VW), BF16),
                   jax.ShapeDtypeStruct((b, h, GDN_DK, GDN_DV), F32)],
        scratch_shapes=[pltpu.VMEM((bg, h, GDN_DK, GDN_DV), F32)],
        compiler_params=_cparams("parallel", "arbitrary"),
        name="gdn_seq",
    )(wq, um, kgt, pm, ee, s0)


def _c_out_kernel(x_ref, o_ref, z_ref, gn_ref, w_ref, g1_ref, *route_refs):
    out_ref = route_refs[5]
    o = o_ref[0].astype(F32)
    z = z_ref[0].astype(F32)
    gn = gn_ref[...]
    parts = []
    for hh in range(GDN_HEADS):
        oh = o[:, GDN_DV * hh:GDN_DV * (hh + 1)]
        zh = z[:, GDN_DV * hh:GDN_DV * (hh + 1)]
        parts.append((_rms(oh, gn) * (zh * jax.nn.sigmoid(zh))).astype(BF16))
    out = _dot(jnp.concatenate(parts, axis=1), w_ref[...])
    xn = x_ref[0] + g1_ref[0] * out
    out_ref[0] = xn
    _route(xn, *route_refs[:5], *route_refs[6:])


def _c_out_call(x, o, z, gn, w, g1, rt, ts):
    b, s, d = x.shape
    r_ins, r_specs, r_ospecs, r_oshape, r_scratch = _route_plumbing(rt, b, s, d, ts)
    return pl.pallas_call(
        _c_out_kernel,
        grid=(b, s // ts),
        in_specs=[_tok(d, ts), _tok(GDN_VW, ts), _tok(GDN_VW, ts), _full(gn), _full(w), _modspec(g1, ts)] + r_specs,
        out_specs=[_tok(d, ts)] + r_ospecs,
        out_shape=[jax.ShapeDtypeStruct((b, s, d), F32)] + r_oshape,
        scratch_shapes=r_scratch,
        compiler_params=_cparams("arbitrary", "arbitrary"),
        name="c_out",
    )(x, o, z, gn, w, g1, *r_ins)


def _pack_halves(v):
    n = v.shape[1] // 2
    bits = pltpu.bitcast(v.astype(BF16).astype(F32), jnp.uint32)
    return (bits[:, :n] >> 16) | (bits[:, n:] & jnp.uint32(0xFFFF0000))


def _unpack_halves(w):
    return pltpu.bitcast(w << 16, F32), pltpu.bitcast(w & jnp.uint32(0xFFFF0000), F32)


def _route(xn, gain_ref, sc_ref, sh_ref, wr_ref, br_ref, h_ref, re_ref, rg_ref, cnt_ref, carry):
    @pl.when(jnp.logical_and(pl.program_id(0) == 0, pl.program_id(1) == 0))
    def _():
        carry[...] = jnp.zeros_like(carry)

    h = _normmod(xn, gain_ref[...], sc_ref[0], sh_ref[0])
    h_ref[0] = _pack_halves(h)
    logits = _dot3(_split(h), (wr_ref[0], wr_ref[1])) + br_ref[...]
    lane = lax.broadcasted_iota(jnp.int32, logits.shape, 1).astype(F32)
    neg = jnp.float32(-jnp.inf)
    big = jnp.float32(1e9)
    gl = jnp.where(lane < N_GROUPS, logits, neg)
    gm = jnp.max(gl, axis=-1, keepdims=True)
    g_val = 1.0 / jnp.sum(jnp.exp(gl - gm), axis=-1, keepdims=True)
    g_idx = jnp.min(jnp.where(gl == gm, lane, big), axis=-1, keepdims=True)
    lo = N_GROUPS + EXPERTS_PER_GROUP * g_idx
    el = jnp.where(jnp.logical_and(lane >= lo, lane < lo + EXPERTS_PER_GROUP), logits, neg)
    em = jnp.max(el, axis=-1, keepdims=True)
    i1 = jnp.min(jnp.where(el == em, lane, big), axis=-1, keepdims=True)
    el2 = jnp.where(lane == i1, neg, el)
    em2 = jnp.max(el2, axis=-1, keepdims=True)
    i2 = jnp.min(jnp.where(el2 == em2, lane, big), axis=-1, keepdims=True)
    es = jnp.sum(jnp.exp(el - em), axis=-1, keepdims=True)
    p1 = 1.0 / es
    p2 = jnp.exp(em2 - em) / es
    den = p1 + p2
    rg_ref[0] = jnp.where(lane == 0, g_val * p1 / den, jnp.where(lane == 1, g_val * p2 / den, 0.0))
    oh1 = lane == i1
    oh2 = lane == i2
    both = jnp.where(jnp.logical_or(oh1, oh2), 1.0, 0.0)
    ts = both.shape[0]
    tri = (lax.broadcasted_iota(jnp.int32, (ts, ts), 0) > lax.broadcasted_iota(jnp.int32, (ts, ts), 1))
    pre = _dot(jnp.where(tri, 1.0, 0.0).astype(BF16), both.astype(BF16)) + carry[...]
    r1 = jnp.sum(jnp.where(oh1, pre, 0.0), axis=-1, keepdims=True)
    r2 = jnp.sum(jnp.where(oh2, pre, 0.0), axis=-1, keepdims=True)
    re_ref[0] = jnp.where(lane == 0, i1 - N_GROUPS, jnp.where(lane == 1, i2 - N_GROUPS, jnp.where(
        lane == 2, r1, jnp.where(lane == 3, r2, 0.0)))).astype(jnp.int32)
    carry[...] += jnp.sum(both, axis=0, keepdims=True)
    cnt_ref[...] = carry[...]


def _route_plumbing(rt, b, s, d, ts):
    gain, sc, sh, wr, br = rt
    ins = [gain, sc, sh, wr, br]
    in_specs = [_full(gain), _modspec(sc, ts), _modspec(sh, ts), _full(wr), _full(br)]
    out_specs = [_tok(d // 2, ts), _tok(LANES, ts), _tok(LANES, ts), pl.BlockSpec((1, LANES), lambda bb, j: (0, 0))]
    out_shape = [jax.ShapeDtypeStruct((b, s, d // 2), jnp.uint32),
                 jax.ShapeDtypeStruct((b, s, LANES), jnp.int32),
                 jax.ShapeDtypeStruct((b, s, LANES), F32),
                 jax.ShapeDtypeStruct((1, LANES), F32)]
    return ins, in_specs, out_specs, out_shape, [pltpu.VMEM((1, LANES), F32)]


def _row_copy(src, dst, src_row, dst_row, sem):
    return pltpu.make_async_copy(src.at[pl.ds(src_row, 1)], dst.at[pl.ds(dst_row, 1)], sem)


ROUTE_W = 2 * TOP_K
DMA_UNROLL = 8


def _slot(ps_ref, rt_ref, r, k):
    return ps_ref[rt_ref[0, 0, ROUTE_W * r + k]] + rt_ref[0, 0, ROUTE_W * r + TOP_K + k]


def _dispatch_kernel(ps_ref, rt_ref, h_ref, xin_hbm, xb_hbm, sem, *, tb):
    del xin_hbm

    def issue(r, c):
        for k in range(TOP_K):
            _row_copy(h_ref, xb_hbm, r, _slot(ps_ref, rt_ref, r, k), sem).start()
        return c
    lax.fori_loop(0, tb, issue, 0, unroll=DMA_UNROLL)
    for k in range(TOP_K):
        pltpu.make_async_copy(h_ref, xb_hbm.at[pl.ds(0, tb)], sem).wait()


def _dispatch_call(pad_start, route, h2, xb0, tb):
    n, dh = h2.shape
    nb = n // tb
    grid_spec = pltpu.PrefetchScalarGridSpec(
        num_scalar_prefetch=1,
        grid=(nb,),
        in_specs=[pl.BlockSpec((1, 1, ROUTE_W * tb), lambda j, ps: (j, 0, 0), memory_space=pltpu.SMEM),
                  pl.BlockSpec((tb, dh), lambda j, ps: (j, 0)),
                  pl.BlockSpec(memory_space=pl.ANY)],
        out_specs=pl.BlockSpec(memory_space=pl.ANY),
        scratch_shapes=[pltpu.SemaphoreType.DMA(())],
    )
    return pl.pallas_call(
        functools.partial(_dispatch_kernel, tb=tb),
        grid_spec=grid_spec,
        out_shape=jax.ShapeDtypeStruct(xb0.shape, xb0.dtype),
        input_output_aliases={3: 0},
        compiler_params=_cparams("arbitrary"),
        name="moe_dispatch",
    )(pad_start, route, h2, xb0)


def _ffn_kernel(be_ref, nu_ref, x_ref, w1_ref, w3_ref, w2_ref, y_ref, w1b, w3b, w2b):
    i = pl.program_id(0)
    half = w1b.shape[0] // 2

    @pl.when(jnp.logical_and(i < nu_ref[0], jnp.logical_or(i == 0, be_ref[i] != be_ref[jnp.maximum(i - 1, 0)])))
    def _():
        w1b[...] = w1_ref[0, 0].astype(BF16)
        w3b[...] = w3_ref[0, 0].astype(BF16)
        w2b[...] = w2_ref[0, 0].astype(BF16)

    @pl.when(i < nu_ref[0])
    def _():
        lo, hi = _unpack_halves(x_ref[...])
        lo = lo.astype(BF16)
        hi = hi.astype(BF16)
        a = _dot(lo, w1b[:half]) + _dot(hi, w1b[half:])
        g = _dot(lo, w3b[:half]) + _dot(hi, w3b[half:])
        mid = (a * jax.nn.sigmoid(a) * g).astype(BF16)
        y_ref[...] = _pack_halves(_dot(mid, w2b[...]))

    @pl.when(i >= nu_ref[0])
    def _():
        y_ref[...] = jnp.zeros_like(y_ref)


def _ffn_call(block_e, n_used, xb, w1, w3, w2, layer):
    n_blocks = block_e.shape[0]
    dh = xb.shape[1]
    d = 2 * dh
    blk = xb.shape[0] // n_blocks
    grid_spec = pltpu.PrefetchScalarGridSpec(
        num_scalar_prefetch=2,
        grid=(n_blocks,),
        in_specs=[pl.BlockSpec((blk, dh), lambda i, be, nu: (i, 0)),
                  pl.BlockSpec((1, 1, d, D_EXPERT), lambda i, be, nu: (layer, be[i], 0, 0)),
                  pl.BlockSpec((1, 1, d, D_EXPERT), lambda i, be, nu: (layer, be[i], 0, 0)),
                  pl.BlockSpec((1, 1, D_EXPERT, d), lambda i, be, nu: (layer, be[i], 0, 0))],
        out_specs=pl.BlockSpec((blk, dh), lambda i, be, nu: (i, 0)),
        scratch_shapes=[pltpu.VMEM((d, D_EXPERT), BF16), pltpu.VMEM((d, D_EXPERT), BF16),
                        pltpu.VMEM((D_EXPERT, d), BF16)],
    )
    return pl.pallas_call(
        _ffn_kernel,
        grid_spec=grid_spec,
        out_shape=jax.ShapeDtypeStruct(xb.shape, jnp.uint32),
        compiler_params=_cparams("arbitrary"),
        name="moe_ffn",
    )(block_e, n_used, xb, w1, w3, w2)


def _combine_kernel(ps_ref, rcur_ref, rnxt_ref, yb_hbm, x_ref, rg_ref, g2_ref, fin_ref, o_ref, b0, b1, sem,
                    *, tb, final):
    j = pl.program_id(0)
    nb = pl.num_programs(0)
    slot = j % 2

    def fetch(rref, sl):
        def issue(r, c):
            _row_copy(yb_hbm, b0.at[sl], _slot(ps_ref, rref, r, 0), r, sem.at[sl]).start()
            _row_copy(yb_hbm, b1.at[sl], _slot(ps_ref, rref, r, 1), r, sem.at[sl]).start()
            return c
        lax.fori_loop(0, tb, issue, 0, unroll=DMA_UNROLL)

    @pl.when(j == 0)
    def _():
        fetch(rcur_ref, 0)

    @pl.when(j + 1 < nb)
    def _():
        fetch(rnxt_ref, 1 - slot)

    pltpu.make_async_copy(yb_hbm.at[pl.ds(0, tb)], b0.at[slot], sem.at[slot]).wait()
    pltpu.make_async_copy(yb_hbm.at[pl.ds(0, tb)], b1.at[slot], sem.at[slot]).wait()
    rg = rg_ref[...]
    lo0, hi0 = _unpack_halves(b0[slot])
    lo1, hi1 = _unpack_halves(b1[slot])
    g0 = rg[:, 0:1]
    g1 = rg[:, 1:2]
    y = jnp.concatenate([g0 * lo0 + g1 * lo1, g0 * hi0 + g1 * hi1], axis=1)
    xn = x_ref[...] + g2_ref[0] * y
    if final:
        xn = _rms(xn, fin_ref[...])
    o_ref[...] = xn


def _combine_call(pad_start, route, yb, x, rg, g2, fin, tb, final):
    b, s, d = x.shape
    n = b * s
    nb = n // tb
    npb = s // tb
    dh = yb.shape[1]
    if g2.shape[1] == 1:
        gspec = pl.BlockSpec((1, 1, d), lambda j, ps: (j // npb, 0, 0))
    else:
        gspec = pl.BlockSpec((1, tb, d), lambda j, ps: (j // npb, j % npb, 0))
    grid_spec = pltpu.PrefetchScalarGridSpec(
        num_scalar_prefetch=1,
        grid=(nb,),
        in_specs=[pl.BlockSpec((1, 1, ROUTE_W * tb), lambda j, ps: (j, 0, 0), memory_space=pltpu.SMEM),
                  pl.BlockSpec((1, 1, ROUTE_W * tb), lambda j, ps: (jnp.minimum(j + 1, nb - 1), 0, 0),
                               memory_space=pltpu.SMEM),
                  pl.BlockSpec(memory_space=pl.ANY),
                  pl.BlockSpec((tb, d), lambda j, ps: (j, 0)), pl.BlockSpec((tb, LANES), lambda j, ps: (j, 0)),
                  gspec, pl.BlockSpec(fin.shape, lambda j, ps: (0, 0))],
        out_specs=pl.BlockSpec((tb, d), lambda j, ps: (j, 0)),
        scratch_shapes=[pltpu.VMEM((2, tb, dh), jnp.uint32), pltpu.VMEM((2, tb, dh), jnp.uint32),
                        pltpu.SemaphoreType.DMA((2,))],
    )
    out = pl.pallas_call(
        functools.partial(_combine_kernel, tb=tb, final=final),
        grid_spec=grid_spec,
        out_shape=jax.ShapeDtypeStruct((n, d), F32),
        compiler_params=_cparams("arbitrary"),
        name="moe_combine",
    )(pad_start, route, route, yb, x.reshape(n, d), rg.reshape(n, LANES), g2, fin)
    return out.reshape(b, s, d)


def _moe(x, routed, g2, mp, fin, ts, final):
    b, s, d = x.shape
    n = b * s
    h2, r_e, r_g, cnt = routed
    counts = cnt[0, N_GROUPS:N_GROUPS + N_EXPERTS].astype(jnp.int32)
    blk = MOE_BLOCK if n * TOP_K >= N_EXPERTS * MOE_BLOCK else MOE_BLOCK_SMALL
    padded = (counts + blk - 1) // blk * blk
    pad_end = jnp.cumsum(padded)
    pad_start = (pad_end - padded).astype(jnp.int32)
    n_blocks = (n * TOP_K + N_EXPERTS * (blk - 1) + blk - 1) // blk
    blk0 = jnp.arange(n_blocks, dtype=jnp.int32) * blk
    block_e = jnp.minimum(jnp.sum((pad_end[None, :] <= blk0[:, None]).astype(jnp.int32), axis=1), N_EXPERTS - 1)
    n_used = (pad_end[-1:] // blk).astype(jnp.int32)
    route = r_e[:, :, :ROUTE_W].reshape(n // ts, 1, ROUTE_W * ts)
    xb0 = jnp.zeros((n_blocks * blk, d // 2), jnp.uint32)
    xb = _dispatch_call(pad_start, route, h2.reshape(n, d // 2), xb0, ts)
    yb = _ffn_call(block_e, n_used, xb, mp["w1"], mp["w3"], mp["w2"], mp["layer"])
    return _combine_call(pad_start, route, yb, x, r_g, g2, fin, ts, final)


def _prep_ab(l, p):
    w_in = p["w_in_ab"][l]
    o2 = S5_WIDTH + MLA_Q_RANK + MLA_KV_RANK
    half = MLA_ROPE // 2
    d = w_in.shape[0]
    kr = w_in[:, o2:]
    kr_rot = jnp.concatenate([-kr[:, half:], kr[:, :half]], axis=1)
    z64 = jnp.zeros((d, MLA_NOPE), F32)
    z32 = jnp.zeros((d, LANES - MLA_NOPE - MLA_ROPE), F32)
    w_ext = jnp.concatenate([w_in[:, :o2], z64, kr, z32, z64, kr_rot, z32], axis=1)
    qu = p["mla_q_up"][l].reshape(MLA_Q_RANK, MLA_HEADS, MLA_NOPE + MLA_ROPE)
    qn, qr = qu[:, :, :MLA_NOPE], qu[:, :, MLA_NOPE:]
    zq = jnp.zeros((MLA_Q_RANK, MLA_HEADS, LANES - MLA_NOPE - MLA_ROPE), F32)
    wqa = jnp.concatenate([qn, qr, zq], axis=2).reshape(MLA_Q_RANK, MLA_HEADS * LANES)
    wqb = jnp.concatenate([jnp.zeros_like(qn), -qr[:, :, half:], qr[:, :, :half], zq], axis=2)
    wqb = wqb.reshape(MLA_Q_RANK, MLA_HEADS * LANES)
    kvu = p["mla_kv_up"][l].reshape(MLA_KV_RANK, MLA_HEADS, MLA_NOPE + MLA_V)
    wk = jnp.concatenate([kvu[:, :, :MLA_NOPE], jnp.zeros((MLA_KV_RANK, MLA_HEADS, LANES - MLA_NOPE), F32)], axis=2)
    wo = p["w_out_ab"][l]
    out = dict(
        w_in=w_ext.astype(BF16), q_norm=p["mla_q_norm"][l].reshape(1, -1), kv_norm=p["mla_kv_norm"][l].reshape(1, -1),
        wqa=wqa.astype(BF16), wqb=wqb.astype(BF16),
        wk=wk.reshape(MLA_KV_RANK, MLA_HEADS * LANES).astype(BF16),
        wv=jnp.concatenate([kvu[:, :, MLA_NOPE:], jnp.zeros((MLA_KV_RANK, MLA_HEADS, LANES - MLA_V), F32)],
                           axis=2).reshape(MLA_KV_RANK, MLA_HEADS * LANES).astype(BF16),
        s5_d=p["s5_d"][l].reshape(1, -1), glu_w=p["s5_glu_w"][l].astype(BF16), glu_b=p["s5_glu_b"][l].reshape(1, -1),
        w_out_s5=wo[:S5_WIDTH].astype(BF16),
        w_out_at=wo[S5_WIDTH:].astype(BF16))
    out.update(_s5_mats(p["s5_a_re"][l], p["s5_a_im"][l], p["s5_b_re"][l], p["s5_b_im"][l],
                        p["s5_c_re"][l], p["s5_c_im"][l], p["s5_log_dt"][l]))
    return out


def _prep_c(l, p):
    w = p["w_in_c"][l]
    pad = jnp.zeros((w.shape[0], LANES - 2 * GDN_HEADS), F32)

    def row(v):
        return jnp.concatenate([v, jnp.zeros((LANES - GDN_HEADS,), F32)]).reshape(1, LANES)

    return dict(w_in=jnp.concatenate([w, pad], axis=1).astype(BF16), conv_w=p["conv_w"][l],
                alog=row(p["gdn_a_log"][l]), dtb=row(p["gdn_dt_bias"][l]),
                gn=p["gdn_norm"][l].reshape(1, -1), w_out=p["w_out_c"][l].astype(BF16))


def _prep_moe(layer, p):
    d = p["moe_w_group"].shape[1]
    wr = jnp.concatenate([p["moe_w_group"][layer], p["moe_w_expert"][layer],
                          jnp.zeros((d, LANES - N_GROUPS - N_EXPERTS), F32)], axis=1)
    br = jnp.concatenate([p["moe_b_group"][layer], p["moe_b_expert"][layer],
                          jnp.zeros((LANES - N_GROUPS - N_EXPERTS,), F32)]).reshape(1, LANES)
    wr_hi = wr.astype(BF16)
    wr_lo = (wr - wr_hi.astype(F32)).astype(BF16)
    return dict(wr=jnp.stack([wr_hi, wr_lo]), br=br, w1=p["moe_w1"], w3=p["moe_w3"], w2=p["moe_w2"], layer=layer)


def _rope_tables(pos):
    half = MLA_ROPE // 2
    inv = ROPE_THETA ** (-jnp.arange(half, dtype=F32) / half)
    ang = pos.astype(F32)[:, None] * inv[None, :]
    cos, sin = jnp.cos(ang), jnp.sin(ang)
    n = pos.shape[0]
    z64 = jnp.zeros((n, MLA_NOPE), F32)
    z32 = jnp.zeros((n, LANES - MLA_NOPE - MLA_ROPE), F32)
    scale = (MLA_NOPE + MLA_ROPE) ** -0.5 * math.log2(math.e)
    ck = jnp.concatenate([z64, cos, cos, z32], axis=1)
    sk = jnp.concatenate([z64, sin, sin, z32], axis=1)
    cq = jnp.concatenate([jnp.ones_like(z64), cos, cos, z32], axis=1) * scale
    return cq, sk * scale, ck, sk


def _trunk(x, mods, caches, wab, wc, wmoe, p, flat):
    b, s, d = x.shape
    lat_c, rope_c, s5re_c, s5im_c, conv_c, gdn_c = caches
    past = 0 if lat_c is None else lat_c.shape[2]
    pos = past + jnp.arange(s)
    if flat:
        xt = x.reshape(1, b * s, d)
        ts = b * s
        pos_rows = jnp.tile(pos, b)

        def mrow(m):
            return jnp.repeat(m, s, axis=0).reshape(1, b * s, d)
    else:
        xt = x
        ts = TOKEN_BLOCK
        pos_rows = pos

        def mrow(m):
            return m.reshape(b, 1, d)
    bt, st = xt.shape[:2]
    outs = {}
    depth = p["norm_mix"].shape[0]
    for layer in range(depth):
        l = layer // 2
        sh1, sc1, g1, sh2, sc2, g2 = [mrow(m) for m in jnp.split(mods[layer], 6, axis=-1)]
        gain = p["norm_mix"][layer].reshape(1, d)
        rt = (p["norm_ffn"][layer].reshape(1, d), sc2, sh2, wmoe[layer]["wr"], wmoe[layer]["br"])
        if layer % 2 == 0:
            wp = wab[l]
            u, q, lat, kr = _ab_in_call(xt, gain, sc1, sh1, wp, _rope_tables(pos_rows), ts)
            lat_b = lat.reshape(b, s, MLA_KV_RANK)
            kr_b = kr.reshape(b, s, LANES)
            outs["lat"] = lat_b
            outs["krope"] = kr_b[:, :, MLA_NOPE:MLA_NOPE + MLA_ROPE]
            if lat_c is None:
                lat_all, kr_all, sk_valid, tkv = lat_b, kr_b, s, TOKEN_BLOCK
                tq, tk = min(ATTN_TQ, s), min(ATTN_TK, s)
            else:
                krc = jnp.pad(rope_c[l], ((0, 0), (0, 0), (MLA_NOPE, LANES - MLA_NOPE - MLA_ROPE)))
                sk_valid = past + s
                skp = (sk_valid + LANES - 1) // LANES * LANES
                lat_all = jnp.pad(jnp.concatenate([lat_c[l], lat_b], axis=1), ((0, 0), (0, skp - sk_valid), (0, 0)))
                kr_all = jnp.pad(jnp.concatenate([krc, kr_b], axis=1), ((0, 0), (0, skp - sk_valid), (0, 0)))
                tkv, tq, tk = skp, s, skp
            kk, vv = _kv_call(lat_all, kr_all, wp, tkv)
            qh = q.reshape(MLA_HEADS, b, s, LANES).transpose(1, 0, 2, 3) if flat else q
            attn = _attn_call(qh, kk, vv, tq, tk, past, sk_valid).reshape(bt, st, MLA_HEADS * MLA_V)
            nchk = s // S5_CHUNK
            r = b * nchk
            u2 = u.reshape(b * s, S5_WIDTH)
            tr = min(r, S5_ROW_BLOCK)
            wre, wim = _s5_w_call(u2, wp["bre"], wp["bim"], tr)
            ngp = S5_GROUPS * S5_STATE

            def to_pairs(v):
                return v.reshape(b, 1, ngp)

            if s5re_c is None:
                x0re = jnp.zeros((b, 1, ngp), F32)
                x0im = jnp.zeros((b, 1, ngp), F32)
            else:
                x0re, x0im = to_pairs(s5re_c[l]), to_pairs(s5im_c[l])
            nchp = (nchk + 7) // 8 * 8
            tc = min(nchp, S5_SCAN_BLOCK)
            wre3 = jnp.pad(wre.reshape(b, nchk, ngp), ((0, 0), (0, nchp - nchk), (0, 0)))
            wim3 = jnp.pad(wim.reshape(b, nchk, ngp), ((0, 0), (0, nchp - nchk), (0, 0)))
            xere, xeim = _s5_scan_call(wre3, wim3, wp["lre"], wp["lim"], x0re, x0im, tc)
            outs["s5re"] = xere[:, nchk - 1].reshape(b, S5_GROUPS, S5_STATE)
            outs["s5im"] = xeim[:, nchk - 1].reshape(b, S5_GROUPS, S5_STATE)
            xsre = jnp.concatenate([x0re, xere[:, :nchk - 1]], axis=1).reshape(r, ngp)
            xsim = jnp.concatenate([x0im, xeim[:, :nchk - 1]], axis=1).reshape(r, ngp)
            ys = _s5_y_call(u2, xsre, xsim, wp["mst"], wp["cc"], tr).reshape(bt, st, S5_WIDTH)
            xt, *routed = _ab_out_call(xt, ys, u, attn, wp, g1, rt, ts)
        else:
            wp = wc[l]
            qkv, z, ab = _c_in_call(xt, gain, sc1, sh1, wp["w_in"], ts)
            qkv_b = qkv.reshape(b, s, GDN_QKV)
            outs["conv"] = qkv_b[:, s - (CONV_W - 1):].astype(F32)
            if conv_c is None:
                st8 = jnp.zeros((b, 8, GDN_QKV), BF16)
                s0 = jnp.zeros((b, GDN_HEADS, GDN_DK, GDN_DV), F32)
            else:
                st8 = jnp.pad(conv_c[l], ((0, 0), (8 - (CONV_W - 1), 0), (0, 0))).astype(BF16)
                s0 = gdn_c[l]
            lg = min(s, CHUNK)
            wq, um, kgt, pm, ee = _gdn_prep_call(qkv_b, st8, wp["conv_w"], ab.reshape(b, s, LANES),
                                                 wp["alog"], wp["dtb"], lg)
            o, sfin = _gdn_seq_call(wq, um, kgt, pm, ee, s0, lg)
            outs["gdn"] = sfin
            xt, *routed = _c_out_call(xt, o.reshape(bt, st, GDN_VW), z, wp["gn"], wp["w_out"], g1, rt, ts)
        final = layer == depth - 1
        xt = _moe(xt, routed, g2, wmoe[layer], p["norm_final"].reshape(1, d), ts, final)
    y = xt.reshape(b, s, d)
    return (y, outs["lat"][None], outs["krope"][None], outs["s5re"][None], outs["s5im"][None],
            outs["conv"][None], outs["gdn"][None])


def kernel(x_prompt, x_sample, c_prompt, c_sample, cache_mla_latent, cache_mla_krope, state_s5_re, state_s5_im,
           state_conv, state_gdn, w_mod, b_mod, norm_mix, norm_ffn, norm_final, w_in_ab, s5_a_re, s5_a_im,
           s5_b_re, s5_b_im, s5_c_re, s5_c_im, s5_d, s5_log_dt, s5_glu_w, s5_glu_b, mla_q_norm, mla_q_up,
           mla_kv_norm, mla_kv_up, w_out_ab, w_in_c, conv_w, gdn_a_log, gdn_dt_bias, gdn_norm, w_out_c,
           moe_w_group, moe_b_group, moe_w_expert, moe_b_expert, moe_w1, moe_w3, moe_w2):
    p = dict(w_mod=w_mod, b_mod=b_mod, norm_mix=norm_mix, norm_ffn=norm_ffn, norm_final=norm_final,
             w_in_ab=w_in_ab, s5_a_re=s5_a_re, s5_a_im=s5_a_im, s5_b_re=s5_b_re, s5_b_im=s5_b_im,
             s5_c_re=s5_c_re, s5_c_im=s5_c_im, s5_d=s5_d, s5_log_dt=s5_log_dt, s5_glu_w=s5_glu_w,
             s5_glu_b=s5_glu_b, mla_q_norm=mla_q_norm, mla_q_up=mla_q_up, mla_kv_norm=mla_kv_norm,
             mla_kv_up=mla_kv_up, w_out_ab=w_out_ab, w_in_c=w_in_c, conv_w=conv_w, gdn_a_log=gdn_a_log,
             gdn_dt_bias=gdn_dt_bias, gdn_norm=gdn_norm, w_out_c=w_out_c, moe_w_group=moe_w_group,
             moe_b_group=moe_b_group, moe_w_expert=moe_w_expert, moe_b_expert=moe_b_expert,
             moe_w1=moe_w1, moe_w3=moe_w3, moe_w2=moe_w2)
    depth = norm_mix.shape[0]
    bp, bs = c_prompt.shape[0], c_sample.shape[0]
    nb = (bp + bs + 7) // 8 * 8
    c_all = jnp.pad(jnp.concatenate([c_prompt, c_sample], axis=0), ((0, nb - bp - bs), (0, 0)))
    mods = _mod_call(c_all, w_mod, b_mod)
    wab = [_prep_ab(l, p) for l in range((depth + 1) // 2)]
    wc = [_prep_c(l, p) for l in range(depth // 2)]
    wmoe = [_prep_moe(layer, p) for layer in range(depth)]
    none6 = (None,) * 6
    outp = _trunk(x_prompt, mods[:, :bp], none6, wab, wc, wmoe, p, flat=False)
    caches = (cache_mla_latent, cache_mla_krope, state_s5_re, state_s5_im, state_conv, state_gdn)
    outs = _trunk(x_sample, mods[:, bp:bp + bs], caches, wab, wc, wmoe, p, flat=True)
    return (outp[0], outs[0]) + tuple(outp[1:]) + tuple(outs[1:])
```

```python
import functools
import math

import jax
import jax.numpy as jnp
from jax import lax
from jax.experimental import pallas as pl
from jax.experimental.pallas import tpu as pltpu

F32 = jnp.float32
BF16 = jnp.bfloat16
EPS = 1e-6

CHUNK = 64
S5_WIDTH = 512
S5_GROUP = 16
S5_GROUPS = 32
S5_STATE = 64
S5_CHUNK = 16
S5_LB = 128 // S5_GROUP
MLA_HEADS = 8
MLA_NOPE = 64
MLA_ROPE = 32
MLA_V = 64
MLA_Q_RANK = 384
MLA_KV_RANK = 256
ROPE_THETA = 10000.0
GDN_HEADS = 8
GDN_DK = 128
GDN_DV = 128
CONV_W = 4
GDN_QKV = GDN_HEADS * (2 * GDN_DK + GDN_DV)
GDN_VW = GDN_HEADS * GDN_DV
N_GROUPS = 4
EXPERTS_PER_GROUP = 8
N_EXPERTS = 32
D_EXPERT = 512
TOP_K = 2
MOE_BLOCK = 512
MOE_BLOCK_SMALL = 128
ATTN_SUB = 512
ATTN_TQ = 512
ATTN_TK = 512
TOKEN_BLOCK = 256
S5_ROW_BLOCK = 256
S5_SCAN_BLOCK = 128
LANES = 128
VMEM_LIMIT = 48 * 1024 * 1024


def _cparams(*sem):
    return pltpu.CompilerParams(dimension_semantics=sem, vmem_limit_bytes=VMEM_LIMIT)


def _dot(a, b):
    return jnp.dot(a, b, preferred_element_type=F32)


def _dot_nt(a, b):
    return lax.dot_general(a, b, (((1,), (1,)), ((), ())), preferred_element_type=F32)


def _full(arr):
    nd = arr.ndim
    return pl.BlockSpec(arr.shape, lambda *_: (0,) * nd)


def _tok(width, ts):
    return pl.BlockSpec((1, ts, width), lambda b, j: (b, j, 0))


def _modspec(arr, ts):
    if arr.shape[1] == 1:
        return pl.BlockSpec((1, 1, arr.shape[2]), lambda b, j: (b, 0, 0))
    return pl.BlockSpec((1, ts, arr.shape[2]), lambda b, j: (b, j, 0))


def _normmod(x, gain, sc, sh):
    ms = jnp.mean(x * x, axis=-1, keepdims=True)
    return x * lax.rsqrt(ms + EPS) * gain * (1.0 + sc) + sh


def _rms(x, gain):
    return x * lax.rsqrt(jnp.mean(x * x, axis=-1, keepdims=True) + EPS) * gain


def _mod_kernel(c_ref, w_ref, b_ref, o_ref):
    c = c_ref[...]
    a = (c * jax.nn.sigmoid(c)).astype(BF16)
    o_ref[0] = _dot(a, w_ref[0]) + b_ref[0]


def _mod_call(c_all, w_mod, b_mod):
    depth, d, n6 = w_mod.shape
    bp = c_all.shape[0]
    tn = 1536
    return pl.pallas_call(
        _mod_kernel,
        grid=(depth, n6 // tn),
        in_specs=[pl.BlockSpec((bp, d), lambda l, n: (0, 0)),
                  pl.BlockSpec((1, d, tn), lambda l, n: (l, 0, n)),
                  pl.BlockSpec((1, 1, tn), lambda l, n: (l, 0, n))],
        out_specs=pl.BlockSpec((1, bp, tn), lambda l, n: (l, 0, n)),
        out_shape=jax.ShapeDtypeStruct((depth, bp, n6), F32),
        compiler_params=_cparams("parallel", "parallel"),
        name="mod",
    )(c_all, w_mod.astype(BF16), b_mod.reshape(depth, 1, n6))


def _ab_in_kernel(x_ref, gain_ref, sc_ref, sh_ref, w_ref, qg_ref, wqa_ref, wqb_ref, kvg_ref,
                  cq_ref, sq_ref, ck_ref, sk_ref, u_ref, q_ref, lat_ref, kr_ref):
    h = _normmod(x_ref[0], gain_ref[...], sc_ref[0], sh_ref[0]).astype(BF16)
    proj = _dot(h, w_ref[...])
    u_ref[0] = proj[:, :S5_WIDTH]
    o1 = S5_WIDTH + MLA_Q_RANK
    o2 = o1 + MLA_KV_RANK
    qn = _rms(proj[:, S5_WIDTH:o1], qg_ref[...]).astype(BF16)
    qa = _dot(qn, wqa_ref[...])
    qb = _dot(qn, wqb_ref[...])
    cq = cq_ref[...]
    sq = sq_ref[...]
    for hh in range(MLA_HEADS):
        sl = slice(LANES * hh, LANES * (hh + 1))
        q_ref[0, hh] = (qa[:, sl] * cq + qb[:, sl] * sq).astype(BF16)
    lat_ref[0] = _rms(proj[:, o1:o2], kvg_ref[...])
    kr_ref[0] = proj[:, o2:o2 + LANES] * ck_ref[...] + proj[:, o2 + LANES:o2 + 2 * LANES] * sk_ref[...]


def _ab_in_call(x, gain, sc, sh, wp, tabs, ts):
    b, s, d = x.shape
    cq, sq, ck, sk = tabs
    tab = pl.BlockSpec((ts, LANES), lambda bb, j: (j, 0))
    ins = [x, gain, sc, sh, wp["w_in"], wp["q_norm"], wp["wqa"], wp["wqb"], wp["kv_norm"], cq, sq, ck, sk]
    specs = [_tok(d, ts), _full(gain), _modspec(sc, ts), _modspec(sh, ts), _full(wp["w_in"]), _full(wp["q_norm"]),
             _full(wp["wqa"]), _full(wp["wqb"]), _full(wp["kv_norm"]), tab, tab, tab, tab]
    return pl.pallas_call(
        _ab_in_kernel,
        grid=(b, s // ts),
        in_specs=specs,
        out_specs=[_tok(S5_WIDTH, ts),
                   pl.BlockSpec((1, MLA_HEADS, ts, LANES), lambda bb, j: (bb, 0, j, 0)),
                   _tok(MLA_KV_RANK, ts), _tok(LANES, ts)],
        out_shape=[jax.ShapeDtypeStruct((b, s, S5_WIDTH), F32),
                   jax.ShapeDtypeStruct((b, MLA_HEADS, s, LANES), BF16),
                   jax.ShapeDtypeStruct((b, s, MLA_KV_RANK), F32),
                   jax.ShapeDtypeStruct((b, s, LANES), F32)],
        compiler_params=_cparams("parallel", "parallel"),
        name="ab_in",
    )(*ins)


def _kv_kernel(lat_ref, kr_ref, wk_ref, wv_ref, k_ref, v_ref):
    lat = lat_ref[0].astype(BF16)
    kk = _dot(lat, wk_ref[...])
    vv = _dot(lat, wv_ref[...])
    kr = kr_ref[0]
    ones = (lax.broadcasted_iota(jnp.int32, (1, LANES), 1) >= MLA_V).astype(F32)
    for hh in range(MLA_HEADS):
        k_ref[0, hh] = (kk[:, LANES * hh:LANES * (hh + 1)] + kr).astype(BF16)
        v_ref[0, hh] = (vv[:, LANES * hh:LANES * (hh + 1)] + ones).astype(BF16)


def _kv_call(lat_all, kr_all, wp, ts):
    b, sk, _ = lat_all.shape
    hspec = pl.BlockSpec((1, MLA_HEADS, ts, LANES), lambda bb, j: (bb, 0, j, 0))
    hshape = jax.ShapeDtypeStruct((b, MLA_HEADS, sk, LANES), BF16)
    return pl.pallas_call(
        _kv_kernel,
        grid=(b, sk // ts),
        in_specs=[_tok(MLA_KV_RANK, ts), _tok(LANES, ts), _full(wp["wk"]), _full(wp["wv"])],
        out_specs=[hspec, hspec],
        out_shape=[hshape, hshape],
        compiler_params=_cparams("parallel", "parallel"),
        name="kv_up",
    )(lat_all, kr_all, wp["wk"], wp["wv"])


def _attn_kernel(q_ref, k_ref, v_ref, *rest, tq, tk, q_off, sk_valid):
    o_ref = rest[-1]
    bias_ref = rest[0] if len(rest) == 2 else None
    i = pl.program_id(2)
    nsub = max(tq // ATTN_SUB, 1)
    tqs = tq // nsub
    chains = [(hh, qi) for hh in range(2) for qi in range(nsub)]
    qs = [q_ref[0, hh, qi * tqs:(qi + 1) * tqs, :] for hh, qi in chains]
    q0 = q_off + i * tq
    lim_full = (q0 // CHUNK + 1) * CHUNK
    lim_tot = jnp.minimum(((q0 + tq - 1) // CHUNK + 1) * CHUNK, sk_valid)
    n_full = jnp.minimum(lim_full // tk, sk_valid // tk)
    n_tot = (lim_tot + tk - 1) // tk

    def step(j, carry, masked):
        off = pl.multiple_of(j * tk, tk)
        nc = range(len(chains))
        if masked and bias_ref is None:
            kpos = off + lax.broadcasted_iota(jnp.int32, (tqs, tk), 1)
            ok = []
            for qi in range(nsub):
                qpos = q0 + qi * tqs + lax.broadcasted_iota(jnp.int32, (tqs, tk), 0)
                ok.append(jnp.logical_and(kpos // CHUNK <= qpos // CHUNK, kpos < sk_valid))
        kt = [k_ref[0, hh, pl.ds(off, tk), :] for hh in range(2)]
        vt = [v_ref[0, hh, pl.ds(off, tk), :] for hh in range(2)]
        s = [_dot_nt(qs[c], kt[chains[c][0]]) for c in nc]
        if masked and bias_ref is None:
            s = [jnp.where(ok[chains[c][1]], s[c], -1e30) for c in nc]
        elif masked:
            s = [s[c] + bias_ref[chains[c][1] * tqs:(chains[c][1] + 1) * tqs, :] for c in nc]
        m_new = [jnp.maximum(carry[c][0], jnp.max(s[c], axis=-1, keepdims=True)) for c in nc]
        alpha = [jnp.exp2(carry[c][0] - m_new[c]) for c in nc]
        p = [jnp.exp2((s[c] - m_new[c]).astype(BF16)) for c in nc]
        acc = [alpha[c] * carry[c][1] + _dot(p[c], vt[chains[c][0]]) for c in nc]
        return tuple((m_new[c], acc[c]) for c in nc)

    one = (jnp.full((tqs, 1), -1e30, F32), jnp.zeros((tqs, LANES), F32))
    carry = lax.fori_loop(0, n_full, functools.partial(step, masked=False), (one,) * len(chains))
    carry = lax.fori_loop(n_full, n_tot, functools.partial(step, masked=True), carry)
    lane = lax.broadcasted_iota(jnp.int32, (tqs, LANES), 1)
    for qi in range(nsub):
        a0 = carry[chains.index((0, qi))][1]
        a1 = carry[chains.index((1, qi))][1]
        r0 = a0 / pltpu.roll(a0, MLA_V, axis=1)
        r1 = a1 / pltpu.roll(a1, MLA_V, axis=1)
        o_ref[0, qi * tqs:(qi + 1) * tqs, :] = jnp.where(
            lane < MLA_V, r0, pltpu.roll(r1, MLA_V, axis=1)).astype(o_ref.dtype)


def _attn_call(q, k, v, tq, tk, q_off, sk_valid):
    b, h, sq, _ = q.shape
    sk = k.shape[2]
    kern = functools.partial(_attn_kernel, tq=tq, tk=tk, q_off=q_off, sk_valid=sk_valid)
    kv_mode = pl.Buffered(2 if sq == tq else 1)
    ins = [q, k, v]
    specs = [pl.BlockSpec((1, 2, tq, LANES), lambda bb, hp, i: (bb, hp, i, 0)),
             pl.BlockSpec((1, 2, sk, LANES), lambda bb, hp, i: (bb, hp, 0, 0), pipeline_mode=kv_mode),
             pl.BlockSpec((1, 2, sk, LANES), lambda bb, hp, i: (bb, hp, 0, 0), pipeline_mode=kv_mode)]
    if q_off == 0 and tq == tk and sk_valid % tk == 0:
        r = jnp.arange(tq, dtype=jnp.int32)
        bias = jnp.where(r[None, :] // CHUNK <= r[:, None] // CHUNK, 0.0, -1e30).astype(F32)
        ins.append(bias)
        specs.append(pl.BlockSpec((tq, tk), lambda bb, hp, i: (0, 0)))
    return pl.pallas_call(
        kern,
        grid=(b, h // 2, sq // tq),
        in_specs=specs,
        out_specs=pl.BlockSpec((1, tq, LANES), lambda bb, hp, i: (bb, i, hp)),
        out_shape=jax.ShapeDtypeStruct((b, sq, h * MLA_V), BF16),
        compiler_params=_cparams("parallel", "parallel", "arbitrary"),
        name="mla_attn",
    )(*ins)


def _chunk_rows(ref, s, tr):
    return ref[pl.ds(s, tr, stride=S5_CHUNK), :]


def _s5_w_kernel(u_ref, bre_ref, bim_ref, wre_ref, wim_ref, *, tr):
    ucat = jnp.concatenate([_chunk_rows(u_ref, s, tr).astype(BF16) for s in range(S5_CHUNK)], axis=1)
    wre_ref[...] = _dot(ucat, bre_ref[0])
    wim_ref[...] = _dot(ucat, bim_ref[0])


def _s5_w_call(u2, bre, bim, tr):
    n = u2.shape[0]
    r = n // S5_CHUNK
    nq = bre.shape[0]
    sw = S5_LB * S5_STATE
    out = jax.ShapeDtypeStruct((r, nq * sw), F32)
    ospec = pl.BlockSpec((tr, sw), lambda q, i: (i, q))
    wspec = pl.BlockSpec((1, S5_CHUNK * LANES, sw), lambda q, i: (q, 0, 0))
    return pl.pallas_call(
        functools.partial(_s5_w_kernel, tr=tr),
        grid=(nq, r // tr),
        in_specs=[pl.BlockSpec((tr * S5_CHUNK, LANES), lambda q, i: (i, q)), wspec, wspec],
        out_specs=[ospec, ospec],
        out_shape=[out, out],
        compiler_params=_cparams("parallel", "parallel"),
        name="s5_chunk_in",
    )(u2, bre, bim)


def _s5_scan_kernel(wre_ref, wim_ref, lre_ref, lim_ref, x0re_ref, x0im_ref, ore_ref, oim_ref, sre, sim, *, tc):
    @pl.when(pl.program_id(1) == 0)
    def _():
        sre[...] = x0re_ref[0]
        sim[...] = x0im_ref[0]

    lr = lre_ref[...]
    li = lim_ref[...]

    def blk(t, carry):
        xr, xi = carry
        base = pl.multiple_of(t * 8, 8)
        wr = wre_ref[0, pl.ds(base, 8), :]
        wi = wim_ref[0, pl.ds(base, 8), :]
        rows_r, rows_i = [], []
        for r in range(8):
            nr = lr * xr - li * xi + wr[r:r + 1]
            ni = lr * xi + li * xr + wi[r:r + 1]
            xr, xi = nr, ni
            rows_r.append(xr)
            rows_i.append(xi)
        ore_ref[0, pl.ds(base, 8), :] = jnp.concatenate(rows_r, axis=0)
        oim_ref[0, pl.ds(base, 8), :] = jnp.concatenate(rows_i, axis=0)
        return xr, xi

    xr, xi = lax.fori_loop(0, tc // 8, blk, (sre[...], sim[...]))
    sre[...] = xr
    sim[...] = xi


def _s5_scan_call(wre, wim, lre, lim, x0re, x0im, tc):
    b, nch, n = wre.shape
    wspec = pl.BlockSpec((1, tc, n), lambda bb, c: (bb, c, 0))
    lspec = pl.BlockSpec((1, n), lambda bb, c: (0, 0))
    xspec = pl.BlockSpec((1, 1, n), lambda bb, c: (bb, 0, 0))
    out = jax.ShapeDtypeStruct((b, nch, n), F32)
    return pl.pallas_call(
        functools.partial(_s5_scan_kernel, tc=tc),
        grid=(b, nch // tc),
        in_specs=[wspec, wspec, lspec, lspec, xspec, xspec],
        out_specs=[wspec, wspec],
        out_shape=[out, out],
        scratch_shapes=[pltpu.VMEM((1, n), F32), pltpu.VMEM((1, n), F32)],
        compiler_params=_cparams("parallel", "arbitrary"),
        name="s5_scan",
    )(wre, wim, lre, lim, x0re, x0im)


def _s5_y_kernel(u_ref, xre_ref, xim_ref, m_ref, c_ref, y_ref, *, tr):
    L = S5_CHUNK
    urev = jnp.concatenate([_chunk_rows(u_ref, s, tr).astype(BF16) for s in reversed(range(L))], axis=1)
    xcat = jnp.concatenate([xre_ref[...].astype(BF16), xim_ref[...].astype(BF16)], axis=1)
    for t in range(L):
        y = _dot(urev[:, (L - 1 - t) * LANES:], m_ref[0, :(t + 1) * LANES, :]) + _dot(xcat, c_ref[0, t])
        y_ref[pl.ds(t, tr, stride=L), :] = y


def _s5_y_call(u2, xre, xim, mst, cc, tr):
    n, w = u2.shape
    nq = mst.shape[0]
    sw = S5_LB * S5_STATE
    uspec = pl.BlockSpec((tr * S5_CHUNK, LANES), lambda q, i: (i, q))
    xspec = pl.BlockSpec((tr, sw), lambda q, i: (i, q))
    return pl.pallas_call(
        functools.partial(_s5_y_kernel, tr=tr),
        grid=(nq, n // (tr * S5_CHUNK)),
        in_specs=[uspec, xspec, xspec,
                  pl.BlockSpec((1, S5_CHUNK * LANES, LANES), lambda q, i: (q, 0, 0)),
                  pl.BlockSpec((1, S5_CHUNK, 2 * sw, LANES), lambda q, i: (q, 0, 0, 0))],
        out_specs=uspec,
        out_shape=jax.ShapeDtypeStruct((n, w), F32),
        compiler_params=_cparams("parallel", "parallel"),
        name="s5_chunk_out",
    )(u2, xre, xim, mst, cc)


def _s5_mats(a_re, a_im, b_re, b_im, c_re, c_im, log_dt):
    g, p = a_re.shape
    L = S5_CHUNK
    lam = lax.complex(jnp.minimum(a_re, -1e-4), a_im)
    lamdt = lam * jnp.exp(log_dt)[:, None]
    lam_bar = jnp.exp(lamdt)
    b_bar = ((lam_bar - 1.0) / lam)[:, :, None] * lax.complex(b_re, b_im)
    cm = lax.complex(c_re, c_im)
    pw = jnp.exp(lamdt[:, :, None] * jnp.arange(L + 1, dtype=F32))
    mm = pw[:, :, :L, None] * b_bar[:, :, None, :]
    kd = jnp.sum((cm[:, :, :, None, None] * mm[:, None]).real, axis=2).transpose(0, 2, 1, 3)
    bp = pw[:, :, L - 1 - jnp.arange(L)][:, :, :, None] * b_bar[:, :, None, :]
    cp = cm[:, :, :, None] * pw[:, None, :, 1:]
    nq = g // S5_LB
    eye = jnp.eye(S5_LB, dtype=F32)

    def lane_block(m, spec, rows, cols):
        return jnp.einsum(spec, m.reshape((nq, S5_LB) + m.shape[1:]), eye).reshape(nq, rows, cols)

    sw = S5_LB * p
    bre = lane_block(bp.real, "qgpsc,hg->qshcgp", L * LANES, sw)
    bim = lane_block(bp.imag, "qgpsc,hg->qshcgp", L * LANES, sw)
    mst = lane_block(kd, "qgdoc,hg->qdhcgo", L * LANES, LANES)
    cre = lane_block(cp.real, "qgopt,hg->qthpgo", L * sw, LANES).reshape(nq, L, sw, LANES)
    cim = lane_block(-cp.imag, "qgopt,hg->qthpgo", L * sw, LANES).reshape(nq, L, sw, LANES)
    return dict(
        bre=bre.astype(BF16), bim=bim.astype(BF16), mst=mst.astype(BF16),
        cc=jnp.concatenate([cre, cim], axis=2).astype(BF16),
        lre=pw[:, :, L].real.reshape(1, g * p), lim=pw[:, :, L].imag.reshape(1, g * p))


def _ab_out_kernel(x_ref, ys_ref, u_ref, at_ref, d_ref, gw_ref, gb_ref, ws_ref, wa_ref, g1_ref, *route_refs):
    o_ref = route_refs[5]
    y = ys_ref[0] + d_ref[...] * u_ref[0]
    y = jax.nn.gelu(y)
    gate = jax.nn.sigmoid(_dot(y.astype(BF16), gw_ref[...]) + gb_ref[...])
    out = _dot((y * gate).astype(BF16), ws_ref[...]) + _dot(at_ref[0], wa_ref[...])
    xn = x_ref[0] + g1_ref[0] * out
    o_ref[0] = xn
    _route(xn, *route_refs[:5], *route_refs[6:])


def _ab_out_call(x, ys, u, attn, wp, g1, rt, ts):
    b, s, d = x.shape
    r_ins, r_specs, r_ospecs, r_oshape, r_scratch = _route_plumbing(rt, b, s, d, ts)
    ins = [x, ys, u, attn, wp["s5_d"], wp["glu_w"], wp["glu_b"], wp["w_out_s5"], wp["w_out_at"], g1] + r_ins
    specs = [_tok(d, ts), _tok(S5_WIDTH, ts), _tok(S5_WIDTH, ts), _tok(MLA_HEADS * MLA_V, ts),
             _full(wp["s5_d"]), _full(wp["glu_w"]), _full(wp["glu_b"]), _full(wp["w_out_s5"]),
             _full(wp["w_out_at"]), _modspec(g1, ts)] + r_specs
    return pl.pallas_call(
        _ab_out_kernel,
        grid=(b, s // ts),
        in_specs=specs,
        out_specs=[_tok(d, ts)] + r_ospecs,
        out_shape=[jax.ShapeDtypeStruct((b, s, d), F32)] + r_oshape,
        scratch_shapes=r_scratch,
        compiler_params=_cparams("arbitrary", "arbitrary"),
        name="ab_out",
    )(*ins)


def _c_in_kernel(x_ref, gain_ref, sc_ref, sh_ref, w_ref, qkv_ref, z_ref, ab_ref):
    h = _normmod(x_ref[0], gain_ref[...], sc_ref[0], sh_ref[0]).astype(BF16)
    proj = _dot(h, w_ref[...])
    qkv_ref[0] = proj[:, :GDN_QKV].astype(BF16)
    z_ref[0] = proj[:, GDN_QKV:GDN_QKV + GDN_VW].astype(BF16)
    ab_ref[0] = proj[:, GDN_QKV + GDN_VW:]


def _c_in_call(x, gain, sc, sh, w, ts):
    b, s, d = x.shape
    return pl.pallas_call(
        _c_in_kernel,
        grid=(b, s // ts),
        in_specs=[_tok(d, ts), _full(gain), _modspec(sc, ts), _modspec(sh, ts), _full(w)],
        out_specs=[_tok(GDN_QKV, ts), _tok(GDN_VW, ts), _tok(LANES, ts)],
        out_shape=[jax.ShapeDtypeStruct((b, s, GDN_QKV), BF16),
                   jax.ShapeDtypeStruct((b, s, GDN_VW), BF16),
                   jax.ShapeDtypeStruct((b, s, LANES), F32)],
        compiler_params=_cparams("parallel", "parallel"),
        name="c_in",
    )(x, gain, sc, sh, w)


def _gdn_prep_kernel(qkv_ref, halo_ref, st_ref, cw_ref, ab_ref, alog_ref, dtb_ref,
                     wq_ref, u_ref, kgt_ref, pm_ref, ee_ref, *, L, nc):
    j = pl.program_id(1)
    rows = nc * L
    prev = jnp.where(j == 0, st_ref[0].astype(F32), halo_ref[0].astype(F32))
    xx = jnp.concatenate([prev, qkv_ref[0].astype(F32)], axis=0)
    cw = cw_ref[...]
    y = xx[8:8 + rows] * cw[CONV_W - 1:CONV_W]
    for t in range(1, CONV_W):
        y += xx[8 - t:8 - t + rows] * cw[CONV_W - 1 - t:CONV_W - t]
    y = y * jax.nn.sigmoid(y)

    ab = ab_ref[0]
    g_all = -jnp.exp(alog_ref[...]) * jax.nn.softplus(ab + dtb_ref[...])
    beta_all = jax.nn.sigmoid(ab)
    row = lax.broadcasted_iota(jnp.int32, (L, L), 0)
    col = lax.broadcasted_iota(jnp.int32, (L, L), 1)
    incl = row >= col
    strict = row > col
    brow = lax.broadcasted_iota(jnp.int32, (rows, rows), 0)
    bcol = lax.broadcasted_iota(jnp.int32, (rows, rows), 1)
    incl_blk = jnp.logical_and(brow >= bcol, brow // L == bcol // L)
    tri_bf = incl_blk.astype(BF16)
    gam_all = sum(_dot(tri_bf, piece) for piece in _split3(g_all))
    eye_l = (lax.broadcasted_iota(jnp.int32, (LANES, LANES), 0)
             == lax.broadcasted_iota(jnp.int32, (LANES, LANES), 1))
    gam_t = sum(_dot_nt(eye_l.astype(BF16), piece) for piece in _split3(gam_all))
    for ci in range(nc):
        ee_ref[0, ci] = jnp.exp(gam_all[(ci + 1) * L - 1:(ci + 1) * L])
    eye64 = (row == col).astype(F32)
    eye_bf = eye_l.astype(BF16)

    ch = [(ci, h) for ci in range(nc) for h in range(GDN_HEADS)]
    cs = range(len(ch))
    koff, voff = GDN_HEADS * GDN_DK, 2 * GDN_HEADS * GDN_DK
    qh = [y[ci * L:(ci + 1) * L, GDN_DK * h:GDN_DK * (h + 1)] for ci, h in ch]
    kh = [y[ci * L:(ci + 1) * L, koff + GDN_DK * h:koff + GDN_DK * (h + 1)] for ci, h in ch]
    vh = [y[ci * L:(ci + 1) * L, voff + GDN_DV * h:voff + GDN_DV * (h + 1)] for ci, h in ch]
    qh = [q * lax.rsqrt(jnp.sum(q * q, axis=-1, keepdims=True) + EPS) * (GDN_DK ** -0.5) for q in qh]
    kh = [k * lax.rsqrt(jnp.sum(k * k, axis=-1, keepdims=True) + EPS) for k in kh]
    gam_c = [gam_all[ci * L:(ci + 1) * L, h:h + 1] for ci, h in ch]
    beta_c = [beta_all[ci * L:(ci + 1) * L, GDN_HEADS + h:GDN_HEADS + h + 1] for ci, h in ch]
    dec = [jnp.exp(jnp.minimum(gam_c[c] - gam_t[h:h + 1, ci * L:(ci + 1) * L], 0.0))
           for c, (ci, h) in enumerate(ch)]
    kb = [k.astype(BF16) for k in kh]
    kk = [_dot_nt(kb[c], kb[c]) for c in cs]
    qk = [_dot_nt(qh[c].astype(BF16), kb[c]) for c in cs]
    a = [jnp.where(strict, beta_c[c] * kk[c] * dec[c], 0.0) for c in cs]
    for c, (ci, h) in enumerate(ch):
        pm_ref[0, ci, h] = jnp.where(incl, qk[c] * dec[c], 0.0).astype(BF16)
    tinv = [eye64 - a[c] for c in cs]
    pw = [_split(a[c]) for c in cs]
    for _ in range(max(L.bit_length() - 2, 0)):
        pw = [_split(_dot3(pw[c], pw[c])) for c in cs]
        tinv = [tinv[c] + _dot3(_split(tinv[c]), pw[c]) for c in cs]
    eg = [jnp.exp(gam_c[c]) for c in cs]
    rhs = [jnp.concatenate([(beta_c[c] * eg[c]) * kh[c], beta_c[c] * vh[c]], axis=1).astype(BF16) for c in cs]
    wu = [_dot(tinv[c].astype(BF16), rhs[c]) for c in cs]
    for c, (ci, h) in enumerate(ch):
        wq_ref[0, ci, h] = jnp.concatenate([wu[c][:, :GDN_DK], qh[c] * eg[c]], axis=0).astype(BF16)
        u_ref[0, h, ci * L:(ci + 1) * L, :] = wu[c][:, GDN_DK:].astype(BF16)
    kg = [(kh[c] * jnp.exp(gam_all[(ci + 1) * L - 1:(ci + 1) * L, h:h + 1] - gam_c[c])).astype(BF16)
          for c, (ci, h) in enumerate(ch)]
    kgt = [_dot_nt(eye_bf, kg[c]) for c in cs]
    for c, (ci, h) in enumerate(ch):
        kgt_ref[0, ci, h] = kgt[c].astype(BF16)


def _split(a):
    hi = a.astype(BF16)
    return hi, (a - hi.astype(F32)).astype(BF16)


def _split3(a):
    hi = a.astype(BF16)
    r1 = a - hi.astype(F32)
    mid = r1.astype(BF16)
    return hi, mid, (r1 - mid.astype(F32)).astype(BF16)


def _dot3(a, b):
    return _dot(a[0], b[0]) + (_dot(a[0], b[1]) + _dot(a[1], b[0]))


GDN_PREP_CHUNKS = 2


def _gdn_prep_call(qkv, st8, conv_w, ab, alog, dtb, L):
    b, s, _ = qkv.shape
    nch = s // L
    nc = GDN_PREP_CHUNKS if nch % GDN_PREP_CHUNKS == 0 else 1
    rows = nc * L
    return pl.pallas_call(
        functools.partial(_gdn_prep_kernel, L=L, nc=nc),
        grid=(b, nch // nc),
        in_specs=[_tok(GDN_QKV, rows),
                  pl.BlockSpec((1, 8, GDN_QKV), lambda bb, j: (bb, jnp.maximum(j * (rows // 8) - 1, 0), 0)),
                  pl.BlockSpec((1, 8, GDN_QKV), lambda bb, j: (bb, 0, 0)),
                  _full(conv_w), _tok(LANES, rows), _full(alog), _full(dtb)],
        out_specs=[pl.BlockSpec((1, nc, GDN_HEADS, 2 * L, LANES), lambda bb, j: (bb, j, 0, 0, 0)),
                   pl.BlockSpec((1, GDN_HEADS, rows, LANES), lambda bb, j: (bb, 0, j, 0)),
                   pl.BlockSpec((1, nc, GDN_HEADS, GDN_DK, L), lambda bb, j: (bb, j, 0, 0, 0)),
                   pl.BlockSpec((1, nc, GDN_HEADS, L, L), lambda bb, j: (bb, j, 0, 0, 0)),
                   pl.BlockSpec((1, nc, 1, LANES), lambda bb, j: (bb, j, 0, 0))],
        out_shape=[jax.ShapeDtypeStruct((b, nch, GDN_HEADS, 2 * L, LANES), BF16),
                   jax.ShapeDtypeStruct((b, GDN_HEADS, s, LANES), BF16),
                   jax.ShapeDtypeStruct((b, nch, GDN_HEADS, GDN_DK, L), BF16),
                   jax.ShapeDtypeStruct((b, nch, GDN_HEADS, L, L), BF16),
                   jax.ShapeDtypeStruct((b, nch, 1, LANES), F32)],
        compiler_params=_cparams("parallel", "parallel"),
        name="gdn_prep",
    )(qkv, qkv, st8, conv_w, ab, alog, dtb)


GDN_BATCH_GROUP = 2


GDN_SEQ_CHUNKS = 8


def _gdn_seq_kernel(wq_ref, u_ref, kgt_ref, pm_ref, ee_ref, s0_ref, o_ref, sf_ref, s_scr, *, L, nc):
    c = pl.program_id(1)

    @pl.when(c == 0)
    def _():
        s_scr[...] = s0_ref[...]

    ch = [(bb, hh) for bb in range(GDN_BATCH_GROUP) for hh in range(GDN_HEADS)]
    st = [s_scr[bb, hh] for bb, hh in ch]
    for ci in range(nc):
        rows = slice(ci * L, (ci + 1) * L)
        sp = [_split(s) for s in st]
        r = [_dot(wq_ref[bb, ci, hh], sp[i][0]) + _dot(wq_ref[bb, ci, hh], sp[i][1]) for i, (bb, hh) in enumerate(ch)]
        v_new = [(u_ref[bb, hh, rows, :].astype(F32) - r[i][:L]).astype(BF16) for i, (bb, hh) in enumerate(ch)]
        o = [r[i][L:] + _dot(pm_ref[bb, ci, hh], v_new[i]) for i, (bb, hh) in enumerate(ch)]
        for i, (bb, hh) in enumerate(ch):
            o_ref[bb, rows, GDN_DV * hh:GDN_DV * (hh + 1)] = o[i].astype(o_ref.dtype)
        st = [ee_ref[bb, ci][:, hh:hh + 1] * st[i] + _dot(kgt_ref[bb, ci, hh], v_new[i])
              for i, (bb, hh) in enumerate(ch)]
    for i, (bb, hh) in enumerate(ch):
        s_scr[bb, hh] = st[i]

    @pl.when(c == pl.num_programs(1) - 1)
    def _():
        sf_ref[...] = s_scr[...]


def _gdn_seq_call(wq, um, kgt, pm, ee, s0, L):
    b, h, s, _ = um.shape
    nch = s // L
    bg = GDN_BATCH_GROUP
    nc = GDN_SEQ_CHUNKS if nch % GDN_SEQ_CHUNKS == 0 else 1
    sspec = pl.BlockSpec((bg, h, GDN_DK, GDN_DV), lambda g, c: (g, 0, 0, 0))
    return pl.pallas_call(
        functools.partial(_gdn_seq_kernel, L=L, nc=nc),
        grid=(b // bg, nch // nc),
        in_specs=[pl.BlockSpec((bg, nc, h, 2 * L, LANES), lambda g, c: (g, c, 0, 0, 0)),
                  pl.BlockSpec((bg, h, nc * L, LANES), lambda g, c: (g, 0, c, 0)),
                  pl.BlockSpec((bg, nc, h, GDN_DK, L), lambda g, c: (g, c, 0, 0, 0)),
                  pl.BlockSpec((bg, nc, h, L, L), lambda g, c: (g, c, 0, 0, 0)),
                  pl.BlockSpec((bg, nc, 1, LANES), lambda g, c: (g, c, 0, 0)),
                  sspec],
        out_specs=[pl.BlockSpec((bg, nc * L, GDN_VW), lambda g, c: (g, c, 0)), sspec],
        out_shape=[jax.ShapeDtypeStruct((b, s, GDN_VW), BF16),
                   jax.ShapeDtypeStruct((b, h, GDN_DK, GDN_DV), F32)],
        scratch_shapes=[pltpu.VMEM((bg, h, GDN_DK, GDN_DV), F32)],
        compiler_params=_cparams("parallel", "arbitrary"),
        name="gdn_seq",
    )(wq, um, kgt, pm, ee, s0)


def _c_out_kernel(x_ref, o_ref, z_ref, gn_ref, w_ref, g1_ref, *route_refs):
    out_ref = route_refs[5]
    o = o_ref[0].astype(F32)
    z = z_ref[0].astype(F32)
    gn = gn_ref[...]
    parts = []
    for hh in range(GDN_HEADS):
        oh = o[:, GDN_DV * hh:GDN_DV * (hh + 1)]
        zh = z[:, GDN_DV * hh:GDN_DV * (hh + 1)]
        parts.append((_rms(oh, gn) * (zh * jax.nn.sigmoid(zh))).astype(BF16))
    out = _dot(jnp.concatenate(parts, axis=1), w_ref[...])
    xn = x_ref[0] + g1_ref[0] * out
    out_ref[0] = xn
    _route(xn, *route_refs[:5], *route_refs[6:])


def _c_out_call(x, o, z, gn, w, g1, rt, ts):
    b, s, d = x.shape
    r_ins, r_specs, r_ospecs, r_oshape, r_scratch = _route_plumbing(rt, b, s, d, ts)
    return pl.pallas_call(
        _c_out_kernel,
        grid=(b, s // ts),
        in_specs=[_tok(d, ts), _tok(GDN_VW, ts), _tok(GDN_VW, ts), _full(gn), _full(w), _modspec(g1, ts)] + r_specs,
        out_specs=[_tok(d, ts)] + r_ospecs,
        out_shape=[jax.ShapeDtypeStruct((b, s, d), F32)] + r_oshape,
        scratch_shapes=r_scratch,
        compiler_params=_cparams("arbitrary", "arbitrary"),
        name="c_out",
    )(x, o, z, gn, w, g1, *r_ins)


def _pack_halves(v):
    n = v.shape[1] // 2
    bits = pltpu.bitcast(v.astype(BF16).astype(F32), jnp.uint32)
    return (bits[:, :n] >> 16) | (bits[:, n:] & jnp.uint32(0xFFFF0000))


def _unpack_halves(w):
    return pltpu.bitcast(w << 16, F32), pltpu.bitcast(w & jnp.uint32(0xFFFF0000), F32)


def _route(xn, gain_ref, sc_ref, sh_ref, wr_ref, br_ref, h_ref, re_ref, rg_ref, cnt_ref, carry):
    @pl.when(jnp.logical_and(pl.program_id(0) == 0, pl.program_id(1) == 0))
    def _():
        carry[...] = jnp.zeros_like(carry)

    h = _normmod(xn, gain_ref[...], sc_ref[0], sh_ref[0])
    h_ref[0] = _pack_halves(h)
    logits = _dot3(_split(h), (wr_ref[0], wr_ref[1])) + br_ref[...]
    lane = lax.broadcasted_iota(jnp.int32, logits.shape, 1).astype(F32)
    neg = jnp.float32(-jnp.inf)
    big = jnp.float32(1e9)
    gl = jnp.where(lane < N_GROUPS, logits, neg)
    gm = jnp.max(gl, axis=-1, keepdims=True)
    g_val = 1.0 / jnp.sum(jnp.exp(gl - gm), axis=-1, keepdims=True)
    g_idx = jnp.min(jnp.where(gl == gm, lane, big), axis=-1, keepdims=True)
    lo = N_GROUPS + EXPERTS_PER_GROUP * g_idx
    el = jnp.where(jnp.logical_and(lane >= lo, lane < lo + EXPERTS_PER_GROUP), logits, neg)
    em = jnp.max(el, axis=-1, keepdims=True)
    i1 = jnp.min(jnp.where(el == em, lane, big), axis=-1, keepdims=True)
    el2 = jnp.where(lane == i1, neg, el)
    em2 = jnp.max(el2, axis=-1, keepdims=True)
    i2 = jnp.min(jnp.where(el2 == em2, lane, big), axis=-1, keepdims=True)
    es = jnp.sum(jnp.exp(el - em), axis=-1, keepdims=True)
    p1 = 1.0 / es
    p2 = jnp.exp(em2 - em) / es
    den = p1 + p2
    rg_ref[0] = jnp.where(lane == 0, g_val * p1 / den, jnp.where(lane == 1, g_val * p2 / den, 0.0))
    oh1 = lane == i1
    oh2 = lane == i2
    both = jnp.where(jnp.logical_or(oh1, oh2), 1.0, 0.0)
    ts = both.shape[0]
    tri = (lax.broadcasted_iota(jnp.int32, (ts, ts), 0) > lax.broadcasted_iota(jnp.int32, (ts, ts), 1))
    pre = _dot(jnp.where(tri, 1.0, 0.0).astype(BF16), both.astype(BF16)) + carry[...]
    r1 = jnp.sum(jnp.where(oh1, pre, 0.0), axis=-1, keepdims=True)
    r2 = jnp.sum(jnp.where(oh2, pre, 0.0), axis=-1, keepdims=True)
    re_ref[0] = jnp.where(lane == 0, i1 - N_GROUPS, jnp.where(lane == 1, i2 - N_GROUPS, jnp.where(
        lane == 2, r1, jnp.where(lane == 3, r2, 0.0)))).astype(jnp.int32)
    carry[...] += jnp.sum(both, axis=0, keepdims=True)
    cnt_ref[...] = carry[...]


def _route_plumbing(rt, b, s, d, ts):
    gain, sc, sh, wr, br = rt
    ins = [gain, sc, sh, wr, br]
    in_specs = [_full(gain), _modspec(sc, ts), _modspec(sh, ts), _full(wr), _full(br)]
    out_specs = [_tok(d // 2, ts), _tok(LANES, ts), _tok(LANES, ts), pl.BlockSpec((1, LANES), lambda bb, j: (0, 0))]
    out_shape = [jax.ShapeDtypeStruct((b, s, d // 2), jnp.uint32),
                 jax.ShapeDtypeStruct((b, s, LANES), jnp.int32),
                 jax.ShapeDtypeStruct((b, s, LANES), F32),
                 jax.ShapeDtypeStruct((1, LANES), F32)]
    return ins, in_specs, out_specs, out_shape, [pltpu.VMEM((1, LANES), F32)]


def _row_copy(src, dst, src_row, dst_row, sem):
    return pltpu.make_async_copy(src.at[pl.ds(src_row, 1)], dst.at[pl.ds(dst_row, 1)], sem)


ROUTE_W = 2 * TOP_K
DMA_UNROLL = 8


def _slot(ps_ref, rt_ref, r, k):
    return ps_ref[rt_ref[0, 0, ROUTE_W * r + k]] + rt_ref[0, 0, ROUTE_W * r + TOP_K + k]


def _dispatch_kernel(ps_ref, rt_ref, h_ref, xin_hbm, xb_hbm, sem, *, tb):
    del xin_hbm

    def issue(r, c):
        for k in range(TOP_K):
            _row_copy(h_ref, xb_hbm, r, _slot(ps_ref, rt_ref, r, k), sem).start()
        return c
    lax.fori_loop(0, tb, issue, 0, unroll=DMA_UNROLL)
    for k in range(TOP_K):
        pltpu.make_async_copy(h_ref, xb_hbm.at[pl.ds(0, tb)], sem).wait()


def _dispatch_call(pad_start, route, h2, xb0, tb):
    n, dh = h2.shape
    nb = n // tb
    grid_spec = pltpu.PrefetchScalarGridSpec(
        num_scalar_prefetch=1,
        grid=(nb,),
        in_specs=[pl.BlockSpec((1, 1, ROUTE_W * tb), lambda j, ps: (j, 0, 0), memory_space=pltpu.SMEM),
                  pl.BlockSpec((tb, dh), lambda j, ps: (j, 0)),
                  pl.BlockSpec(memory_space=pl.ANY)],
        out_specs=pl.BlockSpec(memory_space=pl.ANY),
        scratch_shapes=[pltpu.SemaphoreType.DMA(())],
    )
    return pl.pallas_call(
        functools.partial(_dispatch_kernel, tb=tb),
        grid_spec=grid_spec,
        out_shape=jax.ShapeDtypeStruct(xb0.shape, xb0.dtype),
        input_output_aliases={3: 0},
        compiler_params=_cparams("arbitrary"),
        name="moe_dispatch",
    )(pad_start, route, h2, xb0)


def _ffn_kernel(be_ref, nu_ref, x_ref, w1_ref, w3_ref, w2_ref, y_ref, w1b, w3b, w2b):
    i = pl.program_id(0)
    half = w1b.shape[0] // 2

    @pl.when(jnp.logical_and(i < nu_ref[0], jnp.logical_or(i == 0, be_ref[i] != be_ref[jnp.maximum(i - 1, 0)])))
    def _():
        w1b[...] = w1_ref[0, 0].astype(BF16)
        w3b[...] = w3_ref[0, 0].astype(BF16)
        w2b[...] = w2_ref[0, 0].astype(BF16)

    @pl.when(i < nu_ref[0])
    def _():
        lo, hi = _unpack_halves(x_ref[...])
        lo = lo.astype(BF16)
        hi = hi.astype(BF16)
        a = _dot(lo, w1b[:half]) + _dot(hi, w1b[half:])
        g = _dot(lo, w3b[:half]) + _dot(hi, w3b[half:])
        mid = (a * jax.nn.sigmoid(a) * g).astype(BF16)
        y_ref[...] = _pack_halves(_dot(mid, w2b[...]))

    @pl.when(i >= nu_ref[0])
    def _():
        y_ref[...] = jnp.zeros_like(y_ref)


def _ffn_call(block_e, n_used, xb, w1, w3, w2, layer):
    n_blocks = block_e.shape[0]
    dh = xb.shape[1]
    d = 2 * dh
    blk = xb.shape[0] // n_blocks
    grid_spec = pltpu.PrefetchScalarGridSpec(
        num_scalar_prefetch=2,
        grid=(n_blocks,),
        in_specs=[pl.BlockSpec((blk, dh), lambda i, be, nu: (i, 0)),
                  pl.BlockSpec((1, 1, d, D_EXPERT), lambda i, be, nu: (layer, be[i], 0, 0)),
                  pl.BlockSpec((1, 1, d, D_EXPERT), lambda i, be, nu: (layer, be[i], 0, 0)),
                  pl.BlockSpec((1, 1, D_EXPERT, d), lambda i, be, nu: (layer, be[i], 0, 0))],
        out_specs=pl.BlockSpec((blk, dh), lambda i, be, nu: (i, 0)),
        scratch_shapes=[pltpu.VMEM((d, D_EXPERT), BF16), pltpu.VMEM((d, D_EXPERT), BF16),
                        pltpu.VMEM((D_EXPERT, d), BF16)],
    )
    return pl.pallas_call(
        _ffn_kernel,
        grid_spec=grid_spec,
        out_shape=jax.ShapeDtypeStruct(xb.shape, jnp.uint32),
        compiler_params=_cparams("arbitrary"),
        name="moe_ffn",
    )(block_e, n_used, xb, w1, w3, w2)


def _combine_kernel(ps_ref, rcur_ref, rnxt_ref, yb_hbm, x_ref, rg_ref, g2_ref, fin_ref, o_ref, b0, b1, sem,
                    *, tb, final):
    j = pl.program_id(0)
    nb = pl.num_programs(0)
    slot = j % 2

    def fetch(rref, sl):
        def issue(r, c):
            _row_copy(yb_hbm, b0.at[sl], _slot(ps_ref, rref, r, 0), r, sem.at[sl]).start()
            _row_copy(yb_hbm, b1.at[sl], _slot(ps_ref, rref, r, 1), r, sem.at[sl]).start()
            return c
        lax.fori_loop(0, tb, issue, 0, unroll=DMA_UNROLL)

    @pl.when(j == 0)
    def _():
        fetch(rcur_ref, 0)

    @pl.when(j + 1 < nb)
    def _():
        fetch(rnxt_ref, 1 - slot)

    pltpu.make_async_copy(yb_hbm.at[pl.ds(0, tb)], b0.at[slot], sem.at[slot]).wait()
    pltpu.make_async_copy(yb_hbm.at[pl.ds(0, tb)], b1.at[slot], sem.at[slot]).wait()
    rg = rg_ref[...]
    lo0, hi0 = _unpack_halves(b0[slot])
    lo1, hi1 = _unpack_halves(b1[slot])
    g0 = rg[:, 0:1]
    g1 = rg[:, 1:2]
    y = jnp.concatenate([g0 * lo0 + g1 * lo1, g0 * hi0 + g1 * hi1], axis=1)
    xn = x_ref[...] + g2_ref[0] * y
    if final:
        xn = _rms(xn, fin_ref[...])
    o_ref[...] = xn


def _combine_call(pad_start, route, yb, x, rg, g2, fin, tb, final):
    b, s, d = x.shape
    n = b * s
    nb = n // tb
    npb = s // tb
    dh = yb.shape[1]
    if g2.shape[1] == 1:
        gspec = pl.BlockSpec((1, 1, d), lambda j, ps: (j // npb, 0, 0))
    else:
        gspec = pl.BlockSpec((1, tb, d), lambda j, ps: (j // npb, j % npb, 0))
    grid_spec = pltpu.PrefetchScalarGridSpec(
        num_scalar_prefetch=1,
        grid=(nb,),
        in_specs=[pl.BlockSpec((1, 1, ROUTE_W * tb), lambda j, ps: (j, 0, 0), memory_space=pltpu.SMEM),
                  pl.BlockSpec((1, 1, ROUTE_W * tb), lambda j, ps: (jnp.minimum(j + 1, nb - 1), 0, 0),
                               memory_space=pltpu.SMEM),
                  pl.BlockSpec(memory_space=pl.ANY),
                  pl.BlockSpec((tb, d), lambda j, ps: (j, 0)), pl.BlockSpec((tb, LANES), lambda j, ps: (j, 0)),
                  gspec, pl.BlockSpec(fin.shape, lambda j, ps: (0, 0))],
        out_specs=pl.BlockSpec((tb, d), lambda j, ps: (j, 0)),
        scratch_shapes=[pltpu.VMEM((2, tb, dh), jnp.uint32), pltpu.VMEM((2, tb, dh), jnp.uint32),
                        pltpu.SemaphoreType.DMA((2,))],
    )
    out = pl.pallas_call(
        functools.partial(_combine_kernel, tb=tb, final=final),
        grid_spec=grid_spec,
        out_shape=jax.ShapeDtypeStruct((n, d), F32),
        compiler_params=_cparams("arbitrary"),
        name="moe_combine",
    )(pad_start, route, route, yb, x.reshape(n, d), rg.reshape(n, LANES), g2, fin)
    return out.reshape(b, s, d)


def _moe(x, routed, g2, mp, fin, ts, final):
    b, s, d = x.shape
    n = b * s
    h2, r_e, r_g, cnt = routed
    counts = cnt[0, N_GROUPS:N_GROUPS + N_EXPERTS].astype(jnp.int32)
    blk = MOE_BLOCK if n * TOP_K >= N_EXPERTS * MOE_BLOCK else MOE_BLOCK_SMALL
    padded = (counts + blk - 1) // blk * blk
    pad_end = jnp.cumsum(padded)
    pad_start = (pad_end - padded).astype(jnp.int32)
    n_blocks = (n * TOP_K + N_EXPERTS * (blk - 1) + blk - 1) // blk
    blk0 = jnp.arange(n_blocks, dtype=jnp.int32) * blk
    block_e = jnp.minimum(jnp.sum((pad_end[None, :] <= blk0[:, None]).astype(jnp.int32), axis=1), N_EXPERTS - 1)
    n_used = (pad_end[-1:] // blk).astype(jnp.int32)
    route = r_e[:, :, :ROUTE_W].reshape(n // ts, 1, ROUTE_W * ts)
    xb0 = jnp.zeros((n_blocks * blk, d // 2), jnp.uint32)
    xb = _dispatch_call(pad_start, route, h2.reshape(n, d // 2), xb0, ts)
    yb = _ffn_call(block_e, n_used, xb, mp["w1"], mp["w3"], mp["w2"], mp["layer"])
    return _combine_call(pad_start, route, yb, x, r_g, g2, fin, ts, final)


def _prep_ab(l, p):
    w_in = p["w_in_ab"][l]
    o2 = S5_WIDTH + MLA_Q_RANK + MLA_KV_RANK
    half = MLA_ROPE // 2
    d = w_in.shape[0]
    kr = w_in[:, o2:]
    kr_rot = jnp.concatenate([-kr[:, half:], kr[:, :half]], axis=1)
    z64 = jnp.zeros((d, MLA_NOPE), F32)
    z32 = jnp.zeros((d, LANES - MLA_NOPE - MLA_ROPE), F32)
    w_ext = jnp.concatenate([w_in[:, :o2], z64, kr, z32, z64, kr_rot, z32], axis=1)
    qu = p["mla_q_up"][l].reshape(MLA_Q_RANK, MLA_HEADS, MLA_NOPE + MLA_ROPE)
    qn, qr = qu[:, :, :MLA_NOPE], qu[:, :, MLA_NOPE:]
    zq = jnp.zeros((MLA_Q_RANK, MLA_HEADS, LANES - MLA_NOPE - MLA_ROPE), F32)
    wqa = jnp.concatenate([qn, qr, zq], axis=2).reshape(MLA_Q_RANK, MLA_HEADS * LANES)
    wqb = jnp.concatenate([jnp.zeros_like(qn), -qr[:, :, half:], qr[:, :, :half], zq], axis=2)
    wqb = wqb.reshape(MLA_Q_RANK, MLA_HEADS * LANES)
    kvu = p["mla_kv_up"][l].reshape(MLA_KV_RANK, MLA_HEADS, MLA_NOPE + MLA_V)
    wk = jnp.concatenate([kvu[:, :, :MLA_NOPE], jnp.zeros((MLA_KV_RANK, MLA_HEADS, LANES - MLA_NOPE), F32)], axis=2)
    wo = p["w_out_ab"][l]
    out = dict(
        w_in=w_ext.astype(BF16), q_norm=p["mla_q_norm"][l].reshape(1, -1), kv_norm=p["mla_kv_norm"][l].reshape(1, -1),
        wqa=wqa.astype(BF16), wqb=wqb.astype(BF16),
        wk=wk.reshape(MLA_KV_RANK, MLA_HEADS * LANES).astype(BF16),
        wv=jnp.concatenate([kvu[:, :, MLA_NOPE:], jnp.zeros((MLA_KV_RANK, MLA_HEADS, LANES - MLA_V), F32)],
                           axis=2).reshape(MLA_KV_RANK, MLA_HEADS * LANES).astype(BF16),
        s5_d=p["s5_d"][l].reshape(1, -1), glu_w=p["s5_glu_w"][l].astype(BF16), glu_b=p["s5_glu_b"][l].reshape(1, -1),
        w_out_s5=wo[:S5_WIDTH].astype(BF16),
        w_out_at=wo[S5_WIDTH:].astype(BF16))
    out.update(_s5_mats(p["s5_a_re"][l], p["s5_a_im"][l], p["s5_b_re"][l], p["s5_b_im"][l],
                        p["s5_c_re"][l], p["s5_c_im"][l], p["s5_log_dt"][l]))
    return out


def _prep_c(l, p):
    w = p["w_in_c"][l]
    pad = jnp.zeros((w.shape[0], LANES - 2 * GDN_HEADS), F32)

    def row(v):
        return jnp.concatenate([v, jnp.zeros((LANES - GDN_HEADS,), F32)]).reshape(1, LANES)

    return dict(w_in=jnp.concatenate([w, pad], axis=1).astype(BF16), conv_w=p["conv_w"][l],
                alog=row(p["gdn_a_log"][l]), dtb=row(p["gdn_dt_bias"][l]),
                gn=p["gdn_norm"][l].reshape(1, -1), w_out=p["w_out_c"][l].astype(BF16))


def _prep_moe(layer, p):
    d = p["moe_w_group"].shape[1]
    wr = jnp.concatenate([p["moe_w_group"][layer], p["moe_w_expert"][layer],
                          jnp.zeros((d, LANES - N_GROUPS - N_EXPERTS), F32)], axis=1)
    br = jnp.concatenate([p["moe_b_group"][layer], p["moe_b_expert"][layer],
                          jnp.zeros((LANES - N_GROUPS - N_EXPERTS,), F32)]).reshape(1, LANES)
    wr_hi = wr.astype(BF16)
    wr_lo = (wr - wr_hi.astype(F32)).astype(BF16)
    return dict(wr=jnp.stack([wr_hi, wr_lo]), br=br, w1=p["moe_w1"], w3=p["moe_w3"], w2=p["moe_w2"], layer=layer)


def _rope_tables(pos):
    half = MLA_ROPE // 2
    inv = ROPE_THETA ** (-jnp.arange(half, dtype=F32) / half)
    ang = pos.astype(F32)[:, None] * inv[None, :]
    cos, sin = jnp.cos(ang), jnp.sin(ang)
    n = pos.shape[0]
    z64 = jnp.zeros((n, MLA_NOPE), F32)
    z32 = jnp.zeros((n, LANES - MLA_NOPE - MLA_ROPE), F32)
    scale = (MLA_NOPE + MLA_ROPE) ** -0.5 * math.log2(math.e)
    ck = jnp.concatenate([z64, cos, cos, z32], axis=1)
    sk = jnp.concatenate([z64, sin, sin, z32], axis=1)
    cq = jnp.concatenate([jnp.ones_like(z64), cos, cos, z32], axis=1) * scale
    return cq, sk * scale, ck, sk


def _trunk(x, mods, caches, wab, wc, wmoe, p, flat):
    b, s, d = x.shape
    lat_c, rope_c, s5re_c, s5im_c, conv_c, gdn_c = caches
    past = 0 if lat_c is None else lat_c.shape[2]
    pos = past + jnp.arange(s)
    if flat:
        xt = x.reshape(1, b * s, d)
        ts = b * s
        pos_rows = jnp.tile(pos, b)

        def mrow(m):
            return jnp.repeat(m, s, axis=0).reshape(1, b * s, d)
    else:
        xt = x
        ts = TOKEN_BLOCK
        pos_rows = pos

        def mrow(m):
            return m.reshape(b, 1, d)
    bt, st = xt.shape[:2]
    outs = {}
    depth = p["norm_mix"].shape[0]
    for layer in range(depth):
        l = layer // 2
        sh1, sc1, g1, sh2, sc2, g2 = [mrow(m) for m in jnp.split(mods[layer], 6, axis=-1)]
        gain = p["norm_mix"][layer].reshape(1, d)
        rt = (p["norm_ffn"][layer].reshape(1, d), sc2, sh2, wmoe[layer]["wr"], wmoe[layer]["br"])
        if layer % 2 == 0:
            wp = wab[l]
            u, q, lat, kr = _ab_in_call(xt, gain, sc1, sh1, wp, _rope_tables(pos_rows), ts)
            lat_b = lat.reshape(b, s, MLA_KV_RANK)
            kr_b = kr.reshape(b, s, LANES)
            outs["lat"] = lat_b
            outs["krope"] = kr_b[:, :, MLA_NOPE:MLA_NOPE + MLA_ROPE]
            if lat_c is None:
                lat_all, kr_all, sk_valid, tkv = lat_b, kr_b, s, TOKEN_BLOCK
                tq, tk = min(ATTN_TQ, s), min(ATTN_TK, s)
            else:
                krc = jnp.pad(rope_c[l], ((0, 0), (0, 0), (MLA_NOPE, LANES - MLA_NOPE - MLA_ROPE)))
                sk_valid = past + s
                skp = (sk_valid + LANES - 1) // LANES * LANES
                lat_all = jnp.pad(jnp.concatenate([lat_c[l], lat_b], axis=1), ((0, 0), (0, skp - sk_valid), (0, 0)))
                kr_all = jnp.pad(jnp.concatenate([krc, kr_b], axis=1), ((0, 0), (0, skp - sk_valid), (0, 0)))
                tkv, tq, tk = skp, s, skp
            kk, vv = _kv_call(lat_all, kr_all, wp, tkv)
            qh = q.reshape(MLA_HEADS, b, s, LANES).transpose(1, 0, 2, 3) if flat else q
            attn = _attn_call(qh, kk, vv, tq, tk, past, sk_valid).reshape(bt, st, MLA_HEADS * MLA_V)
            nchk = s // S5_CHUNK
            r = b * nchk
            u2 = u.reshape(b * s, S5_WIDTH)
            tr = min(r, S5_ROW_BLOCK)
            wre, wim = _s5_w_call(u2, wp["bre"], wp["bim"], tr)
            ngp = S5_GROUPS * S5_STATE

            def to_pairs(v):
                return v.reshape(b, 1, ngp)

            if s5re_c is None:
                x0re = jnp.zeros((b, 1, ngp), F32)
                x0im = jnp.zeros((b, 1, ngp), F32)
            else:
                x0re, x0im = to_pairs(s5re_c[l]), to_pairs(s5im_c[l])
            nchp = (nchk + 7) // 8 * 8
            tc = min(nchp, S5_SCAN_BLOCK)
            wre3 = jnp.pad(wre.reshape(b, nchk, ngp), ((0, 0), (0, nchp - nchk), (0, 0)))
            wim3 = jnp.pad(wim.reshape(b, nchk, ngp), ((0, 0), (0, nchp - nchk), (0, 0)))
            xere, xeim = _s5_scan_call(wre3, wim3, wp["lre"], wp["lim"], x0re, x0im, tc)
            outs["s5re"] = xere[:, nchk - 1].reshape(b, S5_GROUPS, S5_STATE)
            outs["s5im"] = xeim[:, nchk - 1].reshape(b, S5_GROUPS, S5_STATE)
            xsre = jnp.concatenate([x0re, xere[:, :nchk - 1]], axis=1).reshape(r, ngp)
            xsim = jnp.concatenate([x0im, xeim[:, :nchk - 1]], axis=1).reshape(r, ngp)
            ys = _s5_y_call(u2, xsre, xsim, wp["mst"], wp["cc"], tr).reshape(bt, st, S5_WIDTH)
            xt, *routed = _ab_out_call(xt, ys, u, attn, wp, g1, rt, ts)
        else:
            wp = wc[l]
            qkv, z, ab = _c_in_call(xt, gain, sc1, sh1, wp["w_in"], ts)
            qkv_b = qkv.reshape(b, s, GDN_QKV)
            outs["conv"] = qkv_b[:, s - (CONV_W - 1):].astype(F32)
            if conv_c is None:
                st8 = jnp.zeros((b, 8, GDN_QKV), BF16)
                s0 = jnp.zeros((b, GDN_HEADS, GDN_DK, GDN_DV), F32)
            else:
                st8 = jnp.pad(conv_c[l], ((0, 0), (8 - (CONV_W - 1), 0), (0, 0))).astype(BF16)
                s0 = gdn_c[l]
            lg = min(s, CHUNK)
            wq, um, kgt, pm, ee = _gdn_prep_call(qkv_b, st8, wp["conv_w"], ab.reshape(b, s, LANES),
                                                 wp["alog"], wp["dtb"], lg)
            o, sfin = _gdn_seq_call(wq, um, kgt, pm, ee, s0, lg)
            outs["gdn"] = sfin
            xt, *routed = _c_out_call(xt, o.reshape(bt, st, GDN_VW), z, wp["gn"], wp["w_out"], g1, rt, ts)
        final = layer == depth - 1
        xt = _moe(xt, routed, g2, wmoe[layer], p["norm_final"].reshape(1, d), ts, final)
    y = xt.reshape(b, s, d)
    return (y, outs["lat"][None], outs["krope"][None], outs["s5re"][None], outs["s5im"][None],
            outs["conv"][None], outs["gdn"][None])


def kernel(x_prompt, x_sample, c_prompt, c_sample, cache_mla_latent, cache_mla_krope, state_s5_re, state_s5_im,
           state_conv, state_gdn, w_mod, b_mod, norm_mix, norm_ffn, norm_final, w_in_ab, s5_a_re, s5_a_im,
           s5_b_re, s5_b_im, s5_c_re, s5_c_im, s5_d, s5_log_dt, s5_glu_w, s5_glu_b, mla_q_norm, mla_q_up,
           mla_kv_norm, mla_kv_up, w_out_ab, w_in_c, conv_w, gdn_a_log, gdn_dt_bias, gdn_norm, w_out_c,
           moe_w_group, moe_b_group, moe_w_expert, moe_b_expert, moe_w1, moe_w3, moe_w2):
    p = dict(w_mod=w_mod, b_mod=b_mod, norm_mix=norm_mix, norm_ffn=norm_ffn, norm_final=norm_final,
             w_in_ab=w_in_ab, s5_a_re=s5_a_re, s5_a_im=s5_a_im, s5_b_re=s5_b_re, s5_b_im=s5_b_im,
             s5_c_re=s5_c_re, s5_c_im=s5_c_im, s5_d=s5_d, s5_log_dt=s5_log_dt, s5_glu_w=s5_glu_w,
             s5_glu_b=s5_glu_b, mla_q_norm=mla_q_norm, mla_q_up=mla_q_up, mla_kv_norm=mla_kv_norm,
             mla_kv_up=mla_kv_up, w_out_ab=w_out_ab, w_in_c=w_in_c, conv_w=conv_w, gdn_a_log=gdn_a_log,
             gdn_dt_bias=gdn_dt_bias, gdn_norm=gdn_norm, w_out_c=w_out_c, moe_w_group=moe_w_group,
             moe_b_group=moe_b_group, moe_w_expert=moe_w_expert, moe_b_expert=moe_b_expert,
             moe_w1=moe_w1, moe_w3=moe_w3, moe_w2=moe_w2)
    depth = norm_mix.shape[0]
    bp, bs = c_prompt.shape[0], c_sample.shape[0]
    nb = (bp + bs + 7) // 8 * 8
    c_all = jnp.pad(jnp.concatenate([c_prompt, c_sample], axis=0), ((0, nb - bp - bs), (0, 0)))
    mods = _mod_call(c_all, w_mod, b_mod)
    wab = [_prep_ab(l, p) for l in range((depth + 1) // 2)]
    wc = [_prep_c(l, p) for l in range(depth // 2)]
    wmoe = [_prep_moe(layer, p) for layer in range(depth)]
    none6 = (None,) * 6
    outp = _trunk(x_prompt, mods[:, :bp], none6, wab, wc, wmoe, p, flat=False)
    caches = (cache_mla_latent, cache_mla_krope, state_s5_re, state_s5_im, state_conv, state_gdn)
    outs = _trunk(x_sample, mods[:, bp:bp + bs], caches, wab, wc, wmoe, p, flat=True)
    return (outp[0], outs[0]) + tuple(outp[1:]) + tuple(outs[1:])
```

```python
import functools
import math

import jax
import jax.numpy as jnp
from jax import lax
from jax.experimental import pallas as pl
from jax.experimental.pallas import tpu as pltpu

F32 = jnp.float32
BF16 = jnp.bfloat16
EPS = 1e-6

CHUNK = 64
S5_WIDTH = 512
S5_GROUP = 16
S5_GROUPS = 32
S5_STATE = 64
S5_CHUNK = 16
S5_LB = 128 // S5_GROUP
MLA_HEADS = 8
MLA_NOPE = 64
MLA_ROPE = 32
MLA_V = 64
MLA_Q_RANK = 384
MLA_KV_RANK = 256
ROPE_THETA = 10000.0
GDN_HEADS = 8
GDN_DK = 128
GDN_DV = 128
CONV_W = 4
GDN_QKV = GDN_HEADS * (2 * GDN_DK + GDN_DV)
GDN_VW = GDN_HEADS * GDN_DV
N_GROUPS = 4
EXPERTS_PER_GROUP = 8
N_EXPERTS = 32
D_EXPERT = 512
TOP_K = 2
MOE_BLOCK = 512
MOE_BLOCK_SMALL = 128
ATTN_SUB = 512
ATTN_TQ = 512
ATTN_TK = 512
TOKEN_BLOCK = 256
S5_ROW_BLOCK = 256
S5_SCAN_BLOCK = 128
LANES = 128
VMEM_LIMIT = 48 * 1024 * 1024


def _cparams(*sem):
    return pltpu.CompilerParams(dimension_semantics=sem, vmem_limit_bytes=VMEM_LIMIT)


def _dot(a, b):
    return jnp.dot(a, b, preferred_element_type=F32)


def _dot_nt(a, b):
    return lax.dot_general(a, b, (((1,), (1,)), ((), ())), preferred_element_type=F32)


def _full(arr):
    nd = arr.ndim
    return pl.BlockSpec(arr.shape, lambda *_: (0,) * nd)


def _tok(width, ts):
    return pl.BlockSpec((1, ts, width), lambda b, j: (b, j, 0))


def _modspec(arr, ts):
    if arr.shape[1] == 1:
        return pl.BlockSpec((1, 1, arr.shape[2]), lambda b, j: (b, 0, 0))
    return pl.BlockSpec((1, ts, arr.shape[2]), lambda b, j: (b, j, 0))


def _normmod(x, gain, sc, sh):
    ms = jnp.mean(x * x, axis=-1, keepdims=True)
    return x * lax.rsqrt(ms + EPS) * gain * (1.0 + sc) + sh


def _rms(x, gain):
    return x * lax.rsqrt(jnp.mean(x * x, axis=-1, keepdims=True) + EPS) * gain


def _mod_kernel(c_ref, w_ref, b_ref, o_ref):
    c = c_ref[...]
    a = (c * jax.nn.sigmoid(c)).astype(BF16)
    o_ref[0] = _dot(a, w_ref[0]) + b_ref[0]


def _mod_call(c_all, w_mod, b_mod):
    depth, d, n6 = w_mod.shape
    bp = c_all.shape[0]
    tn = 1536
    return pl.pallas_call(
        _mod_kernel,
        grid=(depth, n6 // tn),
        in_specs=[pl.BlockSpec((bp, d), lambda l, n: (0, 0)),
                  pl.BlockSpec((1, d, tn), lambda l, n: (l, 0, n)),
                  pl.BlockSpec((1, 1, tn), lambda l, n: (l, 0, n))],
        out_specs=pl.BlockSpec((1, bp, tn), lambda l, n: (l, 0, n)),
        out_shape=jax.ShapeDtypeStruct((depth, bp, n6), F32),
        compiler_params=_cparams("parallel", "parallel"),
        name="mod",
    )(c_all, w_mod.astype(BF16), b_mod.reshape(depth, 1, n6))


def _ab_in_kernel(x_ref, gain_ref, sc_ref, sh_ref, w_ref, qg_ref, wqa_ref, wqb_ref, kvg_ref,
                  cq_ref, sq_ref, ck_ref, sk_ref, *rest, with_kv):
    if with_kv:
        wk_ref, wv_ref, u_ref, q_ref, lat_ref, kr_ref, k_ref, v_ref = rest
    else:
        u_ref, q_ref, lat_ref, kr_ref = rest
    h = _normmod(x_ref[0], gain_ref[...], sc_ref[0], sh_ref[0]).astype(BF16)
    proj = _dot(h, w_ref[...])
    u_ref[0] = proj[:, :S5_WIDTH]
    o1 = S5_WIDTH + MLA_Q_RANK
    o2 = o1 + MLA_KV_RANK
    qn = _rms(proj[:, S5_WIDTH:o1], qg_ref[...]).astype(BF16)
    qa = _dot(qn, wqa_ref[...])
    qb = _dot(qn, wqb_ref[...])
    cq = cq_ref[...]
    sq = sq_ref[...]
    for hh in range(MLA_HEADS):
        sl = slice(LANES * hh, LANES * (hh + 1))
        q_ref[0, hh] = (qa[:, sl] * cq + qb[:, sl] * sq).astype(BF16)
    lat = _rms(proj[:, o1:o2], kvg_ref[...])
    kr = proj[:, o2:o2 + LANES] * ck_ref[...] + proj[:, o2 + LANES:o2 + 2 * LANES] * sk_ref[...]
    lat_ref[0] = lat
    kr_ref[0] = kr
    if with_kv:
        _keys_values(lat.astype(BF16), kr, wk_ref, wv_ref, k_ref, v_ref)


def _keys_values(lat, kr, wk_ref, wv_ref, k_ref, v_ref):
    kk = _dot(lat, wk_ref[...])
    vv = _dot(lat, wv_ref[...])
    ones = (lax.broadcasted_iota(jnp.int32, (1, LANES), 1) >= MLA_V).astype(F32)
    for hh in range(MLA_HEADS):
        k_ref[0, hh] = (kk[:, LANES * hh:LANES * (hh + 1)] + kr).astype(BF16)
        v_ref[0, hh] = (vv[:, LANES * hh:LANES * (hh + 1)] + ones).astype(BF16)


def _ab_in_call(x, gain, sc, sh, wp, tabs, ts, with_kv):
    b, s, d = x.shape
    cq, sq, ck, sk = tabs
    tab = pl.BlockSpec((ts, LANES), lambda bb, j: (j, 0))
    hspec = pl.BlockSpec((1, MLA_HEADS, ts, LANES), lambda bb, j: (bb, 0, j, 0))
    hshape = jax.ShapeDtypeStruct((b, MLA_HEADS, s, LANES), BF16)
    ins = [x, gain, sc, sh, wp["w_in"], wp["q_norm"], wp["wqa"], wp["wqb"], wp["kv_norm"], cq, sq, ck, sk]
    specs = [_tok(d, ts), _full(gain), _modspec(sc, ts), _modspec(sh, ts), _full(wp["w_in"]), _full(wp["q_norm"]),
             _full(wp["wqa"]), _full(wp["wqb"]), _full(wp["kv_norm"]), tab, tab, tab, tab]
    out_specs = [_tok(S5_WIDTH, ts), hspec, _tok(MLA_KV_RANK, ts), _tok(LANES, ts)]
    out_shape = [jax.ShapeDtypeStruct((b, s, S5_WIDTH), F32), hshape,
                 jax.ShapeDtypeStruct((b, s, MLA_KV_RANK), F32),
                 jax.ShapeDtypeStruct((b, s, LANES), F32)]
    if with_kv:
        ins += [wp["wk"], wp["wv"]]
        specs += [_full(wp["wk"]), _full(wp["wv"])]
        out_specs += [hspec, hspec]
        out_shape += [hshape, hshape]
    return pl.pallas_call(
        functools.partial(_ab_in_kernel, with_kv=with_kv),
        grid=(b, s // ts),
        in_specs=specs,
        out_specs=out_specs,
        out_shape=out_shape,
        compiler_params=_cparams("parallel", "parallel"),
        name="ab_in",
    )(*ins)


def _kv_kernel(lat_ref, kr_ref, wk_ref, wv_ref, k_ref, v_ref):
    _keys_values(lat_ref[0].astype(BF16), kr_ref[0], wk_ref, wv_ref, k_ref, v_ref)


def _kv_call(lat_all, kr_all, wp, ts):
    b, sk, _ = lat_all.shape
    hspec = pl.BlockSpec((1, MLA_HEADS, ts, LANES), lambda bb, j: (bb, 0, j, 0))
    hshape = jax.ShapeDtypeStruct((b, MLA_HEADS, sk, LANES), BF16)
    return pl.pallas_call(
        _kv_kernel,
        grid=(b, sk // ts),
        in_specs=[_tok(MLA_KV_RANK, ts), _tok(LANES, ts), _full(wp["wk"]), _full(wp["wv"])],
        out_specs=[hspec, hspec],
        out_shape=[hshape, hshape],
        compiler_params=_cparams("parallel", "parallel"),
        name="kv_up",
    )(lat_all, kr_all, wp["wk"], wp["wv"])


def _attn_kernel(q_ref, k_ref, v_ref, *rest, tq, tk, q_off, sk_valid):
    o_ref = rest[-1]
    bias_ref = rest[0] if len(rest) == 2 else None
    i = pl.program_id(2)
    nsub = max(tq // ATTN_SUB, 1)
    tqs = tq // nsub
    chains = [(hh, qi) for hh in range(2) for qi in range(nsub)]
    qs = [q_ref[0, hh, qi * tqs:(qi + 1) * tqs, :] for hh, qi in chains]
    q0 = q_off + i * tq
    lim_full = (q0 // CHUNK + 1) * CHUNK
    lim_tot = jnp.minimum(((q0 + tq - 1) // CHUNK + 1) * CHUNK, sk_valid)
    n_full = jnp.minimum(lim_full // tk, sk_valid // tk)
    n_tot = (lim_tot + tk - 1) // tk

    def step(j, carry, masked):
        off = pl.multiple_of(j * tk, tk)
        nc = range(len(chains))
        if masked and bias_ref is None:
            kpos = off + lax.broadcasted_iota(jnp.int32, (tqs, tk), 1)
            ok = []
            for qi in range(nsub):
                qpos = q0 + qi * tqs + lax.broadcasted_iota(jnp.int32, (tqs, tk), 0)
                ok.append(jnp.logical_and(kpos // CHUNK <= qpos // CHUNK, kpos < sk_valid))
        kt = [k_ref[0, hh, pl.ds(off, tk), :] for hh in range(2)]
        vt = [v_ref[0, hh, pl.ds(off, tk), :] for hh in range(2)]
        s = [_dot_nt(qs[c], kt[chains[c][0]]) for c in nc]
        if masked and bias_ref is None:
            s = [jnp.where(ok[chains[c][1]], s[c], -1e30) for c in nc]
        elif masked:
            s = [s[c] + bias_ref[chains[c][1] * tqs:(chains[c][1] + 1) * tqs, :] for c in nc]
        m_new = [jnp.maximum(carry[c][0], jnp.max(s[c], axis=-1, keepdims=True)) for c in nc]
        alpha = [jnp.exp2(carry[c][0] - m_new[c]) for c in nc]
        p = [jnp.exp2((s[c] - m_new[c]).astype(BF16)) for c in nc]
        acc = [alpha[c] * carry[c][1] + _dot(p[c], vt[chains[c][0]]) for c in nc]
        return tuple((m_new[c], acc[c]) for c in nc)

    one = (jnp.full((tqs, 1), -1e30, F32), jnp.zeros((tqs, LANES), F32))
    carry = lax.fori_loop(0, n_full, functools.partial(step, masked=False), (one,) * len(chains))
    carry = lax.fori_loop(n_full, n_tot, functools.partial(step, masked=True), carry)
    lane = lax.broadcasted_iota(jnp.int32, (tqs, LANES), 1)
    for qi in range(nsub):
        a0 = carry[chains.index((0, qi))][1]
        a1 = carry[chains.index((1, qi))][1]
        r0 = a0 / pltpu.roll(a0, MLA_V, axis=1)
        r1 = a1 / pltpu.roll(a1, MLA_V, axis=1)
        o_ref[0, qi * tqs:(qi + 1) * tqs, :] = jnp.where(
            lane < MLA_V, r0, pltpu.roll(r1, MLA_V, axis=1)).astype(o_ref.dtype)


def _attn_call(q, k, v, tq, tk, q_off, sk_valid):
    b, h, sq, _ = q.shape
    sk = k.shape[2]
    kern = functools.partial(_attn_kernel, tq=tq, tk=tk, q_off=q_off, sk_valid=sk_valid)
    kv_mode = pl.Buffered(2 if sq == tq else 1)
    ins = [q, k, v]
    specs = [pl.BlockSpec((1, 2, tq, LANES), lambda bb, hp, i: (bb, hp, i, 0)),
             pl.BlockSpec((1, 2, sk, LANES), lambda bb, hp, i: (bb, hp, 0, 0), pipeline_mode=kv_mode),
             pl.BlockSpec((1, 2, sk, LANES), lambda bb, hp, i: (bb, hp, 0, 0), pipeline_mode=kv_mode)]
    if q_off == 0 and tq == tk and sk_valid % tk == 0:
        r = jnp.arange(tq, dtype=jnp.int32)
        bias = jnp.where(r[None, :] // CHUNK <= r[:, None] // CHUNK, 0.0, -1e30).astype(F32)
        ins.append(bias)
        specs.append(pl.BlockSpec((tq, tk), lambda bb, hp, i: (0, 0)))
    return pl.pallas_call(
        kern,
        grid=(b, h // 2, sq // tq),
        in_specs=specs,
        out_specs=pl.BlockSpec((1, tq, LANES), lambda bb, hp, i: (bb, i, hp)),
        out_shape=jax.ShapeDtypeStruct((b, sq, h * MLA_V), BF16),
        compiler_params=_cparams("parallel", "parallel", "arbitrary"),
        name="mla_attn",
    )(*ins)


def _chunk_rows(ref, s, tr):
    return ref[pl.ds(s, tr, stride=S5_CHUNK), :]


def _s5_w_kernel(u_ref, bre_ref, bim_ref, wre_ref, wim_ref, *, tr):
    ucat = jnp.concatenate([_chunk_rows(u_ref, s, tr).astype(BF16) for s in range(S5_CHUNK)], axis=1)
    wre_ref[...] = _dot(ucat, bre_ref[0])
    wim_ref[...] = _dot(ucat, bim_ref[0])


def _s5_w_call(u2, bre, bim, tr):
    n = u2.shape[0]
    r = n // S5_CHUNK
    nq = bre.shape[0]
    sw = S5_LB * S5_STATE
    out = jax.ShapeDtypeStruct((r, nq * sw), F32)
    ospec = pl.BlockSpec((tr, sw), lambda q, i: (i, q))
    wspec = pl.BlockSpec((1, S5_CHUNK * LANES, sw), lambda q, i: (q, 0, 0))
    return pl.pallas_call(
        functools.partial(_s5_w_kernel, tr=tr),
        grid=(nq, r // tr),
        in_specs=[pl.BlockSpec((tr * S5_CHUNK, LANES), lambda q, i: (i, q)), wspec, wspec],
        out_specs=[ospec, ospec],
        out_shape=[out, out],
        compiler_params=_cparams("parallel", "parallel"),
        name="s5_chunk_in",
    )(u2, bre, bim)


def _s5_scan_kernel(wre_ref, wim_ref, lre_ref, lim_ref, x0re_ref, x0im_ref, ore_ref, oim_ref, sre, sim, *, tc):
    @pl.when(pl.program_id(1) == 0)
    def _():
        sre[...] = x0re_ref[0]
        sim[...] = x0im_ref[0]

    lr = lre_ref[...]
    li = lim_ref[...]

    def blk(t, carry):
        xr, xi = carry
        base = pl.multiple_of(t * 8, 8)
        wr = wre_ref[0, pl.ds(base, 8), :]
        wi = wim_ref[0, pl.ds(base, 8), :]
        rows_r, rows_i = [], []
        for r in range(8):
            nr = lr * xr - li * xi + wr[r:r + 1]
            ni = lr * xi + li * xr + wi[r:r + 1]
            xr, xi = nr, ni
            rows_r.append(xr)
            rows_i.append(xi)
        ore_ref[0, pl.ds(base, 8), :] = jnp.concatenate(rows_r, axis=0)
        oim_ref[0, pl.ds(base, 8), :] = jnp.concatenate(rows_i, axis=0)
        return xr, xi

    xr, xi = lax.fori_loop(0, tc // 8, blk, (sre[...], sim[...]))
    sre[...] = xr
    sim[...] = xi


def _s5_scan_call(wre, wim, lre, lim, x0re, x0im, tc):
    b, nch, n = wre.shape
    wspec = pl.BlockSpec((1, tc, n), lambda bb, c: (bb, c, 0))
    lspec = pl.BlockSpec((1, n), lambda bb, c: (0, 0))
    xspec = pl.BlockSpec((1, 1, n), lambda bb, c: (bb, 0, 0))
    out = jax.ShapeDtypeStruct((b, nch, n), F32)
    return pl.pallas_call(
        functools.partial(_s5_scan_kernel, tc=tc),
        grid=(b, nch // tc),
        in_specs=[wspec, wspec, lspec, lspec, xspec, xspec],
        out_specs=[wspec, wspec],
        out_shape=[out, out],
        scratch_shapes=[pltpu.VMEM((1, n), F32), pltpu.VMEM((1, n), F32)],
        compiler_params=_cparams("parallel", "arbitrary"),
        name="s5_scan",
    )(wre, wim, lre, lim, x0re, x0im)


def _s5_y_kernel(u_ref, xre_ref, xim_ref, m_ref, c_ref, y_ref, *, tr):
    L = S5_CHUNK
    urev = jnp.concatenate([_chunk_rows(u_ref, s, tr).astype(BF16) for s in reversed(range(L))], axis=1)
    xcat = jnp.concatenate([xre_ref[...].astype(BF16), xim_ref[...].astype(BF16)], axis=1)
    for t in range(L):
        y = _dot(urev[:, (L - 1 - t) * LANES:], m_ref[0, :(t + 1) * LANES, :]) + _dot(xcat, c_ref[0, t])
        y_ref[pl.ds(t, tr, stride=L), :] = y


def _s5_y_call(u2, xre, xim, mst, cc, tr):
    n, w = u2.shape
    nq = mst.shape[0]
    sw = S5_LB * S5_STATE
    uspec = pl.BlockSpec((tr * S5_CHUNK, LANES), lambda q, i: (i, q))
    xspec = pl.BlockSpec((tr, sw), lambda q, i: (i, q))
    return pl.pallas_call(
        functools.partial(_s5_y_kernel, tr=tr),
        grid=(nq, n // (tr * S5_CHUNK)),
        in_specs=[uspec, xspec, xspec,
                  pl.BlockSpec((1, S5_CHUNK * LANES, LANES), lambda q, i: (q, 0, 0)),
                  pl.BlockSpec((1, S5_CHUNK, 2 * sw, LANES), lambda q, i: (q, 0, 0, 0))],
        out_specs=uspec,
        out_shape=jax.ShapeDtypeStruct((n, w), F32),
        compiler_params=_cparams("parallel", "parallel"),
        name="s5_chunk_out",
    )(u2, xre, xim, mst, cc)


def _s5_mats(a_re, a_im, b_re, b_im, c_re, c_im, log_dt):
    g, p = a_re.shape
    L = S5_CHUNK
    lam = lax.complex(jnp.minimum(a_re, -1e-4), a_im)
    lamdt = lam * jnp.exp(log_dt)[:, None]
    lam_bar = jnp.exp(lamdt)
    b_bar = ((lam_bar - 1.0) / lam)[:, :, None] * lax.complex(b_re, b_im)
    cm = lax.complex(c_re, c_im)
    pw = jnp.exp(lamdt[:, :, None] * jnp.arange(L + 1, dtype=F32))
    mm = pw[:, :, :L, None] * b_bar[:, :, None, :]
    kd = jnp.sum((cm[:, :, :, None, None] * mm[:, None]).real, axis=2).transpose(0, 2, 1, 3)
    bp = pw[:, :, L - 1 - jnp.arange(L)][:, :, :, None] * b_bar[:, :, None, :]
    cp = cm[:, :, :, None] * pw[:, None, :, 1:]
    nq = g // S5_LB
    eye = jnp.eye(S5_LB, dtype=F32)

    def lane_block(m, spec, rows, cols):
        return jnp.einsum(spec, m.reshape((nq, S5_LB) + m.shape[1:]), eye).reshape(nq, rows, cols)

    sw = S5_LB * p
    bre = lane_block(bp.real, "qgpsc,hg->qshcgp", L * LANES, sw)
    bim = lane_block(bp.imag, "qgpsc,hg->qshcgp", L * LANES, sw)
    mst = lane_block(kd, "qgdoc,hg->qdhcgo", L * LANES, LANES)
    cre = lane_block(cp.real, "qgopt,hg->qthpgo", L * sw, LANES).reshape(nq, L, sw, LANES)
    cim = lane_block(-cp.imag, "qgopt,hg->qthpgo", L * sw, LANES).reshape(nq, L, sw, LANES)
    return dict(
        bre=bre.astype(BF16), bim=bim.astype(BF16), mst=mst.astype(BF16),
        cc=jnp.concatenate([cre, cim], axis=2).astype(BF16),
        lre=pw[:, :, L].real.reshape(1, g * p), lim=pw[:, :, L].imag.reshape(1, g * p))


def _ab_out_kernel(x_ref, ys_ref, u_ref, at_ref, d_ref, gw_ref, gb_ref, ws_ref, wa_ref, g1_ref, *route_refs):
    o_ref = route_refs[5]
    y = ys_ref[0] + d_ref[...] * u_ref[0]
    y = jax.nn.gelu(y)
    gate = jax.nn.sigmoid(_dot(y.astype(BF16), gw_ref[...]) + gb_ref[...])
    out = _dot((y * gate).astype(BF16), ws_ref[...]) + _dot(at_ref[0], wa_ref[...])
    xn = x_ref[0] + g1_ref[0] * out
    o_ref[0] = xn
    _route(xn, *route_refs[:5], *route_refs[6:])


def _ab_out_call(x, ys, u, attn, wp, g1, rt, ts):
    b, s, d = x.shape
    r_ins, r_specs, r_ospecs, r_oshape, r_scratch = _route_plumbing(rt, b, s, d, ts)
    ins = [x, ys, u, attn, wp["s5_d"], wp["glu_w"], wp["glu_b"], wp["w_out_s5"], wp["w_out_at"], g1] + r_ins
    specs = [_tok(d, ts), _tok(S5_WIDTH, ts), _tok(S5_WIDTH, ts), _tok(MLA_HEADS * MLA_V, ts),
             _full(wp["s5_d"]), _full(wp["glu_w"]), _full(wp["glu_b"]), _full(wp["w_out_s5"]),
             _full(wp["w_out_at"]), _modspec(g1, ts)] + r_specs
    return pl.pallas_call(
        _ab_out_kernel,
        grid=(b, s // ts),
        in_specs=specs,
        out_specs=[_tok(d, ts)] + r_ospecs,
        out_shape=[jax.ShapeDtypeStruct((b, s, d), F32)] + r_oshape,
        scratch_shapes=r_scratch,
        compiler_params=_cparams("arbitrary", "arbitrary"),
        name="ab_out",
    )(*ins)


def _c_in_kernel(x_ref, gain_ref, sc_ref, sh_ref, w_ref, qkv_ref, z_ref, ab_ref):
    h = _normmod(x_ref[0], gain_ref[...], sc_ref[0], sh_ref[0]).astype(BF16)
    proj = _dot(h, w_ref[...])
    qkv_ref[0] = proj[:, :GDN_QKV].astype(BF16)
    z_ref[0] = proj[:, GDN_QKV:GDN_QKV + GDN_VW].astype(BF16)
    ab_ref[0] = proj[:, GDN_QKV + GDN_VW:]


def _c_in_call(x, gain, sc, sh, w, ts):
    b, s, d = x.shape
    return pl.pallas_call(
        _c_in_kernel,
        grid=(b, s // ts),
        in_specs=[_tok(d, ts), _full(gain), _modspec(sc, ts), _modspec(sh, ts), _full(w)],
        out_specs=[_tok(GDN_QKV, ts), _tok(GDN_VW, ts), _tok(LANES, ts)],
        out_shape=[jax.ShapeDtypeStruct((b, s, GDN_QKV), BF16),
                   jax.ShapeDtypeStruct((b, s, GDN_VW), BF16),
                   jax.ShapeDtypeStruct((b, s, LANES), F32)],
        compiler_params=_cparams("parallel", "parallel"),
        name="c_in",
    )(x, gain, sc, sh, w)


def _gdn_prep_kernel(qkv_ref, halo_ref, st_ref, cw_ref, ab_ref, alog_ref, dtb_ref,
                     wq_ref, u_ref, kgt_ref, pm_ref, ee_ref, *, L, nc):
    j = pl.program_id(1)
    rows = nc * L
    prev = jnp.where(j == 0, st_ref[0].astype(F32), halo_ref[0].astype(F32))
    xx = jnp.concatenate([prev, qkv_ref[0].astype(F32)], axis=0)
    cw = cw_ref[...]
    y = xx[8:8 + rows] * cw[CONV_W - 1:CONV_W]
    for t in range(1, CONV_W):
        y += xx[8 - t:8 - t + rows] * cw[CONV_W - 1 - t:CONV_W - t]
    y = y * jax.nn.sigmoid(y)

    ab = ab_ref[0]
    g_all = -jnp.exp(alog_ref[...]) * jax.nn.softplus(ab + dtb_ref[...])
    beta_all = jax.nn.sigmoid(ab)
    row = lax.broadcasted_iota(jnp.int32, (L, L), 0)
    col = lax.broadcasted_iota(jnp.int32, (L, L), 1)
    incl = row >= col
    strict = row > col
    brow = lax.broadcasted_iota(jnp.int32, (rows, rows), 0)
    bcol = lax.broadcasted_iota(jnp.int32, (rows, rows), 1)
    incl_blk = jnp.logical_and(brow >= bcol, brow // L == bcol // L)
    tri_bf = incl_blk.astype(BF16)
    gam_all = sum(_dot(tri_bf, piece) for piece in _split3(g_all))
    eye_l = (lax.broadcasted_iota(jnp.int32, (LANES, LANES), 0)
             == lax.broadcasted_iota(jnp.int32, (LANES, LANES), 1))
    gam_t = sum(_dot_nt(eye_l.astype(BF16), piece) for piece in _split3(gam_all))
    for ci in range(nc):
        ee_ref[0, ci] = jnp.exp(gam_all[(ci + 1) * L - 1:(ci + 1) * L])
    eye64 = (row == col).astype(F32)
    eye_bf = eye_l.astype(BF16)

    ch = [(ci, h) for ci in range(nc) for h in range(GDN_HEADS)]
    cs = range(len(ch))
    koff, voff = GDN_HEADS * GDN_DK, 2 * GDN_HEADS * GDN_DK
    qh = [y[ci * L:(ci + 1) * L, GDN_DK * h:GDN_DK * (h + 1)] for ci, h in ch]
    kh = [y[ci * L:(ci + 1) * L, koff + GDN_DK * h:koff + GDN_DK * (h + 1)] for ci, h in ch]
    vh = [y[ci * L:(ci + 1) * L, voff + GDN_DV * h:voff + GDN_DV * (h + 1)] for ci, h in ch]
    qh = [q * lax.rsqrt(jnp.sum(q * q, axis=-1, keepdims=True) + EPS) * (GDN_DK ** -0.5) for q in qh]
    kh = [k * lax.rsqrt(jnp.sum(k * k, axis=-1, keepdims=True) + EPS) for k in kh]
    gam_c = [gam_all[ci * L:(ci + 1) * L, h:h + 1] for ci, h in ch]
    beta_c = [beta_all[ci * L:(ci + 1) * L, GDN_HEADS + h:GDN_HEADS + h + 1] for ci, h in ch]
    dec = [jnp.exp(jnp.minimum(gam_c[c] - gam_t[h:h + 1, ci * L:(ci + 1) * L], 0.0))
           for c, (ci, h) in enumerate(ch)]
    kb = [k.astype(BF16) for k in kh]
    kk = [_dot_nt(kb[c], kb[c]) for c in cs]
    qk = [_dot_nt(qh[c].astype(BF16), kb[c]) for c in cs]
    a = [jnp.where(strict, beta_c[c] * kk[c] * dec[c], 0.0) for c in cs]
    for c, (ci, h) in enumerate(ch):
        pm_ref[0, ci, h] = jnp.where(incl, qk[c] * dec[c], 0.0).astype(BF16)
    tinv = [eye64 - a[c] for c in cs]
    pw = [_split(a[c]) for c in cs]
    for _ in range(max(L.bit_length() - 2, 0)):
        pw = [_split(_dot3(pw[c], pw[c])) for c in cs]
        tinv = [tinv[c] + _dot3(_split(tinv[c]), pw[c]) for c in cs]
    eg = [jnp.exp(gam_c[c]) for c in cs]
    rhs = [jnp.concatenate([(beta_c[c] * eg[c]) * kh[c], beta_c[c] * vh[c]], axis=1).astype(BF16) for c in cs]
    wu = [_dot(tinv[c].astype(BF16), rhs[c]) for c in cs]
    for c, (ci, h) in enumerate(ch):
        wq_ref[0, ci, h] = jnp.concatenate([wu[c][:, :GDN_DK], qh[c] * eg[c]], axis=0).astype(BF16)
        u_ref[0, h, ci * L:(ci + 1) * L, :] = wu[c][:, GDN_DK:].astype(BF16)
    kg = [(kh[c] * jnp.exp(gam_all[(ci + 1) * L - 1:(ci + 1) * L, h:h + 1] - gam_c[c])).astype(BF16)
          for c, (ci, h) in enumerate(ch)]
    kgt = [_dot_nt(eye_bf, kg[c]) for c in cs]
    for c, (ci, h) in enumerate(ch):
        kgt_ref[0, ci, h] = kgt[c].astype(BF16)


def _split(a):
    hi = a.astype(BF16)
    return hi, (a - hi.astype(F32)).astype(BF16)


def _split3(a):
    hi = a.astype(BF16)
    r1 = a - hi.astype(F32)
    mid = r1.astype(BF16)
    return hi, mid, (r1 - mid.astype(F32)).astype(BF16)


def _dot3(a, b):
    return _dot(a[0], b[0]) + (_dot(a[0], b[1]) + _dot(a[1], b[0]))


GDN_PREP_CHUNKS = 2


def _gdn_prep_call(qkv, st8, conv_w, ab, alog, dtb, L):
    b, s, _ = qkv.shape
    nch = s // L
    nc = GDN_PREP_CHUNKS if nch % GDN_PREP_CHUNKS == 0 else 1
    rows = nc * L
    return pl.pallas_call(
        functools.partial(_gdn_prep_kernel, L=L, nc=nc),
        grid=(b, nch // nc),
        in_specs=[_tok(GDN_QKV, rows),
                  pl.BlockSpec((1, 8, GDN_QKV), lambda bb, j: (bb, jnp.maximum(j * (rows // 8) - 1, 0), 0)),
                  pl.BlockSpec((1, 8, GDN_QKV), lambda bb, j: (bb, 0, 0)),
                  _full(conv_w), _tok(LANES, rows), _full(alog), _full(dtb)],
        out_specs=[pl.BlockSpec((1, nc, GDN_HEADS, 2 * L, LANES), lambda bb, j: (bb, j, 0, 0, 0)),
                   pl.BlockSpec((1, GDN_HEADS, rows, LANES), lambda bb, j: (bb, 0, j, 0)),
                   pl.BlockSpec((1, nc, GDN_HEADS, GDN_DK, L), lambda bb, j: (bb, j, 0, 0, 0)),
                   pl.BlockSpec((1, nc, GDN_HEADS, L, L), lambda bb, j: (bb, j, 0, 0, 0)),
                   pl.BlockSpec((1, nc, 1, LANES), lambda bb, j: (bb, j, 0, 0))],
        out_shape=[jax.ShapeDtypeStruct((b, nch, GDN_HEADS, 2 * L, LANES), BF16),
                   jax.ShapeDtypeStruct((b, GDN_HEADS, s, LANES), BF16),
                   jax.ShapeDtypeStruct((b, nch, GDN_HEADS, GDN_DK, L), BF16),
                   jax.ShapeDtypeStruct((b, nch, GDN_HEADS, L, L), BF16),
                   jax.ShapeDtypeStruct((b, nch, 1, LANES), F32)],
        compiler_params=_cparams("parallel", "parallel"),
        name="gdn_prep",
    )(qkv, qkv, st8, conv_w, ab, alog, dtb)


GDN_BATCH_GROUP = 2


GDN_SEQ_CHUNKS = 8


def _gdn_seq_kernel(wq_ref, u_ref, kgt_ref, pm_ref, ee_ref, s0_ref, o_ref, sf_ref, s_scr, *, L, nc):
    c = pl.program_id(1)

    @pl.when(c == 0)
    def _():
        s_scr[...] = s0_ref[...]

    ch = [(bb, hh) for bb in range(GDN_BATCH_GROUP) for hh in range(GDN_HEADS)]
    st = [s_scr[bb, hh] for bb, hh in ch]
    for ci in range(nc):
        rows = slice(ci * L, (ci + 1) * L)
        sp = [_split(s) for s in st]
        r = [_dot(wq_ref[bb, ci, hh], sp[i][0]) + _dot(wq_ref[bb, ci, hh], sp[i][1]) for i, (bb, hh) in enumerate(ch)]
        v_new = [(u_ref[bb, hh, rows, :].astype(F32) - r[i][:L]).astype(BF16) for i, (bb, hh) in enumerate(ch)]
        o = [r[i][L:] + _dot(pm_ref[bb, ci, hh], v_new[i]) for i, (bb, hh) in enumerate(ch)]
        for i, (bb, hh) in enumerate(ch):
            o_ref[bb, rows, GDN_DV * hh:GDN_DV * (hh + 1)] = o[i].astype(o_ref.dtype)
        st = [ee_ref[bb, ci][:, hh:hh + 1] * st[i] + _dot(kgt_ref[bb, ci, hh], v_new[i])
              for i, (bb, hh) in enumerate(ch)]
    for i, (bb, hh) in enumerate(ch):
        s_scr[bb, hh] = st[i]

    @pl.when(c == pl.num_programs(1) - 1)
    def _():
        sf_ref[...] = s_scr[...]


def _gdn_seq_call(wq, um, kgt, pm, ee, s0, L):
    b, h, s, _ = um.shape
    nch = s // L
    bg = GDN_BATCH_GROUP
    nc = GDN_SEQ_CHUNKS if nch % GDN_SEQ_CHUNKS == 0 else 1
    sspec = pl.BlockSpec((bg, h, GDN_DK, GDN_DV), lambda g, c: (g, 0, 0, 0))
    return pl.pallas_call(
        functools.partial(_gdn_seq_kernel, L=L, nc=nc),
        grid=(b // bg, nch // nc),
        in_specs=[pl.BlockSpec((bg, nc, h, 2 * L, LANES), lambda g, c: (g, c, 0, 0, 0)),
                  pl.BlockSpec((bg, h, nc * L, LANES), lambda g, c: (g, 0, c, 0)),
                  pl.BlockSpec((bg, nc, h, GDN_DK, L), lambda g, c: (g, c, 0, 0, 0)),
                  pl.BlockSpec((bg, nc, h, L, L), lambda g, c: (g, c, 0, 0, 0)),
                  pl.BlockSpec((bg, nc, 1, LANES), lambda g, c: (g, c, 0, 0)),
                  sspec],
        out_specs=[pl.BlockSpec((bg, nc * L, GDN_VW), lambda g, c: (g, c, 0)), sspec],
        out_shape=[jax.ShapeDtypeStruct((b, s, GDN_VW), BF16),
                   jax.ShapeDtypeStruct((b, h, GDN_DK, GDN_DV), F32)],
        scratch_shapes=[pltpu.VMEM((bg, h, GDN_DK, GDN_DV), F32)],
        compiler_params=_cparams("parallel", "arbitrary"),
        name="gdn_seq",
    )(wq, um, kgt, pm, ee, s0)


def _c_out_kernel(x_ref, o_ref, z_ref, gn_ref, w_ref, g1_ref, *route_refs):
    out_ref = route_refs[5]
    o = o_ref[0].astype(F32)
    z = z_ref[0].astype(F32)
    gn = gn_ref[...]
    parts = []
    for hh in range(GDN_HEADS):
        oh = o[:, GDN_DV * hh:GDN_DV * (hh + 1)]
        zh = z[:, GDN_DV * hh:GDN_DV * (hh + 1)]
        parts.append((_rms(oh, gn) * (zh * jax.nn.sigmoid(zh))).astype(BF16))
    out = _dot(jnp.concatenate(parts, axis=1), w_ref[...])
    xn = x_ref[0] + g1_ref[0] * out
    out_ref[0] = xn
    _route(xn, *route_refs[:5], *route_refs[6:])


def _c_out_call(x, o, z, gn, w, g1, rt, ts):
    b, s, d = x.shape
    r_ins, r_specs, r_ospecs, r_oshape, r_scratch = _route_plumbing(rt, b, s, d, ts)
    return pl.pallas_call(
        _c_out_kernel,
        grid=(b, s // ts),
        in_specs=[_tok(d, ts), _tok(GDN_VW, ts), _tok(GDN_VW, ts), _full(gn), _full(w), _modspec(g1, ts)] + r_specs,
        out_specs=[_tok(d, ts)] + r_ospecs,
        out_shape=[jax.ShapeDtypeStruct((b, s, d), F32)] + r_oshape,
        scratch_shapes=r_scratch,
        compiler_params=_cparams("arbitrary", "arbitrary"),
        name="c_out",
    )(x, o, z, gn, w, g1, *r_ins)


def _pack_halves(v):
    n = v.shape[1] // 2
    bits = pltpu.bitcast(v.astype(BF16).astype(F32), jnp.uint32)
    return (bits[:, :n] >> 16) | (bits[:, n:] & jnp.uint32(0xFFFF0000))


def _unpack_halves(w):
    return pltpu.bitcast(w << 16, F32), pltpu.bitcast(w & jnp.uint32(0xFFFF0000), F32)


def _route(xn, gain_ref, sc_ref, sh_ref, wr_ref, br_ref, h_ref, re_ref, rg_ref, cnt_ref, carry):
    @pl.when(jnp.logical_and(pl.program_id(0) == 0, pl.program_id(1) == 0))
    def _():
        carry[...] = jnp.zeros_like(carry)

    h = _normmod(xn, gain_ref[...], sc_ref[0], sh_ref[0])
    h_ref[0] = _pack_halves(h)
    logits = _dot3(_split(h), (wr_ref[0], wr_ref[1])) + br_ref[...]
    lane = lax.broadcasted_iota(jnp.int32, logits.shape, 1).astype(F32)
    neg = jnp.float32(-jnp.inf)
    big = jnp.float32(1e9)
    gl = jnp.where(lane < N_GROUPS, logits, neg)
    gm = jnp.max(gl, axis=-1, keepdims=True)
    g_val = 1.0 / jnp.sum(jnp.exp(gl - gm), axis=-1, keepdims=True)
    g_idx = jnp.min(jnp.where(gl == gm, lane, big), axis=-1, keepdims=True)
    lo = N_GROUPS + EXPERTS_PER_GROUP * g_idx
    el = jnp.where(jnp.logical_and(lane >= lo, lane < lo + EXPERTS_PER_GROUP), logits, neg)
    em = jnp.max(el, axis=-1, keepdims=True)
    i1 = jnp.min(jnp.where(el == em, lane, big), axis=-1, keepdims=True)
    el2 = jnp.where(lane == i1, neg, el)
    em2 = jnp.max(el2, axis=-1, keepdims=True)
    i2 = jnp.min(jnp.where(el2 == em2, lane, big), axis=-1, keepdims=True)
    es = jnp.sum(jnp.exp(el - em), axis=-1, keepdims=True)
    p1 = 1.0 / es
    p2 = jnp.exp(em2 - em) / es
    den = p1 + p2
    rg_ref[0] = jnp.where(lane == 0, g_val * p1 / den, jnp.where(lane == 1, g_val * p2 / den, 0.0))
    oh1 = lane == i1
    oh2 = lane == i2
    both = jnp.where(jnp.logical_or(oh1, oh2), 1.0, 0.0)
    ts = both.shape[0]
    tri = (lax.broadcasted_iota(jnp.int32, (ts, ts), 0) > lax.broadcasted_iota(jnp.int32, (ts, ts), 1))
    pre = _dot(jnp.where(tri, 1.0, 0.0).astype(BF16), both.astype(BF16)) + carry[...]
    r1 = jnp.sum(jnp.where(oh1, pre, 0.0), axis=-1, keepdims=True)
    r2 = jnp.sum(jnp.where(oh2, pre, 0.0), axis=-1, keepdims=True)
    re_ref[0] = jnp.where(lane == 0, i1 - N_GROUPS, jnp.where(lane == 1, i2 - N_GROUPS, jnp.where(
        lane == 2, r1, jnp.where(lane == 3, r2, 0.0)))).astype(jnp.int32)
    carry[...] += jnp.sum(both, axis=0, keepdims=True)
    cnt_ref[...] = carry[...]


def _route_plumbing(rt, b, s, d, ts):
    gain, sc, sh, wr, br = rt
    ins = [gain, sc, sh, wr, br]
    in_specs = [_full(gain), _modspec(sc, ts), _modspec(sh, ts), _full(wr), _full(br)]
    out_specs = [_tok(d // 2, ts), _tok(LANES, ts), _tok(LANES, ts), pl.BlockSpec((1, LANES), lambda bb, j: (0, 0))]
    out_shape = [jax.ShapeDtypeStruct((b, s, d // 2), jnp.uint32),
                 jax.ShapeDtypeStruct((b, s, LANES), jnp.int32),
                 jax.ShapeDtypeStruct((b, s, LANES), F32),
                 jax.ShapeDtypeStruct((1, LANES), F32)]
    return ins, in_specs, out_specs, out_shape, [pltpu.VMEM((1, LANES), F32)]


def _row_copy(src, dst, src_row, dst_row, sem):
    return pltpu.make_async_copy(src.at[pl.ds(src_row, 1)], dst.at[pl.ds(dst_row, 1)], sem)


ROUTE_W = 2 * TOP_K
DMA_UNROLL = 8


def _slot(ps_ref, rt_ref, r, k):
    return ps_ref[rt_ref[0, 0, ROUTE_W * r + k]] + rt_ref[0, 0, ROUTE_W * r + TOP_K + k]


def _dispatch_kernel(ps_ref, rt_ref, h_ref, xin_hbm, xb_hbm, sem, *, tb):
    del xin_hbm

    def issue(r, c):
        for k in range(TOP_K):
            _row_copy(h_ref, xb_hbm, r, _slot(ps_ref, rt_ref, r, k), sem).start()
        return c
    lax.fori_loop(0, tb, issue, 0, unroll=DMA_UNROLL)
    for k in range(TOP_K):
        pltpu.make_async_copy(h_ref, xb_hbm.at[pl.ds(0, tb)], sem).wait()


def _dispatch_call(pad_start, route, h2, xb0, tb):
    n, dh = h2.shape
    nb = n // tb
    grid_spec = pltpu.PrefetchScalarGridSpec(
        num_scalar_prefetch=1,
        grid=(nb,),
        in_specs=[pl.BlockSpec((1, 1, ROUTE_W * tb), lambda j, ps: (j, 0, 0), memory_space=pltpu.SMEM),
                  pl.BlockSpec((tb, dh), lambda j, ps: (j, 0)),
                  pl.BlockSpec(memory_space=pl.ANY)],
        out_specs=pl.BlockSpec(memory_space=pl.ANY),
        scratch_shapes=[pltpu.SemaphoreType.DMA(())],
    )
    return pl.pallas_call(
        functools.partial(_dispatch_kernel, tb=tb),
        grid_spec=grid_spec,
        out_shape=jax.ShapeDtypeStruct(xb0.shape, xb0.dtype),
        input_output_aliases={3: 0},
        compiler_params=_cparams("arbitrary"),
        name="moe_dispatch",
    )(pad_start, route, h2, xb0)


def _ffn_kernel(be_ref, nu_ref, x_ref, w1_ref, w3_ref, w2_ref, y_ref, w1b, w3b, w2b):
    i = pl.program_id(0)
    half = w1b.shape[0] // 2

    @pl.when(jnp.logical_and(i < nu_ref[0], jnp.logical_or(i == 0, be_ref[i] != be_ref[jnp.maximum(i - 1, 0)])))
    def _():
        w1b[...] = w1_ref[0, 0].astype(BF16)
        w3b[...] = w3_ref[0, 0].astype(BF16)
        w2b[...] = w2_ref[0, 0].astype(BF16)

    @pl.when(i < nu_ref[0])
    def _():
        lo, hi = _unpack_halves(x_ref[...])
        lo = lo.astype(BF16)
        hi = hi.astype(BF16)
        a = _dot(lo, w1b[:half]) + _dot(hi, w1b[half:])
        g = _dot(lo, w3b[:half]) + _dot(hi, w3b[half:])
        mid = (a * jax.nn.sigmoid(a) * g).astype(BF16)
        y_ref[...] = _pack_halves(_dot(mid, w2b[...]))

    @pl.when(i >= nu_ref[0])
    def _():
        y_ref[...] = jnp.zeros_like(y_ref)


def _ffn_call(block_e, n_used, xb, w1, w3, w2, layer):
    n_blocks = block_e.shape[0]
    dh = xb.shape[1]
    d = 2 * dh
    blk = xb.shape[0] // n_blocks
    grid_spec = pltpu.PrefetchScalarGridSpec(
        num_scalar_prefetch=2,
        grid=(n_blocks,),
        in_specs=[pl.BlockSpec((blk, dh), lambda i, be, nu: (i, 0)),
                  pl.BlockSpec((1, 1, d, D_EXPERT), lambda i, be, nu: (layer, be[i], 0, 0)),
                  pl.BlockSpec((1, 1, d, D_EXPERT), lambda i, be, nu: (layer, be[i], 0, 0)),
                  pl.BlockSpec((1, 1, D_EXPERT, d), lambda i, be, nu: (layer, be[i], 0, 0))],
        out_specs=pl.BlockSpec((blk, dh), lambda i, be, nu: (i, 0)),
        scratch_shapes=[pltpu.VMEM((d, D_EXPERT), BF16), pltpu.VMEM((d, D_EXPERT), BF16),
                        pltpu.VMEM((D_EXPERT, d), BF16)],
    )
    return pl.pallas_call(
        _ffn_kernel,
        grid_spec=grid_spec,
        out_shape=jax.ShapeDtypeStruct(xb.shape, jnp.uint32),
        compiler_params=_cparams("arbitrary"),
        name="moe_ffn",
    )(block_e, n_used, xb, w1, w3, w2)


def _combine_kernel(ps_ref, rcur_ref, rnxt_ref, yb_hbm, x_ref, rg_ref, g2_ref, fin_ref, o_ref, b0, b1, sem,
                    *, tb, final):
    j = pl.program_id(0)
    nb = pl.num_programs(0)
    slot = j % 2

    def fetch(rref, sl):
        def issue(r, c):
            _row_copy(yb_hbm, b0.at[sl], _slot(ps_ref, rref, r, 0), r, sem.at[sl]).start()
            _row_copy(yb_hbm, b1.at[sl], _slot(ps_ref, rref, r, 1), r, sem.at[sl]).start()
            return c
        lax.fori_loop(0, tb, issue, 0, unroll=DMA_UNROLL)

    @pl.when(j == 0)
    def _():
        fetch(rcur_ref, 0)

    @pl.when(j + 1 < nb)
    def _():
        fetch(rnxt_ref, 1 - slot)

    pltpu.make_async_copy(yb_hbm.at[pl.ds(0, tb)], b0.at[slot], sem.at[slot]).wait()
    pltpu.make_async_copy(yb_hbm.at[pl.ds(0, tb)], b1.at[slot], sem.at[slot]).wait()
    rg = rg_ref[...]
    lo0, hi0 = _unpack_halves(b0[slot])
    lo1, hi1 = _unpack_halves(b1[slot])
    g0 = rg[:, 0:1]
    g1 = rg[:, 1:2]
    y = jnp.concatenate([g0 * lo0 + g1 * lo1, g0 * hi0 + g1 * hi1], axis=1)
    xn = x_ref[...] + g2_ref[0] * y
    if final:
        xn = _rms(xn, fin_ref[...])
    o_ref[...] = xn


def _combine_call(pad_start, route, yb, x, rg, g2, fin, tb, final):
    b, s, d = x.shape
    n = b * s
    nb = n // tb
    npb = s // tb
    dh = yb.shape[1]
    if g2.shape[1] == 1:
        gspec = pl.BlockSpec((1, 1, d), lambda j, ps: (j // npb, 0, 0))
    else:
        gspec = pl.BlockSpec((1, tb, d), lambda j, ps: (j // npb, j % npb, 0))
    grid_spec = pltpu.PrefetchScalarGridSpec(
        num_scalar_prefetch=1,
        grid=(nb,),
        in_specs=[pl.BlockSpec((1, 1, ROUTE_W * tb), lambda j, ps: (j, 0, 0), memory_space=pltpu.SMEM),
                  pl.BlockSpec((1, 1, ROUTE_W * tb), lambda j, ps: (jnp.minimum(j + 1, nb - 1), 0, 0),
                               memory_space=pltpu.SMEM),
                  pl.BlockSpec(memory_space=pl.ANY),
                  pl.BlockSpec((tb, d), lambda j, ps: (j, 0)), pl.BlockSpec((tb, LANES), lambda j, ps: (j, 0)),
                  gspec, pl.BlockSpec(fin.shape, lambda j, ps: (0, 0))],
        out_specs=pl.BlockSpec((tb, d), lambda j, ps: (j, 0)),
        scratch_shapes=[pltpu.VMEM((2, tb, dh), jnp.uint32), pltpu.VMEM((2, tb, dh), jnp.uint32),
                        pltpu.SemaphoreType.DMA((2,))],
    )
    out = pl.pallas_call(
        functools.partial(_combine_kernel, tb=tb, final=final),
        grid_spec=grid_spec,
        out_shape=jax.ShapeDtypeStruct((n, d), F32),
        compiler_params=_cparams("arbitrary"),
        name="moe_combine",
    )(pad_start, route, route, yb, x.reshape(n, d), rg.reshape(n, LANES), g2, fin)
    return out.reshape(b, s, d)


def _moe(x, routed, g2, mp, fin, ts, final):
    b, s, d = x.shape
    n = b * s
    h2, r_e, r_g, cnt = routed
    counts = cnt[0, N_GROUPS:N_GROUPS + N_EXPERTS].astype(jnp.int32)
    blk = MOE_BLOCK if n * TOP_K >= N_EXPERTS * MOE_BLOCK else MOE_BLOCK_SMALL
    padded = (counts + blk - 1) // blk * blk
    pad_end = jnp.cumsum(padded)
    pad_start = (pad_end - padded).astype(jnp.int32)
    n_blocks = (n * TOP_K + N_EXPERTS * (blk - 1) + blk - 1) // blk
    blk0 = jnp.arange(n_blocks, dtype=jnp.int32) * blk
    block_e = jnp.minimum(jnp.sum((pad_end[None, :] <= blk0[:, None]).astype(jnp.int32), axis=1), N_EXPERTS - 1)
    n_used = (pad_end[-1:] // blk).astype(jnp.int32)
    route = r_e[:, :, :ROUTE_W].reshape(n // ts, 1, ROUTE_W * ts)
    xb0 = jnp.zeros((n_blocks * blk, d // 2), jnp.uint32)
    xb = _dispatch_call(pad_start, route, h2.reshape(n, d // 2), xb0, ts)
    yb = _ffn_call(block_e, n_used, xb, mp["w1"], mp["w3"], mp["w2"], mp["layer"])
    return _combine_call(pad_start, route, yb, x, r_g, g2, fin, ts, final)


def _prep_ab(l, p):
    w_in = p["w_in_ab"][l]
    o2 = S5_WIDTH + MLA_Q_RANK + MLA_KV_RANK
    half = MLA_ROPE // 2
    d = w_in.shape[0]
    kr = w_in[:, o2:]
    kr_rot = jnp.concatenate([-kr[:, half:], kr[:, :half]], axis=1)
    z64 = jnp.zeros((d, MLA_NOPE), F32)
    z32 = jnp.zeros((d, LANES - MLA_NOPE - MLA_ROPE), F32)
    w_ext = jnp.concatenate([w_in[:, :o2], z64, kr, z32, z64, kr_rot, z32], axis=1)
    qu = p["mla_q_up"][l].reshape(MLA_Q_RANK, MLA_HEADS, MLA_NOPE + MLA_ROPE)
    qn, qr = qu[:, :, :MLA_NOPE], qu[:, :, MLA_NOPE:]
    zq = jnp.zeros((MLA_Q_RANK, MLA_HEADS, LANES - MLA_NOPE - MLA_ROPE), F32)
    wqa = jnp.concatenate([qn, qr, zq], axis=2).reshape(MLA_Q_RANK, MLA_HEADS * LANES)
    wqb = jnp.concatenate([jnp.zeros_like(qn), -qr[:, :, half:], qr[:, :, :half], zq], axis=2)
    wqb = wqb.reshape(MLA_Q_RANK, MLA_HEADS * LANES)
    kvu = p["mla_kv_up"][l].reshape(MLA_KV_RANK, MLA_HEADS, MLA_NOPE + MLA_V)
    wk = jnp.concatenate([kvu[:, :, :MLA_NOPE], jnp.zeros((MLA_KV_RANK, MLA_HEADS, LANES - MLA_NOPE), F32)], axis=2)
    wo = p["w_out_ab"][l]
    out = dict(
        w_in=w_ext.astype(BF16), q_norm=p["mla_q_norm"][l].reshape(1, -1), kv_norm=p["mla_kv_norm"][l].reshape(1, -1),
        wqa=wqa.astype(BF16), wqb=wqb.astype(BF16),
        wk=wk.reshape(MLA_KV_RANK, MLA_HEADS * LANES).astype(BF16),
        wv=jnp.concatenate([kvu[:, :, MLA_NOPE:], jnp.zeros((MLA_KV_RANK, MLA_HEADS, LANES - MLA_V), F32)],
                           axis=2).reshape(MLA_KV_RANK, MLA_HEADS * LANES).astype(BF16),
        s5_d=p["s5_d"][l].reshape(1, -1), glu_w=p["s5_glu_w"][l].astype(BF16), glu_b=p["s5_glu_b"][l].reshape(1, -1),
        w_out_s5=wo[:S5_WIDTH].astype(BF16),
        w_out_at=wo[S5_WIDTH:].astype(BF16))
    out.update(_s5_mats(p["s5_a_re"][l], p["s5_a_im"][l], p["s5_b_re"][l], p["s5_b_im"][l],
                        p["s5_c_re"][l], p["s5_c_im"][l], p["s5_log_dt"][l]))
    return out


def _prep_c(l, p):
    w = p["w_in_c"][l]
    pad = jnp.zeros((w.shape[0], LANES - 2 * GDN_HEADS), F32)

    def row(v):
        return jnp.concatenate([v, jnp.zeros((LANES - GDN_HEADS,), F32)]).reshape(1, LANES)

    return dict(w_in=jnp.concatenate([w, pad], axis=1).astype(BF16), conv_w=p["conv_w"][l],
                alog=row(p["gdn_a_log"][l]), dtb=row(p["gdn_dt_bias"][l]),
                gn=p["gdn_norm"][l].reshape(1, -1), w_out=p["w_out_c"][l].astype(BF16))


def _prep_moe(layer, p):
    d = p["moe_w_group"].shape[1]
    wr = jnp.concatenate([p["moe_w_group"][layer], p["moe_w_expert"][layer],
                          jnp.zeros((d, LANES - N_GROUPS - N_EXPERTS), F32)], axis=1)
    br = jnp.concatenate([p["moe_b_group"][layer], p["moe_b_expert"][layer],
                          jnp.zeros((LANES - N_GROUPS - N_EXPERTS,), F32)]).reshape(1, LANES)
    wr_hi = wr.astype(BF16)
    wr_lo = (wr - wr_hi.astype(F32)).astype(BF16)
    return dict(wr=jnp.stack([wr_hi, wr_lo]), br=br, w1=p["moe_w1"], w3=p["moe_w3"], w2=p["moe_w2"], layer=layer)


def _rope_tables(pos):
    half = MLA_ROPE // 2
    inv = ROPE_THETA ** (-jnp.arange(half, dtype=F32) / half)
    ang = pos.astype(F32)[:, None] * inv[None, :]
    cos, sin = jnp.cos(ang), jnp.sin(ang)
    n = pos.shape[0]
    z64 = jnp.zeros((n, MLA_NOPE), F32)
    z32 = jnp.zeros((n, LANES - MLA_NOPE - MLA_ROPE), F32)
    scale = (MLA_NOPE + MLA_ROPE) ** -0.5 * math.log2(math.e)
    ck = jnp.concatenate([z64, cos, cos, z32], axis=1)
    sk = jnp.concatenate([z64, sin, sin, z32], axis=1)
    cq = jnp.concatenate([jnp.ones_like(z64), cos, cos, z32], axis=1) * scale
    return cq, sk * scale, ck, sk


def _trunk(x, mods, caches, wab, wc, wmoe, p, flat):
    b, s, d = x.shape
    lat_c, rope_c, s5re_c, s5im_c, conv_c, gdn_c = caches
    past = 0 if lat_c is None else lat_c.shape[2]
    pos = past + jnp.arange(s)
    if flat:
        xt = x.reshape(1, b * s, d)
        ts = b * s
        pos_rows = jnp.tile(pos, b)

        def mrow(m):
            return jnp.repeat(m, s, axis=0).reshape(1, b * s, d)
    else:
        xt = x
        ts = TOKEN_BLOCK
        pos_rows = pos

        def mrow(m):
            return m.reshape(b, 1, d)
    bt, st = xt.shape[:2]
    outs = {}
    depth = p["norm_mix"].shape[0]
    for layer in range(depth):
        l = layer // 2
        sh1, sc1, g1, sh2, sc2, g2 = [mrow(m) for m in jnp.split(mods[layer], 6, axis=-1)]
        gain = p["norm_mix"][layer].reshape(1, d)
        rt = (p["norm_ffn"][layer].reshape(1, d), sc2, sh2, wmoe[layer]["wr"], wmoe[layer]["br"])
        if layer % 2 == 0:
            wp = wab[l]
            with_kv = lat_c is None and not flat
            u, q, lat, kr, *kv = _ab_in_call(xt, gain, sc1, sh1, wp, _rope_tables(pos_rows), ts, with_kv)
            lat_b = lat.reshape(b, s, MLA_KV_RANK)
            kr_b = kr.reshape(b, s, LANES)
            outs["lat"] = lat_b
            outs["krope"] = kr_b[:, :, MLA_NOPE:MLA_NOPE + MLA_ROPE]
            if lat_c is None:
                lat_all, kr_all, sk_valid, tkv = lat_b, kr_b, s, min(TOKEN_BLOCK, s)
                tq, tk = min(ATTN_TQ, s), min(ATTN_TK, s)
            else:
                krc = jnp.pad(rope_c[l], ((0, 0), (0, 0), (MLA_NOPE, LANES - MLA_NOPE - MLA_ROPE)))
                sk_valid = past + s
                skp = (sk_valid + LANES - 1) // LANES * LANES
                lat_all = jnp.pad(jnp.concatenate([lat_c[l], lat_b], axis=1), ((0, 0), (0, skp - sk_valid), (0, 0)))
                kr_all = jnp.pad(jnp.concatenate([krc, kr_b], axis=1), ((0, 0), (0, skp - sk_valid), (0, 0)))
                tkv, tq, tk = skp, s, skp
            kk, vv = kv if with_kv else _kv_call(lat_all, kr_all, wp, tkv)
            qh = q.reshape(MLA_HEADS, b, s, LANES).transpose(1, 0, 2, 3) if flat else q
            attn = _attn_call(qh, kk, vv, tq, tk, past, sk_valid).reshape(bt, st, MLA_HEADS * MLA_V)
            nchk = s // S5_CHUNK
            r = b * nchk
            u2 = u.reshape(b * s, S5_WIDTH)
            tr = min(r, S5_ROW_BLOCK)
            wre, wim = _s5_w_call(u2, wp["bre"], wp["bim"], tr)
            ngp = S5_GROUPS * S5_STATE

            def to_pairs(v):
                return v.reshape(b, 1, ngp)

            if s5re_c is None:
                x0re = jnp.zeros((b, 1, ngp), F32)
                x0im = jnp.zeros((b, 1, ngp), F32)
            else:
                x0re, x0im = to_pairs(s5re_c[l]), to_pairs(s5im_c[l])
            nchp = (nchk + 7) // 8 * 8
            tc = min(nchp, S5_SCAN_BLOCK)
            wre3 = jnp.pad(wre.reshape(b, nchk, ngp), ((0, 0), (0, nchp - nchk), (0, 0)))
            wim3 = jnp.pad(wim.reshape(b, nchk, ngp), ((0, 0), (0, nchp - nchk), (0, 0)))
            xere, xeim = _s5_scan_call(wre3, wim3, wp["lre"], wp["lim"], x0re, x0im, tc)
            outs["s5re"] = xere[:, nchk - 1].reshape(b, S5_GROUPS, S5_STATE)
            outs["s5im"] = xeim[:, nchk - 1].reshape(b, S5_GROUPS, S5_STATE)
            xsre = jnp.concatenate([x0re, xere[:, :nchk - 1]], axis=1).reshape(r, ngp)
            xsim = jnp.concatenate([x0im, xeim[:, :nchk - 1]], axis=1).reshape(r, ngp)
            ys = _s5_y_call(u2, xsre, xsim, wp["mst"], wp["cc"], tr).reshape(bt, st, S5_WIDTH)
            xt, *routed = _ab_out_call(xt, ys, u, attn, wp, g1, rt, ts)
        else:
            wp = wc[l]
            qkv, z, ab = _c_in_call(xt, gain, sc1, sh1, wp["w_in"], ts)
            qkv_b = qkv.reshape(b, s, GDN_QKV)
            outs["conv"] = qkv_b[:, s - (CONV_W - 1):].astype(F32)
            if conv_c is None:
                st8 = jnp.zeros((b, 8, GDN_QKV), BF16)
                s0 = jnp.zeros((b, GDN_HEADS, GDN_DK, GDN_DV), F32)
            else:
                st8 = jnp.pad(conv_c[l], ((0, 0), (8 - (CONV_W - 1), 0), (0, 0))).astype(BF16)
                s0 = gdn_c[l]
            lg = min(s, CHUNK)
            wq, um, kgt, pm, ee = _gdn_prep_call(qkv_b, st8, wp["conv_w"], ab.reshape(b, s, LANES),
                                                 wp["alog"], wp["dtb"], lg)
            o, sfin = _gdn_seq_call(wq, um, kgt, pm, ee, s0, lg)
            outs["gdn"] = sfin
            xt, *routed = _c_out_call(xt, o.reshape(bt, st, GDN_VW), z, wp["gn"], wp["w_out"], g1, rt, ts)
        final = layer == depth - 1
        xt = _moe(xt, routed, g2, wmoe[layer], p["norm_final"].reshape(1, d), ts, final)
    y = xt.reshape(b, s, d)
    return (y, outs["lat"][None], outs["krope"][None], outs["s5re"][None], outs["s5im"][None],
            outs["conv"][None], outs["gdn"][None])


def kernel(x_prompt, x_sample, c_prompt, c_sample, cache_mla_latent, cache_mla_krope, state_s5_re, state_s5_im,
           state_conv, state_gdn, w_mod, b_mod, norm_mix, norm_ffn, norm_final, w_in_ab, s5_a_re, s5_a_im,
           s5_b_re, s5_b_im, s5_c_re, s5_c_im, s5_d, s5_log_dt, s5_glu_w, s5_glu_b, mla_q_norm, mla_q_up,
           mla_kv_norm, mla_kv_up, w_out_ab, w_in_c, conv_w, gdn_a_log, gdn_dt_bias, gdn_norm, w_out_c,
           moe_w_group, moe_b_group, moe_w_expert, moe_b_expert, moe_w1, moe_w3, moe_w2):
    p = dict(w_mod=w_mod, b_mod=b_mod, norm_mix=norm_mix, norm_ffn=norm_ffn, norm_final=norm_final,
             w_in_ab=w_in_ab, s5_a_re=s5_a_re, s5_a_im=s5_a_im, s5_b_re=s5_b_re, s5_b_im=s5_b_im,
             s5_c_re=s5_c_re, s5_c_im=s5_c_im, s5_d=s5_d, s5_log_dt=s5_log_dt, s5_glu_w=s5_glu_w,
             s5_glu_b=s5_glu_b, mla_q_norm=mla_q_norm, mla_q_up=mla_q_up, mla_kv_norm=mla_kv_norm,
             mla_kv_up=mla_kv_up, w_out_ab=w_out_ab, w_in_c=w_in_c, conv_w=conv_w, gdn_a_log=gdn_a_log,
             gdn_dt_bias=gdn_dt_bias, gdn_norm=gdn_norm, w_out_c=w_out_c, moe_w_group=moe_w_group,
             moe_b_group=moe_b_group, moe_w_expert=moe_w_expert, moe_b_expert=moe_b_expert,
             moe_w1=moe_w1, moe_w3=moe_w3, moe_w2=moe_w2)
    depth = norm_mix.shape[0]
    bp, bs = c_prompt.shape[0], c_sample.shape[0]
    nb = (bp + bs + 7) // 8 * 8
    c_all = jnp.pad(jnp.concatenate([c_prompt, c_sample], axis=0), ((0, nb - bp - bs), (0, 0)))
    mods = _mod_call(c_all, w_mod, b_mod)
    wab = [_prep_ab(l, p) for l in range((depth + 1) // 2)]
    wc = [_prep_c(l, p) for l in range(depth // 2)]
    wmoe = [_prep_moe(layer, p) for layer in range(depth)]
    none6 = (None,) * 6
    outp = _trunk(x_prompt, mods[:, :bp], none6, wab, wc, wmoe, p, flat=False)
    caches = (cache_mla_latent, cache_mla_krope, state_s5_re, state_s5_im, state_conv, state_gdn)
    outs = _trunk(x_sample, mods[:, bp:bp + bs], caches, wab, wc, wmoe, p, flat=True)
    return (outp[0], outs[0]) + tuple(outp[1:]) + tuple(outs[1:])
```

```python
import functools
import math

import jax
import jax.numpy as jnp
from jax import lax
from jax.experimental import pallas as pl
from jax.experimental.pallas import tpu as pltpu

F32 = jnp.float32
BF16 = jnp.bfloat16
EPS = 1e-6

CHUNK = 64
S5_WIDTH = 512
S5_GROUP = 16
S5_GROUPS = 32
S5_STATE = 64
S5_CHUNK = 16
S5_LB = 128 // S5_GROUP
MLA_HEADS = 8
MLA_NOPE = 64
MLA_ROPE = 32
MLA_V = 64
MLA_Q_RANK = 384
MLA_KV_RANK = 256
ROPE_THETA = 10000.0
GDN_HEADS = 8
GDN_DK = 128
GDN_DV = 128
CONV_W = 4
GDN_QKV = GDN_HEADS * (2 * GDN_DK + GDN_DV)
GDN_VW = GDN_HEADS * GDN_DV
N_GROUPS = 4
EXPERTS_PER_GROUP = 8
N_EXPERTS = 32
D_EXPERT = 512
TOP_K = 2
MOE_BLOCK = 512
MOE_BLOCK_SMALL = 128
ATTN_SUB = 512
ATTN_TQ = 512
ATTN_TK = 512
TOKEN_BLOCK = 256
S5_ROW_BLOCK = 256
S5_SCAN_BLOCK = 128
LANES = 128
VMEM_LIMIT = 48 * 1024 * 1024


def _cparams(*sem):
    return pltpu.CompilerParams(dimension_semantics=sem, vmem_limit_bytes=VMEM_LIMIT)


def _dot(a, b):
    return jnp.dot(a, b, preferred_element_type=F32)


def _dot_nt(a, b):
    return lax.dot_general(a, b, (((1,), (1,)), ((), ())), preferred_element_type=F32)


def _full(arr):
    nd = arr.ndim
    return pl.BlockSpec(arr.shape, lambda *_: (0,) * nd)


def _tok(width, ts):
    return pl.BlockSpec((1, ts, width), lambda b, j: (b, j, 0))


def _modspec(arr, ts):
    if arr.shape[1] == 1:
        return pl.BlockSpec((1, 1, arr.shape[2]), lambda b, j: (b, 0, 0))
    return pl.BlockSpec((1, ts, arr.shape[2]), lambda b, j: (b, j, 0))


def _normmod(x, gain, sc, sh):
    ms = jnp.mean(x * x, axis=-1, keepdims=True)
    return x * lax.rsqrt(ms + EPS) * gain * (1.0 + sc) + sh


def _rms(x, gain):
    return x * lax.rsqrt(jnp.mean(x * x, axis=-1, keepdims=True) + EPS) * gain


def _mod_kernel(c_ref, w_ref, b_ref, o_ref):
    c = c_ref[...]
    a = (c * jax.nn.sigmoid(c)).astype(BF16)
    o_ref[0] = _dot(a, w_ref[0]) + b_ref[0]


def _mod_call(c_all, w_mod, b_mod):
    depth, d, n6 = w_mod.shape
    bp = c_all.shape[0]
    tn = 1536
    return pl.pallas_call(
        _mod_kernel,
        grid=(depth, n6 // tn),
        in_specs=[pl.BlockSpec((bp, d), lambda l, n: (0, 0)),
                  pl.BlockSpec((1, d, tn), lambda l, n: (l, 0, n)),
                  pl.BlockSpec((1, 1, tn), lambda l, n: (l, 0, n))],
        out_specs=pl.BlockSpec((1, bp, tn), lambda l, n: (l, 0, n)),
        out_shape=jax.ShapeDtypeStruct((depth, bp, n6), F32),
        compiler_params=_cparams("parallel", "parallel"),
        name="mod",
    )(c_all, w_mod.astype(BF16), b_mod.reshape(depth, 1, n6))


def _ab_in_kernel(x_ref, gain_ref, sc_ref, sh_ref, w_ref, qg_ref, wqa_ref, wqb_ref, kvg_ref,
                  cq_ref, sq_ref, ck_ref, sk_ref, *rest, with_kv):
    if with_kv:
        wk_ref, wv_ref, u_ref, q_ref, lat_ref, kr_ref, k_ref, v_ref = rest
    else:
        u_ref, q_ref, lat_ref, kr_ref = rest
    h = _normmod(x_ref[0], gain_ref[...], sc_ref[0], sh_ref[0]).astype(BF16)
    proj = _dot(h, w_ref[...])
    u_ref[0] = proj[:, :S5_WIDTH]
    o1 = S5_WIDTH + MLA_Q_RANK
    o2 = o1 + MLA_KV_RANK
    qn = _rms(proj[:, S5_WIDTH:o1], qg_ref[...]).astype(BF16)
    qa = _dot(qn, wqa_ref[...])
    qb = _dot(qn, wqb_ref[...])
    cq = cq_ref[...]
    sq = sq_ref[...]
    for hh in range(MLA_HEADS):
        sl = slice(LANES * hh, LANES * (hh + 1))
        q_ref[0, hh] = (qa[:, sl] * cq + qb[:, sl] * sq).astype(BF16)
    lat = _rms(proj[:, o1:o2], kvg_ref[...])
    kr = proj[:, o2:o2 + LANES] * ck_ref[...] + proj[:, o2 + LANES:o2 + 2 * LANES] * sk_ref[...]
    lat_ref[0] = lat
    kr_ref[0] = kr
    if with_kv:
        _keys_values(lat.astype(BF16), kr, wk_ref, wv_ref, k_ref, v_ref)


def _keys_values(lat, kr, wk_ref, wv_ref, k_ref, v_ref):
    kk = _dot(lat, wk_ref[...])
    vv = _dot(lat, wv_ref[...])
    ones = (lax.broadcasted_iota(jnp.int32, (1, LANES), 1) >= MLA_V).astype(F32)
    for hh in range(MLA_HEADS):
        k_ref[0, hh] = (kk[:, LANES * hh:LANES * (hh + 1)] + kr).astype(BF16)
        v_ref[0, hh] = (vv[:, LANES * hh:LANES * (hh + 1)] + ones).astype(BF16)


def _ab_in_call(x, gain, sc, sh, wp, tabs, ts, with_kv):
    b, s, d = x.shape
    cq, sq, ck, sk = tabs
    tab = pl.BlockSpec((ts, LANES), lambda bb, j: (j, 0))
    hspec = pl.BlockSpec((1, MLA_HEADS, ts, LANES), lambda bb, j: (bb, 0, j, 0))
    hshape = jax.ShapeDtypeStruct((b, MLA_HEADS, s, LANES), BF16)
    ins = [x, gain, sc, sh, wp["w_in"], wp["q_norm"], wp["wqa"], wp["wqb"], wp["kv_norm"], cq, sq, ck, sk]
    specs = [_tok(d, ts), _full(gain), _modspec(sc, ts), _modspec(sh, ts), _full(wp["w_in"]), _full(wp["q_norm"]),
             _full(wp["wqa"]), _full(wp["wqb"]), _full(wp["kv_norm"]), tab, tab, tab, tab]
    out_specs = [_tok(S5_WIDTH, ts), hspec, _tok(MLA_KV_RANK, ts), _tok(LANES, ts)]
    out_shape = [jax.ShapeDtypeStruct((b, s, S5_WIDTH), F32), hshape,
                 jax.ShapeDtypeStruct((b, s, MLA_KV_RANK), F32),
                 jax.ShapeDtypeStruct((b, s, LANES), F32)]
    if with_kv:
        ins += [wp["wk"], wp["wv"]]
        specs += [_full(wp["wk"]), _full(wp["wv"])]
        out_specs += [hspec, hspec]
        out_shape += [hshape, hshape]
    return pl.pallas_call(
        functools.partial(_ab_in_kernel, with_kv=with_kv),
        grid=(b, s // ts),
        in_specs=specs,
        out_specs=out_specs,
        out_shape=out_shape,
        compiler_params=_cparams("parallel", "parallel"),
        name="ab_in",
    )(*ins)


def _kv_kernel(lat_ref, kr_ref, wk_ref, wv_ref, k_ref, v_ref):
    _keys_values(lat_ref[0].astype(BF16), kr_ref[0], wk_ref, wv_ref, k_ref, v_ref)


def _kv_call(lat_all, kr_all, wp, ts):
    b, sk, _ = lat_all.shape
    hspec = pl.BlockSpec((1, MLA_HEADS, ts, LANES), lambda bb, j: (bb, 0, j, 0))
    hshape = jax.ShapeDtypeStruct((b, MLA_HEADS, sk, LANES), BF16)
    return pl.pallas_call(
        _kv_kernel,
        grid=(b, sk // ts),
        in_specs=[_tok(MLA_KV_RANK, ts), _tok(LANES, ts), _full(wp["wk"]), _full(wp["wv"])],
        out_specs=[hspec, hspec],
        out_shape=[hshape, hshape],
        compiler_params=_cparams("parallel", "parallel"),
        name="kv_up",
    )(lat_all, kr_all, wp["wk"], wp["wv"])


def _attn_kernel(q_ref, k_ref, v_ref, *rest, tq, tk, q_off, sk_valid):
    o_ref = rest[-1]
    bias_ref = rest[0] if len(rest) == 2 else None
    i = pl.program_id(2)
    nsub = max(tq // ATTN_SUB, 1)
    tqs = tq // nsub
    chains = [(hh, qi) for hh in range(2) for qi in range(nsub)]
    qs = [q_ref[0, hh, qi * tqs:(qi + 1) * tqs, :] for hh, qi in chains]
    q0 = q_off + i * tq
    lim_full = (q0 // CHUNK + 1) * CHUNK
    lim_tot = jnp.minimum(((q0 + tq - 1) // CHUNK + 1) * CHUNK, sk_valid)
    n_full = jnp.minimum(lim_full // tk, sk_valid // tk)
    n_tot = (lim_tot + tk - 1) // tk

    def step(j, carry, masked):
        off = pl.multiple_of(j * tk, tk)
        nc = range(len(chains))
        if masked and bias_ref is None:
            kpos = off + lax.broadcasted_iota(jnp.int32, (tqs, tk), 1)
            ok = []
            for qi in range(nsub):
                qpos = q0 + qi * tqs + lax.broadcasted_iota(jnp.int32, (tqs, tk), 0)
                ok.append(jnp.logical_and(kpos // CHUNK <= qpos // CHUNK, kpos < sk_valid))
        kt = [k_ref[0, hh, pl.ds(off, tk), :] for hh in range(2)]
        vt = [v_ref[0, hh, pl.ds(off, tk), :] for hh in range(2)]
        s = [_dot_nt(qs[c], kt[chains[c][0]]) for c in nc]
        if masked and bias_ref is None:
            s = [jnp.where(ok[chains[c][1]], s[c], -1e30) for c in nc]
        elif masked:
            s = [s[c] + bias_ref[chains[c][1] * tqs:(chains[c][1] + 1) * tqs, :] for c in nc]
        m_new = [jnp.maximum(carry[c][0], jnp.max(s[c], axis=-1, keepdims=True)) for c in nc]
        alpha = [jnp.exp2(carry[c][0] - m_new[c]) for c in nc]
        p = [jnp.exp2((s[c] - m_new[c]).astype(BF16)) for c in nc]
        acc = [alpha[c] * carry[c][1] + _dot(p[c], vt[chains[c][0]]) for c in nc]
        return tuple((m_new[c], acc[c]) for c in nc)

    one = (jnp.full((tqs, 1), -1e30, F32), jnp.zeros((tqs, LANES), F32))
    carry = lax.fori_loop(0, n_full, functools.partial(step, masked=False), (one,) * len(chains))
    carry = lax.fori_loop(n_full, n_tot, functools.partial(step, masked=True), carry)
    lane = lax.broadcasted_iota(jnp.int32, (tqs, LANES), 1)
    for qi in range(nsub):
        a0 = carry[chains.index((0, qi))][1]
        a1 = carry[chains.index((1, qi))][1]
        r0 = a0 / pltpu.roll(a0, MLA_V, axis=1)
        r1 = a1 / pltpu.roll(a1, MLA_V, axis=1)
        o_ref[0, qi * tqs:(qi + 1) * tqs, :] = jnp.where(
            lane < MLA_V, r0, pltpu.roll(r1, MLA_V, axis=1)).astype(o_ref.dtype)


def _attn_call(q, k, v, tq, tk, q_off, sk_valid):
    b, h, sq, _ = q.shape
    sk = k.shape[2]
    kern = functools.partial(_attn_kernel, tq=tq, tk=tk, q_off=q_off, sk_valid=sk_valid)
    kv_mode = pl.Buffered(2 if sq == tq else 1)
    ins = [q, k, v]
    specs = [pl.BlockSpec((1, 2, tq, LANES), lambda bb, hp, i: (bb, hp, i, 0)),
             pl.BlockSpec((1, 2, sk, LANES), lambda bb, hp, i: (bb, hp, 0, 0), pipeline_mode=kv_mode),
             pl.BlockSpec((1, 2, sk, LANES), lambda bb, hp, i: (bb, hp, 0, 0), pipeline_mode=kv_mode)]
    if q_off == 0 and tq == tk and sk_valid % tk == 0:
        r = jnp.arange(tq, dtype=jnp.int32)
        bias = jnp.where(r[None, :] // CHUNK <= r[:, None] // CHUNK, 0.0, -1e30).astype(F32)
        ins.append(bias)
        specs.append(pl.BlockSpec((tq, tk), lambda bb, hp, i: (0, 0)))
    return pl.pallas_call(
        kern,
        grid=(b, h // 2, sq // tq),
        in_specs=specs,
        out_specs=pl.BlockSpec((1, tq, LANES), lambda bb, hp, i: (bb, i, hp)),
        out_shape=jax.ShapeDtypeStruct((b, sq, h * MLA_V), BF16),
        compiler_params=_cparams("parallel", "parallel", "arbitrary"),
        name="mla_attn",
    )(*ins)


def _chunk_rows(ref, s, tr):
    return ref[pl.ds(s, tr, stride=S5_CHUNK), :]


def _s5_w_kernel(u_ref, bre_ref, bim_ref, wre_ref, wim_ref, *, tr):
    ucat = jnp.concatenate([_chunk_rows(u_ref, s, tr).astype(BF16) for s in range(S5_CHUNK)], axis=1)
    wre_ref[...] = _dot(ucat, bre_ref[0])
    wim_ref[...] = _dot(ucat, bim_ref[0])


def _s5_w_call(u2, bre, bim, tr):
    n = u2.shape[0]
    r = n // S5_CHUNK
    nq = bre.shape[0]
    sw = S5_LB * S5_STATE
    out = jax.ShapeDtypeStruct((r, nq * sw), F32)
    ospec = pl.BlockSpec((tr, sw), lambda q, i: (i, q))
    wspec = pl.BlockSpec((1, S5_CHUNK * LANES, sw), lambda q, i: (q, 0, 0))
    return pl.pallas_call(
        functools.partial(_s5_w_kernel, tr=tr),
        grid=(nq, r // tr),
        in_specs=[pl.BlockSpec((tr * S5_CHUNK, LANES), lambda q, i: (i, q)), wspec, wspec],
        out_specs=[ospec, ospec],
        out_shape=[out, out],
        compiler_params=_cparams("parallel", "parallel"),
        name="s5_chunk_in",
    )(u2, bre, bim)


def _s5_scan_kernel(wre_ref, wim_ref, lre_ref, lim_ref, x0re_ref, x0im_ref, ore_ref, oim_ref, sre, sim, *, tc):
    @pl.when(pl.program_id(1) == 0)
    def _():
        sre[...] = x0re_ref[0]
        sim[...] = x0im_ref[0]

    lr = lre_ref[...]
    li = lim_ref[...]

    def blk(t, carry):
        xr, xi = carry
        base = pl.multiple_of(t * 8, 8)
        wr = wre_ref[0, pl.ds(base, 8), :]
        wi = wim_ref[0, pl.ds(base, 8), :]
        rows_r, rows_i = [], []
        for r in range(8):
            nr = lr * xr - li * xi + wr[r:r + 1]
            ni = lr * xi + li * xr + wi[r:r + 1]
            xr, xi = nr, ni
            rows_r.append(xr)
            rows_i.append(xi)
        ore_ref[0, pl.ds(base, 8), :] = jnp.concatenate(rows_r, axis=0)
        oim_ref[0, pl.ds(base, 8), :] = jnp.concatenate(rows_i, axis=0)
        return xr, xi

    xr, xi = lax.fori_loop(0, tc // 8, blk, (sre[...], sim[...]))
    sre[...] = xr
    sim[...] = xi


def _s5_scan_call(wre, wim, lre, lim, x0re, x0im, tc):
    b, nch, n = wre.shape
    wspec = pl.BlockSpec((1, tc, n), lambda bb, c: (bb, c, 0))
    lspec = pl.BlockSpec((1, n), lambda bb, c: (0, 0))
    xspec = pl.BlockSpec((1, 1, n), lambda bb, c: (bb, 0, 0))
    out = jax.ShapeDtypeStruct((b, nch, n), F32)
    return pl.pallas_call(
        functools.partial(_s5_scan_kernel, tc=tc),
        grid=(b, nch // tc),
        in_specs=[wspec, wspec, lspec, lspec, xspec, xspec],
        out_specs=[wspec, wspec],
        out_shape=[out, out],
        scratch_shapes=[pltpu.VMEM((1, n), F32), pltpu.VMEM((1, n), F32)],
        compiler_params=_cparams("parallel", "arbitrary"),
        name="s5_scan",
    )(wre, wim, lre, lim, x0re, x0im)


def _s5_y_kernel(u_ref, xre_ref, xim_ref, m_ref, c_ref, y_ref, *, tr):
    L = S5_CHUNK
    urev = jnp.concatenate([_chunk_rows(u_ref, s, tr).astype(BF16) for s in reversed(range(L))], axis=1)
    xcat = jnp.concatenate([xre_ref[...].astype(BF16), xim_ref[...].astype(BF16)], axis=1)
    for t in range(L):
        y = _dot(urev[:, (L - 1 - t) * LANES:], m_ref[0, :(t + 1) * LANES, :]) + _dot(xcat, c_ref[0, t])
        y_ref[pl.ds(t, tr, stride=L), :] = y


def _s5_y_call(u2, xre, xim, mst, cc, tr):
    n, w = u2.shape
    nq = mst.shape[0]
    sw = S5_LB * S5_STATE
    uspec = pl.BlockSpec((tr * S5_CHUNK, LANES), lambda q, i: (i, q))
    xspec = pl.BlockSpec((tr, sw), lambda q, i: (i, q))
    return pl.pallas_call(
        functools.partial(_s5_y_kernel, tr=tr),
        grid=(nq, n // (tr * S5_CHUNK)),
        in_specs=[uspec, xspec, xspec,
                  pl.BlockSpec((1, S5_CHUNK * LANES, LANES), lambda q, i: (q, 0, 0)),
                  pl.BlockSpec((1, S5_CHUNK, 2 * sw, LANES), lambda q, i: (q, 0, 0, 0))],
        out_specs=uspec,
        out_shape=jax.ShapeDtypeStruct((n, w), F32),
        compiler_params=_cparams("parallel", "parallel"),
        name="s5_chunk_out",
    )(u2, xre, xim, mst, cc)


def _s5_mats(a_re, a_im, b_re, b_im, c_re, c_im, log_dt):
    g, p = a_re.shape
    L = S5_CHUNK
    lam = lax.complex(jnp.minimum(a_re, -1e-4), a_im)
    lamdt = lam * jnp.exp(log_dt)[:, None]
    lam_bar = jnp.exp(lamdt)
    b_bar = ((lam_bar - 1.0) / lam)[:, :, None] * lax.complex(b_re, b_im)
    cm = lax.complex(c_re, c_im)
    pw = jnp.exp(lamdt[:, :, None] * jnp.arange(L + 1, dtype=F32))
    mm = pw[:, :, :L, None] * b_bar[:, :, None, :]
    kd = jnp.sum((cm[:, :, :, None, None] * mm[:, None]).real, axis=2).transpose(0, 2, 1, 3)
    bp = pw[:, :, L - 1 - jnp.arange(L)][:, :, :, None] * b_bar[:, :, None, :]
    cp = cm[:, :, :, None] * pw[:, None, :, 1:]
    nq = g // S5_LB
    eye = jnp.eye(S5_LB, dtype=F32)

    def lane_block(m, spec, rows, cols):
        return jnp.einsum(spec, m.reshape((nq, S5_LB) + m.shape[1:]), eye).reshape(nq, rows, cols)

    sw = S5_LB * p
    bre = lane_block(bp.real, "qgpsc,hg->qshcgp", L * LANES, sw)
    bim = lane_block(bp.imag, "qgpsc,hg->qshcgp", L * LANES, sw)
    mst = lane_block(kd, "qgdoc,hg->qdhcgo", L * LANES, LANES)
    cre = lane_block(cp.real, "qgopt,hg->qthpgo", L * sw, LANES).reshape(nq, L, sw, LANES)
    cim = lane_block(-cp.imag, "qgopt,hg->qthpgo", L * sw, LANES).reshape(nq, L, sw, LANES)
    return dict(
        bre=bre.astype(BF16), bim=bim.astype(BF16), mst=mst.astype(BF16),
        cc=jnp.concatenate([cre, cim], axis=2).astype(BF16),
        lre=pw[:, :, L].real.reshape(1, g * p), lim=pw[:, :, L].imag.reshape(1, g * p))


def _ab_out_kernel(x_ref, ys_ref, u_ref, at_ref, d_ref, gw_ref, gb_ref, ws_ref, wa_ref, g1_ref, *route_refs):
    o_ref = route_refs[5]
    y = ys_ref[0] + d_ref[...] * u_ref[0]
    y = jax.nn.gelu(y)
    gate = jax.nn.sigmoid(_dot(y.astype(BF16), gw_ref[...]) + gb_ref[...])
    out = _dot((y * gate).astype(BF16), ws_ref[...]) + _dot(at_ref[0], wa_ref[...])
    xn = x_ref[0] + g1_ref[0] * out
    o_ref[0] = xn
    _route(xn, *route_refs[:5], *route_refs[6:])


def _ab_out_call(x, ys, u, attn, wp, g1, rt, ts):
    b, s, d = x.shape
    r_ins, r_specs, r_ospecs, r_oshape, r_scratch = _route_plumbing(rt, b, s, d, ts)
    ins = [x, ys, u, attn, wp["s5_d"], wp["glu_w"], wp["glu_b"], wp["w_out_s5"], wp["w_out_at"], g1] + r_ins
    specs = [_tok(d, ts), _tok(S5_WIDTH, ts), _tok(S5_WIDTH, ts), _tok(MLA_HEADS * MLA_V, ts),
             _full(wp["s5_d"]), _full(wp["glu_w"]), _full(wp["glu_b"]), _full(wp["w_out_s5"]),
             _full(wp["w_out_at"]), _modspec(g1, ts)] + r_specs
    return pl.pallas_call(
        _ab_out_kernel,
        grid=(b, s // ts),
        in_specs=specs,
        out_specs=[_tok(d, ts)] + r_ospecs,
        out_shape=[jax.ShapeDtypeStruct((b, s, d), F32)] + r_oshape,
        scratch_shapes=r_scratch,
        compiler_params=_cparams("arbitrary", "arbitrary"),
        name="ab_out",
    )(*ins)


def _c_in_kernel(x_ref, gain_ref, sc_ref, sh_ref, w_ref, qkv_ref, z_ref, ab_ref):
    h = _normmod(x_ref[0], gain_ref[...], sc_ref[0], sh_ref[0]).astype(BF16)
    proj = _dot(h, w_ref[...])
    qkv_ref[0] = proj[:, :GDN_QKV].astype(BF16)
    z_ref[0] = proj[:, GDN_QKV:GDN_QKV + GDN_VW].astype(BF16)
    ab_ref[0] = proj[:, GDN_QKV + GDN_VW:]


def _c_in_call(x, gain, sc, sh, w, ts):
    b, s, d = x.shape
    return pl.pallas_call(
        _c_in_kernel,
        grid=(b, s // ts),
        in_specs=[_tok(d, ts), _full(gain), _modspec(sc, ts), _modspec(sh, ts), _full(w)],
        out_specs=[_tok(GDN_QKV, ts), _tok(GDN_VW, ts), _tok(LANES, ts)],
        out_shape=[jax.ShapeDtypeStruct((b, s, GDN_QKV), BF16),
                   jax.ShapeDtypeStruct((b, s, GDN_VW), BF16),
                   jax.ShapeDtypeStruct((b, s, LANES), F32)],
        compiler_params=_cparams("parallel", "parallel"),
        name="c_in",
    )(x, gain, sc, sh, w)


def _gdn_prep_kernel(qkv_ref, halo_ref, st_ref, cw_ref, ab_ref, alog_ref, dtb_ref,
                     wq_ref, u_ref, kgt_ref, pm_ref, ee_ref, *, L, nc):
    j = pl.program_id(1)
    rows = nc * L
    prev = jnp.where(j == 0, st_ref[0].astype(F32), halo_ref[0].astype(F32))
    xx = jnp.concatenate([prev, qkv_ref[0].astype(F32)], axis=0)
    cw = cw_ref[...]
    y = xx[8:8 + rows] * cw[CONV_W - 1:CONV_W]
    for t in range(1, CONV_W):
        y += xx[8 - t:8 - t + rows] * cw[CONV_W - 1 - t:CONV_W - t]
    y = y * jax.nn.sigmoid(y)

    ab = ab_ref[0]
    g_all = -jnp.exp(alog_ref[...]) * jax.nn.softplus(ab + dtb_ref[...])
    beta_all = jax.nn.sigmoid(ab)
    row = lax.broadcasted_iota(jnp.int32, (L, L), 0)
    col = lax.broadcasted_iota(jnp.int32, (L, L), 1)
    incl = row >= col
    strict = row > col
    brow = lax.broadcasted_iota(jnp.int32, (rows, rows), 0)
    bcol = lax.broadcasted_iota(jnp.int32, (rows, rows), 1)
    incl_blk = jnp.logical_and(brow >= bcol, brow // L == bcol // L)
    tri_bf = incl_blk.astype(BF16)
    gam_all = sum(_dot(tri_bf, piece) for piece in _split3(g_all))
    eye_l = (lax.broadcasted_iota(jnp.int32, (LANES, LANES), 0)
             == lax.broadcasted_iota(jnp.int32, (LANES, LANES), 1))
    gam_t = sum(_dot_nt(eye_l.astype(BF16), piece) for piece in _split3(gam_all))
    for ci in range(nc):
        ee_ref[0, ci] = jnp.exp(gam_all[(ci + 1) * L - 1:(ci + 1) * L])
    eye64 = (row == col).astype(F32)
    eye_bf = eye_l.astype(BF16)

    ch = [(ci, h) for ci in range(nc) for h in range(GDN_HEADS)]
    cs = range(len(ch))
    koff, voff = GDN_HEADS * GDN_DK, 2 * GDN_HEADS * GDN_DK
    qh = [y[ci * L:(ci + 1) * L, GDN_DK * h:GDN_DK * (h + 1)] for ci, h in ch]
    kh = [y[ci * L:(ci + 1) * L, koff + GDN_DK * h:koff + GDN_DK * (h + 1)] for ci, h in ch]
    vh = [y[ci * L:(ci + 1) * L, voff + GDN_DV * h:voff + GDN_DV * (h + 1)] for ci, h in ch]
    qh = [q * lax.rsqrt(jnp.sum(q * q, axis=-1, keepdims=True) + EPS) * (GDN_DK ** -0.5) for q in qh]
    kh = [k * lax.rsqrt(jnp.sum(k * k, axis=-1, keepdims=True) + EPS) for k in kh]
    gam_c = [gam_all[ci * L:(ci + 1) * L, h:h + 1] for ci, h in ch]
    beta_c = [beta_all[ci * L:(ci + 1) * L, GDN_HEADS + h:GDN_HEADS + h + 1] for ci, h in ch]
    dec = [jnp.exp(jnp.minimum(gam_c[c] - gam_t[h:h + 1, ci * L:(ci + 1) * L], 0.0))
           for c, (ci, h) in enumerate(ch)]
    kb = [k.astype(BF16) for k in kh]
    kk = [_dot_nt(kb[c], kb[c]) for c in cs]
    qk = [_dot_nt(qh[c].astype(BF16), kb[c]) for c in cs]
    a = [jnp.where(strict, beta_c[c] * kk[c] * dec[c], 0.0) for c in cs]
    for c, (ci, h) in enumerate(ch):
        pm_ref[0, ci, h] = jnp.where(incl, qk[c] * dec[c], 0.0).astype(BF16)
    tinv = [eye64 - a[c] for c in cs]
    pw = [_split(a[c]) for c in cs]
    for _ in range(max(L.bit_length() - 2, 0)):
        pw = [_split(_dot3(pw[c], pw[c])) for c in cs]
        tinv = [tinv[c] + _dot3(_split(tinv[c]), pw[c]) for c in cs]
    eg = [jnp.exp(gam_c[c]) for c in cs]
    rhs = [jnp.concatenate([(beta_c[c] * eg[c]) * kh[c], beta_c[c] * vh[c]], axis=1).astype(BF16) for c in cs]
    wu = [_dot(tinv[c].astype(BF16), rhs[c]) for c in cs]
    for c, (ci, h) in enumerate(ch):
        wq_ref[0, ci, h] = jnp.concatenate([wu[c][:, :GDN_DK], qh[c] * eg[c]], axis=0).astype(BF16)
        u_ref[0, h, ci * L:(ci + 1) * L, :] = wu[c][:, GDN_DK:].astype(BF16)
    kg = [(kh[c] * jnp.exp(gam_all[(ci + 1) * L - 1:(ci + 1) * L, h:h + 1] - gam_c[c])).astype(BF16)
          for c, (ci, h) in enumerate(ch)]
    kgt = [_dot_nt(eye_bf, kg[c]) for c in cs]
    for c, (ci, h) in enumerate(ch):
        kgt_ref[0, ci, h] = kgt[c].astype(BF16)


def _split(a):
    hi = a.astype(BF16)
    return hi, (a - hi.astype(F32)).astype(BF16)


def _split3(a):
    hi = a.astype(BF16)
    r1 = a - hi.astype(F32)
    mid = r1.astype(BF16)
    return hi, mid, (r1 - mid.astype(F32)).astype(BF16)


def _dot3(a, b):
    return _dot(a[0], b[0]) + (_dot(a[0], b[1]) + _dot(a[1], b[0]))


GDN_PREP_CHUNKS = 2


def _gdn_prep_call(qkv, st8, conv_w, ab, alog, dtb, L):
    b, s, _ = qkv.shape
    nch = s // L
    nc = GDN_PREP_CHUNKS if nch % GDN_PREP_CHUNKS == 0 else 1
    rows = nc * L
    return pl.pallas_call(
        functools.partial(_gdn_prep_kernel, L=L, nc=nc),
        grid=(b, nch // nc),
        in_specs=[_tok(GDN_QKV, rows),
                  pl.BlockSpec((1, 8, GDN_QKV), lambda bb, j: (bb, jnp.maximum(j * (rows // 8) - 1, 0), 0)),
                  pl.BlockSpec((1, 8, GDN_QKV), lambda bb, j: (bb, 0, 0)),
                  _full(conv_w), _tok(LANES, rows), _full(alog), _full(dtb)],
        out_specs=[pl.BlockSpec((1, nc, GDN_HEADS, 2 * L, LANES), lambda bb, j: (bb, j, 0, 0, 0)),
                   pl.BlockSpec((1, GDN_HEADS, rows, LANES), lambda bb, j: (bb, 0, j, 0)),
                   pl.BlockSpec((1, nc, GDN_HEADS, GDN_DK, L), lambda bb, j: (bb, j, 0, 0, 0)),
                   pl.BlockSpec((1, nc, GDN_HEADS, L, L), lambda bb, j: (bb, j, 0, 0, 0)),
                   pl.BlockSpec((1, nc, 1, LANES), lambda bb, j: (bb, j, 0, 0))],
        out_shape=[jax.ShapeDtypeStruct((b, nch, GDN_HEADS, 2 * L, LANES), BF16),
                   jax.ShapeDtypeStruct((b, GDN_HEADS, s, LANES), BF16),
                   jax.ShapeDtypeStruct((b, nch, GDN_HEADS, GDN_DK, L), BF16),
                   jax.ShapeDtypeStruct((b, nch, GDN_HEADS, L, L), BF16),
                   jax.ShapeDtypeStruct((b, nch, 1, LANES), F32)],
        compiler_params=_cparams("parallel", "parallel"),
        name="gdn_prep",
    )(qkv, qkv, st8, conv_w, ab, alog, dtb)


GDN_BATCH_GROUP = 2


GDN_SEQ_CHUNKS = 8


def _gdn_seq_kernel(wq_ref, u_ref, kgt_ref, pm_ref, ee_ref, s0_ref, o_ref, sf_ref, s_scr, *, L, nc):
    c = pl.program_id(1)

    @pl.when(c == 0)
    def _():
        s_scr[...] = s0_ref[...]

    ch = [(bb, hh) for bb in range(GDN_BATCH_GROUP) for hh in range(GDN_HEADS)]
    st = [s_scr[bb, hh] for bb, hh in ch]
    for ci in range(nc):
        rows = slice(ci * L, (ci + 1) * L)
        sp = [_split(s) for s in st]
        r = [_dot(wq_ref[bb, ci, hh], sp[i][0]) + _dot(wq_ref[bb, ci, hh], sp[i][1]) for i, (bb, hh) in enumerate(ch)]
        v_new = [(u_ref[bb, hh, rows, :].astype(F32) - r[i][:L]).astype(BF16) for i, (bb, hh) in enumerate(ch)]
        o = [r[i][L:] + _dot(pm_ref[bb, ci, hh], v_new[i]) for i, (bb, hh) in enumerate(ch)]
        for i, (bb, hh) in enumerate(ch):
            o_ref[bb, rows, GDN_DV * hh:GDN_DV * (hh + 1)] = o[i].astype(o_ref.dtype)
        st = [ee_ref[bb, ci][:, hh:hh + 1] * st[i] + _dot(kgt_ref[bb, ci, hh], v_new[i])
              for i, (bb, hh) in enumerate(ch)]
    for i, (bb, hh) in enumerate(ch):
        s_scr[bb, hh] = st[i]

    @pl.when(c == pl.num_programs(1) - 1)
    def _():
        sf_ref[...] = s_scr[...]


def _gdn_seq_call(wq, um, kgt, pm, ee, s0, L):
    b, h, s, _ = um.shape
    nch = s // L
    bg = GDN_BATCH_GROUP
    nc = GDN_SEQ_CHUNKS if nch % GDN_SEQ_CHUNKS == 0 else 1
    sspec = pl.BlockSpec((bg, h, GDN_DK, GDN_DV), lambda g, c: (g, 0, 0, 0))
    return pl.pallas_call(
        functools.partial(_gdn_seq_kernel, L=L, nc=nc),
        grid=(b // bg, nch // nc),
        in_specs=[pl.BlockSpec((bg, nc, h, 2 * L, LANES), lambda g, c: (g, c, 0, 0, 0)),
                  pl.BlockSpec((bg, h, nc * L, LANES), lambda g, c: (g, 0, c, 0)),
                  pl.BlockSpec((bg, nc, h, GDN_DK, L), lambda g, c: (g, c, 0, 0, 0)),
                  pl.BlockSpec((bg, nc, h, L, L), lambda g, c: (g, c, 0, 0, 0)),
                  pl.BlockSpec((bg, nc, 1, LANES), lambda g, c: (g, c, 0, 0)),
                  sspec],
        out_specs=[pl.BlockSpec((bg, nc * L, GDN_VW), lambda g, c: (g, c, 0)), sspec],
        out_shape=[jax.ShapeDtypeStruct((b, s, GDN_VW), BF16),
                   jax.ShapeDtypeStruct((b, h, GDN_DK, GDN_DV), F32)],
        scratch_shapes=[pltpu.VMEM((bg, h, GDN_DK, GDN_DV), F32)],
        compiler_params=_cparams("parallel", "arbitrary"),
        name="gdn_seq",
    )(wq, um, kgt, pm, ee, s0)


def _c_out_kernel(x_ref, o_ref, z_ref, gn_ref, w_ref, g1_ref, *route_refs):
    out_ref = route_refs[5]
    o = o_ref[0].astype(F32)
    z = z_ref[0].astype(F32)
    gn = gn_ref[...]
    parts = []
    for hh in range(GDN_HEADS):
        oh = o[:, GDN_DV * hh:GDN_DV * (hh + 1)]
        zh = z[:, GDN_DV * hh:GDN_DV * (hh + 1)]
        parts.append((_rms(oh, gn) * (zh * jax.nn.sigmoid(zh))).astype(BF16))
    out = _dot(jnp.concatenate(parts, axis=1), w_ref[...])
    xn = x_ref[0] + g1_ref[0] * out
    out_ref[0] = xn
    _route(xn, *route_refs[:5], *route_refs[6:])


def _c_out_call(x, o, z, gn, w, g1, rt, ts):
    b, s, d = x.shape
    r_ins, r_specs, r_ospecs, r_oshape, r_scratch = _route_plumbing(rt, b, s, d, ts)
    return pl.pallas_call(
        _c_out_kernel,
        grid=(b, s // ts),
        in_specs=[_tok(d, ts), _tok(GDN_VW, ts), _tok(GDN_VW, ts), _full(gn), _full(w), _modspec(g1, ts)] + r_specs,
        out_specs=[_tok(d, ts)] + r_ospecs,
        out_shape=[jax.ShapeDtypeStruct((b, s, d), F32)] + r_oshape,
        scratch_shapes=r_scratch,
        compiler_params=_cparams("arbitrary", "arbitrary"),
        name="c_out",
    )(x, o, z, gn, w, g1, *r_ins)


def _pack_halves(v):
    n = v.shape[1] // 2
    bits = pltpu.bitcast(v.astype(BF16).astype(F32), jnp.uint32)
    return (bits[:, :n] >> 16) | (bits[:, n:] & jnp.uint32(0xFFFF0000))


def _unpack_halves(w):
    return pltpu.bitcast(w << 16, F32), pltpu.bitcast(w & jnp.uint32(0xFFFF0000), F32)


def _route(xn, gain_ref, sc_ref, sh_ref, wr_ref, br_ref, h_ref, re_ref, rg_ref, cnt_ref, carry):
    @pl.when(jnp.logical_and(pl.program_id(0) == 0, pl.program_id(1) == 0))
    def _():
        carry[...] = jnp.zeros_like(carry)

    h = _normmod(xn, gain_ref[...], sc_ref[0], sh_ref[0])
    h_ref[0] = _pack_halves(h)
    logits = _dot3(_split(h), (wr_ref[0], wr_ref[1])) + br_ref[...]
    lane = lax.broadcasted_iota(jnp.int32, logits.shape, 1).astype(F32)
    neg = jnp.float32(-jnp.inf)
    big = jnp.float32(1e9)
    gl = jnp.where(lane < N_GROUPS, logits, neg)
    gm = jnp.max(gl, axis=-1, keepdims=True)
    g_val = 1.0 / jnp.sum(jnp.exp(gl - gm), axis=-1, keepdims=True)
    g_idx = jnp.min(jnp.where(gl == gm, lane, big), axis=-1, keepdims=True)
    lo = N_GROUPS + EXPERTS_PER_GROUP * g_idx
    el = jnp.where(jnp.logical_and(lane >= lo, lane < lo + EXPERTS_PER_GROUP), logits, neg)
    em = jnp.max(el, axis=-1, keepdims=True)
    i1 = jnp.min(jnp.where(el == em, lane, big), axis=-1, keepdims=True)
    el2 = jnp.where(lane == i1, neg, el)
    em2 = jnp.max(el2, axis=-1, keepdims=True)
    i2 = jnp.min(jnp.where(el2 == em2, lane, big), axis=-1, keepdims=True)
    es = jnp.sum(jnp.exp(el - em), axis=-1, keepdims=True)
    p1 = 1.0 / es
    p2 = jnp.exp(em2 - em) / es
    den = p1 + p2
    rg_ref[0] = jnp.where(lane == 0, g_val * p1 / den, jnp.where(lane == 1, g_val * p2 / den, 0.0))
    oh1 = lane == i1
    oh2 = lane == i2
    both = jnp.where(jnp.logical_or(oh1, oh2), 1.0, 0.0)
    ts = both.shape[0]
    tri = (lax.broadcasted_iota(jnp.int32, (ts, ts), 0) > lax.broadcasted_iota(jnp.int32, (ts, ts), 1))
    pre = _dot(jnp.where(tri, 1.0, 0.0).astype(BF16), both.astype(BF16)) + carry[...]
    r1 = jnp.sum(jnp.where(oh1, pre, 0.0), axis=-1, keepdims=True)
    r2 = jnp.sum(jnp.where(oh2, pre, 0.0), axis=-1, keepdims=True)
    re_ref[0] = jnp.where(lane == 0, i1 - N_GROUPS, jnp.where(lane == 1, i2 - N_GROUPS, jnp.where(
        lane == 2, r1, jnp.where(lane == 3, r2, 0.0)))).astype(jnp.int32)
    carry[...] += jnp.sum(both, axis=0, keepdims=True)
    cnt_ref[...] = carry[...]


def _route_plumbing(rt, b, s, d, ts):
    gain, sc, sh, wr, br = rt
    ins = [gain, sc, sh, wr, br]
    in_specs = [_full(gain), _modspec(sc, ts), _modspec(sh, ts), _full(wr), _full(br)]
    out_specs = [_tok(d // 2, ts), _tok(LANES, ts), _tok(LANES, ts), pl.BlockSpec((1, LANES), lambda bb, j: (0, 0))]
    out_shape = [jax.ShapeDtypeStruct((b, s, d // 2), jnp.uint32),
                 jax.ShapeDtypeStruct((b, s, LANES), jnp.int32),
                 jax.ShapeDtypeStruct((b, s, LANES), F32),
                 jax.ShapeDtypeStruct((1, LANES), F32)]
    return ins, in_specs, out_specs, out_shape, [pltpu.VMEM((1, LANES), F32)]


def _row_copy(src, dst, src_row, dst_row, sem):
    return pltpu.make_async_copy(src.at[pl.ds(src_row, 1)], dst.at[pl.ds(dst_row, 1)], sem)


ROUTE_W = 2 * TOP_K
DMA_UNROLL = 8


def _slot(ps_ref, rt_ref, r, k):
    return ps_ref[rt_ref[0, 0, ROUTE_W * r + k]] + rt_ref[0, 0, ROUTE_W * r + TOP_K + k]


def _dispatch_kernel(ps_ref, rt_ref, h_ref, xin_hbm, xb_hbm, sem, *, tb):
    del xin_hbm

    def issue(r, c):
        for k in range(TOP_K):
            _row_copy(h_ref, xb_hbm, r, _slot(ps_ref, rt_ref, r, k), sem).start(priority=k)
        return c
    lax.fori_loop(0, tb, issue, 0, unroll=DMA_UNROLL)
    for k in range(TOP_K):
        pltpu.make_async_copy(h_ref, xb_hbm.at[pl.ds(0, tb)], sem).wait()


def _dispatch_call(pad_start, route, h2, xb0, tb):
    n, dh = h2.shape
    nb = n // tb
    grid_spec = pltpu.PrefetchScalarGridSpec(
        num_scalar_prefetch=1,
        grid=(nb,),
        in_specs=[pl.BlockSpec((1, 1, ROUTE_W * tb), lambda j, ps: (j, 0, 0), memory_space=pltpu.SMEM),
                  pl.BlockSpec((tb, dh), lambda j, ps: (j, 0)),
                  pl.BlockSpec(memory_space=pl.ANY)],
        out_specs=pl.BlockSpec(memory_space=pl.ANY),
        scratch_shapes=[pltpu.SemaphoreType.DMA(())],
    )
    return pl.pallas_call(
        functools.partial(_dispatch_kernel, tb=tb),
        grid_spec=grid_spec,
        out_shape=jax.ShapeDtypeStruct(xb0.shape, xb0.dtype),
        input_output_aliases={3: 0},
        compiler_params=_cparams("arbitrary"),
        name="moe_dispatch",
    )(pad_start, route, h2, xb0)


def _ffn_kernel(be_ref, nu_ref, x_ref, w1_ref, w3_ref, w2_ref, y_ref, w1b, w3b, w2b):
    i = pl.program_id(0)
    half = w1b.shape[0] // 2

    @pl.when(jnp.logical_and(i < nu_ref[0], jnp.logical_or(i == 0, be_ref[i] != be_ref[jnp.maximum(i - 1, 0)])))
    def _():
        w1b[...] = w1_ref[0, 0].astype(BF16)
        w3b[...] = w3_ref[0, 0].astype(BF16)
        w2b[...] = w2_ref[0, 0].astype(BF16)

    @pl.when(i < nu_ref[0])
    def _():
        lo, hi = _unpack_halves(x_ref[...])
        lo = lo.astype(BF16)
        hi = hi.astype(BF16)
        a = _dot(lo, w1b[:half]) + _dot(hi, w1b[half:])
        g = _dot(lo, w3b[:half]) + _dot(hi, w3b[half:])
        mid = (a * jax.nn.sigmoid(a) * g).astype(BF16)
        y_ref[...] = _pack_halves(_dot(mid, w2b[...]))

    @pl.when(i >= nu_ref[0])
    def _():
        y_ref[...] = jnp.zeros_like(y_ref)


def _ffn_call(block_e, n_used, xb, w1, w3, w2, layer):
    n_blocks = block_e.shape[0]
    dh = xb.shape[1]
    d = 2 * dh
    blk = xb.shape[0] // n_blocks
    grid_spec = pltpu.PrefetchScalarGridSpec(
        num_scalar_prefetch=2,
        grid=(n_blocks,),
        in_specs=[pl.BlockSpec((blk, dh), lambda i, be, nu: (i, 0)),
                  pl.BlockSpec((1, 1, d, D_EXPERT), lambda i, be, nu: (layer, be[i], 0, 0)),
                  pl.BlockSpec((1, 1, d, D_EXPERT), lambda i, be, nu: (layer, be[i], 0, 0)),
                  pl.BlockSpec((1, 1, D_EXPERT, d), lambda i, be, nu: (layer, be[i], 0, 0))],
        out_specs=pl.BlockSpec((blk, dh), lambda i, be, nu: (i, 0)),
        scratch_shapes=[pltpu.VMEM((d, D_EXPERT), BF16), pltpu.VMEM((d, D_EXPERT), BF16),
                        pltpu.VMEM((D_EXPERT, d), BF16)],
    )
    return pl.pallas_call(
        _ffn_kernel,
        grid_spec=grid_spec,
        out_shape=jax.ShapeDtypeStruct(xb.shape, jnp.uint32),
        compiler_params=_cparams("arbitrary"),
        name="moe_ffn",
    )(block_e, n_used, xb, w1, w3, w2)


def _combine_kernel(ps_ref, rcur_ref, rnxt_ref, yb_hbm, x_ref, rg_ref, g2_ref, fin_ref, o_ref, b0, b1, sem,
                    *, tb, final):
    j = pl.program_id(0)
    nb = pl.num_programs(0)
    slot = j % 2

    def fetch(rref, sl):
        def issue(r, c):
            _row_copy(yb_hbm, b0.at[sl], _slot(ps_ref, rref, r, 0), r, sem.at[sl]).start(priority=0)
            _row_copy(yb_hbm, b1.at[sl], _slot(ps_ref, rref, r, 1), r, sem.at[sl]).start(priority=1)
            return c
        lax.fori_loop(0, tb, issue, 0, unroll=DMA_UNROLL)

    @pl.when(j == 0)
    def _():
        fetch(rcur_ref, 0)

    @pl.when(j + 1 < nb)
    def _():
        fetch(rnxt_ref, 1 - slot)

    pltpu.make_async_copy(yb_hbm.at[pl.ds(0, tb)], b0.at[slot], sem.at[slot]).wait()
    pltpu.make_async_copy(yb_hbm.at[pl.ds(0, tb)], b1.at[slot], sem.at[slot]).wait()
    rg = rg_ref[...]
    lo0, hi0 = _unpack_halves(b0[slot])
    lo1, hi1 = _unpack_halves(b1[slot])
    g0 = rg[:, 0:1]
    g1 = rg[:, 1:2]
    y = jnp.concatenate([g0 * lo0 + g1 * lo1, g0 * hi0 + g1 * hi1], axis=1)
    xn = x_ref[...] + g2_ref[0] * y
    if final:
        xn = _rms(xn, fin_ref[...])
    o_ref[...] = xn


def _combine_call(pad_start, route, yb, x, rg, g2, fin, tb, final):
    b, s, d = x.shape
    n = b * s
    nb = n // tb
    npb = s // tb
    dh = yb.shape[1]
    if g2.shape[1] == 1:
        gspec = pl.BlockSpec((1, 1, d), lambda j, ps: (j // npb, 0, 0))
    else:
        gspec = pl.BlockSpec((1, tb, d), lambda j, ps: (j // npb, j % npb, 0))
    grid_spec = pltpu.PrefetchScalarGridSpec(
        num_scalar_prefetch=1,
        grid=(nb,),
        in_specs=[pl.BlockSpec((1, 1, ROUTE_W * tb), lambda j, ps: (j, 0, 0), memory_space=pltpu.SMEM),
                  pl.BlockSpec((1, 1, ROUTE_W * tb), lambda j, ps: (jnp.minimum(j + 1, nb - 1), 0, 0),
                               memory_space=pltpu.SMEM),
                  pl.BlockSpec(memory_space=pl.ANY),
                  pl.BlockSpec((tb, d), lambda j, ps: (j, 0)), pl.BlockSpec((tb, LANES), lambda j, ps: (j, 0)),
                  gspec, pl.BlockSpec(fin.shape, lambda j, ps: (0, 0))],
        out_specs=pl.BlockSpec((tb, d), lambda j, ps: (j, 0)),
        scratch_shapes=[pltpu.VMEM((2, tb, dh), jnp.uint32), pltpu.VMEM((2, tb, dh), jnp.uint32),
                        pltpu.SemaphoreType.DMA((2,))],
    )
    out = pl.pallas_call(
        functools.partial(_combine_kernel, tb=tb, final=final),
        grid_spec=grid_spec,
        out_shape=jax.ShapeDtypeStruct((n, d), F32),
        compiler_params=_cparams("arbitrary"),
        name="moe_combine",
    )(pad_start, route, route, yb, x.reshape(n, d), rg.reshape(n, LANES), g2, fin)
    return out.reshape(b, s, d)


def _moe(x, routed, g2, mp, fin, ts, final):
    b, s, d = x.shape
    n = b * s
    h2, r_e, r_g, cnt = routed
    counts = cnt[0, N_GROUPS:N_GROUPS + N_EXPERTS].astype(jnp.int32)
    blk = MOE_BLOCK if n * TOP_K >= N_EXPERTS * MOE_BLOCK else MOE_BLOCK_SMALL
    padded = (counts + blk - 1) // blk * blk
    pad_end = jnp.cumsum(padded)
    pad_start = (pad_end - padded).astype(jnp.int32)
    n_blocks = (n * TOP_K + N_EXPERTS * (blk - 1) + blk - 1) // blk
    blk0 = jnp.arange(n_blocks, dtype=jnp.int32) * blk
    block_e = jnp.minimum(jnp.sum((pad_end[None, :] <= blk0[:, None]).astype(jnp.int32), axis=1), N_EXPERTS - 1)
    n_used = (pad_end[-1:] // blk).astype(jnp.int32)
    route = r_e[:, :, :ROUTE_W].reshape(n // ts, 1, ROUTE_W * ts)
    xb0 = jnp.zeros((n_blocks * blk, d // 2), jnp.uint32)
    xb = _dispatch_call(pad_start, route, h2.reshape(n, d // 2), xb0, ts)
    yb = _ffn_call(block_e, n_used, xb, mp["w1"], mp["w3"], mp["w2"], mp["layer"])
    return _combine_call(pad_start, route, yb, x, r_g, g2, fin, ts, final)


def _prep_ab(l, p):
    w_in = p["w_in_ab"][l]
    o2 = S5_WIDTH + MLA_Q_RANK + MLA_KV_RANK
    half = MLA_ROPE // 2
    d = w_in.shape[0]
    kr = w_in[:, o2:]
    kr_rot = jnp.concatenate([-kr[:, half:], kr[:, :half]], axis=1)
    z64 = jnp.zeros((d, MLA_NOPE), F32)
    z32 = jnp.zeros((d, LANES - MLA_NOPE - MLA_ROPE), F32)
    w_ext = jnp.concatenate([w_in[:, :o2], z64, kr, z32, z64, kr_rot, z32], axis=1)
    qu = p["mla_q_up"][l].reshape(MLA_Q_RANK, MLA_HEADS, MLA_NOPE + MLA_ROPE)
    qn, qr = qu[:, :, :MLA_NOPE], qu[:, :, MLA_NOPE:]
    zq = jnp.zeros((MLA_Q_RANK, MLA_HEADS, LANES - MLA_NOPE - MLA_ROPE), F32)
    wqa = jnp.concatenate([qn, qr, zq], axis=2).reshape(MLA_Q_RANK, MLA_HEADS * LANES)
    wqb = jnp.concatenate([jnp.zeros_like(qn), -qr[:, :, half:], qr[:, :, :half], zq], axis=2)
    wqb = wqb.reshape(MLA_Q_RANK, MLA_HEADS * LANES)
    kvu = p["mla_kv_up"][l].reshape(MLA_KV_RANK, MLA_HEADS, MLA_NOPE + MLA_V)
    wk = jnp.concatenate([kvu[:, :, :MLA_NOPE], jnp.zeros((MLA_KV_RANK, MLA_HEADS, LANES - MLA_NOPE), F32)], axis=2)
    wo = p["w_out_ab"][l]
    out = dict(
        w_in=w_ext.astype(BF16), q_norm=p["mla_q_norm"][l].reshape(1, -1), kv_norm=p["mla_kv_norm"][l].reshape(1, -1),
        wqa=wqa.astype(BF16), wqb=wqb.astype(BF16),
        wk=wk.reshape(MLA_KV_RANK, MLA_HEADS * LANES).astype(BF16),
        wv=jnp.concatenate([kvu[:, :, MLA_NOPE:], jnp.zeros((MLA_KV_RANK, MLA_HEADS, LANES - MLA_V), F32)],
                           axis=2).reshape(MLA_KV_RANK, MLA_HEADS * LANES).astype(BF16),
        s5_d=p["s5_d"][l].reshape(1, -1), glu_w=p["s5_glu_w"][l].astype(BF16), glu_b=p["s5_glu_b"][l].reshape(1, -1),
        w_out_s5=wo[:S5_WIDTH].astype(BF16),
        w_out_at=wo[S5_WIDTH:].astype(BF16))
    out.update(_s5_mats(p["s5_a_re"][l], p["s5_a_im"][l], p["s5_b_re"][l], p["s5_b_im"][l],
                        p["s5_c_re"][l], p["s5_c_im"][l], p["s5_log_dt"][l]))
    return out


def _prep_c(l, p):
    w = p["w_in_c"][l]
    pad = jnp.zeros((w.shape[0], LANES - 2 * GDN_HEADS), F32)

    def row(v):
        return jnp.concatenate([v, jnp.zeros((LANES - GDN_HEADS,), F32)]).reshape(1, LANES)

    return dict(w_in=jnp.concatenate([w, pad], axis=1).astype(BF16), conv_w=p["conv_w"][l],
                alog=row(p["gdn_a_log"][l]), dtb=row(p["gdn_dt_bias"][l]),
                gn=p["gdn_norm"][l].reshape(1, -1), w_out=p["w_out_c"][l].astype(BF16))


def _prep_moe(layer, p):
    d = p["moe_w_group"].shape[1]
    wr = jnp.concatenate([p["moe_w_group"][layer], p["moe_w_expert"][layer],
                          jnp.zeros((d, LANES - N_GROUPS - N_EXPERTS), F32)], axis=1)
    br = jnp.concatenate([p["moe_b_group"][layer], p["moe_b_expert"][layer],
                          jnp.zeros((LANES - N_GROUPS - N_EXPERTS,), F32)]).reshape(1, LANES)
    wr_hi = wr.astype(BF16)
    wr_lo = (wr - wr_hi.astype(F32)).astype(BF16)
    return dict(wr=jnp.stack([wr_hi, wr_lo]), br=br, w1=p["moe_w1"], w3=p["moe_w3"], w2=p["moe_w2"], layer=layer)


def _rope_tables(pos):
    half = MLA_ROPE // 2
    inv = ROPE_THETA ** (-jnp.arange(half, dtype=F32) / half)
    ang = pos.astype(F32)[:, None] * inv[None, :]
    cos, sin = jnp.cos(ang), jnp.sin(ang)
    n = pos.shape[0]
    z64 = jnp.zeros((n, MLA_NOPE), F32)
    z32 = jnp.zeros((n, LANES - MLA_NOPE - MLA_ROPE), F32)
    scale = (MLA_NOPE + MLA_ROPE) ** -0.5 * math.log2(math.e)
    ck = jnp.concatenate([z64, cos, cos, z32], axis=1)
    sk = jnp.concatenate([z64, sin, sin, z32], axis=1)
    cq = jnp.concatenate([jnp.ones_like(z64), cos, cos, z32], axis=1) * scale
    return cq, sk * scale, ck, sk


def _trunk(x, mods, caches, wab, wc, wmoe, p, flat):
    b, s, d = x.shape
    lat_c, rope_c, s5re_c, s5im_c, conv_c, gdn_c = caches
    past = 0 if lat_c is None else lat_c.shape[2]
    pos = past + jnp.arange(s)
    if flat:
        xt = x.reshape(1, b * s, d)
        ts = b * s
        pos_rows = jnp.tile(pos, b)

        def mrow(m):
            return jnp.repeat(m, s, axis=0).reshape(1, b * s, d)
    else:
        xt = x
        ts = TOKEN_BLOCK
        pos_rows = pos

        def mrow(m):
            return m.reshape(b, 1, d)
    bt, st = xt.shape[:2]
    outs = {}
    depth = p["norm_mix"].shape[0]
    for layer in range(depth):
        l = layer // 2
        sh1, sc1, g1, sh2, sc2, g2 = [mrow(m) for m in jnp.split(mods[layer], 6, axis=-1)]
        gain = p["norm_mix"][layer].reshape(1, d)
        rt = (p["norm_ffn"][layer].reshape(1, d), sc2, sh2, wmoe[layer]["wr"], wmoe[layer]["br"])
        if layer % 2 == 0:
            wp = wab[l]
            with_kv = lat_c is None and not flat
            u, q, lat, kr, *kv = _ab_in_call(xt, gain, sc1, sh1, wp, _rope_tables(pos_rows), ts, with_kv)
            lat_b = lat.reshape(b, s, MLA_KV_RANK)
            kr_b = kr.reshape(b, s, LANES)
            outs["lat"] = lat_b
            outs["krope"] = kr_b[:, :, MLA_NOPE:MLA_NOPE + MLA_ROPE]
            if lat_c is None:
                lat_all, kr_all, sk_valid, tkv = lat_b, kr_b, s, min(TOKEN_BLOCK, s)
                tq, tk = min(ATTN_TQ, s), min(ATTN_TK, s)
            else:
                krc = jnp.pad(rope_c[l], ((0, 0), (0, 0), (MLA_NOPE, LANES - MLA_NOPE - MLA_ROPE)))
                sk_valid = past + s
                skp = (sk_valid + LANES - 1) // LANES * LANES
                lat_all = jnp.pad(jnp.concatenate([lat_c[l], lat_b], axis=1), ((0, 0), (0, skp - sk_valid), (0, 0)))
                kr_all = jnp.pad(jnp.concatenate([krc, kr_b], axis=1), ((0, 0), (0, skp - sk_valid), (0, 0)))
                tkv, tq, tk = skp, s, skp
            kk, vv = kv if with_kv else _kv_call(lat_all, kr_all, wp, tkv)
            qh = q.reshape(MLA_HEADS, b, s, LANES).transpose(1, 0, 2, 3) if flat else q
            attn = _attn_call(qh, kk, vv, tq, tk, past, sk_valid).reshape(bt, st, MLA_HEADS * MLA_V)
            nchk = s // S5_CHUNK
            r = b * nchk
            u2 = u.reshape(b * s, S5_WIDTH)
            tr = min(r, S5_ROW_BLOCK)
            wre, wim = _s5_w_call(u2, wp["bre"], wp["bim"], tr)
            ngp = S5_GROUPS * S5_STATE

            def to_pairs(v):
                return v.reshape(b, 1, ngp)

            if s5re_c is None:
                x0re = jnp.zeros((b, 1, ngp), F32)
                x0im = jnp.zeros((b, 1, ngp), F32)
            else:
                x0re, x0im = to_pairs(s5re_c[l]), to_pairs(s5im_c[l])
            nchp = (nchk + 7) // 8 * 8
            tc = min(nchp, S5_SCAN_BLOCK)
            wre3 = jnp.pad(wre.reshape(b, nchk, ngp), ((0, 0), (0, nchp - nchk), (0, 0)))
            wim3 = jnp.pad(wim.reshape(b, nchk, ngp), ((0, 0), (0, nchp - nchk), (0, 0)))
            xere, xeim = _s5_scan_call(wre3, wim3, wp["lre"], wp["lim"], x0re, x0im, tc)
            outs["s5re"] = xere[:, nchk - 1].reshape(b, S5_GROUPS, S5_STATE)
            outs["s5im"] = xeim[:, nchk - 1].reshape(b, S5_GROUPS, S5_STATE)
            xsre = jnp.concatenate([x0re, xere[:, :nchk - 1]], axis=1).reshape(r, ngp)
            xsim = jnp.concatenate([x0im, xeim[:, :nchk - 1]], axis=1).reshape(r, ngp)
            ys = _s5_y_call(u2, xsre, xsim, wp["mst"], wp["cc"], tr).reshape(bt, st, S5_WIDTH)
            xt, *routed = _ab_out_call(xt, ys, u, attn, wp, g1, rt, ts)
        else:
            wp = wc[l]
            qkv, z, ab = _c_in_call(xt, gain, sc1, sh1, wp["w_in"], ts)
            qkv_b = qkv.reshape(b, s, GDN_QKV)
            outs["conv"] = qkv_b[:, s - (CONV_W - 1):].astype(F32)
            if conv_c is None:
                st8 = jnp.zeros((b, 8, GDN_QKV), BF16)
                s0 = jnp.zeros((b, GDN_HEADS, GDN_DK, GDN_DV), F32)
            else:
                st8 = jnp.pad(conv_c[l], ((0, 0), (8 - (CONV_W - 1), 0), (0, 0))).astype(BF16)
                s0 = gdn_c[l]
            lg = min(s, CHUNK)
            wq, um, kgt, pm, ee = _gdn_prep_call(qkv_b, st8, wp["conv_w"], ab.reshape(b, s, LANES),
                                                 wp["alog"], wp["dtb"], lg)
            o, sfin = _gdn_seq_call(wq, um, kgt, pm, ee, s0, lg)
            outs["gdn"] = sfin
            xt, *routed = _c_out_call(xt, o.reshape(bt, st, GDN_VW), z, wp["gn"], wp["w_out"], g1, rt, ts)
        final = layer == depth - 1
        xt = _moe(xt, routed, g2, wmoe[layer], p["norm_final"].reshape(1, d), ts, final)
    y = xt.reshape(b, s, d)
    return (y, outs["lat"][None], outs["krope"][None], outs["s5re"][None], outs["s5im"][None],
            outs["conv"][None], outs["gdn"][None])


def kernel(x_prompt, x_sample, c_prompt, c_sample, cache_mla_latent, cache_mla_krope, state_s5_re, state_s5_im,
           state_conv, state_gdn, w_mod, b_mod, norm_mix, norm_ffn, norm_final, w_in_ab, s5_a_re, s5_a_im,
           s5_b_re, s5_b_im, s5_c_re, s5_c_im, s5_d, s5_log_dt, s5_glu_w, s5_glu_b, mla_q_norm, mla_q_up,
           mla_kv_norm, mla_kv_up, w_out_ab, w_in_c, conv_w, gdn_a_log, gdn_dt_bias, gdn_norm, w_out_c,
           moe_w_group, moe_b_group, moe_w_expert, moe_b_expert, moe_w1, moe_w3, moe_w2):
    p = dict(w_mod=w_mod, b_mod=b_mod, norm_mix=norm_mix, norm_ffn=norm_ffn, norm_final=norm_final,
             w_in_ab=w_in_ab, s5_a_re=s5_a_re, s5_a_im=s5_a_im, s5_b_re=s5_b_re, s5_b_im=s5_b_im,
             s5_c_re=s5_c_re, s5_c_im=s5_c_im, s5_d=s5_d, s5_log_dt=s5_log_dt, s5_glu_w=s5_glu_w,
             s5_glu_b=s5_glu_b, mla_q_norm=mla_q_norm, mla_q_up=mla_q_up, mla_kv_norm=mla_kv_norm,
             mla_kv_up=mla_kv_up, w_out_ab=w_out_ab, w_in_c=w_in_c, conv_w=conv_w, gdn_a_log=gdn_a_log,
             gdn_dt_bias=gdn_dt_bias, gdn_norm=gdn_norm, w_out_c=w_out_c, moe_w_group=moe_w_group,
             moe_b_group=moe_b_group, moe_w_expert=moe_w_expert, moe_b_expert=moe_b_expert,
             moe_w1=moe_w1, moe_w3=moe_w3, moe_w2=moe_w2)
    depth = norm_mix.shape[0]
    bp, bs = c_prompt.shape[0], c_sample.shape[0]
    nb = (bp + bs + 7) // 8 * 8
    c_all = jnp.pad(jnp.concatenate([c_prompt, c_sample], axis=0), ((0, nb - bp - bs), (0, 0)))
    mods = _mod_call(c_all, w_mod, b_mod)
    wab = [_prep_ab(l, p) for l in range((depth + 1) // 2)]
    wc = [_prep_c(l, p) for l in range(depth // 2)]
    wmoe = [_prep_moe(layer, p) for layer in range(depth)]
    none6 = (None,) * 6
    outp = _trunk(x_prompt, mods[:, :bp], none6, wab, wc, wmoe, p, flat=False)
    caches = (cache_mla_latent, cache_mla_krope, state_s5_re, state_s5_im, state_conv, state_gdn)
    outs = _trunk(x_sample, mods[:, bp:bp + bs], caches, wab, wc, wmoe, p, flat=True)
    return (outp[0], outs[0]) + tuple(outp[1:]) + tuple(outs[1:])
```
